```python
import math
import jax
import jax.numpy as jnp
from jax import lax
import numpy as np

D_MODEL = 2048
BATCH = 1
SEQ = 8192
DEPTH = 2
DEC_BATCH = 128
DEC_SEQ = 4
PAST_LEN = 8192
PAGE_SIZE = 128

MIX_HALF = D_MODEL // 2
CONV_A_WIDTH = MIX_HALF
CONV_A_K = 3
HEAD_DIM = 64
ATT_HEADS = MIX_HALF // HEAD_DIM
ATT_KV_HEADS = ATT_HEADS // 4
ATT_GROUP = ATT_HEADS // ATT_KV_HEADS
WINDOW = 128
NUM_BUCKETS = 32
MAX_DISTANCE = 128
CMLP_WIDTH = MIX_HALF
CMLP_GROUPS = 8
CHUNK = 128
SSD_HEAD_DIM = 64
SSD_HEADS = MIX_HALF // SSD_HEAD_DIM
SSD_WIDTH = SSD_HEADS * SSD_HEAD_DIM
SSD_STATE = 128
SSD_GROUPS = 2
SSD_CONV_K = 4
SSD_CHUNK = 128
SSD_CONV_DIM = SSD_WIDTH + 2 * SSD_GROUPS * SSD_STATE
EVEN_IN_WIDTHS = (CONV_A_WIDTH,) * 4 + (ATT_HEADS * HEAD_DIM, ATT_KV_HEADS * HEAD_DIM, ATT_KV_HEADS * HEAD_DIM, ATT_HEADS * HEAD_DIM)
ODD_IN_WIDTHS = (CMLP_WIDTH,) * 3 + (SSD_WIDTH, SSD_CONV_DIM, SSD_HEADS)
NORM_EPS = 1e-6
MASK_VALUE = -1e30

kernel_name = 'hybrid_conv_swa_chunkmlp_ssd_step'


def split_cols(t, widths):
    idx = np.cumsum(widths)[:-1].tolist()
    return jnp.split(t, idx, axis=-1)


def rms_norm(x, g):
    xf = x.astype(jnp.float32)
    y = xf * lax.rsqrt(jnp.mean(xf * xf, axis=-1, keepdims=True) + NORM_EPS)
    return (y * g.astype(jnp.float32)).astype(x.dtype)


def layer_norm(x, g, b):
    xf = x.astype(jnp.float32)
    xc = xf - jnp.mean(xf, axis=-1, keepdims=True)
    y = xc * lax.rsqrt(jnp.mean(xc * xc, axis=-1, keepdims=True) + NORM_EPS)
    return (y * g.astype(jnp.float32) + b.astype(jnp.float32)).astype(x.dtype)


def causal_dwconv(s, buf, w):
    k_width = w.shape[0]
    t_len = s.shape[1]
    full = jnp.concatenate([buf.astype(s.dtype), s], axis=1)
    out = full[:, 0:t_len] * w[0]
    for j in range(1, k_width):
        out = out + full[:, j:j + t_len] * w[j]
    return out, full[:, full.shape[1] - (k_width - 1):]


def rel_bucket(dist):
    max_exact = NUM_BUCKETS // 2
    d = jnp.maximum(dist, 0)
    ratio = jnp.maximum(d, max_exact).astype(jnp.float32) / max_exact
    large = max_exact + (jnp.log(ratio) / math.log(MAX_DISTANCE / max_exact) * (NUM_BUCKETS - max_exact)).astype(jnp.int32)
    return jnp.where(d < max_exact, d, jnp.minimum(large, NUM_BUCKETS - 1))


def sink_attention(q, k, v, dist, valid, rel_bias, sinks):
    hkv, grp, hd = q.shape[-3:]
    s = jnp.einsum('...qhgd,...shd->...hgqs', q, k).astype(jnp.float32) * (hd ** -0.5)
    bias = rel_bias.astype(jnp.float32)[rel_bucket(dist)]
    bias = jnp.moveaxis(bias, -1, 0).reshape(hkv, grp, *dist.shape)
    s = jnp.where(valid, s + bias, MASK_VALUE)
    sink = sinks.astype(jnp.float32).reshape(hkv, grp, 1, 1)
    m = jnp.maximum(jnp.max(s, axis=-1, keepdims=True), sink)
    p = jnp.exp(s - m)
    probs = p / (jnp.sum(p, axis=-1, keepdims=True) + jnp.exp(sink - m))
    return jnp.einsum('...hgqs,...shd->...qhgd', probs.astype(v.dtype), v)


def banded_window_attention(q, k, v, rel_bias, sinks):
    b, t_len, hkv, grp, hd = q.shape
    bs = WINDOW
    nb = t_len // bs
    qb = q.reshape(b, nb, bs, hkv, grp, hd)
    kb = k.reshape(b, nb, bs, hkv, hd)
    vb = v.reshape(b, nb, bs, hkv, hd)
    kk = jnp.concatenate([jnp.concatenate([jnp.zeros_like(kb[:, :1]), kb[:, :-1]], axis=1), kb], axis=2)
    vv = jnp.concatenate([jnp.concatenate([jnp.zeros_like(vb[:, :1]), vb[:, :-1]], axis=1), vb], axis=2)
    dist = (jnp.arange(bs)[:, None] + bs) - jnp.arange(2 * bs)[None, :]
    key_ok = (jnp.arange(nb)[:, None] * bs + jnp.arange(2 * bs)[None, :] - bs) >= 0
    valid = ((dist >= 0) & (dist < WINDOW))[None, None, None, None] & key_ok[None, :, None, None, None, :]
    o = sink_attention(qb, kk, vv, dist, valid, rel_bias, sinks)
    return o.reshape(b, t_len, hkv * grp * hd)


def window_attention_step(q, k, v, k_past, v_past, rel_bias, sinks):
    b, t_len = q.shape[:2]
    k_all = jnp.concatenate([k_past.astype(k.dtype), k], axis=1)
    v_all = jnp.concatenate([v_past.astype(v.dtype), v], axis=1)
    n_keys = k_all.shape[1]
    dist = (jnp.arange(t_len)[:, None] + k_past.shape[1]) - jnp.arange(n_keys)[None, :]
    valid = (dist >= 0) & (dist < WINDOW)
    o = sink_attention(q, k_all, v_all, dist, valid, rel_bias, sinks)
    return o.reshape(b, t_len, -1), k_all[:, n_keys - WINDOW:], v_all[:, n_keys - WINDOW:]


def chunk_mlp(v, ln_g, ln_b, w_s, b_s):
    b, t_len, width = v.shape
    vn = layer_norm(v, ln_g, ln_b)
    size = t_len if t_len <= CHUNK else CHUNK
    n_chunks = -(-t_len // size)
    pad = n_chunks * size - t_len
    vc = jnp.pad(vn, ((0, 0), (0, pad), (0, 0))).reshape(b, n_chunks, size, CMLP_GROUPS, width // CMLP_GROUPS)
    w_causal = jnp.tril(w_s[:, :size, :size])
    mixed = jnp.einsum('gts,bcsgw->bctgw', w_causal, vc) + jnp.transpose(b_s[:, :size])[None, None, :, :, None]
    return mixed.reshape(b, n_chunks * size, width)[:, :t_len], vn


def ssd_scan(x, dt, a, bm, cm, h0):
    b, t_len, n_heads, p_dim = x.shape
    n_grp, n_st = bm.shape[2], bm.shape[3]
    rep = n_heads // n_grp
    size = t_len if t_len <= SSD_CHUNK else SSD_CHUNK
    nc = -(-t_len // size)
    pad = nc * size - t_len
    padt = lambda t: jnp.pad(t, [(0, 0), (0, pad)] + [(0, 0)] * (t.ndim - 2))
    xc = padt(x).reshape(b, nc, size, n_grp, rep, p_dim)
    dtc = padt(dt).reshape(b, nc, size, n_grp, rep)
    bc = padt(bm).reshape(b, nc, size, n_grp, n_st)
    cc = padt(cm).reshape(b, nc, size, n_grp, n_st)
    acs = jnp.cumsum(dtc * a.reshape(n_grp, rep), axis=2)
    causal = jnp.tril(jnp.ones((size, size), dtype=bool))
    seg = acs[:, :, :, None] - acs[:, :, None, :]
    decay = jnp.exp(jnp.where(causal[:, :, None, None], seg, -jnp.inf))
    cb = jnp.einsum('bctgn,bcsgn->bctsg', cc, bc)
    w = cb[..., None] * decay * dtc[:, :, None]
    y_diag = jnp.einsum('bctsgr,bcsgrp->bctgrp', w, xc)
    xw = xc * (jnp.exp(acs[:, :, -1:] - acs) * dtc)[..., None]
    chunk_states = jnp.einsum('bcsgn,bcsgrp->bcgrpn', bc, xw)
    chunk_decay = jnp.exp(acs[:, :, -1])

    def step(h, inp):
        dec, st = inp
        return h * dec[..., None, None] + st, h

    h_last, h_prev = lax.scan(step, h0.reshape(b, n_grp, rep, p_dim, n_st),
                              (jnp.moveaxis(chunk_decay, 1, 0), jnp.moveaxis(chunk_states, 1, 0)))
    h_prev = jnp.moveaxis(h_prev, 0, 1)
    y_off = jnp.einsum('bctgn,bcgrpn->bctgrp', cc, h_prev) * jnp.exp(acs)[..., None]
    y = (y_diag + y_off).reshape(b, nc * size, n_heads, p_dim)[:, :t_len]
    return y, h_last.reshape(b, n_heads, p_dim, n_st)


def even_layer(x, conv_buf, k_past, v_past, g_pre, w_in, conv_w, rel_bias, sinks, w_out, g_post):
    b, t_len, _ = x.shape
    h = rms_norm(x, g_pre)
    a_b, a_c, a_h, a_gate, q, k, v, att_gate = split_cols(h @ w_in, EVEN_IN_WIDTHS)
    conv_out, new_conv = causal_dwconv(a_c * a_h, conv_buf, conv_w)
    y_a = a_b * conv_out * jax.nn.silu(a_gate)
    q = q.reshape(b, t_len, ATT_KV_HEADS, ATT_GROUP, HEAD_DIM)
    k = k.reshape(b, t_len, ATT_KV_HEADS, HEAD_DIM)
    v = v.reshape(b, t_len, ATT_KV_HEADS, HEAD_DIM)
    if k_past is None:
        o = banded_window_attention(q, k, v, rel_bias, sinks)
        new_k, new_v = k[:, t_len - WINDOW:], v[:, t_len - WINDOW:]
    else:
        o, new_k, new_v = window_attention_step(q, k, v, k_past, v_past, rel_bias, sinks)
    y_b = o * jax.nn.silu(att_gate)
    y = jnp.concatenate([y_a, y_b], axis=-1) @ w_out
    return x + rms_norm(y, g_post), new_conv, new_k, new_v


def odd_layer(x, conv_buf, ssm_state, g_pre, w_in, ln_g, ln_b, w_s, b_s, conv_w, conv_b, dt_bias, a_log, d_skip, gate_norm_g, w_out, g_post):
    b, t_len, _ = x.shape
    f32 = jnp.float32
    h = rms_norm(x, g_pre)
    u, v, c_gate, z, xbc, dt_raw = split_cols(h @ w_in, ODD_IN_WIDTHS)
    mixed, v_rows = chunk_mlp(v, ln_g, ln_b, w_s, b_s)
    y_c = u * mixed * jax.nn.silu(c_gate)
    conv_out, new_conv = causal_dwconv(xbc, conv_buf, conv_w)
    xbc = jax.nn.silu(conv_out + conv_b)
    xs, bm, cm = split_cols(xbc, (SSD_WIDTH, SSD_GROUPS * SSD_STATE, SSD_GROUPS * SSD_STATE))
    xs = xs.reshape(b, t_len, SSD_HEADS, SSD_HEAD_DIM).astype(f32)
    bm = bm.reshape(b, t_len, SSD_GROUPS, SSD_STATE).astype(f32)
    cm = cm.reshape(b, t_len, SSD_GROUPS, SSD_STATE).astype(f32)
    dt = jax.nn.softplus(dt_raw.astype(f32) + dt_bias.astype(f32))
    a = -jnp.exp(a_log.astype(f32))
    y_ssd, new_state = ssd_scan(xs, dt, a, bm, cm, ssm_state.astype(f32))
    y_ssd = y_ssd + d_skip.astype(f32)[:, None] * xs
    grp_w = SSD_WIDTH // SSD_GROUPS
    gated = y_ssd.reshape(b, t_len, SSD_GROUPS, grp_w) * jax.nn.silu(z.astype(f32)).reshape(b, t_len, SSD_GROUPS, grp_w)
    gated = gated * lax.rsqrt(jnp.mean(gated * gated, axis=-1, keepdims=True) + NORM_EPS)
    y_d = (gated.reshape(b, t_len, SSD_WIDTH) * gate_norm_g.astype(f32)).astype(x.dtype)
    y = jnp.concatenate([y_c, y_d], axis=-1) @ w_out
    return x + rms_norm(y, g_post), v_rows, new_conv, new_state


def setup_inputs(seed: int = 0) -> dict:
    key = jax.random.key(seed)
    ks = iter(jax.random.split(key, 40))
    f32 = jnp.float32
    d = D_MODEL

    def nrm(shape, scale):
        return jax.random.normal(next(ks), shape, f32) * scale

    def gain(n):
        return 1.0 + nrm((n,), 0.05)

    u_dt = jax.random.uniform(next(ks), (SSD_HEADS,), f32)
    dt0 = jnp.exp(u_dt * (math.log(0.1) - math.log(1e-3)) + math.log(1e-3))
    a_init = jax.random.uniform(next(ks), (SSD_HEADS,), f32, 1.0, 16.0)
    return {
        'x_prompt': nrm((BATCH, SEQ, d), 1.0),
        'x_sample': nrm((DEC_BATCH, DEC_SEQ, d), 1.0),
        'state_conv_a': nrm((DEC_BATCH, CONV_A_K - 1, CONV_A_WIDTH), 1.0),
        'cache_win_k': nrm((DEC_BATCH, WINDOW, ATT_KV_HEADS, HEAD_DIM), 1.0),
        'cache_win_v': nrm((DEC_BATCH, WINDOW, ATT_KV_HEADS, HEAD_DIM), 1.0),
        'state_conv_d': nrm((DEC_BATCH, SSD_CONV_K - 1, SSD_CONV_DIM), 1.0),
        'state_ssm': nrm((DEC_BATCH, SSD_HEADS, SSD_HEAD_DIM, SSD_STATE), 0.5),
        'rel_bias': nrm((NUM_BUCKETS, ATT_HEADS), 0.5),
        'l0_g_pre': gain(d),
        'l0_w_in': nrm((d, sum(EVEN_IN_WIDTHS)), d ** -0.5),
        'l0_conv_w': nrm((CONV_A_K, CONV_A_WIDTH), CONV_A_K ** -0.5),
        'l0_sinks': nrm((ATT_HEADS,), 0.5),
        'l0_w_out': nrm((2 * MIX_HALF, d), (2 * MIX_HALF) ** -0.5),
        'l0_g_post': gain(d),
        'l1_g_pre': gain(d),
        'l1_w_in': nrm((d, sum(ODD_IN_WIDTHS)), d ** -0.5),
        'l1_ln_g': gain(CMLP_WIDTH),
        'l1_ln_b': nrm((CMLP_WIDTH,), 0.02),
        'l1_w_s': nrm((CMLP_GROUPS, CHUNK, CHUNK), CHUNK ** -0.5),
        'l1_b_s': 1.0 + nrm((CMLP_GROUPS, CHUNK), 0.1),
        'l1_conv_w': nrm((SSD_CONV_K, SSD_CONV_DIM), SSD_CONV_K ** -0.5),
        'l1_conv_b': nrm((SSD_CONV_DIM,), 0.02),
        'l1_dt_bias': dt0 + jnp.log(-jnp.expm1(-dt0)),
        'l1_a_log': jnp.log(a_init),
        'l1_d_skip': 1.0 + nrm((SSD_HEADS,), 0.1),
        'l1_gate_norm_g': gain(SSD_WIDTH),
        'l1_w_out': nrm((2 * MIX_HALF, d), (2 * MIX_HALF) ** -0.5),
        'l1_g_post': gain(d),
    }


def reference(x_prompt, x_sample, state_conv_a, cache_win_k, cache_win_v, state_conv_d, state_ssm, rel_bias,
              l0_g_pre, l0_w_in, l0_conv_w, l0_sinks, l0_w_out, l0_g_post,
              l1_g_pre, l1_w_in, l1_ln_g, l1_ln_b, l1_w_s, l1_b_s, l1_conv_w, l1_conv_b, l1_dt_bias, l1_a_log,
              l1_d_skip, l1_gate_norm_g, l1_w_out, l1_g_post):
    y_prompt, y_sample = x_prompt, x_sample
    b_p = x_prompt.shape[0]
    for layer in range(DEPTH):
        if layer % 2 == 0:
            even_w = (l0_g_pre, l0_w_in, l0_conv_w, rel_bias, l0_sinks, l0_w_out, l0_g_post)
            zero_conv_a = jnp.zeros((b_p, CONV_A_K - 1, CONV_A_WIDTH), x_prompt.dtype)
            y_prompt, p_conv_a, p_win_k, p_win_v = even_layer(y_prompt, zero_conv_a, None, None, *even_w)
            y_sample, s_conv_a, s_win_k, s_win_v = even_layer(y_sample, state_conv_a, cache_win_k, cache_win_v, *even_w)
        else:
            odd_w = (l1_g_pre, l1_w_in, l1_ln_g, l1_ln_b, l1_w_s, l1_b_s, l1_conv_w, l1_conv_b, l1_dt_bias,
                     l1_a_log, l1_d_skip, l1_gate_norm_g, l1_w_out, l1_g_post)
            zero_conv_d = jnp.zeros((b_p, SSD_CONV_K - 1, SSD_CONV_DIM), x_prompt.dtype)
            zero_ssm = jnp.zeros((b_p, SSD_HEADS, SSD_HEAD_DIM, SSD_STATE), jnp.float32)
            y_prompt, _, p_conv_d, p_ssm = odd_layer(y_prompt, zero_conv_d, zero_ssm, *odd_w)
            y_sample, s_chunk_v, s_conv_d, s_ssm = odd_layer(y_sample, state_conv_d, state_ssm, *odd_w)
    return (y_prompt, y_sample, p_conv_a, s_conv_a, p_win_k, p_win_v, s_win_k, s_win_v, s_chunk_v,
            p_conv_d, s_conv_d, p_ssm, s_ssm)
```

```python
import functools
import math

import jax
import jax.numpy as jnp
from jax import lax
from jax.experimental import pallas as pl
from jax.experimental.pallas import tpu as pltpu

F32 = jnp.float32
BF16 = jnp.bfloat16

D_MODEL = 2048
HALF = 1024
HEAD_DIM = 64
N_HEADS = 16
N_KV = 4
GROUP = 4
WINDOW = 128
NUM_BUCKETS = 32
MAX_DISTANCE = 128
CMLP_GROUPS = 8
CHUNK = 128
SSD_HEADS = 16
SSD_STATE = 128
SSD_GROUPS = 2
SSD_CONV_DIM = HALF + 2 * SSD_GROUPS * SSD_STATE
DEC_SEQ = 4
NORM_EPS = 1e-6
MASK_VALUE = -1e30

ROW_TILE = 512
VMEM_LIMIT = 56 * 1024 * 1024


def _params(n_axes=1):
    return pltpu.CompilerParams(dimension_semantics=("arbitrary",) * n_axes,
                                vmem_limit_bytes=VMEM_LIMIT)


def _resident(shape):
    nd = len(shape)
    return pl.BlockSpec(shape, lambda *_: (0,) * nd, pipeline_mode=pl.Buffered(1))


def _rows(tile, width):
    return pl.BlockSpec((tile, width), lambda i: (i, 0))


def _rms_bf16(x, g):
    ms = jnp.mean(x * x, axis=-1, keepdims=True)
    return (x * lax.rsqrt(ms + NORM_EPS) * g).astype(BF16)


def _silu(x):
    return x * jax.nn.sigmoid(x)


def _dot(a, b):
    return jnp.dot(a, b, preferred_element_type=F32)


def _dot_nt(a, b):
    return lax.dot_general(a, b, (((1,), (1,)), ((), ())), preferred_element_type=F32)


def _dot_tn(a, b):
    return lax.dot_general(a, b, (((0,), (0,)), ((), ())), preferred_element_type=F32)


def _shift_rows(x, k, prev_rows=None):
    r = pltpu.roll(x, k, 0)
    if prev_rows is None:
        return r
    row = lax.broadcasted_iota(jnp.int32, x.shape, 0)
    n_prev = prev_rows.shape[0]
    for t in range(k):
        src = n_prev - k + t
        r = jnp.where(row == t, prev_rows[src:src + 1, :], r)
    return r


def _out_proj_kernel(ya_ref, yb_ref, x_ref, w_ref, g_ref, o_ref):
    y = _dot(ya_ref[...], w_ref[0:HALF, :]) + _dot(yb_ref[...], w_ref[HALF:2 * HALF, :])
    ms = jnp.mean(y * y, axis=-1, keepdims=True)
    o_ref[...] = x_ref[...] + y * lax.rsqrt(ms + NORM_EPS) * g_ref[...]


def _out_proj(ya, yb, x, w_bf, g):
    t = x.shape[0]
    tile = min(ROW_TILE, t)
    return pl.pallas_call(
        _out_proj_kernel,
        grid=(t // tile,),
        in_specs=[_rows(tile, HALF), _rows(tile, HALF), _rows(tile, D_MODEL),
                  _resident((2 * HALF, D_MODEL)), _resident((1, D_MODEL))],
        out_specs=_rows(tile, D_MODEL),
        out_shape=jax.ShapeDtypeStruct((t, D_MODEL), F32),
        compiler_params=_params(),
        name="out_proj",
    )(ya, yb, x, w_bf, g)


CONV_A_CHUNK = 256


def _conv_a_kernel(*refs, sample):
    if sample:
        x_ref, g_ref, w_ref, cw_ref, p1_ref, p2_ref, ya_ref, s_ref = refs
    else:
        x_ref, g_ref, w_ref, cw_ref, ya_ref, s_ref = refs
    tile = x_ref.shape[0]
    cc = CONV_A_CHUNK
    h = _rms_bf16(x_ref[...], g_ref[...])
    if not sample:
        @pl.when(pl.program_id(0) == 0)
        def _():
            s_ref[...] = jnp.zeros_like(s_ref)
    for c in range(HALF // cc):
        lanes = slice(c * cc, (c + 1) * cc)
        proj = _dot(h, w_ref[c])
        a_b, a_c, a_h, a_g = (proj[:, j * cc:(j + 1) * cc] for j in range(4))
        s = a_c * a_h
        if sample:
            t_in = lax.broadcasted_iota(jnp.int32, s.shape, 0) % DEC_SEQ
            p1 = jnp.where(t_in >= 1, _shift_rows(s, 1), 0.0) + p1_ref[:, lanes]
            p2 = jnp.where(t_in >= 2, _shift_rows(s, 2), 0.0) + p2_ref[:, lanes]
            s_ref[:, lanes] = s
        else:
            prev = s_ref[:, lanes]
            p1 = _shift_rows(s, 1, prev)
            p2 = _shift_rows(s, 2, prev)
            s_ref[:, lanes] = s[tile - 8:tile, :]
        conv = p2 * cw_ref[0:1, lanes] + p1 * cw_ref[1:2, lanes] + s * cw_ref[2:3, lanes]
        ya_ref[:, lanes] = (a_b * conv * _silu(a_g)).astype(BF16)


def _conv_a(x, g_pre, w_chunks, conv_w, state_rows=None):
    t = x.shape[0]
    sample = state_rows is not None
    tile = t if sample else min(ROW_TILE, t)
    n_chunks = HALF // CONV_A_CHUNK
    in_specs = [_rows(tile, D_MODEL), _resident((1, D_MODEL)),
                _resident((n_chunks, D_MODEL, 4 * CONV_A_CHUNK)), _resident((3, HALF))]
    args = [x, g_pre, w_chunks, conv_w]
    if sample:
        in_specs += [_rows(tile, HALF), _rows(tile, HALF)]
        args += list(state_rows)
        s_spec, s_shape = _rows(tile, HALF), (t, HALF)
    else:
        s_spec, s_shape = pl.BlockSpec((8, HALF), lambda i: (0, 0)), (8, HALF)
    return pl.pallas_call(
        functools.partial(_conv_a_kernel, sample=sample),
        grid=(t // tile,),
        in_specs=in_specs,
        out_specs=[_rows(tile, HALF), s_spec],
        out_shape=[jax.ShapeDtypeStruct((t, HALF), BF16), jax.ShapeDtypeStruct(s_shape, F32)],
        compiler_params=_params(),
        name="conv_a_sample" if sample else "conv_a_prompt",
    )(*args)


def _rel_bucket(dist):
    max_exact = NUM_BUCKETS // 2
    d = jnp.maximum(dist, 0)
    ratio = jnp.maximum(d, max_exact).astype(F32) / max_exact
    large = max_exact + (jnp.log(ratio) / math.log(MAX_DISTANCE / max_exact)
                         * (NUM_BUCKETS - max_exact)).astype(jnp.int32)
    return jnp.where(d < max_exact, d, jnp.minimum(large, NUM_BUCKETS - 1))


def _attn_softmax_pv(s, sink, v_bf):
    m = jnp.maximum(jnp.max(s, axis=-1, keepdims=True), sink)
    p = jnp.exp(s - m)
    den = jnp.sum(p, axis=-1, keepdims=True) + jnp.exp(sink - m)
    return _dot(p.astype(BF16), v_bf) / den


def _attn_prompt_kernel(x_ref, g_ref, w_ref, bias_ref, sink_ref, yb_ref, kwin_ref, vwin_ref,
                        q_scr, gate_scr, k_scr, v_scr):
    tile = x_ref.shape[0]
    i = pl.program_id(0)
    kv_w = N_KV * HEAD_DIM
    h = _rms_bf16(x_ref[...], g_ref[...])

    @pl.when(i == 0)
    def _():
        k_scr[0:WINDOW, :] = jnp.zeros((WINDOW, kv_w), BF16)
        v_scr[0:WINDOW, :] = jnp.zeros((WINDOW, kv_w), BF16)

    q_scr[...] = _dot(h, w_ref[:, 0:HALF]).astype(BF16)
    k = _dot(h, w_ref[:, HALF:HALF + kv_w])
    v = _dot(h, w_ref[:, HALF + kv_w:HALF + 2 * kv_w])
    gate_scr[...] = _dot(h, w_ref[:, HALF + 2 * kv_w:2 * HALF + 2 * kv_w])
    k_scr[WINDOW:WINDOW + tile, :] = k.astype(BF16)
    v_scr[WINDOW:WINDOW + tile, :] = v.astype(BF16)
    kwin_ref[...] = k[tile - WINDOW:tile, :]
    vwin_ref[...] = v[tile - WINDOW:tile, :]

    def block(n, carry):
        r0 = pl.multiple_of(n * WINDOW, WINDOW)
        first = jnp.where(jnp.logical_and(i == 0, n == 0), 0, 1)
        for hk in range(N_KV):
            kk = k_scr[pl.ds(r0, 2 * WINDOW), hk * HEAD_DIM:(hk + 1) * HEAD_DIM]
            vv = v_scr[pl.ds(r0, 2 * WINDOW), hk * HEAD_DIM:(hk + 1) * HEAD_DIM]
            for g in range(GROUP):
                head = hk * GROUP + g
                lanes = slice(head * HEAD_DIM, (head + 1) * HEAD_DIM)
                s = _dot_nt(q_scr[pl.ds(r0, WINDOW), lanes], kk) + bias_ref[first, head]
                o = _attn_softmax_pv(s, sink_ref[head], vv)
                yb_ref[pl.ds(r0, WINDOW), lanes] = (o * _silu(gate_scr[pl.ds(r0, WINDOW), lanes])).astype(BF16)
        return carry

    lax.fori_loop(0, tile // WINDOW, block, 0)
    k_scr[0:WINDOW, :] = k_scr[tile:tile + WINDOW, :]
    v_scr[0:WINDOW, :] = v_scr[tile:tile + WINDOW, :]


def _prompt_bias(rel_bias):
    qi = jnp.arange(WINDOW)[:, None]
    kj = jnp.arange(2 * WINDOW)[None, :]
    dist = qi + WINDOW - kj
    bias = jnp.moveaxis(rel_bias.astype(F32)[_rel_bucket(dist)], -1, 0)
    valid = (dist >= 0) & (dist < WINDOW)
    normal = jnp.where(valid[None], bias, MASK_VALUE)
    first = jnp.where((valid & (kj >= WINDOW))[None], bias, MASK_VALUE)
    return jnp.stack([first, normal])


def _attn_prompt(x, g_pre, w_attn, bias, sinks):
    t = x.shape[0]
    tile = min(ROW_TILE, t)
    kv_w = N_KV * HEAD_DIM
    win_spec = pl.BlockSpec((WINDOW, kv_w), lambda i: (0, 0))
    return pl.pallas_call(
        _attn_prompt_kernel,
        grid=(t // tile,),
        in_specs=[_rows(tile, D_MODEL), _resident((1, D_MODEL)), _resident(w_attn.shape),
                  _resident(bias.shape), pl.BlockSpec(memory_space=pltpu.SMEM)],
        out_specs=[_rows(tile, HALF), win_spec, win_spec],
        out_shape=[jax.ShapeDtypeStruct((t, HALF), BF16),
                   jax.ShapeDtypeStruct((WINDOW, kv_w), F32), jax.ShapeDtypeStruct((WINDOW, kv_w), F32)],
        scratch_shapes=[pltpu.VMEM((tile, HALF), BF16), pltpu.VMEM((tile, HALF), F32),
                        pltpu.VMEM((tile + WINDOW, kv_w), BF16), pltpu.VMEM((tile + WINDOW, kv_w), BF16)],
        compiler_params=_params(),
        name="attn_prompt",
    )(x, g_pre, w_attn, bias, sinks)


def _norm_proj_kernel(x_ref, g_ref, w_ref, o_ref):
    o_ref[...] = _dot(_rms_bf16(x_ref[...], g_ref[...]), w_ref[...])


def _norm_proj(x, g_pre, w_bf):
    t, n = x.shape[0], w_bf.shape[1]
    return pl.pallas_call(
        _norm_proj_kernel,
        grid=(1,),
        in_specs=[_resident((t, D_MODEL)), _resident((1, D_MODEL)), _resident(w_bf.shape)],
        out_specs=_resident((t, n)),
        out_shape=jax.ShapeDtypeStruct((t, n), F32),
        compiler_params=_params(),
        name="norm_proj",
    )(x, g_pre, w_bf)


ATTN_S_BATCH = 16
KEYS_PAD = WINDOW + 8


def _attn_sample_kernel(proj_ref, ck_ref, cv_ref, bias_ref, sink_ref, yb_ref, nk_ref, nv_ref, k_scr, v_scr):
    kv_w = N_KV * HEAD_DIM
    row8 = lax.broadcasted_iota(jnp.int32, (8, kv_w), 0)
    lane_head = lax.broadcasted_iota(jnp.int32, (8, kv_w), 1) // HEAD_DIM
    lower = row8 < DEC_SEQ
    pick = [jnp.where(lane_head == 2 * hp + jnp.where(lower, 0, 1), 1.0, 0.0).astype(F32) for hp in range(2)]
    lower_w = lax.broadcasted_iota(jnp.int32, (8, HALF), 0) < DEC_SEQ
    k_scr[WINDOW + DEC_SEQ:KEYS_PAD, :] = jnp.zeros((KEYS_PAD - WINDOW - DEC_SEQ, kv_w), BF16)
    v_scr[WINDOW + DEC_SEQ:KEYS_PAD, :] = jnp.zeros((KEYS_PAD - WINDOW - DEC_SEQ, kv_w), BF16)

    def pair(p, carry):
        r0 = pl.multiple_of(p * 8, 8)
        rows = proj_ref[pl.ds(r0, 8), :]
        q8 = rows[:, 0:HALF]
        k8 = rows[:, HALF:HALF + kv_w]
        v8 = rows[:, HALF + kv_w:HALF + 2 * kv_w]
        gate8 = rows[:, HALF + 2 * kv_w:]
        out8 = []
        for sub in range(2):
            b = 2 * p + sub
            q_swap = pltpu.roll(q8, 4, 0)
            q_dup = jnp.where(lower_w, q8, q_swap) if sub == 0 else jnp.where(lower_w, q_swap, q8)
            k_new = k8[0:4] if sub == 0 else k8[4:8]
            v_new = v8[0:4] if sub == 0 else v8[4:8]
            kc = ck_ref[b]
            vc = cv_ref[b]
            nk_ref[b, 0:WINDOW - DEC_SEQ, :] = kc[DEC_SEQ:WINDOW, :]
            nv_ref[b, 0:WINDOW - DEC_SEQ, :] = vc[DEC_SEQ:WINDOW, :]
            nk_ref[b, WINDOW - DEC_SEQ:WINDOW, :] = k_new
            nv_ref[b, WINDOW - DEC_SEQ:WINDOW, :] = v_new
            k_scr[0:WINDOW, :] = kc.astype(BF16)
            v_scr[0:WINDOW, :] = vc.astype(BF16)
            k_scr[WINDOW:WINDOW + DEC_SEQ, :] = k_new.astype(BF16)
            v_scr[WINDOW:WINDOW + DEC_SEQ, :] = v_new.astype(BF16)
            q_bd = jnp.concatenate(
                [q_dup[:, g * kv_w:(g + 1) * kv_w] * pick[hp] for g in range(GROUP) for hp in range(2)], axis=0)
            s = _dot_nt(q_bd.astype(BF16), k_scr[...]) + bias_ref[...]
            o = _attn_softmax_pv(s, sink_ref[:, 0:1], v_scr[...])
            out_g = []
            for g in range(GROUP):
                acc = None
                for hp in range(2):
                    piece = o[(2 * g + hp) * 8:(2 * g + hp + 1) * 8, :] * pick[hp]
                    piece = piece + pltpu.roll(piece, 4, 0)
                    acc = piece if acc is None else acc + piece
                out_g.append(acc)
            out8.append(jnp.concatenate(out_g, axis=1))
        o8 = jnp.where(lower_w, out8[0], out8[1])
        yb_ref[pl.ds(r0, 8), :] = (o8 * _silu(gate8)).astype(BF16)
        return carry

    lax.fori_loop(0, ATTN_S_BATCH // 2, pair, 0)


def _sample_bias(rel_bias, sinks):
    t = jnp.arange(DEC_SEQ)[:, None]
    j = jnp.arange(KEYS_PAD)[None, :]
    dist = t + WINDOW - j
    valid = (dist >= 0) & (dist < WINDOW) & (j < WINDOW + DEC_SEQ)
    bias = jnp.where(valid[:, :, None], rel_bias.astype(F32)[_rel_bucket(dist)], MASK_VALUE)
    bias = bias.reshape(DEC_SEQ, KEYS_PAD, N_KV, GROUP).transpose(3, 2, 0, 1).reshape(N_HEADS * DEC_SEQ, KEYS_PAD)
    sink = jnp.broadcast_to(sinks.astype(F32).reshape(N_KV, GROUP).T[:, :, None], (GROUP, N_KV, DEC_SEQ))
    return bias, jnp.broadcast_to(sink.reshape(N_HEADS * DEC_SEQ, 1), (N_HEADS * DEC_SEQ, 128))


def _attn_sample(proj, cache_k, cache_v, bias, sink):
    n_seq = cache_k.shape[0]
    kv_w = N_KV * HEAD_DIM
    bb = ATTN_S_BATCH
    cache_spec = pl.BlockSpec((bb, WINDOW, kv_w), lambda i: (i, 0, 0))
    return pl.pallas_call(
        _attn_sample_kernel,
        grid=(n_seq // bb,),
        in_specs=[_rows(bb * DEC_SEQ, proj.shape[1]), cache_spec, cache_spec,
                  _resident(bias.shape), _resident(sink.shape)],
        out_specs=[_rows(bb * DEC_SEQ, HALF), cache_spec, cache_spec],
        out_shape=[jax.ShapeDtypeStruct((n_seq * DEC_SEQ, HALF), BF16),
                   jax.ShapeDtypeStruct(cache_k.shape, F32), jax.ShapeDtypeStruct(cache_v.shape, F32)],
        scratch_shapes=[pltpu.VMEM((KEYS_PAD, kv_w), BF16), pltpu.VMEM((KEYS_PAD, kv_w), BF16)],
        compiler_params=_params(),
        name="attn_sample",
    )(proj, cache_k, cache_v, bias, sink)


def _prep_layer0(g_pre, w_in, conv_w, rel_bias, sinks, w_out, g_post):
    n_chunks = HALF // CONV_A_CHUNK
    w_conv = (w_in[:, :4 * HALF].reshape(D_MODEL, 4, n_chunks, CONV_A_CHUNK)
              .transpose(2, 0, 1, 3).reshape(n_chunks, D_MODEL, 4 * CONV_A_CHUNK).astype(BF16))
    kv_w = N_KV * HEAD_DIM
    w_q = w_in[:, 4 * HALF:5 * HALF] * (HEAD_DIM ** -0.5)
    w_kv = w_in[:, 5 * HALF:5 * HALF + 2 * kv_w]
    w_gate = w_in[:, 5 * HALF + 2 * kv_w:]

    def by_group(w):
        return w.reshape(D_MODEL, N_KV, GROUP, HEAD_DIM).transpose(0, 2, 1, 3).reshape(D_MODEL, HALF)

    w_attn = jnp.concatenate([w_q, w_kv, w_gate], axis=1).astype(BF16)
    w_attn_s = jnp.concatenate([by_group(w_q), w_kv, by_group(w_gate)], axis=1).astype(BF16)
    w_out_b = w_out[HALF:].reshape(N_KV, GROUP, HEAD_DIM, D_MODEL).transpose(1, 0, 2, 3).reshape(HALF, D_MODEL)
    return dict(
        g_pre=g_pre.reshape(1, D_MODEL), w_conv=w_conv, conv_w=conv_w,
        w_attn=w_attn, w_attn_s=w_attn_s, rel_bias=rel_bias, sinks=sinks,
        w_out=w_out.astype(BF16), w_out_s=jnp.concatenate([w_out[:HALF], w_out_b], axis=0).astype(BF16),
        g_post=g_post.reshape(1, D_MODEL))


def _layer0_prompt(x, w):
    ya, s_tail = _conv_a(x, w['g_pre'], w['w_conv'], w['conv_w'])
    yb, kwin, vwin = _attn_prompt(x, w['g_pre'], w['w_attn'], _prompt_bias(w['rel_bias']), w['sinks'])
    y = _out_proj(ya, yb, x, w['w_out'], w['g_post'])
    return (y, s_tail[6:8], kwin.reshape(WINDOW, N_KV, HEAD_DIM), vwin.reshape(WINDOW, N_KV, HEAD_DIM))


def _layer0_sample(x, conv_state, cache_k, cache_v, w):
    n_seq = x.shape[0]
    rows = x.reshape(n_seq * DEC_SEQ, D_MODEL)
    zero = jnp.zeros((n_seq, 1, HALF), F32)
    p1 = jnp.concatenate([conv_state[:, 1:2], zero, zero, zero], axis=1).reshape(n_seq * DEC_SEQ, HALF)
    p2 = jnp.concatenate([conv_state[:, 0:1], conv_state[:, 1:2], zero, zero], axis=1).reshape(n_seq * DEC_SEQ, HALF)
    ya, s = _conv_a(rows, w['g_pre'], w['w_conv'], w['conv_w'], (p1, p2))
    kv_w = N_KV * HEAD_DIM
    proj = _norm_proj(rows, w['g_pre'], w['w_attn_s'])
    bias, sink = _sample_bias(w['rel_bias'], w['sinks'])
    yb, new_k, new_v = _attn_sample(proj, cache_k.reshape(n_seq, WINDOW, kv_w), cache_v.reshape(n_seq, WINDOW, kv_w),
                                    bias, sink)
    y = _out_proj(ya, yb, rows, w['w_out_s'], w['g_post'])
    return (y.reshape(n_seq, DEC_SEQ, D_MODEL), s.reshape(n_seq, DEC_SEQ, HALF)[:, DEC_SEQ - 2:],
            new_k.reshape(cache_k.shape), new_v.reshape(cache_v.shape))


def _layer_norm(v, g, b):
    xc = v - jnp.mean(v, axis=-1, keepdims=True)
    return xc * lax.rsqrt(jnp.mean(xc * xc, axis=-1, keepdims=True) + NORM_EPS) * g + b


def _cmlp_prompt_kernel(x_ref, g_ref, w_ref, lng_ref, lnb_ref, ws_ref, bs_ref, yc_ref, vn_scr):
    tile = x_ref.shape[0]
    h = _rms_bf16(x_ref[...], g_ref[...])
    v = _dot(h, w_ref[:, HALF:2 * HALF])
    vn_scr[...] = _layer_norm(v, lng_ref[...], lnb_ref[...]).astype(BF16)
    gw = HALF // CMLP_GROUPS
    cols = 2 * gw
    for cb in range(HALF // cols):
        u = _dot(h, w_ref[:, cb * cols:(cb + 1) * cols])
        gate = _silu(_dot(h, w_ref[:, 2 * HALF + cb * cols:2 * HALF + (cb + 1) * cols]))
        for gi in range(2):
            grp = 2 * cb + gi
            lanes = slice(grp * gw, (grp + 1) * gw)
            for n in range(tile // CHUNK):
                rows = slice(n * CHUNK, (n + 1) * CHUNK)
                mixed = _dot(ws_ref[grp], vn_scr[rows, lanes]) + bs_ref[grp]
                yc_ref[rows, lanes] = (u[rows, gi * gw:(gi + 1) * gw] * mixed
                                       * gate[rows, gi * gw:(gi + 1) * gw]).astype(BF16)


def _cmlp_prompt(x, g_pre, w_c, ln_g, ln_b, ws_tril, bs_rows):
    t = x.shape[0]
    tile = min(ROW_TILE, t)
    return pl.pallas_call(
        _cmlp_prompt_kernel,
        grid=(t // tile,),
        in_specs=[_rows(tile, D_MODEL), _resident((1, D_MODEL)), _resident(w_c.shape),
                  _resident((1, HALF)), _resident((1, HALF)), _resident(ws_tril.shape), _resident(bs_rows.shape)],
        out_specs=_rows(tile, HALF),
        out_shape=jax.ShapeDtypeStruct((t, HALF), BF16),
        scratch_shapes=[pltpu.VMEM((tile, HALF), BF16)],
        compiler_params=_params(),
        name="cmlp_prompt",
    )(x, g_pre, w_c, ln_g, ln_b, ws_tril, bs_rows)


def _cmlp_sample_kernel(x_ref, g_ref, w_ref, lng_ref, lnb_ref, coef_ref, bias_ref, yc_ref, vn_ref):
    t = x_ref.shape[0]
    h = _rms_bf16(x_ref[...], g_ref[...])
    u = _dot(h, w_ref[:, 0:HALF])
    vn = _layer_norm(_dot(h, w_ref[:, HALF:2 * HALF]), lng_ref[...], lnb_ref[...])
    gate = _silu(_dot(h, w_ref[:, 2 * HALF:3 * HALF]))
    vn_ref[...] = vn

    def tiled(a):
        return a.reshape(t // 8, 8, HALF)

    mixed = tiled(vn) * coef_ref[0][None] + bias_ref[...][None]
    for k in range(1, DEC_SEQ):
        mixed = mixed + tiled(pltpu.roll(vn, k, 0)) * coef_ref[k][None]
    yc_ref[...] = (u * mixed.reshape(t, HALF) * gate).astype(BF16)


def _cmlp_sample(x, g_pre, w_c, ln_g, ln_b, coef, bias):
    t = x.shape[0]
    return pl.pallas_call(
        _cmlp_sample_kernel,
        grid=(1,),
        in_specs=[_resident((t, D_MODEL)), _resident((1, D_MODEL)), _resident(w_c.shape),
                  _resident((1, HALF)), _resident((1, HALF)), _resident(coef.shape), _resident(bias.shape)],
        out_specs=[_resident((t, HALF)), _resident((t, HALF))],
        out_shape=[jax.ShapeDtypeStruct((t, HALF), BF16), jax.ShapeDtypeStruct((t, HALF), F32)],
        compiler_params=_params(),
        name="cmlp_sample",
    )(x, g_pre, w_c, ln_g, ln_b, coef, bias)


HEAD_LANES = 128
SSD_GW = HALF // SSD_GROUPS


def _softplus(x):
    return jnp.maximum(x, 0.0) + jnp.log1p(jnp.exp(-jnp.abs(x)))


def _split3(x):
    hi = x.astype(BF16)
    r1 = x - hi.astype(F32)
    mid = r1.astype(BF16)
    lo = (r1 - mid.astype(F32)).astype(BF16)
    return hi, mid, lo


def _group_norm_gate(y, z, gn):
    gated = y * _silu(z)
    parts = []
    for g in range(SSD_GROUPS):
        part = gated[:, g * SSD_GW:(g + 1) * SSD_GW]
        parts.append(part * lax.rsqrt(jnp.mean(part * part, axis=-1, keepdims=True) + NORM_EPS))
    return (jnp.concatenate(parts, axis=1) * gn).astype(BF16)


def _ssd_prompt_kernel(x_ref, g_ref, w_ref, cw_ref, cb_ref, dtb_ref, alog_ref, aloge_ref, dskip_ref, gn_ref,
                       e3_ref, tril3_ref, yd_ref, tail_ref, ssm_ref, xbc_scr, z_scr, dt_scr, ht_scr):
    tile = x_ref.shape[0]
    i = pl.program_id(0)
    cd = SSD_CONV_DIM
    h = _rms_bf16(x_ref[...], g_ref[...])

    @pl.when(i == 0)
    def _():
        tail_ref[...] = jnp.zeros_like(tail_ref)
        ht_scr[...] = jnp.zeros_like(ht_scr)

    z_scr[...] = _dot(h, w_ref[:, 0:HALF])
    raw = _dot(h, w_ref[:, HALF:HALF + cd])
    dt_scr[...] = _softplus(_dot(h, w_ref[:, HALF + cd:HALF + cd + HEAD_LANES]) + dtb_ref[...])
    prev = tail_ref[...]
    conv = raw * cw_ref[3:4, :] + cb_ref[...]
    for k in range(1, 4):
        conv = conv + _shift_rows(raw, k, prev) * cw_ref[3 - k:4 - k, :]
    xbc_scr[...] = _silu(conv)
    tail_ref[...] = raw[tile - 8:tile, :]

    a16 = -jnp.exp(alog_ref[...])
    a_e = -jnp.exp(aloge_ref[...])
    causal = (lax.broadcasted_iota(jnp.int32, (CHUNK, CHUNK), 0)
              >= lax.broadcasted_iota(jnp.int32, (CHUNK, CHUNK), 1))

    def chunk(n, carry):
        r0 = pl.multiple_of(n * CHUNK, CHUNK)
        rows = pl.ds(r0, CHUNK)
        xs = xbc_scr[rows, 0:HALF]
        dt16 = dt_scr[rows, :]
        dt_e = _dot(jnp.concatenate(_split3(dt16), axis=1), e3_ref[...])
        da_e = dt_e * a_e
        acs_e = _dot(tril3_ref[...], jnp.concatenate(_split3(da_e), axis=0))
        acs16 = _dot(tril3_ref[...], jnp.concatenate(_split3(dt16 * a16), axis=0))
        acs_t = acs16.T
        dt_t = dt16.T
        last_e = acs_e[CHUNK - 1:CHUNK, :]
        xs_bf = xs.astype(BF16)
        xw = (jnp.exp(last_e - acs_e) * dt_e * xs).astype(BF16)
        dec_e = jnp.exp(last_e)
        y_parts = []
        yoff_parts = []
        for g in range(SSD_GROUPS):
            c_g = xbc_scr[rows, HALF + 2 * SSD_STATE + g * SSD_STATE:HALF + 2 * SSD_STATE + (g + 1) * SSD_STATE].astype(BF16)
            b_g = xbc_scr[rows, HALF + g * SSD_STATE:HALF + (g + 1) * SSD_STATE].astype(BF16)
            cb = _dot_nt(c_g, b_g)
            h_prev = ht_scr[g]
            yoff_parts.append(_dot(c_g, h_prev.astype(BF16)))
            for r in range(SSD_HEADS // SSD_GROUPS):
                hd = g * (SSD_HEADS // SSD_GROUPS) + r
                seg = acs16[:, hd:hd + 1] - acs_t[hd:hd + 1, :]
                wgt = cb * jnp.exp(jnp.where(causal, seg, -jnp.inf)) * dt_t[hd:hd + 1, :]
                y_parts.append(_dot(wgt.astype(BF16), xs_bf[:, hd * HEAD_DIM:(hd + 1) * HEAD_DIM]))
            lanes = slice(g * SSD_GW, (g + 1) * SSD_GW)
            ht_scr[g] = h_prev * dec_e[:, lanes] + _dot_tn(b_g, xw[:, lanes])
        y = (jnp.concatenate(y_parts, axis=1) + jnp.concatenate(yoff_parts, axis=1) * jnp.exp(acs_e)
             + dskip_ref[...] * xs)
        yd_ref[rows, :] = _group_norm_gate(y, z_scr[rows, :], gn_ref[...])
        return carry

    lax.fori_loop(0, tile // CHUNK, chunk, 0)

    @pl.when(i == pl.num_programs(0) - 1)
    def _():
        for g in range(SSD_GROUPS):
            ssm_ref[g * SSD_GW:(g + 1) * SSD_GW, :] = ht_scr[g].T


def _ssd_prompt(x, w):
    t = x.shape[0]
    tile = min(ROW_TILE, t)
    cd = SSD_CONV_DIM
    consts = [w['g_pre'], w['w_ssd'], w['conv_w'], w['conv_b'], w['dt_bias16'], w['a_log16'], w['a_log_e'],
              w['d_skip_e'], w['gate_norm_g'], w['expand3'], w['tril3']]
    return pl.pallas_call(
        _ssd_prompt_kernel,
        grid=(t // tile,),
        in_specs=[_rows(tile, D_MODEL)] + [_resident(c.shape) for c in consts],
        out_specs=[_rows(tile, HALF), pl.BlockSpec((8, cd), lambda i: (0, 0)),
                   pl.BlockSpec((HALF, SSD_STATE), lambda i: (0, 0))],
        out_shape=[jax.ShapeDtypeStruct((t, HALF), BF16), jax.ShapeDtypeStruct((8, cd), F32),
                   jax.ShapeDtypeStruct((HALF, SSD_STATE), F32)],
        scratch_shapes=[pltpu.VMEM((tile, cd), F32), pltpu.VMEM((tile, HALF), F32),
                        pltpu.VMEM((tile, HEAD_LANES), F32), pltpu.VMEM((SSD_GROUPS, SSD_STATE, SSD_GW), F32)],
        compiler_params=_params(),
        name="ssd_prompt",
    )(x, *consts)


def _ssd_sample_pre_kernel(x_ref, g_ref, w_ref, cw_ref, cb_ref, p1_ref, p2_ref, p3_ref, dtb_ref, aloge_ref,
                           dskip_ref, seg_ref, raw_ref, z_ref, ysk_ref, eacs_ref, xw_ref, dec_ref, b_ref, c_ref):
    t = x_ref.shape[0]
    cd = SSD_CONV_DIM
    h = _rms_bf16(x_ref[...], g_ref[...])
    z_ref[...] = _dot(h, w_ref[:, 0:HALF])
    raw = _dot(h, w_ref[:, HALF:HALF + cd])
    raw_ref[...] = raw
    dt = _softplus(_dot(h, w_ref[:, HALF + cd:2 * HALF + cd]) + dtb_ref[...])

    def step_of(width):
        return lax.broadcasted_iota(jnp.int32, (t, width), 0) % DEC_SEQ

    def back(a, k):
        return jnp.where(step_of(a.shape[1]) >= k, pltpu.roll(a, k, 0), 0.0)

    def ahead(a, k):
        return jnp.where(step_of(a.shape[1]) + k < DEC_SEQ, pltpu.roll(a, t - k, 0), 0.0)

    conv = (raw * cw_ref[3:4, :] + (back(raw, 1) + p1_ref[...]) * cw_ref[2:3, :]
            + (back(raw, 2) + p2_ref[...]) * cw_ref[1:2, :] + (back(raw, 3) + p3_ref[...]) * cw_ref[0:1, :]
            + cb_ref[...])
    xbc = _silu(conv)
    xs = xbc[:, 0:HALF]
    bm = xbc[:, HALF:HALF + 2 * SSD_STATE]
    cm = xbc[:, HALF + 2 * SSD_STATE:]
    b_ref[...] = bm
    c_ref[...] = cm
    da = dt * (-jnp.exp(aloge_ref[...]))
    acs = da + back(da, 1) + back(da, 2) + back(da, 3)
    suffix = ahead(da, 1) + ahead(da, 2) + ahead(da, 3)
    xdt = xs * dt
    y = _dot((cm * bm).astype(BF16), seg_ref[...]) * xdt
    for k in range(1, DEC_SEQ):
        cbk = _dot((cm * pltpu.roll(bm, k, 0)).astype(BF16), seg_ref[...])
        term = cbk * jnp.exp(acs - pltpu.roll(acs, k, 0)) * pltpu.roll(xdt, k, 0)
        y = y + jnp.where(step_of(HALF) >= k, term, 0.0)
    ysk_ref[...] = y + dskip_ref[...] * xs
    eacs_ref[...] = jnp.exp(acs)
    xw_ref[...] = jnp.exp(suffix) * xdt
    dec_ref[...] = jnp.exp(acs + suffix)


def _ssd_sample_pre(x, p_rows, w):
    t = x.shape[0]
    cd = SSD_CONV_DIM
    consts_a = [w['g_pre'], w['w_ssd_s'], w['conv_w'], w['conv_b']]
    consts_b = [w['dt_bias_e'], w['a_log_e'], w['d_skip_e'], w['seg_expand']]
    args = [x] + consts_a + list(p_rows) + consts_b
    wide = jax.ShapeDtypeStruct((t, HALF), F32)
    narrow = jax.ShapeDtypeStruct((t, 2 * SSD_STATE), F32)
    out_shape = [jax.ShapeDtypeStruct((t, cd), F32), wide, wide, wide, wide, wide, narrow, narrow]
    return pl.pallas_call(
        _ssd_sample_pre_kernel,
        grid=(1,),
        in_specs=[_resident(a.shape) for a in args],
        out_specs=[_resident(s.shape) for s in out_shape],
        out_shape=out_shape,
        compiler_params=_params(),
        name="ssd_sample_pre",
    )(*args)


SSD_S_BATCH = 8


def _ssd_sample_state_kernel(st_ref, c_ref, b_ref, xw_ref, dec_ref, eacs_ref, ysk_ref, z_ref, gn_ref,
                             yd_ref, nst_ref):
    row_n = lax.broadcasted_iota(jnp.int32, (8, SSD_STATE), 0)
    row_w = lax.broadcasted_iota(jnp.int32, (8, SSD_GW), 0)
    row_f = lax.broadcasted_iota(jnp.int32, (8, HALF), 0)
    ones_rows = jnp.where((row_n >= 4) & (row_n < 7), 1.0, 0.0).astype(BF16)

    def pair(p, carry):
        r0 = pl.multiple_of(p * 8, 8)
        rows = pl.ds(r0, 8)
        c8 = c_ref[rows, :].astype(BF16)
        b8 = b_ref[rows, :]
        xw8 = xw_ref[rows, :]
        dec8 = dec_ref[rows, :]
        yoff = []
        for sub in range(2):
            b = 2 * p + sub
            xw_own = xw8 if sub == 0 else pltpu.roll(xw8, 4, 0)
            b_own = b8 if sub == 0 else pltpu.roll(b8, 4, 0)
            hi, mid, lo = (term.astype(F32) for term in _split3(dec8[4 * sub:4 * sub + 1, :]))
            parts = []
            for g in range(SSD_GROUPS):
                lanes = slice(g * SSD_GW, (g + 1) * SSD_GW)
                h0 = st_ref[b, g]
                parts.append(_dot_nt(c8[:, g * SSD_STATE:(g + 1) * SSD_STATE], h0.astype(BF16)))
                lhs = jnp.where(row_w < 4, xw_own[:, lanes],
                                jnp.where(row_w == 4, hi[:, lanes],
                                          jnp.where(row_w == 5, mid[:, lanes],
                                                    jnp.where(row_w == 6, lo[:, lanes], 0.0)))).astype(BF16)
                rhs_b = jnp.where(row_n < 4, b_own[:, g * SSD_STATE:(g + 1) * SSD_STATE], 0.0).astype(BF16)
                decay = _dot_tn(lhs, ones_rows)
                nst_ref[b, g] = h0 * decay + _dot_tn(lhs, rhs_b)
            yoff.append(jnp.concatenate(parts, axis=1))
        yoff8 = jnp.where(row_f < 4, yoff[0], yoff[1])
        y = ysk_ref[rows, :] + yoff8 * eacs_ref[rows, :]
        yd_ref[rows, :] = _group_norm_gate(y, z_ref[rows, :], gn_ref[...])
        return carry

    lax.fori_loop(0, SSD_S_BATCH // 2, pair, 0)


def _ssd_sample_state(state, cm, bm, xw, dec, eacs, ysk, z, gn):
    n_seq = state.shape[0]
    bb = SSD_S_BATCH
    r = bb * DEC_SEQ
    st_spec = pl.BlockSpec((bb, SSD_GROUPS, SSD_GW, SSD_STATE), lambda i: (i, 0, 0, 0))
    return pl.pallas_call(
        _ssd_sample_state_kernel,
        grid=(n_seq // bb,),
        in_specs=[st_spec, _rows(r, 2 * SSD_STATE), _rows(r, 2 * SSD_STATE)] + [_rows(r, HALF)] * 5
                 + [_resident((1, HALF))],
        out_specs=[_rows(r, HALF), st_spec],
        out_shape=[jax.ShapeDtypeStruct((n_seq * DEC_SEQ, HALF), BF16), jax.ShapeDtypeStruct(state.shape, F32)],
        compiler_params=_params(),
        name="ssd_sample_state",
    )(state, cm, bm, xw, dec, eacs, ysk, z, gn)


def _prep_layer1(g_pre, w_in, ln_g, ln_b, w_s, b_s, conv_w, conv_b, dt_bias, a_log, d_skip, gate_norm_g,
                 w_out, g_post):
    cd = SSD_CONV_DIM
    gw = HALF // CMLP_GROUPS
    w_c = w_in[:, :3 * HALF].astype(BF16)
    w_z = w_in[:, 3 * HALF:4 * HALF]
    w_xbc = w_in[:, 4 * HALF:4 * HALF + cd]
    w_dt = w_in[:, 4 * HALF + cd:]
    pad = jnp.zeros((D_MODEL, HEAD_LANES - SSD_HEADS), F32)
    w_ssd = jnp.concatenate([w_z, w_xbc, w_dt, pad], axis=1).astype(BF16)
    w_ssd_s = jnp.concatenate([w_z, w_xbc, jnp.repeat(w_dt, HEAD_DIM, axis=1)], axis=1).astype(BF16)

    def lanes16(v):
        return jnp.pad(v.astype(F32), (0, HEAD_LANES - SSD_HEADS)).reshape(1, HEAD_LANES)

    def per_channel(v):
        return jnp.repeat(v.astype(F32), HEAD_DIM).reshape(1, HALF)

    head_of = jnp.arange(HALF) // HEAD_DIM
    expand = (jnp.arange(HEAD_LANES)[:, None] == head_of[None, :]).astype(BF16)
    tril = jnp.tril(jnp.ones((CHUNK, CHUNK), BF16))
    grp_rows = jnp.arange(2 * SSD_STATE) // SSD_STATE
    seg_expand = (grp_rows[:, None] == (head_of // (SSD_HEADS // SSD_GROUPS))[None, :]).astype(BF16)

    w4 = jnp.tril(w_s[:, :DEC_SEQ, :DEC_SEQ])
    steps = jnp.arange(DEC_SEQ)
    coef = []
    for k in range(DEC_SEQ):
        src = steps - k
        ck = jnp.where((src >= 0)[None, :], w4[:, steps, jnp.maximum(src, 0)], 0.0)
        ck = jnp.repeat(ck.T, gw, axis=1)
        coef.append(jnp.concatenate([ck, ck], axis=0))
    bias4 = jnp.repeat(b_s[:, :DEC_SEQ].T, gw, axis=1)
    return dict(
        g_pre=g_pre.reshape(1, D_MODEL), w_c=w_c, ln_g=ln_g.reshape(1, HALF), ln_b=ln_b.reshape(1, HALF),
        ws_tril=jnp.tril(w_s).astype(BF16),
        bs_rows=jnp.broadcast_to(b_s.astype(F32)[:, :, None], (CMLP_GROUPS, CHUNK, gw)),
        coef=jnp.stack(coef).astype(F32), bias4=jnp.concatenate([bias4, bias4], axis=0).astype(F32),
        w_ssd=w_ssd, w_ssd_s=w_ssd_s, conv_w=conv_w, conv_b=conv_b.reshape(1, cd),
        dt_bias16=lanes16(dt_bias), a_log16=lanes16(a_log), dt_bias_e=per_channel(dt_bias),
        a_log_e=per_channel(a_log), d_skip_e=per_channel(d_skip), gate_norm_g=gate_norm_g.reshape(1, HALF),
        expand3=jnp.concatenate([expand] * 3, axis=0), tril3=jnp.concatenate([tril] * 3, axis=1),
        seg_expand=seg_expand, w_out=w_out.astype(BF16), g_post=g_post.reshape(1, D_MODEL))


def _layer1_prompt(x, w):
    yc = _cmlp_prompt(x, w['g_pre'], w['w_c'], w['ln_g'], w['ln_b'], w['ws_tril'], w['bs_rows'])
    yd, tail, ssm = _ssd_prompt(x, w)
    y = _out_proj(yc, yd, x, w['w_out'], w['g_post'])
    return y, tail[5:8], ssm.reshape(SSD_HEADS, HEAD_DIM, SSD_STATE)


def _layer1_sample(x, conv_state, ssm_state, w):
    n_seq = x.shape[0]
    t = n_seq * DEC_SEQ
    cd = SSD_CONV_DIM
    rows = x.reshape(t, D_MODEL)
    yc, vn = _cmlp_sample(rows, w['g_pre'], w['w_c'], w['ln_g'], w['ln_b'], w['coef'], w['bias4'])
    zero = jnp.zeros((n_seq, 1, cd), F32)
    s0, s1, s2 = conv_state[:, 0:1], conv_state[:, 1:2], conv_state[:, 2:3]
    p_rows = [jnp.concatenate(p, axis=1).reshape(t, cd)
              for p in ([s2, zero, zero, zero], [s1, s2, zero, zero], [s0, s1, s2, zero])]
    raw, z, ysk, eacs, xw, dec, bm, cm = _ssd_sample_pre(rows, p_rows, w)
    state = ssm_state.reshape(n_seq, SSD_GROUPS, SSD_GW, SSD_STATE)
    yd, new_state = _ssd_sample_state(state, cm, bm, xw, dec, eacs, ysk, z, w['gate_norm_g'])
    y = _out_proj(yc, yd, rows, w['w_out'], w['g_post'])
    return (y.reshape(n_seq, DEC_SEQ, D_MODEL), vn.reshape(n_seq, DEC_SEQ, HALF),
            raw.reshape(n_seq, DEC_SEQ, cd)[:, 1:], new_state.reshape(ssm_state.shape))


def kernel(x_prompt, x_sample, state_conv_a, cache_win_k, cache_win_v, state_conv_d, state_ssm, rel_bias,
           l0_g_pre, l0_w_in, l0_conv_w, l0_sinks, l0_w_out, l0_g_post,
           l1_g_pre, l1_w_in, l1_ln_g, l1_ln_b, l1_w_s, l1_b_s, l1_conv_w, l1_conv_b, l1_dt_bias, l1_a_log,
           l1_d_skip, l1_gate_norm_g, l1_w_out, l1_g_post):
    w0 = _prep_layer0(l0_g_pre, l0_w_in, l0_conv_w, rel_bias, l0_sinks, l0_w_out, l0_g_post)
    w1 = _prep_layer1(l1_g_pre, l1_w_in, l1_ln_g, l1_ln_b, l1_w_s, l1_b_s, l1_conv_w, l1_conv_b, l1_dt_bias,
                      l1_a_log, l1_d_skip, l1_gate_norm_g, l1_w_out, l1_g_post)
    yp, p_conv_a, p_win_k, p_win_v = _layer0_prompt(x_prompt[0], w0)
    ys, s_conv_a, s_win_k, s_win_v = _layer0_sample(x_sample, state_conv_a, cache_win_k, cache_win_v, w0)
    yp, p_conv_d, p_ssm = _layer1_prompt(yp, w1)
    ys, s_chunk_v, s_conv_d, s_ssm = _layer1_sample(ys, state_conv_d, state_ssm, w1)
    return (yp[None], ys, p_conv_a[None], s_conv_a, p_win_k[None], p_win_v[None], s_win_k, s_win_v, s_chunk_v,
            p_conv_d[None], s_conv_d, p_ssm[None], s_ssm)
```

```python
import functools
import math

import jax
import jax.numpy as jnp
from jax import lax
from jax.experimental import pallas as pl
from jax.experimental.pallas import tpu as pltpu

F32 = jnp.float32
BF16 = jnp.bfloat16

D_MODEL = 2048
HALF = 1024
HEAD_DIM = 64
N_HEADS = 16
N_KV = 4
GROUP = 4
WINDOW = 128
NUM_BUCKETS = 32
MAX_DISTANCE = 128
CMLP_GROUPS = 8
CHUNK = 128
SSD_HEADS = 16
SSD_STATE = 128
SSD_GROUPS = 2
SSD_CONV_DIM = HALF + 2 * SSD_GROUPS * SSD_STATE
DEC_SEQ = 4
NORM_EPS = 1e-6
MASK_VALUE = -1e30

ROW_TILE = 512
VMEM_LIMIT = 56 * 1024 * 1024


def _params(n_axes=1):
    return pltpu.CompilerParams(dimension_semantics=("arbitrary",) * n_axes,
                                vmem_limit_bytes=VMEM_LIMIT)


def _resident(shape):
    nd = len(shape)
    return pl.BlockSpec(shape, lambda *_: (0,) * nd, pipeline_mode=pl.Buffered(1))


def _rows(tile, width):
    return pl.BlockSpec((tile, width), lambda i: (i, 0))


def _cols(rows, width, block):
    return pl.BlockSpec((rows, width), lambda *_: (0, block), pipeline_mode=pl.Buffered(1))


def _rms_bf16(x, g):
    ms = jnp.mean(x * x, axis=-1, keepdims=True)
    return (x * lax.rsqrt(ms + NORM_EPS) * g).astype(BF16)


def _silu(x):
    return x * jax.nn.sigmoid(x)


def _dot(a, b):
    return jnp.dot(a, b, preferred_element_type=F32)


def _dot_nt(a, b):
    return lax.dot_general(a, b, (((1,), (1,)), ((), ())), preferred_element_type=F32)


def _dot_tn(a, b):
    return lax.dot_general(a, b, (((0,), (0,)), ((), ())), preferred_element_type=F32)


def _split3(x):
    hi = x.astype(BF16)
    r1 = x - hi.astype(F32)
    mid = r1.astype(BF16)
    lo = (r1 - mid.astype(F32)).astype(BF16)
    return hi, mid, lo


def _place_steps(t, placements):
    n_seq = placements[0][1].shape[0]
    row = lax.broadcasted_iota(jnp.int32, (t, n_seq), 0)
    seq = lax.broadcasted_iota(jnp.int32, (t, n_seq), 1)
    lhs, rhs = [], []
    for step, state in placements:
        sel = jnp.where(row == DEC_SEQ * seq + step, 1.0, 0.0).astype(BF16)
        lhs += [sel] * 3
        rhs += list(_split3(state))
    return _dot(jnp.concatenate(lhs, axis=1), jnp.concatenate(rhs, axis=0))


def _shift_rows(x, k, prev_rows=None):
    r = pltpu.roll(x, k, 0)
    if prev_rows is None:
        return r
    row = lax.broadcasted_iota(jnp.int32, x.shape, 0)
    n_prev = prev_rows.shape[0]
    for t in range(k):
        src = n_prev - k + t
        r = jnp.where(row == t, prev_rows[src:src + 1, :], r)
    return r


def _out_proj_kernel(ya_ref, yb_ref, x_ref, w_ref, g_ref, o_ref):
    y = _dot(ya_ref[...], w_ref[0:HALF, :]) + _dot(yb_ref[...], w_ref[HALF:2 * HALF, :])
    ms = jnp.mean(y * y, axis=-1, keepdims=True)
    o_ref[...] = x_ref[...] + y * lax.rsqrt(ms + NORM_EPS) * g_ref[...]


def _out_proj(ya, yb, x, w_bf, g):
    t = x.shape[0]
    tile = min(ROW_TILE, t)
    return pl.pallas_call(
        _out_proj_kernel,
        grid=(t // tile,),
        in_specs=[_rows(tile, HALF), _rows(tile, HALF), _rows(tile, D_MODEL),
                  _resident((2 * HALF, D_MODEL)), _resident((1, D_MODEL))],
        out_specs=_rows(tile, D_MODEL),
        out_shape=jax.ShapeDtypeStruct((t, D_MODEL), F32),
        compiler_params=_params(),
        name="out_proj",
    )(ya, yb, x, w_bf, g)


CONV_A_CHUNK = 256


def _conv_a_kernel(*refs, sample):
    if sample:
        x_ref, g_ref, w_ref, cw_ref, st_ref, ya_ref, s_ref = refs
    else:
        x_ref, g_ref, w_ref, cw_ref, ya_ref, s_ref = refs
    tile = x_ref.shape[0]
    cc = CONV_A_CHUNK
    h = _rms_bf16(x_ref[...], g_ref[...])
    if not sample:
        @pl.when(pl.program_id(0) == 0)
        def _():
            s_ref[...] = jnp.zeros_like(s_ref)
    for c in range(HALF // cc):
        lanes = slice(c * cc, (c + 1) * cc)
        a_b, a_c, a_h, a_g = (_dot(h, w_ref[:, j * HALF + c * cc:j * HALF + (c + 1) * cc]) for j in range(4))
        s = a_c * a_h
        if sample:
            t_in = lax.broadcasted_iota(jnp.int32, s.shape, 0) % DEC_SEQ
            old0 = st_ref[:, c * cc:(c + 1) * cc]
            old1 = st_ref[:, HALF + c * cc:HALF + (c + 1) * cc]
            p1 = jnp.where(t_in >= 1, _shift_rows(s, 1), 0.0) + _place_steps(tile, [(0, old1)])
            p2 = jnp.where(t_in >= 2, _shift_rows(s, 2), 0.0) + _place_steps(tile, [(0, old0), (1, old1)])
            s_ref[:, lanes] = s
        else:
            prev = s_ref[:, lanes]
            p1 = _shift_rows(s, 1, prev)
            p2 = _shift_rows(s, 2, prev)
            s_ref[:, lanes] = s[tile - 8:tile, :]
        conv = p2 * cw_ref[0:1, lanes] + p1 * cw_ref[1:2, lanes] + s * cw_ref[2:3, lanes]
        ya_ref[:, lanes] = (a_b * conv * _silu(a_g)).astype(BF16)


def _conv_a(x, g_pre, w0, conv_w, state=None):
    t = x.shape[0]
    sample = state is not None
    tile = t if sample else min(ROW_TILE, t)
    in_specs = [_rows(tile, D_MODEL), _resident((1, D_MODEL)), _cols(D_MODEL, 4 * HALF, 0), _resident((3, HALF))]
    args = [x, g_pre, w0, conv_w]
    if sample:
        in_specs.append(_resident(state.shape))
        args.append(state)
        s_spec, s_shape = _rows(tile, HALF), (t, HALF)
    else:
        s_spec, s_shape = pl.BlockSpec((8, HALF), lambda i: (0, 0)), (8, HALF)
    return pl.pallas_call(
        functools.partial(_conv_a_kernel, sample=sample),
        grid=(t // tile,),
        in_specs=in_specs,
        out_specs=[_rows(tile, HALF), s_spec],
        out_shape=[jax.ShapeDtypeStruct((t, HALF), BF16), jax.ShapeDtypeStruct(s_shape, F32)],
        compiler_params=_params(),
        name="conv_a_sample" if sample else "conv_a_prompt",
    )(*args)


def _rel_bucket(dist):
    max_exact = NUM_BUCKETS // 2
    d = jnp.maximum(dist, 0)
    ratio = jnp.maximum(d, max_exact).astype(F32) / max_exact
    large = max_exact + (jnp.log(ratio) / math.log(MAX_DISTANCE / max_exact)
                         * (NUM_BUCKETS - max_exact)).astype(jnp.int32)
    return jnp.where(d < max_exact, d, jnp.minimum(large, NUM_BUCKETS - 1))


def _attn_softmax_pv(s, sink, v_bf):
    m = jnp.maximum(jnp.max(s, axis=-1, keepdims=True), sink)
    p = jnp.exp(s - m)
    den = jnp.sum(p, axis=-1, keepdims=True) + jnp.exp(sink - m)
    return _dot(p.astype(BF16), v_bf) / den


def _attn_prompt_kernel(x_ref, g_ref, wq_ref, wg_ref, wkv_ref, bias_ref, sink_ref, yb_ref, kwin_ref, vwin_ref,
                        q_scr, gate_scr, k_scr, v_scr):
    tile = x_ref.shape[0]
    i = pl.program_id(0)
    kv_w = N_KV * HEAD_DIM
    h = _rms_bf16(x_ref[...], g_ref[...])

    @pl.when(i == 0)
    def _():
        k_scr[0:WINDOW, :] = jnp.zeros((WINDOW, kv_w), BF16)
        v_scr[0:WINDOW, :] = jnp.zeros((WINDOW, kv_w), BF16)

    q_scr[...] = _dot(h, wq_ref[...]).astype(BF16)
    k = _dot(h, wkv_ref[:, 0:kv_w])
    v = _dot(h, wkv_ref[:, kv_w:2 * kv_w])
    gate_scr[...] = _dot(h, wg_ref[...])
    k_scr[WINDOW:WINDOW + tile, :] = k.astype(BF16)
    v_scr[WINDOW:WINDOW + tile, :] = v.astype(BF16)
    kwin_ref[...] = k[tile - WINDOW:tile, :]
    vwin_ref[...] = v[tile - WINDOW:tile, :]

    def block(n, carry):
        r0 = pl.multiple_of(n * WINDOW, WINDOW)
        first = jnp.where(jnp.logical_and(i == 0, n == 0), 0, 1)
        for hk in range(N_KV):
            kk = k_scr[pl.ds(r0, 2 * WINDOW), hk * HEAD_DIM:(hk + 1) * HEAD_DIM]
            vv = v_scr[pl.ds(r0, 2 * WINDOW), hk * HEAD_DIM:(hk + 1) * HEAD_DIM]
            for g in range(GROUP):
                head = hk * GROUP + g
                lanes = slice(g * kv_w + hk * HEAD_DIM, g * kv_w + (hk + 1) * HEAD_DIM)
                s = _dot_nt(q_scr[pl.ds(r0, WINDOW), lanes], kk) + bias_ref[first, head]
                o = _attn_softmax_pv(s, sink_ref[head], vv)
                yb_ref[pl.ds(r0, WINDOW), lanes] = (o * _silu(gate_scr[pl.ds(r0, WINDOW), lanes])).astype(BF16)
        return carry

    lax.fori_loop(0, tile // WINDOW, block, 0)
    k_scr[0:WINDOW, :] = k_scr[tile:tile + WINDOW, :]
    v_scr[0:WINDOW, :] = v_scr[tile:tile + WINDOW, :]


def _prompt_bias(rel_bias):
    span = 3 * WINDOW - 1
    dist = jnp.arange(2 * WINDOW - 1, -WINDOW, -1)
    table = jnp.where(((dist >= 0) & (dist < WINDOW))[:, None], rel_bias.astype(F32)[_rel_bucket(dist)], MASK_VALUE)
    flat = jnp.tile(table.T, (1, WINDOW))
    normal = flat[:, WINDOW - 1:WINDOW - 1 + WINDOW * (span - 1)].reshape(N_HEADS, WINDOW, span - 1)[:, :, :2 * WINDOW]
    first = jnp.where(jnp.arange(2 * WINDOW)[None, None, :] >= WINDOW, normal, MASK_VALUE)
    return jnp.stack([first, normal])


def _attn_prompt(x, g_pre, w0, bias, sinks):
    t = x.shape[0]
    tile = min(ROW_TILE, t)
    kv_w = N_KV * HEAD_DIM
    win_spec = pl.BlockSpec((WINDOW, kv_w), lambda i: (0, 0))
    return pl.pallas_call(
        _attn_prompt_kernel,
        grid=(t // tile,),
        in_specs=[_rows(tile, D_MODEL), _resident((1, D_MODEL)),
                  _cols(D_MODEL, HALF, 4), _cols(D_MODEL, HALF, 5), _cols(D_MODEL, 2 * kv_w, 12),
                  _resident(bias.shape), pl.BlockSpec(memory_space=pltpu.SMEM)],
        out_specs=[_rows(tile, HALF), win_spec, win_spec],
        out_shape=[jax.ShapeDtypeStruct((t, HALF), BF16),
                   jax.ShapeDtypeStruct((WINDOW, kv_w), F32), jax.ShapeDtypeStruct((WINDOW, kv_w), F32)],
        scratch_shapes=[pltpu.VMEM((tile, HALF), BF16), pltpu.VMEM((tile, HALF), F32),
                        pltpu.VMEM((tile + WINDOW, kv_w), BF16), pltpu.VMEM((tile + WINDOW, kv_w), BF16)],
        compiler_params=_params(),
        name="attn_prompt",
    )(x, g_pre, w0, w0, w0, bias, sinks)


def _attn_proj_kernel(x_ref, g_ref, wq_ref, wg_ref, wkv_ref, o_ref):
    h = _rms_bf16(x_ref[...], g_ref[...])
    o_ref[:, 0:HALF] = _dot(h, wq_ref[...])
    o_ref[:, HALF:2 * HALF] = _dot(h, wg_ref[...])
    o_ref[:, 2 * HALF:] = _dot(h, wkv_ref[...])


def _attn_proj(x, g_pre, w0):
    t = x.shape[0]
    kv_w = N_KV * HEAD_DIM
    n = 2 * HALF + 2 * kv_w
    return pl.pallas_call(
        _attn_proj_kernel,
        grid=(1,),
        in_specs=[_resident((t, D_MODEL)), _resident((1, D_MODEL)),
                  _cols(D_MODEL, HALF, 4), _cols(D_MODEL, HALF, 5), _cols(D_MODEL, 2 * kv_w, 12)],
        out_specs=_resident((t, n)),
        out_shape=jax.ShapeDtypeStruct((t, n), F32),
        compiler_params=_params(),
        name="attn_proj_sample",
    )(x, g_pre, w0, w0, w0)


ATTN_S_BATCH = 16
KEYS_PAD = WINDOW + 8


def _attn_sample_kernel(proj_ref, ck_ref, cv_ref, bias_ref, sink_ref, yb_ref, nk_ref, nv_ref, k_scr, v_scr):
    kv_w = N_KV * HEAD_DIM
    row8 = lax.broadcasted_iota(jnp.int32, (8, kv_w), 0)
    lane_head = lax.broadcasted_iota(jnp.int32, (8, kv_w), 1) // HEAD_DIM
    lower = row8 < DEC_SEQ
    pick = [jnp.where(lane_head == 2 * hp + jnp.where(lower, 0, 1), 1.0, 0.0).astype(F32) for hp in range(2)]
    lower_w = lax.broadcasted_iota(jnp.int32, (8, HALF), 0) < DEC_SEQ
    k_scr[WINDOW + DEC_SEQ:KEYS_PAD, :] = jnp.zeros((KEYS_PAD - WINDOW - DEC_SEQ, kv_w), BF16)
    v_scr[WINDOW + DEC_SEQ:KEYS_PAD, :] = jnp.zeros((KEYS_PAD - WINDOW - DEC_SEQ, kv_w), BF16)

    def pair(p, carry):
        r0 = pl.multiple_of(p * 8, 8)
        rows = proj_ref[pl.ds(r0, 8), :]
        q8 = rows[:, 0:HALF]
        gate8 = rows[:, HALF:2 * HALF]
        k8 = rows[:, 2 * HALF:2 * HALF + kv_w]
        v8 = rows[:, 2 * HALF + kv_w:]
        out8 = []
        for sub in range(2):
            b = 2 * p + sub
            q_swap = pltpu.roll(q8, 4, 0)
            q_dup = jnp.where(lower_w, q8, q_swap) if sub == 0 else jnp.where(lower_w, q_swap, q8)
            k_new = k8[0:4] if sub == 0 else k8[4:8]
            v_new = v8[0:4] if sub == 0 else v8[4:8]
            kc = ck_ref[b]
            vc = cv_ref[b]
            nk_ref[b, 0:WINDOW - DEC_SEQ, :] = kc[DEC_SEQ:WINDOW, :]
            nv_ref[b, 0:WINDOW - DEC_SEQ, :] = vc[DEC_SEQ:WINDOW, :]
            nk_ref[b, WINDOW - DEC_SEQ:WINDOW, :] = k_new
            nv_ref[b, WINDOW - DEC_SEQ:WINDOW, :] = v_new
            k_scr[0:WINDOW, :] = kc.astype(BF16)
            v_scr[0:WINDOW, :] = vc.astype(BF16)
            k_scr[WINDOW:WINDOW + DEC_SEQ, :] = k_new.astype(BF16)
            v_scr[WINDOW:WINDOW + DEC_SEQ, :] = v_new.astype(BF16)
            q_bd = jnp.concatenate(
                [q_dup[:, g * kv_w:(g + 1) * kv_w] * pick[hp] for g in range(GROUP) for hp in range(2)], axis=0)
            s = _dot_nt(q_bd.astype(BF16), k_scr[...]) + bias_ref[...]
            o = _attn_softmax_pv(s, sink_ref[:, 0:1], v_scr[...])
            out_g = []
            for g in range(GROUP):
                acc = None
                for hp in range(2):
                    piece = o[(2 * g + hp) * 8:(2 * g + hp + 1) * 8, :] * pick[hp]
                    piece = piece + pltpu.roll(piece, 4, 0)
                    acc = piece if acc is None else acc + piece
                out_g.append(acc)
            out8.append(jnp.concatenate(out_g, axis=1))
        o8 = jnp.where(lower_w, out8[0], out8[1])
        yb_ref[pl.ds(r0, 8), :] = (o8 * _silu(gate8)).astype(BF16)
        return carry

    lax.fori_loop(0, ATTN_S_BATCH // 2, pair, 0)


def _sample_bias(rel_bias, sinks):
    t = jnp.arange(DEC_SEQ)[:, None]
    j = jnp.arange(KEYS_PAD)[None, :]
    dist = t + WINDOW - j
    valid = (dist >= 0) & (dist < WINDOW) & (j < WINDOW + DEC_SEQ)
    bias = jnp.where(valid[:, :, None], rel_bias.astype(F32)[_rel_bucket(dist)], MASK_VALUE)
    bias = bias.reshape(DEC_SEQ, KEYS_PAD, N_KV, GROUP).transpose(3, 2, 0, 1).reshape(N_HEADS * DEC_SEQ, KEYS_PAD)
    sink = jnp.broadcast_to(sinks.astype(F32).reshape(N_KV, GROUP).T[:, :, None], (GROUP, N_KV, DEC_SEQ))
    return bias, jnp.broadcast_to(sink.reshape(N_HEADS * DEC_SEQ, 1), (N_HEADS * DEC_SEQ, 128))


def _attn_sample(proj, cache_k, cache_v, bias, sink):
    n_seq = cache_k.shape[0]
    kv_w = N_KV * HEAD_DIM
    bb = ATTN_S_BATCH
    cache_spec = pl.BlockSpec((bb, WINDOW, kv_w), lambda i: (i, 0, 0))
    return pl.pallas_call(
        _attn_sample_kernel,
        grid=(n_seq // bb,),
        in_specs=[_rows(bb * DEC_SEQ, proj.shape[1]), cache_spec, cache_spec,
                  _resident(bias.shape), _resident(sink.shape)],
        out_specs=[_rows(bb * DEC_SEQ, HALF), cache_spec, cache_spec],
        out_shape=[jax.ShapeDtypeStruct((n_seq * DEC_SEQ, HALF), BF16),
                   jax.ShapeDtypeStruct(cache_k.shape, F32), jax.ShapeDtypeStruct(cache_v.shape, F32)],
        scratch_shapes=[pltpu.VMEM((KEYS_PAD, kv_w), BF16), pltpu.VMEM((KEYS_PAD, kv_w), BF16)],
        compiler_params=_params(),
        name="attn_sample",
    )(proj, cache_k, cache_v, bias, sink)


def _prep_layer0(g_pre, w_in, conv_w, rel_bias, sinks, w_out, g_post):
    kv_w = N_KV * HEAD_DIM

    def by_group(w):
        return w.reshape(D_MODEL, N_KV, GROUP, HEAD_DIM).transpose(0, 2, 1, 3).reshape(D_MODEL, HALF)

    w_q = by_group(w_in[:, 4 * HALF:5 * HALF] * (HEAD_DIM ** -0.5))
    w_gate = by_group(w_in[:, 5 * HALF + 2 * kv_w:])
    w0 = jnp.concatenate([w_in[:, :4 * HALF], w_q, w_gate, w_in[:, 5 * HALF:5 * HALF + 2 * kv_w]], axis=1).astype(BF16)
    w_out_b = w_out[HALF:].reshape(N_KV, GROUP, HEAD_DIM, D_MODEL).transpose(1, 0, 2, 3).reshape(HALF, D_MODEL)
    return dict(
        g_pre=g_pre.reshape(1, D_MODEL), w0=w0, conv_w=conv_w, rel_bias=rel_bias, sinks=sinks,
        w_out=jnp.concatenate([w_out[:HALF], w_out_b], axis=0).astype(BF16), g_post=g_post.reshape(1, D_MODEL))


def _layer0_prompt(x, w):
    ya, s_tail = _conv_a(x, w['g_pre'], w['w0'], w['conv_w'])
    yb, kwin, vwin = _attn_prompt(x, w['g_pre'], w['w0'], _prompt_bias(w['rel_bias']), w['sinks'])
    y = _out_proj(ya, yb, x, w['w_out'], w['g_post'])
    return (y, s_tail[6:8], kwin.reshape(WINDOW, N_KV, HEAD_DIM), vwin.reshape(WINDOW, N_KV, HEAD_DIM))


def _layer0_sample(x, conv_state, cache_k, cache_v, w):
    n_seq = x.shape[0]
    rows = x.reshape(n_seq * DEC_SEQ, D_MODEL)
    ya, s = _conv_a(rows, w['g_pre'], w['w0'], w['conv_w'], conv_state.reshape(n_seq, 2 * HALF))
    kv_w = N_KV * HEAD_DIM
    proj = _attn_proj(rows, w['g_pre'], w['w0'])
    bias, sink = _sample_bias(w['rel_bias'], w['sinks'])
    yb, new_k, new_v = _attn_sample(proj, cache_k.reshape(n_seq, WINDOW, kv_w), cache_v.reshape(n_seq, WINDOW, kv_w),
                                    bias, sink)
    y = _out_proj(ya, yb, rows, w['w_out'], w['g_post'])
    return (y.reshape(n_seq, DEC_SEQ, D_MODEL), s.reshape(n_seq, DEC_SEQ, HALF)[:, DEC_SEQ - 2:],
            new_k.reshape(cache_k.shape), new_v.reshape(cache_v.shape))


def _layer_norm(v, g, b):
    xc = v - jnp.mean(v, axis=-1, keepdims=True)
    return xc * lax.rsqrt(jnp.mean(xc * xc, axis=-1, keepdims=True) + NORM_EPS) * g + b


def _cmlp_prompt_kernel(x_ref, g_ref, w_ref, lng_ref, lnb_ref, ws_ref, bs_ref, yc_ref, vn_scr):
    tile = x_ref.shape[0]
    h = _rms_bf16(x_ref[...], g_ref[...])
    v = _dot(h, w_ref[:, HALF:2 * HALF])
    vn_scr[...] = _layer_norm(v, lng_ref[...], lnb_ref[...]).astype(BF16)
    gw = HALF // CMLP_GROUPS
    cols = 2 * gw
    for cb in range(HALF // cols):
        u = _dot(h, w_ref[:, cb * cols:(cb + 1) * cols])
        gate = _silu(_dot(h, w_ref[:, 2 * HALF + cb * cols:2 * HALF + (cb + 1) * cols]))
        for gi in range(2):
            grp = 2 * cb + gi
            lanes = slice(grp * gw, (grp + 1) * gw)
            for n in range(tile // CHUNK):
                rows = slice(n * CHUNK, (n + 1) * CHUNK)
                mixed = _dot(ws_ref[grp], vn_scr[rows, lanes]) + bs_ref[grp]
                yc_ref[rows, lanes] = (u[rows, gi * gw:(gi + 1) * gw] * mixed
                                       * gate[rows, gi * gw:(gi + 1) * gw]).astype(BF16)


def _cmlp_prompt(x, g_pre, w_c, ln_g, ln_b, ws_tril, bs_rows):
    t = x.shape[0]
    tile = min(ROW_TILE, t)
    return pl.pallas_call(
        _cmlp_prompt_kernel,
        grid=(t // tile,),
        in_specs=[_rows(tile, D_MODEL), _resident((1, D_MODEL)), _cols(D_MODEL, 3 * HALF, 0),
                  _resident((1, HALF)), _resident((1, HALF)), _resident(ws_tril.shape), _resident(bs_rows.shape)],
        out_specs=_rows(tile, HALF),
        out_shape=jax.ShapeDtypeStruct((t, HALF), BF16),
        scratch_shapes=[pltpu.VMEM((tile, HALF), BF16)],
        compiler_params=_params(),
        name="cmlp_prompt",
    )(x, g_pre, w_c, ln_g, ln_b, ws_tril, bs_rows)


def _cmlp_sample_kernel(x_ref, g_ref, w_ref, lng_ref, lnb_ref, coef_ref, bias_ref, yc_ref, vn_ref):
    t = x_ref.shape[0]
    h = _rms_bf16(x_ref[...], g_ref[...])
    u = _dot(h, w_ref[:, 0:HALF])
    vn = _layer_norm(_dot(h, w_ref[:, HALF:2 * HALF]), lng_ref[...], lnb_ref[...])
    gate = _silu(_dot(h, w_ref[:, 2 * HALF:3 * HALF]))
    vn_ref[...] = vn

    def tiled(a):
        return a.reshape(t // 8, 8, HALF)

    mixed = tiled(vn) * coef_ref[0][None] + bias_ref[...][None]
    for k in range(1, DEC_SEQ):
        mixed = mixed + tiled(pltpu.roll(vn, k, 0)) * coef_ref[k][None]
    yc_ref[...] = (u * mixed.reshape(t, HALF) * gate).astype(BF16)


def _cmlp_sample(x, g_pre, w_c, ln_g, ln_b, coef, bias):
    t = x.shape[0]
    return pl.pallas_call(
        _cmlp_sample_kernel,
        grid=(1,),
        in_specs=[_resident((t, D_MODEL)), _resident((1, D_MODEL)), _cols(D_MODEL, 3 * HALF, 0),
                  _resident((1, HALF)), _resident((1, HALF)), _resident(coef.shape), _resident(bias.shape)],
        out_specs=[_resident((t, HALF)), _resident((t, HALF))],
        out_shape=[jax.ShapeDtypeStruct((t, HALF), BF16), jax.ShapeDtypeStruct((t, HALF), F32)],
        compiler_params=_params(),
        name="cmlp_sample",
    )(x, g_pre, w_c, ln_g, ln_b, coef, bias)


HEAD_LANES = 128
SSD_GW = HALF // SSD_GROUPS


def _softplus(x):
    return jnp.maximum(x, 0.0) + jnp.log1p(jnp.exp(-jnp.abs(x)))


def _group_norm_gate(y, z, gn):
    gated = y * _silu(z)
    parts = []
    for g in range(SSD_GROUPS):
        part = gated[:, g * SSD_GW:(g + 1) * SSD_GW]
        parts.append(part * lax.rsqrt(jnp.mean(part * part, axis=-1, keepdims=True) + NORM_EPS))
    return (jnp.concatenate(parts, axis=1) * gn).astype(BF16)


def _ssd_prompt_kernel(x_ref, g_ref, wz_ref, wx_ref, cw_ref, cb_ref, dtb_ref, alog_ref, aloge_ref, dskip_ref,
                       gn_ref, e3_ref, tril3_ref, yd_ref, tail_ref, ssm_ref, xbc_scr, z_scr, dt_scr, ht_scr):
    tile = x_ref.shape[0]
    i = pl.program_id(0)
    cd = SSD_CONV_DIM
    h = _rms_bf16(x_ref[...], g_ref[...])

    @pl.when(i == 0)
    def _():
        tail_ref[...] = jnp.zeros_like(tail_ref)
        ht_scr[...] = jnp.zeros_like(ht_scr)

    z_scr[...] = _dot(h, wz_ref[...])
    raw = _dot(h, wx_ref[:, 0:cd])
    dt_scr[...] = _softplus(_dot(h, wx_ref[:, cd:cd + HEAD_LANES]) + dtb_ref[...])
    prev = tail_ref[...]
    conv = raw * cw_ref[3:4, :] + cb_ref[...]
    for k in range(1, 4):
        conv = conv + _shift_rows(raw, k, prev) * cw_ref[3 - k:4 - k, :]
    xbc_scr[...] = _silu(conv)
    tail_ref[...] = raw[tile - 8:tile, :]

    a16 = -jnp.exp(alog_ref[...])
    a_e = -jnp.exp(aloge_ref[...])
    causal = (lax.broadcasted_iota(jnp.int32, (CHUNK, CHUNK), 0)
              >= lax.broadcasted_iota(jnp.int32, (CHUNK, CHUNK), 1))

    def chunk(n, carry):
        r0 = pl.multiple_of(n * CHUNK, CHUNK)
        rows = pl.ds(r0, CHUNK)
        xs = xbc_scr[rows, 0:HALF]
        dt16 = dt_scr[rows, :]
        dt_e = _dot(jnp.concatenate(_split3(dt16), axis=1), e3_ref[...])
        da_e = dt_e * a_e
        acs_e = _dot(tril3_ref[...], jnp.concatenate(_split3(da_e), axis=0))
        acs16 = _dot(tril3_ref[...], jnp.concatenate(_split3(dt16 * a16), axis=0))
        acs_t = acs16.T
        dt_t = dt16.T
        last_e = acs_e[CHUNK - 1:CHUNK, :]
        xs_bf = xs.astype(BF16)
        xw = (jnp.exp(last_e - acs_e) * dt_e * xs).astype(BF16)
        dec_e = jnp.exp(last_e)
        y_parts = []
        yoff_parts = []
        for g in range(SSD_GROUPS):
            c_g = xbc_scr[rows, HALF + 2 * SSD_STATE + g * SSD_STATE:HALF + 2 * SSD_STATE + (g + 1) * SSD_STATE].astype(BF16)
            b_g = xbc_scr[rows, HALF + g * SSD_STATE:HALF + (g + 1) * SSD_STATE].astype(BF16)
            cb = _dot_nt(c_g, b_g)
            h_prev = ht_scr[g]
            yoff_parts.append(_dot(c_g, h_prev.astype(BF16)))
            for r in range(SSD_HEADS // SSD_GROUPS):
                hd = g * (SSD_HEADS // SSD_GROUPS) + r
                seg = acs16[:, hd:hd + 1] - acs_t[hd:hd + 1, :]
                wgt = cb * jnp.exp(jnp.where(causal, seg, -jnp.inf)) * dt_t[hd:hd + 1, :]
                y_parts.append(_dot(wgt.astype(BF16), xs_bf[:, hd * HEAD_DIM:(hd + 1) * HEAD_DIM]))
            lanes = slice(g * SSD_GW, (g + 1) * SSD_GW)
            ht_scr[g] = h_prev * dec_e[:, lanes] + _dot_tn(b_g, xw[:, lanes])
        y = (jnp.concatenate(y_parts, axis=1) + jnp.concatenate(yoff_parts, axis=1) * jnp.exp(acs_e)
             + dskip_ref[...] * xs)
        yd_ref[rows, :] = _group_norm_gate(y, z_scr[rows, :], gn_ref[...])
        return carry

    lax.fori_loop(0, tile // CHUNK, chunk, 0)

    @pl.when(i == pl.num_programs(0) - 1)
    def _():
        for g in range(SSD_GROUPS):
            ssm_ref[g * SSD_GW:(g + 1) * SSD_GW, :] = ht_scr[g].T


def _ssd_prompt(x, w):
    t = x.shape[0]
    tile = min(ROW_TILE, t)
    cd = SSD_CONV_DIM
    consts = [w['conv_w'], w['conv_b'], w['dt_bias16'], w['a_log16'], w['a_log_e'],
              w['d_skip_e'], w['gate_norm_g'], w['expand3'], w['tril3']]
    return pl.pallas_call(
        _ssd_prompt_kernel,
        grid=(t // tile,),
        in_specs=[_rows(tile, D_MODEL), _resident((1, D_MODEL)), _cols(D_MODEL, HALF, 3), _cols(D_MODEL, 2 * HALF, 2)]
                 + [_resident(c.shape) for c in consts],
        out_specs=[_rows(tile, HALF), pl.BlockSpec((8, cd), lambda i: (0, 0)),
                   pl.BlockSpec((HALF, SSD_STATE), lambda i: (0, 0))],
        out_shape=[jax.ShapeDtypeStruct((t, HALF), BF16), jax.ShapeDtypeStruct((8, cd), F32),
                   jax.ShapeDtypeStruct((HALF, SSD_STATE), F32)],
        scratch_shapes=[pltpu.VMEM((tile, cd), F32), pltpu.VMEM((tile, HALF), F32),
                        pltpu.VMEM((tile, HEAD_LANES), F32), pltpu.VMEM((SSD_GROUPS, SSD_STATE, SSD_GW), F32)],
        compiler_params=_params(),
        name="ssd_prompt",
    )(x, w['g_pre'], w['w1'], w['w1'], *consts)


def _ssd_sample_pre_kernel(x_ref, g_ref, wz_ref, wx_ref, cw_ref, cb_ref, st_ref, dtb_ref, aloge_ref,
                           dskip_ref, e3_ref, seg_ref, raw_ref, z_ref, ysk_ref, eacs_ref, xw_ref, dec_ref, b_ref, c_ref):
    t = x_ref.shape[0]
    cd = SSD_CONV_DIM
    h = _rms_bf16(x_ref[...], g_ref[...])
    z_ref[...] = _dot(h, wz_ref[...])
    raw = _dot(h, wx_ref[:, 0:cd])
    raw_ref[...] = raw
    dt16 = _softplus(_dot(h, wx_ref[:, cd:cd + HEAD_LANES]) + dtb_ref[...])
    dt = _dot(jnp.concatenate(_split3(dt16), axis=1), e3_ref[...])
    old = [st_ref[:, j * cd:(j + 1) * cd] for j in range(3)]
    p1 = _place_steps(t, [(0, old[2])])
    p2 = _place_steps(t, [(0, old[1]), (1, old[2])])
    p3 = _place_steps(t, [(0, old[0]), (1, old[1]), (2, old[2])])

    def step_of(width):
        return lax.broadcasted_iota(jnp.int32, (t, width), 0) % DEC_SEQ

    def back(a, k):
        return jnp.where(step_of(a.shape[1]) >= k, pltpu.roll(a, k, 0), 0.0)

    def ahead(a, k):
        return jnp.where(step_of(a.shape[1]) + k < DEC_SEQ, pltpu.roll(a, t - k, 0), 0.0)

    conv = (raw * cw_ref[3:4, :] + (back(raw, 1) + p1) * cw_ref[2:3, :]
            + (back(raw, 2) + p2) * cw_ref[1:2, :] + (back(raw, 3) + p3) * cw_ref[0:1, :]
            + cb_ref[...])
    xbc = _silu(conv)
    xs = xbc[:, 0:HALF]
    bm = xbc[:, HALF:HALF + 2 * SSD_STATE]
    cm = xbc[:, HALF + 2 * SSD_STATE:]
    b_ref[...] = bm
    c_ref[...] = cm
    da = dt * (-jnp.exp(aloge_ref[...]))
    acs = da + back(da, 1) + back(da, 2) + back(da, 3)
    suffix = ahead(da, 1) + ahead(da, 2) + ahead(da, 3)
    xdt = xs * dt
    y = _dot((cm * bm).astype(BF16), seg_ref[...]) * xdt
    for k in range(1, DEC_SEQ):
        cbk = _dot((cm * pltpu.roll(bm, k, 0)).astype(BF16), seg_ref[...])
        term = cbk * jnp.exp(acs - pltpu.roll(acs, k, 0)) * pltpu.roll(xdt, k, 0)
        y = y + jnp.where(step_of(HALF) >= k, term, 0.0)
    ysk_ref[...] = y + dskip_ref[...] * xs
    eacs_ref[...] = jnp.exp(acs)
    xw_ref[...] = jnp.exp(suffix) * xdt
    dec_ref[...] = jnp.exp(acs + suffix)


def _ssd_sample_pre(x, conv_state, w):
    t = x.shape[0]
    cd = SSD_CONV_DIM
    consts = [w['conv_w'], w['conv_b'], conv_state, w['dt_bias16'], w['a_log_e'], w['d_skip_e'], w['expand3'],
              w['seg_expand']]
    args = [x, w['g_pre'], w['w1'], w['w1']] + consts
    wide = jax.ShapeDtypeStruct((t, HALF), F32)
    narrow = jax.ShapeDtypeStruct((t, 2 * SSD_STATE), F32)
    out_shape = [jax.ShapeDtypeStruct((t, cd), F32), wide, wide, wide, wide, wide, narrow, narrow]
    return pl.pallas_call(
        _ssd_sample_pre_kernel,
        grid=(1,),
        in_specs=[_resident((t, D_MODEL)), _resident((1, D_MODEL)), _cols(D_MODEL, HALF, 3), _cols(D_MODEL, 2 * HALF, 2)]
                 + [_resident(c.shape) for c in consts],
        out_specs=[_resident(s.shape) for s in out_shape],
        out_shape=out_shape,
        compiler_params=_params(),
        name="ssd_sample_pre",
    )(*args)


SSD_S_BATCH = 8


def _ssd_sample_state_kernel(st_ref, c_ref, b_ref, xw_ref, dec_ref, eacs_ref, ysk_ref, z_ref, gn_ref,
                             yd_ref, nst_ref):
    row_n = lax.broadcasted_iota(jnp.int32, (8, SSD_STATE), 0)
    row_w = lax.broadcasted_iota(jnp.int32, (8, SSD_GW), 0)
    row_f = lax.broadcasted_iota(jnp.int32, (8, HALF), 0)
    ones_rows = jnp.where((row_n >= 4) & (row_n < 7), 1.0, 0.0).astype(BF16)
    hpg = SSD_HEADS // SSD_GROUPS

    def pair(p, carry):
        r0 = pl.multiple_of(p * 8, 8)
        rows = pl.ds(r0, 8)
        c8 = c_ref[rows, :].astype(BF16)
        b8 = b_ref[rows, :]
        xw8 = xw_ref[rows, :]
        dec8 = dec_ref[rows, :]
        yoff = []
        for sub in range(2):
            b = 2 * p + sub
            xw_own = xw8 if sub == 0 else pltpu.roll(xw8, 4, 0)
            b_own = b8 if sub == 0 else pltpu.roll(b8, 4, 0)
            hi, mid, lo = (term.astype(F32) for term in _split3(dec8[4 * sub:4 * sub + 1, :]))
            parts = []
            for g in range(SSD_GROUPS):
                lanes = slice(g * SSD_GW, (g + 1) * SSD_GW)
                heads = pl.ds(g * hpg, hpg)
                h0 = st_ref[b, heads].reshape(SSD_GW, SSD_STATE)
                parts.append(_dot_nt(c8[:, g * SSD_STATE:(g + 1) * SSD_STATE], h0.astype(BF16)))
                lhs = jnp.where(row_w < 4, xw_own[:, lanes],
                                jnp.where(row_w == 4, hi[:, lanes],
                                          jnp.where(row_w == 5, mid[:, lanes],
                                                    jnp.where(row_w == 6, lo[:, lanes], 0.0)))).astype(BF16)
                rhs_b = jnp.where(row_n < 4, b_own[:, g * SSD_STATE:(g + 1) * SSD_STATE], 0.0).astype(BF16)
                decay = _dot_tn(lhs, ones_rows)
                nst_ref[b, heads] = (h0 * decay + _dot_tn(lhs, rhs_b)).reshape(hpg, HEAD_DIM, SSD_STATE)
            yoff.append(jnp.concatenate(parts, axis=1))
        yoff8 = jnp.where(row_f < 4, yoff[0], yoff[1])
        y = ysk_ref[rows, :] + yoff8 * eacs_ref[rows, :]
        yd_ref[rows, :] = _group_norm_gate(y, z_ref[rows, :], gn_ref[...])
        return carry

    lax.fori_loop(0, SSD_S_BATCH // 2, pair, 0)


def _ssd_sample_state(state, cm, bm, xw, dec, eacs, ysk, z, gn):
    n_seq = state.shape[0]
    bb = SSD_S_BATCH
    r = bb * DEC_SEQ
    st_spec = pl.BlockSpec((bb, SSD_HEADS, HEAD_DIM, SSD_STATE), lambda i: (i, 0, 0, 0))
    return pl.pallas_call(
        _ssd_sample_state_kernel,
        grid=(n_seq // bb,),
        in_specs=[st_spec, _rows(r, 2 * SSD_STATE), _rows(r, 2 * SSD_STATE)] + [_rows(r, HALF)] * 5
                 + [_resident((1, HALF))],
        out_specs=[_rows(r, HALF), st_spec],
        out_shape=[jax.ShapeDtypeStruct((n_seq * DEC_SEQ, HALF), BF16), jax.ShapeDtypeStruct(state.shape, F32)],
        compiler_params=_params(),
        name="ssd_sample_state",
    )(state, cm, bm, xw, dec, eacs, ysk, z, gn)


def _prep_layer1(g_pre, w_in, ln_g, ln_b, w_s, b_s, conv_w, conv_b, dt_bias, a_log, d_skip, gate_norm_g,
                 w_out, g_post):
    cd = SSD_CONV_DIM
    gw = HALF // CMLP_GROUPS
    w1 = jnp.pad(w_in.astype(BF16), ((0, 0), (0, 6 * HALF - w_in.shape[1])))

    def lanes16(v):
        return jnp.pad(v.astype(F32), (0, HEAD_LANES - SSD_HEADS)).reshape(1, HEAD_LANES)

    def per_channel(v):
        return jnp.repeat(v.astype(F32), HEAD_DIM).reshape(1, HALF)

    head_of = jnp.arange(HALF) // HEAD_DIM
    expand = (jnp.arange(HEAD_LANES)[:, None] == head_of[None, :]).astype(BF16)
    tril = jnp.tril(jnp.ones((CHUNK, CHUNK), BF16))
    grp_rows = jnp.arange(2 * SSD_STATE) // SSD_STATE
    seg_expand = (grp_rows[:, None] == (head_of // (SSD_HEADS // SSD_GROUPS))[None, :]).astype(BF16)

    w4 = jnp.tril(w_s[:, :DEC_SEQ, :DEC_SEQ])
    steps = jnp.arange(DEC_SEQ)
    coef = []
    for k in range(DEC_SEQ):
        src = steps - k
        ck = jnp.where((src >= 0)[None, :], w4[:, steps, jnp.maximum(src, 0)], 0.0)
        ck = jnp.repeat(ck.T, gw, axis=1)
        coef.append(jnp.concatenate([ck, ck], axis=0))
    bias4 = jnp.repeat(b_s[:, :DEC_SEQ].T, gw, axis=1)
    return dict(
        g_pre=g_pre.reshape(1, D_MODEL), w1=w1, ln_g=ln_g.reshape(1, HALF), ln_b=ln_b.reshape(1, HALF),
        ws_tril=jnp.tril(w_s).astype(BF16),
        bs_rows=jnp.broadcast_to(b_s.astype(F32)[:, :, None], (CMLP_GROUPS, CHUNK, gw)),
        coef=jnp.stack(coef).astype(F32), bias4=jnp.concatenate([bias4, bias4], axis=0).astype(F32),
        conv_w=conv_w, conv_b=conv_b.reshape(1, cd), dt_bias16=lanes16(dt_bias), a_log16=lanes16(a_log),
        a_log_e=per_channel(a_log), d_skip_e=per_channel(d_skip), gate_norm_g=gate_norm_g.reshape(1, HALF),
        expand3=jnp.concatenate([expand] * 3, axis=0), tril3=jnp.concatenate([tril] * 3, axis=1),
        seg_expand=seg_expand, w_out=w_out.astype(BF16), g_post=g_post.reshape(1, D_MODEL))


def _layer1_prompt(x, w):
    yc = _cmlp_prompt(x, w['g_pre'], w['w1'], w['ln_g'], w['ln_b'], w['ws_tril'], w['bs_rows'])
    yd, tail, ssm = _ssd_prompt(x, w)
    y = _out_proj(yc, yd, x, w['w_out'], w['g_post'])
    return y, tail[5:8], ssm.reshape(SSD_HEADS, HEAD_DIM, SSD_STATE)


def _layer1_sample(x, conv_state, ssm_state, w):
    n_seq = x.shape[0]
    t = n_seq * DEC_SEQ
    cd = SSD_CONV_DIM
    rows = x.reshape(t, D_MODEL)
    yc, vn = _cmlp_sample(rows, w['g_pre'], w['w1'], w['ln_g'], w['ln_b'], w['coef'], w['bias4'])
    raw, z, ysk, eacs, xw, dec, bm, cm = _ssd_sample_pre(rows, conv_state.reshape(n_seq, 3 * cd), w)
    yd, new_state = _ssd_sample_state(ssm_state, cm, bm, xw, dec, eacs, ysk, z, w['gate_norm_g'])
    y = _out_proj(yc, yd, rows, w['w_out'], w['g_post'])
    return (y.reshape(n_seq, DEC_SEQ, D_MODEL), vn.reshape(n_seq, DEC_SEQ, HALF),
            raw.reshape(n_seq, DEC_SEQ, cd)[:, 1:], new_state)


def kernel(x_prompt, x_sample, state_conv_a, cache_win_k, cache_win_v, state_conv_d, state_ssm, rel_bias,
           l0_g_pre, l0_w_in, l0_conv_w, l0_sinks, l0_w_out, l0_g_post,
           l1_g_pre, l1_w_in, l1_ln_g, l1_ln_b, l1_w_s, l1_b_s, l1_conv_w, l1_conv_b, l1_dt_bias, l1_a_log,
           l1_d_skip, l1_gate_norm_g, l1_w_out, l1_g_post):
    w0 = _prep_layer0(l0_g_pre, l0_w_in, l0_conv_w, rel_bias, l0_sinks, l0_w_out, l0_g_post)
    w1 = _prep_layer1(l1_g_pre, l1_w_in, l1_ln_g, l1_ln_b, l1_w_s, l1_b_s, l1_conv_w, l1_conv_b, l1_dt_bias,
                      l1_a_log, l1_d_skip, l1_gate_norm_g, l1_w_out, l1_g_post)
    yp, p_conv_a, p_win_k, p_win_v = _layer0_prompt(x_prompt[0], w0)
    ys, s_conv_a, s_win_k, s_win_v = _layer0_sample(x_sample, state_conv_a, cache_win_k, cache_win_v, w0)
    yp, p_conv_d, p_ssm = _layer1_prompt(yp, w1)
    ys, s_chunk_v, s_conv_d, s_ssm = _layer1_sample(ys, state_conv_d, state_ssm, w1)
    return (yp[None], ys, p_conv_a[None], s_conv_a, p_win_k[None], p_win_v[None], s_win_k, s_win_v, s_chunk_v,
            p_conv_d[None], s_conv_d, p_ssm[None], s_ssm)
```

```python
import functools
import math

import jax
import jax.numpy as jnp
from jax import lax
from jax.experimental import pallas as pl
from jax.experimental.pallas import tpu as pltpu

F32 = jnp.float32
BF16 = jnp.bfloat16

D_MODEL = 2048
HALF = 1024
HEAD_DIM = 64
N_HEADS = 16
N_KV = 4
GROUP = 4
WINDOW = 128
NUM_BUCKETS = 32
MAX_DISTANCE = 128
CMLP_GROUPS = 8
CHUNK = 128
SSD_HEADS = 16
SSD_STATE = 128
SSD_GROUPS = 2
SSD_CONV_DIM = HALF + 2 * SSD_GROUPS * SSD_STATE
DEC_SEQ = 4
NORM_EPS = 1e-6
MASK_VALUE = -1e30

ROW_TILE = 512
VMEM_LIMIT = 56 * 1024 * 1024


def _params(n_axes=1):
    return pltpu.CompilerParams(dimension_semantics=("arbitrary",) * n_axes,
                                vmem_limit_bytes=VMEM_LIMIT)


def _resident(shape):
    nd = len(shape)
    return pl.BlockSpec(shape, lambda *_: (0,) * nd, pipeline_mode=pl.Buffered(1))


def _rows(tile, width):
    return pl.BlockSpec((tile, width), lambda i: (i, 0))


def _cols(rows, width, block):
    return pl.BlockSpec((rows, width), lambda *_: (0, block), pipeline_mode=pl.Buffered(1))


def _rms_bf16(x, g):
    ms = jnp.mean(x * x, axis=-1, keepdims=True)
    return (x * lax.rsqrt(ms + NORM_EPS) * g).astype(BF16)


def _silu(x):
    return x * jax.nn.sigmoid(x)


def _dot(a, b):
    return jnp.dot(a, b, preferred_element_type=F32)


def _dot_nt(a, b):
    return lax.dot_general(a, b, (((1,), (1,)), ((), ())), preferred_element_type=F32)


def _dot_tn(a, b):
    return lax.dot_general(a, b, (((0,), (0,)), ((), ())), preferred_element_type=F32)


def _split3(x):
    hi = x.astype(BF16)
    r1 = x - hi.astype(F32)
    mid = r1.astype(BF16)
    lo = (r1 - mid.astype(F32)).astype(BF16)
    return hi, mid, lo


def _place_steps(t, placements):
    n_seq = placements[0][1].shape[0]
    row = lax.broadcasted_iota(jnp.int32, (t, n_seq), 0)
    seq = lax.broadcasted_iota(jnp.int32, (t, n_seq), 1)
    lhs, rhs = [], []
    for step, state in placements:
        sel = jnp.where(row == DEC_SEQ * seq + step, 1.0, 0.0).astype(BF16)
        lhs += [sel] * 3
        rhs += list(_split3(state))
    return _dot(jnp.concatenate(lhs, axis=1), jnp.concatenate(rhs, axis=0))


def _shift_rows(x, k, prev_rows=None):
    r = pltpu.roll(x, k, 0)
    if prev_rows is None:
        return r
    row = lax.broadcasted_iota(jnp.int32, x.shape, 0)
    n_prev = prev_rows.shape[0]
    for t in range(k):
        src = n_prev - k + t
        r = jnp.where(row == t, prev_rows[src:src + 1, :], r)
    return r


def _out_proj_kernel(ya_ref, yb_ref, x_ref, w_ref, g_ref, o_ref):
    y = _dot(ya_ref[...], w_ref[0:HALF, :]) + _dot(yb_ref[...], w_ref[HALF:2 * HALF, :])
    ms = jnp.mean(y * y, axis=-1, keepdims=True)
    o_ref[...] = x_ref[...] + y * lax.rsqrt(ms + NORM_EPS) * g_ref[...]


def _out_proj(ya, yb, x, w_bf, g):
    t = x.shape[0]
    tile = min(ROW_TILE, t)
    return pl.pallas_call(
        _out_proj_kernel,
        grid=(t // tile,),
        in_specs=[_rows(tile, HALF), _rows(tile, HALF), _rows(tile, D_MODEL),
                  _resident((2 * HALF, D_MODEL)), _resident((1, D_MODEL))],
        out_specs=_rows(tile, D_MODEL),
        out_shape=jax.ShapeDtypeStruct((t, D_MODEL), F32),
        compiler_params=_params(),
        name="out_proj",
    )(ya, yb, x, w_bf, g)


CONV_A_CHUNK = 256


def _conv_a_kernel(*refs, sample):
    if sample:
        x_ref, g_ref, w_ref, cw_ref, st_ref, ya_ref, s_ref = refs
    else:
        x_ref, g_ref, w_ref, cw_ref, ya_ref, s_ref = refs
    tile = x_ref.shape[0]
    cc = CONV_A_CHUNK
    h = _rms_bf16(x_ref[...], g_ref[...])
    if not sample:
        @pl.when(pl.program_id(0) == 0)
        def _():
            s_ref[...] = jnp.zeros_like(s_ref)
    for c in range(HALF // cc):
        lanes = slice(c * cc, (c + 1) * cc)
        a_b, a_c, a_h, a_g = (_dot(h, w_ref[:, j * HALF + c * cc:j * HALF + (c + 1) * cc]) for j in range(4))
        s = a_c * a_h
        if sample:
            t_in = lax.broadcasted_iota(jnp.int32, s.shape, 0) % DEC_SEQ
            old0 = st_ref[:, c * cc:(c + 1) * cc]
            old1 = st_ref[:, HALF + c * cc:HALF + (c + 1) * cc]
            p1 = jnp.where(t_in >= 1, _shift_rows(s, 1), 0.0) + _place_steps(tile, [(0, old1)])
            p2 = jnp.where(t_in >= 2, _shift_rows(s, 2), 0.0) + _place_steps(tile, [(0, old0), (1, old1)])
            s_ref[:, lanes] = s
        else:
            prev = s_ref[:, lanes]
            p1 = _shift_rows(s, 1, prev)
            p2 = _shift_rows(s, 2, prev)
            s_ref[:, lanes] = s[tile - 8:tile, :]
        conv = p2 * cw_ref[0:1, lanes] + p1 * cw_ref[1:2, lanes] + s * cw_ref[2:3, lanes]
        ya_ref[:, lanes] = (a_b * conv * _silu(a_g)).astype(BF16)


def _conv_a(x, g_pre, w0, conv_w, state=None):
    t = x.shape[0]
    sample = state is not None
    tile = t if sample else min(ROW_TILE, t)
    in_specs = [_rows(tile, D_MODEL), _resident((1, D_MODEL)), _cols(D_MODEL, 4 * HALF, 0), _resident((3, HALF))]
    args = [x, g_pre, w0, conv_w]
    if sample:
        in_specs.append(_resident(state.shape))
        args.append(state)
        s_spec, s_shape = _rows(tile, HALF), (t, HALF)
    else:
        s_spec, s_shape = pl.BlockSpec((8, HALF), lambda i: (0, 0)), (8, HALF)
    return pl.pallas_call(
        functools.partial(_conv_a_kernel, sample=sample),
        grid=(t // tile,),
        in_specs=in_specs,
        out_specs=[_rows(tile, HALF), s_spec],
        out_shape=[jax.ShapeDtypeStruct((t, HALF), BF16), jax.ShapeDtypeStruct(s_shape, F32)],
        compiler_params=_params(),
        name="conv_a_sample" if sample else "conv_a_prompt",
    )(*args)


def _rel_bucket(dist):
    max_exact = NUM_BUCKETS // 2
    d = jnp.maximum(dist, 0)
    ratio = jnp.maximum(d, max_exact).astype(F32) / max_exact
    large = max_exact + (jnp.log(ratio) / math.log(MAX_DISTANCE / max_exact)
                         * (NUM_BUCKETS - max_exact)).astype(jnp.int32)
    return jnp.where(d < max_exact, d, jnp.minimum(large, NUM_BUCKETS - 1))


def _attn_softmax_pv(s, sink, v_bf, v_transposed=False):
    m = jnp.maximum(jnp.max(s, axis=-1, keepdims=True), sink)
    p = jnp.exp(s - m)
    den = jnp.sum(p, axis=-1, keepdims=True) + jnp.exp(sink - m)
    pv = _dot_nt(p.astype(BF16), v_bf) if v_transposed else _dot(p.astype(BF16), v_bf)
    return pv / den


def _attn_prompt_kernel(x_ref, g_ref, wq_ref, wkv_ref, wg0_ref, wg1_ref, tab_ref, sink_ref, yb_ref, kwin_ref,
                        vwin_ref, q_scr, gate_scr, k_scr, v_scr, bias_scr):
    tile = x_ref.shape[0]
    i = pl.program_id(0)
    kv_w = N_KV * HEAD_DIM
    h = _rms_bf16(x_ref[...], g_ref[...])

    @pl.when(i == 0)
    def _():
        k_scr[0:WINDOW, :] = jnp.zeros((WINDOW, kv_w), BF16)
        v_scr[0:WINDOW, :] = jnp.zeros((WINDOW, kv_w), BF16)
        in_own = lax.broadcasted_iota(jnp.int32, (WINDOW, 2 * WINDOW), 1) >= WINDOW
        for head in range(N_HEADS):
            row = jnp.broadcast_to(tab_ref[head:head + 1, :], (WINDOW, BIAS_SPAN))
            band = pltpu.roll(row, 0, 1, stride=1, stride_axis=0)[:, 0:2 * WINDOW]
            bias_scr[1, head] = band
            bias_scr[0, head] = jnp.where(in_own, band, MASK_VALUE)

    q_scr[...] = (_dot(h, wq_ref[...]) * (HEAD_DIM ** -0.5)).astype(BF16)
    k = _dot(h, wkv_ref[:, 0:kv_w])
    v = _dot(h, wkv_ref[:, kv_w:2 * kv_w])
    gate_scr[:, 0:HALF // 2] = _dot(h, wg0_ref[...])
    gate_scr[:, HALF // 2:HALF] = _dot(h, wg1_ref[...])
    k_scr[WINDOW:WINDOW + tile, :] = k.astype(BF16)
    v_scr[WINDOW:WINDOW + tile, :] = v.astype(BF16)
    kwin_ref[...] = k[tile - WINDOW:tile, :]
    vwin_ref[...] = v[tile - WINDOW:tile, :]

    def block(n, carry):
        r0 = pl.multiple_of(n * WINDOW, WINDOW)
        first = jnp.where(jnp.logical_and(i == 0, n == 0), 0, 1)
        for hk in range(N_KV):
            kk = k_scr[pl.ds(r0, 2 * WINDOW), hk * HEAD_DIM:(hk + 1) * HEAD_DIM]
            vv = v_scr[pl.ds(r0, 2 * WINDOW), hk * HEAD_DIM:(hk + 1) * HEAD_DIM]
            for g in range(GROUP):
                head = hk * GROUP + g
                lanes = slice(head * HEAD_DIM, (head + 1) * HEAD_DIM)
                s = _dot_nt(q_scr[pl.ds(r0, WINDOW), lanes], kk) + bias_scr[first, head]
                o = _attn_softmax_pv(s, sink_ref[head], vv)
                yb_ref[pl.ds(r0, WINDOW), lanes] = (o * _silu(gate_scr[pl.ds(r0, WINDOW), lanes])).astype(BF16)
        return carry

    lax.fori_loop(0, tile // WINDOW, block, 0)
    k_scr[0:WINDOW, :] = k_scr[tile:tile + WINDOW, :]
    v_scr[0:WINDOW, :] = v_scr[tile:tile + WINDOW, :]


BIAS_SPAN = 3 * WINDOW


def _prompt_bias_table(rel_bias):
    dist = WINDOW - jnp.arange(BIAS_SPAN)
    table = jnp.where(((dist >= 0) & (dist < WINDOW))[:, None], rel_bias.astype(F32)[_rel_bucket(dist)], MASK_VALUE)
    return table.T


def _attn_prompt(x, g_pre, w0, table, sinks):
    t = x.shape[0]
    tile = min(ROW_TILE, t)
    kv_w = N_KV * HEAD_DIM
    win_spec = pl.BlockSpec((WINDOW, kv_w), lambda i: (0, 0))
    return pl.pallas_call(
        _attn_prompt_kernel,
        grid=(t // tile,),
        in_specs=[_rows(tile, D_MODEL), _resident((1, D_MODEL)),
                  _cols(D_MODEL, HALF, 4), _cols(D_MODEL, 2 * kv_w, 10),
                  _cols(D_MODEL, HALF // 2, 11), _cols(D_MODEL, HALF // 2, 12),
                  _resident(table.shape), pl.BlockSpec(memory_space=pltpu.SMEM)],
        out_specs=[_rows(tile, HALF), win_spec, win_spec],
        out_shape=[jax.ShapeDtypeStruct((t, HALF), BF16),
                   jax.ShapeDtypeStruct((WINDOW, kv_w), F32), jax.ShapeDtypeStruct((WINDOW, kv_w), F32)],
        scratch_shapes=[pltpu.VMEM((tile, HALF), BF16), pltpu.VMEM((tile, HALF), F32),
                        pltpu.VMEM((tile + WINDOW, kv_w), BF16), pltpu.VMEM((tile + WINDOW, kv_w), BF16),
                        pltpu.VMEM((2, N_HEADS, WINDOW, 2 * WINDOW), F32)],
        compiler_params=_params(),
        name="attn_prompt",
    )(x, g_pre, w0, w0, w0, w0, table, sinks)


def _attn_proj_kernel(x_ref, g_ref, wq_ref, wkv_ref, wg0_ref, wg1_ref, qg_ref, kt_ref, vt_ref, kv_scr):
    kv_w = N_KV * HEAD_DIM
    h = _rms_bf16(x_ref[...], g_ref[...])
    q = _dot(h, wq_ref[...]) * (HEAD_DIM ** -0.5)
    for hk in range(N_KV):
        for g in range(GROUP):
            src = (hk * GROUP + g) * HEAD_DIM
            dst = (g * N_KV + hk) * HEAD_DIM
            qg_ref[:, dst:dst + HEAD_DIM] = q[:, src:src + HEAD_DIM]
    qg_ref[:, HALF:HALF + HALF // 2] = _dot(h, wg0_ref[...])
    qg_ref[:, HALF + HALF // 2:2 * HALF] = _dot(h, wg1_ref[...])
    kv_scr[...] = _dot(h, wkv_ref[...])
    for j in range(kt_ref.shape[0]):
        kt_ref[j] = kv_scr[j * WINDOW:(j + 1) * WINDOW, 0:kv_w].T
        vt_ref[j] = kv_scr[j * WINDOW:(j + 1) * WINDOW, kv_w:2 * kv_w].T


def _attn_proj(x, g_pre, w0):
    t = x.shape[0]
    kv_w = N_KV * HEAD_DIM
    out_shape = [jax.ShapeDtypeStruct((t, 2 * HALF), F32), jax.ShapeDtypeStruct((t // WINDOW, kv_w, WINDOW), F32),
                 jax.ShapeDtypeStruct((t // WINDOW, kv_w, WINDOW), F32)]
    return pl.pallas_call(
        _attn_proj_kernel,
        grid=(1,),
        in_specs=[_resident((t, D_MODEL)), _resident((1, D_MODEL)),
                  _cols(D_MODEL, HALF, 4), _cols(D_MODEL, 2 * kv_w, 10),
                  _cols(D_MODEL, HALF // 2, 11), _cols(D_MODEL, HALF // 2, 12)],
        out_specs=[_resident(s.shape) for s in out_shape],
        out_shape=out_shape,
        scratch_shapes=[pltpu.VMEM((t, 2 * kv_w), F32)],
        compiler_params=_params(),
        name="attn_proj_sample",
    )(x, g_pre, w0, w0, w0, w0)


ATTN_S_BATCH = 16
KEYS_PAD = 2 * WINDOW


def _attn_sample_kernel(qg_ref, ktn_ref, vtn_ref, ck_ref, cv_ref, bias_ref, sink_ref, yb_ref, nk_ref, nv_ref):
    kv_w = N_KV * HEAD_DIM
    row8 = lax.broadcasted_iota(jnp.int32, (8, kv_w), 0)
    lane_head = lax.broadcasted_iota(jnp.int32, (8, kv_w), 1) // HEAD_DIM
    lower = row8 < DEC_SEQ
    pick = [jnp.where(lane_head == 2 * hp + jnp.where(lower, 0, 1), 1.0, 0.0).astype(F32) for hp in range(2)]
    lower_w = lax.broadcasted_iota(jnp.int32, (8, HALF), 0) < DEC_SEQ
    kept = lax.broadcasted_iota(jnp.int32, (kv_w, WINDOW), 1) < WINDOW - DEC_SEQ
    seq0 = pl.program_id(0) * ATTN_S_BATCH
    per_tile = WINDOW // DEC_SEQ

    def slide(old, new):
        return jnp.where(kept, pltpu.roll(old, WINDOW - DEC_SEQ, 1), pltpu.roll(new, WINDOW - DEC_SEQ, 1))

    def pair(p, carry):
        r0 = pl.multiple_of(p * 8, 8)
        rows = qg_ref[pl.ds(r0, 8), :]
        q8 = rows[:, 0:HALF]
        gate8 = rows[:, HALF:2 * HALF]
        out8 = []
        for sub in range(2):
            b = 2 * p + sub
            q_swap = pltpu.roll(q8, 4, 0)
            q_dup = jnp.where(lower_w, q8, q_swap) if sub == 0 else jnp.where(lower_w, q_swap, q8)
            tile = (seq0 + b) // per_tile
            shift = (WINDOW - DEC_SEQ * ((seq0 + b) % per_tile)) % WINDOW
            k_new = pltpu.roll(ktn_ref[tile], shift, 1)
            v_new = pltpu.roll(vtn_ref[tile], shift, 1)
            k_old = ck_ref[b].reshape(kv_w, WINDOW)
            v_old = cv_ref[b].reshape(kv_w, WINDOW)
            nk_ref[b] = slide(k_old, k_new).reshape(N_KV, HEAD_DIM, WINDOW)
            nv_ref[b] = slide(v_old, v_new).reshape(N_KV, HEAD_DIM, WINDOW)
            k_all = jnp.concatenate([k_old, k_new], axis=1).astype(BF16)
            v_all = jnp.concatenate([v_old, v_new], axis=1).astype(BF16)
            q_bd = jnp.concatenate(
                [q_dup[:, g * kv_w:(g + 1) * kv_w] * pick[hp] for g in range(GROUP) for hp in range(2)], axis=0)
            s = _dot(q_bd.astype(BF16), k_all) + bias_ref[...]
            o = _attn_softmax_pv(s, sink_ref[:, 0:1], v_all, v_transposed=True)
            out_g = []
            for g in range(GROUP):
                acc = None
                for hp in range(2):
                    piece = o[(2 * g + hp) * 8:(2 * g + hp + 1) * 8, :] * pick[hp]
                    piece = piece + pltpu.roll(piece, 4, 0)
                    acc = piece if acc is None else acc + piece
                out_g.append(acc)
            out8.append(jnp.concatenate(
                [out_g[g][:, hk * HEAD_DIM:(hk + 1) * HEAD_DIM] for hk in range(N_KV) for g in range(GROUP)], axis=1))
        o8 = jnp.where(lower_w, out8[0], out8[1])
        yb_ref[pl.ds(r0, 8), :] = (o8 * _silu(gate8)).astype(BF16)
        return carry

    lax.fori_loop(0, ATTN_S_BATCH // 2, pair, 0)


def _sample_bias(rel_bias, sinks):
    t = jnp.arange(DEC_SEQ)[:, None]
    j = jnp.arange(KEYS_PAD)[None, :]
    dist = t + WINDOW - j
    valid = (dist >= 0) & (dist < WINDOW) & (j < WINDOW + DEC_SEQ)
    bias = jnp.where(valid[:, :, None], rel_bias.astype(F32)[_rel_bucket(dist)], MASK_VALUE)
    bias = bias.reshape(DEC_SEQ, KEYS_PAD, N_KV, GROUP).transpose(3, 2, 0, 1).reshape(N_HEADS * DEC_SEQ, KEYS_PAD)
    sink = jnp.broadcast_to(sinks.astype(F32).reshape(N_KV, GROUP).T[:, :, None], (GROUP, N_KV, DEC_SEQ))
    return bias, jnp.broadcast_to(sink.reshape(N_HEADS * DEC_SEQ, 1), (N_HEADS * DEC_SEQ, 128))


def _attn_sample(qg, kt_new, vt_new, cache_kt, cache_vt, bias, sink):
    n_seq = cache_kt.shape[0]
    bb = ATTN_S_BATCH
    cache_spec = pl.BlockSpec((bb, N_KV, HEAD_DIM, WINDOW), lambda i: (i, 0, 0, 0))
    return pl.pallas_call(
        _attn_sample_kernel,
        grid=(n_seq // bb,),
        in_specs=[_rows(bb * DEC_SEQ, 2 * HALF), _resident(kt_new.shape), _resident(vt_new.shape),
                  cache_spec, cache_spec, _resident(bias.shape), _resident(sink.shape)],
        out_specs=[_rows(bb * DEC_SEQ, HALF), cache_spec, cache_spec],
        out_shape=[jax.ShapeDtypeStruct((n_seq * DEC_SEQ, HALF), BF16),
                   jax.ShapeDtypeStruct(cache_kt.shape, F32), jax.ShapeDtypeStruct(cache_vt.shape, F32)],
        compiler_params=_params(),
        name="attn_sample",
    )(qg, kt_new, vt_new, cache_kt, cache_vt, bias, sink)


def _prep_layer0(g_pre, w_in, conv_w, rel_bias, sinks, w_out, g_post):
    return dict(
        g_pre=g_pre.reshape(1, D_MODEL), w0=w_in.astype(BF16), conv_w=conv_w, rel_bias=rel_bias, sinks=sinks,
        w_out=w_out.astype(BF16), g_post=g_post.reshape(1, D_MODEL))


def _layer0_prompt(x, w):
    ya, s_tail = _conv_a(x, w['g_pre'], w['w0'], w['conv_w'])
    yb, kwin, vwin = _attn_prompt(x, w['g_pre'], w['w0'], _prompt_bias_table(w['rel_bias']), w['sinks'])
    y = _out_proj(ya, yb, x, w['w_out'], w['g_post'])
    return (y, s_tail[6:8], kwin.reshape(WINDOW, N_KV, HEAD_DIM), vwin.reshape(WINDOW, N_KV, HEAD_DIM))


def _layer0_sample(x, conv_state, cache_k, cache_v, w):
    n_seq = x.shape[0]
    rows = x.reshape(n_seq * DEC_SEQ, D_MODEL)
    ya, s = _conv_a(rows, w['g_pre'], w['w0'], w['conv_w'], conv_state.reshape(n_seq, 2 * HALF))
    qg, kt_new, vt_new = _attn_proj(rows, w['g_pre'], w['w0'])
    bias, sink = _sample_bias(w['rel_bias'], w['sinks'])
    yb, new_kt, new_vt = _attn_sample(qg, kt_new, vt_new, cache_k.transpose(0, 2, 3, 1), cache_v.transpose(0, 2, 3, 1),
                                      bias, sink)
    y = _out_proj(ya, yb, rows, w['w_out'], w['g_post'])
    return (y.reshape(n_seq, DEC_SEQ, D_MODEL), s.reshape(n_seq, DEC_SEQ, HALF)[:, DEC_SEQ - 2:],
            new_kt.transpose(0, 3, 1, 2), new_vt.transpose(0, 3, 1, 2))


def _layer_norm(v, g, b):
    xc = v - jnp.mean(v, axis=-1, keepdims=True)
    return xc * lax.rsqrt(jnp.mean(xc * xc, axis=-1, keepdims=True) + NORM_EPS) * g + b


def _cmlp_prompt_kernel(x_ref, g_ref, w_ref, lng_ref, lnb_ref, ws_ref, bs_ref, yc_ref, vn_scr):
    tile = x_ref.shape[0]
    h = _rms_bf16(x_ref[...], g_ref[...])
    v = _dot(h, w_ref[:, HALF:2 * HALF])
    vn_scr[...] = _layer_norm(v, lng_ref[...], lnb_ref[...]).astype(BF16)
    gw = HALF // CMLP_GROUPS
    cols = 2 * gw
    for cb in range(HALF // cols):
        u = _dot(h, w_ref[:, cb * cols:(cb + 1) * cols])
        gate = _silu(_dot(h, w_ref[:, 2 * HALF + cb * cols:2 * HALF + (cb + 1) * cols]))
        for gi in range(2):
            grp = 2 * cb + gi
            lanes = slice(grp * gw, (grp + 1) * gw)
            for n in range(tile // CHUNK):
                rows = slice(n * CHUNK, (n + 1) * CHUNK)
                mixed = _dot(ws_ref[grp], vn_scr[rows, lanes]) + bs_ref[grp]
                yc_ref[rows, lanes] = (u[rows, gi * gw:(gi + 1) * gw] * mixed
                                       * gate[rows, gi * gw:(gi + 1) * gw]).astype(BF16)


def _cmlp_prompt(x, g_pre, w_c, ln_g, ln_b, ws_tril, bs_rows):
    t = x.shape[0]
    tile = min(ROW_TILE, t)
    return pl.pallas_call(
        _cmlp_prompt_kernel,
        grid=(t // tile,),
        in_specs=[_rows(tile, D_MODEL), _resident((1, D_MODEL)), _cols(D_MODEL, 3 * HALF, 0),
                  _resident((1, HALF)), _resident((1, HALF)), _resident(ws_tril.shape), _resident(bs_rows.shape)],
        out_specs=_rows(tile, HALF),
        out_shape=jax.ShapeDtypeStruct((t, HALF), BF16),
        scratch_shapes=[pltpu.VMEM((tile, HALF), BF16)],
        compiler_params=_params(),
        name="cmlp_prompt",
    )(x, g_pre, w_c, ln_g, ln_b, ws_tril, bs_rows)


def _cmlp_sample_kernel(x_ref, g_ref, w_ref, lng_ref, lnb_ref, coef_ref, bias_ref, yc_ref, vn_ref):
    t = x_ref.shape[0]
    h = _rms_bf16(x_ref[...], g_ref[...])
    u = _dot(h, w_ref[:, 0:HALF])
    vn = _layer_norm(_dot(h, w_ref[:, HALF:2 * HALF]), lng_ref[...], lnb_ref[...])
    gate = _silu(_dot(h, w_ref[:, 2 * HALF:3 * HALF]))
    vn_ref[...] = vn

    def tiled(a):
        return a.reshape(t // 8, 8, HALF)

    mixed = tiled(vn) * coef_ref[0][None] + bias_ref[...][None]
    for k in range(1, DEC_SEQ):
        mixed = mixed + tiled(pltpu.roll(vn, k, 0)) * coef_ref[k][None]
    yc_ref[...] = (u * mixed.reshape(t, HALF) * gate).astype(BF16)


def _cmlp_sample(x, g_pre, w_c, ln_g, ln_b, coef, bias):
    t = x.shape[0]
    return pl.pallas_call(
        _cmlp_sample_kernel,
        grid=(1,),
        in_specs=[_resident((t, D_MODEL)), _resident((1, D_MODEL)), _cols(D_MODEL, 3 * HALF, 0),
                  _resident((1, HALF)), _resident((1, HALF)), _resident(coef.shape), _resident(bias.shape)],
        out_specs=[_resident((t, HALF)), _resident((t, HALF))],
        out_shape=[jax.ShapeDtypeStruct((t, HALF), BF16), jax.ShapeDtypeStruct((t, HALF), F32)],
        compiler_params=_params(),
        name="cmlp_sample",
    )(x, g_pre, w_c, ln_g, ln_b, coef, bias)


HEAD_LANES = 128
SSD_GW = HALF // SSD_GROUPS


def _softplus(x):
    return jnp.maximum(x, 0.0) + jnp.log1p(jnp.exp(-jnp.abs(x)))


def _group_norm_gate(y, z, gn):
    gated = y * _silu(z)
    parts = []
    for g in range(SSD_GROUPS):
        part = gated[:, g * SSD_GW:(g + 1) * SSD_GW]
        parts.append(part * lax.rsqrt(jnp.mean(part * part, axis=-1, keepdims=True) + NORM_EPS))
    return (jnp.concatenate(parts, axis=1) * gn).astype(BF16)


def _ssd_prompt_kernel(x_ref, g_ref, wz_ref, wx0_ref, wx1_ref, wx2_ref, wdt_ref, cw_ref, cb_ref, dtb_ref, alog_ref,
                       aloge_ref, dskip_ref, gn_ref, e3_ref, tril3_ref, yd_ref, tail_ref, ssm_ref,
                       xbc_scr, z_scr, dt_scr, ht_scr):
    tile = x_ref.shape[0]
    i = pl.program_id(0)
    cd = SSD_CONV_DIM
    h = _rms_bf16(x_ref[...], g_ref[...])

    @pl.when(i == 0)
    def _():
        tail_ref[...] = jnp.zeros_like(tail_ref)
        ht_scr[...] = jnp.zeros_like(ht_scr)

    z_scr[...] = _dot(h, wz_ref[...])
    raw = jnp.concatenate([_dot(h, wx0_ref[...]), _dot(h, wx1_ref[...]), _dot(h, wx2_ref[...])], axis=1)
    dt_scr[...] = _softplus(_dot(h, wdt_ref[...]) + dtb_ref[...])
    prev = tail_ref[...]
    conv = raw * cw_ref[3:4, :] + cb_ref[...]
    for k in range(1, 4):
        conv = conv + _shift_rows(raw, k, prev) * cw_ref[3 - k:4 - k, :]
    xbc_scr[...] = _silu(conv)
    tail_ref[...] = raw[tile - 8:tile, :]

    a16 = -jnp.exp(alog_ref[...])
    a_e = -jnp.exp(aloge_ref[...])
    causal = (lax.broadcasted_iota(jnp.int32, (CHUNK, CHUNK), 0)
              >= lax.broadcasted_iota(jnp.int32, (CHUNK, CHUNK), 1))

    def chunk(n, carry):
        r0 = pl.multiple_of(n * CHUNK, CHUNK)
        rows = pl.ds(r0, CHUNK)
        xs = xbc_scr[rows, 0:HALF]
        dt16 = dt_scr[rows, :]
        dt_e = _dot(jnp.concatenate(_split3(dt16), axis=1), e3_ref[...])
        da_e = dt_e * a_e
        acs_e = _dot(tril3_ref[...], jnp.concatenate(_split3(da_e), axis=0))
        acs16 = _dot(tril3_ref[...], jnp.concatenate(_split3(dt16 * a16), axis=0))
        acs_t = acs16.T
        dt_t = dt16.T
        last_e = acs_e[CHUNK - 1:CHUNK, :]
        xs_bf = xs.astype(BF16)
        xw = (jnp.exp(last_e - acs_e) * dt_e * xs).astype(BF16)
        dec_e = jnp.exp(last_e)
        y_parts = []
        yoff_parts = []
        for g in range(SSD_GROUPS):
            c_g = xbc_scr[rows, HALF + 2 * SSD_STATE + g * SSD_STATE:HALF + 2 * SSD_STATE + (g + 1) * SSD_STATE].astype(BF16)
            b_g = xbc_scr[rows, HALF + g * SSD_STATE:HALF + (g + 1) * SSD_STATE].astype(BF16)
            cb = _dot_nt(c_g, b_g)
            h_prev = ht_scr[g]
            yoff_parts.append(_dot(c_g, h_prev.astype(BF16)))
            for r in range(SSD_HEADS // SSD_GROUPS):
                hd = g * (SSD_HEADS // SSD_GROUPS) + r
                seg = acs16[:, hd:hd + 1] - acs_t[hd:hd + 1, :]
                wgt = cb * jnp.exp(jnp.where(causal, seg, -jnp.inf)) * dt_t[hd:hd + 1, :]
                y_parts.append(_dot(wgt.astype(BF16), xs_bf[:, hd * HEAD_DIM:(hd + 1) * HEAD_DIM]))
            lanes = slice(g * SSD_GW, (g + 1) * SSD_GW)
            ht_scr[g] = h_prev * dec_e[:, lanes] + _dot_tn(b_g, xw[:, lanes])
        y = (jnp.concatenate(y_parts, axis=1) + jnp.concatenate(yoff_parts, axis=1) * jnp.exp(acs_e)
             + dskip_ref[...] * xs)
        yd_ref[rows, :] = _group_norm_gate(y, z_scr[rows, :], gn_ref[...])
        return carry

    lax.fori_loop(0, tile // CHUNK, chunk, 0)

    @pl.when(i == pl.num_programs(0) - 1)
    def _():
        for g in range(SSD_GROUPS):
            ssm_ref[g * SSD_GW:(g + 1) * SSD_GW, :] = ht_scr[g].T


def _ssd_weight_specs():
    third = SSD_CONV_DIM // 3
    first = 4 * HALF // third
    return ([_cols(D_MODEL, HALF, 3)] + [_cols(D_MODEL, third, first + j) for j in range(3)]
            + [_resident((D_MODEL, HEAD_LANES))])


def _ssd_prompt(x, w):
    t = x.shape[0]
    tile = min(ROW_TILE, t)
    cd = SSD_CONV_DIM
    consts = [w['conv_w'], w['conv_b'], w['dt_bias16'], w['a_log16'], w['a_log_e'],
              w['d_skip_e'], w['gate_norm_g'], w['expand3'], w['tril3']]
    return pl.pallas_call(
        _ssd_prompt_kernel,
        grid=(t // tile,),
        in_specs=[_rows(tile, D_MODEL), _resident((1, D_MODEL))] + _ssd_weight_specs()
                 + [_resident(c.shape) for c in consts],
        out_specs=[_rows(tile, HALF), pl.BlockSpec((8, cd), lambda i: (0, 0)),
                   pl.BlockSpec((HALF, SSD_STATE), lambda i: (0, 0))],
        out_shape=[jax.ShapeDtypeStruct((t, HALF), BF16), jax.ShapeDtypeStruct((8, cd), F32),
                   jax.ShapeDtypeStruct((HALF, SSD_STATE), F32)],
        scratch_shapes=[pltpu.VMEM((tile, cd), F32), pltpu.VMEM((tile, HALF), F32),
                        pltpu.VMEM((tile, HEAD_LANES), F32), pltpu.VMEM((SSD_GROUPS, SSD_STATE, SSD_GW), F32)],
        compiler_params=_params(),
        name="ssd_prompt",
    )(x, w['g_pre'], w['w1'], w['w1'], w['w1'], w['w1'], w['w_dt'], *consts)


def _ssd_sample_pre_kernel(x_ref, g_ref, wz_ref, wx0_ref, wx1_ref, wx2_ref, wdt_ref, cw_ref, cb_ref, st_ref,
                           dtb_ref, aloge_ref, dskip_ref, e3_ref, seg_ref,
                           nconv_ref, z_ref, ysk_ref, eacs_ref, xw_ref, dec_ref, b_ref, c_ref, raw_scr):
    t = x_ref.shape[0]
    n_seq = t // DEC_SEQ
    h = _rms_bf16(x_ref[...], g_ref[...])
    z_ref[...] = _dot(h, wz_ref[...])
    raw = jnp.concatenate([_dot(h, wx0_ref[...]), _dot(h, wx1_ref[...]), _dot(h, wx2_ref[...])], axis=1)
    for c in range(raw_scr.shape[0]):
        lanes = slice(c * 128, (c + 1) * 128)
        raw_scr[c] = raw[:, lanes]
        for j in range(3):
            nconv_ref[j, :, lanes] = raw_scr[c, pl.ds(j + 1, n_seq, stride=DEC_SEQ), :]
    dt16 = _softplus(_dot(h, wdt_ref[...]) + dtb_ref[...])
    dt = _dot(jnp.concatenate(_split3(dt16), axis=1), e3_ref[...])
    old = [st_ref[j] for j in range(3)]
    p1 = _place_steps(t, [(0, old[2])])
    p2 = _place_steps(t, [(0, old[1]), (1, old[2])])
    p3 = _place_steps(t, [(0, old[0]), (1, old[1]), (2, old[2])])

    def step_of(width):
        return lax.broadcasted_iota(jnp.int32, (t, width), 0) % DEC_SEQ

    def back(a, k):
        return jnp.where(step_of(a.shape[1]) >= k, pltpu.roll(a, k, 0), 0.0)

    def ahead(a, k):
        return jnp.where(step_of(a.shape[1]) + k < DEC_SEQ, pltpu.roll(a, t - k, 0), 0.0)

    conv = (raw * cw_ref[3:4, :] + (back(raw, 1) + p1) * cw_ref[2:3, :]
            + (back(raw, 2) + p2) * cw_ref[1:2, :] + (back(raw, 3) + p3) * cw_ref[0:1, :]
            + cb_ref[...])
    xbc = _silu(conv)
    xs = xbc[:, 0:HALF]
    bm = xbc[:, HALF:HALF + 2 * SSD_STATE]
    cm = xbc[:, HALF + 2 * SSD_STATE:]
    b_ref[...] = bm
    c_ref[...] = cm
    da = dt * (-jnp.exp(aloge_ref[...]))
    acs = da + back(da, 1) + back(da, 2) + back(da, 3)
    suffix = ahead(da, 1) + ahead(da, 2) + ahead(da, 3)
    xdt = xs * dt
    y = _dot((cm * bm).astype(BF16), seg_ref[...]) * xdt
    for k in range(1, DEC_SEQ):
        cbk = _dot((cm * pltpu.roll(bm, k, 0)).astype(BF16), seg_ref[...])
        term = cbk * jnp.exp(acs - pltpu.roll(acs, k, 0)) * pltpu.roll(xdt, k, 0)
        y = y + jnp.where(step_of(HALF) >= k, term, 0.0)
    ysk_ref[...] = y + dskip_ref[...] * xs
    eacs_ref[...] = jnp.exp(acs)
    xw_ref[...] = jnp.exp(suffix) * xdt
    dec_ref[...] = jnp.exp(acs + suffix)


def _ssd_sample_pre(x, conv_state, w):
    t = x.shape[0]
    cd = SSD_CONV_DIM
    consts = [w['conv_w'], w['conv_b'], conv_state, w['dt_bias16'], w['a_log_e'], w['d_skip_e'], w['expand3'],
              w['seg_expand']]
    args = [x, w['g_pre'], w['w1'], w['w1'], w['w1'], w['w1'], w['w_dt']] + consts
    wide = jax.ShapeDtypeStruct((t, HALF), F32)
    narrow = jax.ShapeDtypeStruct((t, 2 * SSD_STATE), F32)
    out_shape = [jax.ShapeDtypeStruct(conv_state.shape, F32), wide, wide, wide, wide, wide, narrow, narrow]
    return pl.pallas_call(
        _ssd_sample_pre_kernel,
        grid=(1,),
        in_specs=[_resident((t, D_MODEL)), _resident((1, D_MODEL))] + _ssd_weight_specs()
                 + [_resident(c.shape) for c in consts],
        out_specs=[_resident(s.shape) for s in out_shape],
        out_shape=out_shape,
        scratch_shapes=[pltpu.VMEM((cd // 128, t, 128), F32)],
        compiler_params=_params(),
        name="ssd_sample_pre",
    )(*args)


SSD_S_BATCH = 8


def _ssd_sample_state_kernel(st_ref, c_ref, b_ref, xw_ref, dec_ref, eacs_ref, ysk_ref, z_ref, gn_ref,
                             yd_ref, nst_ref):
    row_n = lax.broadcasted_iota(jnp.int32, (8, SSD_STATE), 0)
    row_w = lax.broadcasted_iota(jnp.int32, (8, SSD_GW), 0)
    row_f = lax.broadcasted_iota(jnp.int32, (8, HALF), 0)
    ones_rows = jnp.where((row_n >= 4) & (row_n < 7), 1.0, 0.0).astype(BF16)
    hpg = SSD_HEADS // SSD_GROUPS

    def pair(p, carry):
        r0 = pl.multiple_of(p * 8, 8)
        rows = pl.ds(r0, 8)
        c8 = c_ref[rows, :].astype(BF16)
        b8 = b_ref[rows, :]
        xw8 = xw_ref[rows, :]
        dec8 = dec_ref[rows, :]
        yoff = []
        for sub in range(2):
            b = 2 * p + sub
            xw_own = xw8 if sub == 0 else pltpu.roll(xw8, 4, 0)
            b_own = b8 if sub == 0 else pltpu.roll(b8, 4, 0)
            hi, mid, lo = (term.astype(F32) for term in _split3(dec8[4 * sub:4 * sub + 1, :]))
            parts = []
            for g in range(SSD_GROUPS):
                lanes = slice(g * SSD_GW, (g + 1) * SSD_GW)
                heads = pl.ds(g * hpg, hpg)
                h0 = st_ref[b, heads].reshape(SSD_GW, SSD_STATE)
                parts.append(_dot_nt(c8[:, g * SSD_STATE:(g + 1) * SSD_STATE], h0.astype(BF16)))
                lhs = jnp.where(row_w < 4, xw_own[:, lanes],
                                jnp.where(row_w == 4, hi[:, lanes],
                                          jnp.where(row_w == 5, mid[:, lanes],
                                                    jnp.where(row_w == 6, lo[:, lanes], 0.0)))).astype(BF16)
                rhs_b = jnp.where(row_n < 4, b_own[:, g * SSD_STATE:(g + 1) * SSD_STATE], 0.0).astype(BF16)
                decay = _dot_tn(lhs, ones_rows)
                nst_ref[b, heads] = (h0 * decay + _dot_tn(lhs, rhs_b)).reshape(hpg, HEAD_DIM, SSD_STATE)
            yoff.append(jnp.concatenate(parts, axis=1))
        yoff8 = jnp.where(row_f < 4, yoff[0], yoff[1])
        y = ysk_ref[rows, :] + yoff8 * eacs_ref[rows, :]
        yd_ref[rows, :] = _group_norm_gate(y, z_ref[rows, :], gn_ref[...])
        return carry

    lax.fori_loop(0, SSD_S_BATCH // 2, pair, 0)


def _ssd_sample_state(state, cm, bm, xw, dec, eacs, ysk, z, gn):
    n_seq = state.shape[0]
    bb = SSD_S_BATCH
    r = bb * DEC_SEQ
    st_spec = pl.BlockSpec((bb, SSD_HEADS, HEAD_DIM, SSD_STATE), lambda i: (i, 0, 0, 0))
    return pl.pallas_call(
        _ssd_sample_state_kernel,
        grid=(n_seq // bb,),
        in_specs=[st_spec, _rows(r, 2 * SSD_STATE), _rows(r, 2 * SSD_STATE)] + [_rows(r, HALF)] * 5
                 + [_resident((1, HALF))],
        out_specs=[_rows(r, HALF), st_spec],
        out_shape=[jax.ShapeDtypeStruct((n_seq * DEC_SEQ, HALF), BF16), jax.ShapeDtypeStruct(state.shape, F32)],
        compiler_params=_params(),
        name="ssd_sample_state",
    )(state, cm, bm, xw, dec, eacs, ysk, z, gn)


def _prep_layer1(g_pre, w_in, ln_g, ln_b, w_s, b_s, conv_w, conv_b, dt_bias, a_log, d_skip, gate_norm_g,
                 w_out, g_post):
    cd = SSD_CONV_DIM
    gw = HALF // CMLP_GROUPS
    w1 = w_in.astype(BF16)
    w_dt = jnp.pad(w_in[:, 4 * HALF + cd:], ((0, 0), (0, HEAD_LANES - SSD_HEADS))).astype(BF16)

    def lanes16(v):
        return jnp.pad(v.astype(F32), (0, HEAD_LANES - SSD_HEADS)).reshape(1, HEAD_LANES)

    def per_channel(v):
        return jnp.repeat(v.astype(F32), HEAD_DIM).reshape(1, HALF)

    head_of = jnp.arange(HALF) // HEAD_DIM
    expand = (jnp.arange(HEAD_LANES)[:, None] == head_of[None, :]).astype(BF16)
    tril = jnp.tril(jnp.ones((CHUNK, CHUNK), BF16))
    grp_rows = jnp.arange(2 * SSD_STATE) // SSD_STATE
    seg_expand = (grp_rows[:, None] == (head_of // (SSD_HEADS // SSD_GROUPS))[None, :]).astype(BF16)

    w4 = jnp.tril(w_s[:, :DEC_SEQ, :DEC_SEQ])
    steps = jnp.arange(DEC_SEQ)
    coef = []
    for k in range(DEC_SEQ):
        src = steps - k
        ck = jnp.where((src >= 0)[None, :], w4[:, steps, jnp.maximum(src, 0)], 0.0)
        ck = jnp.repeat(ck.T, gw, axis=1)
        coef.append(jnp.concatenate([ck, ck], axis=0))
    bias4 = jnp.repeat(b_s[:, :DEC_SEQ].T, gw, axis=1)
    return dict(
        g_pre=g_pre.reshape(1, D_MODEL), w1=w1, w_dt=w_dt, ln_g=ln_g.reshape(1, HALF), ln_b=ln_b.reshape(1, HALF),
        ws_tril=jnp.tril(w_s).astype(BF16),
        bs_rows=jnp.broadcast_to(b_s.astype(F32)[:, :, None], (CMLP_GROUPS, CHUNK, gw)),
        coef=jnp.stack(coef).astype(F32), bias4=jnp.concatenate([bias4, bias4], axis=0).astype(F32),
        conv_w=conv_w, conv_b=conv_b.reshape(1, cd), dt_bias16=lanes16(dt_bias), a_log16=lanes16(a_log),
        a_log_e=per_channel(a_log), d_skip_e=per_channel(d_skip), gate_norm_g=gate_norm_g.reshape(1, HALF),
        expand3=jnp.concatenate([expand] * 3, axis=0), tril3=jnp.concatenate([tril] * 3, axis=1),
        seg_expand=seg_expand, w_out=w_out.astype(BF16), g_post=g_post.reshape(1, D_MODEL))


def _layer1_prompt(x, w):
    yc = _cmlp_prompt(x, w['g_pre'], w['w1'], w['ln_g'], w['ln_b'], w['ws_tril'], w['bs_rows'])
    yd, tail, ssm = _ssd_prompt(x, w)
    y = _out_proj(yc, yd, x, w['w_out'], w['g_post'])
    return y, tail[5:8], ssm.reshape(SSD_HEADS, HEAD_DIM, SSD_STATE)


def _layer1_sample(x, conv_state, ssm_state, w):
    n_seq = x.shape[0]
    t = n_seq * DEC_SEQ
    cd = SSD_CONV_DIM
    rows = x.reshape(t, D_MODEL)
    yc, vn = _cmlp_sample(rows, w['g_pre'], w['w1'], w['ln_g'], w['ln_b'], w['coef'], w['bias4'])
    new_conv, z, ysk, eacs, xw, dec, bm, cm = _ssd_sample_pre(rows, conv_state.transpose(1, 0, 2), w)
    yd, new_state = _ssd_sample_state(ssm_state, cm, bm, xw, dec, eacs, ysk, z, w['gate_norm_g'])
    y = _out_proj(yc, yd, rows, w['w_out'], w['g_post'])
    return (y.reshape(n_seq, DEC_SEQ, D_MODEL), vn.reshape(n_seq, DEC_SEQ, HALF),
            new_conv.transpose(1, 0, 2), new_state)


def kernel(x_prompt, x_sample, state_conv_a, cache_win_k, cache_win_v, state_conv_d, state_ssm, rel_bias,
           l0_g_pre, l0_w_in, l0_conv_w, l0_sinks, l0_w_out, l0_g_post,
           l1_g_pre, l1_w_in, l1_ln_g, l1_ln_b, l1_w_s, l1_b_s, l1_conv_w, l1_conv_b, l1_dt_bias, l1_a_log,
           l1_d_skip, l1_gate_norm_g, l1_w_out, l1_g_post):
    w0 = _prep_layer0(l0_g_pre, l0_w_in, l0_conv_w, rel_bias, l0_sinks, l0_w_out, l0_g_post)
    w1 = _prep_layer1(l1_g_pre, l1_w_in, l1_ln_g, l1_ln_b, l1_w_s, l1_b_s, l1_conv_w, l1_conv_b, l1_dt_bias,
                      l1_a_log, l1_d_skip, l1_gate_norm_g, l1_w_out, l1_g_post)
    yp, p_conv_a, p_win_k, p_win_v = _layer0_prompt(x_prompt[0], w0)
    ys, s_conv_a, s_win_k, s_win_v = _layer0_sample(x_sample, state_conv_a, cache_win_k, cache_win_v, w0)
    yp, p_conv_d, p_ssm = _layer1_prompt(yp, w1)
    ys, s_chunk_v, s_conv_d, s_ssm = _layer1_sample(ys, state_conv_d, state_ssm, w1)
    return (yp[None], ys, p_conv_a[None], s_conv_a, p_win_k[None], p_win_v[None], s_win_k, s_win_v, s_chunk_v,
            p_conv_d[None], s_conv_d, p_ssm[None], s_ssm)
```

```python
import functools
import math

import jax
import jax.numpy as jnp
from jax import lax
from jax.experimental import pallas as pl
from jax.experimental.pallas import tpu as pltpu

F32 = jnp.float32
BF16 = jnp.bfloat16

D_MODEL = 2048
HALF = 1024
HEAD_DIM = 64
N_HEADS = 16
N_KV = 4
GROUP = 4
WINDOW = 128
NUM_BUCKETS = 32
MAX_DISTANCE = 128
CMLP_GROUPS = 8
CHUNK = 128
SSD_HEADS = 16
SSD_STATE = 128
SSD_GROUPS = 2
SSD_CONV_DIM = HALF + 2 * SSD_GROUPS * SSD_STATE
DEC_SEQ = 4
NORM_EPS = 1e-6
MASK_VALUE = -1e30

ROW_TILE = 512
VMEM_LIMIT = 56 * 1024 * 1024


def _params(n_axes=1):
    return pltpu.CompilerParams(dimension_semantics=("arbitrary",) * n_axes,
                                vmem_limit_bytes=VMEM_LIMIT)


def _resident(shape):
    nd = len(shape)
    return pl.BlockSpec(shape, lambda *_: (0,) * nd, pipeline_mode=pl.Buffered(1))


def _rows(tile, width):
    return pl.BlockSpec((tile, width), lambda i: (i, 0))


def _cols(rows, width, block):
    return pl.BlockSpec((rows, width), lambda *_: (0, block), pipeline_mode=pl.Buffered(1))


def _rms_bf16(x, g):
    ms = jnp.mean(x * x, axis=-1, keepdims=True)
    return (x * lax.rsqrt(ms + NORM_EPS) * g).astype(BF16)


def _silu(x):
    return x * jax.nn.sigmoid(x)


def _dot(a, b):
    return jnp.dot(a, b, preferred_element_type=F32)


def _dot_nt(a, b):
    return lax.dot_general(a, b, (((1,), (1,)), ((), ())), preferred_element_type=F32)


def _dot_tn(a, b):
    return lax.dot_general(a, b, (((0,), (0,)), ((), ())), preferred_element_type=F32)


def _split3(x):
    hi = x.astype(BF16)
    r1 = x - hi.astype(F32)
    mid = r1.astype(BF16)
    lo = (r1 - mid.astype(F32)).astype(BF16)
    return hi, mid, lo


def _place_steps(t, placements):
    n_seq = placements[0][1].shape[0]
    row = lax.broadcasted_iota(jnp.int32, (t, n_seq), 0)
    seq = lax.broadcasted_iota(jnp.int32, (t, n_seq), 1)
    lhs, rhs = [], []
    for step, state in placements:
        sel = jnp.where(row == DEC_SEQ * seq + step, 1.0, 0.0).astype(BF16)
        lhs += [sel] * 3
        rhs += list(_split3(state))
    return _dot(jnp.concatenate(lhs, axis=1), jnp.concatenate(rhs, axis=0))


def _shift_rows(x, k, prev_rows=None):
    r = pltpu.roll(x, k, 0)
    if prev_rows is None:
        return r
    head = r[0:8, :]
    row = lax.broadcasted_iota(jnp.int32, head.shape, 0)
    n_prev = prev_rows.shape[0]
    for t in range(k):
        src = n_prev - k + t
        head = jnp.where(row == t, prev_rows[src:src + 1, :], head)
    return jnp.concatenate([head, r[8:, :]], axis=0)


def _out_proj_kernel(ya_ref, yb_ref, x_ref, w_ref, g_ref, o_ref):
    y = _dot(ya_ref[...], w_ref[0:HALF, :]) + _dot(yb_ref[...], w_ref[HALF:2 * HALF, :])
    ms = jnp.mean(y * y, axis=-1, keepdims=True)
    o_ref[...] = x_ref[...] + y * lax.rsqrt(ms + NORM_EPS) * g_ref[...]


def _out_proj(ya, yb, x, w_bf, g):
    t = x.shape[0]
    tile = min(ROW_TILE, t)
    return pl.pallas_call(
        _out_proj_kernel,
        grid=(t // tile,),
        in_specs=[_rows(tile, HALF), _rows(tile, HALF), _rows(tile, D_MODEL),
                  _resident((2 * HALF, D_MODEL)), _resident((1, D_MODEL))],
        out_specs=_rows(tile, D_MODEL),
        out_shape=jax.ShapeDtypeStruct((t, D_MODEL), F32),
        compiler_params=_params(),
        name="out_proj",
    )(ya, yb, x, w_bf, g)


CONV_A_CHUNK = 256


def _conv_a_kernel(*refs, sample):
    if sample:
        x_ref, g_ref, w_ref, cw_ref, st_ref, ya_ref, s_ref = refs
    else:
        x_ref, g_ref, w_ref, cw_ref, ya_ref, s_ref = refs
    tile = x_ref.shape[0]
    cc = CONV_A_CHUNK
    h = _rms_bf16(x_ref[...], g_ref[...])
    if not sample:
        @pl.when(pl.program_id(0) == 0)
        def _():
            s_ref[...] = jnp.zeros_like(s_ref)
    for c in range(HALF // cc):
        lanes = slice(c * cc, (c + 1) * cc)
        a_b, a_c, a_h, a_g = (_dot(h, w_ref[:, j * HALF + c * cc:j * HALF + (c + 1) * cc]) for j in range(4))
        s = a_c * a_h
        if sample:
            t_in = lax.broadcasted_iota(jnp.int32, s.shape, 0) % DEC_SEQ
            old0 = st_ref[:, c * cc:(c + 1) * cc]
            old1 = st_ref[:, HALF + c * cc:HALF + (c + 1) * cc]
            p1 = jnp.where(t_in >= 1, _shift_rows(s, 1), 0.0) + _place_steps(tile, [(0, old1)])
            p2 = jnp.where(t_in >= 2, _shift_rows(s, 2), 0.0) + _place_steps(tile, [(0, old0), (1, old1)])
            s_ref[:, lanes] = s
        else:
            prev = s_ref[:, lanes]
            p1 = _shift_rows(s, 1, prev)
            p2 = _shift_rows(s, 2, prev)
            s_ref[:, lanes] = s[tile - 8:tile, :]
        conv = p2 * cw_ref[0:1, lanes] + p1 * cw_ref[1:2, lanes] + s * cw_ref[2:3, lanes]
        ya_ref[:, lanes] = (a_b * conv * _silu(a_g)).astype(BF16)


def _conv_a(x, g_pre, w0, conv_w, state=None):
    t = x.shape[0]
    sample = state is not None
    tile = t if sample else min(ROW_TILE, t)
    in_specs = [_rows(tile, D_MODEL), _resident((1, D_MODEL)), _cols(D_MODEL, 4 * HALF, 0), _resident((3, HALF))]
    args = [x, g_pre, w0, conv_w]
    if sample:
        in_specs.append(_resident(state.shape))
        args.append(state)
        s_spec, s_shape = _rows(tile, HALF), (t, HALF)
    else:
        s_spec, s_shape = pl.BlockSpec((8, HALF), lambda i: (0, 0)), (8, HALF)
    return pl.pallas_call(
        functools.partial(_conv_a_kernel, sample=sample),
        grid=(t // tile,),
        in_specs=in_specs,
        out_specs=[_rows(tile, HALF), s_spec],
        out_shape=[jax.ShapeDtypeStruct((t, HALF), BF16), jax.ShapeDtypeStruct(s_shape, F32)],
        compiler_params=_params(),
        name="conv_a_sample" if sample else "conv_a_prompt",
    )(*args)


def _rel_bucket(dist):
    max_exact = NUM_BUCKETS // 2
    d = jnp.maximum(dist, 0)
    ratio = jnp.maximum(d, max_exact).astype(F32) / max_exact
    large = max_exact + (jnp.log(ratio) / math.log(MAX_DISTANCE / max_exact)
                         * (NUM_BUCKETS - max_exact)).astype(jnp.int32)
    return jnp.where(d < max_exact, d, jnp.minimum(large, NUM_BUCKETS - 1))


def _attn_softmax_pv(s, sink, v_bf, v_transposed=False):
    m = jnp.maximum(jnp.max(s, axis=-1, keepdims=True), sink)
    p = jnp.exp(s - m)
    den = jnp.sum(p, axis=-1, keepdims=True) + jnp.exp(sink - m)
    pv = _dot_nt(p.astype(BF16), v_bf) if v_transposed else _dot(p.astype(BF16), v_bf)
    return pv / den


def _attn_prompt_kernel(x_ref, g_ref, wq_ref, wkv_ref, wg0_ref, wg1_ref, tab_ref, sink_ref, yb_ref, kwin_ref,
                        vwin_ref, q_scr, gate_scr, k_scr, v_scr, bias_scr):
    tile = x_ref.shape[0]
    i = pl.program_id(0)
    kv_w = N_KV * HEAD_DIM
    h = _rms_bf16(x_ref[...], g_ref[...])

    kw, vw = 2 * HEAD_DIM, 4 * HEAD_DIM

    @pl.when(i == 0)
    def _():
        k_scr[0:WINDOW, :] = jnp.zeros((WINDOW, N_KV * kw), BF16)
        v_scr[0:WINDOW, :] = jnp.zeros((WINDOW, N_KV * vw), BF16)
        for hk in range(N_KV):
            v_scr[:, hk * vw + kw:(hk + 1) * vw] = jnp.ones((tile + WINDOW, kw), BF16)
        in_own = lax.broadcasted_iota(jnp.int32, (WINDOW, 2 * WINDOW), 1) >= WINDOW
        for head in range(N_HEADS):
            row = jnp.broadcast_to(tab_ref[head:head + 1, :], (WINDOW, BIAS_SPAN))
            band = pltpu.roll(row, 0, 1, stride=1, stride_axis=0)[:, 0:2 * WINDOW]
            rows = slice((head % 2) * WINDOW, (head % 2 + 1) * WINDOW)
            bias_scr[1, head // 2, rows, :] = band
            bias_scr[0, head // 2, rows, :] = jnp.where(in_own, band, MASK_VALUE)

    q_scr[...] = (_dot(h, wq_ref[...]) * (HEAD_DIM ** -0.5)).astype(BF16)
    k = _dot(h, wkv_ref[:, 0:kv_w])
    v = _dot(h, wkv_ref[:, kv_w:2 * kv_w])
    gate_scr[:, 0:HALF // 2] = _silu(_dot(h, wg0_ref[...]))
    gate_scr[:, HALF // 2:HALF] = _silu(_dot(h, wg1_ref[...]))
    for hk in range(N_KV):
        k_h = k[:, hk * HEAD_DIM:(hk + 1) * HEAD_DIM].astype(BF16)
        v_h = v[:, hk * HEAD_DIM:(hk + 1) * HEAD_DIM].astype(BF16)
        k_scr[WINDOW:WINDOW + tile, hk * kw:(hk + 1) * kw] = jnp.concatenate([k_h, k_h], axis=1)
        v_scr[WINDOW:WINDOW + tile, hk * vw:hk * vw + kw] = jnp.concatenate([v_h, v_h], axis=1)
    kwin_ref[...] = k[tile - WINDOW:tile, :]
    vwin_ref[...] = v[tile - WINDOW:tile, :]

    lane = lax.broadcasted_iota(jnp.int32, (WINDOW, kw), 1)
    lo = lane < HEAD_DIM
    keep_a = jnp.where(lo, 1.0, 0.0).astype(BF16)
    keep_b = jnp.where(lo, 0.0, 1.0).astype(BF16)
    is_a = lax.broadcasted_iota(jnp.int32, (2 * WINDOW, 1), 0) < WINDOW

    def block(n, carry):
        r0 = pl.multiple_of(n * WINDOW, WINDOW)
        rows = pl.ds(r0, WINDOW)
        keys = pl.ds(r0, 2 * WINDOW)
        first = jnp.where(jnp.logical_and(i == 0, n == 0), 0, 1)
        for hk in range(N_KV):
            for gp in range(GROUP // 2):
                a = hk * GROUP + 2 * gp
                slab = slice(a * HEAD_DIM, (a + 2) * HEAD_DIM)
                q2 = q_scr[rows, slab]
                lhs = jnp.concatenate([q2 * keep_a, q2 * keep_b], axis=0)
                s = _dot_nt(lhs, k_scr[keys, hk * kw:(hk + 1) * kw]) + bias_scr[first, a // 2]
                sink = jnp.where(is_a, sink_ref[a], sink_ref[a + 1])
                m = jnp.maximum(jnp.max(s, axis=-1, keepdims=True), sink)
                p = jnp.exp(s - m).astype(BF16)
                pv = _dot(p, v_scr[keys, hk * vw:(hk + 1) * vw])
                num = jnp.where(lo, pv[0:WINDOW, 0:kw], pv[WINDOW:2 * WINDOW, 0:kw])
                den = jnp.where(lo, pv[0:WINDOW, kw:2 * kw], pv[WINDOW:2 * WINDOW, kw:2 * kw])
                m_slab = jnp.where(lo, m[0:WINDOW], m[WINDOW:2 * WINDOW])
                den = den + jnp.exp(jnp.where(lo, sink_ref[a], sink_ref[a + 1]) - m_slab)
                yb_ref[rows, slab] = (num / den * gate_scr[rows, slab]).astype(BF16)
        return carry

    lax.fori_loop(0, tile // WINDOW, block, 0)
    k_scr[0:WINDOW, :] = k_scr[tile:tile + WINDOW, :]
    v_scr[0:WINDOW, :] = v_scr[tile:tile + WINDOW, :]


BIAS_SPAN = 3 * WINDOW


def _prompt_bias_table(rel_bias):
    dist = WINDOW - jnp.arange(BIAS_SPAN)
    table = jnp.where(((dist >= 0) & (dist < WINDOW))[:, None], rel_bias.astype(F32)[_rel_bucket(dist)], MASK_VALUE)
    return table.T


def _attn_prompt(x, g_pre, w0, table, sinks):
    t = x.shape[0]
    tile = min(ROW_TILE, t)
    kv_w = N_KV * HEAD_DIM
    win_spec = pl.BlockSpec((WINDOW, kv_w), lambda i: (0, 0))
    return pl.pallas_call(
        _attn_prompt_kernel,
        grid=(t // tile,),
        in_specs=[_rows(tile, D_MODEL), _resident((1, D_MODEL)),
                  _cols(D_MODEL, HALF, 4), _cols(D_MODEL, 2 * kv_w, 10),
                  _cols(D_MODEL, HALF // 2, 11), _cols(D_MODEL, HALF // 2, 12),
                  _resident(table.shape), pl.BlockSpec(memory_space=pltpu.SMEM)],
        out_specs=[_rows(tile, HALF), win_spec, win_spec],
        out_shape=[jax.ShapeDtypeStruct((t, HALF), BF16),
                   jax.ShapeDtypeStruct((WINDOW, kv_w), F32), jax.ShapeDtypeStruct((WINDOW, kv_w), F32)],
        scratch_shapes=[pltpu.VMEM((tile, HALF), BF16), pltpu.VMEM((tile, HALF), F32),
                        pltpu.VMEM((tile + WINDOW, 2 * kv_w), BF16), pltpu.VMEM((tile + WINDOW, 4 * kv_w), BF16),
                        pltpu.VMEM((2, N_HEADS // 2, 2 * WINDOW, 2 * WINDOW), F32)],
        compiler_params=_params(),
        name="attn_prompt",
    )(x, g_pre, w0, w0, w0, w0, table, sinks)


def _attn_proj_kernel(x_ref, g_ref, wq_ref, wkv_ref, wg0_ref, wg1_ref, qg_ref, kt_ref, vt_ref, kv_scr):
    kv_w = N_KV * HEAD_DIM
    h = _rms_bf16(x_ref[...], g_ref[...])
    q = _dot(h, wq_ref[...]) * (HEAD_DIM ** -0.5)
    for hk in range(N_KV):
        for g in range(GROUP):
            src = (hk * GROUP + g) * HEAD_DIM
            dst = (g * N_KV + hk) * HEAD_DIM
            qg_ref[:, dst:dst + HEAD_DIM] = q[:, src:src + HEAD_DIM]
    qg_ref[:, HALF:HALF + HALF // 2] = _dot(h, wg0_ref[...])
    qg_ref[:, HALF + HALF // 2:2 * HALF] = _dot(h, wg1_ref[...])
    kv_scr[...] = _dot(h, wkv_ref[...])
    for j in range(kt_ref.shape[0]):
        kt_ref[j] = kv_scr[j * WINDOW:(j + 1) * WINDOW, 0:kv_w].T
        vt_ref[j] = kv_scr[j * WINDOW:(j + 1) * WINDOW, kv_w:2 * kv_w].T


def _attn_proj(x, g_pre, w0):
    t = x.shape[0]
    kv_w = N_KV * HEAD_DIM
    out_shape = [jax.ShapeDtypeStruct((t, 2 * HALF), F32), jax.ShapeDtypeStruct((t // WINDOW, kv_w, WINDOW), F32),
                 jax.ShapeDtypeStruct((t // WINDOW, kv_w, WINDOW), F32)]
    return pl.pallas_call(
        _attn_proj_kernel,
        grid=(1,),
        in_specs=[_resident((t, D_MODEL)), _resident((1, D_MODEL)),
                  _cols(D_MODEL, HALF, 4), _cols(D_MODEL, 2 * kv_w, 10),
                  _cols(D_MODEL, HALF // 2, 11), _cols(D_MODEL, HALF // 2, 12)],
        out_specs=[_resident(s.shape) for s in out_shape],
        out_shape=out_shape,
        scratch_shapes=[pltpu.VMEM((t, 2 * kv_w), F32)],
        compiler_params=_params(),
        name="attn_proj_sample",
    )(x, g_pre, w0, w0, w0, w0)


ATTN_S_BATCH = 16
KEYS_PAD = 2 * WINDOW


def _attn_sample_kernel(qg_ref, ktn_ref, vtn_ref, ck_ref, cv_ref, bias_ref, sink_ref, yb_ref, nk_ref, nv_ref):
    kv_w = N_KV * HEAD_DIM
    row8 = lax.broadcasted_iota(jnp.int32, (8, kv_w), 0)
    lane_head = lax.broadcasted_iota(jnp.int32, (8, kv_w), 1) // HEAD_DIM
    lower = row8 < DEC_SEQ
    pick = [jnp.where(lane_head == 2 * hp + jnp.where(lower, 0, 1), 1.0, 0.0).astype(F32) for hp in range(2)]
    lower_w = lax.broadcasted_iota(jnp.int32, (8, HALF), 0) < DEC_SEQ
    kept = lax.broadcasted_iota(jnp.int32, (kv_w, WINDOW), 1) < WINDOW - DEC_SEQ
    seq0 = pl.program_id(0) * ATTN_S_BATCH
    per_tile = WINDOW // DEC_SEQ

    def slide(old, new_tile, shift):
        return jnp.where(kept, pltpu.roll(old, WINDOW - DEC_SEQ, 1), pltpu.roll(new_tile, shift, 1))

    def pair(p, carry):
        r0 = pl.multiple_of(p * 8, 8)
        rows = qg_ref[pl.ds(r0, 8), :]
        q8 = rows[:, 0:HALF]
        gate8 = rows[:, HALF:2 * HALF]
        out8 = []
        for sub in range(2):
            b = 2 * p + sub
            q_swap = pltpu.roll(q8, 4, 0)
            q_dup = jnp.where(lower_w, q8, q_swap) if sub == 0 else jnp.where(lower_w, q_swap, q8)
            tile = (seq0 + b) // per_tile
            shift = (2 * WINDOW - DEC_SEQ - DEC_SEQ * ((seq0 + b) % per_tile)) % WINDOW
            k_old = ck_ref[b].reshape(kv_w, WINDOW)
            v_old = cv_ref[b].reshape(kv_w, WINDOW)
            k_win = slide(k_old, ktn_ref[tile], shift)
            v_win = slide(v_old, vtn_ref[tile], shift)
            nk_ref[b] = k_win.reshape(N_KV, HEAD_DIM, WINDOW)
            nv_ref[b] = v_win.reshape(N_KV, HEAD_DIM, WINDOW)
            k_all = jnp.concatenate([k_old, k_win], axis=1).astype(BF16)
            v_all = jnp.concatenate([v_old, v_win], axis=1).astype(BF16)
            q_bd = jnp.concatenate(
                [q_dup[:, g * kv_w:(g + 1) * kv_w] * pick[hp] for g in range(GROUP) for hp in range(2)], axis=0)
            s = _dot(q_bd.astype(BF16), k_all) + bias_ref[...]
            o = _attn_softmax_pv(s, sink_ref[:, 0:1], v_all, v_transposed=True)
            out_g = []
            for g in range(GROUP):
                acc = None
                for hp in range(2):
                    piece = o[(2 * g + hp) * 8:(2 * g + hp + 1) * 8, :] * pick[hp]
                    piece = piece + pltpu.roll(piece, 4, 0)
                    acc = piece if acc is None else acc + piece
                out_g.append(acc)
            out8.append(jnp.concatenate(
                [out_g[g][:, hk * HEAD_DIM:(hk + 1) * HEAD_DIM] for hk in range(N_KV) for g in range(GROUP)], axis=1))
        o8 = jnp.where(lower_w, out8[0], out8[1])
        yb_ref[pl.ds(r0, 8), :] = (o8 * _silu(gate8)).astype(BF16)
        return carry

    lax.fori_loop(0, ATTN_S_BATCH // 2, pair, 0)


def _sample_bias(rel_bias, sinks):
    t = jnp.arange(DEC_SEQ)[:, None]
    j = jnp.arange(KEYS_PAD)[None, :]
    pos = jnp.where(j < WINDOW, j, j - (KEYS_PAD - DEC_SEQ) + WINDOW)
    dist = t + WINDOW - pos
    valid = (dist >= 0) & (dist < WINDOW) & ((j < WINDOW) | (j >= KEYS_PAD - DEC_SEQ))
    bias = jnp.where(valid[:, :, None], rel_bias.astype(F32)[_rel_bucket(dist)], MASK_VALUE)
    bias = bias.reshape(DEC_SEQ, KEYS_PAD, N_KV, GROUP).transpose(3, 2, 0, 1).reshape(N_HEADS * DEC_SEQ, KEYS_PAD)
    sink = jnp.broadcast_to(sinks.astype(F32).reshape(N_KV, GROUP).T[:, :, None], (GROUP, N_KV, DEC_SEQ))
    return bias, jnp.broadcast_to(sink.reshape(N_HEADS * DEC_SEQ, 1), (N_HEADS * DEC_SEQ, 128))


def _attn_sample(qg, kt_new, vt_new, cache_kt, cache_vt, bias, sink):
    n_seq = cache_kt.shape[0]
    bb = ATTN_S_BATCH
    cache_spec = pl.BlockSpec((bb, N_KV, HEAD_DIM, WINDOW), lambda i: (i, 0, 0, 0))
    return pl.pallas_call(
        _attn_sample_kernel,
        grid=(n_seq // bb,),
        in_specs=[_rows(bb * DEC_SEQ, 2 * HALF), _resident(kt_new.shape), _resident(vt_new.shape),
                  cache_spec, cache_spec, _resident(bias.shape), _resident(sink.shape)],
        out_specs=[_rows(bb * DEC_SEQ, HALF), cache_spec, cache_spec],
        out_shape=[jax.ShapeDtypeStruct((n_seq * DEC_SEQ, HALF), BF16),
                   jax.ShapeDtypeStruct(cache_kt.shape, F32), jax.ShapeDtypeStruct(cache_vt.shape, F32)],
        compiler_params=_params(),
        name="attn_sample",
    )(qg, kt_new, vt_new, cache_kt, cache_vt, bias, sink)


def _prep_layer0(g_pre, w_in, conv_w, rel_bias, sinks, w_out, g_post):
    return dict(
        g_pre=g_pre.reshape(1, D_MODEL), w0=w_in.astype(BF16), conv_w=conv_w, rel_bias=rel_bias, sinks=sinks,
        w_out=w_out.astype(BF16), g_post=g_post.reshape(1, D_MODEL))


def _layer0_prompt(x, w):
    ya, s_tail = _conv_a(x, w['g_pre'], w['w0'], w['conv_w'])
    yb, kwin, vwin = _attn_prompt(x, w['g_pre'], w['w0'], _prompt_bias_table(w['rel_bias']), w['sinks'])
    y = _out_proj(ya, yb, x, w['w_out'], w['g_post'])
    return (y, s_tail[6:8], kwin.reshape(WINDOW, N_KV, HEAD_DIM), vwin.reshape(WINDOW, N_KV, HEAD_DIM))


def _layer0_sample(x, conv_state, cache_k, cache_v, w):
    n_seq = x.shape[0]
    rows = x.reshape(n_seq * DEC_SEQ, D_MODEL)
    ya, s = _conv_a(rows, w['g_pre'], w['w0'], w['conv_w'], conv_state.reshape(n_seq, 2 * HALF))
    qg, kt_new, vt_new = _attn_proj(rows, w['g_pre'], w['w0'])
    bias, sink = _sample_bias(w['rel_bias'], w['sinks'])
    yb, new_kt, new_vt = _attn_sample(qg, kt_new, vt_new, cache_k.transpose(0, 2, 3, 1), cache_v.transpose(0, 2, 3, 1),
                                      bias, sink)
    y = _out_proj(ya, yb, rows, w['w_out'], w['g_post'])
    return (y.reshape(n_seq, DEC_SEQ, D_MODEL), s.reshape(n_seq, DEC_SEQ, HALF)[:, DEC_SEQ - 2:],
            new_kt.transpose(0, 3, 1, 2), new_vt.transpose(0, 3, 1, 2))


def _layer_norm(v, g, b):
    xc = v - jnp.mean(v, axis=-1, keepdims=True)
    return xc * lax.rsqrt(jnp.mean(xc * xc, axis=-1, keepdims=True) + NORM_EPS) * g + b


def _cmlp_prompt_kernel(x_ref, g_ref, w_ref, lng_ref, lnb_ref, ws_ref, bs_ref, yc_ref, vn_scr):
    tile = x_ref.shape[0]
    h = _rms_bf16(x_ref[...], g_ref[...])
    v = _dot(h, w_ref[:, HALF:2 * HALF])
    vn_scr[...] = _layer_norm(v, lng_ref[...], lnb_ref[...]).astype(BF16)
    gw = HALF // CMLP_GROUPS
    cols = 2 * gw
    for cb in range(HALF // cols):
        u = _dot(h, w_ref[:, cb * cols:(cb + 1) * cols])
        gate = _silu(_dot(h, w_ref[:, 2 * HALF + cb * cols:2 * HALF + (cb + 1) * cols]))
        for gi in range(2):
            grp = 2 * cb + gi
            lanes = slice(grp * gw, (grp + 1) * gw)
            for n in range(tile // CHUNK):
                rows = slice(n * CHUNK, (n + 1) * CHUNK)
                mixed = _dot(ws_ref[grp], vn_scr[rows, lanes]) + bs_ref[grp]
                yc_ref[rows, lanes] = (u[rows, gi * gw:(gi + 1) * gw] * mixed
                                       * gate[rows, gi * gw:(gi + 1) * gw]).astype(BF16)


def _cmlp_prompt(x, g_pre, w_c, ln_g, ln_b, ws_tril, bs_rows):
    t = x.shape[0]
    tile = min(ROW_TILE, t)
    return pl.pallas_call(
        _cmlp_prompt_kernel,
        grid=(t // tile,),
        in_specs=[_rows(tile, D_MODEL), _resident((1, D_MODEL)), _cols(D_MODEL, 3 * HALF, 0),
                  _resident((1, HALF)), _resident((1, HALF)), _resident(ws_tril.shape), _resident(bs_rows.shape)],
        out_specs=_rows(tile, HALF),
        out_shape=jax.ShapeDtypeStruct((t, HALF), BF16),
        scratch_shapes=[pltpu.VMEM((tile, HALF), BF16)],
        compiler_params=_params(),
        name="cmlp_prompt",
    )(x, g_pre, w_c, ln_g, ln_b, ws_tril, bs_rows)


def _cmlp_sample_kernel(x_ref, g_ref, w_ref, lng_ref, lnb_ref, coef_ref, bias_ref, yc_ref, vn_ref):
    t = x_ref.shape[0]
    h = _rms_bf16(x_ref[...], g_ref[...])
    u = _dot(h, w_ref[:, 0:HALF])
    vn = _layer_norm(_dot(h, w_ref[:, HALF:2 * HALF]), lng_ref[...], lnb_ref[...])
    gate = _silu(_dot(h, w_ref[:, 2 * HALF:3 * HALF]))
    vn_ref[...] = vn

    def tiled(a):
        return a.reshape(t // 8, 8, HALF)

    mixed = tiled(vn) * coef_ref[0][None] + bias_ref[...][None]
    for k in range(1, DEC_SEQ):
        mixed = mixed + tiled(pltpu.roll(vn, k, 0)) * coef_ref[k][None]
    yc_ref[...] = (u * mixed.reshape(t, HALF) * gate).astype(BF16)


def _cmlp_sample(x, g_pre, w_c, ln_g, ln_b, coef, bias):
    t = x.shape[0]
    return pl.pallas_call(
        _cmlp_sample_kernel,
        grid=(1,),
        in_specs=[_resident((t, D_MODEL)), _resident((1, D_MODEL)), _cols(D_MODEL, 3 * HALF, 0),
                  _resident((1, HALF)), _resident((1, HALF)), _resident(coef.shape), _resident(bias.shape)],
        out_specs=[_resident((t, HALF)), _resident((t, HALF))],
        out_shape=[jax.ShapeDtypeStruct((t, HALF), BF16), jax.ShapeDtypeStruct((t, HALF), F32)],
        compiler_params=_params(),
        name="cmlp_sample",
    )(x, g_pre, w_c, ln_g, ln_b, coef, bias)


HEAD_LANES = 128
SSD_GW = HALF // SSD_GROUPS


def _softplus(x):
    return jnp.maximum(x, 0.0) + jnp.log1p(jnp.exp(-jnp.abs(x)))


def _group_norm_gate(y, z, gn):
    gated = y * _silu(z)
    parts = []
    for g in range(SSD_GROUPS):
        part = gated[:, g * SSD_GW:(g + 1) * SSD_GW]
        parts.append(part * lax.rsqrt(jnp.mean(part * part, axis=-1, keepdims=True) + NORM_EPS))
    return (jnp.concatenate(parts, axis=1) * gn).astype(BF16)


def _ssd_prompt_kernel(x_ref, g_ref, wz_ref, wx0_ref, wx1_ref, wx2_ref, wdt_ref, cw_ref, cb_ref, dtb_ref, alog_ref,
                       aloge_ref, dskip_ref, gn_ref, e3_ref, tril3_ref, yd_ref, tail_ref, ssm_ref,
                       xbc_scr, z_scr, dt_scr, ht_scr):
    tile = x_ref.shape[0]
    i = pl.program_id(0)
    cd = SSD_CONV_DIM
    h = _rms_bf16(x_ref[...], g_ref[...])

    @pl.when(i == 0)
    def _():
        tail_ref[...] = jnp.zeros_like(tail_ref)
        ht_scr[...] = jnp.zeros_like(ht_scr)

    z_scr[...] = _dot(h, wz_ref[...])
    raw = jnp.concatenate([_dot(h, wx0_ref[...]), _dot(h, wx1_ref[...]), _dot(h, wx2_ref[...])], axis=1)
    dt_scr[...] = _softplus(_dot(h, wdt_ref[...]) + dtb_ref[...])
    prev = tail_ref[...]
    conv = raw * cw_ref[3:4, :] + cb_ref[...]
    for k in range(1, 4):
        conv = conv + _shift_rows(raw, k, prev) * cw_ref[3 - k:4 - k, :]
    xbc_scr[...] = _silu(conv)
    tail_ref[...] = raw[tile - 8:tile, :]

    a16 = -jnp.exp(alog_ref[...])
    a_e = -jnp.exp(aloge_ref[...])
    causal = (lax.broadcasted_iota(jnp.int32, (CHUNK, CHUNK), 0)
              >= lax.broadcasted_iota(jnp.int32, (CHUNK, CHUNK), 1))
    first_half = lax.broadcasted_iota(jnp.int32, (CHUNK, 2 * HEAD_DIM), 1) < HEAD_DIM
    keep_a = jnp.where(first_half, 1.0, 0.0).astype(BF16)
    keep_b = jnp.where(first_half, 0.0, 1.0).astype(BF16)

    def chunk(n, carry):
        r0 = pl.multiple_of(n * CHUNK, CHUNK)
        rows = pl.ds(r0, CHUNK)
        xs = xbc_scr[rows, 0:HALF]
        dt16 = dt_scr[rows, :]
        dt_e = _dot(jnp.concatenate(_split3(dt16), axis=1), e3_ref[...])
        da_e = dt_e * a_e
        acs_e = _dot(tril3_ref[...], jnp.concatenate(_split3(da_e), axis=0))
        acs16 = _dot(tril3_ref[...], jnp.concatenate(_split3(dt16 * a16), axis=0))
        acs_t = acs16.T
        last_e = acs_e[CHUNK - 1:CHUNK, :]
        xdt = xs * dt_e
        xdt_bf = xdt.astype(BF16)
        xw = (jnp.exp(last_e - acs_e) * xdt).astype(BF16)
        dec_e = jnp.exp(last_e)
        y_parts = []
        yoff_parts = []
        for g in range(SSD_GROUPS):
            c_g = xbc_scr[rows, HALF + 2 * SSD_STATE + g * SSD_STATE:HALF + 2 * SSD_STATE + (g + 1) * SSD_STATE].astype(BF16)
            b_g = xbc_scr[rows, HALF + g * SSD_STATE:HALF + (g + 1) * SSD_STATE].astype(BF16)
            cb = _dot_nt(c_g, b_g)
            h_prev = ht_scr[g]
            yoff_parts.append(_dot(c_g, h_prev.astype(BF16)))
            for r in range(0, SSD_HEADS // SSD_GROUPS, 2):
                wgt = []
                for hd in (g * (SSD_HEADS // SSD_GROUPS) + r, g * (SSD_HEADS // SSD_GROUPS) + r + 1):
                    seg = acs16[:, hd:hd + 1] - acs_t[hd:hd + 1, :]
                    wgt.append(cb * jnp.exp(jnp.where(causal, seg, -jnp.inf)))
                a = g * (SSD_HEADS // SSD_GROUPS) + r
                slab = xdt_bf[:, a * HEAD_DIM:(a + 2) * HEAD_DIM]
                rhs = jnp.concatenate([slab * keep_a, slab * keep_b], axis=0)
                y_parts.append(_dot(jnp.concatenate(wgt, axis=1).astype(BF16), rhs))
            lanes = slice(g * SSD_GW, (g + 1) * SSD_GW)
            ht_scr[g] = h_prev * dec_e[:, lanes] + _dot_tn(b_g, xw[:, lanes])
        y = (jnp.concatenate(y_parts, axis=1) + jnp.concatenate(yoff_parts, axis=1) * jnp.exp(acs_e)
             + dskip_ref[...] * xs)
        yd_ref[rows, :] = _group_norm_gate(y, z_scr[rows, :], gn_ref[...])
        return carry

    lax.fori_loop(0, tile // CHUNK, chunk, 0)

    @pl.when(i == pl.num_programs(0) - 1)
    def _():
        for g in range(SSD_GROUPS):
            ssm_ref[g * SSD_GW:(g + 1) * SSD_GW, :] = ht_scr[g].T


def _ssd_weight_specs():
    third = SSD_CONV_DIM // 3
    first = 4 * HALF // third
    return ([_cols(D_MODEL, HALF, 3)] + [_cols(D_MODEL, third, first + j) for j in range(3)]
            + [_resident((D_MODEL, HEAD_LANES))])


def _ssd_prompt(x, w):
    t = x.shape[0]
    tile = min(ROW_TILE, t)
    cd = SSD_CONV_DIM
    consts = [w['conv_w'], w['conv_b'], w['dt_bias16'], w['a_log16'], w['a_log_e'],
              w['d_skip_e'], w['gate_norm_g'], w['expand3'], w['tril3']]
    return pl.pallas_call(
        _ssd_prompt_kernel,
        grid=(t // tile,),
        in_specs=[_rows(tile, D_MODEL), _resident((1, D_MODEL))] + _ssd_weight_specs()
                 + [_resident(c.shape) for c in consts],
        out_specs=[_rows(tile, HALF), pl.BlockSpec((8, cd), lambda i: (0, 0)),
                   pl.BlockSpec((HALF, SSD_STATE), lambda i: (0, 0))],
        out_shape=[jax.ShapeDtypeStruct((t, HALF), BF16), jax.ShapeDtypeStruct((8, cd), F32),
                   jax.ShapeDtypeStruct((HALF, SSD_STATE), F32)],
        scratch_shapes=[pltpu.VMEM((tile, cd), F32), pltpu.VMEM((tile, HALF), F32),
                        pltpu.VMEM((tile, HEAD_LANES), F32), pltpu.VMEM((SSD_GROUPS, SSD_STATE, SSD_GW), F32)],
        compiler_params=_params(),
        name="ssd_prompt",
    )(x, w['g_pre'], w['w1'], w['w1'], w['w1'], w['w1'], w['w_dt'], *consts)


def _ssd_sample_pre_kernel(x_ref, g_ref, wz_ref, wx0_ref, wx1_ref, wx2_ref, wdt_ref, cw_ref, cb_ref, st_ref,
                           dtb_ref, aloge_ref, dskip_ref, e3_ref, seg_ref,
                           nconv_ref, z_ref, ysk_ref, eacs_ref, xw_ref, dec_ref, b_ref, c_ref, raw_scr):
    t = x_ref.shape[0]
    n_seq = t // DEC_SEQ
    h = _rms_bf16(x_ref[...], g_ref[...])
    z_ref[...] = _dot(h, wz_ref[...])
    raw = jnp.concatenate([_dot(h, wx0_ref[...]), _dot(h, wx1_ref[...]), _dot(h, wx2_ref[...])], axis=1)
    for c in range(raw_scr.shape[0]):
        lanes = slice(c * 128, (c + 1) * 128)
        raw_scr[c] = raw[:, lanes]
        for j in range(3):
            nconv_ref[j, :, lanes] = raw_scr[c, pl.ds(j + 1, n_seq, stride=DEC_SEQ), :]
    dt16 = _softplus(_dot(h, wdt_ref[...]) + dtb_ref[...])
    dt = _dot(jnp.concatenate(_split3(dt16), axis=1), e3_ref[...])
    old = [st_ref[j] for j in range(3)]
    p1 = _place_steps(t, [(0, old[2])])
    p2 = _place_steps(t, [(0, old[1]), (1, old[2])])
    p3 = _place_steps(t, [(0, old[0]), (1, old[1]), (2, old[2])])

    def step_of(width):
        return lax.broadcasted_iota(jnp.int32, (t, width), 0) % DEC_SEQ

    def back(a, k):
        return jnp.where(step_of(a.shape[1]) >= k, pltpu.roll(a, k, 0), 0.0)

    def ahead(a, k):
        return jnp.where(step_of(a.shape[1]) + k < DEC_SEQ, pltpu.roll(a, t - k, 0), 0.0)

    conv = (raw * cw_ref[3:4, :] + (back(raw, 1) + p1) * cw_ref[2:3, :]
            + (back(raw, 2) + p2) * cw_ref[1:2, :] + (back(raw, 3) + p3) * cw_ref[0:1, :]
            + cb_ref[...])
    xbc = _silu(conv)
    xs = xbc[:, 0:HALF]
    bm = xbc[:, HALF:HALF + 2 * SSD_STATE]
    cm = xbc[:, HALF + 2 * SSD_STATE:]
    b_ref[...] = bm
    c_ref[...] = cm
    da = dt * (-jnp.exp(aloge_ref[...]))
    acs = da + back(da, 1) + back(da, 2) + back(da, 3)
    suffix = ahead(da, 1) + ahead(da, 2) + ahead(da, 3)
    xdt = xs * dt
    y = _dot((cm * bm).astype(BF16), seg_ref[...]) * xdt
    for k in range(1, DEC_SEQ):
        cbk = _dot((cm * pltpu.roll(bm, k, 0)).astype(BF16), seg_ref[...])
        term = cbk * jnp.exp(acs - pltpu.roll(acs, k, 0)) * pltpu.roll(xdt, k, 0)
        y = y + jnp.where(step_of(HALF) >= k, term, 0.0)
    ysk_ref[...] = y + dskip_ref[...] * xs
    eacs_ref[...] = jnp.exp(acs)
    xw_ref[...] = jnp.exp(suffix) * xdt
    dec_ref[...] = jnp.exp(acs + suffix)


def _ssd_sample_pre(x, conv_state, w):
    t = x.shape[0]
    cd = SSD_CONV_DIM
    consts = [w['conv_w'], w['conv_b'], conv_state, w['dt_bias16'], w['a_log_e'], w['d_skip_e'], w['expand3'],
              w['seg_expand']]
    args = [x, w['g_pre'], w['w1'], w['w1'], w['w1'], w['w1'], w['w_dt']] + consts
    wide = jax.ShapeDtypeStruct((t, HALF), F32)
    narrow = jax.ShapeDtypeStruct((t, 2 * SSD_STATE), F32)
    out_shape = [jax.ShapeDtypeStruct(conv_state.shape, F32), wide, wide, wide, wide, wide, narrow, narrow]
    return pl.pallas_call(
        _ssd_sample_pre_kernel,
        grid=(1,),
        in_specs=[_resident((t, D_MODEL)), _resident((1, D_MODEL))] + _ssd_weight_specs()
                 + [_resident(c.shape) for c in consts],
        out_specs=[_resident(s.shape) for s in out_shape],
        out_shape=out_shape,
        scratch_shapes=[pltpu.VMEM((cd // 128, t, 128), F32)],
        compiler_params=_params(),
        name="ssd_sample_pre",
    )(*args)


SSD_S_BATCH = 8


def _ssd_sample_state_kernel(st_ref, c_ref, b_ref, xw_ref, dec_ref, eacs_ref, ysk_ref, z_ref, gn_ref,
                             yd_ref, nst_ref):
    row_n = lax.broadcasted_iota(jnp.int32, (8, SSD_STATE), 0)
    row_w = lax.broadcasted_iota(jnp.int32, (8, SSD_GW), 0)
    row_f = lax.broadcasted_iota(jnp.int32, (8, HALF), 0)
    ones_rows = jnp.where((row_n >= 4) & (row_n < 7), 1.0, 0.0).astype(BF16)
    hpg = SSD_HEADS // SSD_GROUPS

    def pair(p, carry):
        r0 = pl.multiple_of(p * 8, 8)
        rows = pl.ds(r0, 8)
        c8 = c_ref[rows, :].astype(BF16)
        b8 = b_ref[rows, :]
        xw8 = xw_ref[rows, :]
        dec8 = dec_ref[rows, :]
        yoff = []
        for sub in range(2):
            b = 2 * p + sub
            xw_own = xw8 if sub == 0 else pltpu.roll(xw8, 4, 0)
            b_own = b8 if sub == 0 else pltpu.roll(b8, 4, 0)
            hi, mid, lo = (term.astype(F32) for term in _split3(dec8[4 * sub:4 * sub + 1, :]))
            parts = []
            for g in range(SSD_GROUPS):
                lanes = slice(g * SSD_GW, (g + 1) * SSD_GW)
                heads = pl.ds(g * hpg, hpg)
                h0 = st_ref[b, heads].reshape(SSD_GW, SSD_STATE)
                parts.append(_dot_nt(c8[:, g * SSD_STATE:(g + 1) * SSD_STATE], h0.astype(BF16)))
                lhs = jnp.where(row_w < 4, xw_own[:, lanes],
                                jnp.where(row_w == 4, hi[:, lanes],
                                          jnp.where(row_w == 5, mid[:, lanes],
                                                    jnp.where(row_w == 6, lo[:, lanes], 0.0)))).astype(BF16)
                rhs_b = jnp.where(row_n < 4, b_own[:, g * SSD_STATE:(g + 1) * SSD_STATE], 0.0).astype(BF16)
                decay = _dot_tn(lhs, ones_rows)
                nst_ref[b, heads] = (h0 * decay + _dot_tn(lhs, rhs_b)).reshape(hpg, HEAD_DIM, SSD_STATE)
            yoff.append(jnp.concatenate(parts, axis=1))
        yoff8 = jnp.where(row_f < 4, yoff[0], yoff[1])
        y = ysk_ref[rows, :] + yoff8 * eacs_ref[rows, :]
        yd_ref[rows, :] = _group_norm_gate(y, z_ref[rows, :], gn_ref[...])
        return carry

    lax.fori_loop(0, SSD_S_BATCH // 2, pair, 0)


def _ssd_sample_state(state, cm, bm, xw, dec, eacs, ysk, z, gn):
    n_seq = state.shape[0]
    bb = SSD_S_BATCH
    r = bb * DEC_SEQ
    st_spec = pl.BlockSpec((bb, SSD_HEADS, HEAD_DIM, SSD_STATE), lambda i: (i, 0, 0, 0))
    return pl.pallas_call(
        _ssd_sample_state_kernel,
        grid=(n_seq // bb,),
        in_specs=[st_spec, _rows(r, 2 * SSD_STATE), _rows(r, 2 * SSD_STATE)] + [_rows(r, HALF)] * 5
                 + [_resident((1, HALF))],
        out_specs=[_rows(r, HALF), st_spec],
        out_shape=[jax.ShapeDtypeStruct((n_seq * DEC_SEQ, HALF), BF16), jax.ShapeDtypeStruct(state.shape, F32)],
        compiler_params=_params(),
        name="ssd_sample_state",
    )(state, cm, bm, xw, dec, eacs, ysk, z, gn)


def _prep_layer1(g_pre, w_in, ln_g, ln_b, w_s, b_s, conv_w, conv_b, dt_bias, a_log, d_skip, gate_norm_g,
                 w_out, g_post):
    cd = SSD_CONV_DIM
    gw = HALF // CMLP_GROUPS
    w1 = w_in.astype(BF16)
    w_dt = jnp.pad(w_in[:, 4 * HALF + cd:], ((0, 0), (0, HEAD_LANES - SSD_HEADS))).astype(BF16)

    def lanes16(v):
        return jnp.pad(v.astype(F32), (0, HEAD_LANES - SSD_HEADS)).reshape(1, HEAD_LANES)

    def per_channel(v):
        return jnp.repeat(v.astype(F32), HEAD_DIM).reshape(1, HALF)

    head_of = jnp.arange(HALF) // HEAD_DIM
    expand = (jnp.arange(HEAD_LANES)[:, None] == head_of[None, :]).astype(BF16)
    tril = jnp.tril(jnp.ones((CHUNK, CHUNK), BF16))
    grp_rows = jnp.arange(2 * SSD_STATE) // SSD_STATE
    seg_expand = (grp_rows[:, None] == (head_of // (SSD_HEADS // SSD_GROUPS))[None, :]).astype(BF16)

    w4 = jnp.tril(w_s[:, :DEC_SEQ, :DEC_SEQ])
    steps = jnp.arange(DEC_SEQ)
    coef = []
    for k in range(DEC_SEQ):
        src = steps - k
        ck = jnp.where((src >= 0)[None, :], w4[:, steps, jnp.maximum(src, 0)], 0.0)
        ck = jnp.repeat(ck.T, gw, axis=1)
        coef.append(jnp.concatenate([ck, ck], axis=0))
    bias4 = jnp.repeat(b_s[:, :DEC_SEQ].T, gw, axis=1)
    return dict(
        g_pre=g_pre.reshape(1, D_MODEL), w1=w1, w_dt=w_dt, ln_g=ln_g.reshape(1, HALF), ln_b=ln_b.reshape(1, HALF),
        ws_tril=jnp.tril(w_s).astype(BF16),
        bs_rows=jnp.broadcast_to(b_s.astype(F32)[:, :, None], (CMLP_GROUPS, CHUNK, gw)),
        coef=jnp.stack(coef).astype(F32), bias4=jnp.concatenate([bias4, bias4], axis=0).astype(F32),
        conv_w=conv_w, conv_b=conv_b.reshape(1, cd), dt_bias16=lanes16(dt_bias), a_log16=lanes16(a_log),
        a_log_e=per_channel(a_log), d_skip_e=per_channel(d_skip), gate_norm_g=gate_norm_g.reshape(1, HALF),
        expand3=jnp.concatenate([expand] * 3, axis=0), tril3=jnp.concatenate([tril] * 3, axis=1),
        seg_expand=seg_expand, w_out=w_out.astype(BF16), g_post=g_post.reshape(1, D_MODEL))


def _layer1_prompt(x, w):
    yc = _cmlp_prompt(x, w['g_pre'], w['w1'], w['ln_g'], w['ln_b'], w['ws_tril'], w['bs_rows'])
    yd, tail, ssm = _ssd_prompt(x, w)
    y = _out_proj(yc, yd, x, w['w_out'], w['g_post'])
    return y, tail[5:8], ssm.reshape(SSD_HEADS, HEAD_DIM, SSD_STATE)


def _layer1_sample(x, conv_state, ssm_state, w):
    n_seq = x.shape[0]
    t = n_seq * DEC_SEQ
    cd = SSD_CONV_DIM
    rows = x.reshape(t, D_MODEL)
    yc, vn = _cmlp_sample(rows, w['g_pre'], w['w1'], w['ln_g'], w['ln_b'], w['coef'], w['bias4'])
    new_conv, z, ysk, eacs, xw, dec, bm, cm = _ssd_sample_pre(rows, conv_state.transpose(1, 0, 2), w)
    yd, new_state = _ssd_sample_state(ssm_state, cm, bm, xw, dec, eacs, ysk, z, w['gate_norm_g'])
    y = _out_proj(yc, yd, rows, w['w_out'], w['g_post'])
    return (y.reshape(n_seq, DEC_SEQ, D_MODEL), vn.reshape(n_seq, DEC_SEQ, HALF),
            new_conv.transpose(1, 0, 2), new_state)


def kernel(x_prompt, x_sample, state_conv_a, cache_win_k, cache_win_v, state_conv_d, state_ssm, rel_bias,
           l0_g_pre, l0_w_in, l0_conv_w, l0_sinks, l0_w_out, l0_g_post,
           l1_g_pre, l1_w_in, l1_ln_g, l1_ln_b, l1_w_s, l1_b_s, l1_conv_w, l1_conv_b, l1_dt_bias, l1_a_log,
           l1_d_skip, l1_gate_norm_g, l1_w_out, l1_g_post):
    w0 = _prep_layer0(l0_g_pre, l0_w_in, l0_conv_w, rel_bias, l0_sinks, l0_w_out, l0_g_post)
    w1 = _prep_layer1(l1_g_pre, l1_w_in, l1_ln_g, l1_ln_b, l1_w_s, l1_b_s, l1_conv_w, l1_conv_b, l1_dt_bias,
                      l1_a_log, l1_d_skip, l1_gate_norm_g, l1_w_out, l1_g_post)
    yp, p_conv_a, p_win_k, p_win_v = _layer0_prompt(x_prompt[0], w0)
    ys, s_conv_a, s_win_k, s_win_v = _layer0_sample(x_sample, state_conv_a, cache_win_k, cache_win_v, w0)
    yp, p_conv_d, p_ssm = _layer1_prompt(yp, w1)
    ys, s_chunk_v, s_conv_d, s_ssm = _layer1_sample(ys, state_conv_d, state_ssm, w1)
    return (yp[None], ys, p_conv_a[None], s_conv_a, p_win_k[None], p_win_v[None], s_win_k, s_win_v, s_chunk_v,
            p_conv_d[None], s_conv_d, p_ssm[None], s_ssm)
```

```python
import functools
import math

import jax
import jax.numpy as jnp
from jax import lax
from jax.experimental import pallas as pl
from jax.experimental.pallas import tpu as pltpu

F32 = jnp.float32
BF16 = jnp.bfloat16

D_MODEL = 2048
HALF = 1024
HEAD_DIM = 64
N_HEADS = 16
N_KV = 4
GROUP = 4
WINDOW = 128
NUM_BUCKETS = 32
MAX_DISTANCE = 128
CMLP_GROUPS = 8
CHUNK = 128
SSD_HEADS = 16
SSD_STATE = 128
SSD_GROUPS = 2
SSD_CONV_DIM = HALF + 2 * SSD_GROUPS * SSD_STATE
DEC_SEQ = 4
NORM_EPS = 1e-6
MASK_VALUE = -1e30

ROW_TILE = 512
VMEM_LIMIT = 56 * 1024 * 1024


def _params(n_axes=1):
    return pltpu.CompilerParams(dimension_semantics=("arbitrary",) * n_axes,
                                vmem_limit_bytes=VMEM_LIMIT)


def _resident(shape):
    nd = len(shape)
    return pl.BlockSpec(shape, lambda *_: (0,) * nd, pipeline_mode=pl.Buffered(1))


def _rows(tile, width):
    return pl.BlockSpec((tile, width), lambda i: (i, 0))


def _cols(rows, width, block):
    return pl.BlockSpec((rows, width), lambda *_: (0, block), pipeline_mode=pl.Buffered(1))


def _rowwin(height, cols, block):
    return pl.BlockSpec((height, cols), lambda *_: (block, 0), pipeline_mode=pl.Buffered(1))


def _rms_bf16(x, g):
    ms = jnp.mean(x * x, axis=-1, keepdims=True)
    return (x * lax.rsqrt(ms + NORM_EPS) * g).astype(BF16)


def _silu(x):
    return x * jax.nn.sigmoid(x)


def _dot(a, b):
    return jnp.dot(a, b, preferred_element_type=F32)


def _dot_nt(a, b):
    return lax.dot_general(a, b, (((1,), (1,)), ((), ())), preferred_element_type=F32)


def _dot_tn(a, b):
    return lax.dot_general(a, b, (((0,), (0,)), ((), ())), preferred_element_type=F32)


def _dot_w(a, w):
    return _dot(a, w.astype(BF16))


def _dot_wt(a, wt):
    return _dot_nt(a, wt.astype(BF16))


def _split3(x):
    hi = x.astype(BF16)
    r1 = x - hi.astype(F32)
    mid = r1.astype(BF16)
    lo = (r1 - mid.astype(F32)).astype(BF16)
    return hi, mid, lo


def _place_steps(t, placements):
    n_seq = placements[0][1].shape[0]
    row = lax.broadcasted_iota(jnp.int32, (t, n_seq), 0)
    seq = lax.broadcasted_iota(jnp.int32, (t, n_seq), 1)
    lhs, rhs = [], []
    for step, state in placements:
        sel = jnp.where(row == DEC_SEQ * seq + step, 1.0, 0.0).astype(BF16)
        lhs += [sel] * 3
        rhs += list(_split3(state))
    return _dot(jnp.concatenate(lhs, axis=1), jnp.concatenate(rhs, axis=0))


def _shift_rows(x, k, prev_rows=None):
    r = pltpu.roll(x, k, 0)
    if prev_rows is None:
        return r
    head = r[0:8, :]
    row = lax.broadcasted_iota(jnp.int32, head.shape, 0)
    n_prev = prev_rows.shape[0]
    for t in range(k):
        src = n_prev - k + t
        head = jnp.where(row == t, prev_rows[src:src + 1, :], head)
    return jnp.concatenate([head, r[8:, :]], axis=0)


def _out_proj_kernel(ya_ref, yb_ref, x_ref, w_ref, g_ref, o_ref):
    y = _dot_w(ya_ref[...], w_ref[0:HALF, :]) + _dot_w(yb_ref[...], w_ref[HALF:2 * HALF, :])
    ms = jnp.mean(y * y, axis=-1, keepdims=True)
    o_ref[...] = x_ref[...] + y * lax.rsqrt(ms + NORM_EPS) * g_ref[...]


def _out_proj(ya, yb, x, w_bf, g):
    t = x.shape[0]
    tile = min(ROW_TILE, t)
    return pl.pallas_call(
        _out_proj_kernel,
        grid=(t // tile,),
        in_specs=[_rows(tile, HALF), _rows(tile, HALF), _rows(tile, D_MODEL),
                  _resident((2 * HALF, D_MODEL)), _resident((1, D_MODEL))],
        out_specs=_rows(tile, D_MODEL),
        out_shape=jax.ShapeDtypeStruct((t, D_MODEL), F32),
        compiler_params=_params(),
        name="out_proj",
    )(ya, yb, x, w_bf, g)


CONV_A_CHUNK = 256


def _conv_a_kernel(*refs, sample):
    if sample:
        x_ref, g_ref, w_ref, cw_ref, st_ref, ya_ref, s_ref = refs
    else:
        x_ref, g_ref, w_ref, cw_ref, ya_ref, s_ref = refs
    tile = x_ref.shape[0]
    cc = CONV_A_CHUNK
    h = _rms_bf16(x_ref[...], g_ref[...])
    if not sample:
        @pl.when(pl.program_id(0) == 0)
        def _():
            s_ref[...] = jnp.zeros_like(s_ref)
    for c in range(HALF // cc):
        lanes = slice(c * cc, (c + 1) * cc)
        a_b, a_c, a_h, a_g = (_dot_w(h, w_ref[:, j * HALF + c * cc:j * HALF + (c + 1) * cc]) for j in range(4))
        s = a_c * a_h
        if sample:
            t_in = lax.broadcasted_iota(jnp.int32, s.shape, 0) % DEC_SEQ
            old0 = st_ref[:, c * cc:(c + 1) * cc]
            old1 = st_ref[:, HALF + c * cc:HALF + (c + 1) * cc]
            p1 = jnp.where(t_in >= 1, _shift_rows(s, 1), 0.0) + _place_steps(tile, [(0, old1)])
            p2 = jnp.where(t_in >= 2, _shift_rows(s, 2), 0.0) + _place_steps(tile, [(0, old0), (1, old1)])
            s_ref[:, lanes] = s
        else:
            prev = s_ref[:, lanes]
            p1 = _shift_rows(s, 1, prev)
            p2 = _shift_rows(s, 2, prev)
            s_ref[:, lanes] = s[tile - 8:tile, :]
        conv = p2 * cw_ref[0:1, lanes] + p1 * cw_ref[1:2, lanes] + s * cw_ref[2:3, lanes]
        ya_ref[:, lanes] = (a_b * conv * _silu(a_g)).astype(BF16)


def _conv_a(x, g_pre, w0, conv_w, state=None):
    t = x.shape[0]
    sample = state is not None
    tile = t if sample else min(ROW_TILE, t)
    in_specs = [_rows(tile, D_MODEL), _resident((1, D_MODEL)), _cols(D_MODEL, 4 * HALF, 0), _resident((3, HALF))]
    args = [x, g_pre, w0, conv_w]
    if sample:
        in_specs.append(_resident(state.shape))
        args.append(state)
        s_spec, s_shape = _rows(tile, HALF), (t, HALF)
    else:
        s_spec, s_shape = pl.BlockSpec((8, HALF), lambda i: (0, 0)), (8, HALF)
    return pl.pallas_call(
        functools.partial(_conv_a_kernel, sample=sample),
        grid=(t // tile,),
        in_specs=in_specs,
        out_specs=[_rows(tile, HALF), s_spec],
        out_shape=[jax.ShapeDtypeStruct((t, HALF), BF16), jax.ShapeDtypeStruct(s_shape, F32)],
        compiler_params=_params(),
        name="conv_a_sample" if sample else "conv_a_prompt",
    )(*args)


def _rel_bucket(dist):
    max_exact = NUM_BUCKETS // 2
    d = jnp.maximum(dist, 0)
    ratio = jnp.maximum(d, max_exact).astype(F32) / max_exact
    large = max_exact + (jnp.log(ratio) / math.log(MAX_DISTANCE / max_exact)
                         * (NUM_BUCKETS - max_exact)).astype(jnp.int32)
    return jnp.where(d < max_exact, d, jnp.minimum(large, NUM_BUCKETS - 1))


def _attn_softmax_pv(s, sink, v_bf, v_transposed=False):
    m = jnp.maximum(jnp.max(s, axis=-1, keepdims=True), sink)
    p = jnp.exp(s - m)
    den = jnp.sum(p, axis=-1, keepdims=True) + jnp.exp(sink - m)
    pv = _dot_nt(p.astype(BF16), v_bf) if v_transposed else _dot(p.astype(BF16), v_bf)
    return pv / den


def _attn_prompt_kernel(x_ref, g_ref, wq_ref, wkv_ref, wg0_ref, wg1_ref, tab_ref, sink_ref, yb_ref, kwin_ref,
                        vwin_ref, q_scr, gate_scr, k_scr, v_scr, bias_scr):
    tile = x_ref.shape[0]
    i = pl.program_id(0)
    kv_w = N_KV * HEAD_DIM
    h = _rms_bf16(x_ref[...], g_ref[...])

    kw, vw = 2 * HEAD_DIM, 4 * HEAD_DIM

    @pl.when(i == 0)
    def _():
        k_scr[0:WINDOW, :] = jnp.zeros((WINDOW, N_KV * kw), BF16)
        v_scr[0:WINDOW, :] = jnp.zeros((WINDOW, N_KV * vw), BF16)
        for hk in range(N_KV):
            v_scr[:, hk * vw + kw:(hk + 1) * vw] = jnp.ones((tile + WINDOW, kw), BF16)
        in_own = lax.broadcasted_iota(jnp.int32, (WINDOW, 2 * WINDOW), 1) >= WINDOW
        for head in range(N_HEADS):
            row = jnp.broadcast_to(tab_ref[head:head + 1, :], (WINDOW, BIAS_SPAN))
            band = pltpu.roll(row, 0, 1, stride=1, stride_axis=0)[:, 0:2 * WINDOW]
            rows = slice((head % 2) * WINDOW, (head % 2 + 1) * WINDOW)
            bias_scr[1, head // 2, rows, :] = band
            bias_scr[0, head // 2, rows, :] = jnp.where(in_own, band, MASK_VALUE)

    q_scr[...] = (_dot_w(h, wq_ref[...]) * (HEAD_DIM ** -0.5)).astype(BF16)
    k = _dot_w(h, wkv_ref[:, 0:kv_w])
    v = _dot_w(h, wkv_ref[:, kv_w:2 * kv_w])
    gate_scr[:, 0:HALF // 2] = _silu(_dot_w(h, wg0_ref[...]))
    gate_scr[:, HALF // 2:HALF] = _silu(_dot_w(h, wg1_ref[...]))
    for hk in range(N_KV):
        k_h = k[:, hk * HEAD_DIM:(hk + 1) * HEAD_DIM].astype(BF16)
        v_h = v[:, hk * HEAD_DIM:(hk + 1) * HEAD_DIM].astype(BF16)
        k_scr[WINDOW:WINDOW + tile, hk * kw:(hk + 1) * kw] = jnp.concatenate([k_h, k_h], axis=1)
        v_scr[WINDOW:WINDOW + tile, hk * vw:hk * vw + kw] = jnp.concatenate([v_h, v_h], axis=1)
    kwin_ref[...] = k[tile - WINDOW:tile, :]
    vwin_ref[...] = v[tile - WINDOW:tile, :]

    lane = lax.broadcasted_iota(jnp.int32, (WINDOW, kw), 1)
    lo = lane < HEAD_DIM
    keep_a = jnp.where(lo, 1.0, 0.0).astype(BF16)
    keep_b = jnp.where(lo, 0.0, 1.0).astype(BF16)
    is_a = lax.broadcasted_iota(jnp.int32, (2 * WINDOW, 1), 0) < WINDOW

    def block(n, carry):
        r0 = pl.multiple_of(n * WINDOW, WINDOW)
        rows = pl.ds(r0, WINDOW)
        keys = pl.ds(r0, 2 * WINDOW)
        first = jnp.where(jnp.logical_and(i == 0, n == 0), 0, 1)
        for hk in range(N_KV):
            for gp in range(GROUP // 2):
                a = hk * GROUP + 2 * gp
                slab = slice(a * HEAD_DIM, (a + 2) * HEAD_DIM)
                q2 = q_scr[rows, slab]
                lhs = jnp.concatenate([q2 * keep_a, q2 * keep_b], axis=0)
                s = _dot_nt(lhs, k_scr[keys, hk * kw:(hk + 1) * kw]) + bias_scr[first, a // 2]
                sink = jnp.where(is_a, sink_ref[a], sink_ref[a + 1])
                m = jnp.maximum(jnp.max(s, axis=-1, keepdims=True), sink)
                p = jnp.exp(s - m).astype(BF16)
                pv = _dot(p, v_scr[keys, hk * vw:(hk + 1) * vw])
                num = jnp.where(lo, pv[0:WINDOW, 0:kw], pv[WINDOW:2 * WINDOW, 0:kw])
                den = jnp.where(lo, pv[0:WINDOW, kw:2 * kw], pv[WINDOW:2 * WINDOW, kw:2 * kw])
                m_slab = jnp.where(lo, m[0:WINDOW], m[WINDOW:2 * WINDOW])
                den = den + jnp.exp(jnp.where(lo, sink_ref[a], sink_ref[a + 1]) - m_slab)
                yb_ref[rows, slab] = (num / den * gate_scr[rows, slab]).astype(BF16)
        return carry

    lax.fori_loop(0, tile // WINDOW, block, 0)
    k_scr[0:WINDOW, :] = k_scr[tile:tile + WINDOW, :]
    v_scr[0:WINDOW, :] = v_scr[tile:tile + WINDOW, :]


BIAS_SPAN = 3 * WINDOW


def _prompt_bias_table(rel_bias):
    dist = WINDOW - jnp.arange(BIAS_SPAN)
    table = jnp.where(((dist >= 0) & (dist < WINDOW))[:, None], rel_bias.astype(F32)[_rel_bucket(dist)], MASK_VALUE)
    return table.T


def _attn_prompt(x, g_pre, w0, table, sinks):
    t = x.shape[0]
    tile = min(ROW_TILE, t)
    kv_w = N_KV * HEAD_DIM
    win_spec = pl.BlockSpec((WINDOW, kv_w), lambda i: (0, 0))
    return pl.pallas_call(
        _attn_prompt_kernel,
        grid=(t // tile,),
        in_specs=[_rows(tile, D_MODEL), _resident((1, D_MODEL)),
                  _cols(D_MODEL, HALF, 4), _cols(D_MODEL, 2 * kv_w, 10),
                  _cols(D_MODEL, HALF // 2, 11), _cols(D_MODEL, HALF // 2, 12),
                  _resident(table.shape), pl.BlockSpec(memory_space=pltpu.SMEM)],
        out_specs=[_rows(tile, HALF), win_spec, win_spec],
        out_shape=[jax.ShapeDtypeStruct((t, HALF), BF16),
                   jax.ShapeDtypeStruct((WINDOW, kv_w), F32), jax.ShapeDtypeStruct((WINDOW, kv_w), F32)],
        scratch_shapes=[pltpu.VMEM((tile, HALF), BF16), pltpu.VMEM((tile, HALF), F32),
                        pltpu.VMEM((tile + WINDOW, 2 * kv_w), BF16), pltpu.VMEM((tile + WINDOW, 4 * kv_w), BF16),
                        pltpu.VMEM((2, N_HEADS // 2, 2 * WINDOW, 2 * WINDOW), F32)],
        compiler_params=_params(),
        name="attn_prompt",
    )(x, g_pre, w0, w0, w0, w0, table, sinks)


def _attn_proj_kernel(x_ref, g_ref, wq_ref, wkv_ref, wg0_ref, wg1_ref, qg_ref, kt_ref, vt_ref, kv_scr):
    kv_w = N_KV * HEAD_DIM
    h = _rms_bf16(x_ref[...], g_ref[...])
    q = _dot_w(h, wq_ref[...]) * (HEAD_DIM ** -0.5)
    for hk in range(N_KV):
        for g in range(GROUP):
            src = (hk * GROUP + g) * HEAD_DIM
            dst = (g * N_KV + hk) * HEAD_DIM
            qg_ref[:, dst:dst + HEAD_DIM] = q[:, src:src + HEAD_DIM]
    qg_ref[:, HALF:HALF + HALF // 2] = _dot_w(h, wg0_ref[...])
    qg_ref[:, HALF + HALF // 2:2 * HALF] = _dot_w(h, wg1_ref[...])
    kv_scr[...] = _dot_w(h, wkv_ref[...])
    for j in range(kt_ref.shape[0]):
        kt_ref[j] = kv_scr[j * WINDOW:(j + 1) * WINDOW, 0:kv_w].T
        vt_ref[j] = kv_scr[j * WINDOW:(j + 1) * WINDOW, kv_w:2 * kv_w].T


def _attn_proj(x, g_pre, w0):
    t = x.shape[0]
    kv_w = N_KV * HEAD_DIM
    out_shape = [jax.ShapeDtypeStruct((t, 2 * HALF), F32), jax.ShapeDtypeStruct((t // WINDOW, kv_w, WINDOW), F32),
                 jax.ShapeDtypeStruct((t // WINDOW, kv_w, WINDOW), F32)]
    return pl.pallas_call(
        _attn_proj_kernel,
        grid=(1,),
        in_specs=[_resident((t, D_MODEL)), _resident((1, D_MODEL)),
                  _cols(D_MODEL, HALF, 4), _cols(D_MODEL, 2 * kv_w, 10),
                  _cols(D_MODEL, HALF // 2, 11), _cols(D_MODEL, HALF // 2, 12)],
        out_specs=[_resident(s.shape) for s in out_shape],
        out_shape=out_shape,
        scratch_shapes=[pltpu.VMEM((t, 2 * kv_w), F32)],
        compiler_params=_params(),
        name="attn_proj_sample",
    )(x, g_pre, w0, w0, w0, w0)


ATTN_S_BATCH = 16
KEYS_PAD = 2 * WINDOW


def _attn_sample_kernel(qg_ref, ktn_ref, vtn_ref, ck_ref, cv_ref, bias_ref, sink_ref, yb_ref, nk_ref, nv_ref):
    kv_w = N_KV * HEAD_DIM
    row8 = lax.broadcasted_iota(jnp.int32, (8, kv_w), 0)
    lane_head = lax.broadcasted_iota(jnp.int32, (8, kv_w), 1) // HEAD_DIM
    lower = row8 < DEC_SEQ
    pick = [jnp.where(lane_head == 2 * hp + jnp.where(lower, 0, 1), 1.0, 0.0).astype(F32) for hp in range(2)]
    lower_w = lax.broadcasted_iota(jnp.int32, (8, HALF), 0) < DEC_SEQ
    kept = lax.broadcasted_iota(jnp.int32, (kv_w, WINDOW), 1) < WINDOW - DEC_SEQ
    seq0 = pl.program_id(0) * ATTN_S_BATCH
    per_tile = WINDOW // DEC_SEQ

    def slide(old, new_tile, shift):
        return jnp.where(kept, pltpu.roll(old, WINDOW - DEC_SEQ, 1), pltpu.roll(new_tile, shift, 1))

    def pair(p, carry):
        r0 = pl.multiple_of(p * 8, 8)
        rows = qg_ref[pl.ds(r0, 8), :]
        q8 = rows[:, 0:HALF]
        gate8 = rows[:, HALF:2 * HALF]
        out8 = []
        for sub in range(2):
            b = 2 * p + sub
            q_swap = pltpu.roll(q8, 4, 0)
            q_dup = jnp.where(lower_w, q8, q_swap) if sub == 0 else jnp.where(lower_w, q_swap, q8)
            tile = (seq0 + b) // per_tile
            shift = (2 * WINDOW - DEC_SEQ - DEC_SEQ * ((seq0 + b) % per_tile)) % WINDOW
            k_old = ck_ref[b].reshape(kv_w, WINDOW)
            v_old = cv_ref[b].reshape(kv_w, WINDOW)
            k_win = slide(k_old, ktn_ref[tile], shift)
            v_win = slide(v_old, vtn_ref[tile], shift)
            nk_ref[b] = k_win.reshape(N_KV, HEAD_DIM, WINDOW)
            nv_ref[b] = v_win.reshape(N_KV, HEAD_DIM, WINDOW)
            k_all = jnp.concatenate([k_old, k_win], axis=1).astype(BF16)
            v_all = jnp.concatenate([v_old, v_win], axis=1).astype(BF16)
            q_bd = jnp.concatenate(
                [q_dup[:, g * kv_w:(g + 1) * kv_w] * pick[hp] for g in range(GROUP) for hp in range(2)], axis=0)
            s = _dot(q_bd.astype(BF16), k_all) + bias_ref[...]
            o = _attn_softmax_pv(s, sink_ref[:, 0:1], v_all, v_transposed=True)
            out_g = []
            for g in range(GROUP):
                acc = None
                for hp in range(2):
                    piece = o[(2 * g + hp) * 8:(2 * g + hp + 1) * 8, :] * pick[hp]
                    piece = piece + pltpu.roll(piece, 4, 0)
                    acc = piece if acc is None else acc + piece
                out_g.append(acc)
            out8.append(jnp.concatenate(
                [out_g[g][:, hk * HEAD_DIM:(hk + 1) * HEAD_DIM] for hk in range(N_KV) for g in range(GROUP)], axis=1))
        o8 = jnp.where(lower_w, out8[0], out8[1])
        yb_ref[pl.ds(r0, 8), :] = (o8 * _silu(gate8)).astype(BF16)
        return carry

    lax.fori_loop(0, ATTN_S_BATCH // 2, pair, 0)


def _sample_bias(rel_bias, sinks):
    t = jnp.arange(DEC_SEQ)[:, None]
    j = jnp.arange(KEYS_PAD)[None, :]
    pos = jnp.where(j < WINDOW, j, j - (KEYS_PAD - DEC_SEQ) + WINDOW)
    dist = t + WINDOW - pos
    valid = (dist >= 0) & (dist < WINDOW) & ((j < WINDOW) | (j >= KEYS_PAD - DEC_SEQ))
    bias = jnp.where(valid[:, :, None], rel_bias.astype(F32)[_rel_bucket(dist)], MASK_VALUE)
    bias = bias.reshape(DEC_SEQ, KEYS_PAD, N_KV, GROUP).transpose(3, 2, 0, 1).reshape(N_HEADS * DEC_SEQ, KEYS_PAD)
    sink = jnp.broadcast_to(sinks.astype(F32).reshape(N_KV, GROUP).T[:, :, None], (GROUP, N_KV, DEC_SEQ))
    return bias, jnp.broadcast_to(sink.reshape(N_HEADS * DEC_SEQ, 1), (N_HEADS * DEC_SEQ, 128))


def _attn_sample(qg, kt_new, vt_new, cache_kt, cache_vt, bias, sink):
    n_seq = cache_kt.shape[0]
    bb = ATTN_S_BATCH
    cache_spec = pl.BlockSpec((bb, N_KV, HEAD_DIM, WINDOW), lambda i: (i, 0, 0, 0))
    return pl.pallas_call(
        _attn_sample_kernel,
        grid=(n_seq // bb,),
        in_specs=[_rows(bb * DEC_SEQ, 2 * HALF), _resident(kt_new.shape), _resident(vt_new.shape),
                  cache_spec, cache_spec, _resident(bias.shape), _resident(sink.shape)],
        out_specs=[_rows(bb * DEC_SEQ, HALF), cache_spec, cache_spec],
        out_shape=[jax.ShapeDtypeStruct((n_seq * DEC_SEQ, HALF), BF16),
                   jax.ShapeDtypeStruct(cache_kt.shape, F32), jax.ShapeDtypeStruct(cache_vt.shape, F32)],
        compiler_params=_params(),
        name="attn_sample",
    )(qg, kt_new, vt_new, cache_kt, cache_vt, bias, sink)


def _prep_layer0(g_pre, w_in, conv_w, rel_bias, sinks, w_out, g_post):
    return dict(
        g_pre=g_pre.reshape(1, D_MODEL), w0=w_in, conv_w=conv_w, rel_bias=rel_bias, sinks=sinks,
        w_out=w_out, g_post=g_post.reshape(1, D_MODEL))


def _layer0_prompt(x, w):
    ya, s_tail = _conv_a(x, w['g_pre'], w['w0'], w['conv_w'])
    yb, kwin, vwin = _attn_prompt(x, w['g_pre'], w['w0'], _prompt_bias_table(w['rel_bias']), w['sinks'])
    y = _out_proj(ya, yb, x, w['w_out'], w['g_post'])
    return (y, s_tail[6:8], kwin.reshape(WINDOW, N_KV, HEAD_DIM), vwin.reshape(WINDOW, N_KV, HEAD_DIM))


def _layer0_sample(x, conv_state, cache_k, cache_v, w):
    n_seq = x.shape[0]
    rows = x.reshape(n_seq * DEC_SEQ, D_MODEL)
    ya, s = _conv_a(rows, w['g_pre'], w['w0'], w['conv_w'], conv_state.reshape(n_seq, 2 * HALF))
    qg, kt_new, vt_new = _attn_proj(rows, w['g_pre'], w['w0'])
    bias, sink = _sample_bias(w['rel_bias'], w['sinks'])
    yb, new_kt, new_vt = _attn_sample(qg, kt_new, vt_new, cache_k.transpose(0, 2, 3, 1), cache_v.transpose(0, 2, 3, 1),
                                      bias, sink)
    y = _out_proj(ya, yb, rows, w['w_out'], w['g_post'])
    return (y.reshape(n_seq, DEC_SEQ, D_MODEL), s.reshape(n_seq, DEC_SEQ, HALF)[:, DEC_SEQ - 2:],
            new_kt.transpose(0, 3, 1, 2), new_vt.transpose(0, 3, 1, 2))


def _layer_norm(v, g, b):
    xc = v - jnp.mean(v, axis=-1, keepdims=True)
    return xc * lax.rsqrt(jnp.mean(xc * xc, axis=-1, keepdims=True) + NORM_EPS) * g + b


def _cmlp_prompt_kernel(x_ref, g_ref, w_ref, lng_ref, lnb_ref, ws_ref, bs_ref, yc_ref, vn_scr):
    tile = x_ref.shape[0]
    h = _rms_bf16(x_ref[...], g_ref[...])
    v = _dot_wt(h, w_ref[HALF:2 * HALF, :])
    vn_scr[...] = _layer_norm(v, lng_ref[...], lnb_ref[...]).astype(BF16)
    gw = HALF // CMLP_GROUPS
    cols = 2 * gw
    for cb in range(HALF // cols):
        u = _dot_wt(h, w_ref[cb * cols:(cb + 1) * cols, :])
        gate = _silu(_dot_wt(h, w_ref[2 * HALF + cb * cols:2 * HALF + (cb + 1) * cols, :]))
        for gi in range(2):
            grp = 2 * cb + gi
            lanes = slice(grp * gw, (grp + 1) * gw)
            for n in range(tile // CHUNK):
                rows = slice(n * CHUNK, (n + 1) * CHUNK)
                mixed = _dot(ws_ref[grp], vn_scr[rows, lanes]) + bs_ref[grp]
                yc_ref[rows, lanes] = (u[rows, gi * gw:(gi + 1) * gw] * mixed
                                       * gate[rows, gi * gw:(gi + 1) * gw]).astype(BF16)


def _cmlp_prompt(x, g_pre, w_c, ln_g, ln_b, ws_tril, bs_rows):
    t = x.shape[0]
    tile = min(ROW_TILE, t)
    return pl.pallas_call(
        _cmlp_prompt_kernel,
        grid=(t // tile,),
        in_specs=[_rows(tile, D_MODEL), _resident((1, D_MODEL)), _rowwin(3 * HALF, D_MODEL, 0),
                  _resident((1, HALF)), _resident((1, HALF)), _resident(ws_tril.shape), _resident(bs_rows.shape)],
        out_specs=_rows(tile, HALF),
        out_shape=jax.ShapeDtypeStruct((t, HALF), BF16),
        scratch_shapes=[pltpu.VMEM((tile, HALF), BF16)],
        compiler_params=_params(),
        name="cmlp_prompt",
    )(x, g_pre, w_c, ln_g, ln_b, ws_tril, bs_rows)


def _cmlp_sample_kernel(x_ref, g_ref, w_ref, lng_ref, lnb_ref, coef_ref, bias_ref, yc_ref, vn_ref):
    t = x_ref.shape[0]
    h = _rms_bf16(x_ref[...], g_ref[...])
    u = _dot_wt(h, w_ref[0:HALF, :])
    vn = _layer_norm(_dot_wt(h, w_ref[HALF:2 * HALF, :]), lng_ref[...], lnb_ref[...])
    gate = _silu(_dot_wt(h, w_ref[2 * HALF:3 * HALF, :]))
    vn_ref[...] = vn

    def tiled(a):
        return a.reshape(t // 8, 8, HALF)

    mixed = tiled(vn) * coef_ref[0][None] + bias_ref[...][None]
    for k in range(1, DEC_SEQ):
        mixed = mixed + tiled(pltpu.roll(vn, k, 0)) * coef_ref[k][None]
    yc_ref[...] = (u * mixed.reshape(t, HALF) * gate).astype(BF16)


def _cmlp_sample(x, g_pre, w_c, ln_g, ln_b, coef, bias):
    t = x.shape[0]
    return pl.pallas_call(
        _cmlp_sample_kernel,
        grid=(1,),
        in_specs=[_resident((t, D_MODEL)), _resident((1, D_MODEL)), _rowwin(3 * HALF, D_MODEL, 0),
                  _resident((1, HALF)), _resident((1, HALF)), _resident(coef.shape), _resident(bias.shape)],
        out_specs=[_resident((t, HALF)), _resident((t, HALF))],
        out_shape=[jax.ShapeDtypeStruct((t, HALF), BF16), jax.ShapeDtypeStruct((t, HALF), F32)],
        compiler_params=_params(),
        name="cmlp_sample",
    )(x, g_pre, w_c, ln_g, ln_b, coef, bias)


HEAD_LANES = 128
SSD_GW = HALF // SSD_GROUPS


def _softplus(x):
    return jnp.maximum(x, 0.0) + jnp.log1p(jnp.exp(-jnp.abs(x)))


def _dt_proj(h, wdt_ref):
    pad = jnp.zeros((HEAD_LANES - SSD_HEADS, D_MODEL), F32)
    return _dot_wt(h, jnp.concatenate([wdt_ref[...], pad], axis=0))


def _group_norm_gate(y, z, gn):
    gated = y * _silu(z)
    parts = []
    for g in range(SSD_GROUPS):
        part = gated[:, g * SSD_GW:(g + 1) * SSD_GW]
        parts.append(part * lax.rsqrt(jnp.mean(part * part, axis=-1, keepdims=True) + NORM_EPS))
    return (jnp.concatenate(parts, axis=1) * gn).astype(BF16)


def _ssd_prompt_kernel(x_ref, g_ref, wz_ref, wx0_ref, wx1_ref, wx2_ref, wdt_ref, cw_ref, cb_ref, dtb_ref, alog_ref,
                       aloge_ref, dskip_ref, gn_ref, e3_ref, tril3_ref, yd_ref, tail_ref, ssm_ref,
                       xbc_scr, z_scr, dt_scr, ht_scr):
    tile = x_ref.shape[0]
    i = pl.program_id(0)
    cd = SSD_CONV_DIM
    h = _rms_bf16(x_ref[...], g_ref[...])

    @pl.when(i == 0)
    def _():
        tail_ref[...] = jnp.zeros_like(tail_ref)
        ht_scr[...] = jnp.zeros_like(ht_scr)

    z_scr[...] = _dot_wt(h, wz_ref[...])
    raw = jnp.concatenate([_dot_wt(h, wx0_ref[...]), _dot_wt(h, wx1_ref[...]), _dot_wt(h, wx2_ref[...])], axis=1)
    dt_scr[...] = _softplus(_dt_proj(h, wdt_ref) + dtb_ref[...])
    prev = tail_ref[...]
    conv = raw * cw_ref[3:4, :] + cb_ref[...]
    for k in range(1, 4):
        conv = conv + _shift_rows(raw, k, prev) * cw_ref[3 - k:4 - k, :]
    xbc_scr[...] = _silu(conv)
    tail_ref[...] = raw[tile - 8:tile, :]

    a16 = -jnp.exp(alog_ref[...])
    a_e = -jnp.exp(aloge_ref[...])
    causal = (lax.broadcasted_iota(jnp.int32, (CHUNK, CHUNK), 0)
              >= lax.broadcasted_iota(jnp.int32, (CHUNK, CHUNK), 1))
    first_half = lax.broadcasted_iota(jnp.int32, (CHUNK, 2 * HEAD_DIM), 1) < HEAD_DIM
    keep_a = jnp.where(first_half, 1.0, 0.0).astype(BF16)
    keep_b = jnp.where(first_half, 0.0, 1.0).astype(BF16)

    def chunk(n, carry):
        r0 = pl.multiple_of(n * CHUNK, CHUNK)
        rows = pl.ds(r0, CHUNK)
        xs = xbc_scr[rows, 0:HALF]
        dt16 = dt_scr[rows, :]
        dt_e = _dot(jnp.concatenate(_split3(dt16), axis=1), e3_ref[...])
        da_e = dt_e * a_e
        acs_e = _dot(tril3_ref[...], jnp.concatenate(_split3(da_e), axis=0))
        acs16 = _dot(tril3_ref[...], jnp.concatenate(_split3(dt16 * a16), axis=0))
        acs_t = acs16.T
        last_e = acs_e[CHUNK - 1:CHUNK, :]
        xdt = xs * dt_e
        xdt_bf = xdt.astype(BF16)
        xw = (jnp.exp(last_e - acs_e) * xdt).astype(BF16)
        dec_e = jnp.exp(last_e)
        y_parts = []
        yoff_parts = []
        for g in range(SSD_GROUPS):
            c_g = xbc_scr[rows, HALF + 2 * SSD_STATE + g * SSD_STATE:HALF + 2 * SSD_STATE + (g + 1) * SSD_STATE].astype(BF16)
            b_g = xbc_scr[rows, HALF + g * SSD_STATE:HALF + (g + 1) * SSD_STATE].astype(BF16)
            cb = _dot_nt(c_g, b_g)
            h_prev = ht_scr[g]
            yoff_parts.append(_dot(c_g, h_prev.astype(BF16)))
            for r in range(0, SSD_HEADS // SSD_GROUPS, 2):
                wgt = []
                for hd in (g * (SSD_HEADS // SSD_GROUPS) + r, g * (SSD_HEADS // SSD_GROUPS) + r + 1):
                    seg = acs16[:, hd:hd + 1] - acs_t[hd:hd + 1, :]
                    wgt.append(cb * jnp.exp(jnp.where(causal, seg, -jnp.inf)))
                a = g * (SSD_HEADS // SSD_GROUPS) + r
                slab = xdt_bf[:, a * HEAD_DIM:(a + 2) * HEAD_DIM]
                rhs = jnp.concatenate([slab * keep_a, slab * keep_b], axis=0)
                y_parts.append(_dot(jnp.concatenate(wgt, axis=1).astype(BF16), rhs))
            lanes = slice(g * SSD_GW, (g + 1) * SSD_GW)
            ht_scr[g] = h_prev * dec_e[:, lanes] + _dot_tn(b_g, xw[:, lanes])
        y = (jnp.concatenate(y_parts, axis=1) + jnp.concatenate(yoff_parts, axis=1) * jnp.exp(acs_e)
             + dskip_ref[...] * xs)
        yd_ref[rows, :] = _group_norm_gate(y, z_scr[rows, :], gn_ref[...])
        return carry

    lax.fori_loop(0, tile // CHUNK, chunk, 0)

    @pl.when(i == pl.num_programs(0) - 1)
    def _():
        for g in range(SSD_GROUPS):
            ssm_ref[g * SSD_GW:(g + 1) * SSD_GW, :] = ht_scr[g].T


def _ssd_weight_specs():
    third = SSD_CONV_DIM // 3
    first = 4 * HALF // third
    return ([_rowwin(HALF, D_MODEL, 3)] + [_rowwin(third, D_MODEL, first + j) for j in range(3)]
            + [_rowwin(SSD_HEADS, D_MODEL, (4 * HALF + SSD_CONV_DIM) // SSD_HEADS)])


def _ssd_prompt(x, w):
    t = x.shape[0]
    tile = min(ROW_TILE, t)
    cd = SSD_CONV_DIM
    consts = [w['conv_w'], w['conv_b'], w['dt_bias16'], w['a_log16'], w['a_log_e'],
              w['d_skip_e'], w['gate_norm_g'], w['expand3'], w['tril3']]
    return pl.pallas_call(
        _ssd_prompt_kernel,
        grid=(t // tile,),
        in_specs=[_rows(tile, D_MODEL), _resident((1, D_MODEL))] + _ssd_weight_specs()
                 + [_resident(c.shape) for c in consts],
        out_specs=[_rows(tile, HALF), pl.BlockSpec((8, cd), lambda i: (0, 0)),
                   pl.BlockSpec((HALF, SSD_STATE), lambda i: (0, 0))],
        out_shape=[jax.ShapeDtypeStruct((t, HALF), BF16), jax.ShapeDtypeStruct((8, cd), F32),
                   jax.ShapeDtypeStruct((HALF, SSD_STATE), F32)],
        scratch_shapes=[pltpu.VMEM((tile, cd), F32), pltpu.VMEM((tile, HALF), F32),
                        pltpu.VMEM((tile, HEAD_LANES), F32), pltpu.VMEM((SSD_GROUPS, SSD_STATE, SSD_GW), F32)],
        compiler_params=_params(),
        name="ssd_prompt",
    )(x, w['g_pre'], w['w1'], w['w1'], w['w1'], w['w1'], w['w1'], *consts)


def _ssd_sample_pre_kernel(x_ref, g_ref, wz_ref, wx0_ref, wx1_ref, wx2_ref, wdt_ref, cw_ref, cb_ref, st_ref,
                           dtb_ref, aloge_ref, dskip_ref, e3_ref, seg_ref,
                           nconv_ref, z_ref, ysk_ref, eacs_ref, xw_ref, dec_ref, b_ref, c_ref, raw_scr):
    t = x_ref.shape[0]
    n_seq = t // DEC_SEQ
    h = _rms_bf16(x_ref[...], g_ref[...])
    z_ref[...] = _dot_wt(h, wz_ref[...])
    raw = jnp.concatenate([_dot_wt(h, wx0_ref[...]), _dot_wt(h, wx1_ref[...]), _dot_wt(h, wx2_ref[...])], axis=1)
    for c in range(raw_scr.shape[0]):
        lanes = slice(c * 128, (c + 1) * 128)
        raw_scr[c] = raw[:, lanes]
        for j in range(3):
            nconv_ref[j, :, lanes] = raw_scr[c, pl.ds(j + 1, n_seq, stride=DEC_SEQ), :]
    dt16 = _softplus(_dt_proj(h, wdt_ref) + dtb_ref[...])
    dt = _dot(jnp.concatenate(_split3(dt16), axis=1), e3_ref[...])
    old = [st_ref[j] for j in range(3)]
    p1 = _place_steps(t, [(0, old[2])])
    p2 = _place_steps(t, [(0, old[1]), (1, old[2])])
    p3 = _place_steps(t, [(0, old[0]), (1, old[1]), (2, old[2])])

    def step_of(width):
        return lax.broadcasted_iota(jnp.int32, (t, width), 0) % DEC_SEQ

    def back(a, k):
        return jnp.where(step_of(a.shape[1]) >= k, pltpu.roll(a, k, 0), 0.0)

    def ahead(a, k):
        return jnp.where(step_of(a.shape[1]) + k < DEC_SEQ, pltpu.roll(a, t - k, 0), 0.0)

    conv = (raw * cw_ref[3:4, :] + (back(raw, 1) + p1) * cw_ref[2:3, :]
            + (back(raw, 2) + p2) * cw_ref[1:2, :] + (back(raw, 3) + p3) * cw_ref[0:1, :]
            + cb_ref[...])
    xbc = _silu(conv)
    xs = xbc[:, 0:HALF]
    bm = xbc[:, HALF:HALF + 2 * SSD_STATE]
    cm = xbc[:, HALF + 2 * SSD_STATE:]
    b_ref[...] = bm
    c_ref[...] = cm
    da = dt * (-jnp.exp(aloge_ref[...]))
    acs = da + back(da, 1) + back(da, 2) + back(da, 3)
    suffix = ahead(da, 1) + ahead(da, 2) + ahead(da, 3)
    xdt = xs * dt
    y = _dot((cm * bm).astype(BF16), seg_ref[...]) * xdt
    for k in range(1, DEC_SEQ):
        cbk = _dot((cm * pltpu.roll(bm, k, 0)).astype(BF16), seg_ref[...])
        term = cbk * jnp.exp(acs - pltpu.roll(acs, k, 0)) * pltpu.roll(xdt, k, 0)
        y = y + jnp.where(step_of(HALF) >= k, term, 0.0)
    ysk_ref[...] = y + dskip_ref[...] * xs
    eacs_ref[...] = jnp.exp(acs)
    xw_ref[...] = jnp.exp(suffix) * xdt
    dec_ref[...] = jnp.exp(acs + suffix)


def _ssd_sample_pre(x, conv_state, w):
    t = x.shape[0]
    cd = SSD_CONV_DIM
    tile = min(SSD_PRE_ROWS, t)
    seqs = tile // DEC_SEQ
    state_spec = pl.BlockSpec((3, seqs, cd), lambda i: (0, i, 0))
    head = [w['conv_w'], w['conv_b']]
    tail = [w['dt_bias16'], w['a_log_e'], w['d_skip_e'], w['expand3'], w['seg_expand']]
    args = [x, w['g_pre'], w['w1'], w['w1'], w['w1'], w['w1'], w['w1']] + head + [conv_state] + tail
    wide = jax.ShapeDtypeStruct((t, HALF), F32)
    narrow = jax.ShapeDtypeStruct((t, 2 * SSD_STATE), F32)
    out_shape = [jax.ShapeDtypeStruct(conv_state.shape, F32), wide, wide, wide, wide, wide, narrow, narrow]
    return pl.pallas_call(
        _ssd_sample_pre_kernel,
        grid=(t // tile,),
        in_specs=[_rows(tile, D_MODEL), _resident((1, D_MODEL))] + _ssd_weight_specs()
                 + [_resident(c.shape) for c in head] + [state_spec] + [_resident(c.shape) for c in tail],
        out_specs=[state_spec] + [_rows(tile, HALF)] * 5 + [_rows(tile, 2 * SSD_STATE)] * 2,
        out_shape=out_shape,
        scratch_shapes=[pltpu.VMEM((cd // 128, tile, 128), F32)],
        compiler_params=_params(),
        name="ssd_sample_pre",
    )(*args)


SSD_S_BATCH = 8
SSD_PRE_ROWS = 256


def _ssd_sample_state_kernel(st_ref, c_ref, b_ref, xw_ref, dec_ref, eacs_ref, ysk_ref, z_ref, gn_ref,
                             yd_ref, nst_ref):
    row_n = lax.broadcasted_iota(jnp.int32, (8, SSD_STATE), 0)
    row_w = lax.broadcasted_iota(jnp.int32, (8, SSD_GW), 0)
    row_f = lax.broadcasted_iota(jnp.int32, (8, HALF), 0)
    ones_rows = jnp.where((row_n >= 4) & (row_n < 7), 1.0, 0.0).astype(BF16)
    hpg = SSD_HEADS // SSD_GROUPS

    def pair(p, carry):
        r0 = pl.multiple_of(p * 8, 8)
        rows = pl.ds(r0, 8)
        c8 = c_ref[rows, :].astype(BF16)
        b8 = b_ref[rows, :]
        xw8 = xw_ref[rows, :]
        dec8 = dec_ref[rows, :]
        yoff = []
        for sub in range(2):
            b = 2 * p + sub
            xw_own = xw8 if sub == 0 else pltpu.roll(xw8, 4, 0)
            b_own = b8 if sub == 0 else pltpu.roll(b8, 4, 0)
            hi, mid, lo = (term.astype(F32) for term in _split3(dec8[4 * sub:4 * sub + 1, :]))
            parts = []
            for g in range(SSD_GROUPS):
                lanes = slice(g * SSD_GW, (g + 1) * SSD_GW)
                heads = pl.ds(g * hpg, hpg)
                h0 = st_ref[b, heads].reshape(SSD_GW, SSD_STATE)
                parts.append(_dot_nt(c8[:, g * SSD_STATE:(g + 1) * SSD_STATE], h0.astype(BF16)))
                lhs = jnp.where(row_w < 4, xw_own[:, lanes],
                                jnp.where(row_w == 4, hi[:, lanes],
                                          jnp.where(row_w == 5, mid[:, lanes],
                                                    jnp.where(row_w == 6, lo[:, lanes], 0.0)))).astype(BF16)
                rhs_b = jnp.where(row_n < 4, b_own[:, g * SSD_STATE:(g + 1) * SSD_STATE], 0.0).astype(BF16)
                decay = _dot_tn(lhs, ones_rows)
                nst_ref[b, heads] = (h0 * decay + _dot_tn(lhs, rhs_b)).reshape(hpg, HEAD_DIM, SSD_STATE)
            yoff.append(jnp.concatenate(parts, axis=1))
        yoff8 = jnp.where(row_f < 4, yoff[0], yoff[1])
        y = ysk_ref[rows, :] + yoff8 * eacs_ref[rows, :]
        yd_ref[rows, :] = _group_norm_gate(y, z_ref[rows, :], gn_ref[...])
        return carry

    lax.fori_loop(0, SSD_S_BATCH // 2, pair, 0)


def _ssd_sample_state(state, cm, bm, xw, dec, eacs, ysk, z, gn):
    n_seq = state.shape[0]
    bb = SSD_S_BATCH
    r = bb * DEC_SEQ
    st_spec = pl.BlockSpec((bb, SSD_HEADS, HEAD_DIM, SSD_STATE), lambda i: (i, 0, 0, 0))
    return pl.pallas_call(
        _ssd_sample_state_kernel,
        grid=(n_seq // bb,),
        in_specs=[st_spec, _rows(r, 2 * SSD_STATE), _rows(r, 2 * SSD_STATE)] + [_rows(r, HALF)] * 5
                 + [_resident((1, HALF))],
        out_specs=[_rows(r, HALF), st_spec],
        out_shape=[jax.ShapeDtypeStruct((n_seq * DEC_SEQ, HALF), BF16), jax.ShapeDtypeStruct(state.shape, F32)],
        compiler_params=_params(),
        name="ssd_sample_state",
    )(state, cm, bm, xw, dec, eacs, ysk, z, gn)


def _prep_layer1(g_pre, w_in, ln_g, ln_b, w_s, b_s, conv_w, conv_b, dt_bias, a_log, d_skip, gate_norm_g,
                 w_out, g_post):
    cd = SSD_CONV_DIM
    gw = HALF // CMLP_GROUPS
    w1 = w_in.T

    def lanes16(v):
        return jnp.pad(v.astype(F32), (0, HEAD_LANES - SSD_HEADS)).reshape(1, HEAD_LANES)

    def per_channel(v):
        return jnp.repeat(v.astype(F32), HEAD_DIM).reshape(1, HALF)

    head_of = jnp.arange(HALF) // HEAD_DIM
    expand = (jnp.arange(HEAD_LANES)[:, None] == head_of[None, :]).astype(BF16)
    tril = jnp.tril(jnp.ones((CHUNK, CHUNK), BF16))
    grp_rows = jnp.arange(2 * SSD_STATE) // SSD_STATE
    seg_expand = (grp_rows[:, None] == (head_of // (SSD_HEADS // SSD_GROUPS))[None, :]).astype(BF16)

    w4 = jnp.tril(w_s[:, :DEC_SEQ, :DEC_SEQ])
    steps = jnp.arange(DEC_SEQ)
    coef = []
    for k in range(DEC_SEQ):
        src = steps - k
        ck = jnp.where((src >= 0)[None, :], w4[:, steps, jnp.maximum(src, 0)], 0.0)
        ck = jnp.repeat(ck.T, gw, axis=1)
        coef.append(jnp.concatenate([ck, ck], axis=0))
    bias4 = jnp.repeat(b_s[:, :DEC_SEQ].T, gw, axis=1)
    return dict(
        g_pre=g_pre.reshape(1, D_MODEL), w1=w1, ln_g=ln_g.reshape(1, HALF), ln_b=ln_b.reshape(1, HALF),
        ws_tril=jnp.tril(w_s).astype(BF16),
        bs_rows=jnp.broadcast_to(b_s.astype(F32)[:, :, None], (CMLP_GROUPS, CHUNK, gw)),
        coef=jnp.stack(coef).astype(F32), bias4=jnp.concatenate([bias4, bias4], axis=0).astype(F32),
        conv_w=conv_w, conv_b=conv_b.reshape(1, cd), dt_bias16=lanes16(dt_bias), a_log16=lanes16(a_log),
        a_log_e=per_channel(a_log), d_skip_e=per_channel(d_skip), gate_norm_g=gate_norm_g.reshape(1, HALF),
        expand3=jnp.concatenate([expand] * 3, axis=0), tril3=jnp.concatenate([tril] * 3, axis=1),
        seg_expand=seg_expand, w_out=w_out, g_post=g_post.reshape(1, D_MODEL))


def _layer1_prompt(x, w):
    yc = _cmlp_prompt(x, w['g_pre'], w['w1'], w['ln_g'], w['ln_b'], w['ws_tril'], w['bs_rows'])
    yd, tail, ssm = _ssd_prompt(x, w)
    y = _out_proj(yc, yd, x, w['w_out'], w['g_post'])
    return y, tail[5:8], ssm.reshape(SSD_HEADS, HEAD_DIM, SSD_STATE)


def _layer1_sample(x, conv_state, ssm_state, w):
    n_seq = x.shape[0]
    t = n_seq * DEC_SEQ
    cd = SSD_CONV_DIM
    rows = x.reshape(t, D_MODEL)
    yc, vn = _cmlp_sample(rows, w['g_pre'], w['w1'], w['ln_g'], w['ln_b'], w['coef'], w['bias4'])
    new_conv, z, ysk, eacs, xw, dec, bm, cm = _ssd_sample_pre(rows, conv_state.transpose(1, 0, 2), w)
    yd, new_state = _ssd_sample_state(ssm_state, cm, bm, xw, dec, eacs, ysk, z, w['gate_norm_g'])
    y = _out_proj(yc, yd, rows, w['w_out'], w['g_post'])
    return (y.reshape(n_seq, DEC_SEQ, D_MODEL), vn.reshape(n_seq, DEC_SEQ, HALF),
            new_conv.transpose(1, 0, 2), new_state)


def kernel(x_prompt, x_sample, state_conv_a, cache_win_k, cache_win_v, state_conv_d, state_ssm, rel_bias,
           l0_g_pre, l0_w_in, l0_conv_w, l0_sinks, l0_w_out, l0_g_post,
           l1_g_pre, l1_w_in, l1_ln_g, l1_ln_b, l1_w_s, l1_b_s, l1_conv_w, l1_conv_b, l1_dt_bias, l1_a_log,
           l1_d_skip, l1_gate_norm_g, l1_w_out, l1_g_post):
    w0 = _prep_layer0(l0_g_pre, l0_w_in, l0_conv_w, rel_bias, l0_sinks, l0_w_out, l0_g_post)
    w1 = _prep_layer1(l1_g_pre, l1_w_in, l1_ln_g, l1_ln_b, l1_w_s, l1_b_s, l1_conv_w, l1_conv_b, l1_dt_bias,
                      l1_a_log, l1_d_skip, l1_gate_norm_g, l1_w_out, l1_g_post)
    yp, p_conv_a, p_win_k, p_win_v = _layer0_prompt(x_prompt[0], w0)
    ys, s_conv_a, s_win_k, s_win_v = _layer0_sample(x_sample, state_conv_a, cache_win_k, cache_win_v, w0)
    yp, p_conv_d, p_ssm = _layer1_prompt(yp, w1)
    ys, s_chunk_v, s_conv_d, s_ssm = _layer1_sample(ys, state_conv_d, state_ssm, w1)
    return (yp[None], ys, p_conv_a[None], s_conv_a, p_win_k[None], p_win_v[None], s_win_k, s_win_v, s_chunk_v,
            p_conv_d[None], s_conv_d, p_ssm[None], s_ssm)
```

```python
import functools
import math

import jax
import jax.numpy as jnp
from jax import lax
from jax.experimental import pallas as pl
from jax.experimental.pallas import tpu as pltpu

F32 = jnp.float32
BF16 = jnp.bfloat16

D_MODEL = 2048
HALF = 1024
HEAD_DIM = 64
N_HEADS = 16
N_KV = 4
GROUP = 4
WINDOW = 128
NUM_BUCKETS = 32
MAX_DISTANCE = 128
CMLP_GROUPS = 8
CHUNK = 128
SSD_HEADS = 16
SSD_STATE = 128
SSD_GROUPS = 2
SSD_CONV_DIM = HALF + 2 * SSD_GROUPS * SSD_STATE
DEC_SEQ = 4
NORM_EPS = 1e-6
MASK_VALUE = -1e30

ROW_TILE = 512
VMEM_LIMIT = 56 * 1024 * 1024


def _params(n_axes=1):
    return pltpu.CompilerParams(dimension_semantics=("arbitrary",) * n_axes,
                                vmem_limit_bytes=VMEM_LIMIT)


def _resident(shape):
    nd = len(shape)
    return pl.BlockSpec(shape, lambda *_: (0,) * nd, pipeline_mode=pl.Buffered(1))


def _rows(tile, width):
    return pl.BlockSpec((tile, width), lambda i: (i, 0))


def _cols(rows, width, block):
    return pl.BlockSpec((rows, width), lambda *_: (0, block), pipeline_mode=pl.Buffered(1))


def _rowwin(height, cols, block):
    return pl.BlockSpec((height, cols), lambda *_: (block, 0), pipeline_mode=pl.Buffered(1))


def _rms_bf16(x, g):
    ms = jnp.mean(x * x, axis=-1, keepdims=True)
    return (x * lax.rsqrt(ms + NORM_EPS) * g).astype(BF16)


def _silu(x):
    return x * jax.nn.sigmoid(x)


def _dot(a, b):
    return jnp.dot(a, b, preferred_element_type=F32)


def _dot_nt(a, b):
    return lax.dot_general(a, b, (((1,), (1,)), ((), ())), preferred_element_type=F32)


def _dot_tn(a, b):
    return lax.dot_general(a, b, (((0,), (0,)), ((), ())), preferred_element_type=F32)


def _dot_w(a, w):
    return _dot(a, w.astype(BF16))


def _dot_wt(a, wt):
    return _dot_nt(a, wt.astype(BF16))


def _split3(x):
    hi = x.astype(BF16)
    r1 = x - hi.astype(F32)
    mid = r1.astype(BF16)
    lo = (r1 - mid.astype(F32)).astype(BF16)
    return hi, mid, lo


def _place_steps(t, placements):
    n_seq = placements[0][1].shape[0]
    row = lax.broadcasted_iota(jnp.int32, (t, n_seq), 0)
    seq = lax.broadcasted_iota(jnp.int32, (t, n_seq), 1)
    lhs, rhs = [], []
    for step, state in placements:
        sel = jnp.where(row == DEC_SEQ * seq + step, 1.0, 0.0).astype(BF16)
        lhs += [sel] * 3
        rhs += list(_split3(state))
    return _dot(jnp.concatenate(lhs, axis=1), jnp.concatenate(rhs, axis=0))


def _shift_rows(x, k, prev_rows=None):
    r = pltpu.roll(x, k, 0)
    if prev_rows is None:
        return r
    head = r[0:8, :]
    row = lax.broadcasted_iota(jnp.int32, head.shape, 0)
    n_prev = prev_rows.shape[0]
    for t in range(k):
        src = n_prev - k + t
        head = jnp.where(row == t, prev_rows[src:src + 1, :], head)
    return jnp.concatenate([head, r[8:, :]], axis=0)


def _prompt_rows(tile, width, n_p):
    return pl.BlockSpec((tile, width), lambda i: (jnp.minimum(i, n_p - 1), 0))


def _out_proj_kernel(*refs, n_p, x_pair, out_pair):
    refs = list(refs)
    ya_ref, ybp_ref, ybs_ref = refs[:3]
    xs = refs[3:5] if x_pair else refs[3:4]
    w_ref, g_ref = refs[3 + len(xs):5 + len(xs)]
    outs = refs[5 + len(xs):]
    is_sample = pl.program_id(0) == n_p
    yb = jnp.where(is_sample, ybs_ref[...], ybp_ref[...])
    x = jnp.where(is_sample, xs[1][...], xs[0][...]) if x_pair else xs[0][...]
    y = _dot_w(ya_ref[...], w_ref[0:HALF, :]) + _dot_w(yb, w_ref[HALF:2 * HALF, :])
    ms = jnp.mean(y * y, axis=-1, keepdims=True)
    out = x + y * lax.rsqrt(ms + NORM_EPS) * g_ref[...]
    if out_pair:
        @pl.when(jnp.logical_not(is_sample))
        def _():
            outs[0][...] = out

        @pl.when(is_sample)
        def _():
            outs[1][...] = out
    else:
        outs[0][...] = out


def _out_proj(ya_all, yb_p, yb_s, x, w, g, out_pair):
    tile = yb_s.shape[0]
    n_p = yb_p.shape[0] // tile
    x_pair = isinstance(x, tuple)
    x_args = list(x) if x_pair else [x]
    x_specs = ([_prompt_rows(tile, D_MODEL, n_p), _resident((tile, D_MODEL))] if x_pair
               else [_rows(tile, D_MODEL)])
    if out_pair:
        out_specs = [_prompt_rows(tile, D_MODEL, n_p), _resident((tile, D_MODEL))]
        out_shape = [jax.ShapeDtypeStruct((n_p * tile, D_MODEL), F32), jax.ShapeDtypeStruct((tile, D_MODEL), F32)]
    else:
        out_specs = [_rows(tile, D_MODEL)]
        out_shape = [jax.ShapeDtypeStruct(((n_p + 1) * tile, D_MODEL), F32)]
    return pl.pallas_call(
        functools.partial(_out_proj_kernel, n_p=n_p, x_pair=x_pair, out_pair=out_pair),
        grid=(n_p + 1,),
        in_specs=[_rows(tile, HALF), _prompt_rows(tile, HALF, n_p), _resident((tile, HALF))] + x_specs
                 + [_resident((2 * HALF, D_MODEL)), _resident((1, D_MODEL))],
        out_specs=out_specs,
        out_shape=out_shape,
        compiler_params=_params(),
        name="out_proj",
    )(ya_all, yb_p, yb_s, *x_args, w, g)


CONV_A_CHUNK = 256


def _conv_a_kernel(xp_ref, xs_ref, g_ref, w_ref, cw_ref, st_ref, ya_ref, tail_ref, s_ref,
                   p1_scr, p2_scr, *, n_p):
    i = pl.program_id(0)
    is_sample = i == n_p
    tile = xp_ref.shape[0]
    cc = CONV_A_CHUNK
    h = _rms_bf16(jnp.where(is_sample, xs_ref[...], xp_ref[...]), g_ref[...])

    @pl.when(i == 0)
    def _():
        tail_ref[...] = jnp.zeros_like(tail_ref)

    for c in range(HALF // cc):
        lanes = slice(c * cc, (c + 1) * cc)
        a_b, a_c, a_h, a_g = (_dot_w(h, w_ref[:, j * HALF + c * cc:j * HALF + (c + 1) * cc]) for j in range(4))
        s = a_c * a_h

        @pl.when(jnp.logical_not(is_sample))
        def _():
            prev = tail_ref[:, lanes]
            p1_scr[...] = _shift_rows(s, 1, prev)
            p2_scr[...] = _shift_rows(s, 2, prev)
            tail_ref[:, lanes] = s[tile - 8:tile, :]

        @pl.when(is_sample)
        def _():
            t_in = lax.broadcasted_iota(jnp.int32, s.shape, 0) % DEC_SEQ
            old0 = st_ref[:, c * cc:(c + 1) * cc]
            old1 = st_ref[:, HALF + c * cc:HALF + (c + 1) * cc]
            p1_scr[...] = jnp.where(t_in >= 1, _shift_rows(s, 1), 0.0) + _place_steps(tile, [(0, old1)])
            p2_scr[...] = (jnp.where(t_in >= 2, _shift_rows(s, 2), 0.0)
                           + _place_steps(tile, [(0, old0), (1, old1)]))
            s_ref[:, lanes] = s

        conv = p2_scr[...] * cw_ref[0:1, lanes] + p1_scr[...] * cw_ref[1:2, lanes] + s * cw_ref[2:3, lanes]
        ya_ref[:, lanes] = (a_b * conv * _silu(a_g)).astype(BF16)


def _conv_a(x_p, x_s, g_pre, w0, conv_w, state):
    tile = x_s.shape[0]
    n_p = x_p.shape[0] // tile
    return pl.pallas_call(
        functools.partial(_conv_a_kernel, n_p=n_p),
        grid=(n_p + 1,),
        in_specs=[_prompt_rows(tile, D_MODEL, n_p), _resident((tile, D_MODEL)), _resident((1, D_MODEL)),
                  _cols(D_MODEL, 4 * HALF, 0), _resident((3, HALF)), _resident(state.shape)],
        out_specs=[_rows(tile, HALF), _resident((8, HALF)), _resident((tile, HALF))],
        out_shape=[jax.ShapeDtypeStruct(((n_p + 1) * tile, HALF), BF16), jax.ShapeDtypeStruct((8, HALF), F32),
                   jax.ShapeDtypeStruct((tile, HALF), F32)],
        scratch_shapes=[pltpu.VMEM((tile, CONV_A_CHUNK), F32), pltpu.VMEM((tile, CONV_A_CHUNK), F32)],
        compiler_params=_params(),
        name="conv_a",
    )(x_p, x_s, g_pre, w0, conv_w, state)


def _rel_bucket(dist):
    max_exact = NUM_BUCKETS // 2
    d = jnp.maximum(dist, 0)
    ratio = jnp.maximum(d, max_exact).astype(F32) / max_exact
    large = max_exact + (jnp.log(ratio) / math.log(MAX_DISTANCE / max_exact)
                         * (NUM_BUCKETS - max_exact)).astype(jnp.int32)
    return jnp.where(d < max_exact, d, jnp.minimum(large, NUM_BUCKETS - 1))


def _attn_softmax_pv(s, sink, v_bf, v_transposed=False):
    m = jnp.maximum(jnp.max(s, axis=-1, keepdims=True), sink)
    p = jnp.exp(s - m)
    den = jnp.sum(p, axis=-1, keepdims=True) + jnp.exp(sink - m)
    pv = _dot_nt(p.astype(BF16), v_bf) if v_transposed else _dot(p.astype(BF16), v_bf)
    return pv / den


def _attn_prompt_kernel(x_ref, g_ref, wq_ref, wkv_ref, wg0_ref, wg1_ref, tab_ref, sink_ref, yb_ref, kwin_ref,
                        vwin_ref, q_scr, gate_scr, k_scr, v_scr, bias_scr):
    tile = x_ref.shape[0]
    i = pl.program_id(0)
    kv_w = N_KV * HEAD_DIM
    h = _rms_bf16(x_ref[...], g_ref[...])

    kw, vw = 2 * HEAD_DIM, 4 * HEAD_DIM

    @pl.when(i == 0)
    def _():
        k_scr[0:WINDOW, :] = jnp.zeros((WINDOW, N_KV * kw), BF16)
        v_scr[0:WINDOW, :] = jnp.zeros((WINDOW, N_KV * vw), BF16)
        for hk in range(N_KV):
            v_scr[:, hk * vw + kw:(hk + 1) * vw] = jnp.ones((tile + WINDOW, kw), BF16)
        in_own = lax.broadcasted_iota(jnp.int32, (WINDOW, 2 * WINDOW), 1) >= WINDOW
        for head in range(N_HEADS):
            row = jnp.broadcast_to(tab_ref[head:head + 1, :], (WINDOW, BIAS_SPAN))
            band = pltpu.roll(row, 0, 1, stride=1, stride_axis=0)[:, 0:2 * WINDOW]
            rows = slice((head % 2) * WINDOW, (head % 2 + 1) * WINDOW)
            bias_scr[1, head // 2, rows, :] = band
            bias_scr[0, head // 2, rows, :] = jnp.where(in_own, band, MASK_VALUE)

    q_scr[...] = (_dot_w(h, wq_ref[...]) * (HEAD_DIM ** -0.5)).astype(BF16)
    k = _dot_w(h, wkv_ref[:, 0:kv_w])
    v = _dot_w(h, wkv_ref[:, kv_w:2 * kv_w])
    gate_scr[:, 0:HALF // 2] = _silu(_dot_w(h, wg0_ref[...]))
    gate_scr[:, HALF // 2:HALF] = _silu(_dot_w(h, wg1_ref[...]))
    for hk in range(N_KV):
        k_h = k[:, hk * HEAD_DIM:(hk + 1) * HEAD_DIM].astype(BF16)
        v_h = v[:, hk * HEAD_DIM:(hk + 1) * HEAD_DIM].astype(BF16)
        k_scr[WINDOW:WINDOW + tile, hk * kw:(hk + 1) * kw] = jnp.concatenate([k_h, k_h], axis=1)
        v_scr[WINDOW:WINDOW + tile, hk * vw:hk * vw + kw] = jnp.concatenate([v_h, v_h], axis=1)
    kwin_ref[...] = k[tile - WINDOW:tile, :]
    vwin_ref[...] = v[tile - WINDOW:tile, :]

    lane = lax.broadcasted_iota(jnp.int32, (WINDOW, kw), 1)
    lo = lane < HEAD_DIM
    keep_a = jnp.where(lo, 1.0, 0.0).astype(BF16)
    keep_b = jnp.where(lo, 0.0, 1.0).astype(BF16)
    is_a = lax.broadcasted_iota(jnp.int32, (2 * WINDOW, 1), 0) < WINDOW

    def block(n, carry):
        r0 = pl.multiple_of(n * WINDOW, WINDOW)
        rows = pl.ds(r0, WINDOW)
        keys = pl.ds(r0, 2 * WINDOW)
        first = jnp.where(jnp.logical_and(i == 0, n == 0), 0, 1)
        for hk in range(N_KV):
            for gp in range(GROUP // 2):
                a = hk * GROUP + 2 * gp
                slab = slice(a * HEAD_DIM, (a + 2) * HEAD_DIM)
                q2 = q_scr[rows, slab]
                lhs = jnp.concatenate([q2 * keep_a, q2 * keep_b], axis=0)
                s = _dot_nt(lhs, k_scr[keys, hk * kw:(hk + 1) * kw]) + bias_scr[first, a // 2]
                sink = jnp.where(is_a, sink_ref[a], sink_ref[a + 1])
                m = jnp.maximum(jnp.max(s, axis=-1, keepdims=True), sink)
                p = jnp.exp(s - m).astype(BF16)
                pv = _dot(p, v_scr[keys, hk * vw:(hk + 1) * vw])
                num = jnp.where(lo, pv[0:WINDOW, 0:kw], pv[WINDOW:2 * WINDOW, 0:kw])
                den = jnp.where(lo, pv[0:WINDOW, kw:2 * kw], pv[WINDOW:2 * WINDOW, kw:2 * kw])
                m_slab = jnp.where(lo, m[0:WINDOW], m[WINDOW:2 * WINDOW])
                den = den + jnp.exp(jnp.where(lo, sink_ref[a], sink_ref[a + 1]) - m_slab)
                yb_ref[rows, slab] = (num / den * gate_scr[rows, slab]).astype(BF16)
        return carry

    lax.fori_loop(0, tile // WINDOW, block, 0)
    k_scr[0:WINDOW, :] = k_scr[tile:tile + WINDOW, :]
    v_scr[0:WINDOW, :] = v_scr[tile:tile + WINDOW, :]


BIAS_SPAN = 3 * WINDOW


def _prompt_bias_table(rel_bias):
    dist = WINDOW - jnp.arange(BIAS_SPAN)
    table = jnp.where(((dist >= 0) & (dist < WINDOW))[:, None], rel_bias.astype(F32)[_rel_bucket(dist)], MASK_VALUE)
    return table.T


def _attn_prompt(x, g_pre, w0, table, sinks):
    t = x.shape[0]
    tile = min(ROW_TILE, t)
    kv_w = N_KV * HEAD_DIM
    win_spec = pl.BlockSpec((WINDOW, kv_w), lambda i: (0, 0))
    return pl.pallas_call(
        _attn_prompt_kernel,
        grid=(t // tile,),
        in_specs=[_rows(tile, D_MODEL), _resident((1, D_MODEL)),
                  _cols(D_MODEL, HALF, 4), _cols(D_MODEL, 2 * kv_w, 10),
                  _cols(D_MODEL, HALF // 2, 11), _cols(D_MODEL, HALF // 2, 12),
                  _resident(table.shape), pl.BlockSpec(memory_space=pltpu.SMEM)],
        out_specs=[_rows(tile, HALF), win_spec, win_spec],
        out_shape=[jax.ShapeDtypeStruct((t, HALF), BF16),
                   jax.ShapeDtypeStruct((WINDOW, kv_w), F32), jax.ShapeDtypeStruct((WINDOW, kv_w), F32)],
        scratch_shapes=[pltpu.VMEM((tile, HALF), BF16), pltpu.VMEM((tile, HALF), F32),
                        pltpu.VMEM((tile + WINDOW, 2 * kv_w), BF16), pltpu.VMEM((tile + WINDOW, 4 * kv_w), BF16),
                        pltpu.VMEM((2, N_HEADS // 2, 2 * WINDOW, 2 * WINDOW), F32)],
        compiler_params=_params(),
        name="attn_prompt",
    )(x, g_pre, w0, w0, w0, w0, table, sinks)


def _attn_proj_kernel(x_ref, g_ref, wq_ref, wkv_ref, wg0_ref, wg1_ref, qg_ref, kt_ref, vt_ref, kv_scr):
    kv_w = N_KV * HEAD_DIM
    h = _rms_bf16(x_ref[...], g_ref[...])
    q = _dot_w(h, wq_ref[...]) * (HEAD_DIM ** -0.5)
    for hk in range(N_KV):
        for g in range(GROUP):
            src = (hk * GROUP + g) * HEAD_DIM
            dst = (g * N_KV + hk) * HEAD_DIM
            qg_ref[:, dst:dst + HEAD_DIM] = q[:, src:src + HEAD_DIM]
    qg_ref[:, HALF:HALF + HALF // 2] = _dot_w(h, wg0_ref[...])
    qg_ref[:, HALF + HALF // 2:2 * HALF] = _dot_w(h, wg1_ref[...])
    kv_scr[...] = _dot_w(h, wkv_ref[...])
    for j in range(kt_ref.shape[0]):
        kt_ref[j] = kv_scr[j * WINDOW:(j + 1) * WINDOW, 0:kv_w].T
        vt_ref[j] = kv_scr[j * WINDOW:(j + 1) * WINDOW, kv_w:2 * kv_w].T


def _attn_proj(x, g_pre, w0):
    t = x.shape[0]
    kv_w = N_KV * HEAD_DIM
    out_shape = [jax.ShapeDtypeStruct((t, 2 * HALF), F32), jax.ShapeDtypeStruct((t // WINDOW, kv_w, WINDOW), F32),
                 jax.ShapeDtypeStruct((t // WINDOW, kv_w, WINDOW), F32)]
    return pl.pallas_call(
        _attn_proj_kernel,
        grid=(1,),
        in_specs=[_resident((t, D_MODEL)), _resident((1, D_MODEL)),
                  _cols(D_MODEL, HALF, 4), _cols(D_MODEL, 2 * kv_w, 10),
                  _cols(D_MODEL, HALF // 2, 11), _cols(D_MODEL, HALF // 2, 12)],
        out_specs=[_resident(s.shape) for s in out_shape],
        out_shape=out_shape,
        scratch_shapes=[pltpu.VMEM((t, 2 * kv_w), F32)],
        compiler_params=_params(),
        name="attn_proj_sample",
    )(x, g_pre, w0, w0, w0, w0)


ATTN_S_BATCH = 16
KEYS_PAD = 2 * WINDOW


def _attn_sample_kernel(qg_ref, ktn_ref, vtn_ref, ck_ref, cv_ref, bias_ref, sink_ref, yb_ref, nk_ref, nv_ref):
    kv_w = N_KV * HEAD_DIM
    row8 = lax.broadcasted_iota(jnp.int32, (8, kv_w), 0)
    lane_head = lax.broadcasted_iota(jnp.int32, (8, kv_w), 1) // HEAD_DIM
    lower = row8 < DEC_SEQ
    pick = [jnp.where(lane_head == 2 * hp + jnp.where(lower, 0, 1), 1.0, 0.0).astype(F32) for hp in range(2)]
    lower_w = lax.broadcasted_iota(jnp.int32, (8, HALF), 0) < DEC_SEQ
    kept = lax.broadcasted_iota(jnp.int32, (kv_w, WINDOW), 1) < WINDOW - DEC_SEQ
    seq0 = pl.program_id(0) * ATTN_S_BATCH
    per_tile = WINDOW // DEC_SEQ

    def slide(old, new_tile, shift):
        return jnp.where(kept, pltpu.roll(old, WINDOW - DEC_SEQ, 1), pltpu.roll(new_tile, shift, 1))

    def pair(p, carry):
        r0 = pl.multiple_of(p * 8, 8)
        rows = qg_ref[pl.ds(r0, 8), :]
        q8 = rows[:, 0:HALF]
        gate8 = rows[:, HALF:2 * HALF]
        out8 = []
        for sub in range(2):
            b = 2 * p + sub
            q_swap = pltpu.roll(q8, 4, 0)
            q_dup = jnp.where(lower_w, q8, q_swap) if sub == 0 else jnp.where(lower_w, q_swap, q8)
            tile = (seq0 + b) // per_tile
            shift = (2 * WINDOW - DEC_SEQ - DEC_SEQ * ((seq0 + b) % per_tile)) % WINDOW
            k_old = ck_ref[b].reshape(kv_w, WINDOW)
            v_old = cv_ref[b].reshape(kv_w, WINDOW)
            k_win = slide(k_old, ktn_ref[tile], shift)
            v_win = slide(v_old, vtn_ref[tile], shift)
            nk_ref[b] = k_win.reshape(N_KV, HEAD_DIM, WINDOW)
            nv_ref[b] = v_win.reshape(N_KV, HEAD_DIM, WINDOW)
            k_all = jnp.concatenate([k_old, k_win], axis=1).astype(BF16)
            v_all = jnp.concatenate([v_old, v_win], axis=1).astype(BF16)
            q_bd = jnp.concatenate(
                [q_dup[:, g * kv_w:(g + 1) * kv_w] * pick[hp] for g in range(GROUP) for hp in range(2)], axis=0)
            s = _dot(q_bd.astype(BF16), k_all) + bias_ref[...]
            o = _attn_softmax_pv(s, sink_ref[:, 0:1], v_all, v_transposed=True)
            out_g = []
            for g in range(GROUP):
                acc = None
                for hp in range(2):
                    piece = o[(2 * g + hp) * 8:(2 * g + hp + 1) * 8, :] * pick[hp]
                    piece = piece + pltpu.roll(piece, 4, 0)
                    acc = piece if acc is None else acc + piece
                out_g.append(acc)
            out8.append(jnp.concatenate(
                [out_g[g][:, hk * HEAD_DIM:(hk + 1) * HEAD_DIM] for hk in range(N_KV) for g in range(GROUP)], axis=1))
        o8 = jnp.where(lower_w, out8[0], out8[1])
        yb_ref[pl.ds(r0, 8), :] = (o8 * _silu(gate8)).astype(BF16)
        return carry

    lax.fori_loop(0, ATTN_S_BATCH // 2, pair, 0)


def _sample_bias(rel_bias, sinks):
    t = jnp.arange(DEC_SEQ)[:, None]
    j = jnp.arange(KEYS_PAD)[None, :]
    pos = jnp.where(j < WINDOW, j, j - (KEYS_PAD - DEC_SEQ) + WINDOW)
    dist = t + WINDOW - pos
    valid = (dist >= 0) & (dist < WINDOW) & ((j < WINDOW) | (j >= KEYS_PAD - DEC_SEQ))
    bias = jnp.where(valid[:, :, None], rel_bias.astype(F32)[_rel_bucket(dist)], MASK_VALUE)
    bias = bias.reshape(DEC_SEQ, KEYS_PAD, N_KV, GROUP).transpose(3, 2, 0, 1).reshape(N_HEADS * DEC_SEQ, KEYS_PAD)
    sink = jnp.broadcast_to(sinks.astype(F32).reshape(N_KV, GROUP).T[:, :, None], (GROUP, N_KV, DEC_SEQ))
    return bias, jnp.broadcast_to(sink.reshape(N_HEADS * DEC_SEQ, 1), (N_HEADS * DEC_SEQ, 128))


def _attn_sample(qg, kt_new, vt_new, cache_kt, cache_vt, bias, sink):
    n_seq = cache_kt.shape[0]
    bb = ATTN_S_BATCH
    cache_spec = pl.BlockSpec((bb, N_KV, HEAD_DIM, WINDOW), lambda i: (i, 0, 0, 0))
    return pl.pallas_call(
        _attn_sample_kernel,
        grid=(n_seq // bb,),
        in_specs=[_rows(bb * DEC_SEQ, 2 * HALF), _resident(kt_new.shape), _resident(vt_new.shape),
                  cache_spec, cache_spec, _resident(bias.shape), _resident(sink.shape)],
        out_specs=[_rows(bb * DEC_SEQ, HALF), cache_spec, cache_spec],
        out_shape=[jax.ShapeDtypeStruct((n_seq * DEC_SEQ, HALF), BF16),
                   jax.ShapeDtypeStruct(cache_kt.shape, F32), jax.ShapeDtypeStruct(cache_vt.shape, F32)],
        compiler_params=_params(),
        name="attn_sample",
    )(qg, kt_new, vt_new, cache_kt, cache_vt, bias, sink)


def _prep_layer0(g_pre, w_in, conv_w, rel_bias, sinks, w_out, g_post):
    return dict(
        g_pre=g_pre.reshape(1, D_MODEL), w0=w_in, conv_w=conv_w, rel_bias=rel_bias, sinks=sinks,
        w_out=w_out, g_post=g_post.reshape(1, D_MODEL))


def _layer0(x_p, x_s, conv_state, cache_k, cache_v, w):
    n_seq = x_s.shape[0]
    rows = x_s.reshape(n_seq * DEC_SEQ, D_MODEL)
    ya_all, s_tail, s_s = _conv_a(x_p, rows, w['g_pre'], w['w0'], w['conv_w'], conv_state.reshape(n_seq, 2 * HALF))
    yb_p, kwin, vwin = _attn_prompt(x_p, w['g_pre'], w['w0'], _prompt_bias_table(w['rel_bias']), w['sinks'])
    qg, kt_new, vt_new = _attn_proj(rows, w['g_pre'], w['w0'])
    bias, sink = _sample_bias(w['rel_bias'], w['sinks'])
    yb_s, new_kt, new_vt = _attn_sample(qg, kt_new, vt_new, cache_k.transpose(0, 2, 3, 1),
                                        cache_v.transpose(0, 2, 3, 1), bias, sink)
    (y_all,) = _out_proj(ya_all, yb_p, yb_s, (x_p, rows), w['w_out'], w['g_post'], out_pair=False)
    prompt_state = (s_tail[6:8], kwin.reshape(WINDOW, N_KV, HEAD_DIM), vwin.reshape(WINDOW, N_KV, HEAD_DIM))
    sample_state = (s_s.reshape(n_seq, DEC_SEQ, HALF)[:, DEC_SEQ - 2:], new_kt.transpose(0, 3, 1, 2),
                    new_vt.transpose(0, 3, 1, 2))
    return y_all, prompt_state, sample_state


def _layer_norm(v, g, b):
    xc = v - jnp.mean(v, axis=-1, keepdims=True)
    return xc * lax.rsqrt(jnp.mean(xc * xc, axis=-1, keepdims=True) + NORM_EPS) * g + b


def _cmlp_kernel(x_ref, g_ref, w_ref, lng_ref, lnb_ref, ws_ref, bs_ref, coef_ref, bias_ref, yc_ref, vns_ref,
                 vn_scr, *, n_p):
    tile = x_ref.shape[0]
    is_sample = pl.program_id(0) == n_p
    h = _rms_bf16(x_ref[...], g_ref[...])
    vn = _layer_norm(_dot_wt(h, w_ref[HALF:2 * HALF, :]), lng_ref[...], lnb_ref[...])
    vn_scr[...] = vn.astype(BF16)

    @pl.when(is_sample)
    def _():
        vns_ref[...] = vn

    gw = HALF // CMLP_GROUPS
    cols = 2 * gw
    for cb in range(HALF // cols):
        span = slice(cb * cols, (cb + 1) * cols)
        u = _dot_wt(h, w_ref[span, :])
        gate = _silu(_dot_wt(h, w_ref[2 * HALF + cb * cols:2 * HALF + (cb + 1) * cols, :]))

        @pl.when(jnp.logical_not(is_sample))
        def _():
            for gi in range(2):
                grp = 2 * cb + gi
                lanes = slice(grp * gw, (grp + 1) * gw)
                for n in range(tile // CHUNK):
                    rows = slice(n * CHUNK, (n + 1) * CHUNK)
                    mixed = _dot(ws_ref[grp], vn_scr[rows, lanes]) + bs_ref[grp]
                    yc_ref[rows, lanes] = (u[rows, gi * gw:(gi + 1) * gw] * mixed
                                           * gate[rows, gi * gw:(gi + 1) * gw]).astype(BF16)

        @pl.when(is_sample)
        def _():
            def tiled(a):
                return a.reshape(tile // 8, 8, cols)

            vn_c = vns_ref[:, span]
            mixed = tiled(vn_c) * coef_ref[0, :, span][None] + bias_ref[:, span][None]
            for k in range(1, DEC_SEQ):
                mixed = mixed + tiled(pltpu.roll(vn_c, k, 0)) * coef_ref[k, :, span][None]
            yc_ref[:, span] = (u * mixed.reshape(tile, cols) * gate).astype(BF16)


def _cmlp(x_all, n_p, w):
    tile = x_all.shape[0] // (n_p + 1)
    consts = [w['ln_g'], w['ln_b'], w['ws_tril'], w['bs_rows'], w['coef'], w['bias4']]
    return pl.pallas_call(
        functools.partial(_cmlp_kernel, n_p=n_p),
        grid=(n_p + 1,),
        in_specs=[_rows(tile, D_MODEL), _resident((1, D_MODEL)), _rowwin(3 * HALF, D_MODEL, 0)]
                 + [_resident(c.shape) for c in consts],
        out_specs=[_rows(tile, HALF), _resident((tile, HALF))],
        out_shape=[jax.ShapeDtypeStruct(x_all.shape[:1] + (HALF,), BF16), jax.ShapeDtypeStruct((tile, HALF), F32)],
        scratch_shapes=[pltpu.VMEM((tile, HALF), BF16)],
        compiler_params=_params(),
        name="cmlp",
    )(x_all, w['g_pre'], w['w1'], *consts)


HEAD_LANES = 128
SSD_GW = HALF // SSD_GROUPS


def _softplus(x):
    return jnp.maximum(x, 0.0) + jnp.log1p(jnp.exp(-jnp.abs(x)))


def _dt_proj(h, wdt_ref):
    pad = jnp.zeros((HEAD_LANES - SSD_HEADS, D_MODEL), F32)
    return _dot_wt(h, jnp.concatenate([wdt_ref[...], pad], axis=0))


def _group_norm_gate(y, z, gn):
    gated = y * _silu(z)
    parts = []
    for g in range(SSD_GROUPS):
        part = gated[:, g * SSD_GW:(g + 1) * SSD_GW]
        parts.append(part * lax.rsqrt(jnp.mean(part * part, axis=-1, keepdims=True) + NORM_EPS))
    return (jnp.concatenate(parts, axis=1) * gn).astype(BF16)


def _ssd_prompt_kernel(x_ref, g_ref, wz_ref, wx0_ref, wx1_ref, wx2_ref, wdt_ref, cw_ref, cb_ref, dtb_ref, alog_ref,
                       aloge_ref, dskip_ref, gn_ref, e3_ref, tril3_ref, yd_ref, tail_ref, ssm_ref,
                       xbc_scr, z_scr, dt_scr, ht_scr):
    tile = x_ref.shape[0]
    i = pl.program_id(0)
    cd = SSD_CONV_DIM
    h = _rms_bf16(x_ref[...], g_ref[...])

    @pl.when(i == 0)
    def _():
        tail_ref[...] = jnp.zeros_like(tail_ref)
        ht_scr[...] = jnp.zeros_like(ht_scr)

    z_scr[...] = _dot_wt(h, wz_ref[...])
    raw = jnp.concatenate([_dot_wt(h, wx0_ref[...]), _dot_wt(h, wx1_ref[...]), _dot_wt(h, wx2_ref[...])], axis=1)
    dt_scr[...] = _softplus(_dt_proj(h, wdt_ref) + dtb_ref[...])
    prev = tail_ref[...]
    conv = raw * cw_ref[3:4, :] + cb_ref[...]
    for k in range(1, 4):
        conv = conv + _shift_rows(raw, k, prev) * cw_ref[3 - k:4 - k, :]
    xbc_scr[...] = _silu(conv)
    tail_ref[...] = raw[tile - 8:tile, :]

    a16 = -jnp.exp(alog_ref[...])
    a_e = -jnp.exp(aloge_ref[...])
    causal = (lax.broadcasted_iota(jnp.int32, (CHUNK, CHUNK), 0)
              >= lax.broadcasted_iota(jnp.int32, (CHUNK, CHUNK), 1))
    first_half = lax.broadcasted_iota(jnp.int32, (CHUNK, 2 * HEAD_DIM), 1) < HEAD_DIM
    keep_a = jnp.where(first_half, 1.0, 0.0).astype(BF16)
    keep_b = jnp.where(first_half, 0.0, 1.0).astype(BF16)

    def chunk(n, carry):
        r0 = pl.multiple_of(n * CHUNK, CHUNK)
        rows = pl.ds(r0, CHUNK)
        xs = xbc_scr[rows, 0:HALF]
        dt16 = dt_scr[rows, :]
        dt_e = _dot(jnp.concatenate(_split3(dt16), axis=1), e3_ref[...])
        da_e = dt_e * a_e
        acs_e = _dot(tril3_ref[...], jnp.concatenate(_split3(da_e), axis=0))
        acs16 = _dot(tril3_ref[...], jnp.concatenate(_split3(dt16 * a16), axis=0))
        acs_t = acs16.T
        last_e = acs_e[CHUNK - 1:CHUNK, :]
        xdt = xs * dt_e
        xdt_bf = xdt.astype(BF16)
        xw = (jnp.exp(last_e - acs_e) * xdt).astype(BF16)
        dec_e = jnp.exp(last_e)
        y_parts = []
        yoff_parts = []
        for g in range(SSD_GROUPS):
            c_g = xbc_scr[rows, HALF + 2 * SSD_STATE + g * SSD_STATE:HALF + 2 * SSD_STATE + (g + 1) * SSD_STATE].astype(BF16)
            b_g = xbc_scr[rows, HALF + g * SSD_STATE:HALF + (g + 1) * SSD_STATE].astype(BF16)
            cb = _dot_nt(c_g, b_g)
            h_prev = ht_scr[g]
            yoff_parts.append(_dot(c_g, h_prev.astype(BF16)))
            for r in range(0, SSD_HEADS // SSD_GROUPS, 2):
                wgt = []
                for hd in (g * (SSD_HEADS // SSD_GROUPS) + r, g * (SSD_HEADS // SSD_GROUPS) + r + 1):
                    seg = acs16[:, hd:hd + 1] - acs_t[hd:hd + 1, :]
                    wgt.append(cb * jnp.exp(jnp.where(causal, seg, -jnp.inf)))
                a = g * (SSD_HEADS // SSD_GROUPS) + r
                slab = xdt_bf[:, a * HEAD_DIM:(a + 2) * HEAD_DIM]
                rhs = jnp.concatenate([slab * keep_a, slab * keep_b], axis=0)
                y_parts.append(_dot(jnp.concatenate(wgt, axis=1).astype(BF16), rhs))
            lanes = slice(g * SSD_GW, (g + 1) * SSD_GW)
            ht_scr[g] = h_prev * dec_e[:, lanes] + _dot_tn(b_g, xw[:, lanes])
        y = (jnp.concatenate(y_parts, axis=1) + jnp.concatenate(yoff_parts, axis=1) * jnp.exp(acs_e)
             + dskip_ref[...] * xs)
        yd_ref[rows, :] = _group_norm_gate(y, z_scr[rows, :], gn_ref[...])
        return carry

    lax.fori_loop(0, tile // CHUNK, chunk, 0)

    @pl.when(i == pl.num_programs(0) - 1)
    def _():
        for g in range(SSD_GROUPS):
            ssm_ref[g * SSD_GW:(g + 1) * SSD_GW, :] = ht_scr[g].T


def _ssd_weight_specs():
    third = SSD_CONV_DIM // 3
    first = 4 * HALF // third
    return ([_rowwin(HALF, D_MODEL, 3)] + [_rowwin(third, D_MODEL, first + j) for j in range(3)]
            + [_rowwin(SSD_HEADS, D_MODEL, (4 * HALF + SSD_CONV_DIM) // SSD_HEADS)])


def _ssd_prompt(x_all, n_p, w):
    tile = x_all.shape[0] // (n_p + 1)
    t = n_p * tile
    x = x_all
    cd = SSD_CONV_DIM
    consts = [w['conv_w'], w['conv_b'], w['dt_bias16'], w['a_log16'], w['a_log_e'],
              w['d_skip_e'], w['gate_norm_g'], w['expand3'], w['tril3']]
    return pl.pallas_call(
        _ssd_prompt_kernel,
        grid=(t // tile,),
        in_specs=[_rows(tile, D_MODEL), _resident((1, D_MODEL))] + _ssd_weight_specs()
                 + [_resident(c.shape) for c in consts],
        out_specs=[_rows(tile, HALF), pl.BlockSpec((8, cd), lambda i: (0, 0)),
                   pl.BlockSpec((HALF, SSD_STATE), lambda i: (0, 0))],
        out_shape=[jax.ShapeDtypeStruct((t, HALF), BF16), jax.ShapeDtypeStruct((8, cd), F32),
                   jax.ShapeDtypeStruct((HALF, SSD_STATE), F32)],
        scratch_shapes=[pltpu.VMEM((tile, cd), F32), pltpu.VMEM((tile, HALF), F32),
                        pltpu.VMEM((tile, HEAD_LANES), F32), pltpu.VMEM((SSD_GROUPS, SSD_STATE, SSD_GW), F32)],
        compiler_params=_params(),
        name="ssd_prompt",
    )(x, w['g_pre'], w['w1'], w['w1'], w['w1'], w['w1'], w['w1'], *consts)


def _ssd_sample_pre_kernel(x_ref, g_ref, wz_ref, wx0_ref, wx1_ref, wx2_ref, wdt_ref, cw_ref, cb_ref, st_ref,
                           dtb_ref, aloge_ref, dskip_ref, e3_ref, seg_ref,
                           nconv_ref, z_ref, ysk_ref, eacs_ref, xw_ref, dec_ref, b_ref, c_ref, raw_scr):
    t = x_ref.shape[0]
    n_seq = t // DEC_SEQ
    h = _rms_bf16(x_ref[...], g_ref[...])
    z_ref[...] = _dot_wt(h, wz_ref[...])
    raw = jnp.concatenate([_dot_wt(h, wx0_ref[...]), _dot_wt(h, wx1_ref[...]), _dot_wt(h, wx2_ref[...])], axis=1)
    for c in range(raw_scr.shape[0]):
        lanes = slice(c * 128, (c + 1) * 128)
        raw_scr[c] = raw[:, lanes]
        for j in range(3):
            nconv_ref[j, :, lanes] = raw_scr[c, pl.ds(j + 1, n_seq, stride=DEC_SEQ), :]
    dt16 = _softplus(_dt_proj(h, wdt_ref) + dtb_ref[...])
    dt = _dot(jnp.concatenate(_split3(dt16), axis=1), e3_ref[...])
    old = [st_ref[j] for j in range(3)]
    p1 = _place_steps(t, [(0, old[2])])
    p2 = _place_steps(t, [(0, old[1]), (1, old[2])])
    p3 = _place_steps(t, [(0, old[0]), (1, old[1]), (2, old[2])])

    def step_of(width):
        return lax.broadcasted_iota(jnp.int32, (t, width), 0) % DEC_SEQ

    def back(a, k):
        return jnp.where(step_of(a.shape[1]) >= k, pltpu.roll(a, k, 0), 0.0)

    def ahead(a, k):
        return jnp.where(step_of(a.shape[1]) + k < DEC_SEQ, pltpu.roll(a, t - k, 0), 0.0)

    conv = (raw * cw_ref[3:4, :] + (back(raw, 1) + p1) * cw_ref[2:3, :]
            + (back(raw, 2) + p2) * cw_ref[1:2, :] + (back(raw, 3) + p3) * cw_ref[0:1, :]
            + cb_ref[...])
    xbc = _silu(conv)
    xs = xbc[:, 0:HALF]
    bm = xbc[:, HALF:HALF + 2 * SSD_STATE]
    cm = xbc[:, HALF + 2 * SSD_STATE:]
    b_ref[...] = bm
    c_ref[...] = cm
    da = dt * (-jnp.exp(aloge_ref[...]))
    acs = da + back(da, 1) + back(da, 2) + back(da, 3)
    suffix = ahead(da, 1) + ahead(da, 2) + ahead(da, 3)
    xdt = xs * dt
    y = _dot((cm * bm).astype(BF16), seg_ref[...]) * xdt
    for k in range(1, DEC_SEQ):
        cbk = _dot((cm * pltpu.roll(bm, k, 0)).astype(BF16), seg_ref[...])
        term = cbk * jnp.exp(acs - pltpu.roll(acs, k, 0)) * pltpu.roll(xdt, k, 0)
        y = y + jnp.where(step_of(HALF) >= k, term, 0.0)
    ysk_ref[...] = y + dskip_ref[...] * xs
    eacs_ref[...] = jnp.exp(acs)
    xw_ref[...] = jnp.exp(suffix) * xdt
    dec_ref[...] = jnp.exp(acs + suffix)


def _ssd_sample_pre(x_all, n_p, conv_state, w):
    t = x_all.shape[0] // (n_p + 1)
    x = x_all
    cd = SSD_CONV_DIM
    tile = min(SSD_PRE_ROWS, t)
    first = n_p * t // tile
    seqs = tile // DEC_SEQ
    state_spec = pl.BlockSpec((3, seqs, cd), lambda i: (0, i, 0))
    head = [w['conv_w'], w['conv_b']]
    tail = [w['dt_bias16'], w['a_log_e'], w['d_skip_e'], w['expand3'], w['seg_expand']]
    args = [x, w['g_pre'], w['w1'], w['w1'], w['w1'], w['w1'], w['w1']] + head + [conv_state] + tail
    wide = jax.ShapeDtypeStruct((t, HALF), F32)
    narrow = jax.ShapeDtypeStruct((t, 2 * SSD_STATE), F32)
    out_shape = [jax.ShapeDtypeStruct(conv_state.shape, F32), wide, wide, wide, wide, wide, narrow, narrow]
    return pl.pallas_call(
        _ssd_sample_pre_kernel,
        grid=(t // tile,),
        in_specs=[pl.BlockSpec((tile, D_MODEL), lambda i: (first + i, 0)), _resident((1, D_MODEL))]
                 + _ssd_weight_specs()
                 + [_resident(c.shape) for c in head] + [state_spec] + [_resident(c.shape) for c in tail],
        out_specs=[state_spec] + [_rows(tile, HALF)] * 5 + [_rows(tile, 2 * SSD_STATE)] * 2,
        out_shape=out_shape,
        scratch_shapes=[pltpu.VMEM((cd // 128, tile, 128), F32)],
        compiler_params=_params(),
        name="ssd_sample_pre",
    )(*args)


SSD_S_BATCH = 8
SSD_PRE_ROWS = 256


def _ssd_sample_state_kernel(st_ref, c_ref, b_ref, xw_ref, dec_ref, eacs_ref, ysk_ref, z_ref, gn_ref,
                             yd_ref, nst_ref):
    row_n = lax.broadcasted_iota(jnp.int32, (8, SSD_STATE), 0)
    row_w = lax.broadcasted_iota(jnp.int32, (8, SSD_GW), 0)
    row_f = lax.broadcasted_iota(jnp.int32, (8, HALF), 0)
    ones_rows = jnp.where((row_n >= 4) & (row_n < 7), 1.0, 0.0).astype(BF16)
    hpg = SSD_HEADS // SSD_GROUPS

    def pair(p, carry):
        r0 = pl.multiple_of(p * 8, 8)
        rows = pl.ds(r0, 8)
        c8 = c_ref[rows, :].astype(BF16)
        b8 = b_ref[rows, :]
        xw8 = xw_ref[rows, :]
        dec8 = dec_ref[rows, :]
        yoff = []
        for sub in range(2):
            b = 2 * p + sub
            xw_own = xw8 if sub == 0 else pltpu.roll(xw8, 4, 0)
            b_own = b8 if sub == 0 else pltpu.roll(b8, 4, 0)
            hi, mid, lo = (term.astype(F32) for term in _split3(dec8[4 * sub:4 * sub + 1, :]))
            parts = []
            for g in range(SSD_GROUPS):
                lanes = slice(g * SSD_GW, (g + 1) * SSD_GW)
                heads = pl.ds(g * hpg, hpg)
                h0 = st_ref[b, heads].reshape(SSD_GW, SSD_STATE)
                parts.append(_dot_nt(c8[:, g * SSD_STATE:(g + 1) * SSD_STATE], h0.astype(BF16)))
                lhs = jnp.where(row_w < 4, xw_own[:, lanes],
                                jnp.where(row_w == 4, hi[:, lanes],
                                          jnp.where(row_w == 5, mid[:, lanes],
                                                    jnp.where(row_w == 6, lo[:, lanes], 0.0)))).astype(BF16)
                rhs_b = jnp.where(row_n < 4, b_own[:, g * SSD_STATE:(g + 1) * SSD_STATE], 0.0).astype(BF16)
                decay = _dot_tn(lhs, ones_rows)
                nst_ref[b, heads] = (h0 * decay + _dot_tn(lhs, rhs_b)).reshape(hpg, HEAD_DIM, SSD_STATE)
            yoff.append(jnp.concatenate(parts, axis=1))
        yoff8 = jnp.where(row_f < 4, yoff[0], yoff[1])
        y = ysk_ref[rows, :] + yoff8 * eacs_ref[rows, :]
        yd_ref[rows, :] = _group_norm_gate(y, z_ref[rows, :], gn_ref[...])
        return carry

    lax.fori_loop(0, SSD_S_BATCH // 2, pair, 0)


def _ssd_sample_state(state, cm, bm, xw, dec, eacs, ysk, z, gn):
    n_seq = state.shape[0]
    bb = SSD_S_BATCH
    r = bb * DEC_SEQ
    st_spec = pl.BlockSpec((bb, SSD_HEADS, HEAD_DIM, SSD_STATE), lambda i: (i, 0, 0, 0))
    return pl.pallas_call(
        _ssd_sample_state_kernel,
        grid=(n_seq // bb,),
        in_specs=[st_spec, _rows(r, 2 * SSD_STATE), _rows(r, 2 * SSD_STATE)] + [_rows(r, HALF)] * 5
                 + [_resident((1, HALF))],
        out_specs=[_rows(r, HALF), st_spec],
        out_shape=[jax.ShapeDtypeStruct((n_seq * DEC_SEQ, HALF), BF16), jax.ShapeDtypeStruct(state.shape, F32)],
        compiler_params=_params(),
        name="ssd_sample_state",
    )(state, cm, bm, xw, dec, eacs, ysk, z, gn)


def _prep_layer1(g_pre, w_in, ln_g, ln_b, w_s, b_s, conv_w, conv_b, dt_bias, a_log, d_skip, gate_norm_g,
                 w_out, g_post):
    cd = SSD_CONV_DIM
    gw = HALF // CMLP_GROUPS
    w1 = w_in.T

    def lanes16(v):
        return jnp.pad(v.astype(F32), (0, HEAD_LANES - SSD_HEADS)).reshape(1, HEAD_LANES)

    def per_channel(v):
        return jnp.repeat(v.astype(F32), HEAD_DIM).reshape(1, HALF)

    head_of = jnp.arange(HALF) // HEAD_DIM
    expand = (jnp.arange(HEAD_LANES)[:, None] == head_of[None, :]).astype(BF16)
    tril = jnp.tril(jnp.ones((CHUNK, CHUNK), BF16))
    grp_rows = jnp.arange(2 * SSD_STATE) // SSD_STATE
    seg_expand = (grp_rows[:, None] == (head_of // (SSD_HEADS // SSD_GROUPS))[None, :]).astype(BF16)

    w4 = jnp.tril(w_s[:, :DEC_SEQ, :DEC_SEQ])
    steps = jnp.arange(DEC_SEQ)
    coef = []
    for k in range(DEC_SEQ):
        src = steps - k
        ck = jnp.where((src >= 0)[None, :], w4[:, steps, jnp.maximum(src, 0)], 0.0)
        ck = jnp.repeat(ck.T, gw, axis=1)
        coef.append(jnp.concatenate([ck, ck], axis=0))
    bias4 = jnp.repeat(b_s[:, :DEC_SEQ].T, gw, axis=1)
    return dict(
        g_pre=g_pre.reshape(1, D_MODEL), w1=w1, ln_g=ln_g.reshape(1, HALF), ln_b=ln_b.reshape(1, HALF),
        ws_tril=jnp.tril(w_s).astype(BF16),
        bs_rows=jnp.broadcast_to(b_s.astype(F32)[:, :, None], (CMLP_GROUPS, CHUNK, gw)),
        coef=jnp.stack(coef).astype(F32), bias4=jnp.concatenate([bias4, bias4], axis=0).astype(F32),
        conv_w=conv_w, conv_b=conv_b.reshape(1, cd), dt_bias16=lanes16(dt_bias), a_log16=lanes16(a_log),
        a_log_e=per_channel(a_log), d_skip_e=per_channel(d_skip), gate_norm_g=gate_norm_g.reshape(1, HALF),
        expand3=jnp.concatenate([expand] * 3, axis=0), tril3=jnp.concatenate([tril] * 3, axis=1),
        seg_expand=seg_expand, w_out=w_out, g_post=g_post.reshape(1, D_MODEL))


def _layer1(x_all, n_p, conv_state, ssm_state, w):
    n_seq = ssm_state.shape[0]
    yc_all, vn = _cmlp(x_all, n_p, w)
    yd_p, tail, ssm = _ssd_prompt(x_all, n_p, w)
    new_conv, z, ysk, eacs, xw, dec, bm, cm = _ssd_sample_pre(x_all, n_p, conv_state.transpose(1, 0, 2), w)
    yd_s, new_state = _ssd_sample_state(ssm_state, cm, bm, xw, dec, eacs, ysk, z, w['gate_norm_g'])
    y_p, y_s = _out_proj(yc_all, yd_p, yd_s, x_all, w['w_out'], w['g_post'], out_pair=True)
    prompt_out = (y_p, tail[5:8], ssm.reshape(SSD_HEADS, HEAD_DIM, SSD_STATE))
    sample_out = (y_s.reshape(n_seq, DEC_SEQ, D_MODEL), vn.reshape(n_seq, DEC_SEQ, HALF),
                  new_conv.transpose(1, 0, 2), new_state)
    return prompt_out, sample_out


def kernel(x_prompt, x_sample, state_conv_a, cache_win_k, cache_win_v, state_conv_d, state_ssm, rel_bias,
           l0_g_pre, l0_w_in, l0_conv_w, l0_sinks, l0_w_out, l0_g_post,
           l1_g_pre, l1_w_in, l1_ln_g, l1_ln_b, l1_w_s, l1_b_s, l1_conv_w, l1_conv_b, l1_dt_bias, l1_a_log,
           l1_d_skip, l1_gate_norm_g, l1_w_out, l1_g_post):
    w0 = _prep_layer0(l0_g_pre, l0_w_in, l0_conv_w, rel_bias, l0_sinks, l0_w_out, l0_g_post)
    w1 = _prep_layer1(l1_g_pre, l1_w_in, l1_ln_g, l1_ln_b, l1_w_s, l1_b_s, l1_conv_w, l1_conv_b, l1_dt_bias,
                      l1_a_log, l1_d_skip, l1_gate_norm_g, l1_w_out, l1_g_post)
    x_p = x_prompt[0]
    tile = x_sample.shape[0] * DEC_SEQ
    n_p = x_p.shape[0] // tile
    y_all, (p_conv_a, p_win_k, p_win_v), (s_conv_a, s_win_k, s_win_v) = _layer0(
        x_p, x_sample, state_conv_a, cache_win_k, cache_win_v, w0)
    (yp, p_conv_d, p_ssm), (ys, s_chunk_v, s_conv_d, s_ssm) = _layer1(y_all, n_p, state_conv_d, state_ssm, w1)
    return (yp[None], ys, p_conv_a[None], s_conv_a, p_win_k[None], p_win_v[None], s_win_k, s_win_v, s_chunk_v,
            p_conv_d[None], s_conv_d, p_ssm[None], s_ssm)
```

```python
import functools
import math

import jax
import jax.numpy as jnp
from jax import lax
from jax.experimental import pallas as pl
from jax.experimental.pallas import tpu as pltpu

F32 = jnp.float32
BF16 = jnp.bfloat16

D_MODEL = 2048
HALF = 1024
HEAD_DIM = 64
N_HEADS = 16
N_KV = 4
GROUP = 4
WINDOW = 128
NUM_BUCKETS = 32
MAX_DISTANCE = 128
CMLP_GROUPS = 8
CHUNK = 128
SSD_HEADS = 16
SSD_STATE = 128
SSD_GROUPS = 2
SSD_CONV_DIM = HALF + 2 * SSD_GROUPS * SSD_STATE
DEC_SEQ = 4
NORM_EPS = 1e-6
MASK_VALUE = -1e30

ROW_TILE = 512
VMEM_LIMIT = 56 * 1024 * 1024


def _params(n_axes=1):
    return pltpu.CompilerParams(dimension_semantics=("arbitrary",) * n_axes,
                                vmem_limit_bytes=VMEM_LIMIT)


def _resident(shape):
    nd = len(shape)
    return pl.BlockSpec(shape, lambda *_: (0,) * nd, pipeline_mode=pl.Buffered(1))


def _rows(tile, width):
    return pl.BlockSpec((tile, width), lambda i: (i, 0))


def _cols(rows, width, block):
    return pl.BlockSpec((rows, width), lambda *_: (0, block), pipeline_mode=pl.Buffered(1))


def _rowwin(height, cols, block):
    return pl.BlockSpec((height, cols), lambda *_: (block, 0), pipeline_mode=pl.Buffered(1))


def _rms_bf16(x, g):
    ms = jnp.mean(x * x, axis=-1, keepdims=True)
    return (x * lax.rsqrt(ms + NORM_EPS) * g).astype(BF16)


def _silu(x):
    return x * jax.nn.sigmoid(x)


def _dot(a, b):
    return jnp.dot(a, b, preferred_element_type=F32)


def _dot_nt(a, b):
    return lax.dot_general(a, b, (((1,), (1,)), ((), ())), preferred_element_type=F32)


def _dot_tn(a, b):
    return lax.dot_general(a, b, (((0,), (0,)), ((), ())), preferred_element_type=F32)


def _dot_w(a, w):
    return _dot(a, w.astype(BF16))


def _dot_wt(a, wt):
    return _dot_nt(a, wt.astype(BF16))


W_PIECE = 256


def _stream(w_hbm, scr, sem, windows):
    copies = [pltpu.make_async_copy(w_hbm.at[win], scr.at[n], sem.at[n]) for n, win in enumerate(windows)]
    for cp in copies:
        cp.start()
    return copies


def _col_piece(start):
    return (slice(None), slice(start, start + W_PIECE))


def _row_piece(start):
    return (slice(start, start + W_PIECE), slice(None))


_HBM = pl.BlockSpec(memory_space=pl.ANY)


def _split3(x):
    hi = x.astype(BF16)
    r1 = x - hi.astype(F32)
    mid = r1.astype(BF16)
    lo = (r1 - mid.astype(F32)).astype(BF16)
    return hi, mid, lo


def _place_steps(t, placements):
    n_seq = placements[0][1].shape[0]
    row = lax.broadcasted_iota(jnp.int32, (t, n_seq), 0)
    seq = lax.broadcasted_iota(jnp.int32, (t, n_seq), 1)
    lhs, rhs = [], []
    for step, state in placements:
        sel = jnp.where(row == DEC_SEQ * seq + step, 1.0, 0.0).astype(BF16)
        lhs += [sel] * 3
        rhs += list(_split3(state))
    return _dot(jnp.concatenate(lhs, axis=1), jnp.concatenate(rhs, axis=0))


def _shift_rows(x, k, prev_rows=None):
    r = pltpu.roll(x, k, 0)
    if prev_rows is None:
        return r
    head = r[0:8, :]
    row = lax.broadcasted_iota(jnp.int32, head.shape, 0)
    n_prev = prev_rows.shape[0]
    for t in range(k):
        src = n_prev - k + t
        head = jnp.where(row == t, prev_rows[src:src + 1, :], head)
    return jnp.concatenate([head, r[8:, :]], axis=0)


def _out_proj_kernel(ya_ref, yb_ref, x_ref, w_ref, g_ref, o_ref):
    y = _dot_w(ya_ref[...], w_ref[0:HALF, :]) + _dot_w(yb_ref[...], w_ref[HALF:2 * HALF, :])
    ms = jnp.mean(y * y, axis=-1, keepdims=True)
    o_ref[...] = x_ref[...] + y * lax.rsqrt(ms + NORM_EPS) * g_ref[...]


def _out_proj_stream_kernel(ya_ref, yb_ref, x_ref, w_hbm, g_ref, o_ref, w_scr, w_sem):
    n = D_MODEL // W_PIECE
    copies = _stream(w_hbm, w_scr, w_sem, [_col_piece(c * W_PIECE) for c in range(n)])
    ya, yb = ya_ref[...], yb_ref[...]
    parts = []
    for c in range(n):
        copies[c].wait()
        parts.append(_dot_w(ya, w_scr[c, 0:HALF, :]) + _dot_w(yb, w_scr[c, HALF:2 * HALF, :]))
    y = jnp.concatenate(parts, axis=1)
    ms = jnp.mean(y * y, axis=-1, keepdims=True)
    o_ref[...] = x_ref[...] + y * lax.rsqrt(ms + NORM_EPS) * g_ref[...]


def _out_proj(ya, yb, x, w_bf, g):
    t = x.shape[0]
    tile = min(ROW_TILE, t)
    if t == tile:
        n = D_MODEL // W_PIECE
        return pl.pallas_call(
            _out_proj_stream_kernel,
            grid=(1,),
            in_specs=[_resident((t, HALF)), _resident((t, HALF)), _resident((t, D_MODEL)), _HBM,
                      _resident((1, D_MODEL))],
            out_specs=_resident((t, D_MODEL)),
            out_shape=jax.ShapeDtypeStruct((t, D_MODEL), F32),
            scratch_shapes=[pltpu.VMEM((n, 2 * HALF, W_PIECE), F32), pltpu.SemaphoreType.DMA((n,))],
            compiler_params=_params(),
            name="out_proj_sample",
        )(ya, yb, x, w_bf, g)
    return pl.pallas_call(
        _out_proj_kernel,
        grid=(t // tile,),
        in_specs=[_rows(tile, HALF), _rows(tile, HALF), _rows(tile, D_MODEL),
                  _resident((2 * HALF, D_MODEL)), _resident((1, D_MODEL))],
        out_specs=_rows(tile, D_MODEL),
        out_shape=jax.ShapeDtypeStruct((t, D_MODEL), F32),
        compiler_params=_params(),
        name="out_proj",
    )(ya, yb, x, w_bf, g)


CONV_A_CHUNK = 256


def _conv_a_kernel(*refs, sample):
    if sample:
        x_ref, g_ref, w_hbm, cw_ref, st_ref, ya_ref, s_ref, w_scr, w_sem = refs
    else:
        x_ref, g_ref, w_ref, cw_ref, ya_ref, s_ref = refs
    tile = x_ref.shape[0]
    cc = CONV_A_CHUNK
    n_c = HALF // cc
    if sample:
        copies = _stream(w_hbm, w_scr, w_sem, [_col_piece(j * HALF + c * cc) for c in range(n_c) for j in range(4)])
    h = _rms_bf16(x_ref[...], g_ref[...])
    if not sample:
        @pl.when(pl.program_id(0) == 0)
        def _():
            s_ref[...] = jnp.zeros_like(s_ref)
    for c in range(n_c):
        lanes = slice(c * cc, (c + 1) * cc)
        if sample:
            parts = []
            for j in range(4):
                copies[4 * c + j].wait()
                parts.append(_dot_w(h, w_scr[4 * c + j]))
            a_b, a_c, a_h, a_g = parts
        else:
            a_b, a_c, a_h, a_g = (_dot_w(h, w_ref[:, j * HALF + c * cc:j * HALF + (c + 1) * cc]) for j in range(4))
        s = a_c * a_h
        if sample:
            t_in = lax.broadcasted_iota(jnp.int32, s.shape, 0) % DEC_SEQ
            old0 = st_ref[:, c * cc:(c + 1) * cc]
            old1 = st_ref[:, HALF + c * cc:HALF + (c + 1) * cc]
            p1 = jnp.where(t_in >= 1, _shift_rows(s, 1), 0.0) + _place_steps(tile, [(0, old1)])
            p2 = jnp.where(t_in >= 2, _shift_rows(s, 2), 0.0) + _place_steps(tile, [(0, old0), (1, old1)])
            s_ref[:, lanes] = s
        else:
            prev = s_ref[:, lanes]
            p1 = _shift_rows(s, 1, prev)
            p2 = _shift_rows(s, 2, prev)
            s_ref[:, lanes] = s[tile - 8:tile, :]
        conv = p2 * cw_ref[0:1, lanes] + p1 * cw_ref[1:2, lanes] + s * cw_ref[2:3, lanes]
        ya_ref[:, lanes] = (a_b * conv * _silu(a_g)).astype(BF16)


def _conv_a(x, g_pre, w0, conv_w, state=None):
    t = x.shape[0]
    sample = state is not None
    tile = t if sample else min(ROW_TILE, t)
    w_spec = _HBM if sample else _cols(D_MODEL, 4 * HALF, 0)
    in_specs = [_rows(tile, D_MODEL), _resident((1, D_MODEL)), w_spec, _resident((3, HALF))]
    args = [x, g_pre, w0, conv_w]
    scratch = []
    if sample:
        in_specs.append(_resident(state.shape))
        args.append(state)
        s_spec, s_shape = _rows(tile, HALF), (t, HALF)
        n_pieces = 4 * HALF // W_PIECE
        scratch = [pltpu.VMEM((n_pieces, D_MODEL, W_PIECE), F32), pltpu.SemaphoreType.DMA((n_pieces,))]
    else:
        s_spec, s_shape = pl.BlockSpec((8, HALF), lambda i: (0, 0)), (8, HALF)
    return pl.pallas_call(
        functools.partial(_conv_a_kernel, sample=sample),
        grid=(t // tile,),
        in_specs=in_specs,
        out_specs=[_rows(tile, HALF), s_spec],
        out_shape=[jax.ShapeDtypeStruct((t, HALF), BF16), jax.ShapeDtypeStruct(s_shape, F32)],
        scratch_shapes=scratch,
        compiler_params=_params(),
        name="conv_a_sample" if sample else "conv_a_prompt",
    )(*args)


def _rel_bucket(dist):
    max_exact = NUM_BUCKETS // 2
    d = jnp.maximum(dist, 0)
    ratio = jnp.maximum(d, max_exact).astype(F32) / max_exact
    large = max_exact + (jnp.log(ratio) / math.log(MAX_DISTANCE / max_exact)
                         * (NUM_BUCKETS - max_exact)).astype(jnp.int32)
    return jnp.where(d < max_exact, d, jnp.minimum(large, NUM_BUCKETS - 1))


def _attn_softmax_pv(s, sink, v_bf, v_transposed=False):
    m = jnp.maximum(jnp.max(s, axis=-1, keepdims=True), sink)
    p = jnp.exp(s - m)
    den = jnp.sum(p, axis=-1, keepdims=True) + jnp.exp(sink - m)
    pv = _dot_nt(p.astype(BF16), v_bf) if v_transposed else _dot(p.astype(BF16), v_bf)
    return pv / den


def _attn_prompt_kernel(x_ref, g_ref, wq_ref, wkv_ref, wg0_ref, wg1_ref, tab_ref, sink_ref, yb_ref, kwin_ref,
                        vwin_ref, q_scr, gate_scr, k_scr, v_scr, bias_scr):
    tile = x_ref.shape[0]
    i = pl.program_id(0)
    kv_w = N_KV * HEAD_DIM
    h = _rms_bf16(x_ref[...], g_ref[...])

    kw, vw = 2 * HEAD_DIM, 4 * HEAD_DIM

    @pl.when(i == 0)
    def _():
        k_scr[0:WINDOW, :] = jnp.zeros((WINDOW, N_KV * kw), BF16)
        v_scr[0:WINDOW, :] = jnp.zeros((WINDOW, N_KV * vw), BF16)
        for hk in range(N_KV):
            v_scr[:, hk * vw + kw:(hk + 1) * vw] = jnp.ones((tile + WINDOW, kw), BF16)
        in_own = lax.broadcasted_iota(jnp.int32, (WINDOW, 2 * WINDOW), 1) >= WINDOW
        for head in range(N_HEADS):
            row = jnp.broadcast_to(tab_ref[head:head + 1, :], (WINDOW, BIAS_SPAN))
            band = pltpu.roll(row, 0, 1, stride=1, stride_axis=0)[:, 0:2 * WINDOW]
            rows = slice((head % 2) * WINDOW, (head % 2 + 1) * WINDOW)
            bias_scr[1, head // 2, rows, :] = band
            bias_scr[0, head // 2, rows, :] = jnp.where(in_own, band, MASK_VALUE)

    q_scr[...] = (_dot_w(h, wq_ref[...]) * (HEAD_DIM ** -0.5)).astype(BF16)
    k = _dot_w(h, wkv_ref[:, 0:kv_w])
    v = _dot_w(h, wkv_ref[:, kv_w:2 * kv_w])
    gate_scr[:, 0:HALF // 2] = _silu(_dot_w(h, wg0_ref[...]))
    gate_scr[:, HALF // 2:HALF] = _silu(_dot_w(h, wg1_ref[...]))
    for hk in range(N_KV):
        k_h = k[:, hk * HEAD_DIM:(hk + 1) * HEAD_DIM].astype(BF16)
        v_h = v[:, hk * HEAD_DIM:(hk + 1) * HEAD_DIM].astype(BF16)
        k_scr[WINDOW:WINDOW + tile, hk * kw:(hk + 1) * kw] = jnp.concatenate([k_h, k_h], axis=1)
        v_scr[WINDOW:WINDOW + tile, hk * vw:hk * vw + kw] = jnp.concatenate([v_h, v_h], axis=1)
    kwin_ref[...] = k[tile - WINDOW:tile, :]
    vwin_ref[...] = v[tile - WINDOW:tile, :]

    lane = lax.broadcasted_iota(jnp.int32, (WINDOW, kw), 1)
    lo = lane < HEAD_DIM
    keep_a = jnp.where(lo, 1.0, 0.0).astype(BF16)
    keep_b = jnp.where(lo, 0.0, 1.0).astype(BF16)
    is_a = lax.broadcasted_iota(jnp.int32, (2 * WINDOW, 1), 0) < WINDOW

    def block(n, carry):
        r0 = pl.multiple_of(n * WINDOW, WINDOW)
        rows = pl.ds(r0, WINDOW)
        keys = pl.ds(r0, 2 * WINDOW)
        first = jnp.where(jnp.logical_and(i == 0, n == 0), 0, 1)
        for hk in range(N_KV):
            for gp in range(GROUP // 2):
                a = hk * GROUP + 2 * gp
                slab = slice(a * HEAD_DIM, (a + 2) * HEAD_DIM)
                q2 = q_scr[rows, slab]
                lhs = jnp.concatenate([q2 * keep_a, q2 * keep_b], axis=0)
                s = _dot_nt(lhs, k_scr[keys, hk * kw:(hk + 1) * kw]) + bias_scr[first, a // 2]
                sink = jnp.where(is_a, sink_ref[a], sink_ref[a + 1])
                m = jnp.maximum(jnp.max(s, axis=-1, keepdims=True), sink)
                p = jnp.exp(s - m).astype(BF16)
                pv = _dot(p, v_scr[keys, hk * vw:(hk + 1) * vw])
                num = jnp.where(lo, pv[0:WINDOW, 0:kw], pv[WINDOW:2 * WINDOW, 0:kw])
                den = jnp.where(lo, pv[0:WINDOW, kw:2 * kw], pv[WINDOW:2 * WINDOW, kw:2 * kw])
                m_slab = jnp.where(lo, m[0:WINDOW], m[WINDOW:2 * WINDOW])
                den = den + jnp.exp(jnp.where(lo, sink_ref[a], sink_ref[a + 1]) - m_slab)
                yb_ref[rows, slab] = (num / den * gate_scr[rows, slab]).astype(BF16)
        return carry

    lax.fori_loop(0, tile // WINDOW, block, 0)
    k_scr[0:WINDOW, :] = k_scr[tile:tile + WINDOW, :]
    v_scr[0:WINDOW, :] = v_scr[tile:tile + WINDOW, :]


BIAS_SPAN = 3 * WINDOW


def _prompt_bias_table(rel_bias):
    dist = WINDOW - jnp.arange(BIAS_SPAN)
    table = jnp.where(((dist >= 0) & (dist < WINDOW))[:, None], rel_bias.astype(F32)[_rel_bucket(dist)], MASK_VALUE)
    return table.T


def _attn_prompt(x, g_pre, w0, table, sinks):
    t = x.shape[0]
    tile = min(ROW_TILE, t)
    kv_w = N_KV * HEAD_DIM
    win_spec = pl.BlockSpec((WINDOW, kv_w), lambda i: (0, 0))
    return pl.pallas_call(
        _attn_prompt_kernel,
        grid=(t // tile,),
        in_specs=[_rows(tile, D_MODEL), _resident((1, D_MODEL)),
                  _cols(D_MODEL, HALF, 4), _cols(D_MODEL, 2 * kv_w, 10),
                  _cols(D_MODEL, HALF // 2, 11), _cols(D_MODEL, HALF // 2, 12),
                  _resident(table.shape), pl.BlockSpec(memory_space=pltpu.SMEM)],
        out_specs=[_rows(tile, HALF), win_spec, win_spec],
        out_shape=[jax.ShapeDtypeStruct((t, HALF), BF16),
                   jax.ShapeDtypeStruct((WINDOW, kv_w), F32), jax.ShapeDtypeStruct((WINDOW, kv_w), F32)],
        scratch_shapes=[pltpu.VMEM((tile, HALF), BF16), pltpu.VMEM((tile, HALF), F32),
                        pltpu.VMEM((tile + WINDOW, 2 * kv_w), BF16), pltpu.VMEM((tile + WINDOW, 4 * kv_w), BF16),
                        pltpu.VMEM((2, N_HEADS // 2, 2 * WINDOW, 2 * WINDOW), F32)],
        compiler_params=_params(),
        name="attn_prompt",
    )(x, g_pre, w0, w0, w0, w0, table, sinks)


def _attn_proj_kernel(x_ref, g_ref, w_hbm, qg_ref, kt_ref, vt_ref, kv_scr, w_scr, w_sem):
    kv_w = N_KV * HEAD_DIM
    n_q = HALF // W_PIECE
    copies = _stream(w_hbm, w_scr, w_sem, [_col_piece(4 * HALF + n * W_PIECE) for n in range(2 * HALF // W_PIECE + 2)])
    h = _rms_bf16(x_ref[...], g_ref[...])
    for hk in range(n_q):
        copies[hk].wait()
        q = _dot_w(h, w_scr[hk]) * (HEAD_DIM ** -0.5)
        for g in range(GROUP):
            dst = (g * N_KV + hk) * HEAD_DIM
            qg_ref[:, dst:dst + HEAD_DIM] = q[:, g * HEAD_DIM:(g + 1) * HEAD_DIM]
    for n in range(2):
        copies[n_q + n].wait()
        kv_scr[:, n * kv_w:(n + 1) * kv_w] = _dot_w(h, w_scr[n_q + n])
    for j in range(kt_ref.shape[0]):
        kt_ref[j] = kv_scr[j * WINDOW:(j + 1) * WINDOW, 0:kv_w].T
        vt_ref[j] = kv_scr[j * WINDOW:(j + 1) * WINDOW, kv_w:2 * kv_w].T
    for n in range(n_q):
        copies[n_q + 2 + n].wait()
        qg_ref[:, HALF + n * W_PIECE:HALF + (n + 1) * W_PIECE] = _dot_w(h, w_scr[n_q + 2 + n])


def _attn_proj(x, g_pre, w0):
    t = x.shape[0]
    kv_w = N_KV * HEAD_DIM
    n_pieces = 2 * HALF // W_PIECE + 2
    out_shape = [jax.ShapeDtypeStruct((t, 2 * HALF), F32), jax.ShapeDtypeStruct((t // WINDOW, kv_w, WINDOW), F32),
                 jax.ShapeDtypeStruct((t // WINDOW, kv_w, WINDOW), F32)]
    return pl.pallas_call(
        _attn_proj_kernel,
        grid=(1,),
        in_specs=[_resident((t, D_MODEL)), _resident((1, D_MODEL)), _HBM],
        out_specs=[_resident(s.shape) for s in out_shape],
        out_shape=out_shape,
        scratch_shapes=[pltpu.VMEM((t, 2 * kv_w), F32), pltpu.VMEM((n_pieces, D_MODEL, W_PIECE), F32),
                        pltpu.SemaphoreType.DMA((n_pieces,))],
        compiler_params=_params(),
        name="attn_proj_sample",
    )(x, g_pre, w0)


ATTN_S_BATCH = 16
KEYS_PAD = 2 * WINDOW


def _attn_sample_kernel(qg_ref, ktn_ref, vtn_ref, ck_ref, cv_ref, bias_ref, sink_ref, yb_ref, nk_ref, nv_ref):
    kv_w = N_KV * HEAD_DIM
    row8 = lax.broadcasted_iota(jnp.int32, (8, kv_w), 0)
    lane_head = lax.broadcasted_iota(jnp.int32, (8, kv_w), 1) // HEAD_DIM
    lower = row8 < DEC_SEQ
    pick = [jnp.where(lane_head == 2 * hp + jnp.where(lower, 0, 1), 1.0, 0.0).astype(F32) for hp in range(2)]
    lower_w = lax.broadcasted_iota(jnp.int32, (8, HALF), 0) < DEC_SEQ
    kept = lax.broadcasted_iota(jnp.int32, (kv_w, WINDOW), 1) < WINDOW - DEC_SEQ
    seq0 = pl.program_id(0) * ATTN_S_BATCH
    per_tile = WINDOW // DEC_SEQ

    def slide(old, new_tile, shift):
        return jnp.where(kept, pltpu.roll(old, WINDOW - DEC_SEQ, 1), pltpu.roll(new_tile, shift, 1))

    def pair(p, carry):
        r0 = pl.multiple_of(p * 8, 8)
        rows = qg_ref[pl.ds(r0, 8), :]
        q8 = rows[:, 0:HALF]
        gate8 = rows[:, HALF:2 * HALF]
        out8 = []
        for sub in range(2):
            b = 2 * p + sub
            q_swap = pltpu.roll(q8, 4, 0)
            q_dup = jnp.where(lower_w, q8, q_swap) if sub == 0 else jnp.where(lower_w, q_swap, q8)
            tile = (seq0 + b) // per_tile
            shift = (2 * WINDOW - DEC_SEQ - DEC_SEQ * ((seq0 + b) % per_tile)) % WINDOW
            k_old = ck_ref[b].reshape(kv_w, WINDOW)
            v_old = cv_ref[b].reshape(kv_w, WINDOW)
            k_win = slide(k_old, ktn_ref[tile], shift)
            v_win = slide(v_old, vtn_ref[tile], shift)
            nk_ref[b] = k_win.reshape(N_KV, HEAD_DIM, WINDOW)
            nv_ref[b] = v_win.reshape(N_KV, HEAD_DIM, WINDOW)
            k_all = jnp.concatenate([k_old, k_win], axis=1).astype(BF16)
            v_all = jnp.concatenate([v_old, v_win], axis=1).astype(BF16)
            q_bd = jnp.concatenate(
                [q_dup[:, g * kv_w:(g + 1) * kv_w] * pick[hp] for g in range(GROUP) for hp in range(2)], axis=0)
            s = _dot(q_bd.astype(BF16), k_all) + bias_ref[...]
            o = _attn_softmax_pv(s, sink_ref[:, 0:1], v_all, v_transposed=True)
            out_g = []
            for g in range(GROUP):
                acc = None
                for hp in range(2):
                    piece = o[(2 * g + hp) * 8:(2 * g + hp + 1) * 8, :] * pick[hp]
                    piece = piece + pltpu.roll(piece, 4, 0)
                    acc = piece if acc is None else acc + piece
                out_g.append(acc)
            out8.append(jnp.concatenate(
                [out_g[g][:, hk * HEAD_DIM:(hk + 1) * HEAD_DIM] for hk in range(N_KV) for g in range(GROUP)], axis=1))
        o8 = jnp.where(lower_w, out8[0], out8[1])
        yb_ref[pl.ds(r0, 8), :] = (o8 * _silu(gate8)).astype(BF16)
        return carry

    lax.fori_loop(0, ATTN_S_BATCH // 2, pair, 0)


def _sample_bias(rel_bias, sinks):
    t = jnp.arange(DEC_SEQ)[:, None]
    j = jnp.arange(KEYS_PAD)[None, :]
    pos = jnp.where(j < WINDOW, j, j - (KEYS_PAD - DEC_SEQ) + WINDOW)
    dist = t + WINDOW - pos
    valid = (dist >= 0) & (dist < WINDOW) & ((j < WINDOW) | (j >= KEYS_PAD - DEC_SEQ))
    bias = jnp.where(valid[:, :, None], rel_bias.astype(F32)[_rel_bucket(dist)], MASK_VALUE)
    bias = bias.reshape(DEC_SEQ, KEYS_PAD, N_KV, GROUP).transpose(3, 2, 0, 1).reshape(N_HEADS * DEC_SEQ, KEYS_PAD)
    sink = jnp.broadcast_to(sinks.astype(F32).reshape(N_KV, GROUP).T[:, :, None], (GROUP, N_KV, DEC_SEQ))
    return bias, jnp.broadcast_to(sink.reshape(N_HEADS * DEC_SEQ, 1), (N_HEADS * DEC_SEQ, 128))


def _attn_sample(qg, kt_new, vt_new, cache_kt, cache_vt, bias, sink):
    n_seq = cache_kt.shape[0]
    bb = ATTN_S_BATCH
    cache_spec = pl.BlockSpec((bb, N_KV, HEAD_DIM, WINDOW), lambda i: (i, 0, 0, 0))
    return pl.pallas_call(
        _attn_sample_kernel,
        grid=(n_seq // bb,),
        in_specs=[_rows(bb * DEC_SEQ, 2 * HALF), _resident(kt_new.shape), _resident(vt_new.shape),
                  cache_spec, cache_spec, _resident(bias.shape), _resident(sink.shape)],
        out_specs=[_rows(bb * DEC_SEQ, HALF), cache_spec, cache_spec],
        out_shape=[jax.ShapeDtypeStruct((n_seq * DEC_SEQ, HALF), BF16),
                   jax.ShapeDtypeStruct(cache_kt.shape, F32), jax.ShapeDtypeStruct(cache_vt.shape, F32)],
        compiler_params=_params(),
        name="attn_sample",
    )(qg, kt_new, vt_new, cache_kt, cache_vt, bias, sink)


def _prep_layer0(g_pre, w_in, conv_w, rel_bias, sinks, w_out, g_post):
    return dict(
        g_pre=g_pre.reshape(1, D_MODEL), w0=w_in, conv_w=conv_w, rel_bias=rel_bias, sinks=sinks,
        w_out=w_out, g_post=g_post.reshape(1, D_MODEL))


def _layer0_prompt(x, w):
    ya, s_tail = _conv_a(x, w['g_pre'], w['w0'], w['conv_w'])
    yb, kwin, vwin = _attn_prompt(x, w['g_pre'], w['w0'], _prompt_bias_table(w['rel_bias']), w['sinks'])
    y = _out_proj(ya, yb, x, w['w_out'], w['g_post'])
    return (y, s_tail[6:8], kwin.reshape(WINDOW, N_KV, HEAD_DIM), vwin.reshape(WINDOW, N_KV, HEAD_DIM))


def _layer0_sample(x, conv_state, cache_k, cache_v, w):
    n_seq = x.shape[0]
    rows = x.reshape(n_seq * DEC_SEQ, D_MODEL)
    ya, s = _conv_a(rows, w['g_pre'], w['w0'], w['conv_w'], conv_state.reshape(n_seq, 2 * HALF))
    qg, kt_new, vt_new = _attn_proj(rows, w['g_pre'], w['w0'])
    bias, sink = _sample_bias(w['rel_bias'], w['sinks'])
    yb, new_kt, new_vt = _attn_sample(qg, kt_new, vt_new, cache_k.transpose(0, 2, 3, 1), cache_v.transpose(0, 2, 3, 1),
                                      bias, sink)
    y = _out_proj(ya, yb, rows, w['w_out'], w['g_post'])
    return (y.reshape(n_seq, DEC_SEQ, D_MODEL), s.reshape(n_seq, DEC_SEQ, HALF)[:, DEC_SEQ - 2:],
            new_kt.transpose(0, 3, 1, 2), new_vt.transpose(0, 3, 1, 2))


def _layer_norm(v, g, b):
    xc = v - jnp.mean(v, axis=-1, keepdims=True)
    return xc * lax.rsqrt(jnp.mean(xc * xc, axis=-1, keepdims=True) + NORM_EPS) * g + b


def _cmlp_prompt_kernel(x_ref, g_ref, w_ref, lng_ref, lnb_ref, ws_ref, bs_ref, yc_ref, vn_scr):
    tile = x_ref.shape[0]
    h = _rms_bf16(x_ref[...], g_ref[...])
    v = _dot_wt(h, w_ref[HALF:2 * HALF, :])
    vn_scr[...] = _layer_norm(v, lng_ref[...], lnb_ref[...]).astype(BF16)
    gw = HALF // CMLP_GROUPS
    cols = 2 * gw
    for cb in range(HALF // cols):
        u = _dot_wt(h, w_ref[cb * cols:(cb + 1) * cols, :])
        gate = _silu(_dot_wt(h, w_ref[2 * HALF + cb * cols:2 * HALF + (cb + 1) * cols, :]))
        for gi in range(2):
            grp = 2 * cb + gi
            lanes = slice(grp * gw, (grp + 1) * gw)
            for n in range(tile // CHUNK):
                rows = slice(n * CHUNK, (n + 1) * CHUNK)
                mixed = _dot(ws_ref[grp], vn_scr[rows, lanes]) + bs_ref[grp]
                yc_ref[rows, lanes] = (u[rows, gi * gw:(gi + 1) * gw] * mixed
                                       * gate[rows, gi * gw:(gi + 1) * gw]).astype(BF16)


def _cmlp_prompt(x, g_pre, w_c, ln_g, ln_b, ws_tril, bs_rows):
    t = x.shape[0]
    tile = min(ROW_TILE, t)
    return pl.pallas_call(
        _cmlp_prompt_kernel,
        grid=(t // tile,),
        in_specs=[_rows(tile, D_MODEL), _resident((1, D_MODEL)), _rowwin(3 * HALF, D_MODEL, 0),
                  _resident((1, HALF)), _resident((1, HALF)), _resident(ws_tril.shape), _resident(bs_rows.shape)],
        out_specs=_rows(tile, HALF),
        out_shape=jax.ShapeDtypeStruct((t, HALF), BF16),
        scratch_shapes=[pltpu.VMEM((tile, HALF), BF16)],
        compiler_params=_params(),
        name="cmlp_prompt",
    )(x, g_pre, w_c, ln_g, ln_b, ws_tril, bs_rows)


def _cmlp_sample_kernel(x_ref, g_ref, w_hbm, lng_ref, lnb_ref, coef_ref, bias_ref, yc_ref, vn_ref, w_scr, w_sem):
    t = x_ref.shape[0]
    n = HALF // W_PIECE
    order = [HALF + c * W_PIECE for c in range(n)] + [c * W_PIECE for c in range(n)] \
        + [2 * HALF + c * W_PIECE for c in range(n)]
    copies = _stream(w_hbm, w_scr, w_sem, [_row_piece(r) for r in order])
    h = _rms_bf16(x_ref[...], g_ref[...])

    def proj(first):
        parts = []
        for c in range(n):
            copies[first + c].wait()
            parts.append(_dot_wt(h, w_scr[first + c]))
        return jnp.concatenate(parts, axis=1)

    vn = _layer_norm(proj(0), lng_ref[...], lnb_ref[...])
    u = proj(n)
    gate = _silu(proj(2 * n))
    vn_ref[...] = vn

    def tiled(a):
        return a.reshape(t // 8, 8, HALF)

    mixed = tiled(vn) * coef_ref[0][None] + bias_ref[...][None]
    for k in range(1, DEC_SEQ):
        mixed = mixed + tiled(pltpu.roll(vn, k, 0)) * coef_ref[k][None]
    yc_ref[...] = (u * mixed.reshape(t, HALF) * gate).astype(BF16)


def _cmlp_sample(x, g_pre, w_c, ln_g, ln_b, coef, bias):
    t = x.shape[0]
    return pl.pallas_call(
        _cmlp_sample_kernel,
        grid=(1,),
        in_specs=[_resident((t, D_MODEL)), _resident((1, D_MODEL)), _HBM,
                  _resident((1, HALF)), _resident((1, HALF)), _resident(coef.shape), _resident(bias.shape)],
        out_specs=[_resident((t, HALF)), _resident((t, HALF))],
        out_shape=[jax.ShapeDtypeStruct((t, HALF), BF16), jax.ShapeDtypeStruct((t, HALF), F32)],
        scratch_shapes=[pltpu.VMEM((3 * HALF // W_PIECE, W_PIECE, D_MODEL), F32),
                        pltpu.SemaphoreType.DMA((3 * HALF // W_PIECE,))],
        compiler_params=_params(),
        name="cmlp_sample",
    )(x, g_pre, w_c, ln_g, ln_b, coef, bias)


HEAD_LANES = 128
SSD_GW = HALF // SSD_GROUPS


def _softplus(x):
    return jnp.maximum(x, 0.0) + jnp.log1p(jnp.exp(-jnp.abs(x)))


def _dt_proj(h, wdt_ref):
    pad = jnp.zeros((HEAD_LANES - SSD_HEADS, D_MODEL), F32)
    return _dot_wt(h, jnp.concatenate([wdt_ref[...], pad], axis=0))


def _group_norm_gate(y, z, gn):
    gated = y * _silu(z)
    parts = []
    for g in range(SSD_GROUPS):
        part = gated[:, g * SSD_GW:(g + 1) * SSD_GW]
        parts.append(part * lax.rsqrt(jnp.mean(part * part, axis=-1, keepdims=True) + NORM_EPS))
    return (jnp.concatenate(parts, axis=1) * gn).astype(BF16)


def _ssd_prompt_kernel(x_ref, g_ref, wz_ref, wx0_ref, wx1_ref, wx2_ref, wdt_ref, cw_ref, cb_ref, dtb_ref, alog_ref,
                       aloge_ref, dskip_ref, gn_ref, e3_ref, tril3_ref, yd_ref, tail_ref, ssm_ref,
                       xbc_scr, z_scr, dt_scr, ht_scr, shift_scr):
    tile = x_ref.shape[0]
    i = pl.program_id(0)
    cd = SSD_CONV_DIM
    h = _rms_bf16(x_ref[...], g_ref[...])

    @pl.when(i == 0)
    def _():
        tail_ref[...] = jnp.zeros_like(tail_ref)
        ht_scr[...] = jnp.zeros_like(ht_scr)

    z_scr[...] = _dot_wt(h, wz_ref[...])
    dt_scr[...] = _softplus(_dt_proj(h, wdt_ref) + dtb_ref[...])
    third = cd // 3
    for j, wx_ref in enumerate((wx0_ref, wx1_ref, wx2_ref)):
        cols = slice(j * third, (j + 1) * third)
        raw = _dot_wt(h, wx_ref[...])
        shift_scr[0:8, :] = tail_ref[:, cols]
        shift_scr[8:8 + tile, :] = raw
        conv = raw * cw_ref[3:4, cols] + cb_ref[:, cols]
        for k in range(1, 4):
            conv = conv + shift_scr[8 - k:8 - k + tile, :] * cw_ref[3 - k:4 - k, cols]
        xbc_scr[:, cols] = _silu(conv)
        tail_ref[:, cols] = raw[tile - 8:tile, :]

    a16 = -jnp.exp(alog_ref[...])
    a_e = -jnp.exp(aloge_ref[...])
    causal = (lax.broadcasted_iota(jnp.int32, (CHUNK, CHUNK), 0)
              >= lax.broadcasted_iota(jnp.int32, (CHUNK, CHUNK), 1))
    first_half = lax.broadcasted_iota(jnp.int32, (CHUNK, 2 * HEAD_DIM), 1) < HEAD_DIM
    keep_a = jnp.where(first_half, 1.0, 0.0).astype(BF16)
    keep_b = jnp.where(first_half, 0.0, 1.0).astype(BF16)

    def chunk(n, carry):
        r0 = pl.multiple_of(n * CHUNK, CHUNK)
        rows = pl.ds(r0, CHUNK)
        xs = xbc_scr[rows, 0:HALF]
        dt16 = dt_scr[rows, :]
        dt_e = _dot(jnp.concatenate(_split3(dt16), axis=1), e3_ref[...])
        da_e = dt_e * a_e
        acs_e = _dot(tril3_ref[...], jnp.concatenate(_split3(da_e), axis=0))
        acs16 = _dot(tril3_ref[...], jnp.concatenate(_split3(dt16 * a16), axis=0))
        acs_t = acs16.T
        last_e = acs_e[CHUNK - 1:CHUNK, :]
        xdt = xs * dt_e
        xdt_bf = xdt.astype(BF16)
        xw = (jnp.exp(last_e - acs_e) * xdt).astype(BF16)
        dec_e = jnp.exp(last_e)
        y_parts = []
        yoff_parts = []
        for g in range(SSD_GROUPS):
            c_g = xbc_scr[rows, HALF + 2 * SSD_STATE + g * SSD_STATE:HALF + 2 * SSD_STATE + (g + 1) * SSD_STATE].astype(BF16)
            b_g = xbc_scr[rows, HALF + g * SSD_STATE:HALF + (g + 1) * SSD_STATE].astype(BF16)
            cb = _dot_nt(c_g, b_g)
            h_prev = ht_scr[g]
            yoff_parts.append(_dot(c_g, h_prev.astype(BF16)))
            for r in range(0, SSD_HEADS // SSD_GROUPS, 2):
                wgt = []
                for hd in (g * (SSD_HEADS // SSD_GROUPS) + r, g * (SSD_HEADS // SSD_GROUPS) + r + 1):
                    seg = acs16[:, hd:hd + 1] - acs_t[hd:hd + 1, :]
                    wgt.append(cb * jnp.exp(jnp.where(causal, seg, -jnp.inf)))
                a = g * (SSD_HEADS // SSD_GROUPS) + r
                slab = xdt_bf[:, a * HEAD_DIM:(a + 2) * HEAD_DIM]
                rhs = jnp.concatenate([slab * keep_a, slab * keep_b], axis=0)
                y_parts.append(_dot(jnp.concatenate(wgt, axis=1).astype(BF16), rhs))
            lanes = slice(g * SSD_GW, (g + 1) * SSD_GW)
            ht_scr[g] = h_prev * dec_e[:, lanes] + _dot_tn(b_g, xw[:, lanes])
        y = (jnp.concatenate(y_parts, axis=1) + jnp.concatenate(yoff_parts, axis=1) * jnp.exp(acs_e)
             + dskip_ref[...] * xs)
        yd_ref[rows, :] = _group_norm_gate(y, z_scr[rows, :], gn_ref[...])
        return carry

    lax.fori_loop(0, tile // CHUNK, chunk, 0)

    @pl.when(i == pl.num_programs(0) - 1)
    def _():
        for g in range(SSD_GROUPS):
            ssm_ref[g * SSD_GW:(g + 1) * SSD_GW, :] = ht_scr[g].T


def _ssd_weight_specs():
    third = SSD_CONV_DIM // 3
    first = 4 * HALF // third
    return ([_rowwin(HALF, D_MODEL, 3)] + [_rowwin(third, D_MODEL, first + j) for j in range(3)]
            + [_rowwin(SSD_HEADS, D_MODEL, (4 * HALF + SSD_CONV_DIM) // SSD_HEADS)])


def _ssd_prompt(x, w):
    t = x.shape[0]
    tile = min(ROW_TILE, t)
    cd = SSD_CONV_DIM
    consts = [w['conv_w'], w['conv_b'], w['dt_bias16'], w['a_log16'], w['a_log_e'],
              w['d_skip_e'], w['gate_norm_g'], w['expand3'], w['tril3']]
    return pl.pallas_call(
        _ssd_prompt_kernel,
        grid=(t // tile,),
        in_specs=[_rows(tile, D_MODEL), _resident((1, D_MODEL))] + _ssd_weight_specs()
                 + [_resident(c.shape) for c in consts],
        out_specs=[_rows(tile, HALF), pl.BlockSpec((8, cd), lambda i: (0, 0)),
                   pl.BlockSpec((HALF, SSD_STATE), lambda i: (0, 0))],
        out_shape=[jax.ShapeDtypeStruct((t, HALF), BF16), jax.ShapeDtypeStruct((8, cd), F32),
                   jax.ShapeDtypeStruct((HALF, SSD_STATE), F32)],
        scratch_shapes=[pltpu.VMEM((tile, cd), F32), pltpu.VMEM((tile, HALF), F32),
                        pltpu.VMEM((tile, HEAD_LANES), F32), pltpu.VMEM((SSD_GROUPS, SSD_STATE, SSD_GW), F32),
                        pltpu.VMEM((8 + tile, cd // 3), F32)],
        compiler_params=_params(),
        name="ssd_prompt",
    )(x, w['g_pre'], w['w1'], w['w1'], w['w1'], w['w1'], w['w1'], *consts)


def _ssd_sample_pre_kernel(x_ref, g_ref, wz_ref, wx0_ref, wx1_ref, wx2_ref, wdt_ref, cw_ref, cb_ref, st_ref,
                           dtb_ref, aloge_ref, dskip_ref, e3_ref, seg_ref,
                           nconv_ref, z_ref, ysk_ref, eacs_ref, xw_ref, dec_ref, b_ref, c_ref, raw_scr):
    t = x_ref.shape[0]
    n_seq = t // DEC_SEQ
    h = _rms_bf16(x_ref[...], g_ref[...])
    z_ref[...] = _dot_wt(h, wz_ref[...])
    raw = jnp.concatenate([_dot_wt(h, wx0_ref[...]), _dot_wt(h, wx1_ref[...]), _dot_wt(h, wx2_ref[...])], axis=1)
    for c in range(raw_scr.shape[0]):
        lanes = slice(c * 128, (c + 1) * 128)
        raw_scr[c] = raw[:, lanes]
        for j in range(3):
            nconv_ref[j, :, lanes] = raw_scr[c, pl.ds(j + 1, n_seq, stride=DEC_SEQ), :]
    dt16 = _softplus(_dt_proj(h, wdt_ref) + dtb_ref[...])
    dt = _dot(jnp.concatenate(_split3(dt16), axis=1), e3_ref[...])
    old = [st_ref[j] for j in range(3)]
    p1 = _place_steps(t, [(0, old[2])])
    p2 = _place_steps(t, [(0, old[1]), (1, old[2])])
    p3 = _place_steps(t, [(0, old[0]), (1, old[1]), (2, old[2])])

    def step_of(width):
        return lax.broadcasted_iota(jnp.int32, (t, width), 0) % DEC_SEQ

    def back(a, k):
        return jnp.where(step_of(a.shape[1]) >= k, pltpu.roll(a, k, 0), 0.0)

    def ahead(a, k):
        return jnp.where(step_of(a.shape[1]) + k < DEC_SEQ, pltpu.roll(a, t - k, 0), 0.0)

    conv = (raw * cw_ref[3:4, :] + (back(raw, 1) + p1) * cw_ref[2:3, :]
            + (back(raw, 2) + p2) * cw_ref[1:2, :] + (back(raw, 3) + p3) * cw_ref[0:1, :]
            + cb_ref[...])
    xbc = _silu(conv)
    xs = xbc[:, 0:HALF]
    bm = xbc[:, HALF:HALF + 2 * SSD_STATE]
    cm = xbc[:, HALF + 2 * SSD_STATE:]
    b_ref[...] = bm
    c_ref[...] = cm
    da = dt * (-jnp.exp(aloge_ref[...]))
    acs = da + back(da, 1) + back(da, 2) + back(da, 3)
    suffix = ahead(da, 1) + ahead(da, 2) + ahead(da, 3)
    xdt = xs * dt
    y = _dot((cm * bm).astype(BF16), seg_ref[...]) * xdt
    for k in range(1, DEC_SEQ):
        cbk = _dot((cm * pltpu.roll(bm, k, 0)).astype(BF16), seg_ref[...])
        term = cbk * jnp.exp(acs - pltpu.roll(acs, k, 0)) * pltpu.roll(xdt, k, 0)
        y = y + jnp.where(step_of(HALF) >= k, term, 0.0)
    ysk_ref[...] = y + dskip_ref[...] * xs
    eacs_ref[...] = jnp.exp(acs)
    xw_ref[...] = jnp.exp(suffix) * xdt
    dec_ref[...] = jnp.exp(acs + suffix)


def _ssd_sample_pre(x, conv_state, w):
    t = x.shape[0]
    cd = SSD_CONV_DIM
    tile = min(SSD_PRE_ROWS, t)
    seqs = tile // DEC_SEQ
    state_spec = pl.BlockSpec((3, seqs, cd), lambda i: (0, i, 0))
    head = [w['conv_w'], w['conv_b']]
    tail = [w['dt_bias16'], w['a_log_e'], w['d_skip_e'], w['expand3'], w['seg_expand']]
    args = [x, w['g_pre'], w['w1'], w['w1'], w['w1'], w['w1'], w['w1']] + head + [conv_state] + tail
    wide = jax.ShapeDtypeStruct((t, HALF), F32)
    narrow = jax.ShapeDtypeStruct((t, 2 * SSD_STATE), F32)
    out_shape = [jax.ShapeDtypeStruct(conv_state.shape, F32), wide, wide, wide, wide, wide, narrow, narrow]
    return pl.pallas_call(
        _ssd_sample_pre_kernel,
        grid=(t // tile,),
        in_specs=[_rows(tile, D_MODEL), _resident((1, D_MODEL))] + _ssd_weight_specs()
                 + [_resident(c.shape) for c in head] + [state_spec] + [_resident(c.shape) for c in tail],
        out_specs=[state_spec] + [_rows(tile, HALF)] * 5 + [_rows(tile, 2 * SSD_STATE)] * 2,
        out_shape=out_shape,
        scratch_shapes=[pltpu.VMEM((cd // 128, tile, 128), F32)],
        compiler_params=_params(),
        name="ssd_sample_pre",
    )(*args)


SSD_S_BATCH = 8
SSD_PRE_ROWS = 256


def _ssd_sample_state_kernel(st_ref, c_ref, b_ref, xw_ref, dec_ref, eacs_ref, ysk_ref, z_ref, gn_ref,
                             yd_ref, nst_ref):
    row_n = lax.broadcasted_iota(jnp.int32, (8, SSD_STATE), 0)
    row_w = lax.broadcasted_iota(jnp.int32, (8, SSD_GW), 0)
    row_f = lax.broadcasted_iota(jnp.int32, (8, HALF), 0)
    ones_rows = jnp.where((row_n >= 4) & (row_n < 7), 1.0, 0.0).astype(BF16)
    hpg = SSD_HEADS // SSD_GROUPS

    def pair(p, carry):
        r0 = pl.multiple_of(p * 8, 8)
        rows = pl.ds(r0, 8)
        c8 = c_ref[rows, :].astype(BF16)
        b8 = b_ref[rows, :]
        xw8 = xw_ref[rows, :]
        dec8 = dec_ref[rows, :]
        yoff = []
        for sub in range(2):
            b = 2 * p + sub
            xw_own = xw8 if sub == 0 else pltpu.roll(xw8, 4, 0)
            b_own = b8 if sub == 0 else pltpu.roll(b8, 4, 0)
            hi, mid, lo = (term.astype(F32) for term in _split3(dec8[4 * sub:4 * sub + 1, :]))
            parts = []
            for g in range(SSD_GROUPS):
                lanes = slice(g * SSD_GW, (g + 1) * SSD_GW)
                heads = pl.ds(g * hpg, hpg)
                h0 = st_ref[b, heads].reshape(SSD_GW, SSD_STATE)
                parts.append(_dot_nt(c8[:, g * SSD_STATE:(g + 1) * SSD_STATE], h0.astype(BF16)))
                lhs = jnp.where(row_w < 4, xw_own[:, lanes],
                                jnp.where(row_w == 4, hi[:, lanes],
                                          jnp.where(row_w == 5, mid[:, lanes],
                                                    jnp.where(row_w == 6, lo[:, lanes], 0.0)))).astype(BF16)
                rhs_b = jnp.where(row_n < 4, b_own[:, g * SSD_STATE:(g + 1) * SSD_STATE], 0.0).astype(BF16)
                decay = _dot_tn(lhs, ones_rows)
                nst_ref[b, heads] = (h0 * decay + _dot_tn(lhs, rhs_b)).reshape(hpg, HEAD_DIM, SSD_STATE)
            yoff.append(jnp.concatenate(parts, axis=1))
        yoff8 = jnp.where(row_f < 4, yoff[0], yoff[1])
        y = ysk_ref[rows, :] + yoff8 * eacs_ref[rows, :]
        yd_ref[rows, :] = _group_norm_gate(y, z_ref[rows, :], gn_ref[...])
        return carry

    lax.fori_loop(0, SSD_S_BATCH // 2, pair, 0)


def _ssd_sample_state(state, cm, bm, xw, dec, eacs, ysk, z, gn):
    n_seq = state.shape[0]
    bb = SSD_S_BATCH
    r = bb * DEC_SEQ
    st_spec = pl.BlockSpec((bb, SSD_HEADS, HEAD_DIM, SSD_STATE), lambda i: (i, 0, 0, 0))
    return pl.pallas_call(
        _ssd_sample_state_kernel,
        grid=(n_seq // bb,),
        in_specs=[st_spec, _rows(r, 2 * SSD_STATE), _rows(r, 2 * SSD_STATE)] + [_rows(r, HALF)] * 5
                 + [_resident((1, HALF))],
        out_specs=[_rows(r, HALF), st_spec],
        out_shape=[jax.ShapeDtypeStruct((n_seq * DEC_SEQ, HALF), BF16), jax.ShapeDtypeStruct(state.shape, F32)],
        compiler_params=_params(),
        name="ssd_sample_state",
    )(state, cm, bm, xw, dec, eacs, ysk, z, gn)


def _prep_layer1(g_pre, w_in, ln_g, ln_b, w_s, b_s, conv_w, conv_b, dt_bias, a_log, d_skip, gate_norm_g,
                 w_out, g_post):
    cd = SSD_CONV_DIM
    gw = HALF // CMLP_GROUPS
    w1 = w_in.T

    def lanes16(v):
        return jnp.pad(v.astype(F32), (0, HEAD_LANES - SSD_HEADS)).reshape(1, HEAD_LANES)

    def per_channel(v):
        return jnp.repeat(v.astype(F32), HEAD_DIM).reshape(1, HALF)

    head_of = jnp.arange(HALF) // HEAD_DIM
    expand = (jnp.arange(HEAD_LANES)[:, None] == head_of[None, :]).astype(BF16)
    tril = jnp.tril(jnp.ones((CHUNK, CHUNK), BF16))
    grp_rows = jnp.arange(2 * SSD_STATE) // SSD_STATE
    seg_expand = (grp_rows[:, None] == (head_of // (SSD_HEADS // SSD_GROUPS))[None, :]).astype(BF16)

    w4 = jnp.tril(w_s[:, :DEC_SEQ, :DEC_SEQ])
    steps = jnp.arange(DEC_SEQ)
    coef = []
    for k in range(DEC_SEQ):
        src = steps - k
        ck = jnp.where((src >= 0)[None, :], w4[:, steps, jnp.maximum(src, 0)], 0.0)
        ck = jnp.repeat(ck.T, gw, axis=1)
        coef.append(jnp.concatenate([ck, ck], axis=0))
    bias4 = jnp.repeat(b_s[:, :DEC_SEQ].T, gw, axis=1)
    return dict(
        g_pre=g_pre.reshape(1, D_MODEL), w1=w1, ln_g=ln_g.reshape(1, HALF), ln_b=ln_b.reshape(1, HALF),
        ws_tril=jnp.tril(w_s).astype(BF16),
        bs_rows=jnp.broadcast_to(b_s.astype(F32)[:, :, None], (CMLP_GROUPS, CHUNK, gw)),
        coef=jnp.stack(coef).astype(F32), bias4=jnp.concatenate([bias4, bias4], axis=0).astype(F32),
        conv_w=conv_w, conv_b=conv_b.reshape(1, cd), dt_bias16=lanes16(dt_bias), a_log16=lanes16(a_log),
        a_log_e=per_channel(a_log), d_skip_e=per_channel(d_skip), gate_norm_g=gate_norm_g.reshape(1, HALF),
        expand3=jnp.concatenate([expand] * 3, axis=0), tril3=jnp.concatenate([tril] * 3, axis=1),
        seg_expand=seg_expand, w_out=w_out, g_post=g_post.reshape(1, D_MODEL))


def _layer1_prompt(x, w):
    yc = _cmlp_prompt(x, w['g_pre'], w['w1'], w['ln_g'], w['ln_b'], w['ws_tril'], w['bs_rows'])
    yd, tail, ssm = _ssd_prompt(x, w)
    y = _out_proj(yc, yd, x, w['w_out'], w['g_post'])
    return y, tail[5:8], ssm.reshape(SSD_HEADS, HEAD_DIM, SSD_STATE)


def _layer1_sample(x, conv_state, ssm_state, w):
    n_seq = x.shape[0]
    t = n_seq * DEC_SEQ
    rows = x.reshape(t, D_MODEL)
    yc, vn = _cmlp_sample(rows, w['g_pre'], w['w1'], w['ln_g'], w['ln_b'], w['coef'], w['bias4'])
    new_conv, z, ysk, eacs, xw, dec, bm, cm = _ssd_sample_pre(rows, conv_state.transpose(1, 0, 2), w)
    yd, new_state = _ssd_sample_state(ssm_state, cm, bm, xw, dec, eacs, ysk, z, w['gate_norm_g'])
    y = _out_proj(yc, yd, rows, w['w_out'], w['g_post'])
    return (y.reshape(n_seq, DEC_SEQ, D_MODEL), vn.reshape(n_seq, DEC_SEQ, HALF),
            new_conv.transpose(1, 0, 2), new_state)


def kernel(x_prompt, x_sample, state_conv_a, cache_win_k, cache_win_v, state_conv_d, state_ssm, rel_bias,
           l0_g_pre, l0_w_in, l0_conv_w, l0_sinks, l0_w_out, l0_g_post,
           l1_g_pre, l1_w_in, l1_ln_g, l1_ln_b, l1_w_s, l1_b_s, l1_conv_w, l1_conv_b, l1_dt_bias, l1_a_log,
           l1_d_skip, l1_gate_norm_g, l1_w_out, l1_g_post):
    w0 = _prep_layer0(l0_g_pre, l0_w_in, l0_conv_w, rel_bias, l0_sinks, l0_w_out, l0_g_post)
    w1 = _prep_layer1(l1_g_pre, l1_w_in, l1_ln_g, l1_ln_b, l1_w_s, l1_b_s, l1_conv_w, l1_conv_b, l1_dt_bias,
                      l1_a_log, l1_d_skip, l1_gate_norm_g, l1_w_out, l1_g_post)
    yp, p_conv_a, p_win_k, p_win_v = _layer0_prompt(x_prompt[0], w0)
    ys, s_conv_a, s_win_k, s_win_v = _layer0_sample(x_sample, state_conv_a, cache_win_k, cache_win_v, w0)
    yp, p_conv_d, p_ssm = _layer1_prompt(yp, w1)
    ys, s_chunk_v, s_conv_d, s_ssm = _layer1_sample(ys, state_conv_d, state_ssm, w1)
    return (yp[None], ys, p_conv_a[None], s_conv_a, p_win_k[None], p_win_v[None], s_win_k, s_win_v, s_chunk_v,
            p_conv_d[None], s_conv_d, p_ssm[None], s_ssm)
```

```python
import functools
import math

import jax
import jax.numpy as jnp
from jax import lax
from jax.experimental import pallas as pl
from jax.experimental.pallas import tpu as pltpu

F32 = jnp.float32
BF16 = jnp.bfloat16

D_MODEL = 2048
HALF = 1024
HEAD_DIM = 64
N_HEADS = 16
N_KV = 4
GROUP = 4
WINDOW = 128
NUM_BUCKETS = 32
MAX_DISTANCE = 128
CMLP_GROUPS = 8
CHUNK = 128
SSD_HEADS = 16
SSD_STATE = 128
SSD_GROUPS = 2
SSD_CONV_DIM = HALF + 2 * SSD_GROUPS * SSD_STATE
DEC_SEQ = 4
NORM_EPS = 1e-6
MASK_VALUE = -1e30

ROW_TILE = 512
VMEM_LIMIT = 56 * 1024 * 1024


def _params(n_axes=1):
    return pltpu.CompilerParams(dimension_semantics=("arbitrary",) * n_axes,
                                vmem_limit_bytes=VMEM_LIMIT)


def _resident(shape):
    nd = len(shape)
    return pl.BlockSpec(shape, lambda *_: (0,) * nd, pipeline_mode=pl.Buffered(1))


def _rows(tile, width):
    return pl.BlockSpec((tile, width), lambda i: (i, 0))


def _cols(rows, width, block):
    return pl.BlockSpec((rows, width), lambda *_: (0, block), pipeline_mode=pl.Buffered(1))


def _rowwin(height, cols, block):
    return pl.BlockSpec((height, cols), lambda *_: (block, 0), pipeline_mode=pl.Buffered(1))


def _rms_bf16(x, g):
    ms = jnp.mean(x * x, axis=-1, keepdims=True)
    return (x * lax.rsqrt(ms + NORM_EPS) * g).astype(BF16)


def _silu(x):
    return x * jax.nn.sigmoid(x)


def _dot(a, b):
    return jnp.dot(a, b, preferred_element_type=F32)


def _dot_nt(a, b):
    return lax.dot_general(a, b, (((1,), (1,)), ((), ())), preferred_element_type=F32)


def _dot_tn(a, b):
    return lax.dot_general(a, b, (((0,), (0,)), ((), ())), preferred_element_type=F32)


def _dot_w(a, w):
    return _dot(a, w.astype(BF16))


def _dot_wt(a, wt):
    return _dot_nt(a, wt.astype(BF16))


W_PIECE = 256


STREAM_DEPTH = 3


def _stream(w_hbm, scr, sem, windows):
    copies = [pltpu.make_async_copy(w_hbm.at[win], scr.at[n], sem.at[n]) for n, win in enumerate(windows)]
    for cp in copies[:STREAM_DEPTH]:
        cp.start()
    return copies


def _arrive(copies, n):
    copies[n].wait()
    if n + STREAM_DEPTH < len(copies):
        copies[n + STREAM_DEPTH].start()


def _col_piece(start):
    return (slice(None), slice(start, start + W_PIECE))


def _row_piece(start):
    return (slice(start, start + W_PIECE), slice(None))


_HBM = pl.BlockSpec(memory_space=pl.ANY)


def _split3(x):
    hi = x.astype(BF16)
    r1 = x - hi.astype(F32)
    mid = r1.astype(BF16)
    lo = (r1 - mid.astype(F32)).astype(BF16)
    return hi, mid, lo


def _place_steps(t, placements):
    n_seq = placements[0][1].shape[0]
    row = lax.broadcasted_iota(jnp.int32, (t, n_seq), 0)
    seq = lax.broadcasted_iota(jnp.int32, (t, n_seq), 1)
    lhs, rhs = [], []
    for step, state in placements:
        sel = jnp.where(row == DEC_SEQ * seq + step, 1.0, 0.0).astype(BF16)
        lhs += [sel] * 3
        rhs += list(_split3(state))
    return _dot(jnp.concatenate(lhs, axis=1), jnp.concatenate(rhs, axis=0))


def _shift_rows(x, k, prev_rows=None):
    r = pltpu.roll(x, k, 0)
    if prev_rows is None:
        return r
    head = r[0:8, :]
    row = lax.broadcasted_iota(jnp.int32, head.shape, 0)
    n_prev = prev_rows.shape[0]
    for t in range(k):
        src = n_prev - k + t
        head = jnp.where(row == t, prev_rows[src:src + 1, :], head)
    return jnp.concatenate([head, r[8:, :]], axis=0)


def _out_proj_kernel(ya_ref, yb_ref, x_ref, w_ref, g_ref, o_ref):
    y = _dot_w(ya_ref[...], w_ref[0:HALF, :]) + _dot_w(yb_ref[...], w_ref[HALF:2 * HALF, :])
    ms = jnp.mean(y * y, axis=-1, keepdims=True)
    o_ref[...] = x_ref[...] + y * lax.rsqrt(ms + NORM_EPS) * g_ref[...]


def _out_proj_stream_kernel(ya_ref, yb_ref, x_ref, w_hbm, g_ref, o_ref, w_scr, w_sem):
    n = D_MODEL // W_PIECE
    copies = _stream(w_hbm, w_scr, w_sem, [_col_piece(c * W_PIECE) for c in range(n)])
    ya, yb = ya_ref[...], yb_ref[...]
    parts = []
    for c in range(n):
        _arrive(copies, c)
        parts.append(_dot_w(ya, w_scr[c, 0:HALF, :]) + _dot_w(yb, w_scr[c, HALF:2 * HALF, :]))
    y = jnp.concatenate(parts, axis=1)
    ms = jnp.mean(y * y, axis=-1, keepdims=True)
    o_ref[...] = x_ref[...] + y * lax.rsqrt(ms + NORM_EPS) * g_ref[...]


def _out_proj(ya, yb, x, w_bf, g):
    t = x.shape[0]
    tile = min(ROW_TILE, t)
    if t == tile:
        n = D_MODEL // W_PIECE
        return pl.pallas_call(
            _out_proj_stream_kernel,
            grid=(1,),
            in_specs=[_resident((t, HALF)), _resident((t, HALF)), _resident((t, D_MODEL)), _HBM,
                      _resident((1, D_MODEL))],
            out_specs=_resident((t, D_MODEL)),
            out_shape=jax.ShapeDtypeStruct((t, D_MODEL), F32),
            scratch_shapes=[pltpu.VMEM((n, 2 * HALF, W_PIECE), F32), pltpu.SemaphoreType.DMA((n,))],
            compiler_params=_params(),
            name="out_proj_sample",
        )(ya, yb, x, w_bf, g)
    return pl.pallas_call(
        _out_proj_kernel,
        grid=(t // tile,),
        in_specs=[_rows(tile, HALF), _rows(tile, HALF), _rows(tile, D_MODEL),
                  _resident((2 * HALF, D_MODEL)), _resident((1, D_MODEL))],
        out_specs=_rows(tile, D_MODEL),
        out_shape=jax.ShapeDtypeStruct((t, D_MODEL), F32),
        compiler_params=_params(),
        name="out_proj",
    )(ya, yb, x, w_bf, g)


CONV_A_CHUNK = 256


def _conv_a_kernel(*refs, sample):
    if sample:
        x_ref, g_ref, w_hbm, cw_ref, st_ref, ya_ref, s_ref, w_scr, w_sem = refs
    else:
        x_ref, g_ref, w_ref, cw_ref, ya_ref, s_ref = refs
    tile = x_ref.shape[0]
    cc = CONV_A_CHUNK
    n_c = HALF // cc
    if sample:
        copies = _stream(w_hbm, w_scr, w_sem, [_col_piece(j * HALF + c * cc) for c in range(n_c) for j in range(4)])
    h = _rms_bf16(x_ref[...], g_ref[...])
    if not sample:
        @pl.when(pl.program_id(0) == 0)
        def _():
            s_ref[...] = jnp.zeros_like(s_ref)
    for c in range(n_c):
        lanes = slice(c * cc, (c + 1) * cc)
        if sample:
            parts = []
            for j in range(4):
                _arrive(copies, 4 * c + j)
                parts.append(_dot_w(h, w_scr[4 * c + j]))
            a_b, a_c, a_h, a_g = parts
        else:
            a_b, a_c, a_h, a_g = (_dot_w(h, w_ref[:, j * HALF + c * cc:j * HALF + (c + 1) * cc]) for j in range(4))
        s = a_c * a_h
        if sample:
            t_in = lax.broadcasted_iota(jnp.int32, s.shape, 0) % DEC_SEQ
            old0 = st_ref[:, c * cc:(c + 1) * cc]
            old1 = st_ref[:, HALF + c * cc:HALF + (c + 1) * cc]
            p1 = jnp.where(t_in >= 1, _shift_rows(s, 1), 0.0) + _place_steps(tile, [(0, old1)])
            p2 = jnp.where(t_in >= 2, _shift_rows(s, 2), 0.0) + _place_steps(tile, [(0, old0), (1, old1)])
            s_ref[:, lanes] = s
        else:
            prev = s_ref[:, lanes]
            p1 = _shift_rows(s, 1, prev)
            p2 = _shift_rows(s, 2, prev)
            s_ref[:, lanes] = s[tile - 8:tile, :]
        conv = p2 * cw_ref[0:1, lanes] + p1 * cw_ref[1:2, lanes] + s * cw_ref[2:3, lanes]
        ya_ref[:, lanes] = (a_b * conv * _silu(a_g)).astype(BF16)


def _conv_a(x, g_pre, w0, conv_w, state=None):
    t = x.shape[0]
    sample = state is not None
    tile = t if sample else min(ROW_TILE, t)
    w_spec = _HBM if sample else _cols(D_MODEL, 4 * HALF, 0)
    in_specs = [_rows(tile, D_MODEL), _resident((1, D_MODEL)), w_spec, _resident((3, HALF))]
    args = [x, g_pre, w0, conv_w]
    scratch = []
    if sample:
        in_specs.append(_resident(state.shape))
        args.append(state)
        s_spec, s_shape = _rows(tile, HALF), (t, HALF)
        n_pieces = 4 * HALF // W_PIECE
        scratch = [pltpu.VMEM((n_pieces, D_MODEL, W_PIECE), F32), pltpu.SemaphoreType.DMA((n_pieces,))]
    else:
        s_spec, s_shape = pl.BlockSpec((8, HALF), lambda i: (0, 0)), (8, HALF)
    return pl.pallas_call(
        functools.partial(_conv_a_kernel, sample=sample),
        grid=(t // tile,),
        in_specs=in_specs,
        out_specs=[_rows(tile, HALF), s_spec],
        out_shape=[jax.ShapeDtypeStruct((t, HALF), BF16), jax.ShapeDtypeStruct(s_shape, F32)],
        scratch_shapes=scratch,
        compiler_params=_params(),
        name="conv_a_sample" if sample else "conv_a_prompt",
    )(*args)


def _rel_bucket(dist):
    max_exact = NUM_BUCKETS // 2
    d = jnp.maximum(dist, 0)
    ratio = jnp.maximum(d, max_exact).astype(F32) / max_exact
    large = max_exact + (jnp.log(ratio) / math.log(MAX_DISTANCE / max_exact)
                         * (NUM_BUCKETS - max_exact)).astype(jnp.int32)
    return jnp.where(d < max_exact, d, jnp.minimum(large, NUM_BUCKETS - 1))


def _attn_softmax_pv(s, sink, v_bf, v_transposed=False):
    m = jnp.maximum(jnp.max(s, axis=-1, keepdims=True), sink)
    p = jnp.exp(s - m)
    den = jnp.sum(p, axis=-1, keepdims=True) + jnp.exp(sink - m)
    pv = _dot_nt(p.astype(BF16), v_bf) if v_transposed else _dot(p.astype(BF16), v_bf)
    return pv / den


def _attn_prompt_kernel(x_ref, g_ref, wq_ref, wkv_ref, wg0_ref, wg1_ref, tab_ref, sink_ref, yb_ref, kwin_ref,
                        vwin_ref, q_scr, gate_scr, k_scr, v_scr, bias_scr):
    tile = x_ref.shape[0]
    i = pl.program_id(0)
    kv_w = N_KV * HEAD_DIM
    h = _rms_bf16(x_ref[...], g_ref[...])

    kw, vw = 2 * HEAD_DIM, 4 * HEAD_DIM

    @pl.when(i == 0)
    def _():
        k_scr[0:WINDOW, :] = jnp.zeros((WINDOW, N_KV * kw), BF16)
        v_scr[0:WINDOW, :] = jnp.zeros((WINDOW, N_KV * vw), BF16)
        for hk in range(N_KV):
            v_scr[:, hk * vw + kw:(hk + 1) * vw] = jnp.ones((tile + WINDOW, kw), BF16)
        in_own = lax.broadcasted_iota(jnp.int32, (WINDOW, 2 * WINDOW), 1) >= WINDOW
        for head in range(N_HEADS):
            row = jnp.broadcast_to(tab_ref[head:head + 1, :], (WINDOW, BIAS_SPAN))
            band = pltpu.roll(row, 0, 1, stride=1, stride_axis=0)[:, 0:2 * WINDOW]
            rows = slice((head % 2) * WINDOW, (head % 2 + 1) * WINDOW)
            bias_scr[1, head // 2, rows, :] = band
            bias_scr[0, head // 2, rows, :] = jnp.where(in_own, band, MASK_VALUE)

    q_scr[...] = (_dot_w(h, wq_ref[...]) * (HEAD_DIM ** -0.5)).astype(BF16)
    k = _dot_w(h, wkv_ref[:, 0:kv_w])
    v = _dot_w(h, wkv_ref[:, kv_w:2 * kv_w])
    gate_scr[:, 0:HALF // 2] = _silu(_dot_w(h, wg0_ref[...]))
    gate_scr[:, HALF // 2:HALF] = _silu(_dot_w(h, wg1_ref[...]))
    for hk in range(N_KV):
        k_h = k[:, hk * HEAD_DIM:(hk + 1) * HEAD_DIM].astype(BF16)
        v_h = v[:, hk * HEAD_DIM:(hk + 1) * HEAD_DIM].astype(BF16)
        k_scr[WINDOW:WINDOW + tile, hk * kw:(hk + 1) * kw] = jnp.concatenate([k_h, k_h], axis=1)
        v_scr[WINDOW:WINDOW + tile, hk * vw:hk * vw + kw] = jnp.concatenate([v_h, v_h], axis=1)
    kwin_ref[...] = k[tile - WINDOW:tile, :]
    vwin_ref[...] = v[tile - WINDOW:tile, :]

    lane = lax.broadcasted_iota(jnp.int32, (WINDOW, kw), 1)
    lo = lane < HEAD_DIM
    keep_a = jnp.where(lo, 1.0, 0.0).astype(BF16)
    keep_b = jnp.where(lo, 0.0, 1.0).astype(BF16)
    is_a = lax.broadcasted_iota(jnp.int32, (2 * WINDOW, 1), 0) < WINDOW

    def block(n, carry):
        r0 = pl.multiple_of(n * WINDOW, WINDOW)
        rows = pl.ds(r0, WINDOW)
        keys = pl.ds(r0, 2 * WINDOW)
        first = jnp.where(jnp.logical_and(i == 0, n == 0), 0, 1)
        for hk in range(N_KV):
            for gp in range(GROUP // 2):
                a = hk * GROUP + 2 * gp
                slab = slice(a * HEAD_DIM, (a + 2) * HEAD_DIM)
                q2 = q_scr[rows, slab]
                lhs = jnp.concatenate([q2 * keep_a, q2 * keep_b], axis=0)
                s = _dot_nt(lhs, k_scr[keys, hk * kw:(hk + 1) * kw]) + bias_scr[first, a // 2]
                sink = jnp.where(is_a, sink_ref[a], sink_ref[a + 1])
                m = jnp.maximum(jnp.max(s, axis=-1, keepdims=True), sink)
                p = jnp.exp(s - m).astype(BF16)
                pv = _dot(p, v_scr[keys, hk * vw:(hk + 1) * vw])
                num = jnp.where(lo, pv[0:WINDOW, 0:kw], pv[WINDOW:2 * WINDOW, 0:kw])
                den = jnp.where(lo, pv[0:WINDOW, kw:2 * kw], pv[WINDOW:2 * WINDOW, kw:2 * kw])
                m_slab = jnp.where(lo, m[0:WINDOW], m[WINDOW:2 * WINDOW])
                den = den + jnp.exp(jnp.where(lo, sink_ref[a], sink_ref[a + 1]) - m_slab)
                yb_ref[rows, slab] = (num / den * gate_scr[rows, slab]).astype(BF16)
        return carry

    lax.fori_loop(0, tile // WINDOW, block, 0)
    k_scr[0:WINDOW, :] = k_scr[tile:tile + WINDOW, :]
    v_scr[0:WINDOW, :] = v_scr[tile:tile + WINDOW, :]


BIAS_SPAN = 3 * WINDOW


def _prompt_bias_table(rel_bias):
    dist = WINDOW - jnp.arange(BIAS_SPAN)
    table = jnp.where(((dist >= 0) & (dist < WINDOW))[:, None], rel_bias.astype(F32)[_rel_bucket(dist)], MASK_VALUE)
    return table.T


def _attn_prompt(x, g_pre, w0, table, sinks):
    t = x.shape[0]
    tile = min(ROW_TILE, t)
    kv_w = N_KV * HEAD_DIM
    win_spec = pl.BlockSpec((WINDOW, kv_w), lambda i: (0, 0))
    return pl.pallas_call(
        _attn_prompt_kernel,
        grid=(t // tile,),
        in_specs=[_rows(tile, D_MODEL), _resident((1, D_MODEL)),
                  _cols(D_MODEL, HALF, 4), _cols(D_MODEL, 2 * kv_w, 10),
                  _cols(D_MODEL, HALF // 2, 11), _cols(D_MODEL, HALF // 2, 12),
                  _resident(table.shape), pl.BlockSpec(memory_space=pltpu.SMEM)],
        out_specs=[_rows(tile, HALF), win_spec, win_spec],
        out_shape=[jax.ShapeDtypeStruct((t, HALF), BF16),
                   jax.ShapeDtypeStruct((WINDOW, kv_w), F32), jax.ShapeDtypeStruct((WINDOW, kv_w), F32)],
        scratch_shapes=[pltpu.VMEM((tile, HALF), BF16), pltpu.VMEM((tile, HALF), F32),
                        pltpu.VMEM((tile + WINDOW, 2 * kv_w), BF16), pltpu.VMEM((tile + WINDOW, 4 * kv_w), BF16),
                        pltpu.VMEM((2, N_HEADS // 2, 2 * WINDOW, 2 * WINDOW), F32)],
        compiler_params=_params(),
        name="attn_prompt",
    )(x, g_pre, w0, w0, w0, w0, table, sinks)


def _attn_proj_kernel(x_ref, g_ref, w_hbm, qg_ref, kt_ref, vt_ref, kv_scr, w_scr, w_sem):
    kv_w = N_KV * HEAD_DIM
    n_q = HALF // W_PIECE
    copies = _stream(w_hbm, w_scr, w_sem, [_col_piece(4 * HALF + n * W_PIECE) for n in range(2 * HALF // W_PIECE + 2)])
    h = _rms_bf16(x_ref[...], g_ref[...])
    for hk in range(n_q):
        _arrive(copies, hk)
        q = _dot_w(h, w_scr[hk]) * (HEAD_DIM ** -0.5)
        for g in range(GROUP):
            dst = (g * N_KV + hk) * HEAD_DIM
            qg_ref[:, dst:dst + HEAD_DIM] = q[:, g * HEAD_DIM:(g + 1) * HEAD_DIM]
    for n in range(2):
        _arrive(copies, n_q + n)
        kv_scr[:, n * kv_w:(n + 1) * kv_w] = _dot_w(h, w_scr[n_q + n])
    for j in range(kt_ref.shape[0]):
        kt_ref[j] = kv_scr[j * WINDOW:(j + 1) * WINDOW, 0:kv_w].T
        vt_ref[j] = kv_scr[j * WINDOW:(j + 1) * WINDOW, kv_w:2 * kv_w].T
    for n in range(n_q):
        _arrive(copies, n_q + 2 + n)
        qg_ref[:, HALF + n * W_PIECE:HALF + (n + 1) * W_PIECE] = _dot_w(h, w_scr[n_q + 2 + n])


def _attn_proj(x, g_pre, w0):
    t = x.shape[0]
    kv_w = N_KV * HEAD_DIM
    n_pieces = 2 * HALF // W_PIECE + 2
    out_shape = [jax.ShapeDtypeStruct((t, 2 * HALF), F32), jax.ShapeDtypeStruct((t // WINDOW, kv_w, WINDOW), F32),
                 jax.ShapeDtypeStruct((t // WINDOW, kv_w, WINDOW), F32)]
    return pl.pallas_call(
        _attn_proj_kernel,
        grid=(1,),
        in_specs=[_resident((t, D_MODEL)), _resident((1, D_MODEL)), _HBM],
        out_specs=[_resident(s.shape) for s in out_shape],
        out_shape=out_shape,
        scratch_shapes=[pltpu.VMEM((t, 2 * kv_w), F32), pltpu.VMEM((n_pieces, D_MODEL, W_PIECE), F32),
                        pltpu.SemaphoreType.DMA((n_pieces,))],
        compiler_params=_params(),
        name="attn_proj_sample",
    )(x, g_pre, w0)


ATTN_S_BATCH = 16
KEYS_PAD = 2 * WINDOW


def _attn_sample_kernel(qg_ref, ktn_ref, vtn_ref, ck_ref, cv_ref, bias_ref, sink_ref, yb_ref, nk_ref, nv_ref):
    kv_w = N_KV * HEAD_DIM
    row8 = lax.broadcasted_iota(jnp.int32, (8, kv_w), 0)
    lane_head = lax.broadcasted_iota(jnp.int32, (8, kv_w), 1) // HEAD_DIM
    lower = row8 < DEC_SEQ
    pick = [jnp.where(lane_head == 2 * hp + jnp.where(lower, 0, 1), 1.0, 0.0).astype(F32) for hp in range(2)]
    lower_w = lax.broadcasted_iota(jnp.int32, (8, HALF), 0) < DEC_SEQ
    kept = lax.broadcasted_iota(jnp.int32, (kv_w, WINDOW), 1) < WINDOW - DEC_SEQ
    seq0 = pl.program_id(0) * ATTN_S_BATCH
    per_tile = WINDOW // DEC_SEQ

    def slide(old, new_tile, shift):
        return jnp.where(kept, pltpu.roll(old, WINDOW - DEC_SEQ, 1), pltpu.roll(new_tile, shift, 1))

    def pair(p, carry):
        r0 = pl.multiple_of(p * 8, 8)
        rows = qg_ref[pl.ds(r0, 8), :]
        q8 = rows[:, 0:HALF]
        gate8 = rows[:, HALF:2 * HALF]
        out8 = []
        for sub in range(2):
            b = 2 * p + sub
            q_swap = pltpu.roll(q8, 4, 0)
            q_dup = jnp.where(lower_w, q8, q_swap) if sub == 0 else jnp.where(lower_w, q_swap, q8)
            tile = (seq0 + b) // per_tile
            shift = (2 * WINDOW - DEC_SEQ - DEC_SEQ * ((seq0 + b) % per_tile)) % WINDOW
            k_old = ck_ref[b].reshape(kv_w, WINDOW)
            v_old = cv_ref[b].reshape(kv_w, WINDOW)
            k_win = slide(k_old, ktn_ref[tile], shift)
            v_win = slide(v_old, vtn_ref[tile], shift)
            nk_ref[b] = k_win.reshape(N_KV, HEAD_DIM, WINDOW)
            nv_ref[b] = v_win.reshape(N_KV, HEAD_DIM, WINDOW)
            k_all = jnp.concatenate([k_old, k_win], axis=1).astype(BF16)
            v_all = jnp.concatenate([v_old, v_win], axis=1).astype(BF16)
            q_bd = jnp.concatenate(
                [q_dup[:, g * kv_w:(g + 1) * kv_w] * pick[hp] for g in range(GROUP) for hp in range(2)], axis=0)
            s = _dot(q_bd.astype(BF16), k_all) + bias_ref[...]
            o = _attn_softmax_pv(s, sink_ref[:, 0:1], v_all, v_transposed=True)
            out_g = []
            for g in range(GROUP):
                acc = None
                for hp in range(2):
                    piece = o[(2 * g + hp) * 8:(2 * g + hp + 1) * 8, :] * pick[hp]
                    piece = piece + pltpu.roll(piece, 4, 0)
                    acc = piece if acc is None else acc + piece
                out_g.append(acc)
            out8.append(jnp.concatenate(
                [out_g[g][:, hk * HEAD_DIM:(hk + 1) * HEAD_DIM] for hk in range(N_KV) for g in range(GROUP)], axis=1))
        o8 = jnp.where(lower_w, out8[0], out8[1])
        yb_ref[pl.ds(r0, 8), :] = (o8 * _silu(gate8)).astype(BF16)
        return carry

    lax.fori_loop(0, ATTN_S_BATCH // 2, pair, 0)


def _sample_bias(rel_bias, sinks):
    t = jnp.arange(DEC_SEQ)[:, None]
    j = jnp.arange(KEYS_PAD)[None, :]
    pos = jnp.where(j < WINDOW, j, j - (KEYS_PAD - DEC_SEQ) + WINDOW)
    dist = t + WINDOW - pos
    valid = (dist >= 0) & (dist < WINDOW) & ((j < WINDOW) | (j >= KEYS_PAD - DEC_SEQ))
    bias = jnp.where(valid[:, :, None], rel_bias.astype(F32)[_rel_bucket(dist)], MASK_VALUE)
    bias = bias.reshape(DEC_SEQ, KEYS_PAD, N_KV, GROUP).transpose(3, 2, 0, 1).reshape(N_HEADS * DEC_SEQ, KEYS_PAD)
    sink = jnp.broadcast_to(sinks.astype(F32).reshape(N_KV, GROUP).T[:, :, None], (GROUP, N_KV, DEC_SEQ))
    return bias, jnp.broadcast_to(sink.reshape(N_HEADS * DEC_SEQ, 1), (N_HEADS * DEC_SEQ, 128))


def _attn_sample(qg, kt_new, vt_new, cache_kt, cache_vt, bias, sink):
    n_seq = cache_kt.shape[0]
    bb = ATTN_S_BATCH
    cache_spec = pl.BlockSpec((bb, N_KV, HEAD_DIM, WINDOW), lambda i: (i, 0, 0, 0))
    return pl.pallas_call(
        _attn_sample_kernel,
        grid=(n_seq // bb,),
        in_specs=[_rows(bb * DEC_SEQ, 2 * HALF), _resident(kt_new.shape), _resident(vt_new.shape),
                  cache_spec, cache_spec, _resident(bias.shape), _resident(sink.shape)],
        out_specs=[_rows(bb * DEC_SEQ, HALF), cache_spec, cache_spec],
        out_shape=[jax.ShapeDtypeStruct((n_seq * DEC_SEQ, HALF), BF16),
                   jax.ShapeDtypeStruct(cache_kt.shape, F32), jax.ShapeDtypeStruct(cache_vt.shape, F32)],
        compiler_params=_params(),
        name="attn_sample",
    )(qg, kt_new, vt_new, cache_kt, cache_vt, bias, sink)


def _prep_layer0(g_pre, w_in, conv_w, rel_bias, sinks, w_out, g_post):
    return dict(
        g_pre=g_pre.reshape(1, D_MODEL), w0=w_in, conv_w=conv_w, rel_bias=rel_bias, sinks=sinks,
        w_out=w_out, g_post=g_post.reshape(1, D_MODEL))


def _layer0_prompt(x, w):
    ya, s_tail = _conv_a(x, w['g_pre'], w['w0'], w['conv_w'])
    yb, kwin, vwin = _attn_prompt(x, w['g_pre'], w['w0'], _prompt_bias_table(w['rel_bias']), w['sinks'])
    y = _out_proj(ya, yb, x, w['w_out'], w['g_post'])
    return (y, s_tail[6:8], kwin.reshape(WINDOW, N_KV, HEAD_DIM), vwin.reshape(WINDOW, N_KV, HEAD_DIM))


def _layer0_sample(x, conv_state, cache_k, cache_v, w):
    n_seq = x.shape[0]
    rows = x.reshape(n_seq * DEC_SEQ, D_MODEL)
    ya, s = _conv_a(rows, w['g_pre'], w['w0'], w['conv_w'], conv_state.reshape(n_seq, 2 * HALF))
    qg, kt_new, vt_new = _attn_proj(rows, w['g_pre'], w['w0'])
    bias, sink = _sample_bias(w['rel_bias'], w['sinks'])
    yb, new_kt, new_vt = _attn_sample(qg, kt_new, vt_new, cache_k.transpose(0, 2, 3, 1), cache_v.transpose(0, 2, 3, 1),
                                      bias, sink)
    y = _out_proj(ya, yb, rows, w['w_out'], w['g_post'])
    return (y.reshape(n_seq, DEC_SEQ, D_MODEL), s.reshape(n_seq, DEC_SEQ, HALF)[:, DEC_SEQ - 2:],
            new_kt.transpose(0, 3, 1, 2), new_vt.transpose(0, 3, 1, 2))


def _layer_norm(v, g, b):
    xc = v - jnp.mean(v, axis=-1, keepdims=True)
    return xc * lax.rsqrt(jnp.mean(xc * xc, axis=-1, keepdims=True) + NORM_EPS) * g + b


def _cmlp_prompt_kernel(x_ref, g_ref, w_ref, lng_ref, lnb_ref, ws_ref, bs_ref, yc_ref, vn_scr):
    tile = x_ref.shape[0]
    h = _rms_bf16(x_ref[...], g_ref[...])
    v = _dot_wt(h, w_ref[HALF:2 * HALF, :])
    vn_scr[...] = _layer_norm(v, lng_ref[...], lnb_ref[...]).astype(BF16)
    gw = HALF // CMLP_GROUPS
    cols = 2 * gw
    for cb in range(HALF // cols):
        u = _dot_wt(h, w_ref[cb * cols:(cb + 1) * cols, :])
        gate = _silu(_dot_wt(h, w_ref[2 * HALF + cb * cols:2 * HALF + (cb + 1) * cols, :]))
        for gi in range(2):
            grp = 2 * cb + gi
            lanes = slice(grp * gw, (grp + 1) * gw)
            for n in range(tile // CHUNK):
                rows = slice(n * CHUNK, (n + 1) * CHUNK)
                mixed = _dot(ws_ref[grp], vn_scr[rows, lanes]) + bs_ref[grp]
                yc_ref[rows, lanes] = (u[rows, gi * gw:(gi + 1) * gw] * mixed
                                       * gate[rows, gi * gw:(gi + 1) * gw]).astype(BF16)


def _cmlp_prompt(x, g_pre, w_c, ln_g, ln_b, ws_tril, bs_rows):
    t = x.shape[0]
    tile = min(ROW_TILE, t)
    return pl.pallas_call(
        _cmlp_prompt_kernel,
        grid=(t // tile,),
        in_specs=[_rows(tile, D_MODEL), _resident((1, D_MODEL)), _rowwin(3 * HALF, D_MODEL, 0),
                  _resident((1, HALF)), _resident((1, HALF)), _resident(ws_tril.shape), _resident(bs_rows.shape)],
        out_specs=_rows(tile, HALF),
        out_shape=jax.ShapeDtypeStruct((t, HALF), BF16),
        scratch_shapes=[pltpu.VMEM((tile, HALF), BF16)],
        compiler_params=_params(),
        name="cmlp_prompt",
    )(x, g_pre, w_c, ln_g, ln_b, ws_tril, bs_rows)


def _cmlp_sample_kernel(x_ref, g_ref, w_hbm, lng_ref, lnb_ref, coef_ref, bias_ref, yc_ref, vn_ref, w_scr, w_sem):
    t = x_ref.shape[0]
    n = HALF // W_PIECE
    order = [HALF + c * W_PIECE for c in range(n)] + [c * W_PIECE for c in range(n)] \
        + [2 * HALF + c * W_PIECE for c in range(n)]
    copies = _stream(w_hbm, w_scr, w_sem, [_row_piece(r) for r in order])
    h = _rms_bf16(x_ref[...], g_ref[...])

    def proj(first):
        parts = []
        for c in range(n):
            _arrive(copies, first + c)
            parts.append(_dot_wt(h, w_scr[first + c]))
        return jnp.concatenate(parts, axis=1)

    vn = _layer_norm(proj(0), lng_ref[...], lnb_ref[...])
    u = proj(n)
    gate = _silu(proj(2 * n))
    vn_ref[...] = vn

    def tiled(a):
        return a.reshape(t // 8, 8, HALF)

    mixed = tiled(vn) * coef_ref[0][None] + bias_ref[...][None]
    for k in range(1, DEC_SEQ):
        mixed = mixed + tiled(pltpu.roll(vn, k, 0)) * coef_ref[k][None]
    yc_ref[...] = (u * mixed.reshape(t, HALF) * gate).astype(BF16)


def _cmlp_sample(x, g_pre, w_c, ln_g, ln_b, coef, bias):
    t = x.shape[0]
    return pl.pallas_call(
        _cmlp_sample_kernel,
        grid=(1,),
        in_specs=[_resident((t, D_MODEL)), _resident((1, D_MODEL)), _HBM,
                  _resident((1, HALF)), _resident((1, HALF)), _resident(coef.shape), _resident(bias.shape)],
        out_specs=[_resident((t, HALF)), _resident((t, HALF))],
        out_shape=[jax.ShapeDtypeStruct((t, HALF), BF16), jax.ShapeDtypeStruct((t, HALF), F32)],
        scratch_shapes=[pltpu.VMEM((3 * HALF // W_PIECE, W_PIECE, D_MODEL), F32),
                        pltpu.SemaphoreType.DMA((3 * HALF // W_PIECE,))],
        compiler_params=_params(),
        name="cmlp_sample",
    )(x, g_pre, w_c, ln_g, ln_b, coef, bias)


HEAD_LANES = 128
SSD_GW = HALF // SSD_GROUPS


def _softplus(x):
    return jnp.maximum(x, 0.0) + jnp.log1p(jnp.exp(-jnp.abs(x)))


def _dt_proj(h, wdt_ref):
    pad = jnp.zeros((HEAD_LANES - SSD_HEADS, D_MODEL), F32)
    return _dot_wt(h, jnp.concatenate([wdt_ref[...], pad], axis=0))


def _group_norm_gate(y, z, gn):
    gated = y * _silu(z)
    parts = []
    for g in range(SSD_GROUPS):
        part = gated[:, g * SSD_GW:(g + 1) * SSD_GW]
        parts.append(part * lax.rsqrt(jnp.mean(part * part, axis=-1, keepdims=True) + NORM_EPS))
    return (jnp.concatenate(parts, axis=1) * gn).astype(BF16)


def _ssd_prompt_kernel(x_ref, g_ref, wz_ref, wx0_ref, wx1_ref, wx2_ref, wdt_ref, cw_ref, cb_ref, dtb_ref, alog_ref,
                       aloge_ref, dskip_ref, gn_ref, e3_ref, tril3_ref, yd_ref, tail_ref, ssm_ref,
                       xbc_scr, z_scr, dt_scr, ht_scr, shift_scr):
    tile = x_ref.shape[0]
    i = pl.program_id(0)
    cd = SSD_CONV_DIM
    h = _rms_bf16(x_ref[...], g_ref[...])

    @pl.when(i == 0)
    def _():
        tail_ref[...] = jnp.zeros_like(tail_ref)
        ht_scr[...] = jnp.zeros_like(ht_scr)

    z_scr[...] = _dot_wt(h, wz_ref[...])
    dt_scr[...] = _softplus(_dt_proj(h, wdt_ref) + dtb_ref[...])
    third = cd // 3
    for j, wx_ref in enumerate((wx0_ref, wx1_ref, wx2_ref)):
        cols = slice(j * third, (j + 1) * third)
        raw = _dot_wt(h, wx_ref[...])
        shift_scr[0:8, :] = tail_ref[:, cols]
        shift_scr[8:8 + tile, :] = raw
        conv = raw * cw_ref[3:4, cols] + cb_ref[:, cols]
        for k in range(1, 4):
            conv = conv + shift_scr[8 - k:8 - k + tile, :] * cw_ref[3 - k:4 - k, cols]
        xbc_scr[:, cols] = _silu(conv)
        tail_ref[:, cols] = raw[tile - 8:tile, :]

    a16 = -jnp.exp(alog_ref[...])
    a_e = -jnp.exp(aloge_ref[...])
    causal = (lax.broadcasted_iota(jnp.int32, (CHUNK, CHUNK), 0)
              >= lax.broadcasted_iota(jnp.int32, (CHUNK, CHUNK), 1))
    first_half = lax.broadcasted_iota(jnp.int32, (CHUNK, 2 * HEAD_DIM), 1) < HEAD_DIM
    keep_a = jnp.where(first_half, 1.0, 0.0).astype(BF16)
    keep_b = jnp.where(first_half, 0.0, 1.0).astype(BF16)

    def chunk(n, carry):
        r0 = pl.multiple_of(n * CHUNK, CHUNK)
        rows = pl.ds(r0, CHUNK)
        xs = xbc_scr[rows, 0:HALF]
        dt16 = dt_scr[rows, :]
        dt_e = _dot(jnp.concatenate(_split3(dt16), axis=1), e3_ref[...])
        da_e = dt_e * a_e
        acs_e = _dot(tril3_ref[...], jnp.concatenate(_split3(da_e), axis=0))
        acs16 = _dot(tril3_ref[...], jnp.concatenate(_split3(dt16 * a16), axis=0))
        acs_t = acs16.T
        last_e = acs_e[CHUNK - 1:CHUNK, :]
        xdt = xs * dt_e
        xdt_bf = xdt.astype(BF16)
        xw = (jnp.exp(last_e - acs_e) * xdt).astype(BF16)
        dec_e = jnp.exp(last_e)
        y_parts = []
        yoff_parts = []
        for g in range(SSD_GROUPS):
            c_g = xbc_scr[rows, HALF + 2 * SSD_STATE + g * SSD_STATE:HALF + 2 * SSD_STATE + (g + 1) * SSD_STATE].astype(BF16)
            b_g = xbc_scr[rows, HALF + g * SSD_STATE:HALF + (g + 1) * SSD_STATE].astype(BF16)
            cb = _dot_nt(c_g, b_g)
            h_prev = ht_scr[g]
            yoff_parts.append(_dot(c_g, h_prev.astype(BF16)))
            for r in range(0, SSD_HEADS // SSD_GROUPS, 2):
                wgt = []
                for hd in (g * (SSD_HEADS // SSD_GROUPS) + r, g * (SSD_HEADS // SSD_GROUPS) + r + 1):
                    seg = acs16[:, hd:hd + 1] - acs_t[hd:hd + 1, :]
                    wgt.append(cb * jnp.exp(jnp.where(causal, seg, -jnp.inf)))
                a = g * (SSD_HEADS // SSD_GROUPS) + r
                slab = xdt_bf[:, a * HEAD_DIM:(a + 2) * HEAD_DIM]
                rhs = jnp.concatenate([slab * keep_a, slab * keep_b], axis=0)
                y_parts.append(_dot(jnp.concatenate(wgt, axis=1).astype(BF16), rhs))
            lanes = slice(g * SSD_GW, (g + 1) * SSD_GW)
            ht_scr[g] = h_prev * dec_e[:, lanes] + _dot_tn(b_g, xw[:, lanes])
        y = (jnp.concatenate(y_parts, axis=1) + jnp.concatenate(yoff_parts, axis=1) * jnp.exp(acs_e)
             + dskip_ref[...] * xs)
        yd_ref[rows, :] = _group_norm_gate(y, z_scr[rows, :], gn_ref[...])
        return carry

    lax.fori_loop(0, tile // CHUNK, chunk, 0)

    @pl.when(i == pl.num_programs(0) - 1)
    def _():
        for g in range(SSD_GROUPS):
            ssm_ref[g * SSD_GW:(g + 1) * SSD_GW, :] = ht_scr[g].T


def _ssd_weight_specs():
    third = SSD_CONV_DIM // 3
    first = 4 * HALF // third
    return ([_rowwin(HALF, D_MODEL, 3)] + [_rowwin(third, D_MODEL, first + j) for j in range(3)]
            + [_rowwin(SSD_HEADS, D_MODEL, (4 * HALF + SSD_CONV_DIM) // SSD_HEADS)])


def _ssd_prompt(x, w):
    t = x.shape[0]
    tile = min(ROW_TILE, t)
    cd = SSD_CONV_DIM
    consts = [w['conv_w'], w['conv_b'], w['dt_bias16'], w['a_log16'], w['a_log_e'],
              w['d_skip_e'], w['gate_norm_g'], w['expand3'], w['tril3']]
    return pl.pallas_call(
        _ssd_prompt_kernel,
        grid=(t // tile,),
        in_specs=[_rows(tile, D_MODEL), _resident((1, D_MODEL))] + _ssd_weight_specs()
                 + [_resident(c.shape) for c in consts],
        out_specs=[_rows(tile, HALF), pl.BlockSpec((8, cd), lambda i: (0, 0)),
                   pl.BlockSpec((HALF, SSD_STATE), lambda i: (0, 0))],
        out_shape=[jax.ShapeDtypeStruct((t, HALF), BF16), jax.ShapeDtypeStruct((8, cd), F32),
                   jax.ShapeDtypeStruct((HALF, SSD_STATE), F32)],
        scratch_shapes=[pltpu.VMEM((tile, cd), F32), pltpu.VMEM((tile, HALF), F32),
                        pltpu.VMEM((tile, HEAD_LANES), F32), pltpu.VMEM((SSD_GROUPS, SSD_STATE, SSD_GW), F32),
                        pltpu.VMEM((8 + tile, cd // 3), F32)],
        compiler_params=_params(),
        name="ssd_prompt",
    )(x, w['g_pre'], w['w1'], w['w1'], w['w1'], w['w1'], w['w1'], *consts)


def _ssd_sample_pre_kernel(x_ref, g_ref, wz_ref, wx0_ref, wx1_ref, wx2_ref, wdt_ref, cw_ref, cb_ref, st_ref,
                           dtb_ref, aloge_ref, dskip_ref, e3_ref, seg_ref,
                           nconv_ref, z_ref, ysk_ref, eacs_ref, xw_ref, dec_ref, b_ref, c_ref, raw_scr):
    t = x_ref.shape[0]
    n_seq = t // DEC_SEQ
    h = _rms_bf16(x_ref[...], g_ref[...])
    z_ref[...] = _dot_wt(h, wz_ref[...])
    raw = jnp.concatenate([_dot_wt(h, wx0_ref[...]), _dot_wt(h, wx1_ref[...]), _dot_wt(h, wx2_ref[...])], axis=1)
    for c in range(raw_scr.shape[0]):
        lanes = slice(c * 128, (c + 1) * 128)
        raw_scr[c] = raw[:, lanes]
        for j in range(3):
            nconv_ref[j, :, lanes] = raw_scr[c, pl.ds(j + 1, n_seq, stride=DEC_SEQ), :]
    dt16 = _softplus(_dt_proj(h, wdt_ref) + dtb_ref[...])
    dt = _dot(jnp.concatenate(_split3(dt16), axis=1), e3_ref[...])
    old = [st_ref[j] for j in range(3)]
    p1 = _place_steps(t, [(0, old[2])])
    p2 = _place_steps(t, [(0, old[1]), (1, old[2])])
    p3 = _place_steps(t, [(0, old[0]), (1, old[1]), (2, old[2])])

    def step_of(width):
        return lax.broadcasted_iota(jnp.int32, (t, width), 0) % DEC_SEQ

    def back(a, k):
        return jnp.where(step_of(a.shape[1]) >= k, pltpu.roll(a, k, 0), 0.0)

    def ahead(a, k):
        return jnp.where(step_of(a.shape[1]) + k < DEC_SEQ, pltpu.roll(a, t - k, 0), 0.0)

    conv = (raw * cw_ref[3:4, :] + (back(raw, 1) + p1) * cw_ref[2:3, :]
            + (back(raw, 2) + p2) * cw_ref[1:2, :] + (back(raw, 3) + p3) * cw_ref[0:1, :]
            + cb_ref[...])
    xbc = _silu(conv)
    xs = xbc[:, 0:HALF]
    bm = xbc[:, HALF:HALF + 2 * SSD_STATE]
    cm = xbc[:, HALF + 2 * SSD_STATE:]
    b_ref[...] = bm
    c_ref[...] = cm
    da = dt * (-jnp.exp(aloge_ref[...]))
    acs = da + back(da, 1) + back(da, 2) + back(da, 3)
    suffix = ahead(da, 1) + ahead(da, 2) + ahead(da, 3)
    xdt = xs * dt
    y = _dot((cm * bm).astype(BF16), seg_ref[...]) * xdt
    for k in range(1, DEC_SEQ):
        cbk = _dot((cm * pltpu.roll(bm, k, 0)).astype(BF16), seg_ref[...])
        term = cbk * jnp.exp(acs - pltpu.roll(acs, k, 0)) * pltpu.roll(xdt, k, 0)
        y = y + jnp.where(step_of(HALF) >= k, term, 0.0)
    ysk_ref[...] = y + dskip_ref[...] * xs
    eacs_ref[...] = jnp.exp(acs)
    xw_ref[...] = jnp.exp(suffix) * xdt
    dec_ref[...] = jnp.exp(acs + suffix)


def _ssd_sample_pre(x, conv_state, w):
    t = x.shape[0]
    cd = SSD_CONV_DIM
    tile = min(SSD_PRE_ROWS, t)
    seqs = tile // DEC_SEQ
    state_spec = pl.BlockSpec((3, seqs, cd), lambda i: (0, i, 0))
    head = [w['conv_w'], w['conv_b']]
    tail = [w['dt_bias16'], w['a_log_e'], w['d_skip_e'], w['expand3'], w['seg_expand']]
    args = [x, w['g_pre'], w['w1'], w['w1'], w['w1'], w['w1'], w['w1']] + head + [conv_state] + tail
    wide = jax.ShapeDtypeStruct((t, HALF), F32)
    narrow = jax.ShapeDtypeStruct((t, 2 * SSD_STATE), F32)
    out_shape = [jax.ShapeDtypeStruct(conv_state.shape, F32), wide, wide, wide, wide, wide, narrow, narrow]
    return pl.pallas_call(
        _ssd_sample_pre_kernel,
        grid=(t // tile,),
        in_specs=[_rows(tile, D_MODEL), _resident((1, D_MODEL))] + _ssd_weight_specs()
                 + [_resident(c.shape) for c in head] + [state_spec] + [_resident(c.shape) for c in tail],
        out_specs=[state_spec] + [_rows(tile, HALF)] * 5 + [_rows(tile, 2 * SSD_STATE)] * 2,
        out_shape=out_shape,
        scratch_shapes=[pltpu.VMEM((cd // 128, tile, 128), F32)],
        compiler_params=_params(),
        name="ssd_sample_pre",
    )(*args)


SSD_S_BATCH = 8
SSD_PRE_ROWS = 256


def _ssd_sample_state_kernel(st_ref, c_ref, b_ref, xw_ref, dec_ref, eacs_ref, ysk_ref, z_ref, gn_ref,
                             yd_ref, nst_ref):
    row_n = lax.broadcasted_iota(jnp.int32, (8, SSD_STATE), 0)
    row_w = lax.broadcasted_iota(jnp.int32, (8, SSD_GW), 0)
    row_f = lax.broadcasted_iota(jnp.int32, (8, HALF), 0)
    ones_rows = jnp.where((row_n >= 4) & (row_n < 7), 1.0, 0.0).astype(BF16)
    hpg = SSD_HEADS // SSD_GROUPS

    def pair(p, carry):
        r0 = pl.multiple_of(p * 8, 8)
        rows = pl.ds(r0, 8)
        c8 = c_ref[rows, :].astype(BF16)
        b8 = b_ref[rows, :]
        xw8 = xw_ref[rows, :]
        dec8 = dec_ref[rows, :]
        yoff = []
        for sub in range(2):
            b = 2 * p + sub
            xw_own = xw8 if sub == 0 else pltpu.roll(xw8, 4, 0)
            b_own = b8 if sub == 0 else pltpu.roll(b8, 4, 0)
            hi, mid, lo = (term.astype(F32) for term in _split3(dec8[4 * sub:4 * sub + 1, :]))
            parts = []
            for g in range(SSD_GROUPS):
                lanes = slice(g * SSD_GW, (g + 1) * SSD_GW)
                heads = pl.ds(g * hpg, hpg)
                h0 = st_ref[b, heads].reshape(SSD_GW, SSD_STATE)
                parts.append(_dot_nt(c8[:, g * SSD_STATE:(g + 1) * SSD_STATE], h0.astype(BF16)))
                lhs = jnp.where(row_w < 4, xw_own[:, lanes],
                                jnp.where(row_w == 4, hi[:, lanes],
                                          jnp.where(row_w == 5, mid[:, lanes],
                                                    jnp.where(row_w == 6, lo[:, lanes], 0.0)))).astype(BF16)
                rhs_b = jnp.where(row_n < 4, b_own[:, g * SSD_STATE:(g + 1) * SSD_STATE], 0.0).astype(BF16)
                decay = _dot_tn(lhs, ones_rows)
                nst_ref[b, heads] = (h0 * decay + _dot_tn(lhs, rhs_b)).reshape(hpg, HEAD_DIM, SSD_STATE)
            yoff.append(jnp.concatenate(parts, axis=1))
        yoff8 = jnp.where(row_f < 4, yoff[0], yoff[1])
        y = ysk_ref[rows, :] + yoff8 * eacs_ref[rows, :]
        yd_ref[rows, :] = _group_norm_gate(y, z_ref[rows, :], gn_ref[...])
        return carry

    lax.fori_loop(0, SSD_S_BATCH // 2, pair, 0)


def _ssd_sample_state(state, cm, bm, xw, dec, eacs, ysk, z, gn):
    n_seq = state.shape[0]
    bb = SSD_S_BATCH
    r = bb * DEC_SEQ
    st_spec = pl.BlockSpec((bb, SSD_HEADS, HEAD_DIM, SSD_STATE), lambda i: (i, 0, 0, 0))
    return pl.pallas_call(
        _ssd_sample_state_kernel,
        grid=(n_seq // bb,),
        in_specs=[st_spec, _rows(r, 2 * SSD_STATE), _rows(r, 2 * SSD_STATE)] + [_rows(r, HALF)] * 5
                 + [_resident((1, HALF))],
        out_specs=[_rows(r, HALF), st_spec],
        out_shape=[jax.ShapeDtypeStruct((n_seq * DEC_SEQ, HALF), BF16), jax.ShapeDtypeStruct(state.shape, F32)],
        compiler_params=_params(),
        name="ssd_sample_state",
    )(state, cm, bm, xw, dec, eacs, ysk, z, gn)


def _prep_layer1(g_pre, w_in, ln_g, ln_b, w_s, b_s, conv_w, conv_b, dt_bias, a_log, d_skip, gate_norm_g,
                 w_out, g_post):
    cd = SSD_CONV_DIM
    gw = HALF // CMLP_GROUPS
    w1 = w_in.T

    def lanes16(v):
        return jnp.pad(v.astype(F32), (0, HEAD_LANES - SSD_HEADS)).reshape(1, HEAD_LANES)

    def per_channel(v):
        return jnp.repeat(v.astype(F32), HEAD_DIM).reshape(1, HALF)

    head_of = jnp.arange(HALF) // HEAD_DIM
    expand = (jnp.arange(HEAD_LANES)[:, None] == head_of[None, :]).astype(BF16)
    tril = jnp.tril(jnp.ones((CHUNK, CHUNK), BF16))
    grp_rows = jnp.arange(2 * SSD_STATE) // SSD_STATE
    seg_expand = (grp_rows[:, None] == (head_of // (SSD_HEADS // SSD_GROUPS))[None, :]).astype(BF16)

    w4 = jnp.tril(w_s[:, :DEC_SEQ, :DEC_SEQ])
    steps = jnp.arange(DEC_SEQ)
    coef = []
    for k in range(DEC_SEQ):
        src = steps - k
        ck = jnp.where((src >= 0)[None, :], w4[:, steps, jnp.maximum(src, 0)], 0.0)
        ck = jnp.repeat(ck.T, gw, axis=1)
        coef.append(jnp.concatenate([ck, ck], axis=0))
    bias4 = jnp.repeat(b_s[:, :DEC_SEQ].T, gw, axis=1)
    return dict(
        g_pre=g_pre.reshape(1, D_MODEL), w1=w1, ln_g=ln_g.reshape(1, HALF), ln_b=ln_b.reshape(1, HALF),
        ws_tril=jnp.tril(w_s).astype(BF16),
        bs_rows=jnp.broadcast_to(b_s.astype(F32)[:, :, None], (CMLP_GROUPS, CHUNK, gw)),
        coef=jnp.stack(coef).astype(F32), bias4=jnp.concatenate([bias4, bias4], axis=0).astype(F32),
        conv_w=conv_w, conv_b=conv_b.reshape(1, cd), dt_bias16=lanes16(dt_bias), a_log16=lanes16(a_log),
        a_log_e=per_channel(a_log), d_skip_e=per_channel(d_skip), gate_norm_g=gate_norm_g.reshape(1, HALF),
        expand3=jnp.concatenate([expand] * 3, axis=0), tril3=jnp.concatenate([tril] * 3, axis=1),
        seg_expand=seg_expand, w_out=w_out, g_post=g_post.reshape(1, D_MODEL))


def _layer1_prompt(x, w):
    yc = _cmlp_prompt(x, w['g_pre'], w['w1'], w['ln_g'], w['ln_b'], w['ws_tril'], w['bs_rows'])
    yd, tail, ssm = _ssd_prompt(x, w)
    y = _out_proj(yc, yd, x, w['w_out'], w['g_post'])
    return y, tail[5:8], ssm.reshape(SSD_HEADS, HEAD_DIM, SSD_STATE)


def _layer1_sample(x, conv_state, ssm_state, w):
    n_seq = x.shape[0]
    t = n_seq * DEC_SEQ
    rows = x.reshape(t, D_MODEL)
    yc, vn = _cmlp_sample(rows, w['g_pre'], w['w1'], w['ln_g'], w['ln_b'], w['coef'], w['bias4'])
    new_conv, z, ysk, eacs, xw, dec, bm, cm = _ssd_sample_pre(rows, conv_state.transpose(1, 0, 2), w)
    yd, new_state = _ssd_sample_state(ssm_state, cm, bm, xw, dec, eacs, ysk, z, w['gate_norm_g'])
    y = _out_proj(yc, yd, rows, w['w_out'], w['g_post'])
    return (y.reshape(n_seq, DEC_SEQ, D_MODEL), vn.reshape(n_seq, DEC_SEQ, HALF),
            new_conv.transpose(1, 0, 2), new_state)


def kernel(x_prompt, x_sample, state_conv_a, cache_win_k, cache_win_v, state_conv_d, state_ssm, rel_bias,
           l0_g_pre, l0_w_in, l0_conv_w, l0_sinks, l0_w_out, l0_g_post,
           l1_g_pre, l1_w_in, l1_ln_g, l1_ln_b, l1_w_s, l1_b_s, l1_conv_w, l1_conv_b, l1_dt_bias, l1_a_log,
           l1_d_skip, l1_gate_norm_g, l1_w_out, l1_g_post):
    w0 = _prep_layer0(l0_g_pre, l0_w_in, l0_conv_w, rel_bias, l0_sinks, l0_w_out, l0_g_post)
    w1 = _prep_layer1(l1_g_pre, l1_w_in, l1_ln_g, l1_ln_b, l1_w_s, l1_b_s, l1_conv_w, l1_conv_b, l1_dt_bias,
                      l1_a_log, l1_d_skip, l1_gate_norm_g, l1_w_out, l1_g_post)
    yp, p_conv_a, p_win_k, p_win_v = _layer0_prompt(x_prompt[0], w0)
    ys, s_conv_a, s_win_k, s_win_v = _layer0_sample(x_sample, state_conv_a, cache_win_k, cache_win_v, w0)
    yp, p_conv_d, p_ssm = _layer1_prompt(yp, w1)
    ys, s_chunk_v, s_conv_d, s_ssm = _layer1_sample(ys, state_conv_d, state_ssm, w1)
    return (yp[None], ys, p_conv_a[None], s_conv_a, p_win_k[None], p_win_v[None], s_win_k, s_win_v, s_chunk_v,
            p_conv_d[None], s_conv_d, p_ssm[None], s_ssm)
```

```python
import functools
import math

import jax
import jax.numpy as jnp
from jax import lax
from jax.experimental import pallas as pl
from jax.experimental.pallas import tpu as pltpu

F32 = jnp.float32
BF16 = jnp.bfloat16

D_MODEL = 2048
HALF = 1024
HEAD_DIM = 64
N_HEADS = 16
N_KV = 4
GROUP = 4
WINDOW = 128
NUM_BUCKETS = 32
MAX_DISTANCE = 128
CMLP_GROUPS = 8
CHUNK = 128
SSD_HEADS = 16
SSD_STATE = 128
SSD_GROUPS = 2
SSD_CONV_DIM = HALF + 2 * SSD_GROUPS * SSD_STATE
DEC_SEQ = 4
NORM_EPS = 1e-6
MASK_VALUE = -1e30

ROW_TILE = 512
VMEM_LIMIT = 56 * 1024 * 1024


def _params(n_axes=1):
    return pltpu.CompilerParams(dimension_semantics=("arbitrary",) * n_axes,
                                vmem_limit_bytes=VMEM_LIMIT)


def _resident(shape):
    nd = len(shape)
    return pl.BlockSpec(shape, lambda *_: (0,) * nd, pipeline_mode=pl.Buffered(1))


def _rows(tile, width):
    return pl.BlockSpec((tile, width), lambda i: (i, 0))


def _cols(rows, width, block):
    return pl.BlockSpec((rows, width), lambda *_: (0, block), pipeline_mode=pl.Buffered(1))


def _rowwin(height, cols, block):
    return pl.BlockSpec((height, cols), lambda *_: (block, 0), pipeline_mode=pl.Buffered(1))


def _rms_bf16(x, g):
    ms = jnp.mean(x * x, axis=-1, keepdims=True)
    return (x * lax.rsqrt(ms + NORM_EPS) * g).astype(BF16)


def _silu(x):
    return x * jax.nn.sigmoid(x)


def _dot(a, b):
    return jnp.dot(a, b, preferred_element_type=F32)


def _dot_nt(a, b):
    return lax.dot_general(a, b, (((1,), (1,)), ((), ())), preferred_element_type=F32)


def _dot_tn(a, b):
    return lax.dot_general(a, b, (((0,), (0,)), ((), ())), preferred_element_type=F32)


def _dot_w(a, w):
    return _dot(a, w.astype(BF16))


def _dot_wt(a, wt):
    return _dot_nt(a, wt.astype(BF16))


def _split3(x):
    hi = x.astype(BF16)
    r1 = x - hi.astype(F32)
    mid = r1.astype(BF16)
    lo = (r1 - mid.astype(F32)).astype(BF16)
    return hi, mid, lo


def _place_steps(t, placements):
    n_seq = placements[0][1].shape[0]
    row = lax.broadcasted_iota(jnp.int32, (t, n_seq), 0)
    seq = lax.broadcasted_iota(jnp.int32, (t, n_seq), 1)
    lhs, rhs = [], []
    for step, state in placements:
        sel = jnp.where(row == DEC_SEQ * seq + step, 1.0, 0.0).astype(BF16)
        lhs += [sel] * 3
        rhs += list(_split3(state))
    return _dot(jnp.concatenate(lhs, axis=1), jnp.concatenate(rhs, axis=0))


def _out_proj_kernel(ya_ref, yb_ref, x_ref, w_ref, g_ref, o_ref):
    y = _dot_w(ya_ref[...], w_ref[0:HALF, :]) + _dot_w(yb_ref[...], w_ref[HALF:2 * HALF, :])
    ms = jnp.mean(y * y, axis=-1, keepdims=True)
    o_ref[...] = x_ref[...] + y * lax.rsqrt(ms + NORM_EPS) * g_ref[...]


def _out_proj(ya, yb, x, w_bf, g):
    t = x.shape[0]
    tile = min(ROW_TILE, t)
    return pl.pallas_call(
        _out_proj_kernel,
        grid=(t // tile,),
        in_specs=[_rows(tile, HALF), _rows(tile, HALF), _rows(tile, D_MODEL),
                  _resident((2 * HALF, D_MODEL)), _resident((1, D_MODEL))],
        out_specs=_rows(tile, D_MODEL),
        out_shape=jax.ShapeDtypeStruct((t, D_MODEL), F32),
        compiler_params=_params(),
        name="out_proj",
    )(ya, yb, x, w_bf, g)


CONV_A_CHUNK = 256


def _conv_a_kernel(*refs, sample):
    if sample:
        x_ref, g_ref, w_ref, cw_ref, st_ref, ya_ref, s_ref = refs
    else:
        x_ref, g_ref, w_ref, cw_ref, ya_ref, s_ref, shift_scr = refs
    tile = x_ref.shape[0]
    cc = CONV_A_CHUNK
    h = _rms_bf16(x_ref[...], g_ref[...])
    if not sample:
        @pl.when(pl.program_id(0) == 0)
        def _():
            s_ref[...] = jnp.zeros_like(s_ref)
    for c in range(HALF // cc):
        lanes = slice(c * cc, (c + 1) * cc)
        a_b, a_c, a_h, a_g = (_dot_w(h, w_ref[:, j * HALF + c * cc:j * HALF + (c + 1) * cc]) for j in range(4))
        s = a_c * a_h
        if sample:
            t_in = lax.broadcasted_iota(jnp.int32, s.shape, 0) % DEC_SEQ
            old0 = st_ref[:, c * cc:(c + 1) * cc]
            old1 = st_ref[:, HALF + c * cc:HALF + (c + 1) * cc]
            p1 = jnp.where(t_in >= 1, pltpu.roll(s, 1, 0), 0.0) + _place_steps(tile, [(0, old1)])
            p2 = jnp.where(t_in >= 2, pltpu.roll(s, 2, 0), 0.0) + _place_steps(tile, [(0, old0), (1, old1)])
            s_ref[:, lanes] = s
        else:
            shift_scr[0:8, :] = s_ref[:, lanes]
            shift_scr[8:8 + tile, :] = s
            p1 = shift_scr[7:7 + tile, :]
            p2 = shift_scr[6:6 + tile, :]
            s_ref[:, lanes] = s[tile - 8:tile, :]
        conv = p2 * cw_ref[0:1, lanes] + p1 * cw_ref[1:2, lanes] + s * cw_ref[2:3, lanes]
        ya_ref[:, lanes] = (a_b * conv * _silu(a_g)).astype(BF16)


def _conv_a(x, g_pre, w0, conv_w, state=None):
    t = x.shape[0]
    sample = state is not None
    tile = t if sample else min(ROW_TILE, t)
    in_specs = [_rows(tile, D_MODEL), _resident((1, D_MODEL)), _cols(D_MODEL, 4 * HALF, 0), _resident((3, HALF))]
    args = [x, g_pre, w0, conv_w]
    scratch = []
    if sample:
        in_specs.append(_resident(state.shape))
        args.append(state)
        s_spec, s_shape = _rows(tile, HALF), (t, HALF)
    else:
        s_spec, s_shape = pl.BlockSpec((8, HALF), lambda i: (0, 0)), (8, HALF)
        scratch = [pltpu.VMEM((8 + tile, CONV_A_CHUNK), F32)]
    return pl.pallas_call(
        functools.partial(_conv_a_kernel, sample=sample),
        grid=(t // tile,),
        in_specs=in_specs,
        out_specs=[_rows(tile, HALF), s_spec],
        out_shape=[jax.ShapeDtypeStruct((t, HALF), BF16), jax.ShapeDtypeStruct(s_shape, F32)],
        scratch_shapes=scratch,
        compiler_params=_params(),
        name="conv_a_sample" if sample else "conv_a_prompt",
    )(*args)


def _rel_bucket(dist):
    max_exact = NUM_BUCKETS // 2
    d = jnp.maximum(dist, 0)
    ratio = jnp.maximum(d, max_exact).astype(F32) / max_exact
    large = max_exact + (jnp.log(ratio) / math.log(MAX_DISTANCE / max_exact)
                         * (NUM_BUCKETS - max_exact)).astype(jnp.int32)
    return jnp.where(d < max_exact, d, jnp.minimum(large, NUM_BUCKETS - 1))


def _attn_softmax_pv(s, sink, v_bf, v_transposed=False):
    m = jnp.maximum(jnp.max(s, axis=-1, keepdims=True), sink)
    p = jnp.exp(s - m)
    den = jnp.sum(p, axis=-1, keepdims=True) + jnp.exp(sink - m)
    pv = _dot_nt(p.astype(BF16), v_bf) if v_transposed else _dot(p.astype(BF16), v_bf)
    return pv / den


def _attn_prompt_kernel(x_ref, g_ref, wq_ref, wkv_ref, wg0_ref, wg1_ref, tab_ref, sink_ref, yb_ref, kwin_ref,
                        vwin_ref, q_scr, gate_scr, k_scr, v_scr, bias_scr):
    tile = x_ref.shape[0]
    i = pl.program_id(0)
    kv_w = N_KV * HEAD_DIM
    h = _rms_bf16(x_ref[...], g_ref[...])

    kw, vw = 2 * HEAD_DIM, 4 * HEAD_DIM

    @pl.when(i == 0)
    def _():
        k_scr[0:WINDOW, :] = jnp.zeros((WINDOW, N_KV * kw), BF16)
        v_scr[0:WINDOW, :] = jnp.zeros((WINDOW, N_KV * vw), BF16)
        for hk in range(N_KV):
            v_scr[:, hk * vw + kw:(hk + 1) * vw] = jnp.ones((tile + WINDOW, kw), BF16)
        in_own = lax.broadcasted_iota(jnp.int32, (WINDOW, 2 * WINDOW), 1) >= WINDOW
        for head in range(N_HEADS):
            row = jnp.broadcast_to(tab_ref[head:head + 1, :], (WINDOW, BIAS_SPAN))
            band = pltpu.roll(row, 0, 1, stride=1, stride_axis=0)[:, 0:2 * WINDOW]
            rows = slice((head % 2) * WINDOW, (head % 2 + 1) * WINDOW)
            bias_scr[1, head // 2, rows, :] = band
            bias_scr[0, head // 2, rows, :] = jnp.where(in_own, band, MASK_VALUE)

    q_scr[...] = (_dot_w(h, wq_ref[...]) * (HEAD_DIM ** -0.5)).astype(BF16)
    k = _dot_w(h, wkv_ref[:, 0:kv_w])
    v = _dot_w(h, wkv_ref[:, kv_w:2 * kv_w])
    gate_scr[:, 0:HALF // 2] = _silu(_dot_w(h, wg0_ref[...]))
    gate_scr[:, HALF // 2:HALF] = _silu(_dot_w(h, wg1_ref[...]))
    for hk in range(N_KV):
        k_h = k[:, hk * HEAD_DIM:(hk + 1) * HEAD_DIM].astype(BF16)
        v_h = v[:, hk * HEAD_DIM:(hk + 1) * HEAD_DIM].astype(BF16)
        k_scr[WINDOW:WINDOW + tile, hk * kw:(hk + 1) * kw] = jnp.concatenate([k_h, k_h], axis=1)
        v_scr[WINDOW:WINDOW + tile, hk * vw:hk * vw + kw] = jnp.concatenate([v_h, v_h], axis=1)
    kwin_ref[...] = k[tile - WINDOW:tile, :]
    vwin_ref[...] = v[tile - WINDOW:tile, :]

    lane = lax.broadcasted_iota(jnp.int32, (WINDOW, kw), 1)
    lo = lane < HEAD_DIM
    keep_a = jnp.where(lo, 1.0, 0.0).astype(BF16)
    keep_b = jnp.where(lo, 0.0, 1.0).astype(BF16)
    is_a = lax.broadcasted_iota(jnp.int32, (2 * WINDOW, 1), 0) < WINDOW

    def block(n, carry):
        r0 = pl.multiple_of(n * WINDOW, WINDOW)
        rows = pl.ds(r0, WINDOW)
        keys = pl.ds(r0, 2 * WINDOW)
        first = jnp.where(jnp.logical_and(i == 0, n == 0), 0, 1)
        for hk in range(N_KV):
            for gp in range(GROUP // 2):
                a = hk * GROUP + 2 * gp
                slab = slice(a * HEAD_DIM, (a + 2) * HEAD_DIM)
                q2 = q_scr[rows, slab]
                lhs = jnp.concatenate([q2 * keep_a, q2 * keep_b], axis=0)
                s = _dot_nt(lhs, k_scr[keys, hk * kw:(hk + 1) * kw]) + bias_scr[first, a // 2]
                sink = jnp.where(is_a, sink_ref[a], sink_ref[a + 1])
                m = jnp.maximum(jnp.max(s, axis=-1, keepdims=True), sink)
                p = jnp.exp(s - m).astype(BF16)
                pv = _dot(p, v_scr[keys, hk * vw:(hk + 1) * vw])
                num = jnp.where(lo, pv[0:WINDOW, 0:kw], pv[WINDOW:2 * WINDOW, 0:kw])
                den = jnp.where(lo, pv[0:WINDOW, kw:2 * kw], pv[WINDOW:2 * WINDOW, kw:2 * kw])
                m_slab = jnp.where(lo, m[0:WINDOW], m[WINDOW:2 * WINDOW])
                den = den + jnp.exp(jnp.where(lo, sink_ref[a], sink_ref[a + 1]) - m_slab)
                yb_ref[rows, slab] = (num / den * gate_scr[rows, slab]).astype(BF16)
        return carry

    lax.fori_loop(0, tile // WINDOW, block, 0)
    k_scr[0:WINDOW, :] = k_scr[tile:tile + WINDOW, :]
    v_scr[0:WINDOW, :] = v_scr[tile:tile + WINDOW, :]


BIAS_SPAN = 3 * WINDOW


def _prompt_bias_table(rel_bias):
    dist = WINDOW - jnp.arange(BIAS_SPAN)
    table = jnp.where(((dist >= 0) & (dist < WINDOW))[:, None], rel_bias.astype(F32)[_rel_bucket(dist)], MASK_VALUE)
    return table.T


def _attn_prompt(x, g_pre, w0, table, sinks):
    t = x.shape[0]
    tile = min(ROW_TILE, t)
    kv_w = N_KV * HEAD_DIM
    win_spec = pl.BlockSpec((WINDOW, kv_w), lambda i: (0, 0))
    return pl.pallas_call(
        _attn_prompt_kernel,
        grid=(t // tile,),
        in_specs=[_rows(tile, D_MODEL), _resident((1, D_MODEL)),
                  _cols(D_MODEL, HALF, 4), _cols(D_MODEL, 2 * kv_w, 10),
                  _cols(D_MODEL, HALF // 2, 11), _cols(D_MODEL, HALF // 2, 12),
                  _resident(table.shape), pl.BlockSpec(memory_space=pltpu.SMEM)],
        out_specs=[_rows(tile, HALF), win_spec, win_spec],
        out_shape=[jax.ShapeDtypeStruct((t, HALF), BF16),
                   jax.ShapeDtypeStruct((WINDOW, kv_w), F32), jax.ShapeDtypeStruct((WINDOW, kv_w), F32)],
        scratch_shapes=[pltpu.VMEM((tile, HALF), BF16), pltpu.VMEM((tile, HALF), F32),
                        pltpu.VMEM((tile + WINDOW, 2 * kv_w), BF16), pltpu.VMEM((tile + WINDOW, 4 * kv_w), BF16),
                        pltpu.VMEM((2, N_HEADS // 2, 2 * WINDOW, 2 * WINDOW), F32)],
        compiler_params=_params(),
        name="attn_prompt",
    )(x, g_pre, w0, w0, w0, w0, table, sinks)


def _attn_proj_kernel(x_ref, g_ref, wq_ref, wkv_ref, wg0_ref, wg1_ref, qg_ref, kt_ref, vt_ref, kv_scr):
    kv_w = N_KV * HEAD_DIM
    h = _rms_bf16(x_ref[...], g_ref[...])
    q = _dot_w(h, wq_ref[...]) * (HEAD_DIM ** -0.5)
    for hk in range(N_KV):
        for g in range(GROUP):
            src = (hk * GROUP + g) * HEAD_DIM
            dst = (g * N_KV + hk) * HEAD_DIM
            qg_ref[:, dst:dst + HEAD_DIM] = q[:, src:src + HEAD_DIM]
    qg_ref[:, HALF:HALF + HALF // 2] = _dot_w(h, wg0_ref[...])
    qg_ref[:, HALF + HALF // 2:2 * HALF] = _dot_w(h, wg1_ref[...])
    kv_scr[...] = _dot_w(h, wkv_ref[...])
    for j in range(kt_ref.shape[0]):
        kt_ref[j] = kv_scr[j * WINDOW:(j + 1) * WINDOW, 0:kv_w].T
        vt_ref[j] = kv_scr[j * WINDOW:(j + 1) * WINDOW, kv_w:2 * kv_w].T


def _attn_proj(x, g_pre, w0):
    t = x.shape[0]
    kv_w = N_KV * HEAD_DIM
    out_shape = [jax.ShapeDtypeStruct((t, 2 * HALF), F32), jax.ShapeDtypeStruct((t // WINDOW, kv_w, WINDOW), F32),
                 jax.ShapeDtypeStruct((t // WINDOW, kv_w, WINDOW), F32)]
    return pl.pallas_call(
        _attn_proj_kernel,
        grid=(1,),
        in_specs=[_resident((t, D_MODEL)), _resident((1, D_MODEL)),
                  _cols(D_MODEL, HALF, 4), _cols(D_MODEL, 2 * kv_w, 10),
                  _cols(D_MODEL, HALF // 2, 11), _cols(D_MODEL, HALF // 2, 12)],
        out_specs=[_resident(s.shape) for s in out_shape],
        out_shape=out_shape,
        scratch_shapes=[pltpu.VMEM((t, 2 * kv_w), F32)],
        compiler_params=_params(),
        name="attn_proj_sample",
    )(x, g_pre, w0, w0, w0, w0)


ATTN_S_BATCH = 16
KEYS_PAD = 2 * WINDOW


def _attn_sample_kernel(qg_ref, ktn_ref, vtn_ref, ck_ref, cv_ref, bias_ref, sink_ref, yb_ref, nk_ref, nv_ref):
    kv_w = N_KV * HEAD_DIM
    row8 = lax.broadcasted_iota(jnp.int32, (8, kv_w), 0)
    lane_head = lax.broadcasted_iota(jnp.int32, (8, kv_w), 1) // HEAD_DIM
    lower = row8 < DEC_SEQ
    pick = [jnp.where(lane_head == 2 * hp + jnp.where(lower, 0, 1), 1.0, 0.0).astype(F32) for hp in range(2)]
    lower_w = lax.broadcasted_iota(jnp.int32, (8, HALF), 0) < DEC_SEQ
    kept = lax.broadcasted_iota(jnp.int32, (kv_w, WINDOW), 1) < WINDOW - DEC_SEQ
    seq0 = pl.program_id(0) * ATTN_S_BATCH
    per_tile = WINDOW // DEC_SEQ

    def slide(old, new_tile, shift):
        return jnp.where(kept, pltpu.roll(old, WINDOW - DEC_SEQ, 1), pltpu.roll(new_tile, shift, 1))

    def pair(p, carry):
        r0 = pl.multiple_of(p * 8, 8)
        rows = qg_ref[pl.ds(r0, 8), :]
        q8 = rows[:, 0:HALF]
        gate8 = rows[:, HALF:2 * HALF]
        out8 = []
        for sub in range(2):
            b = 2 * p + sub
            q_swap = pltpu.roll(q8, 4, 0)
            q_dup = jnp.where(lower_w, q8, q_swap) if sub == 0 else jnp.where(lower_w, q_swap, q8)
            tile = (seq0 + b) // per_tile
            shift = (2 * WINDOW - DEC_SEQ - DEC_SEQ * ((seq0 + b) % per_tile)) % WINDOW
            k_old = ck_ref[b].reshape(kv_w, WINDOW)
            v_old = cv_ref[b].reshape(kv_w, WINDOW)
            k_win = slide(k_old, ktn_ref[tile], shift)
            v_win = slide(v_old, vtn_ref[tile], shift)
            nk_ref[b] = k_win.reshape(N_KV, HEAD_DIM, WINDOW)
            nv_ref[b] = v_win.reshape(N_KV, HEAD_DIM, WINDOW)
            k_all = jnp.concatenate([k_old, k_win], axis=1).astype(BF16)
            v_all = jnp.concatenate([v_old, v_win], axis=1).astype(BF16)
            q_bd = jnp.concatenate(
                [q_dup[:, g * kv_w:(g + 1) * kv_w] * pick[hp] for g in range(GROUP) for hp in range(2)], axis=0)
            s = _dot(q_bd.astype(BF16), k_all) + bias_ref[...]
            o = _attn_softmax_pv(s, sink_ref[:, 0:1], v_all, v_transposed=True)
            out_g = []
            for g in range(GROUP):
                acc = None
                for hp in range(2):
                    piece = o[(2 * g + hp) * 8:(2 * g + hp + 1) * 8, :] * pick[hp]
                    piece = piece + pltpu.roll(piece, 4, 0)
                    acc = piece if acc is None else acc + piece
                out_g.append(acc)
            out8.append(jnp.concatenate(
                [out_g[g][:, hk * HEAD_DIM:(hk + 1) * HEAD_DIM] for hk in range(N_KV) for g in range(GROUP)], axis=1))
        o8 = jnp.where(lower_w, out8[0], out8[1])
        yb_ref[pl.ds(r0, 8), :] = (o8 * _silu(gate8)).astype(BF16)
        return carry

    lax.fori_loop(0, ATTN_S_BATCH // 2, pair, 0, unroll=2)


def _sample_bias(rel_bias, sinks):
    t = jnp.arange(DEC_SEQ)[:, None]
    j = jnp.arange(KEYS_PAD)[None, :]
    pos = jnp.where(j < WINDOW, j, j - (KEYS_PAD - DEC_SEQ) + WINDOW)
    dist = t + WINDOW - pos
    valid = (dist >= 0) & (dist < WINDOW) & ((j < WINDOW) | (j >= KEYS_PAD - DEC_SEQ))
    bias = jnp.where(valid[:, :, None], rel_bias.astype(F32)[_rel_bucket(dist)], MASK_VALUE)
    bias = bias.reshape(DEC_SEQ, KEYS_PAD, N_KV, GROUP).transpose(3, 2, 0, 1).reshape(N_HEADS * DEC_SEQ, KEYS_PAD)
    sink = jnp.broadcast_to(sinks.astype(F32).reshape(N_KV, GROUP).T[:, :, None], (GROUP, N_KV, DEC_SEQ))
    return bias, jnp.broadcast_to(sink.reshape(N_HEADS * DEC_SEQ, 1), (N_HEADS * DEC_SEQ, 128))


def _attn_sample(qg, kt_new, vt_new, cache_kt, cache_vt, bias, sink):
    n_seq = cache_kt.shape[0]
    bb = ATTN_S_BATCH
    cache_spec = pl.BlockSpec((bb, N_KV, HEAD_DIM, WINDOW), lambda i: (i, 0, 0, 0))
    return pl.pallas_call(
        _attn_sample_kernel,
        grid=(n_seq // bb,),
        in_specs=[_rows(bb * DEC_SEQ, 2 * HALF), _resident(kt_new.shape), _resident(vt_new.shape),
                  cache_spec, cache_spec, _resident(bias.shape), _resident(sink.shape)],
        out_specs=[_rows(bb * DEC_SEQ, HALF), cache_spec, cache_spec],
        out_shape=[jax.ShapeDtypeStruct((n_seq * DEC_SEQ, HALF), BF16),
                   jax.ShapeDtypeStruct(cache_kt.shape, F32), jax.ShapeDtypeStruct(cache_vt.shape, F32)],
        compiler_params=_params(),
        name="attn_sample",
    )(qg, kt_new, vt_new, cache_kt, cache_vt, bias, sink)


def _prep_layer0(g_pre, w_in, conv_w, rel_bias, sinks, w_out, g_post):
    return dict(
        g_pre=g_pre.reshape(1, D_MODEL), w0=w_in, conv_w=conv_w, rel_bias=rel_bias, sinks=sinks,
        w_out=w_out, g_post=g_post.reshape(1, D_MODEL))


def _layer0_prompt(x, w):
    ya, s_tail = _conv_a(x, w['g_pre'], w['w0'], w['conv_w'])
    yb, kwin, vwin = _attn_prompt(x, w['g_pre'], w['w0'], _prompt_bias_table(w['rel_bias']), w['sinks'])
    y = _out_proj(ya, yb, x, w['w_out'], w['g_post'])
    return (y, s_tail[6:8], kwin.reshape(WINDOW, N_KV, HEAD_DIM), vwin.reshape(WINDOW, N_KV, HEAD_DIM))


def _layer0_sample(x, conv_state, cache_k, cache_v, w):
    n_seq = x.shape[0]
    rows = x.reshape(n_seq * DEC_SEQ, D_MODEL)
    ya, s = _conv_a(rows, w['g_pre'], w['w0'], w['conv_w'], conv_state.reshape(n_seq, 2 * HALF))
    qg, kt_new, vt_new = _attn_proj(rows, w['g_pre'], w['w0'])
    bias, sink = _sample_bias(w['rel_bias'], w['sinks'])
    yb, new_kt, new_vt = _attn_sample(qg, kt_new, vt_new, cache_k.transpose(0, 2, 3, 1), cache_v.transpose(0, 2, 3, 1),
                                      bias, sink)
    y = _out_proj(ya, yb, rows, w['w_out'], w['g_post'])
    return (y.reshape(n_seq, DEC_SEQ, D_MODEL), s.reshape(n_seq, DEC_SEQ, HALF)[:, DEC_SEQ - 2:],
            new_kt.transpose(0, 3, 1, 2), new_vt.transpose(0, 3, 1, 2))


def _layer_norm(v, g, b):
    xc = v - jnp.mean(v, axis=-1, keepdims=True)
    return xc * lax.rsqrt(jnp.mean(xc * xc, axis=-1, keepdims=True) + NORM_EPS) * g + b


def _cmlp_prompt_kernel(x_ref, g_ref, w_ref, lng_ref, lnb_ref, ws_ref, bs_ref, yc_ref, vn_scr):
    tile = x_ref.shape[0]
    h = _rms_bf16(x_ref[...], g_ref[...])
    v = _dot_wt(h, w_ref[HALF:2 * HALF, :])
    vn_scr[...] = _layer_norm(v, lng_ref[...], lnb_ref[...]).astype(BF16)
    gw = HALF // CMLP_GROUPS
    cols = 2 * gw
    for cb in range(HALF // cols):
        u = _dot_wt(h, w_ref[cb * cols:(cb + 1) * cols, :])
        gate = _silu(_dot_wt(h, w_ref[2 * HALF + cb * cols:2 * HALF + (cb + 1) * cols, :]))
        for gi in range(2):
            grp = 2 * cb + gi
            lanes = slice(grp * gw, (grp + 1) * gw)
            for n in range(tile // CHUNK):
                rows = slice(n * CHUNK, (n + 1) * CHUNK)
                mixed = _dot(ws_ref[grp], vn_scr[rows, lanes]) + bs_ref[grp]
                yc_ref[rows, lanes] = (u[rows, gi * gw:(gi + 1) * gw] * mixed
                                       * gate[rows, gi * gw:(gi + 1) * gw]).astype(BF16)


def _cmlp_prompt(x, g_pre, w_c, ln_g, ln_b, ws_tril, bs_rows):
    t = x.shape[0]
    tile = min(ROW_TILE, t)
    return pl.pallas_call(
        _cmlp_prompt_kernel,
        grid=(t // tile,),
        in_specs=[_rows(tile, D_MODEL), _resident((1, D_MODEL)), _rowwin(3 * HALF, D_MODEL, 0),
                  _resident((1, HALF)), _resident((1, HALF)), _resident(ws_tril.shape), _resident(bs_rows.shape)],
        out_specs=_rows(tile, HALF),
        out_shape=jax.ShapeDtypeStruct((t, HALF), BF16),
        scratch_shapes=[pltpu.VMEM((tile, HALF), BF16)],
        compiler_params=_params(),
        name="cmlp_prompt",
    )(x, g_pre, w_c, ln_g, ln_b, ws_tril, bs_rows)


def _cmlp_sample_kernel(x_ref, g_ref, w_ref, lng_ref, lnb_ref, coef_ref, bias_ref, yc_ref, vn_ref):
    t = x_ref.shape[0]
    h = _rms_bf16(x_ref[...], g_ref[...])
    u = _dot_wt(h, w_ref[0:HALF, :])
    vn = _layer_norm(_dot_wt(h, w_ref[HALF:2 * HALF, :]), lng_ref[...], lnb_ref[...])
    gate = _silu(_dot_wt(h, w_ref[2 * HALF:3 * HALF, :]))
    vn_ref[...] = vn

    def tiled(a):
        return a.reshape(t // 8, 8, HALF)

    mixed = tiled(vn) * coef_ref[0][None] + bias_ref[...][None]
    for k in range(1, DEC_SEQ):
        mixed = mixed + tiled(pltpu.roll(vn, k, 0)) * coef_ref[k][None]
    yc_ref[...] = (u * mixed.reshape(t, HALF) * gate).astype(BF16)


def _cmlp_sample(x, g_pre, w_c, ln_g, ln_b, coef, bias):
    t = x.shape[0]
    return pl.pallas_call(
        _cmlp_sample_kernel,
        grid=(1,),
        in_specs=[_resident((t, D_MODEL)), _resident((1, D_MODEL)), _rowwin(3 * HALF, D_MODEL, 0),
                  _resident((1, HALF)), _resident((1, HALF)), _resident(coef.shape), _resident(bias.shape)],
        out_specs=[_resident((t, HALF)), _resident((t, HALF))],
        out_shape=[jax.ShapeDtypeStruct((t, HALF), BF16), jax.ShapeDtypeStruct((t, HALF), F32)],
        compiler_params=_params(),
        name="cmlp_sample",
    )(x, g_pre, w_c, ln_g, ln_b, coef, bias)


HEAD_LANES = 128
SSD_GW = HALF // SSD_GROUPS


def _softplus(x):
    return jnp.maximum(x, 0.0) + jnp.log1p(jnp.exp(-jnp.abs(x)))


def _dt_proj(h, wdt_ref):
    pad = jnp.zeros((HEAD_LANES - SSD_HEADS, D_MODEL), F32)
    return _dot_wt(h, jnp.concatenate([wdt_ref[...], pad], axis=0))


def _group_norm_gate(y, z, gn):
    gated = y * _silu(z)
    parts = []
    for g in range(SSD_GROUPS):
        part = gated[:, g * SSD_GW:(g + 1) * SSD_GW]
        parts.append(part * lax.rsqrt(jnp.mean(part * part, axis=-1, keepdims=True) + NORM_EPS))
    return (jnp.concatenate(parts, axis=1) * gn).astype(BF16)


def _ssd_prompt_kernel(x_ref, g_ref, wz_ref, wx0_ref, wx1_ref, wx2_ref, wdt_ref, cw_ref, cb_ref, dtb_ref, alog_ref,
                       dskip_ref, gn_ref, e3_ref, tril3_ref, yd_ref, tail_ref, ssm_ref,
                       xbc_scr, z_scr, dt_scr, ht_scr, shift_scr):
    tile = x_ref.shape[0]
    i = pl.program_id(0)
    cd = SSD_CONV_DIM
    h = _rms_bf16(x_ref[...], g_ref[...])

    @pl.when(i == 0)
    def _():
        tail_ref[...] = jnp.zeros_like(tail_ref)
        ht_scr[...] = jnp.zeros_like(ht_scr)

    z_scr[...] = _dot_wt(h, wz_ref[...])
    dt_scr[...] = _softplus(_dt_proj(h, wdt_ref) + dtb_ref[...])
    third = cd // 3
    for j, wx_ref in enumerate((wx0_ref, wx1_ref, wx2_ref)):
        cols = slice(j * third, (j + 1) * third)
        raw = _dot_wt(h, wx_ref[...])
        shift_scr[0:8, :] = tail_ref[:, cols]
        shift_scr[8:8 + tile, :] = raw
        conv = raw * cw_ref[3:4, cols] + cb_ref[:, cols]
        for k in range(1, 4):
            conv = conv + shift_scr[8 - k:8 - k + tile, :] * cw_ref[3 - k:4 - k, cols]
        xbc_scr[:, cols] = _silu(conv)
        tail_ref[:, cols] = raw[tile - 8:tile, :]

    a16 = -jnp.exp(alog_ref[...])
    causal =(lax.broadcasted_iota(jnp.int32, (CHUNK, CHUNK), 0)
              >= lax.broadcasted_iota(jnp.int32, (CHUNK, CHUNK), 1))
    first_half = lax.broadcasted_iota(jnp.int32, (CHUNK, 2 * HEAD_DIM), 1) < HEAD_DIM
    keep_a = jnp.where(first_half, 1.0, 0.0).astype(BF16)
    keep_b = jnp.where(first_half, 0.0, 1.0).astype(BF16)

    def chunk(n, carry):
        r0 = pl.multiple_of(n * CHUNK, CHUNK)
        rows = pl.ds(r0, CHUNK)
        xs = xbc_scr[rows, 0:HALF]
        dt16 = dt_scr[rows, :]
        dt_e = _dot(jnp.concatenate(_split3(dt16), axis=1), e3_ref[...])
        acs16 = _dot(tril3_ref[...], jnp.concatenate(_split3(dt16 * a16), axis=0))
        acs_e = _dot(jnp.concatenate(_split3(acs16), axis=1), e3_ref[...])
        acs_t = acs16.T
        last_e = acs_e[CHUNK - 1:CHUNK, :]
        xdt = xs * dt_e
        xdt_bf = xdt.astype(BF16)
        xw = (jnp.exp(last_e - acs_e) * xdt).astype(BF16)
        dec_e = jnp.exp(last_e)
        y_parts = []
        yoff_parts = []
        for g in range(SSD_GROUPS):
            c_g = xbc_scr[rows, HALF + 2 * SSD_STATE + g * SSD_STATE:HALF + 2 * SSD_STATE + (g + 1) * SSD_STATE].astype(BF16)
            b_g = xbc_scr[rows, HALF + g * SSD_STATE:HALF + (g + 1) * SSD_STATE].astype(BF16)
            cb = _dot_nt(c_g, b_g)
            h_prev = ht_scr[g]
            yoff_parts.append(_dot(c_g, h_prev.astype(BF16)))
            for r in range(0, SSD_HEADS // SSD_GROUPS, 2):
                wgt = []
                for hd in (g * (SSD_HEADS // SSD_GROUPS) + r, g * (SSD_HEADS // SSD_GROUPS) + r + 1):
                    seg = acs16[:, hd:hd + 1] - acs_t[hd:hd + 1, :]
                    wgt.append(cb * jnp.exp(jnp.where(causal, seg, -jnp.inf)))
                a = g * (SSD_HEADS // SSD_GROUPS) + r
                slab = xdt_bf[:, a * HEAD_DIM:(a + 2) * HEAD_DIM]
                rhs = jnp.concatenate([slab * keep_a, slab * keep_b], axis=0)
                y_parts.append(_dot(jnp.concatenate(wgt, axis=1).astype(BF16), rhs))
            lanes = slice(g * SSD_GW, (g + 1) * SSD_GW)
            ht_scr[g] = h_prev * dec_e[:, lanes] + _dot_tn(b_g, xw[:, lanes])
        y = (jnp.concatenate(y_parts, axis=1) + jnp.concatenate(yoff_parts, axis=1) * jnp.exp(acs_e)
             + dskip_ref[...] * xs)
        yd_ref[rows, :] = _group_norm_gate(y, z_scr[rows, :], gn_ref[...])
        return carry

    lax.fori_loop(0, tile // CHUNK, chunk, 0)

    @pl.when(i == pl.num_programs(0) - 1)
    def _():
        for g in range(SSD_GROUPS):
            ssm_ref[g * SSD_GW:(g + 1) * SSD_GW, :] = ht_scr[g].T


def _ssd_weight_specs():
    third = SSD_CONV_DIM // 3
    first = 4 * HALF // third
    return ([_rowwin(HALF, D_MODEL, 3)] + [_rowwin(third, D_MODEL, first + j) for j in range(3)]
            + [_rowwin(SSD_HEADS, D_MODEL, (4 * HALF + SSD_CONV_DIM) // SSD_HEADS)])


def _ssd_prompt(x, w):
    t = x.shape[0]
    tile = min(ROW_TILE, t)
    cd = SSD_CONV_DIM
    consts = [w['conv_w'], w['conv_b'], w['dt_bias16'], w['a_log16'],
              w['d_skip_e'], w['gate_norm_g'], w['expand3'], w['tril3']]
    return pl.pallas_call(
        _ssd_prompt_kernel,
        grid=(t // tile,),
        in_specs=[_rows(tile, D_MODEL), _resident((1, D_MODEL))] + _ssd_weight_specs()
                 + [_resident(c.shape) for c in consts],
        out_specs=[_rows(tile, HALF), pl.BlockSpec((8, cd), lambda i: (0, 0)),
                   pl.BlockSpec((HALF, SSD_STATE), lambda i: (0, 0))],
        out_shape=[jax.ShapeDtypeStruct((t, HALF), BF16), jax.ShapeDtypeStruct((8, cd), F32),
                   jax.ShapeDtypeStruct((HALF, SSD_STATE), F32)],
        scratch_shapes=[pltpu.VMEM((tile, cd), F32), pltpu.VMEM((tile, HALF), F32),
                        pltpu.VMEM((tile, HEAD_LANES), F32), pltpu.VMEM((SSD_GROUPS, SSD_STATE, SSD_GW), F32),
                        pltpu.VMEM((8 + tile, cd // 3), F32)],
        compiler_params=_params(),
        name="ssd_prompt",
    )(x, w['g_pre'], w['w1'], w['w1'], w['w1'], w['w1'], w['w1'], *consts)


def _ssd_sample_pre_kernel(x_ref, g_ref, wz_ref, wx0_ref, wx1_ref, wx2_ref, wdt_ref, cw_ref, cb_ref, st_ref,
                           dtb_ref, aloge_ref, dskip_ref, e3_ref, seg_ref,
                           nconv_ref, z_ref, ysk_ref, eacs_ref, xw_ref, dec_ref, b_ref, c_ref, raw_scr):
    t = x_ref.shape[0]
    n_seq = t // DEC_SEQ
    h = _rms_bf16(x_ref[...], g_ref[...])
    z_ref[...] = _dot_wt(h, wz_ref[...])
    raw = jnp.concatenate([_dot_wt(h, wx0_ref[...]), _dot_wt(h, wx1_ref[...]), _dot_wt(h, wx2_ref[...])], axis=1)
    for c in range(raw_scr.shape[0]):
        lanes = slice(c * 128, (c + 1) * 128)
        raw_scr[c] = raw[:, lanes]
        for j in range(3):
            nconv_ref[j, :, lanes] = raw_scr[c, pl.ds(j + 1, n_seq, stride=DEC_SEQ), :]
    dt16 = _softplus(_dt_proj(h, wdt_ref) + dtb_ref[...])
    dt = _dot(jnp.concatenate(_split3(dt16), axis=1), e3_ref[...])
    old = [st_ref[j] for j in range(3)]
    p1 = _place_steps(t, [(0, old[2])])
    p2 = _place_steps(t, [(0, old[1]), (1, old[2])])
    p3 = _place_steps(t, [(0, old[0]), (1, old[1]), (2, old[2])])

    def step_of(width):
        return lax.broadcasted_iota(jnp.int32, (t, width), 0) % DEC_SEQ

    def back(a, k):
        return jnp.where(step_of(a.shape[1]) >= k, pltpu.roll(a, k, 0), 0.0)

    def ahead(a, k):
        return jnp.where(step_of(a.shape[1]) + k < DEC_SEQ, pltpu.roll(a, t - k, 0), 0.0)

    conv = (raw * cw_ref[3:4, :] + (back(raw, 1) + p1) * cw_ref[2:3, :]
            + (back(raw, 2) + p2) * cw_ref[1:2, :] + (back(raw, 3) + p3) * cw_ref[0:1, :]
            + cb_ref[...])
    xbc = _silu(conv)
    xs = xbc[:, 0:HALF]
    bm = xbc[:, HALF:HALF + 2 * SSD_STATE]
    cm = xbc[:, HALF + 2 * SSD_STATE:]
    b_ref[...] = bm
    c_ref[...] = cm
    da = dt * (-jnp.exp(aloge_ref[...]))
    acs = da + back(da, 1) + back(da, 2) + back(da, 3)
    suffix = ahead(da, 1) + ahead(da, 2) + ahead(da, 3)
    xdt = xs * dt
    y = _dot((cm * bm).astype(BF16), seg_ref[...]) * xdt
    for k in range(1, DEC_SEQ):
        cbk = _dot((cm * pltpu.roll(bm, k, 0)).astype(BF16), seg_ref[...])
        term = cbk * jnp.exp(acs - pltpu.roll(acs, k, 0)) * pltpu.roll(xdt, k, 0)
        y = y + jnp.where(step_of(HALF) >= k, term, 0.0)
    ysk_ref[...] = y + dskip_ref[...] * xs
    eacs_ref[...] = jnp.exp(acs)
    xw_ref[...] = jnp.exp(suffix) * xdt
    dec_ref[...] = jnp.exp(acs + suffix)


def _ssd_sample_pre(x, conv_state, w):
    t = x.shape[0]
    cd = SSD_CONV_DIM
    tile = min(SSD_PRE_ROWS, t)
    seqs = tile // DEC_SEQ
    state_spec = pl.BlockSpec((3, seqs, cd), lambda i: (0, i, 0))
    head = [w['conv_w'], w['conv_b']]
    tail = [w['dt_bias16'], w['a_log_e'], w['d_skip_e'], w['expand3'], w['seg_expand']]
    args = [x, w['g_pre'], w['w1'], w['w1'], w['w1'], w['w1'], w['w1']] + head + [conv_state] + tail
    wide = jax.ShapeDtypeStruct((t, HALF), F32)
    narrow = jax.ShapeDtypeStruct((t, 2 * SSD_STATE), F32)
    out_shape = [jax.ShapeDtypeStruct(conv_state.shape, F32), wide, wide, wide, wide, wide, narrow, narrow]
    return pl.pallas_call(
        _ssd_sample_pre_kernel,
        grid=(t // tile,),
        in_specs=[_rows(tile, D_MODEL), _resident((1, D_MODEL))] + _ssd_weight_specs()
                 + [_resident(c.shape) for c in head] + [state_spec] + [_resident(c.shape) for c in tail],
        out_specs=[state_spec] + [_rows(tile, HALF)] * 5 + [_rows(tile, 2 * SSD_STATE)] * 2,
        out_shape=out_shape,
        scratch_shapes=[pltpu.VMEM((cd // 128, tile, 128), F32)],
        compiler_params=_params(),
        name="ssd_sample_pre",
    )(*args)


SSD_S_BATCH = 8
SSD_PRE_ROWS = 256


def _ssd_sample_state_kernel(st_ref, c_ref, b_ref, xw_ref, dec_ref, eacs_ref, ysk_ref, z_ref, gn_ref,
                             yd_ref, nst_ref):
    row_n = lax.broadcasted_iota(jnp.int32, (8, SSD_STATE), 0)
    row_w = lax.broadcasted_iota(jnp.int32, (8, SSD_GW), 0)
    row_f = lax.broadcasted_iota(jnp.int32, (8, HALF), 0)
    ones_rows = jnp.where((row_n >= 4) & (row_n < 7), 1.0, 0.0).astype(BF16)
    hpg = SSD_HEADS // SSD_GROUPS

    def pair(p, carry):
        r0 = pl.multiple_of(p * 8, 8)
        rows = pl.ds(r0, 8)
        c8 = c_ref[rows, :].astype(BF16)
        b8 = b_ref[rows, :]
        xw8 = xw_ref[rows, :]
        dec8 = dec_ref[rows, :]
        yoff = []
        for sub in range(2):
            b = 2 * p + sub
            xw_own = xw8 if sub == 0 else pltpu.roll(xw8, 4, 0)
            b_own = b8 if sub == 0 else pltpu.roll(b8, 4, 0)
            hi, mid, lo = (term.astype(F32) for term in _split3(dec8[4 * sub:4 * sub + 1, :]))
            parts = []
            for g in range(SSD_GROUPS):
                lanes = slice(g * SSD_GW, (g + 1) * SSD_GW)
                heads = pl.ds(g * hpg, hpg)
                h0 = st_ref[b, heads].reshape(SSD_GW, SSD_STATE)
                parts.append(_dot_nt(c8[:, g * SSD_STATE:(g + 1) * SSD_STATE], h0.astype(BF16)))
                lhs = jnp.where(row_w < 4, xw_own[:, lanes],
                                jnp.where(row_w == 4, hi[:, lanes],
                                          jnp.where(row_w == 5, mid[:, lanes],
                                                    jnp.where(row_w == 6, lo[:, lanes], 0.0)))).astype(BF16)
                rhs_b = jnp.where(row_n < 4, b_own[:, g * SSD_STATE:(g + 1) * SSD_STATE], 0.0).astype(BF16)
                decay = _dot_tn(lhs, ones_rows)
                nst_ref[b, heads] = (h0 * decay + _dot_tn(lhs, rhs_b)).reshape(hpg, HEAD_DIM, SSD_STATE)
            yoff.append(jnp.concatenate(parts, axis=1))
        yoff8 = jnp.where(row_f < 4, yoff[0], yoff[1])
        y = ysk_ref[rows, :] + yoff8 * eacs_ref[rows, :]
        yd_ref[rows, :] = _group_norm_gate(y, z_ref[rows, :], gn_ref[...])
        return carry

    lax.fori_loop(0, SSD_S_BATCH // 2, pair, 0)


def _ssd_sample_state(state, cm, bm, xw, dec, eacs, ysk, z, gn):
    n_seq = state.shape[0]
    bb = SSD_S_BATCH
    r = bb * DEC_SEQ
    st_spec = pl.BlockSpec((bb, SSD_HEADS, HEAD_DIM, SSD_STATE), lambda i: (i, 0, 0, 0))
    return pl.pallas_call(
        _ssd_sample_state_kernel,
        grid=(n_seq // bb,),
        in_specs=[st_spec, _rows(r, 2 * SSD_STATE), _rows(r, 2 * SSD_STATE)] + [_rows(r, HALF)] * 5
                 + [_resident((1, HALF))],
        out_specs=[_rows(r, HALF), st_spec],
        out_shape=[jax.ShapeDtypeStruct((n_seq * DEC_SEQ, HALF), BF16), jax.ShapeDtypeStruct(state.shape, F32)],
        compiler_params=_params(),
        name="ssd_sample_state",
    )(state, cm, bm, xw, dec, eacs, ysk, z, gn)


def _prep_layer1(g_pre, w_in, ln_g, ln_b, w_s, b_s, conv_w, conv_b, dt_bias, a_log, d_skip, gate_norm_g,
                 w_out, g_post):
    cd = SSD_CONV_DIM
    gw = HALF // CMLP_GROUPS
    w1 = w_in.T

    def lanes16(v):
        return jnp.pad(v.astype(F32), (0, HEAD_LANES - SSD_HEADS)).reshape(1, HEAD_LANES)

    def per_channel(v):
        return jnp.repeat(v.astype(F32), HEAD_DIM).reshape(1, HALF)

    head_of = jnp.arange(HALF) // HEAD_DIM
    expand = (jnp.arange(HEAD_LANES)[:, None] == head_of[None, :]).astype(BF16)
    tril = jnp.tril(jnp.ones((CHUNK, CHUNK), BF16))
    grp_rows = jnp.arange(2 * SSD_STATE) // SSD_STATE
    seg_expand = (grp_rows[:, None] == (head_of // (SSD_HEADS // SSD_GROUPS))[None, :]).astype(BF16)

    w4 = jnp.tril(w_s[:, :DEC_SEQ, :DEC_SEQ])
    steps = jnp.arange(DEC_SEQ)
    coef = []
    for k in range(DEC_SEQ):
        src = steps - k
        ck = jnp.where((src >= 0)[None, :], w4[:, steps, jnp.maximum(src, 0)], 0.0)
        ck = jnp.repeat(ck.T, gw, axis=1)
        coef.append(jnp.concatenate([ck, ck], axis=0))
    bias4 = jnp.repeat(b_s[:, :DEC_SEQ].T, gw, axis=1)
    return dict(
        g_pre=g_pre.reshape(1, D_MODEL), w1=w1, ln_g=ln_g.reshape(1, HALF), ln_b=ln_b.reshape(1, HALF),
        ws_tril=jnp.tril(w_s).astype(BF16),
        bs_rows=jnp.broadcast_to(b_s.astype(F32)[:, :, None], (CMLP_GROUPS, CHUNK, gw)),
        coef=jnp.stack(coef).astype(F32), bias4=jnp.concatenate([bias4, bias4], axis=0).astype(F32),
        conv_w=conv_w, conv_b=conv_b.reshape(1, cd), dt_bias16=lanes16(dt_bias), a_log16=lanes16(a_log),
        a_log_e=per_channel(a_log), d_skip_e=per_channel(d_skip), gate_norm_g=gate_norm_g.reshape(1, HALF),
        expand3=jnp.concatenate([expand] * 3, axis=0), tril3=jnp.concatenate([tril] * 3, axis=1),
        seg_expand=seg_expand, w_out=w_out, g_post=g_post.reshape(1, D_MODEL))


def _layer1_prompt(x, w):
    yc = _cmlp_prompt(x, w['g_pre'], w['w1'], w['ln_g'], w['ln_b'], w['ws_tril'], w['bs_rows'])
    yd, tail, ssm = _ssd_prompt(x, w)
    y = _out_proj(yc, yd, x, w['w_out'], w['g_post'])
    return y, tail[5:8], ssm.reshape(SSD_HEADS, HEAD_DIM, SSD_STATE)


def _layer1_sample(x, conv_state, ssm_state, w):
    n_seq = x.shape[0]
    t = n_seq * DEC_SEQ
    rows = x.reshape(t, D_MODEL)
    yc, vn = _cmlp_sample(rows, w['g_pre'], w['w1'], w['ln_g'], w['ln_b'], w['coef'], w['bias4'])
    new_conv, z, ysk, eacs, xw, dec, bm, cm = _ssd_sample_pre(rows, conv_state.transpose(1, 0, 2), w)
    yd, new_state = _ssd_sample_state(ssm_state, cm, bm, xw, dec, eacs, ysk, z, w['gate_norm_g'])
    y = _out_proj(yc, yd, rows, w['w_out'], w['g_post'])
    return (y.reshape(n_seq, DEC_SEQ, D_MODEL), vn.reshape(n_seq, DEC_SEQ, HALF),
            new_conv.transpose(1, 0, 2), new_state)


def kernel(x_prompt, x_sample, state_conv_a, cache_win_k, cache_win_v, state_conv_d, state_ssm, rel_bias,
           l0_g_pre, l0_w_in, l0_conv_w, l0_sinks, l0_w_out, l0_g_post,
           l1_g_pre, l1_w_in, l1_ln_g, l1_ln_b, l1_w_s, l1_b_s, l1_conv_w, l1_conv_b, l1_dt_bias, l1_a_log,
           l1_d_skip, l1_gate_norm_g, l1_w_out, l1_g_post):
    w0 = _prep_layer0(l0_g_pre, l0_w_in, l0_conv_w, rel_bias, l0_sinks, l0_w_out, l0_g_post)
    w1 = _prep_layer1(l1_g_pre, l1_w_in, l1_ln_g, l1_ln_b, l1_w_s, l1_b_s, l1_conv_w, l1_conv_b, l1_dt_bias,
                      l1_a_log, l1_d_skip, l1_gate_norm_g, l1_w_out, l1_g_post)
    yp, p_conv_a, p_win_k, p_win_v = _layer0_prompt(x_prompt[0], w0)
    ys, s_conv_a, s_win_k, s_win_v = _layer0_sample(x_sample, state_conv_a, cache_win_k, cache_win_v, w0)
    yp, p_conv_d, p_ssm = _layer1_prompt(yp, w1)
    ys, s_chunk_v, s_conv_d, s_ssm = _layer1_sample(ys, state_conv_d, state_ssm, w1)
    return (yp[None], ys, p_conv_a[None], s_conv_a, p_win_k[None], p_win_v[None], s_win_k, s_win_v, s_chunk_v,
            p_conv_d[None], s_conv_d, p_ssm[None], s_ssm)
```

```python
import functools
import math

import jax
import jax.numpy as jnp
from jax import lax
from jax.experimental import pallas as pl
from jax.experimental.pallas import tpu as pltpu

F32 = jnp.float32
BF16 = jnp.bfloat16

D_MODEL = 2048
HALF = 1024
HEAD_DIM = 64
N_HEADS = 16
N_KV = 4
GROUP = 4
WINDOW = 128
NUM_BUCKETS = 32
MAX_DISTANCE = 128
CMLP_GROUPS = 8
CHUNK = 128
SSD_HEADS = 16
SSD_STATE = 128
SSD_GROUPS = 2
SSD_CONV_DIM = HALF + 2 * SSD_GROUPS * SSD_STATE
DEC_SEQ = 4
NORM_EPS = 1e-6
MASK_VALUE = -1e30

ROW_TILE = 512
VMEM_LIMIT = 56 * 1024 * 1024


def _params(n_axes=1):
    return pltpu.CompilerParams(dimension_semantics=("arbitrary",) * n_axes,
                                vmem_limit_bytes=VMEM_LIMIT)


def _resident(shape):
    nd = len(shape)
    return pl.BlockSpec(shape, lambda *_: (0,) * nd, pipeline_mode=pl.Buffered(1))


def _rows(tile, width):
    return pl.BlockSpec((tile, width), lambda i: (i, 0))


def _cols(rows, width, block):
    return pl.BlockSpec((rows, width), lambda *_: (0, block), pipeline_mode=pl.Buffered(1))


def _rowwin(height, cols, block):
    return pl.BlockSpec((height, cols), lambda *_: (block, 0), pipeline_mode=pl.Buffered(1))


def _rms_bf16(x, g):
    ms = jnp.mean(x * x, axis=-1, keepdims=True)
    return (x * lax.rsqrt(ms + NORM_EPS) * g).astype(BF16)


def _silu(x):
    return x * jax.nn.sigmoid(x)


def _dot(a, b):
    return jnp.dot(a, b, preferred_element_type=F32)


def _dot_nt(a, b):
    return lax.dot_general(a, b, (((1,), (1,)), ((), ())), preferred_element_type=F32)


def _dot_tn(a, b):
    return lax.dot_general(a, b, (((0,), (0,)), ((), ())), preferred_element_type=F32)


def _dot_w(a, w):
    return _dot(a, w.astype(BF16))


def _dot_wt(a, wt):
    return _dot_nt(a, wt.astype(BF16))


def _split3(x):
    hi = x.astype(BF16)
    r1 = x - hi.astype(F32)
    mid = r1.astype(BF16)
    lo = (r1 - mid.astype(F32)).astype(BF16)
    return hi, mid, lo


def _place_steps(t, placements):
    n_seq = placements[0][1].shape[0]
    row = lax.broadcasted_iota(jnp.int32, (t, n_seq), 0)
    seq = lax.broadcasted_iota(jnp.int32, (t, n_seq), 1)
    lhs, rhs = [], []
    for step, state in placements:
        sel = jnp.where(row == DEC_SEQ * seq + step, 1.0, 0.0).astype(BF16)
        lhs += [sel] * 3
        rhs += list(_split3(state))
    return _dot(jnp.concatenate(lhs, axis=1), jnp.concatenate(rhs, axis=0))


def _out_proj_kernel(ya_ref, yb_ref, x_ref, w_ref, g_ref, o_ref):
    y = _dot_w(ya_ref[...], w_ref[0:HALF, :]) + _dot_w(yb_ref[...], w_ref[HALF:2 * HALF, :])
    ms = jnp.mean(y * y, axis=-1, keepdims=True)
    o_ref[...] = x_ref[...] + y * lax.rsqrt(ms + NORM_EPS) * g_ref[...]


def _out_proj(ya, yb, x, w_bf, g):
    t = x.shape[0]
    tile = min(ROW_TILE, t)
    return pl.pallas_call(
        _out_proj_kernel,
        grid=(t // tile,),
        in_specs=[_rows(tile, HALF), _rows(tile, HALF), _rows(tile, D_MODEL),
                  _resident((2 * HALF, D_MODEL)), _resident((1, D_MODEL))],
        out_specs=_rows(tile, D_MODEL),
        out_shape=jax.ShapeDtypeStruct((t, D_MODEL), F32),
        compiler_params=_params(),
        name="out_proj",
    )(ya, yb, x, w_bf, g)


CONV_A_CHUNK = 256


def _conv_a_kernel(*refs, sample):
    if sample:
        x_ref, g_ref, w_ref, cw_ref, st_ref, ya_ref, s_ref, h_ref = refs
    else:
        x_ref, g_ref, w_ref, cw_ref, ya_ref, s_ref, h_ref, shift_scr = refs
    tile = x_ref.shape[0]
    cc = CONV_A_CHUNK
    h = _rms_bf16(x_ref[...], g_ref[...])
    h_ref[...] = h
    if not sample:
        @pl.when(pl.program_id(0) == 0)
        def _():
            s_ref[...] = jnp.zeros_like(s_ref)
    for c in range(HALF // cc):
        lanes = slice(c * cc, (c + 1) * cc)
        a_b, a_c, a_h, a_g = (_dot_w(h, w_ref[:, j * HALF + c * cc:j * HALF + (c + 1) * cc]) for j in range(4))
        s = a_c * a_h
        if sample:
            t_in = lax.broadcasted_iota(jnp.int32, s.shape, 0) % DEC_SEQ
            old0 = st_ref[:, c * cc:(c + 1) * cc]
            old1 = st_ref[:, HALF + c * cc:HALF + (c + 1) * cc]
            p1 = jnp.where(t_in >= 1, pltpu.roll(s, 1, 0), 0.0) + _place_steps(tile, [(0, old1)])
            p2 = jnp.where(t_in >= 2, pltpu.roll(s, 2, 0), 0.0) + _place_steps(tile, [(0, old0), (1, old1)])
            s_ref[:, lanes] = s
        else:
            shift_scr[0:8, :] = s_ref[:, lanes]
            shift_scr[8:8 + tile, :] = s
            p1 = shift_scr[7:7 + tile, :]
            p2 = shift_scr[6:6 + tile, :]
            s_ref[:, lanes] = s[tile - 8:tile, :]
        conv = p2 * cw_ref[0:1, lanes] + p1 * cw_ref[1:2, lanes] + s * cw_ref[2:3, lanes]
        ya_ref[:, lanes] = (a_b * conv * _silu(a_g)).astype(BF16)


def _conv_a(x, g_pre, w0, conv_w, state=None):
    t = x.shape[0]
    sample = state is not None
    tile = t if sample else min(ROW_TILE, t)
    in_specs = [_rows(tile, D_MODEL), _resident((1, D_MODEL)), _cols(D_MODEL, 4 * HALF, 0), _resident((3, HALF))]
    args = [x, g_pre, w0, conv_w]
    scratch = []
    if sample:
        in_specs.append(_resident(state.shape))
        args.append(state)
        s_spec, s_shape = _rows(tile, HALF), (t, HALF)
    else:
        s_spec, s_shape = pl.BlockSpec((8, HALF), lambda i: (0, 0)), (8, HALF)
        scratch = [pltpu.VMEM((8 + tile, CONV_A_CHUNK), F32)]
    return pl.pallas_call(
        functools.partial(_conv_a_kernel, sample=sample),
        grid=(t // tile,),
        in_specs=in_specs,
        out_specs=[_rows(tile, HALF), s_spec, _rows(tile, D_MODEL)],
        out_shape=[jax.ShapeDtypeStruct((t, HALF), BF16), jax.ShapeDtypeStruct(s_shape, F32),
                   jax.ShapeDtypeStruct((t, D_MODEL), BF16)],
        scratch_shapes=scratch,
        compiler_params=_params(),
        name="conv_a_sample" if sample else "conv_a_prompt",
    )(*args)


def _rel_bucket(dist):
    max_exact = NUM_BUCKETS // 2
    d = jnp.maximum(dist, 0)
    ratio = jnp.maximum(d, max_exact).astype(F32) / max_exact
    large = max_exact + (jnp.log(ratio) / math.log(MAX_DISTANCE / max_exact)
                         * (NUM_BUCKETS - max_exact)).astype(jnp.int32)
    return jnp.where(d < max_exact, d, jnp.minimum(large, NUM_BUCKETS - 1))


def _attn_softmax_pv(s, sink, v_bf, v_transposed=False):
    m = jnp.maximum(jnp.max(s, axis=-1, keepdims=True), sink)
    p = jnp.exp(s - m)
    den = jnp.sum(p, axis=-1, keepdims=True) + jnp.exp(sink - m)
    pv = _dot_nt(p.astype(BF16), v_bf) if v_transposed else _dot(p.astype(BF16), v_bf)
    return pv / den


def _attn_prompt_kernel(h_ref, wq_ref, wkv_ref, wg0_ref, wg1_ref, tab_ref, sink_ref, yb_ref, kwin_ref,
                        vwin_ref, q_scr, gate_scr, k_scr, v_scr, bias_scr):
    tile = h_ref.shape[0]
    i = pl.program_id(0)
    kv_w = N_KV * HEAD_DIM
    h = h_ref[...]

    kw, vw = 2 * HEAD_DIM, 4 * HEAD_DIM

    @pl.when(i == 0)
    def _():
        k_scr[0:WINDOW, :] = jnp.zeros((WINDOW, N_KV * kw), BF16)
        v_scr[0:WINDOW, :] = jnp.zeros((WINDOW, N_KV * vw), BF16)
        for hk in range(N_KV):
            v_scr[:, hk * vw + kw:(hk + 1) * vw] = jnp.ones((tile + WINDOW, kw), BF16)
        in_own = lax.broadcasted_iota(jnp.int32, (WINDOW, 2 * WINDOW), 1) >= WINDOW
        for head in range(N_HEADS):
            row = jnp.broadcast_to(tab_ref[head:head + 1, :], (WINDOW, BIAS_SPAN))
            band = pltpu.roll(row, 0, 1, stride=1, stride_axis=0)[:, 0:2 * WINDOW]
            rows = slice((head % 2) * WINDOW, (head % 2 + 1) * WINDOW)
            bias_scr[1, head // 2, rows, :] = band
            bias_scr[0, head // 2, rows, :] = jnp.where(in_own, band, MASK_VALUE)

    q_scr[...] = (_dot_w(h, wq_ref[...]) * (HEAD_DIM ** -0.5)).astype(BF16)
    k = _dot_w(h, wkv_ref[:, 0:kv_w])
    v = _dot_w(h, wkv_ref[:, kv_w:2 * kv_w])
    gate_scr[:, 0:HALF // 2] = _silu(_dot_w(h, wg0_ref[...]))
    gate_scr[:, HALF // 2:HALF] = _silu(_dot_w(h, wg1_ref[...]))
    for hk in range(N_KV):
        k_h = k[:, hk * HEAD_DIM:(hk + 1) * HEAD_DIM].astype(BF16)
        v_h = v[:, hk * HEAD_DIM:(hk + 1) * HEAD_DIM].astype(BF16)
        k_scr[WINDOW:WINDOW + tile, hk * kw:(hk + 1) * kw] = jnp.concatenate([k_h, k_h], axis=1)
        v_scr[WINDOW:WINDOW + tile, hk * vw:hk * vw + kw] = jnp.concatenate([v_h, v_h], axis=1)
    kwin_ref[...] = k[tile - WINDOW:tile, :]
    vwin_ref[...] = v[tile - WINDOW:tile, :]

    lane = lax.broadcasted_iota(jnp.int32, (WINDOW, kw), 1)
    lo = lane < HEAD_DIM
    keep_a = jnp.where(lo, 1.0, 0.0).astype(BF16)
    keep_b = jnp.where(lo, 0.0, 1.0).astype(BF16)
    is_a = lax.broadcasted_iota(jnp.int32, (2 * WINDOW, 1), 0) < WINDOW

    def block(n, carry):
        r0 = pl.multiple_of(n * WINDOW, WINDOW)
        rows = pl.ds(r0, WINDOW)
        keys = pl.ds(r0, 2 * WINDOW)
        first = jnp.where(jnp.logical_and(i == 0, n == 0), 0, 1)
        for hk in range(N_KV):
            for gp in range(GROUP // 2):
                a = hk * GROUP + 2 * gp
                slab = slice(a * HEAD_DIM, (a + 2) * HEAD_DIM)
                q2 = q_scr[rows, slab]
                lhs = jnp.concatenate([q2 * keep_a, q2 * keep_b], axis=0)
                s = _dot_nt(lhs, k_scr[keys, hk * kw:(hk + 1) * kw]) + bias_scr[first, a // 2]
                sink = jnp.where(is_a, sink_ref[a], sink_ref[a + 1])
                m = jnp.maximum(jnp.max(s, axis=-1, keepdims=True), sink)
                p = jnp.exp(s - m).astype(BF16)
                pv = _dot(p, v_scr[keys, hk * vw:(hk + 1) * vw])
                num = jnp.where(lo, pv[0:WINDOW, 0:kw], pv[WINDOW:2 * WINDOW, 0:kw])
                den = jnp.where(lo, pv[0:WINDOW, kw:2 * kw], pv[WINDOW:2 * WINDOW, kw:2 * kw])
                m_slab = jnp.where(lo, m[0:WINDOW], m[WINDOW:2 * WINDOW])
                den = den + jnp.exp(jnp.where(lo, sink_ref[a], sink_ref[a + 1]) - m_slab)
                yb_ref[rows, slab] = (num / den * gate_scr[rows, slab]).astype(BF16)
        return carry

    lax.fori_loop(0, tile // WINDOW, block, 0)
    k_scr[0:WINDOW, :] = k_scr[tile:tile + WINDOW, :]
    v_scr[0:WINDOW, :] = v_scr[tile:tile + WINDOW, :]


BIAS_SPAN = 3 * WINDOW


def _prompt_bias_table(rel_bias):
    dist = WINDOW - jnp.arange(BIAS_SPAN)
    table = jnp.where(((dist >= 0) & (dist < WINDOW))[:, None], rel_bias.astype(F32)[_rel_bucket(dist)], MASK_VALUE)
    return table.T


def _attn_prompt(h, w0, table, sinks):
    t = h.shape[0]
    tile = min(ROW_TILE, t)
    kv_w = N_KV * HEAD_DIM
    win_spec = pl.BlockSpec((WINDOW, kv_w), lambda i: (0, 0))
    return pl.pallas_call(
        _attn_prompt_kernel,
        grid=(t // tile,),
        in_specs=[_rows(tile, D_MODEL),
                  _cols(D_MODEL, HALF, 4), _cols(D_MODEL, 2 * kv_w, 10),
                  _cols(D_MODEL, HALF // 2, 11), _cols(D_MODEL, HALF // 2, 12),
                  _resident(table.shape), pl.BlockSpec(memory_space=pltpu.SMEM)],
        out_specs=[_rows(tile, HALF), win_spec, win_spec],
        out_shape=[jax.ShapeDtypeStruct((t, HALF), BF16),
                   jax.ShapeDtypeStruct((WINDOW, kv_w), F32), jax.ShapeDtypeStruct((WINDOW, kv_w), F32)],
        scratch_shapes=[pltpu.VMEM((tile, HALF), BF16), pltpu.VMEM((tile, HALF), F32),
                        pltpu.VMEM((tile + WINDOW, 2 * kv_w), BF16), pltpu.VMEM((tile + WINDOW, 4 * kv_w), BF16),
                        pltpu.VMEM((2, N_HEADS // 2, 2 * WINDOW, 2 * WINDOW), F32)],
        compiler_params=_params(),
        name="attn_prompt",
    )(h, w0, w0, w0, w0, table, sinks)


def _attn_proj_kernel(h_ref, wq_ref, wkv_ref, wg0_ref, wg1_ref, qg_ref, kt_ref, vt_ref, kv_scr):
    kv_w = N_KV * HEAD_DIM
    h = h_ref[...]
    q = _dot_w(h, wq_ref[...]) * (HEAD_DIM ** -0.5)
    for hk in range(N_KV):
        for g in range(GROUP):
            src = (hk * GROUP + g) * HEAD_DIM
            dst = (g * N_KV + hk) * HEAD_DIM
            qg_ref[:, dst:dst + HEAD_DIM] = q[:, src:src + HEAD_DIM]
    qg_ref[:, HALF:HALF + HALF // 2] = _dot_w(h, wg0_ref[...])
    qg_ref[:, HALF + HALF // 2:2 * HALF] = _dot_w(h, wg1_ref[...])
    kv_scr[...] = _dot_w(h, wkv_ref[...])
    for j in range(kt_ref.shape[0]):
        kt_ref[j] = kv_scr[j * WINDOW:(j + 1) * WINDOW, 0:kv_w].T
        vt_ref[j] = kv_scr[j * WINDOW:(j + 1) * WINDOW, kv_w:2 * kv_w].T


def _attn_proj(h, w0):
    t = h.shape[0]
    kv_w = N_KV * HEAD_DIM
    out_shape = [jax.ShapeDtypeStruct((t, 2 * HALF), F32), jax.ShapeDtypeStruct((t // WINDOW, kv_w, WINDOW), F32),
                 jax.ShapeDtypeStruct((t // WINDOW, kv_w, WINDOW), F32)]
    return pl.pallas_call(
        _attn_proj_kernel,
        grid=(1,),
        in_specs=[_resident((t, D_MODEL)),
                  _cols(D_MODEL, HALF, 4), _cols(D_MODEL, 2 * kv_w, 10),
                  _cols(D_MODEL, HALF // 2, 11), _cols(D_MODEL, HALF // 2, 12)],
        out_specs=[_resident(s.shape) for s in out_shape],
        out_shape=out_shape,
        scratch_shapes=[pltpu.VMEM((t, 2 * kv_w), F32)],
        compiler_params=_params(),
        name="attn_proj_sample",
    )(h, w0, w0, w0, w0)


ATTN_S_BATCH = 16
KEYS_PAD = 2 * WINDOW


def _attn_sample_kernel(qg_ref, ktn_ref, vtn_ref, ck_ref, cv_ref, bias_ref, sink_ref, yb_ref, nk_ref, nv_ref):
    kv_w = N_KV * HEAD_DIM
    row8 = lax.broadcasted_iota(jnp.int32, (8, kv_w), 0)
    lane_head = lax.broadcasted_iota(jnp.int32, (8, kv_w), 1) // HEAD_DIM
    lower = row8 < DEC_SEQ
    pick = [jnp.where(lane_head == 2 * hp + jnp.where(lower, 0, 1), 1.0, 0.0).astype(F32) for hp in range(2)]
    lower_w = lax.broadcasted_iota(jnp.int32, (8, HALF), 0) < DEC_SEQ
    kept = lax.broadcasted_iota(jnp.int32, (kv_w, WINDOW), 1) < WINDOW - DEC_SEQ
    seq0 = pl.program_id(0) * ATTN_S_BATCH
    per_tile = WINDOW // DEC_SEQ

    def slide(old, new_tile, shift):
        return jnp.where(kept, pltpu.roll(old, WINDOW - DEC_SEQ, 1), pltpu.roll(new_tile, shift, 1))

    def pair(p, carry):
        r0 = pl.multiple_of(p * 8, 8)
        rows = qg_ref[pl.ds(r0, 8), :]
        q8 = rows[:, 0:HALF]
        gate8 = rows[:, HALF:2 * HALF]
        out8 = []
        for sub in range(2):
            b = 2 * p + sub
            q_swap = pltpu.roll(q8, 4, 0)
            q_dup = jnp.where(lower_w, q8, q_swap) if sub == 0 else jnp.where(lower_w, q_swap, q8)
            tile = (seq0 + b) // per_tile
            shift = (2 * WINDOW - DEC_SEQ - DEC_SEQ * ((seq0 + b) % per_tile)) % WINDOW
            k_old = ck_ref[b].reshape(kv_w, WINDOW)
            v_old = cv_ref[b].reshape(kv_w, WINDOW)
            k_win = slide(k_old, ktn_ref[tile], shift)
            v_win = slide(v_old, vtn_ref[tile], shift)
            nk_ref[b] = k_win.reshape(N_KV, HEAD_DIM, WINDOW)
            nv_ref[b] = v_win.reshape(N_KV, HEAD_DIM, WINDOW)
            k_all = jnp.concatenate([k_old, k_win], axis=1).astype(BF16)
            v_all = jnp.concatenate([v_old, v_win], axis=1).astype(BF16)
            q_bd = jnp.concatenate(
                [q_dup[:, g * kv_w:(g + 1) * kv_w] * pick[hp] for g in range(GROUP) for hp in range(2)], axis=0)
            s = _dot(q_bd.astype(BF16), k_all) + bias_ref[...]
            o = _attn_softmax_pv(s, sink_ref[:, 0:1], v_all, v_transposed=True)
            out_g = []
            for g in range(GROUP):
                acc = None
                for hp in range(2):
                    piece = o[(2 * g + hp) * 8:(2 * g + hp + 1) * 8, :] * pick[hp]
                    piece = piece + pltpu.roll(piece, 4, 0)
                    acc = piece if acc is None else acc + piece
                out_g.append(acc)
            out8.append(jnp.concatenate(
                [out_g[g][:, hk * HEAD_DIM:(hk + 1) * HEAD_DIM] for hk in range(N_KV) for g in range(GROUP)], axis=1))
        o8 = jnp.where(lower_w, out8[0], out8[1])
        yb_ref[pl.ds(r0, 8), :] = (o8 * _silu(gate8)).astype(BF16)
        return carry

    lax.fori_loop(0, ATTN_S_BATCH // 2, pair, 0, unroll=2)


def _sample_bias(rel_bias, sinks):
    t = jnp.arange(DEC_SEQ)[:, None]
    j = jnp.arange(KEYS_PAD)[None, :]
    pos = jnp.where(j < WINDOW, j, j - (KEYS_PAD - DEC_SEQ) + WINDOW)
    dist = t + WINDOW - pos
    valid = (dist >= 0) & (dist < WINDOW) & ((j < WINDOW) | (j >= KEYS_PAD - DEC_SEQ))
    bias = jnp.where(valid[:, :, None], rel_bias.astype(F32)[_rel_bucket(dist)], MASK_VALUE)
    bias = bias.reshape(DEC_SEQ, KEYS_PAD, N_KV, GROUP).transpose(3, 2, 0, 1).reshape(N_HEADS * DEC_SEQ, KEYS_PAD)
    sink = jnp.broadcast_to(sinks.astype(F32).reshape(N_KV, GROUP).T[:, :, None], (GROUP, N_KV, DEC_SEQ))
    return bias, jnp.broadcast_to(sink.reshape(N_HEADS * DEC_SEQ, 1), (N_HEADS * DEC_SEQ, 128))


def _attn_sample(qg, kt_new, vt_new, cache_kt, cache_vt, bias, sink):
    n_seq = cache_kt.shape[0]
    bb = ATTN_S_BATCH
    cache_spec = pl.BlockSpec((bb, N_KV, HEAD_DIM, WINDOW), lambda i: (i, 0, 0, 0))
    return pl.pallas_call(
        _attn_sample_kernel,
        grid=(n_seq // bb,),
        in_specs=[_rows(bb * DEC_SEQ, 2 * HALF), _resident(kt_new.shape), _resident(vt_new.shape),
                  cache_spec, cache_spec, _resident(bias.shape), _resident(sink.shape)],
        out_specs=[_rows(bb * DEC_SEQ, HALF), cache_spec, cache_spec],
        out_shape=[jax.ShapeDtypeStruct((n_seq * DEC_SEQ, HALF), BF16),
                   jax.ShapeDtypeStruct(cache_kt.shape, F32), jax.ShapeDtypeStruct(cache_vt.shape, F32)],
        compiler_params=_params(),
        name="attn_sample",
    )(qg, kt_new, vt_new, cache_kt, cache_vt, bias, sink)


def _prep_layer0(g_pre, w_in, conv_w, rel_bias, sinks, w_out, g_post):
    return dict(
        g_pre=g_pre.reshape(1, D_MODEL), w0=w_in, conv_w=conv_w, rel_bias=rel_bias, sinks=sinks,
        w_out=w_out, g_post=g_post.reshape(1, D_MODEL))


def _layer0_prompt(x, w):
    ya, s_tail, h = _conv_a(x, w['g_pre'], w['w0'], w['conv_w'])
    yb, kwin, vwin = _attn_prompt(h, w['w0'], _prompt_bias_table(w['rel_bias']), w['sinks'])
    y = _out_proj(ya, yb, x, w['w_out'], w['g_post'])
    return (y, s_tail[6:8], kwin.reshape(WINDOW, N_KV, HEAD_DIM), vwin.reshape(WINDOW, N_KV, HEAD_DIM))


def _layer0_sample(x, conv_state, cache_k, cache_v, w):
    n_seq = x.shape[0]
    rows = x.reshape(n_seq * DEC_SEQ, D_MODEL)
    ya, s, h = _conv_a(rows, w['g_pre'], w['w0'], w['conv_w'], conv_state.reshape(n_seq, 2 * HALF))
    qg, kt_new, vt_new = _attn_proj(h, w['w0'])
    bias, sink = _sample_bias(w['rel_bias'], w['sinks'])
    yb, new_kt, new_vt = _attn_sample(qg, kt_new, vt_new, cache_k.transpose(0, 2, 3, 1), cache_v.transpose(0, 2, 3, 1),
                                      bias, sink)
    y = _out_proj(ya, yb, rows, w['w_out'], w['g_post'])
    return (y.reshape(n_seq, DEC_SEQ, D_MODEL), s.reshape(n_seq, DEC_SEQ, HALF)[:, DEC_SEQ - 2:],
            new_kt.transpose(0, 3, 1, 2), new_vt.transpose(0, 3, 1, 2))


def _layer_norm(v, g, b):
    xc = v - jnp.mean(v, axis=-1, keepdims=True)
    return xc * lax.rsqrt(jnp.mean(xc * xc, axis=-1, keepdims=True) + NORM_EPS) * g + b


def _cmlp_prompt_kernel(x_ref, g_ref, w_ref, lng_ref, lnb_ref, ws_ref, bs_ref, yc_ref, h_ref, vn_scr):
    tile = x_ref.shape[0]
    h = _rms_bf16(x_ref[...], g_ref[...])
    h_ref[...] = h
    v = _dot_wt(h, w_ref[HALF:2 * HALF, :])
    vn_scr[...] = _layer_norm(v, lng_ref[...], lnb_ref[...]).astype(BF16)
    gw = HALF // CMLP_GROUPS
    cols = 2 * gw
    for cb in range(HALF // cols):
        u = _dot_wt(h, w_ref[cb * cols:(cb + 1) * cols, :])
        gate = _silu(_dot_wt(h, w_ref[2 * HALF + cb * cols:2 * HALF + (cb + 1) * cols, :]))
        for gi in range(2):
            grp = 2 * cb + gi
            lanes = slice(grp * gw, (grp + 1) * gw)
            for n in range(tile // CHUNK):
                rows = slice(n * CHUNK, (n + 1) * CHUNK)
                mixed = _dot(ws_ref[grp], vn_scr[rows, lanes]) + bs_ref[grp]
                yc_ref[rows, lanes] = (u[rows, gi * gw:(gi + 1) * gw] * mixed
                                       * gate[rows, gi * gw:(gi + 1) * gw]).astype(BF16)


def _cmlp_prompt(x, g_pre, w_c, ln_g, ln_b, ws_tril, bs_rows):
    t = x.shape[0]
    tile = min(ROW_TILE, t)
    return pl.pallas_call(
        _cmlp_prompt_kernel,
        grid=(t // tile,),
        in_specs=[_rows(tile, D_MODEL), _resident((1, D_MODEL)), _rowwin(3 * HALF, D_MODEL, 0),
                  _resident((1, HALF)), _resident((1, HALF)), _resident(ws_tril.shape), _resident(bs_rows.shape)],
        out_specs=[_rows(tile, HALF), _rows(tile, D_MODEL)],
        out_shape=[jax.ShapeDtypeStruct((t, HALF), BF16), jax.ShapeDtypeStruct((t, D_MODEL), BF16)],
        scratch_shapes=[pltpu.VMEM((tile, HALF), BF16)],
        compiler_params=_params(),
        name="cmlp_prompt",
    )(x, g_pre, w_c, ln_g, ln_b, ws_tril, bs_rows)


def _cmlp_sample_kernel(x_ref, g_ref, w_ref, lng_ref, lnb_ref, coef_ref, bias_ref, yc_ref, vn_ref, h_ref):
    t = x_ref.shape[0]
    h = _rms_bf16(x_ref[...], g_ref[...])
    h_ref[...] = h
    u = _dot_wt(h, w_ref[0:HALF, :])
    vn = _layer_norm(_dot_wt(h, w_ref[HALF:2 * HALF, :]), lng_ref[...], lnb_ref[...])
    gate = _silu(_dot_wt(h, w_ref[2 * HALF:3 * HALF, :]))
    vn_ref[...] = vn

    def tiled(a):
        return a.reshape(t // 8, 8, HALF)

    mixed = tiled(vn) * coef_ref[0][None] + bias_ref[...][None]
    for k in range(1, DEC_SEQ):
        mixed = mixed + tiled(pltpu.roll(vn, k, 0)) * coef_ref[k][None]
    yc_ref[...] = (u * mixed.reshape(t, HALF) * gate).astype(BF16)


def _cmlp_sample(x, g_pre, w_c, ln_g, ln_b, coef, bias):
    t = x.shape[0]
    return pl.pallas_call(
        _cmlp_sample_kernel,
        grid=(1,),
        in_specs=[_resident((t, D_MODEL)), _resident((1, D_MODEL)), _rowwin(3 * HALF, D_MODEL, 0),
                  _resident((1, HALF)), _resident((1, HALF)), _resident(coef.shape), _resident(bias.shape)],
        out_specs=[_resident((t, HALF)), _resident((t, HALF)), _resident((t, D_MODEL))],
        out_shape=[jax.ShapeDtypeStruct((t, HALF), BF16), jax.ShapeDtypeStruct((t, HALF), F32),
                   jax.ShapeDtypeStruct((t, D_MODEL), BF16)],
        compiler_params=_params(),
        name="cmlp_sample",
    )(x, g_pre, w_c, ln_g, ln_b, coef, bias)


HEAD_LANES = 128
SSD_GW = HALF // SSD_GROUPS


def _softplus(x):
    return jnp.maximum(x, 0.0) + jnp.log1p(jnp.exp(-jnp.abs(x)))


def _dt_proj(h, wdt_ref):
    pad = jnp.zeros((HEAD_LANES - SSD_HEADS, D_MODEL), F32)
    return _dot_wt(h, jnp.concatenate([wdt_ref[...], pad], axis=0))


def _group_norm_gate(y, z, gn):
    gated = y * _silu(z)
    parts = []
    for g in range(SSD_GROUPS):
        part = gated[:, g * SSD_GW:(g + 1) * SSD_GW]
        parts.append(part * lax.rsqrt(jnp.mean(part * part, axis=-1, keepdims=True) + NORM_EPS))
    return (jnp.concatenate(parts, axis=1) * gn).astype(BF16)


def _ssd_prompt_kernel(h_ref, wz_ref, wx0_ref, wx1_ref, wx2_ref, wdt_ref, cw_ref, cb_ref, dtb_ref, alog_ref,
                       dskip_ref, gn_ref, e3_ref, tril3_ref, yd_ref, tail_ref, ssm_ref,
                       xbc_scr, z_scr, dt_scr, ht_scr, shift_scr):
    tile = h_ref.shape[0]
    i = pl.program_id(0)
    cd = SSD_CONV_DIM
    h = h_ref[...]

    @pl.when(i == 0)
    def _():
        tail_ref[...] = jnp.zeros_like(tail_ref)
        ht_scr[...] = jnp.zeros_like(ht_scr)

    z_scr[...] = _dot_wt(h, wz_ref[...])
    dt_scr[...] = _softplus(_dt_proj(h, wdt_ref) + dtb_ref[...])
    third = cd // 3
    for j, wx_ref in enumerate((wx0_ref, wx1_ref, wx2_ref)):
        cols = slice(j * third, (j + 1) * third)
        raw = _dot_wt(h, wx_ref[...])
        shift_scr[0:8, :] = tail_ref[:, cols]
        shift_scr[8:8 + tile, :] = raw
        conv = raw * cw_ref[3:4, cols] + cb_ref[:, cols]
        for k in range(1, 4):
            conv = conv + shift_scr[8 - k:8 - k + tile, :] * cw_ref[3 - k:4 - k, cols]
        xbc_scr[:, cols] = _silu(conv)
        tail_ref[:, cols] = raw[tile - 8:tile, :]

    a16 = -jnp.exp(alog_ref[...])
    causal =(lax.broadcasted_iota(jnp.int32, (CHUNK, CHUNK), 0)
              >= lax.broadcasted_iota(jnp.int32, (CHUNK, CHUNK), 1))
    first_half = lax.broadcasted_iota(jnp.int32, (CHUNK, 2 * HEAD_DIM), 1) < HEAD_DIM
    keep_a = jnp.where(first_half, 1.0, 0.0).astype(BF16)
    keep_b = jnp.where(first_half, 0.0, 1.0).astype(BF16)

    def chunk(n, carry):
        r0 = pl.multiple_of(n * CHUNK, CHUNK)
        rows = pl.ds(r0, CHUNK)
        xs = xbc_scr[rows, 0:HALF]
        dt16 = dt_scr[rows, :]
        dt_e = _dot(jnp.concatenate(_split3(dt16), axis=1), e3_ref[...])
        acs16 = _dot(tril3_ref[...], jnp.concatenate(_split3(dt16 * a16), axis=0))
        acs_e = _dot(jnp.concatenate(_split3(acs16), axis=1), e3_ref[...])
        acs_t = acs16.T
        last_e = acs_e[CHUNK - 1:CHUNK, :]
        xdt = xs * dt_e
        xdt_bf = xdt.astype(BF16)
        xw = (jnp.exp(last_e - acs_e) * xdt).astype(BF16)
        dec_e = jnp.exp(last_e)
        y_parts = []
        yoff_parts = []
        for g in range(SSD_GROUPS):
            c_g = xbc_scr[rows, HALF + 2 * SSD_STATE + g * SSD_STATE:HALF + 2 * SSD_STATE + (g + 1) * SSD_STATE].astype(BF16)
            b_g = xbc_scr[rows, HALF + g * SSD_STATE:HALF + (g + 1) * SSD_STATE].astype(BF16)
            cb = _dot_nt(c_g, b_g)
            h_prev = ht_scr[g]
            yoff_parts.append(_dot(c_g, h_prev.astype(BF16)))
            for r in range(0, SSD_HEADS // SSD_GROUPS, 2):
                wgt = []
                for hd in (g * (SSD_HEADS // SSD_GROUPS) + r, g * (SSD_HEADS // SSD_GROUPS) + r + 1):
                    seg = acs16[:, hd:hd + 1] - acs_t[hd:hd + 1, :]
                    wgt.append(cb * jnp.exp(jnp.where(causal, seg, -jnp.inf)))
                a = g * (SSD_HEADS // SSD_GROUPS) + r
                slab = xdt_bf[:, a * HEAD_DIM:(a + 2) * HEAD_DIM]
                rhs = jnp.concatenate([slab * keep_a, slab * keep_b], axis=0)
                y_parts.append(_dot(jnp.concatenate(wgt, axis=1).astype(BF16), rhs))
            lanes = slice(g * SSD_GW, (g + 1) * SSD_GW)
            ht_scr[g] = h_prev * dec_e[:, lanes] + _dot_tn(b_g, xw[:, lanes])
        y = (jnp.concatenate(y_parts, axis=1) + jnp.concatenate(yoff_parts, axis=1) * jnp.exp(acs_e)
             + dskip_ref[...] * xs)
        yd_ref[rows, :] = _group_norm_gate(y, z_scr[rows, :], gn_ref[...])
        return carry

    lax.fori_loop(0, tile // CHUNK, chunk, 0)

    @pl.when(i == pl.num_programs(0) - 1)
    def _():
        for g in range(SSD_GROUPS):
            ssm_ref[g * SSD_GW:(g + 1) * SSD_GW, :] = ht_scr[g].T


def _ssd_weight_specs():
    third = SSD_CONV_DIM // 3
    first = 4 * HALF // third
    return ([_rowwin(HALF, D_MODEL, 3)] + [_rowwin(third, D_MODEL, first + j) for j in range(3)]
            + [_rowwin(SSD_HEADS, D_MODEL, (4 * HALF + SSD_CONV_DIM) // SSD_HEADS)])


def _ssd_prompt(h, w):
    t = h.shape[0]
    tile = min(ROW_TILE, t)
    cd = SSD_CONV_DIM
    consts = [w['conv_w'], w['conv_b'], w['dt_bias16'], w['a_log16'],
              w['d_skip_e'], w['gate_norm_g'], w['expand3'], w['tril3']]
    return pl.pallas_call(
        _ssd_prompt_kernel,
        grid=(t // tile,),
        in_specs=[_rows(tile, D_MODEL)] + _ssd_weight_specs() + [_resident(c.shape) for c in consts],
        out_specs=[_rows(tile, HALF), pl.BlockSpec((8, cd), lambda i: (0, 0)),
                   pl.BlockSpec((HALF, SSD_STATE), lambda i: (0, 0))],
        out_shape=[jax.ShapeDtypeStruct((t, HALF), BF16), jax.ShapeDtypeStruct((8, cd), F32),
                   jax.ShapeDtypeStruct((HALF, SSD_STATE), F32)],
        scratch_shapes=[pltpu.VMEM((tile, cd), F32), pltpu.VMEM((tile, HALF), F32),
                        pltpu.VMEM((tile, HEAD_LANES), F32), pltpu.VMEM((SSD_GROUPS, SSD_STATE, SSD_GW), F32),
                        pltpu.VMEM((8 + tile, cd // 3), F32)],
        compiler_params=_params(),
        name="ssd_prompt",
    )(h, w['w1'], w['w1'], w['w1'], w['w1'], w['w1'], *consts)


def _ssd_sample_pre_kernel(h_ref, wz_ref, wx0_ref, wx1_ref, wx2_ref, wdt_ref, cw_ref, cb_ref, st_ref,
                           dtb_ref, aloge_ref, dskip_ref, e3_ref, seg_ref,
                           nconv_ref, z_ref, ysk_ref, eacs_ref, xw_ref, dec_ref, b_ref, c_ref, raw_scr):
    t = h_ref.shape[0]
    n_seq = t // DEC_SEQ
    h = h_ref[...]
    z_ref[...] = _dot_wt(h, wz_ref[...])
    raw = jnp.concatenate([_dot_wt(h, wx0_ref[...]), _dot_wt(h, wx1_ref[...]), _dot_wt(h, wx2_ref[...])], axis=1)
    for c in range(raw_scr.shape[0]):
        lanes = slice(c * 128, (c + 1) * 128)
        raw_scr[c] = raw[:, lanes]
        for j in range(3):
            nconv_ref[j, :, lanes] = raw_scr[c, pl.ds(j + 1, n_seq, stride=DEC_SEQ), :]
    dt16 = _softplus(_dt_proj(h, wdt_ref) + dtb_ref[...])
    dt = _dot(jnp.concatenate(_split3(dt16), axis=1), e3_ref[...])
    old = [st_ref[j] for j in range(3)]
    p1 = _place_steps(t, [(0, old[2])])
    p2 = _place_steps(t, [(0, old[1]), (1, old[2])])
    p3 = _place_steps(t, [(0, old[0]), (1, old[1]), (2, old[2])])

    def step_of(width):
        return lax.broadcasted_iota(jnp.int32, (t, width), 0) % DEC_SEQ

    def back(a, k):
        return jnp.where(step_of(a.shape[1]) >= k, pltpu.roll(a, k, 0), 0.0)

    def ahead(a, k):
        return jnp.where(step_of(a.shape[1]) + k < DEC_SEQ, pltpu.roll(a, t - k, 0), 0.0)

    conv = (raw * cw_ref[3:4, :] + (back(raw, 1) + p1) * cw_ref[2:3, :]
            + (back(raw, 2) + p2) * cw_ref[1:2, :] + (back(raw, 3) + p3) * cw_ref[0:1, :]
            + cb_ref[...])
    xbc = _silu(conv)
    xs = xbc[:, 0:HALF]
    bm = xbc[:, HALF:HALF + 2 * SSD_STATE]
    cm = xbc[:, HALF + 2 * SSD_STATE:]
    b_ref[...] = bm
    c_ref[...] = cm
    da = dt * (-jnp.exp(aloge_ref[...]))
    acs = da + back(da, 1) + back(da, 2) + back(da, 3)
    suffix = ahead(da, 1) + ahead(da, 2) + ahead(da, 3)
    xdt = xs * dt
    y = _dot((cm * bm).astype(BF16), seg_ref[...]) * xdt
    for k in range(1, DEC_SEQ):
        cbk = _dot((cm * pltpu.roll(bm, k, 0)).astype(BF16), seg_ref[...])
        term = cbk * jnp.exp(acs - pltpu.roll(acs, k, 0)) * pltpu.roll(xdt, k, 0)
        y = y + jnp.where(step_of(HALF) >= k, term, 0.0)
    ysk_ref[...] = y + dskip_ref[...] * xs
    eacs_ref[...] = jnp.exp(acs)
    xw_ref[...] = jnp.exp(suffix) * xdt
    dec_ref[...] = jnp.exp(acs + suffix)


def _ssd_sample_pre(h, conv_state, w):
    t = h.shape[0]
    cd = SSD_CONV_DIM
    tile = min(SSD_PRE_ROWS, t)
    seqs = tile // DEC_SEQ
    state_spec = pl.BlockSpec((3, seqs, cd), lambda i: (0, i, 0))
    head = [w['conv_w'], w['conv_b']]
    tail = [w['dt_bias16'], w['a_log_e'], w['d_skip_e'], w['expand3'], w['seg_expand']]
    args = [h, w['w1'], w['w1'], w['w1'], w['w1'], w['w1']] + head + [conv_state] + tail
    wide = jax.ShapeDtypeStruct((t, HALF), F32)
    narrow = jax.ShapeDtypeStruct((t, 2 * SSD_STATE), F32)
    out_shape = [jax.ShapeDtypeStruct(conv_state.shape, F32), wide, wide, wide, wide, wide, narrow, narrow]
    return pl.pallas_call(
        _ssd_sample_pre_kernel,
        grid=(t // tile,),
        in_specs=[_rows(tile, D_MODEL)] + _ssd_weight_specs()
                 + [_resident(c.shape) for c in head] + [state_spec] + [_resident(c.shape) for c in tail],
        out_specs=[state_spec] + [_rows(tile, HALF)] * 5 + [_rows(tile, 2 * SSD_STATE)] * 2,
        out_shape=out_shape,
        scratch_shapes=[pltpu.VMEM((cd // 128, tile, 128), F32)],
        compiler_params=_params(),
        name="ssd_sample_pre",
    )(*args)


SSD_S_BATCH = 8
SSD_PRE_ROWS = 256


def _ssd_sample_state_kernel(st_ref, c_ref, b_ref, xw_ref, dec_ref, eacs_ref, ysk_ref, z_ref, gn_ref,
                             yd_ref, nst_ref):
    row_n = lax.broadcasted_iota(jnp.int32, (8, SSD_STATE), 0)
    row_w = lax.broadcasted_iota(jnp.int32, (8, SSD_GW), 0)
    row_f = lax.broadcasted_iota(jnp.int32, (8, HALF), 0)
    ones_rows = jnp.where((row_n >= 4) & (row_n < 7), 1.0, 0.0).astype(BF16)
    hpg = SSD_HEADS // SSD_GROUPS

    def pair(p, carry):
        r0 = pl.multiple_of(p * 8, 8)
        rows = pl.ds(r0, 8)
        c8 = c_ref[rows, :].astype(BF16)
        b8 = b_ref[rows, :]
        xw8 = xw_ref[rows, :]
        dec8 = dec_ref[rows, :]
        yoff = []
        for sub in range(2):
            b = 2 * p + sub
            xw_own = xw8 if sub == 0 else pltpu.roll(xw8, 4, 0)
            b_own = b8 if sub == 0 else pltpu.roll(b8, 4, 0)
            hi, mid, lo = (term.astype(F32) for term in _split3(dec8[4 * sub:4 * sub + 1, :]))
            parts = []
            for g in range(SSD_GROUPS):
                lanes = slice(g * SSD_GW, (g + 1) * SSD_GW)
                heads = pl.ds(g * hpg, hpg)
                h0 = st_ref[b, heads].reshape(SSD_GW, SSD_STATE)
                parts.append(_dot_nt(c8[:, g * SSD_STATE:(g + 1) * SSD_STATE], h0.astype(BF16)))
                lhs = jnp.where(row_w < 4, xw_own[:, lanes],
                                jnp.where(row_w == 4, hi[:, lanes],
                                          jnp.where(row_w == 5, mid[:, lanes],
                                                    jnp.where(row_w == 6, lo[:, lanes], 0.0)))).astype(BF16)
                rhs_b = jnp.where(row_n < 4, b_own[:, g * SSD_STATE:(g + 1) * SSD_STATE], 0.0).astype(BF16)
                decay = _dot_tn(lhs, ones_rows)
                nst_ref[b, heads] = (h0 * decay + _dot_tn(lhs, rhs_b)).reshape(hpg, HEAD_DIM, SSD_STATE)
            yoff.append(jnp.concatenate(parts, axis=1))
        yoff8 = jnp.where(row_f < 4, yoff[0], yoff[1])
        y = ysk_ref[rows, :] + yoff8 * eacs_ref[rows, :]
        yd_ref[rows, :] = _group_norm_gate(y, z_ref[rows, :], gn_ref[...])
        return carry

    lax.fori_loop(0, SSD_S_BATCH // 2, pair, 0)


def _ssd_sample_state(state, cm, bm, xw, dec, eacs, ysk, z, gn):
    n_seq = state.shape[0]
    bb = SSD_S_BATCH
    r = bb * DEC_SEQ
    st_spec = pl.BlockSpec((bb, SSD_HEADS, HEAD_DIM, SSD_STATE), lambda i: (i, 0, 0, 0))
    return pl.pallas_call(
        _ssd_sample_state_kernel,
        grid=(n_seq // bb,),
        in_specs=[st_spec, _rows(r, 2 * SSD_STATE), _rows(r, 2 * SSD_STATE)] + [_rows(r, HALF)] * 5
                 + [_resident((1, HALF))],
        out_specs=[_rows(r, HALF), st_spec],
        out_shape=[jax.ShapeDtypeStruct((n_seq * DEC_SEQ, HALF), BF16), jax.ShapeDtypeStruct(state.shape, F32)],
        compiler_params=_params(),
        name="ssd_sample_state",
    )(state, cm, bm, xw, dec, eacs, ysk, z, gn)


def _prep_layer1(g_pre, w_in, ln_g, ln_b, w_s, b_s, conv_w, conv_b, dt_bias, a_log, d_skip, gate_norm_g,
                 w_out, g_post):
    cd = SSD_CONV_DIM
    gw = HALF // CMLP_GROUPS
    w1 = w_in.T

    def lanes16(v):
        return jnp.pad(v.astype(F32), (0, HEAD_LANES - SSD_HEADS)).reshape(1, HEAD_LANES)

    def per_channel(v):
        return jnp.repeat(v.astype(F32), HEAD_DIM).reshape(1, HALF)

    head_of = jnp.arange(HALF) // HEAD_DIM
    expand = (jnp.arange(HEAD_LANES)[:, None] == head_of[None, :]).astype(BF16)
    tril = jnp.tril(jnp.ones((CHUNK, CHUNK), BF16))
    grp_rows = jnp.arange(2 * SSD_STATE) // SSD_STATE
    seg_expand = (grp_rows[:, None] == (head_of // (SSD_HEADS // SSD_GROUPS))[None, :]).astype(BF16)

    w4 = jnp.tril(w_s[:, :DEC_SEQ, :DEC_SEQ])
    steps = jnp.arange(DEC_SEQ)
    coef = []
    for k in range(DEC_SEQ):
        src = steps - k
        ck = jnp.where((src >= 0)[None, :], w4[:, steps, jnp.maximum(src, 0)], 0.0)
        ck = jnp.repeat(ck.T, gw, axis=1)
        coef.append(jnp.concatenate([ck, ck], axis=0))
    bias4 = jnp.repeat(b_s[:, :DEC_SEQ].T, gw, axis=1)
    return dict(
        g_pre=g_pre.reshape(1, D_MODEL), w1=w1, ln_g=ln_g.reshape(1, HALF), ln_b=ln_b.reshape(1, HALF),
        ws_tril=jnp.tril(w_s).astype(BF16),
        bs_rows=jnp.broadcast_to(b_s.astype(F32)[:, :, None], (CMLP_GROUPS, CHUNK, gw)),
        coef=jnp.stack(coef).astype(F32), bias4=jnp.concatenate([bias4, bias4], axis=0).astype(F32),
        conv_w=conv_w, conv_b=conv_b.reshape(1, cd), dt_bias16=lanes16(dt_bias), a_log16=lanes16(a_log),
        a_log_e=per_channel(a_log), d_skip_e=per_channel(d_skip), gate_norm_g=gate_norm_g.reshape(1, HALF),
        expand3=jnp.concatenate([expand] * 3, axis=0), tril3=jnp.concatenate([tril] * 3, axis=1),
        seg_expand=seg_expand, w_out=w_out, g_post=g_post.reshape(1, D_MODEL))


def _layer1_prompt(x, w):
    yc, h = _cmlp_prompt(x, w['g_pre'], w['w1'], w['ln_g'], w['ln_b'], w['ws_tril'], w['bs_rows'])
    yd, tail, ssm = _ssd_prompt(h, w)
    y = _out_proj(yc, yd, x, w['w_out'], w['g_post'])
    return y, tail[5:8], ssm.reshape(SSD_HEADS, HEAD_DIM, SSD_STATE)


def _layer1_sample(x, conv_state, ssm_state, w):
    n_seq = x.shape[0]
    t = n_seq * DEC_SEQ
    rows = x.reshape(t, D_MODEL)
    yc, vn, h = _cmlp_sample(rows, w['g_pre'], w['w1'], w['ln_g'], w['ln_b'], w['coef'], w['bias4'])
    new_conv, z, ysk, eacs, xw, dec, bm, cm = _ssd_sample_pre(h, conv_state.transpose(1, 0, 2), w)
    yd, new_state = _ssd_sample_state(ssm_state, cm, bm, xw, dec, eacs, ysk, z, w['gate_norm_g'])
    y = _out_proj(yc, yd, rows, w['w_out'], w['g_post'])
    return (y.reshape(n_seq, DEC_SEQ, D_MODEL), vn.reshape(n_seq, DEC_SEQ, HALF),
            new_conv.transpose(1, 0, 2), new_state)


def kernel(x_prompt, x_sample, state_conv_a, cache_win_k, cache_win_v, state_conv_d, state_ssm, rel_bias,
           l0_g_pre, l0_w_in, l0_conv_w, l0_sinks, l0_w_out, l0_g_post,
           l1_g_pre, l1_w_in, l1_ln_g, l1_ln_b, l1_w_s, l1_b_s, l1_conv_w, l1_conv_b, l1_dt_bias, l1_a_log,
           l1_d_skip, l1_gate_norm_g, l1_w_out, l1_g_post):
    w0 = _prep_layer0(l0_g_pre, l0_w_in, l0_conv_w, rel_bias, l0_sinks, l0_w_out, l0_g_post)
    w1 = _prep_layer1(l1_g_pre, l1_w_in, l1_ln_g, l1_ln_b, l1_w_s, l1_b_s, l1_conv_w, l1_conv_b, l1_dt_bias,
                      l1_a_log, l1_d_skip, l1_gate_norm_g, l1_w_out, l1_g_post)
    yp, p_conv_a, p_win_k, p_win_v = _layer0_prompt(x_prompt[0], w0)
    ys, s_conv_a, s_win_k, s_win_v = _layer0_sample(x_sample, state_conv_a, cache_win_k, cache_win_v, w0)
    yp, p_conv_d, p_ssm = _layer1_prompt(yp, w1)
    ys, s_chunk_v, s_conv_d, s_ssm = _layer1_sample(ys, state_conv_d, state_ssm, w1)
    return (yp[None], ys, p_conv_a[None], s_conv_a, p_win_k[None], p_win_v[None], s_win_k, s_win_v, s_chunk_v,
            p_conv_d[None], s_conv_d, p_ssm[None], s_ssm)
```

```python
import functools
import math

import jax
import jax.numpy as jnp
from jax import lax
from jax.experimental import pallas as pl
from jax.experimental.pallas import tpu as pltpu

F32 = jnp.float32
BF16 = jnp.bfloat16

D_MODEL = 2048
HALF = 1024
HEAD_DIM = 64
N_HEADS = 16
N_KV = 4
GROUP = 4
WINDOW = 128
NUM_BUCKETS = 32
MAX_DISTANCE = 128
CMLP_GROUPS = 8
CHUNK = 128
SSD_HEADS = 16
SSD_STATE = 128
SSD_GROUPS = 2
SSD_CONV_DIM = HALF + 2 * SSD_GROUPS * SSD_STATE
DEC_SEQ = 4
NORM_EPS = 1e-6
MASK_VALUE = -1e30

ROW_TILE = 512
VMEM_LIMIT = 56 * 1024 * 1024


def _params(n_axes=1):
    return pltpu.CompilerParams(dimension_semantics=("arbitrary",) * n_axes,
                                vmem_limit_bytes=VMEM_LIMIT)


def _resident(shape):
    nd = len(shape)
    return pl.BlockSpec(shape, lambda *_: (0,) * nd, pipeline_mode=pl.Buffered(1))


def _rows(tile, width):
    return pl.BlockSpec((tile, width), lambda i: (i, 0))


def _cols(rows, width, block):
    return pl.BlockSpec((rows, width), lambda *_: (0, block), pipeline_mode=pl.Buffered(1))


def _rowwin(height, cols, block):
    return pl.BlockSpec((height, cols), lambda *_: (block, 0), pipeline_mode=pl.Buffered(1))


def _rms_bf16(x, g):
    ms = jnp.mean(x * x, axis=-1, keepdims=True)
    return (x * lax.rsqrt(ms + NORM_EPS) * g).astype(BF16)


def _silu(x):
    return x * jax.nn.sigmoid(x)


def _dot(a, b):
    return jnp.dot(a, b, preferred_element_type=F32)


def _dot_nt(a, b):
    return lax.dot_general(a, b, (((1,), (1,)), ((), ())), preferred_element_type=F32)


def _dot_tn(a, b):
    return lax.dot_general(a, b, (((0,), (0,)), ((), ())), preferred_element_type=F32)


def _dot_w(a, w):
    return _dot(a, w.astype(BF16))


def _dot_wt(a, wt):
    return _dot_nt(a, wt.astype(BF16))


def _split3(x):
    hi = x.astype(BF16)
    r1 = x - hi.astype(F32)
    mid = r1.astype(BF16)
    lo = (r1 - mid.astype(F32)).astype(BF16)
    return hi, mid, lo


def _place_steps(t, placements):
    n_seq = placements[0][1].shape[0]
    row = lax.broadcasted_iota(jnp.int32, (t, n_seq), 0)
    seq = lax.broadcasted_iota(jnp.int32, (t, n_seq), 1)
    lhs, rhs = [], []
    for step, state in placements:
        sel = jnp.where(row == DEC_SEQ * seq + step, 1.0, 0.0).astype(BF16)
        lhs += [sel] * 3
        rhs += list(_split3(state))
    return _dot(jnp.concatenate(lhs, axis=1), jnp.concatenate(rhs, axis=0))


def _out_proj_kernel(ya_ref, yb_ref, x_ref, w_ref, g_ref, o_ref):
    y = _dot_w(ya_ref[...], w_ref[0:HALF, :]) + _dot_w(yb_ref[...], w_ref[HALF:2 * HALF, :])
    ms = jnp.mean(y * y, axis=-1, keepdims=True)
    o_ref[...] = x_ref[...] + y * lax.rsqrt(ms + NORM_EPS) * g_ref[...]


def _out_proj(ya, yb, x, w_bf, g):
    t = x.shape[0]
    tile = min(ROW_TILE, t)
    return pl.pallas_call(
        _out_proj_kernel,
        grid=(t // tile,),
        in_specs=[_rows(tile, HALF), _rows(tile, HALF), _rows(tile, D_MODEL),
                  _resident((2 * HALF, D_MODEL)), _resident((1, D_MODEL))],
        out_specs=_rows(tile, D_MODEL),
        out_shape=jax.ShapeDtypeStruct((t, D_MODEL), F32),
        compiler_params=_params(),
        name="out_proj",
    )(ya, yb, x, w_bf, g)


CONV_A_CHUNK = 256


def _conv_a_kernel(*refs, sample):
    if sample:
        x_ref, g_ref, w_ref, cw_ref, st_ref, ya_ref, s_ref, h_ref = refs
    else:
        x_ref, g_ref, w_ref, cw_ref, ya_ref, s_ref, h_ref, shift_scr = refs
    tile = x_ref.shape[0]
    cc = CONV_A_CHUNK
    h = _rms_bf16(x_ref[...], g_ref[...])
    h_ref[...] = h
    if not sample:
        @pl.when(pl.program_id(0) == 0)
        def _():
            s_ref[...] = jnp.zeros_like(s_ref)
    for c in range(HALF // cc):
        lanes = slice(c * cc, (c + 1) * cc)
        a_b, a_c, a_h, a_g = (_dot_w(h, w_ref[:, j * HALF + c * cc:j * HALF + (c + 1) * cc]) for j in range(4))
        s = a_c * a_h
        if sample:
            t_in = lax.broadcasted_iota(jnp.int32, s.shape, 0) % DEC_SEQ
            old0 = st_ref[:, c * cc:(c + 1) * cc]
            old1 = st_ref[:, HALF + c * cc:HALF + (c + 1) * cc]
            p1 = jnp.where(t_in >= 1, pltpu.roll(s, 1, 0), 0.0) + _place_steps(tile, [(0, old1)])
            p2 = jnp.where(t_in >= 2, pltpu.roll(s, 2, 0), 0.0) + _place_steps(tile, [(0, old0), (1, old1)])
            s_ref[:, lanes] = s
        else:
            shift_scr[0:8, :] = s_ref[:, lanes]
            shift_scr[8:8 + tile, :] = s
            p1 = shift_scr[7:7 + tile, :]
            p2 = shift_scr[6:6 + tile, :]
            s_ref[:, lanes] = s[tile - 8:tile, :]
        conv = p2 * cw_ref[0:1, lanes] + p1 * cw_ref[1:2, lanes] + s * cw_ref[2:3, lanes]
        ya_ref[:, lanes] = (a_b * conv * _silu(a_g)).astype(BF16)


def _conv_a(x, g_pre, w0, conv_w, state=None):
    t = x.shape[0]
    sample = state is not None
    tile = t if sample else min(ROW_TILE, t)
    in_specs = [_rows(tile, D_MODEL), _resident((1, D_MODEL)), _cols(D_MODEL, 4 * HALF, 0), _resident((3, HALF))]
    args = [x, g_pre, w0, conv_w]
    scratch = []
    if sample:
        in_specs.append(_resident(state.shape))
        args.append(state)
        s_spec, s_shape = _rows(tile, HALF), (t, HALF)
    else:
        s_spec, s_shape = pl.BlockSpec((8, HALF), lambda i: (0, 0)), (8, HALF)
        scratch = [pltpu.VMEM((8 + tile, CONV_A_CHUNK), F32)]
    return pl.pallas_call(
        functools.partial(_conv_a_kernel, sample=sample),
        grid=(t // tile,),
        in_specs=in_specs,
        out_specs=[_rows(tile, HALF), s_spec, _rows(tile, D_MODEL)],
        out_shape=[jax.ShapeDtypeStruct((t, HALF), BF16), jax.ShapeDtypeStruct(s_shape, F32),
                   jax.ShapeDtypeStruct((t, D_MODEL), BF16)],
        scratch_shapes=scratch,
        compiler_params=_params(),
        name="conv_a_sample" if sample else "conv_a_prompt",
    )(*args)


def _rel_bucket(dist):
    max_exact = NUM_BUCKETS // 2
    d = jnp.maximum(dist, 0)
    ratio = jnp.maximum(d, max_exact).astype(F32) / max_exact
    large = max_exact + (jnp.log(ratio) / math.log(MAX_DISTANCE / max_exact)
                         * (NUM_BUCKETS - max_exact)).astype(jnp.int32)
    return jnp.where(d < max_exact, d, jnp.minimum(large, NUM_BUCKETS - 1))


def _attn_softmax_pv(s, sink, v_bf, v_transposed=False):
    m = jnp.maximum(jnp.max(s, axis=-1, keepdims=True), sink)
    p = jnp.exp(s - m)
    den = jnp.sum(p, axis=-1, keepdims=True) + jnp.exp(sink - m)
    pv = _dot_nt(p.astype(BF16), v_bf) if v_transposed else _dot(p.astype(BF16), v_bf)
    return pv / den


def _attn_prompt_kernel(h_ref, wq_ref, wkv_ref, wg0_ref, wg1_ref, tab_ref, sink_ref, yb_ref, kwin_ref,
                        vwin_ref, q_scr, gate_scr, k_scr, v_scr, bias_scr):
    tile = h_ref.shape[0]
    i = pl.program_id(0)
    kv_w = N_KV * HEAD_DIM
    h = h_ref[...]

    kw, vw = 2 * HEAD_DIM, 4 * HEAD_DIM

    @pl.when(i == 0)
    def _():
        k_scr[0:WINDOW, :] = jnp.zeros((WINDOW, N_KV * kw), BF16)
        v_scr[0:WINDOW, :] = jnp.zeros((WINDOW, N_KV * vw), BF16)
        for hk in range(N_KV):
            v_scr[:, hk * vw + kw:(hk + 1) * vw] = jnp.ones((tile + WINDOW, kw), BF16)
        in_own = lax.broadcasted_iota(jnp.int32, (WINDOW, 2 * WINDOW), 1) >= WINDOW
        for head in range(N_HEADS):
            row = jnp.broadcast_to(tab_ref[head:head + 1, :], (WINDOW, BIAS_SPAN))
            band = pltpu.roll(row, 0, 1, stride=1, stride_axis=0)[:, 0:2 * WINDOW]
            rows = slice((head % 2) * WINDOW, (head % 2 + 1) * WINDOW)
            bias_scr[1, head // 2, rows, :] = band
            bias_scr[0, head // 2, rows, :] = jnp.where(in_own, band, MASK_VALUE)

    q_scr[...] = (_dot_w(h, wq_ref[...]) * (HEAD_DIM ** -0.5)).astype(BF16)
    k = _dot_w(h, wkv_ref[:, 0:kv_w])
    v = _dot_w(h, wkv_ref[:, kv_w:2 * kv_w])
    gate_scr[:, 0:HALF // 2] = _silu(_dot_w(h, wg0_ref[...]))
    gate_scr[:, HALF // 2:HALF] = _silu(_dot_w(h, wg1_ref[...]))
    for hk in range(N_KV):
        k_h = k[:, hk * HEAD_DIM:(hk + 1) * HEAD_DIM].astype(BF16)
        v_h = v[:, hk * HEAD_DIM:(hk + 1) * HEAD_DIM].astype(BF16)
        k_scr[WINDOW:WINDOW + tile, hk * kw:(hk + 1) * kw] = jnp.concatenate([k_h, k_h], axis=1)
        v_scr[WINDOW:WINDOW + tile, hk * vw:hk * vw + kw] = jnp.concatenate([v_h, v_h], axis=1)
    kwin_ref[...] = k[tile - WINDOW:tile, :]
    vwin_ref[...] = v[tile - WINDOW:tile, :]

    lane = lax.broadcasted_iota(jnp.int32, (WINDOW, kw), 1)
    lo = lane < HEAD_DIM
    keep_a = jnp.where(lo, 1.0, 0.0).astype(BF16)
    keep_b = jnp.where(lo, 0.0, 1.0).astype(BF16)
    is_a = lax.broadcasted_iota(jnp.int32, (2 * WINDOW, 1), 0) < WINDOW

    def block(n, carry):
        r0 = pl.multiple_of(n * WINDOW, WINDOW)
        rows = pl.ds(r0, WINDOW)
        keys = pl.ds(r0, 2 * WINDOW)
        first = jnp.where(jnp.logical_and(i == 0, n == 0), 0, 1)
        for hk in range(N_KV):
            for gp in range(GROUP // 2):
                a = hk * GROUP + 2 * gp
                slab = slice(a * HEAD_DIM, (a + 2) * HEAD_DIM)
                q2 = q_scr[rows, slab]
                lhs = jnp.concatenate([q2 * keep_a, q2 * keep_b], axis=0)
                s = _dot_nt(lhs, k_scr[keys, hk * kw:(hk + 1) * kw]) + bias_scr[first, a // 2]
                sink = jnp.where(is_a, sink_ref[a], sink_ref[a + 1])
                m = jnp.maximum(jnp.max(s, axis=-1, keepdims=True), sink)
                p = jnp.exp(s - m).astype(BF16)
                pv = _dot(p, v_scr[keys, hk * vw:(hk + 1) * vw])
                num = jnp.where(lo, pv[0:WINDOW, 0:kw], pv[WINDOW:2 * WINDOW, 0:kw])
                den = jnp.where(lo, pv[0:WINDOW, kw:2 * kw], pv[WINDOW:2 * WINDOW, kw:2 * kw])
                m_slab = jnp.where(lo, m[0:WINDOW], m[WINDOW:2 * WINDOW])
                den = den + jnp.exp(jnp.where(lo, sink_ref[a], sink_ref[a + 1]) - m_slab)
                yb_ref[rows, slab] = (num / den * gate_scr[rows, slab]).astype(BF16)
        return carry

    lax.fori_loop(0, tile // WINDOW, block, 0, unroll=True)
    k_scr[0:WINDOW, :] = k_scr[tile:tile + WINDOW, :]
    v_scr[0:WINDOW, :] = v_scr[tile:tile + WINDOW, :]


BIAS_SPAN = 3 * WINDOW


def _prompt_bias_table(rel_bias):
    dist = WINDOW - jnp.arange(BIAS_SPAN)
    table = jnp.where(((dist >= 0) & (dist < WINDOW))[:, None], rel_bias.astype(F32)[_rel_bucket(dist)], MASK_VALUE)
    return table.T


def _attn_prompt(h, w0, table, sinks):
    t = h.shape[0]
    tile = min(ROW_TILE, t)
    kv_w = N_KV * HEAD_DIM
    win_spec = pl.BlockSpec((WINDOW, kv_w), lambda i: (0, 0))
    return pl.pallas_call(
        _attn_prompt_kernel,
        grid=(t // tile,),
        in_specs=[_rows(tile, D_MODEL),
                  _cols(D_MODEL, HALF, 4), _cols(D_MODEL, 2 * kv_w, 10),
                  _cols(D_MODEL, HALF // 2, 11), _cols(D_MODEL, HALF // 2, 12),
                  _resident(table.shape), pl.BlockSpec(memory_space=pltpu.SMEM)],
        out_specs=[_rows(tile, HALF), win_spec, win_spec],
        out_shape=[jax.ShapeDtypeStruct((t, HALF), BF16),
                   jax.ShapeDtypeStruct((WINDOW, kv_w), F32), jax.ShapeDtypeStruct((WINDOW, kv_w), F32)],
        scratch_shapes=[pltpu.VMEM((tile, HALF), BF16), pltpu.VMEM((tile, HALF), F32),
                        pltpu.VMEM((tile + WINDOW, 2 * kv_w), BF16), pltpu.VMEM((tile + WINDOW, 4 * kv_w), BF16),
                        pltpu.VMEM((2, N_HEADS // 2, 2 * WINDOW, 2 * WINDOW), F32)],
        compiler_params=_params(),
        name="attn_prompt",
    )(h, w0, w0, w0, w0, table, sinks)


def _attn_proj_kernel(h_ref, wq_ref, wkv_ref, wg0_ref, wg1_ref, qg_ref, kt_ref, vt_ref, kv_scr):
    kv_w = N_KV * HEAD_DIM
    h = h_ref[...]
    q = _dot_w(h, wq_ref[...]) * (HEAD_DIM ** -0.5)
    for hk in range(N_KV):
        for g in range(GROUP):
            src = (hk * GROUP + g) * HEAD_DIM
            dst = (g * N_KV + hk) * HEAD_DIM
            qg_ref[:, dst:dst + HEAD_DIM] = q[:, src:src + HEAD_DIM]
    qg_ref[:, HALF:HALF + HALF // 2] = _dot_w(h, wg0_ref[...])
    qg_ref[:, HALF + HALF // 2:2 * HALF] = _dot_w(h, wg1_ref[...])
    kv_scr[...] = _dot_w(h, wkv_ref[...])
    for j in range(kt_ref.shape[0]):
        kt_ref[j] = kv_scr[j * WINDOW:(j + 1) * WINDOW, 0:kv_w].T
        vt_ref[j] = kv_scr[j * WINDOW:(j + 1) * WINDOW, kv_w:2 * kv_w].T


def _attn_proj(h, w0):
    t = h.shape[0]
    kv_w = N_KV * HEAD_DIM
    out_shape = [jax.ShapeDtypeStruct((t, 2 * HALF), F32), jax.ShapeDtypeStruct((t // WINDOW, kv_w, WINDOW), F32),
                 jax.ShapeDtypeStruct((t // WINDOW, kv_w, WINDOW), F32)]
    return pl.pallas_call(
        _attn_proj_kernel,
        grid=(1,),
        in_specs=[_resident((t, D_MODEL)),
                  _cols(D_MODEL, HALF, 4), _cols(D_MODEL, 2 * kv_w, 10),
                  _cols(D_MODEL, HALF // 2, 11), _cols(D_MODEL, HALF // 2, 12)],
        out_specs=[_resident(s.shape) for s in out_shape],
        out_shape=out_shape,
        scratch_shapes=[pltpu.VMEM((t, 2 * kv_w), F32)],
        compiler_params=_params(),
        name="attn_proj_sample",
    )(h, w0, w0, w0, w0)


ATTN_S_BATCH = 16
KEYS_PAD = 2 * WINDOW


def _attn_sample_kernel(qg_ref, ktn_ref, vtn_ref, ck_ref, cv_ref, bias_ref, sink_ref, yb_ref, nk_ref, nv_ref):
    kv_w = N_KV * HEAD_DIM
    row8 = lax.broadcasted_iota(jnp.int32, (8, kv_w), 0)
    lane_head = lax.broadcasted_iota(jnp.int32, (8, kv_w), 1) // HEAD_DIM
    lower = row8 < DEC_SEQ
    pick = [jnp.where(lane_head == 2 * hp + jnp.where(lower, 0, 1), 1.0, 0.0).astype(F32) for hp in range(2)]
    lower_w = lax.broadcasted_iota(jnp.int32, (8, HALF), 0) < DEC_SEQ
    kept = lax.broadcasted_iota(jnp.int32, (kv_w, WINDOW), 1) < WINDOW - DEC_SEQ
    seq0 = pl.program_id(0) * ATTN_S_BATCH
    per_tile = WINDOW // DEC_SEQ

    def slide(old, new_tile, shift):
        return jnp.where(kept, pltpu.roll(old, WINDOW - DEC_SEQ, 1), pltpu.roll(new_tile, shift, 1))

    def pair(p, carry):
        r0 = pl.multiple_of(p * 8, 8)
        rows = qg_ref[pl.ds(r0, 8), :]
        q8 = rows[:, 0:HALF]
        gate8 = rows[:, HALF:2 * HALF]
        out8 = []
        for sub in range(2):
            b = 2 * p + sub
            q_swap = pltpu.roll(q8, 4, 0)
            q_dup = jnp.where(lower_w, q8, q_swap) if sub == 0 else jnp.where(lower_w, q_swap, q8)
            tile = (seq0 + b) // per_tile
            shift = (2 * WINDOW - DEC_SEQ - DEC_SEQ * ((seq0 + b) % per_tile)) % WINDOW
            k_old = ck_ref[b].reshape(kv_w, WINDOW)
            v_old = cv_ref[b].reshape(kv_w, WINDOW)
            k_win = slide(k_old, ktn_ref[tile], shift)
            v_win = slide(v_old, vtn_ref[tile], shift)
            nk_ref[b] = k_win.reshape(N_KV, HEAD_DIM, WINDOW)
            nv_ref[b] = v_win.reshape(N_KV, HEAD_DIM, WINDOW)
            k_all = jnp.concatenate([k_old, k_win], axis=1).astype(BF16)
            v_all = jnp.concatenate([v_old, v_win], axis=1).astype(BF16)
            q_bd = jnp.concatenate(
                [q_dup[:, g * kv_w:(g + 1) * kv_w] * pick[hp] for g in range(GROUP) for hp in range(2)], axis=0)
            s = _dot(q_bd.astype(BF16), k_all) + bias_ref[...]
            o = _attn_softmax_pv(s, sink_ref[:, 0:1], v_all, v_transposed=True)
            out_g = []
            for g in range(GROUP):
                acc = None
                for hp in range(2):
                    piece = o[(2 * g + hp) * 8:(2 * g + hp + 1) * 8, :] * pick[hp]
                    piece = piece + pltpu.roll(piece, 4, 0)
                    acc = piece if acc is None else acc + piece
                out_g.append(acc)
            out8.append(jnp.concatenate(
                [out_g[g][:, hk * HEAD_DIM:(hk + 1) * HEAD_DIM] for hk in range(N_KV) for g in range(GROUP)], axis=1))
        o8 = jnp.where(lower_w, out8[0], out8[1])
        yb_ref[pl.ds(r0, 8), :] = (o8 * _silu(gate8)).astype(BF16)
        return carry

    lax.fori_loop(0, ATTN_S_BATCH // 2, pair, 0, unroll=2)


def _sample_bias(rel_bias, sinks):
    t = jnp.arange(DEC_SEQ)[:, None]
    j = jnp.arange(KEYS_PAD)[None, :]
    pos = jnp.where(j < WINDOW, j, j - (KEYS_PAD - DEC_SEQ) + WINDOW)
    dist = t + WINDOW - pos
    valid = (dist >= 0) & (dist < WINDOW) & ((j < WINDOW) | (j >= KEYS_PAD - DEC_SEQ))
    bias = jnp.where(valid[:, :, None], rel_bias.astype(F32)[_rel_bucket(dist)], MASK_VALUE)
    bias = bias.reshape(DEC_SEQ, KEYS_PAD, N_KV, GROUP).transpose(3, 2, 0, 1).reshape(N_HEADS * DEC_SEQ, KEYS_PAD)
    sink = jnp.broadcast_to(sinks.astype(F32).reshape(N_KV, GROUP).T[:, :, None], (GROUP, N_KV, DEC_SEQ))
    return bias, jnp.broadcast_to(sink.reshape(N_HEADS * DEC_SEQ, 1), (N_HEADS * DEC_SEQ, 128))


def _attn_sample(qg, kt_new, vt_new, cache_kt, cache_vt, bias, sink):
    n_seq = cache_kt.shape[0]
    bb = ATTN_S_BATCH
    cache_spec = pl.BlockSpec((bb, N_KV, HEAD_DIM, WINDOW), lambda i: (i, 0, 0, 0))
    return pl.pallas_call(
        _attn_sample_kernel,
        grid=(n_seq // bb,),
        in_specs=[_rows(bb * DEC_SEQ, 2 * HALF), _resident(kt_new.shape), _resident(vt_new.shape),
                  cache_spec, cache_spec, _resident(bias.shape), _resident(sink.shape)],
        out_specs=[_rows(bb * DEC_SEQ, HALF), cache_spec, cache_spec],
        out_shape=[jax.ShapeDtypeStruct((n_seq * DEC_SEQ, HALF), BF16),
                   jax.ShapeDtypeStruct(cache_kt.shape, F32), jax.ShapeDtypeStruct(cache_vt.shape, F32)],
        compiler_params=_params(),
        name="attn_sample",
    )(qg, kt_new, vt_new, cache_kt, cache_vt, bias, sink)


def _prep_layer0(g_pre, w_in, conv_w, rel_bias, sinks, w_out, g_post):
    return dict(
        g_pre=g_pre.reshape(1, D_MODEL), w0=w_in, conv_w=conv_w, rel_bias=rel_bias, sinks=sinks,
        w_out=w_out, g_post=g_post.reshape(1, D_MODEL))


def _layer0_prompt(x, w):
    ya, s_tail, h = _conv_a(x, w['g_pre'], w['w0'], w['conv_w'])
    yb, kwin, vwin = _attn_prompt(h, w['w0'], _prompt_bias_table(w['rel_bias']), w['sinks'])
    y = _out_proj(ya, yb, x, w['w_out'], w['g_post'])
    return (y, s_tail[6:8], kwin.reshape(WINDOW, N_KV, HEAD_DIM), vwin.reshape(WINDOW, N_KV, HEAD_DIM))


def _layer0_sample(x, conv_state, cache_k, cache_v, w):
    n_seq = x.shape[0]
    rows = x.reshape(n_seq * DEC_SEQ, D_MODEL)
    ya, s, h = _conv_a(rows, w['g_pre'], w['w0'], w['conv_w'], conv_state.reshape(n_seq, 2 * HALF))
    qg, kt_new, vt_new = _attn_proj(h, w['w0'])
    bias, sink = _sample_bias(w['rel_bias'], w['sinks'])
    yb, new_kt, new_vt = _attn_sample(qg, kt_new, vt_new, cache_k.transpose(0, 2, 3, 1), cache_v.transpose(0, 2, 3, 1),
                                      bias, sink)
    y = _out_proj(ya, yb, rows, w['w_out'], w['g_post'])
    return (y.reshape(n_seq, DEC_SEQ, D_MODEL), s.reshape(n_seq, DEC_SEQ, HALF)[:, DEC_SEQ - 2:],
            new_kt.transpose(0, 3, 1, 2), new_vt.transpose(0, 3, 1, 2))


def _layer_norm(v, g, b):
    xc = v - jnp.mean(v, axis=-1, keepdims=True)
    return xc * lax.rsqrt(jnp.mean(xc * xc, axis=-1, keepdims=True) + NORM_EPS) * g + b


def _cmlp_prompt_kernel(x_ref, g_ref, w_ref, lng_ref, lnb_ref, ws_ref, bs_ref, yc_ref, h_ref, vn_scr):
    tile = x_ref.shape[0]
    h = _rms_bf16(x_ref[...], g_ref[...])
    h_ref[...] = h
    v = _dot_wt(h, w_ref[HALF:2 * HALF, :])
    vn_scr[...] = _layer_norm(v, lng_ref[...], lnb_ref[...]).astype(BF16)
    gw = HALF // CMLP_GROUPS
    cols = 2 * gw
    for cb in range(HALF // cols):
        u = _dot_wt(h, w_ref[cb * cols:(cb + 1) * cols, :])
        gate = _silu(_dot_wt(h, w_ref[2 * HALF + cb * cols:2 * HALF + (cb + 1) * cols, :]))
        for gi in range(2):
            grp = 2 * cb + gi
            lanes = slice(grp * gw, (grp + 1) * gw)
            for n in range(tile // CHUNK):
                rows = slice(n * CHUNK, (n + 1) * CHUNK)
                mixed = _dot(ws_ref[grp], vn_scr[rows, lanes]) + bs_ref[grp]
                yc_ref[rows, lanes] = (u[rows, gi * gw:(gi + 1) * gw] * mixed
                                       * gate[rows, gi * gw:(gi + 1) * gw]).astype(BF16)


def _cmlp_prompt(x, g_pre, w_c, ln_g, ln_b, ws_tril, bs_rows):
    t = x.shape[0]
    tile = min(ROW_TILE, t)
    return pl.pallas_call(
        _cmlp_prompt_kernel,
        grid=(t // tile,),
        in_specs=[_rows(tile, D_MODEL), _resident((1, D_MODEL)), _rowwin(3 * HALF, D_MODEL, 0),
                  _resident((1, HALF)), _resident((1, HALF)), _resident(ws_tril.shape), _resident(bs_rows.shape)],
        out_specs=[_rows(tile, HALF), _rows(tile, D_MODEL)],
        out_shape=[jax.ShapeDtypeStruct((t, HALF), BF16), jax.ShapeDtypeStruct((t, D_MODEL), BF16)],
        scratch_shapes=[pltpu.VMEM((tile, HALF), BF16)],
        compiler_params=_params(),
        name="cmlp_prompt",
    )(x, g_pre, w_c, ln_g, ln_b, ws_tril, bs_rows)


def _cmlp_sample_kernel(x_ref, g_ref, w_ref, lng_ref, lnb_ref, coef_ref, bias_ref, yc_ref, vn_ref, h_ref):
    t = x_ref.shape[0]
    h = _rms_bf16(x_ref[...], g_ref[...])
    h_ref[...] = h
    u = _dot_wt(h, w_ref[0:HALF, :])
    vn = _layer_norm(_dot_wt(h, w_ref[HALF:2 * HALF, :]), lng_ref[...], lnb_ref[...])
    gate = _silu(_dot_wt(h, w_ref[2 * HALF:3 * HALF, :]))
    vn_ref[...] = vn

    def tiled(a):
        return a.reshape(t // 8, 8, HALF)

    mixed = tiled(vn) * coef_ref[0][None] + bias_ref[...][None]
    for k in range(1, DEC_SEQ):
        mixed = mixed + tiled(pltpu.roll(vn, k, 0)) * coef_ref[k][None]
    yc_ref[...] = (u * mixed.reshape(t, HALF) * gate).astype(BF16)


def _cmlp_sample(x, g_pre, w_c, ln_g, ln_b, coef, bias):
    t = x.shape[0]
    return pl.pallas_call(
        _cmlp_sample_kernel,
        grid=(1,),
        in_specs=[_resident((t, D_MODEL)), _resident((1, D_MODEL)), _rowwin(3 * HALF, D_MODEL, 0),
                  _resident((1, HALF)), _resident((1, HALF)), _resident(coef.shape), _resident(bias.shape)],
        out_specs=[_resident((t, HALF)), _resident((t, HALF)), _resident((t, D_MODEL))],
        out_shape=[jax.ShapeDtypeStruct((t, HALF), BF16), jax.ShapeDtypeStruct((t, HALF), F32),
                   jax.ShapeDtypeStruct((t, D_MODEL), BF16)],
        compiler_params=_params(),
        name="cmlp_sample",
    )(x, g_pre, w_c, ln_g, ln_b, coef, bias)


HEAD_LANES = 128
SSD_GW = HALF // SSD_GROUPS


def _softplus(x):
    return jnp.maximum(x, 0.0) + jnp.log1p(jnp.exp(-jnp.abs(x)))


def _dt_proj(h, wdt_ref):
    pad = jnp.zeros((HEAD_LANES - SSD_HEADS, D_MODEL), F32)
    return _dot_wt(h, jnp.concatenate([wdt_ref[...], pad], axis=0))


def _group_norm_gate(y, z, gn):
    gated = y * _silu(z)
    parts = []
    for g in range(SSD_GROUPS):
        part = gated[:, g * SSD_GW:(g + 1) * SSD_GW]
        parts.append(part * lax.rsqrt(jnp.mean(part * part, axis=-1, keepdims=True) + NORM_EPS))
    return (jnp.concatenate(parts, axis=1) * gn).astype(BF16)


def _ssd_prompt_kernel(h_ref, wz_ref, wx0_ref, wx1_ref, wx2_ref, wdt_ref, cw_ref, cb_ref, dtb_ref, alog_ref,
                       dskip_ref, gn_ref, e3_ref, tril3_ref, yd_ref, tail_ref, ssm_ref,
                       xbc_scr, z_scr, dt_scr, ht_scr, shift_scr):
    tile = h_ref.shape[0]
    i = pl.program_id(0)
    cd = SSD_CONV_DIM
    h = h_ref[...]

    @pl.when(i == 0)
    def _():
        tail_ref[...] = jnp.zeros_like(tail_ref)
        ht_scr[...] = jnp.zeros_like(ht_scr)

    z_scr[...] = _dot_wt(h, wz_ref[...])
    dt_scr[...] = _softplus(_dt_proj(h, wdt_ref) + dtb_ref[...])
    third = cd // 3
    for j, wx_ref in enumerate((wx0_ref, wx1_ref, wx2_ref)):
        cols = slice(j * third, (j + 1) * third)
        raw = _dot_wt(h, wx_ref[...])
        shift_scr[0:8, :] = tail_ref[:, cols]
        shift_scr[8:8 + tile, :] = raw
        conv = raw * cw_ref[3:4, cols] + cb_ref[:, cols]
        for k in range(1, 4):
            conv = conv + shift_scr[8 - k:8 - k + tile, :] * cw_ref[3 - k:4 - k, cols]
        xbc_scr[:, cols] = _silu(conv)
        tail_ref[:, cols] = raw[tile - 8:tile, :]

    a16 = -jnp.exp(alog_ref[...])
    causal =(lax.broadcasted_iota(jnp.int32, (CHUNK, CHUNK), 0)
              >= lax.broadcasted_iota(jnp.int32, (CHUNK, CHUNK), 1))
    first_half = lax.broadcasted_iota(jnp.int32, (CHUNK, 2 * HEAD_DIM), 1) < HEAD_DIM
    keep_a = jnp.where(first_half, 1.0, 0.0).astype(BF16)
    keep_b = jnp.where(first_half, 0.0, 1.0).astype(BF16)

    def chunk(n, carry):
        r0 = pl.multiple_of(n * CHUNK, CHUNK)
        rows = pl.ds(r0, CHUNK)
        xs = xbc_scr[rows, 0:HALF]
        dt16 = dt_scr[rows, :]
        dt_e = _dot(jnp.concatenate(_split3(dt16), axis=1), e3_ref[...])
        acs16 = _dot(tril3_ref[...], jnp.concatenate(_split3(dt16 * a16), axis=0))
        acs_e = _dot(jnp.concatenate(_split3(acs16), axis=1), e3_ref[...])
        acs_t = acs16.T
        last_e = acs_e[CHUNK - 1:CHUNK, :]
        xdt = xs * dt_e
        xdt_bf = xdt.astype(BF16)
        xw = (jnp.exp(last_e - acs_e) * xdt).astype(BF16)
        dec_e = jnp.exp(last_e)
        y_parts = []
        yoff_parts = []
        for g in range(SSD_GROUPS):
            c_g = xbc_scr[rows, HALF + 2 * SSD_STATE + g * SSD_STATE:HALF + 2 * SSD_STATE + (g + 1) * SSD_STATE].astype(BF16)
            b_g = xbc_scr[rows, HALF + g * SSD_STATE:HALF + (g + 1) * SSD_STATE].astype(BF16)
            cb = _dot_nt(c_g, b_g)
            h_prev = ht_scr[g]
            yoff_parts.append(_dot(c_g, h_prev.astype(BF16)))
            for r in range(0, SSD_HEADS // SSD_GROUPS, 2):
                wgt = []
                for hd in (g * (SSD_HEADS // SSD_GROUPS) + r, g * (SSD_HEADS // SSD_GROUPS) + r + 1):
                    seg = acs16[:, hd:hd + 1] - acs_t[hd:hd + 1, :]
                    wgt.append(cb * jnp.exp(jnp.where(causal, seg, -jnp.inf)))
                a = g * (SSD_HEADS // SSD_GROUPS) + r
                slab = xdt_bf[:, a * HEAD_DIM:(a + 2) * HEAD_DIM]
                rhs = jnp.concatenate([slab * keep_a, slab * keep_b], axis=0)
                y_parts.append(_dot(jnp.concatenate(wgt, axis=1).astype(BF16), rhs))
            lanes = slice(g * SSD_GW, (g + 1) * SSD_GW)
            ht_scr[g] = h_prev * dec_e[:, lanes] + _dot_tn(b_g, xw[:, lanes])
        y = (jnp.concatenate(y_parts, axis=1) + jnp.concatenate(yoff_parts, axis=1) * jnp.exp(acs_e)
             + dskip_ref[...] * xs)
        yd_ref[rows, :] = _group_norm_gate(y, z_scr[rows, :], gn_ref[...])
        return carry

    lax.fori_loop(0, tile // CHUNK, chunk, 0, unroll=True)

    @pl.when(i == pl.num_programs(0) - 1)
    def _():
        for g in range(SSD_GROUPS):
            ssm_ref[g * SSD_GW:(g + 1) * SSD_GW, :] = ht_scr[g].T


def _ssd_weight_specs():
    third = SSD_CONV_DIM // 3
    first = 4 * HALF // third
    return ([_rowwin(HALF, D_MODEL, 3)] + [_rowwin(third, D_MODEL, first + j) for j in range(3)]
            + [_rowwin(SSD_HEADS, D_MODEL, (4 * HALF + SSD_CONV_DIM) // SSD_HEADS)])


def _ssd_prompt(h, w):
    t = h.shape[0]
    tile = min(ROW_TILE, t)
    cd = SSD_CONV_DIM
    consts = [w['conv_w'], w['conv_b'], w['dt_bias16'], w['a_log16'],
              w['d_skip_e'], w['gate_norm_g'], w['expand3'], w['tril3']]
    return pl.pallas_call(
        _ssd_prompt_kernel,
        grid=(t // tile,),
        in_specs=[_rows(tile, D_MODEL)] + _ssd_weight_specs() + [_resident(c.shape) for c in consts],
        out_specs=[_rows(tile, HALF), pl.BlockSpec((8, cd), lambda i: (0, 0)),
                   pl.BlockSpec((HALF, SSD_STATE), lambda i: (0, 0))],
        out_shape=[jax.ShapeDtypeStruct((t, HALF), BF16), jax.ShapeDtypeStruct((8, cd), F32),
                   jax.ShapeDtypeStruct((HALF, SSD_STATE), F32)],
        scratch_shapes=[pltpu.VMEM((tile, cd), F32), pltpu.VMEM((tile, HALF), F32),
                        pltpu.VMEM((tile, HEAD_LANES), F32), pltpu.VMEM((SSD_GROUPS, SSD_STATE, SSD_GW), F32),
                        pltpu.VMEM((8 + tile, cd // 3), F32)],
        compiler_params=_params(),
        name="ssd_prompt",
    )(h, w['w1'], w['w1'], w['w1'], w['w1'], w['w1'], *consts)


def _ssd_sample_pre_kernel(h_ref, wz_ref, wx0_ref, wx1_ref, wx2_ref, wdt_ref, cw_ref, cb_ref, st_ref,
                           dtb_ref, aloge_ref, dskip_ref, e3_ref, seg_ref,
                           nconv_ref, z_ref, ysk_ref, eacs_ref, xw_ref, dec_ref, b_ref, c_ref, raw_scr):
    t = h_ref.shape[0]
    n_seq = t // DEC_SEQ
    h = h_ref[...]
    z_ref[...] = _dot_wt(h, wz_ref[...])
    raw = jnp.concatenate([_dot_wt(h, wx0_ref[...]), _dot_wt(h, wx1_ref[...]), _dot_wt(h, wx2_ref[...])], axis=1)
    for c in range(raw_scr.shape[0]):
        lanes = slice(c * 128, (c + 1) * 128)
        raw_scr[c] = raw[:, lanes]
        for j in range(3):
            nconv_ref[j, :, lanes] = raw_scr[c, pl.ds(j + 1, n_seq, stride=DEC_SEQ), :]
    dt16 = _softplus(_dt_proj(h, wdt_ref) + dtb_ref[...])
    dt = _dot(jnp.concatenate(_split3(dt16), axis=1), e3_ref[...])
    old = [st_ref[j] for j in range(3)]
    p1 = _place_steps(t, [(0, old[2])])
    p2 = _place_steps(t, [(0, old[1]), (1, old[2])])
    p3 = _place_steps(t, [(0, old[0]), (1, old[1]), (2, old[2])])

    def step_of(width):
        return lax.broadcasted_iota(jnp.int32, (t, width), 0) % DEC_SEQ

    def back(a, k):
        return jnp.where(step_of(a.shape[1]) >= k, pltpu.roll(a, k, 0), 0.0)

    def ahead(a, k):
        return jnp.where(step_of(a.shape[1]) + k < DEC_SEQ, pltpu.roll(a, t - k, 0), 0.0)

    conv = (raw * cw_ref[3:4, :] + (back(raw, 1) + p1) * cw_ref[2:3, :]
            + (back(raw, 2) + p2) * cw_ref[1:2, :] + (back(raw, 3) + p3) * cw_ref[0:1, :]
            + cb_ref[...])
    xbc = _silu(conv)
    xs = xbc[:, 0:HALF]
    bm = xbc[:, HALF:HALF + 2 * SSD_STATE]
    cm = xbc[:, HALF + 2 * SSD_STATE:]
    b_ref[...] = bm
    c_ref[...] = cm
    da = dt * (-jnp.exp(aloge_ref[...]))
    acs = da + back(da, 1) + back(da, 2) + back(da, 3)
    suffix = ahead(da, 1) + ahead(da, 2) + ahead(da, 3)
    xdt = xs * dt
    y = _dot((cm * bm).astype(BF16), seg_ref[...]) * xdt
    for k in range(1, DEC_SEQ):
        cbk = _dot((cm * pltpu.roll(bm, k, 0)).astype(BF16), seg_ref[...])
        term = cbk * jnp.exp(acs - pltpu.roll(acs, k, 0)) * pltpu.roll(xdt, k, 0)
        y = y + jnp.where(step_of(HALF) >= k, term, 0.0)
    ysk_ref[...] = y + dskip_ref[...] * xs
    eacs_ref[...] = jnp.exp(acs)
    xw_ref[...] = jnp.exp(suffix) * xdt
    dec_ref[...] = jnp.exp(acs + suffix)


def _ssd_sample_pre(h, conv_state, w):
    t = h.shape[0]
    cd = SSD_CONV_DIM
    tile = min(SSD_PRE_ROWS, t)
    seqs = tile // DEC_SEQ
    state_spec = pl.BlockSpec((3, seqs, cd), lambda i: (0, i, 0))
    head = [w['conv_w'], w['conv_b']]
    tail = [w['dt_bias16'], w['a_log_e'], w['d_skip_e'], w['expand3'], w['seg_expand']]
    args = [h, w['w1'], w['w1'], w['w1'], w['w1'], w['w1']] + head + [conv_state] + tail
    wide = jax.ShapeDtypeStruct((t, HALF), F32)
    narrow = jax.ShapeDtypeStruct((t, 2 * SSD_STATE), F32)
    out_shape = [jax.ShapeDtypeStruct(conv_state.shape, F32), wide, wide, wide, wide, wide, narrow, narrow]
    return pl.pallas_call(
        _ssd_sample_pre_kernel,
        grid=(t // tile,),
        in_specs=[_rows(tile, D_MODEL)] + _ssd_weight_specs()
                 + [_resident(c.shape) for c in head] + [state_spec] + [_resident(c.shape) for c in tail],
        out_specs=[state_spec] + [_rows(tile, HALF)] * 5 + [_rows(tile, 2 * SSD_STATE)] * 2,
        out_shape=out_shape,
        scratch_shapes=[pltpu.VMEM((cd // 128, tile, 128), F32)],
        compiler_params=_params(),
        name="ssd_sample_pre",
    )(*args)


SSD_S_BATCH = 8
SSD_PRE_ROWS = 256


def _ssd_sample_state_kernel(st_ref, c_ref, b_ref, xw_ref, dec_ref, eacs_ref, ysk_ref, z_ref, gn_ref,
                             yd_ref, nst_ref):
    row_n = lax.broadcasted_iota(jnp.int32, (8, SSD_STATE), 0)
    row_w = lax.broadcasted_iota(jnp.int32, (8, SSD_GW), 0)
    row_f = lax.broadcasted_iota(jnp.int32, (8, HALF), 0)
    ones_rows = jnp.where((row_n >= 4) & (row_n < 7), 1.0, 0.0).astype(BF16)
    hpg = SSD_HEADS // SSD_GROUPS

    def pair(p, carry):
        r0 = pl.multiple_of(p * 8, 8)
        rows = pl.ds(r0, 8)
        c8 = c_ref[rows, :].astype(BF16)
        b8 = b_ref[rows, :]
        xw8 = xw_ref[rows, :]
        dec8 = dec_ref[rows, :]
        yoff = []
        for sub in range(2):
            b = 2 * p + sub
            xw_own = xw8 if sub == 0 else pltpu.roll(xw8, 4, 0)
            b_own = b8 if sub == 0 else pltpu.roll(b8, 4, 0)
            hi, mid, lo = (term.astype(F32) for term in _split3(dec8[4 * sub:4 * sub + 1, :]))
            parts = []
            for g in range(SSD_GROUPS):
                lanes = slice(g * SSD_GW, (g + 1) * SSD_GW)
                heads = pl.ds(g * hpg, hpg)
                h0 = st_ref[b, heads].reshape(SSD_GW, SSD_STATE)
                parts.append(_dot_nt(c8[:, g * SSD_STATE:(g + 1) * SSD_STATE], h0.astype(BF16)))
                lhs = jnp.where(row_w < 4, xw_own[:, lanes],
                                jnp.where(row_w == 4, hi[:, lanes],
                                          jnp.where(row_w == 5, mid[:, lanes],
                                                    jnp.where(row_w == 6, lo[:, lanes], 0.0)))).astype(BF16)
                rhs_b = jnp.where(row_n < 4, b_own[:, g * SSD_STATE:(g + 1) * SSD_STATE], 0.0).astype(BF16)
                decay = _dot_tn(lhs, ones_rows)
                nst_ref[b, heads] = (h0 * decay + _dot_tn(lhs, rhs_b)).reshape(hpg, HEAD_DIM, SSD_STATE)
            yoff.append(jnp.concatenate(parts, axis=1))
        yoff8 = jnp.where(row_f < 4, yoff[0], yoff[1])
        y = ysk_ref[rows, :] + yoff8 * eacs_ref[rows, :]
        yd_ref[rows, :] = _group_norm_gate(y, z_ref[rows, :], gn_ref[...])
        return carry

    lax.fori_loop(0, SSD_S_BATCH // 2, pair, 0)


def _ssd_sample_state(state, cm, bm, xw, dec, eacs, ysk, z, gn):
    n_seq = state.shape[0]
    bb = SSD_S_BATCH
    r = bb * DEC_SEQ
    st_spec = pl.BlockSpec((bb, SSD_HEADS, HEAD_DIM, SSD_STATE), lambda i: (i, 0, 0, 0))
    return pl.pallas_call(
        _ssd_sample_state_kernel,
        grid=(n_seq // bb,),
        in_specs=[st_spec, _rows(r, 2 * SSD_STATE), _rows(r, 2 * SSD_STATE)] + [_rows(r, HALF)] * 5
                 + [_resident((1, HALF))],
        out_specs=[_rows(r, HALF), st_spec],
        out_shape=[jax.ShapeDtypeStruct((n_seq * DEC_SEQ, HALF), BF16), jax.ShapeDtypeStruct(state.shape, F32)],
        compiler_params=_params(),
        name="ssd_sample_state",
    )(state, cm, bm, xw, dec, eacs, ysk, z, gn)


def _prep_layer1(g_pre, w_in, ln_g, ln_b, w_s, b_s, conv_w, conv_b, dt_bias, a_log, d_skip, gate_norm_g,
                 w_out, g_post):
    cd = SSD_CONV_DIM
    gw = HALF // CMLP_GROUPS
    w1 = w_in.T

    def lanes16(v):
        return jnp.pad(v.astype(F32), (0, HEAD_LANES - SSD_HEADS)).reshape(1, HEAD_LANES)

    def per_channel(v):
        return jnp.repeat(v.astype(F32), HEAD_DIM).reshape(1, HALF)

    head_of = jnp.arange(HALF) // HEAD_DIM
    expand = (jnp.arange(HEAD_LANES)[:, None] == head_of[None, :]).astype(BF16)
    tril = jnp.tril(jnp.ones((CHUNK, CHUNK), BF16))
    grp_rows = jnp.arange(2 * SSD_STATE) // SSD_STATE
    seg_expand = (grp_rows[:, None] == (head_of // (SSD_HEADS // SSD_GROUPS))[None, :]).astype(BF16)

    w4 = jnp.tril(w_s[:, :DEC_SEQ, :DEC_SEQ])
    steps = jnp.arange(DEC_SEQ)
    coef = []
    for k in range(DEC_SEQ):
        src = steps - k
        ck = jnp.where((src >= 0)[None, :], w4[:, steps, jnp.maximum(src, 0)], 0.0)
        ck = jnp.repeat(ck.T, gw, axis=1)
        coef.append(jnp.concatenate([ck, ck], axis=0))
    bias4 = jnp.repeat(b_s[:, :DEC_SEQ].T, gw, axis=1)
    return dict(
        g_pre=g_pre.reshape(1, D_MODEL), w1=w1, ln_g=ln_g.reshape(1, HALF), ln_b=ln_b.reshape(1, HALF),
        ws_tril=jnp.tril(w_s).astype(BF16),
        bs_rows=jnp.broadcast_to(b_s.astype(F32)[:, :, None], (CMLP_GROUPS, CHUNK, gw)),
        coef=jnp.stack(coef).astype(F32), bias4=jnp.concatenate([bias4, bias4], axis=0).astype(F32),
        conv_w=conv_w, conv_b=conv_b.reshape(1, cd), dt_bias16=lanes16(dt_bias), a_log16=lanes16(a_log),
        a_log_e=per_channel(a_log), d_skip_e=per_channel(d_skip), gate_norm_g=gate_norm_g.reshape(1, HALF),
        expand3=jnp.concatenate([expand] * 3, axis=0), tril3=jnp.concatenate([tril] * 3, axis=1),
        seg_expand=seg_expand, w_out=w_out, g_post=g_post.reshape(1, D_MODEL))


def _layer1_prompt(x, w):
    yc, h = _cmlp_prompt(x, w['g_pre'], w['w1'], w['ln_g'], w['ln_b'], w['ws_tril'], w['bs_rows'])
    yd, tail, ssm = _ssd_prompt(h, w)
    y = _out_proj(yc, yd, x, w['w_out'], w['g_post'])
    return y, tail[5:8], ssm.reshape(SSD_HEADS, HEAD_DIM, SSD_STATE)


def _layer1_sample(x, conv_state, ssm_state, w):
    n_seq = x.shape[0]
    t = n_seq * DEC_SEQ
    rows = x.reshape(t, D_MODEL)
    yc, vn, h = _cmlp_sample(rows, w['g_pre'], w['w1'], w['ln_g'], w['ln_b'], w['coef'], w['bias4'])
    new_conv, z, ysk, eacs, xw, dec, bm, cm = _ssd_sample_pre(h, conv_state.transpose(1, 0, 2), w)
    yd, new_state = _ssd_sample_state(ssm_state, cm, bm, xw, dec, eacs, ysk, z, w['gate_norm_g'])
    y = _out_proj(yc, yd, rows, w['w_out'], w['g_post'])
    return (y.reshape(n_seq, DEC_SEQ, D_MODEL), vn.reshape(n_seq, DEC_SEQ, HALF),
            new_conv.transpose(1, 0, 2), new_state)


def kernel(x_prompt, x_sample, state_conv_a, cache_win_k, cache_win_v, state_conv_d, state_ssm, rel_bias,
           l0_g_pre, l0_w_in, l0_conv_w, l0_sinks, l0_w_out, l0_g_post,
           l1_g_pre, l1_w_in, l1_ln_g, l1_ln_b, l1_w_s, l1_b_s, l1_conv_w, l1_conv_b, l1_dt_bias, l1_a_log,
           l1_d_skip, l1_gate_norm_g, l1_w_out, l1_g_post):
    w0 = _prep_layer0(l0_g_pre, l0_w_in, l0_conv_w, rel_bias, l0_sinks, l0_w_out, l0_g_post)
    w1 = _prep_layer1(l1_g_pre, l1_w_in, l1_ln_g, l1_ln_b, l1_w_s, l1_b_s, l1_conv_w, l1_conv_b, l1_dt_bias,
                      l1_a_log, l1_d_skip, l1_gate_norm_g, l1_w_out, l1_g_post)
    yp, p_conv_a, p_win_k, p_win_v = _layer0_prompt(x_prompt[0], w0)
    ys, s_conv_a, s_win_k, s_win_v = _layer0_sample(x_sample, state_conv_a, cache_win_k, cache_win_v, w0)
    yp, p_conv_d, p_ssm = _layer1_prompt(yp, w1)
    ys, s_chunk_v, s_conv_d, s_ssm = _layer1_sample(ys, state_conv_d, state_ssm, w1)
    return (yp[None], ys, p_conv_a[None], s_conv_a, p_win_k[None], p_win_v[None], s_win_k, s_win_v, s_chunk_v,
            p_conv_d[None], s_conv_d, p_ssm[None], s_ssm)
```

```python
import functools
import math

import jax
import jax.numpy as jnp
from jax import lax
from jax.experimental import pallas as pl
from jax.experimental.pallas import tpu as pltpu

F32 = jnp.float32
BF16 = jnp.bfloat16

D_MODEL = 2048
HALF = 1024
HEAD_DIM = 64
N_HEADS = 16
N_KV = 4
GROUP = 4
WINDOW = 128
NUM_BUCKETS = 32
MAX_DISTANCE = 128
CMLP_GROUPS = 8
CHUNK = 128
SSD_HEADS = 16
SSD_STATE = 128
SSD_GROUPS = 2
SSD_CONV_DIM = HALF + 2 * SSD_GROUPS * SSD_STATE
DEC_SEQ = 4
NORM_EPS = 1e-6
MASK_VALUE = -1e30

ROW_TILE = 512
VMEM_LIMIT = 56 * 1024 * 1024


def _params(n_axes=1):
    return pltpu.CompilerParams(dimension_semantics=("arbitrary",) * n_axes,
                                vmem_limit_bytes=VMEM_LIMIT)


def _resident(shape):
    nd = len(shape)
    return pl.BlockSpec(shape, lambda *_: (0,) * nd, pipeline_mode=pl.Buffered(1))


def _rows(tile, width):
    return pl.BlockSpec((tile, width), lambda i: (i, 0))


def _cols(rows, width, block):
    return pl.BlockSpec((rows, width), lambda *_: (0, block), pipeline_mode=pl.Buffered(1))


def _rowwin(height, cols, block):
    return pl.BlockSpec((height, cols), lambda *_: (block, 0), pipeline_mode=pl.Buffered(1))


def _rms_bf16(x, g):
    ms = jnp.mean(x * x, axis=-1, keepdims=True)
    return (x * lax.rsqrt(ms + NORM_EPS) * g).astype(BF16)


def _silu(x):
    return x * jax.nn.sigmoid(x)


def _dot(a, b):
    return jnp.dot(a, b, preferred_element_type=F32)


def _dot_nt(a, b):
    return lax.dot_general(a, b, (((1,), (1,)), ((), ())), preferred_element_type=F32)


def _dot_tn(a, b):
    return lax.dot_general(a, b, (((0,), (0,)), ((), ())), preferred_element_type=F32)


def _dot_w(a, w):
    return _dot(a, w.astype(BF16))


def _dot_wt(a, wt):
    return _dot_nt(a, wt.astype(BF16))


def _split3(x):
    hi = x.astype(BF16)
    r1 = x - hi.astype(F32)
    mid = r1.astype(BF16)
    lo = (r1 - mid.astype(F32)).astype(BF16)
    return hi, mid, lo


def _place_steps(t, placements):
    n_seq = placements[0][1].shape[0]
    row = lax.broadcasted_iota(jnp.int32, (t, n_seq), 0)
    seq = lax.broadcasted_iota(jnp.int32, (t, n_seq), 1)
    lhs, rhs = [], []
    for step, state in placements:
        sel = jnp.where(row == DEC_SEQ * seq + step, 1.0, 0.0).astype(BF16)
        lhs += [sel] * 3
        rhs += list(_split3(state))
    return _dot(jnp.concatenate(lhs, axis=1), jnp.concatenate(rhs, axis=0))


def _out_proj_kernel(ya_ref, yb_ref, x_ref, w_ref, g_ref, o_ref):
    y = _dot_w(ya_ref[...], w_ref[0:HALF, :]) + _dot_w(yb_ref[...], w_ref[HALF:2 * HALF, :])
    ms = jnp.mean(y * y, axis=-1, keepdims=True)
    o_ref[...] = x_ref[...] + y * lax.rsqrt(ms + NORM_EPS) * g_ref[...]


def _out_proj(ya, yb, x, w_bf, g):
    t = x.shape[0]
    tile = min(ROW_TILE, t)
    return pl.pallas_call(
        _out_proj_kernel,
        grid=(t // tile,),
        in_specs=[_rows(tile, HALF), _rows(tile, HALF), _rows(tile, D_MODEL),
                  _resident((2 * HALF, D_MODEL)), _resident((1, D_MODEL))],
        out_specs=_rows(tile, D_MODEL),
        out_shape=jax.ShapeDtypeStruct((t, D_MODEL), F32),
        compiler_params=_params(),
        name="out_proj",
    )(ya, yb, x, w_bf, g)


CONV_A_CHUNK = 256


def _conv_a_kernel(*refs, sample):
    if sample:
        x_ref, g_ref, w_ref, cw_ref, st_ref, ya_ref, s_ref, h_ref = refs
    else:
        x_ref, g_ref, w_ref, cw_ref, ya_ref, s_ref, h_ref, shift_scr = refs
    tile = x_ref.shape[0]
    cc = CONV_A_CHUNK
    h = _rms_bf16(x_ref[...], g_ref[...])
    h_ref[...] = h
    if not sample:
        @pl.when(pl.program_id(0) == 0)
        def _():
            s_ref[...] = jnp.zeros_like(s_ref)
    for c in range(HALF // cc):
        lanes = slice(c * cc, (c + 1) * cc)
        a_b, a_c, a_h, a_g = (_dot_w(h, w_ref[:, j * HALF + c * cc:j * HALF + (c + 1) * cc]) for j in range(4))
        s = a_c * a_h
        if sample:
            t_in = lax.broadcasted_iota(jnp.int32, s.shape, 0) % DEC_SEQ
            old0 = st_ref[:, c * cc:(c + 1) * cc]
            old1 = st_ref[:, HALF + c * cc:HALF + (c + 1) * cc]
            p1 = jnp.where(t_in >= 1, pltpu.roll(s, 1, 0), 0.0) + _place_steps(tile, [(0, old1)])
            p2 = jnp.where(t_in >= 2, pltpu.roll(s, 2, 0), 0.0) + _place_steps(tile, [(0, old0), (1, old1)])
            s_ref[:, lanes] = s
        else:
            shift_scr[0:8, :] = s_ref[:, lanes]
            shift_scr[8:8 + tile, :] = s
            p1 = shift_scr[7:7 + tile, :]
            p2 = shift_scr[6:6 + tile, :]
            s_ref[:, lanes] = s[tile - 8:tile, :]
        conv = p2 * cw_ref[0:1, lanes] + p1 * cw_ref[1:2, lanes] + s * cw_ref[2:3, lanes]
        ya_ref[:, lanes] = (a_b * conv * _silu(a_g)).astype(BF16)


def _conv_a(x, g_pre, w0, conv_w, state=None):
    t = x.shape[0]
    sample = state is not None
    tile = t if sample else min(ROW_TILE, t)
    in_specs = [_rows(tile, D_MODEL), _resident((1, D_MODEL)), _cols(D_MODEL, 4 * HALF, 0), _resident((3, HALF))]
    args = [x, g_pre, w0, conv_w]
    scratch = []
    if sample:
        in_specs.append(_resident(state.shape))
        args.append(state)
        s_spec, s_shape = _rows(tile, HALF), (t, HALF)
    else:
        s_spec, s_shape = pl.BlockSpec((8, HALF), lambda i: (0, 0)), (8, HALF)
        scratch = [pltpu.VMEM((8 + tile, CONV_A_CHUNK), F32)]
    return pl.pallas_call(
        functools.partial(_conv_a_kernel, sample=sample),
        grid=(t // tile,),
        in_specs=in_specs,
        out_specs=[_rows(tile, HALF), s_spec, _rows(tile, D_MODEL)],
        out_shape=[jax.ShapeDtypeStruct((t, HALF), BF16), jax.ShapeDtypeStruct(s_shape, F32),
                   jax.ShapeDtypeStruct((t, D_MODEL), BF16)],
        scratch_shapes=scratch,
        compiler_params=_params(),
        name="conv_a_sample" if sample else "conv_a_prompt",
    )(*args)


def _rel_bucket(dist):
    max_exact = NUM_BUCKETS // 2
    d = jnp.maximum(dist, 0)
    ratio = jnp.maximum(d, max_exact).astype(F32) / max_exact
    large = max_exact + (jnp.log(ratio) / math.log(MAX_DISTANCE / max_exact)
                         * (NUM_BUCKETS - max_exact)).astype(jnp.int32)
    return jnp.where(d < max_exact, d, jnp.minimum(large, NUM_BUCKETS - 1))


def _attn_softmax_pv(s, sink, v_bf, v_transposed=False):
    m = jnp.maximum(jnp.max(s, axis=-1, keepdims=True), sink)
    p = jnp.exp(s - m)
    den = jnp.sum(p, axis=-1, keepdims=True) + jnp.exp(sink - m)
    pv = _dot_nt(p.astype(BF16), v_bf) if v_transposed else _dot(p.astype(BF16), v_bf)
    return pv / den


def _attn_prompt_kernel(h_ref, wq_ref, wkv_ref, wg0_ref, wg1_ref, tab_ref, sink_ref, yb_ref, kwin_ref,
                        vwin_ref, q_scr, gate_scr, k_scr, v_scr, bias_scr):
    tile = h_ref.shape[0]
    i = pl.program_id(0)
    kv_w = N_KV * HEAD_DIM
    h = h_ref[...]

    kw, vw = 2 * HEAD_DIM, 4 * HEAD_DIM

    @pl.when(i == 0)
    def _():
        k_scr[0:WINDOW, :] = jnp.zeros((WINDOW, N_KV * kw), BF16)
        v_scr[0:WINDOW, :] = jnp.zeros((WINDOW, N_KV * vw), BF16)
        for hk in range(N_KV):
            v_scr[:, hk * vw + kw:(hk + 1) * vw] = jnp.ones((tile + WINDOW, kw), BF16)
        in_own = lax.broadcasted_iota(jnp.int32, (WINDOW, 2 * WINDOW), 1) >= WINDOW
        for head in range(N_HEADS):
            row = jnp.broadcast_to(tab_ref[head:head + 1, :], (WINDOW, BIAS_SPAN))
            band = pltpu.roll(row, 0, 1, stride=1, stride_axis=0)[:, 0:2 * WINDOW]
            rows = slice((head % 2) * WINDOW, (head % 2 + 1) * WINDOW)
            bias_scr[1, head // 2, rows, :] = band
            bias_scr[0, head // 2, rows, :] = jnp.where(in_own, band, MASK_VALUE)

    q_scr[...] = (_dot_w(h, wq_ref[...]) * (HEAD_DIM ** -0.5)).astype(BF16)
    k = _dot_w(h, wkv_ref[:, 0:kv_w])
    v = _dot_w(h, wkv_ref[:, kv_w:2 * kv_w])
    gate_scr[:, 0:HALF // 2] = _silu(_dot_w(h, wg0_ref[...]))
    gate_scr[:, HALF // 2:HALF] = _silu(_dot_w(h, wg1_ref[...]))
    for hk in range(N_KV):
        k_h = k[:, hk * HEAD_DIM:(hk + 1) * HEAD_DIM].astype(BF16)
        v_h = v[:, hk * HEAD_DIM:(hk + 1) * HEAD_DIM].astype(BF16)
        k_scr[WINDOW:WINDOW + tile, hk * kw:(hk + 1) * kw] = jnp.concatenate([k_h, k_h], axis=1)
        v_scr[WINDOW:WINDOW + tile, hk * vw:hk * vw + kw] = jnp.concatenate([v_h, v_h], axis=1)
    kwin_ref[...] = k[tile - WINDOW:tile, :]
    vwin_ref[...] = v[tile - WINDOW:tile, :]

    lane = lax.broadcasted_iota(jnp.int32, (WINDOW, kw), 1)
    lo = lane < HEAD_DIM
    keep_a = jnp.where(lo, 1.0, 0.0).astype(BF16)
    keep_b = jnp.where(lo, 0.0, 1.0).astype(BF16)
    is_a = lax.broadcasted_iota(jnp.int32, (2 * WINDOW, 1), 0) < WINDOW

    def block(n, carry):
        r0 = pl.multiple_of(n * WINDOW, WINDOW)
        rows = pl.ds(r0, WINDOW)
        keys = pl.ds(r0, 2 * WINDOW)
        first = jnp.where(jnp.logical_and(i == 0, n == 0), 0, 1)
        for hk in range(N_KV):
            for gp in range(GROUP // 2):
                a = hk * GROUP + 2 * gp
                slab = slice(a * HEAD_DIM, (a + 2) * HEAD_DIM)
                q2 = q_scr[rows, slab]
                lhs = jnp.concatenate([q2 * keep_a, q2 * keep_b], axis=0)
                s = _dot_nt(lhs, k_scr[keys, hk * kw:(hk + 1) * kw]) + bias_scr[first, a // 2]
                sink = jnp.where(is_a, sink_ref[a], sink_ref[a + 1])
                m = jnp.maximum(jnp.max(s, axis=-1, keepdims=True), sink)
                p = jnp.exp(s - m).astype(BF16)
                pv = _dot(p, v_scr[keys, hk * vw:(hk + 1) * vw])
                num = jnp.where(lo, pv[0:WINDOW, 0:kw], pv[WINDOW:2 * WINDOW, 0:kw])
                den = jnp.where(lo, pv[0:WINDOW, kw:2 * kw], pv[WINDOW:2 * WINDOW, kw:2 * kw])
                m_slab = jnp.where(lo, m[0:WINDOW], m[WINDOW:2 * WINDOW])
                den = den + jnp.exp(jnp.where(lo, sink_ref[a], sink_ref[a + 1]) - m_slab)
                yb_ref[rows, slab] = (num / den * gate_scr[rows, slab]).astype(BF16)
        return carry

    lax.fori_loop(0, tile // WINDOW, block, 0, unroll=True)
    k_scr[0:WINDOW, :] = k_scr[tile:tile + WINDOW, :]
    v_scr[0:WINDOW, :] = v_scr[tile:tile + WINDOW, :]


BIAS_SPAN = 3 * WINDOW


def _prompt_bias_table(rel_bias):
    dist = WINDOW - jnp.arange(BIAS_SPAN)
    table = jnp.where(((dist >= 0) & (dist < WINDOW))[:, None], rel_bias.astype(F32)[_rel_bucket(dist)], MASK_VALUE)
    return table.T


def _attn_prompt(h, w0, table, sinks):
    t = h.shape[0]
    tile = min(ROW_TILE, t)
    kv_w = N_KV * HEAD_DIM
    win_spec = pl.BlockSpec((WINDOW, kv_w), lambda i: (0, 0))
    return pl.pallas_call(
        _attn_prompt_kernel,
        grid=(t // tile,),
        in_specs=[_rows(tile, D_MODEL),
                  _cols(D_MODEL, HALF, 4), _cols(D_MODEL, 2 * kv_w, 10),
                  _cols(D_MODEL, HALF // 2, 11), _cols(D_MODEL, HALF // 2, 12),
                  _resident(table.shape), pl.BlockSpec(memory_space=pltpu.SMEM)],
        out_specs=[_rows(tile, HALF), win_spec, win_spec],
        out_shape=[jax.ShapeDtypeStruct((t, HALF), BF16),
                   jax.ShapeDtypeStruct((WINDOW, kv_w), F32), jax.ShapeDtypeStruct((WINDOW, kv_w), F32)],
        scratch_shapes=[pltpu.VMEM((tile, HALF), BF16), pltpu.VMEM((tile, HALF), F32),
                        pltpu.VMEM((tile + WINDOW, 2 * kv_w), BF16), pltpu.VMEM((tile + WINDOW, 4 * kv_w), BF16),
                        pltpu.VMEM((2, N_HEADS // 2, 2 * WINDOW, 2 * WINDOW), F32)],
        compiler_params=_params(),
        name="attn_prompt",
    )(h, w0, w0, w0, w0, table, sinks)


def _attn_proj_kernel(h_ref, wq_ref, wkv_ref, wg0_ref, wg1_ref, qg_ref, kt_ref, vt_ref, kv_scr):
    kv_w = N_KV * HEAD_DIM
    h = h_ref[...]
    q = _dot_w(h, wq_ref[...]) * (HEAD_DIM ** -0.5)
    for hk in range(N_KV):
        for g in range(GROUP):
            src = (hk * GROUP + g) * HEAD_DIM
            dst = (g * N_KV + hk) * HEAD_DIM
            qg_ref[:, dst:dst + HEAD_DIM] = q[:, src:src + HEAD_DIM]
    qg_ref[:, HALF:HALF + HALF // 2] = _dot_w(h, wg0_ref[...])
    qg_ref[:, HALF + HALF // 2:2 * HALF] = _dot_w(h, wg1_ref[...])
    kv_scr[...] = _dot_w(h, wkv_ref[...])
    for j in range(kt_ref.shape[0]):
        kt_ref[j] = kv_scr[j * WINDOW:(j + 1) * WINDOW, 0:kv_w].T
        vt_ref[j] = kv_scr[j * WINDOW:(j + 1) * WINDOW, kv_w:2 * kv_w].T


def _attn_proj(h, w0):
    t = h.shape[0]
    kv_w = N_KV * HEAD_DIM
    out_shape = [jax.ShapeDtypeStruct((t, 2 * HALF), F32), jax.ShapeDtypeStruct((t // WINDOW, kv_w, WINDOW), F32),
                 jax.ShapeDtypeStruct((t // WINDOW, kv_w, WINDOW), F32)]
    return pl.pallas_call(
        _attn_proj_kernel,
        grid=(1,),
        in_specs=[_resident((t, D_MODEL)),
                  _cols(D_MODEL, HALF, 4), _cols(D_MODEL, 2 * kv_w, 10),
                  _cols(D_MODEL, HALF // 2, 11), _cols(D_MODEL, HALF // 2, 12)],
        out_specs=[_resident(s.shape) for s in out_shape],
        out_shape=out_shape,
        scratch_shapes=[pltpu.VMEM((t, 2 * kv_w), F32)],
        compiler_params=_params(),
        name="attn_proj_sample",
    )(h, w0, w0, w0, w0)


ATTN_S_BATCH = 16
KEYS_PAD = 2 * WINDOW


def _attn_sample_kernel(qg_ref, ktn_ref, vtn_ref, ck_ref, cv_ref, bias_ref, sink_ref, yb_ref, nk_ref, nv_ref):
    kv_w = N_KV * HEAD_DIM
    row8 = lax.broadcasted_iota(jnp.int32, (8, kv_w), 0)
    lane_head = lax.broadcasted_iota(jnp.int32, (8, kv_w), 1) // HEAD_DIM
    lower = row8 < DEC_SEQ
    pick = [jnp.where(lane_head == 2 * hp + jnp.where(lower, 0, 1), 1.0, 0.0).astype(F32) for hp in range(2)]
    lower_w = lax.broadcasted_iota(jnp.int32, (8, HALF), 0) < DEC_SEQ
    kept = lax.broadcasted_iota(jnp.int32, (kv_w, WINDOW), 1) < WINDOW - DEC_SEQ
    seq0 = pl.program_id(0) * ATTN_S_BATCH
    per_tile = WINDOW // DEC_SEQ

    def slide(old, new_tile, shift):
        return jnp.where(kept, pltpu.roll(old, WINDOW - DEC_SEQ, 1), pltpu.roll(new_tile, shift, 1))

    def pair(p, carry):
        r0 = pl.multiple_of(p * 8, 8)
        rows = qg_ref[pl.ds(r0, 8), :]
        q8 = rows[:, 0:HALF]
        gate8 = rows[:, HALF:2 * HALF]
        out8 = []
        for sub in range(2):
            b = 2 * p + sub
            q_swap = pltpu.roll(q8, 4, 0)
            q_dup = jnp.where(lower_w, q8, q_swap) if sub == 0 else jnp.where(lower_w, q_swap, q8)
            tile = (seq0 + b) // per_tile
            shift = (2 * WINDOW - DEC_SEQ - DEC_SEQ * ((seq0 + b) % per_tile)) % WINDOW
            k_old = ck_ref[b].reshape(kv_w, WINDOW)
            v_old = cv_ref[b].reshape(kv_w, WINDOW)
            k_win = slide(k_old, ktn_ref[tile], shift)
            v_win = slide(v_old, vtn_ref[tile], shift)
            nk_ref[b] = k_win.reshape(N_KV, HEAD_DIM, WINDOW)
            nv_ref[b] = v_win.reshape(N_KV, HEAD_DIM, WINDOW)
            k_all = jnp.concatenate([k_old, k_win], axis=1).astype(BF16)
            v_all = jnp.concatenate([v_old, v_win], axis=1).astype(BF16)
            q_bd = jnp.concatenate(
                [q_dup[:, g * kv_w:(g + 1) * kv_w] * pick[hp] for g in range(GROUP) for hp in range(2)], axis=0)
            s = _dot(q_bd.astype(BF16), k_all) + bias_ref[...]
            o = _attn_softmax_pv(s, sink_ref[:, 0:1], v_all, v_transposed=True)
            out_g = []
            for g in range(GROUP):
                acc = None
                for hp in range(2):
                    piece = o[(2 * g + hp) * 8:(2 * g + hp + 1) * 8, :] * pick[hp]
                    piece = piece + pltpu.roll(piece, 4, 0)
                    acc = piece if acc is None else acc + piece
                out_g.append(acc)
            out8.append(jnp.concatenate(
                [out_g[g][:, hk * HEAD_DIM:(hk + 1) * HEAD_DIM] for hk in range(N_KV) for g in range(GROUP)], axis=1))
        o8 = jnp.where(lower_w, out8[0], out8[1])
        yb_ref[pl.ds(r0, 8), :] = (o8 * _silu(gate8)).astype(BF16)
        return carry

    lax.fori_loop(0, ATTN_S_BATCH // 2, pair, 0, unroll=4)


def _sample_bias(rel_bias, sinks):
    t = jnp.arange(DEC_SEQ)[:, None]
    j = jnp.arange(KEYS_PAD)[None, :]
    pos = jnp.where(j < WINDOW, j, j - (KEYS_PAD - DEC_SEQ) + WINDOW)
    dist = t + WINDOW - pos
    valid = (dist >= 0) & (dist < WINDOW) & ((j < WINDOW) | (j >= KEYS_PAD - DEC_SEQ))
    bias = jnp.where(valid[:, :, None], rel_bias.astype(F32)[_rel_bucket(dist)], MASK_VALUE)
    bias = bias.reshape(DEC_SEQ, KEYS_PAD, N_KV, GROUP).transpose(3, 2, 0, 1).reshape(N_HEADS * DEC_SEQ, KEYS_PAD)
    sink = jnp.broadcast_to(sinks.astype(F32).reshape(N_KV, GROUP).T[:, :, None], (GROUP, N_KV, DEC_SEQ))
    return bias, jnp.broadcast_to(sink.reshape(N_HEADS * DEC_SEQ, 1), (N_HEADS * DEC_SEQ, 128))


def _attn_sample(qg, kt_new, vt_new, cache_kt, cache_vt, bias, sink):
    n_seq = cache_kt.shape[0]
    bb = ATTN_S_BATCH
    cache_spec = pl.BlockSpec((bb, N_KV, HEAD_DIM, WINDOW), lambda i: (i, 0, 0, 0))
    return pl.pallas_call(
        _attn_sample_kernel,
        grid=(n_seq // bb,),
        in_specs=[_rows(bb * DEC_SEQ, 2 * HALF), _resident(kt_new.shape), _resident(vt_new.shape),
                  cache_spec, cache_spec, _resident(bias.shape), _resident(sink.shape)],
        out_specs=[_rows(bb * DEC_SEQ, HALF), cache_spec, cache_spec],
        out_shape=[jax.ShapeDtypeStruct((n_seq * DEC_SEQ, HALF), BF16),
                   jax.ShapeDtypeStruct(cache_kt.shape, F32), jax.ShapeDtypeStruct(cache_vt.shape, F32)],
        compiler_params=_params(),
        name="attn_sample",
    )(qg, kt_new, vt_new, cache_kt, cache_vt, bias, sink)


def _prep_layer0(g_pre, w_in, conv_w, rel_bias, sinks, w_out, g_post):
    return dict(
        g_pre=g_pre.reshape(1, D_MODEL), w0=w_in, conv_w=conv_w, rel_bias=rel_bias, sinks=sinks,
        w_out=w_out, g_post=g_post.reshape(1, D_MODEL))


def _layer0_prompt(x, w):
    ya, s_tail, h = _conv_a(x, w['g_pre'], w['w0'], w['conv_w'])
    yb, kwin, vwin = _attn_prompt(h, w['w0'], _prompt_bias_table(w['rel_bias']), w['sinks'])
    y = _out_proj(ya, yb, x, w['w_out'], w['g_post'])
    return (y, s_tail[6:8], kwin.reshape(WINDOW, N_KV, HEAD_DIM), vwin.reshape(WINDOW, N_KV, HEAD_DIM))


def _layer0_sample(x, conv_state, cache_k, cache_v, w):
    n_seq = x.shape[0]
    rows = x.reshape(n_seq * DEC_SEQ, D_MODEL)
    ya, s, h = _conv_a(rows, w['g_pre'], w['w0'], w['conv_w'], conv_state.reshape(n_seq, 2 * HALF))
    qg, kt_new, vt_new = _attn_proj(h, w['w0'])
    bias, sink = _sample_bias(w['rel_bias'], w['sinks'])
    yb, new_kt, new_vt = _attn_sample(qg, kt_new, vt_new, cache_k.transpose(0, 2, 3, 1), cache_v.transpose(0, 2, 3, 1),
                                      bias, sink)
    y = _out_proj(ya, yb, rows, w['w_out'], w['g_post'])
    return (y.reshape(n_seq, DEC_SEQ, D_MODEL), s.reshape(n_seq, DEC_SEQ, HALF)[:, DEC_SEQ - 2:],
            new_kt.transpose(0, 3, 1, 2), new_vt.transpose(0, 3, 1, 2))


def _layer_norm(v, g, b):
    xc = v - jnp.mean(v, axis=-1, keepdims=True)
    return xc * lax.rsqrt(jnp.mean(xc * xc, axis=-1, keepdims=True) + NORM_EPS) * g + b


def _cmlp_prompt_kernel(x_ref, g_ref, w_ref, lng_ref, lnb_ref, ws_ref, bs_ref, yc_ref, h_ref, vn_scr):
    tile = x_ref.shape[0]
    h = _rms_bf16(x_ref[...], g_ref[...])
    h_ref[...] = h
    v = _dot_wt(h, w_ref[HALF:2 * HALF, :])
    vn_scr[...] = _layer_norm(v, lng_ref[...], lnb_ref[...]).astype(BF16)
    gw = HALF // CMLP_GROUPS
    cols = 2 * gw
    for cb in range(HALF // cols):
        u = _dot_wt(h, w_ref[cb * cols:(cb + 1) * cols, :])
        gate = _silu(_dot_wt(h, w_ref[2 * HALF + cb * cols:2 * HALF + (cb + 1) * cols, :]))
        for gi in range(2):
            grp = 2 * cb + gi
            lanes = slice(grp * gw, (grp + 1) * gw)
            for n in range(tile // CHUNK):
                rows = slice(n * CHUNK, (n + 1) * CHUNK)
                mixed = _dot(ws_ref[grp], vn_scr[rows, lanes]) + bs_ref[grp]
                yc_ref[rows, lanes] = (u[rows, gi * gw:(gi + 1) * gw] * mixed
                                       * gate[rows, gi * gw:(gi + 1) * gw]).astype(BF16)


def _cmlp_prompt(x, g_pre, w_c, ln_g, ln_b, ws_tril, bs_rows):
    t = x.shape[0]
    tile = min(ROW_TILE, t)
    return pl.pallas_call(
        _cmlp_prompt_kernel,
        grid=(t // tile,),
        in_specs=[_rows(tile, D_MODEL), _resident((1, D_MODEL)), _rowwin(3 * HALF, D_MODEL, 0),
                  _resident((1, HALF)), _resident((1, HALF)), _resident(ws_tril.shape), _resident(bs_rows.shape)],
        out_specs=[_rows(tile, HALF), _rows(tile, D_MODEL)],
        out_shape=[jax.ShapeDtypeStruct((t, HALF), BF16), jax.ShapeDtypeStruct((t, D_MODEL), BF16)],
        scratch_shapes=[pltpu.VMEM((tile, HALF), BF16)],
        compiler_params=_params(),
        name="cmlp_prompt",
    )(x, g_pre, w_c, ln_g, ln_b, ws_tril, bs_rows)


def _cmlp_sample_kernel(x_ref, g_ref, w_ref, lng_ref, lnb_ref, coef_ref, bias_ref, yc_ref, vn_ref, h_ref):
    t = x_ref.shape[0]
    h = _rms_bf16(x_ref[...], g_ref[...])
    h_ref[...] = h
    u = _dot_wt(h, w_ref[0:HALF, :])
    vn = _layer_norm(_dot_wt(h, w_ref[HALF:2 * HALF, :]), lng_ref[...], lnb_ref[...])
    gate = _silu(_dot_wt(h, w_ref[2 * HALF:3 * HALF, :]))
    vn_ref[...] = vn

    def tiled(a):
        return a.reshape(t // 8, 8, HALF)

    mixed = tiled(vn) * coef_ref[0][None] + bias_ref[...][None]
    for k in range(1, DEC_SEQ):
        mixed = mixed + tiled(pltpu.roll(vn, k, 0)) * coef_ref[k][None]
    yc_ref[...] = (u * mixed.reshape(t, HALF) * gate).astype(BF16)


def _cmlp_sample(x, g_pre, w_c, ln_g, ln_b, coef, bias):
    t = x.shape[0]
    return pl.pallas_call(
        _cmlp_sample_kernel,
        grid=(1,),
        in_specs=[_resident((t, D_MODEL)), _resident((1, D_MODEL)), _rowwin(3 * HALF, D_MODEL, 0),
                  _resident((1, HALF)), _resident((1, HALF)), _resident(coef.shape), _resident(bias.shape)],
        out_specs=[_resident((t, HALF)), _resident((t, HALF)), _resident((t, D_MODEL))],
        out_shape=[jax.ShapeDtypeStruct((t, HALF), BF16), jax.ShapeDtypeStruct((t, HALF), F32),
                   jax.ShapeDtypeStruct((t, D_MODEL), BF16)],
        compiler_params=_params(),
        name="cmlp_sample",
    )(x, g_pre, w_c, ln_g, ln_b, coef, bias)


HEAD_LANES = 128
SSD_GW = HALF // SSD_GROUPS


def _softplus(x):
    return jnp.maximum(x, 0.0) + jnp.log1p(jnp.exp(-jnp.abs(x)))


def _dt_proj(h, wdt_ref):
    pad = jnp.zeros((HEAD_LANES - SSD_HEADS, D_MODEL), F32)
    return _dot_wt(h, jnp.concatenate([wdt_ref[...], pad], axis=0))


def _group_norm_gate(y, z, gn):
    gated = y * _silu(z)
    parts = []
    for g in range(SSD_GROUPS):
        part = gated[:, g * SSD_GW:(g + 1) * SSD_GW]
        parts.append(part * lax.rsqrt(jnp.mean(part * part, axis=-1, keepdims=True) + NORM_EPS))
    return (jnp.concatenate(parts, axis=1) * gn).astype(BF16)


def _ssd_prompt_kernel(h_ref, wz_ref, wx0_ref, wx1_ref, wx2_ref, wdt_ref, cw_ref, cb_ref, dtb_ref, alog_ref,
                       dskip_ref, gn_ref, e3_ref, tril3_ref, yd_ref, tail_ref, ssm_ref,
                       xbc_scr, z_scr, dt_scr, ht_scr, shift_scr):
    tile = h_ref.shape[0]
    i = pl.program_id(0)
    cd = SSD_CONV_DIM
    h = h_ref[...]

    @pl.when(i == 0)
    def _():
        tail_ref[...] = jnp.zeros_like(tail_ref)
        ht_scr[...] = jnp.zeros_like(ht_scr)

    z_scr[...] = _dot_wt(h, wz_ref[...])
    dt_scr[...] = _softplus(_dt_proj(h, wdt_ref) + dtb_ref[...])
    third = cd // 3
    for j, wx_ref in enumerate((wx0_ref, wx1_ref, wx2_ref)):
        cols = slice(j * third, (j + 1) * third)
        raw = _dot_wt(h, wx_ref[...])
        shift_scr[0:8, :] = tail_ref[:, cols]
        shift_scr[8:8 + tile, :] = raw
        conv = raw * cw_ref[3:4, cols] + cb_ref[:, cols]
        for k in range(1, 4):
            conv = conv + shift_scr[8 - k:8 - k + tile, :] * cw_ref[3 - k:4 - k, cols]
        xbc_scr[:, cols] = _silu(conv)
        tail_ref[:, cols] = raw[tile - 8:tile, :]

    a16 = -jnp.exp(alog_ref[...])
    causal =(lax.broadcasted_iota(jnp.int32, (CHUNK, CHUNK), 0)
              >= lax.broadcasted_iota(jnp.int32, (CHUNK, CHUNK), 1))
    first_half = lax.broadcasted_iota(jnp.int32, (CHUNK, 2 * HEAD_DIM), 1) < HEAD_DIM
    keep_a = jnp.where(first_half, 1.0, 0.0).astype(BF16)
    keep_b = jnp.where(first_half, 0.0, 1.0).astype(BF16)

    def chunk(n, carry):
        r0 = pl.multiple_of(n * CHUNK, CHUNK)
        rows = pl.ds(r0, CHUNK)
        xs = xbc_scr[rows, 0:HALF]
        dt16 = dt_scr[rows, :]
        dt_e = _dot(jnp.concatenate(_split3(dt16), axis=1), e3_ref[...])
        acs16 = _dot(tril3_ref[...], jnp.concatenate(_split3(dt16 * a16), axis=0))
        acs_e = _dot(jnp.concatenate(_split3(acs16), axis=1), e3_ref[...])
        acs_t = acs16.T
        last_e = acs_e[CHUNK - 1:CHUNK, :]
        xdt = xs * dt_e
        xdt_bf = xdt.astype(BF16)
        xw = (jnp.exp(last_e - acs_e) * xdt).astype(BF16)
        dec_e = jnp.exp(last_e)
        y_parts = []
        yoff_parts = []
        for g in range(SSD_GROUPS):
            c_g = xbc_scr[rows, HALF + 2 * SSD_STATE + g * SSD_STATE:HALF + 2 * SSD_STATE + (g + 1) * SSD_STATE].astype(BF16)
            b_g = xbc_scr[rows, HALF + g * SSD_STATE:HALF + (g + 1) * SSD_STATE].astype(BF16)
            cb = _dot_nt(c_g, b_g)
            h_prev = ht_scr[g]
            yoff_parts.append(_dot(c_g, h_prev.astype(BF16)))
            for r in range(0, SSD_HEADS // SSD_GROUPS, 2):
                wgt = []
                for hd in (g * (SSD_HEADS // SSD_GROUPS) + r, g * (SSD_HEADS // SSD_GROUPS) + r + 1):
                    seg = acs16[:, hd:hd + 1] - acs_t[hd:hd + 1, :]
                    wgt.append(cb * jnp.exp(jnp.where(causal, seg, -jnp.inf)))
                a = g * (SSD_HEADS // SSD_GROUPS) + r
                slab = xdt_bf[:, a * HEAD_DIM:(a + 2) * HEAD_DIM]
                rhs = jnp.concatenate([slab * keep_a, slab * keep_b], axis=0)
                y_parts.append(_dot(jnp.concatenate(wgt, axis=1).astype(BF16), rhs))
            lanes = slice(g * SSD_GW, (g + 1) * SSD_GW)
            ht_scr[g] = h_prev * dec_e[:, lanes] + _dot_tn(b_g, xw[:, lanes])
        y = (jnp.concatenate(y_parts, axis=1) + jnp.concatenate(yoff_parts, axis=1) * jnp.exp(acs_e)
             + dskip_ref[...] * xs)
        yd_ref[rows, :] = _group_norm_gate(y, z_scr[rows, :], gn_ref[...])
        return carry

    lax.fori_loop(0, tile // CHUNK, chunk, 0, unroll=True)

    @pl.when(i == pl.num_programs(0) - 1)
    def _():
        for g in range(SSD_GROUPS):
            ssm_ref[g * SSD_GW:(g + 1) * SSD_GW, :] = ht_scr[g].T


def _ssd_weight_specs():
    third = SSD_CONV_DIM // 3
    first = 4 * HALF // third
    return ([_rowwin(HALF, D_MODEL, 3)] + [_rowwin(third, D_MODEL, first + j) for j in range(3)]
            + [_rowwin(SSD_HEADS, D_MODEL, (4 * HALF + SSD_CONV_DIM) // SSD_HEADS)])


def _ssd_prompt(h, w):
    t = h.shape[0]
    tile = min(ROW_TILE, t)
    cd = SSD_CONV_DIM
    consts = [w['conv_w'], w['conv_b'], w['dt_bias16'], w['a_log16'],
              w['d_skip_e'], w['gate_norm_g'], w['expand3'], w['tril3']]
    return pl.pallas_call(
        _ssd_prompt_kernel,
        grid=(t // tile,),
        in_specs=[_rows(tile, D_MODEL)] + _ssd_weight_specs() + [_resident(c.shape) for c in consts],
        out_specs=[_rows(tile, HALF), pl.BlockSpec((8, cd), lambda i: (0, 0)),
                   pl.BlockSpec((HALF, SSD_STATE), lambda i: (0, 0))],
        out_shape=[jax.ShapeDtypeStruct((t, HALF), BF16), jax.ShapeDtypeStruct((8, cd), F32),
                   jax.ShapeDtypeStruct((HALF, SSD_STATE), F32)],
        scratch_shapes=[pltpu.VMEM((tile, cd), F32), pltpu.VMEM((tile, HALF), F32),
                        pltpu.VMEM((tile, HEAD_LANES), F32), pltpu.VMEM((SSD_GROUPS, SSD_STATE, SSD_GW), F32),
                        pltpu.VMEM((8 + tile, cd // 3), F32)],
        compiler_params=_params(),
        name="ssd_prompt",
    )(h, w['w1'], w['w1'], w['w1'], w['w1'], w['w1'], *consts)


def _ssd_sample_pre_kernel(h_ref, wz_ref, wx0_ref, wx1_ref, wx2_ref, wdt_ref, cw_ref, cb_ref, st_ref,
                           dtb_ref, aloge_ref, dskip_ref, e3_ref, seg_ref,
                           nconv_ref, z_ref, ysk_ref, eacs_ref, xw_ref, dec_ref, b_ref, c_ref, raw_scr):
    t = h_ref.shape[0]
    n_seq = t // DEC_SEQ
    h = h_ref[...]
    z_ref[...] = _dot_wt(h, wz_ref[...])
    raw = jnp.concatenate([_dot_wt(h, wx0_ref[...]), _dot_wt(h, wx1_ref[...]), _dot_wt(h, wx2_ref[...])], axis=1)
    for c in range(raw_scr.shape[0]):
        lanes = slice(c * 128, (c + 1) * 128)
        raw_scr[c] = raw[:, lanes]
        for j in range(3):
            nconv_ref[j, :, lanes] = raw_scr[c, pl.ds(j + 1, n_seq, stride=DEC_SEQ), :]
    dt16 = _softplus(_dt_proj(h, wdt_ref) + dtb_ref[...])
    dt = _dot(jnp.concatenate(_split3(dt16), axis=1), e3_ref[...])
    old = [st_ref[j] for j in range(3)]
    p1 = _place_steps(t, [(0, old[2])])
    p2 = _place_steps(t, [(0, old[1]), (1, old[2])])
    p3 = _place_steps(t, [(0, old[0]), (1, old[1]), (2, old[2])])

    def step_of(width):
        return lax.broadcasted_iota(jnp.int32, (t, width), 0) % DEC_SEQ

    def back(a, k):
        return jnp.where(step_of(a.shape[1]) >= k, pltpu.roll(a, k, 0), 0.0)

    def ahead(a, k):
        return jnp.where(step_of(a.shape[1]) + k < DEC_SEQ, pltpu.roll(a, t - k, 0), 0.0)

    conv = (raw * cw_ref[3:4, :] + (back(raw, 1) + p1) * cw_ref[2:3, :]
            + (back(raw, 2) + p2) * cw_ref[1:2, :] + (back(raw, 3) + p3) * cw_ref[0:1, :]
            + cb_ref[...])
    xbc = _silu(conv)
    xs = xbc[:, 0:HALF]
    bm = xbc[:, HALF:HALF + 2 * SSD_STATE]
    cm = xbc[:, HALF + 2 * SSD_STATE:]
    b_ref[...] = bm
    c_ref[...] = cm
    da = dt * (-jnp.exp(aloge_ref[...]))
    acs = da + back(da, 1) + back(da, 2) + back(da, 3)
    suffix = ahead(da, 1) + ahead(da, 2) + ahead(da, 3)
    xdt = xs * dt
    y = _dot((cm * bm).astype(BF16), seg_ref[...]) * xdt
    for k in range(1, DEC_SEQ):
        cbk = _dot((cm * pltpu.roll(bm, k, 0)).astype(BF16), seg_ref[...])
        term = cbk * jnp.exp(acs - pltpu.roll(acs, k, 0)) * pltpu.roll(xdt, k, 0)
        y = y + jnp.where(step_of(HALF) >= k, term, 0.0)
    ysk_ref[...] = y + dskip_ref[...] * xs
    eacs_ref[...] = jnp.exp(acs)
    xw_ref[...] = jnp.exp(suffix) * xdt
    dec_ref[...] = jnp.exp(acs + suffix)


def _ssd_sample_pre(h, conv_state, w):
    t = h.shape[0]
    cd = SSD_CONV_DIM
    tile = min(SSD_PRE_ROWS, t)
    seqs = tile // DEC_SEQ
    state_spec = pl.BlockSpec((3, seqs, cd), lambda i: (0, i, 0))
    head = [w['conv_w'], w['conv_b']]
    tail = [w['dt_bias16'], w['a_log_e'], w['d_skip_e'], w['expand3'], w['seg_expand']]
    args = [h, w['w1'], w['w1'], w['w1'], w['w1'], w['w1']] + head + [conv_state] + tail
    wide = jax.ShapeDtypeStruct((t, HALF), F32)
    narrow = jax.ShapeDtypeStruct((t, 2 * SSD_STATE), F32)
    out_shape = [jax.ShapeDtypeStruct(conv_state.shape, F32), wide, wide, wide, wide, wide, narrow, narrow]
    return pl.pallas_call(
        _ssd_sample_pre_kernel,
        grid=(t // tile,),
        in_specs=[_rows(tile, D_MODEL)] + _ssd_weight_specs()
                 + [_resident(c.shape) for c in head] + [state_spec] + [_resident(c.shape) for c in tail],
        out_specs=[state_spec] + [_rows(tile, HALF)] * 5 + [_rows(tile, 2 * SSD_STATE)] * 2,
        out_shape=out_shape,
        scratch_shapes=[pltpu.VMEM((cd // 128, tile, 128), F32)],
        compiler_params=_params(),
        name="ssd_sample_pre",
    )(*args)


SSD_S_BATCH = 8
SSD_PRE_ROWS = 256


def _ssd_sample_state_kernel(st_ref, c_ref, b_ref, xw_ref, dec_ref, eacs_ref, ysk_ref, z_ref, gn_ref,
                             yd_ref, nst_ref):
    row_n = lax.broadcasted_iota(jnp.int32, (8, SSD_STATE), 0)
    row_w = lax.broadcasted_iota(jnp.int32, (8, SSD_GW), 0)
    row_f = lax.broadcasted_iota(jnp.int32, (8, HALF), 0)
    ones_rows = jnp.where((row_n >= 4) & (row_n < 7), 1.0, 0.0).astype(BF16)
    hpg = SSD_HEADS // SSD_GROUPS

    def pair(p, carry):
        r0 = pl.multiple_of(p * 8, 8)
        rows = pl.ds(r0, 8)
        c8 = c_ref[rows, :].astype(BF16)
        b8 = b_ref[rows, :]
        xw8 = xw_ref[rows, :]
        dec8 = dec_ref[rows, :]
        yoff = []
        for sub in range(2):
            b = 2 * p + sub
            xw_own = xw8 if sub == 0 else pltpu.roll(xw8, 4, 0)
            b_own = b8 if sub == 0 else pltpu.roll(b8, 4, 0)
            hi, mid, lo = (term.astype(F32) for term in _split3(dec8[4 * sub:4 * sub + 1, :]))
            parts = []
            for g in range(SSD_GROUPS):
                lanes = slice(g * SSD_GW, (g + 1) * SSD_GW)
                heads = pl.ds(g * hpg, hpg)
                h0 = st_ref[b, heads].reshape(SSD_GW, SSD_STATE)
                parts.append(_dot_nt(c8[:, g * SSD_STATE:(g + 1) * SSD_STATE], h0.astype(BF16)))
                lhs = jnp.where(row_w < 4, xw_own[:, lanes],
                                jnp.where(row_w == 4, hi[:, lanes],
                                          jnp.where(row_w == 5, mid[:, lanes],
                                                    jnp.where(row_w == 6, lo[:, lanes], 0.0)))).astype(BF16)
                rhs_b = jnp.where(row_n < 4, b_own[:, g * SSD_STATE:(g + 1) * SSD_STATE], 0.0).astype(BF16)
                decay = _dot_tn(lhs, ones_rows)
                nst_ref[b, heads] = (h0 * decay + _dot_tn(lhs, rhs_b)).reshape(hpg, HEAD_DIM, SSD_STATE)
            yoff.append(jnp.concatenate(parts, axis=1))
        yoff8 = jnp.where(row_f < 4, yoff[0], yoff[1])
        y = ysk_ref[rows, :] + yoff8 * eacs_ref[rows, :]
        yd_ref[rows, :] = _group_norm_gate(y, z_ref[rows, :], gn_ref[...])
        return carry

    lax.fori_loop(0, SSD_S_BATCH // 2, pair, 0, unroll=2)


def _ssd_sample_state(state, cm, bm, xw, dec, eacs, ysk, z, gn):
    n_seq = state.shape[0]
    bb = SSD_S_BATCH
    r = bb * DEC_SEQ
    st_spec = pl.BlockSpec((bb, SSD_HEADS, HEAD_DIM, SSD_STATE), lambda i: (i, 0, 0, 0))
    return pl.pallas_call(
        _ssd_sample_state_kernel,
        grid=(n_seq // bb,),
        in_specs=[st_spec, _rows(r, 2 * SSD_STATE), _rows(r, 2 * SSD_STATE)] + [_rows(r, HALF)] * 5
                 + [_resident((1, HALF))],
        out_specs=[_rows(r, HALF), st_spec],
        out_shape=[jax.ShapeDtypeStruct((n_seq * DEC_SEQ, HALF), BF16), jax.ShapeDtypeStruct(state.shape, F32)],
        compiler_params=_params(),
        name="ssd_sample_state",
    )(state, cm, bm, xw, dec, eacs, ysk, z, gn)


def _prep_layer1(g_pre, w_in, ln_g, ln_b, w_s, b_s, conv_w, conv_b, dt_bias, a_log, d_skip, gate_norm_g,
                 w_out, g_post):
    cd = SSD_CONV_DIM
    gw = HALF // CMLP_GROUPS
    w1 = w_in.T

    def lanes16(v):
        return jnp.pad(v.astype(F32), (0, HEAD_LANES - SSD_HEADS)).reshape(1, HEAD_LANES)

    def per_channel(v):
        return jnp.repeat(v.astype(F32), HEAD_DIM).reshape(1, HALF)

    head_of = jnp.arange(HALF) // HEAD_DIM
    expand = (jnp.arange(HEAD_LANES)[:, None] == head_of[None, :]).astype(BF16)
    tril = jnp.tril(jnp.ones((CHUNK, CHUNK), BF16))
    grp_rows = jnp.arange(2 * SSD_STATE) // SSD_STATE
    seg_expand = (grp_rows[:, None] == (head_of // (SSD_HEADS // SSD_GROUPS))[None, :]).astype(BF16)

    w4 = jnp.tril(w_s[:, :DEC_SEQ, :DEC_SEQ])
    steps = jnp.arange(DEC_SEQ)
    coef = []
    for k in range(DEC_SEQ):
        src = steps - k
        ck = jnp.where((src >= 0)[None, :], w4[:, steps, jnp.maximum(src, 0)], 0.0)
        ck = jnp.repeat(ck.T, gw, axis=1)
        coef.append(jnp.concatenate([ck, ck], axis=0))
    bias4 = jnp.repeat(b_s[:, :DEC_SEQ].T, gw, axis=1)
    return dict(
        g_pre=g_pre.reshape(1, D_MODEL), w1=w1, ln_g=ln_g.reshape(1, HALF), ln_b=ln_b.reshape(1, HALF),
        ws_tril=jnp.tril(w_s).astype(BF16),
        bs_rows=jnp.broadcast_to(b_s.astype(F32)[:, :, None], (CMLP_GROUPS, CHUNK, gw)),
        coef=jnp.stack(coef).astype(F32), bias4=jnp.concatenate([bias4, bias4], axis=0).astype(F32),
        conv_w=conv_w, conv_b=conv_b.reshape(1, cd), dt_bias16=lanes16(dt_bias), a_log16=lanes16(a_log),
        a_log_e=per_channel(a_log), d_skip_e=per_channel(d_skip), gate_norm_g=gate_norm_g.reshape(1, HALF),
        expand3=jnp.concatenate([expand] * 3, axis=0), tril3=jnp.concatenate([tril] * 3, axis=1),
        seg_expand=seg_expand, w_out=w_out, g_post=g_post.reshape(1, D_MODEL))


def _layer1_prompt(x, w):
    yc, h = _cmlp_prompt(x, w['g_pre'], w['w1'], w['ln_g'], w['ln_b'], w['ws_tril'], w['bs_rows'])
    yd, tail, ssm = _ssd_prompt(h, w)
    y = _out_proj(yc, yd, x, w['w_out'], w['g_post'])
    return y, tail[5:8], ssm.reshape(SSD_HEADS, HEAD_DIM, SSD_STATE)


def _layer1_sample(x, conv_state, ssm_state, w):
    n_seq = x.shape[0]
    t = n_seq * DEC_SEQ
    rows = x.reshape(t, D_MODEL)
    yc, vn, h = _cmlp_sample(rows, w['g_pre'], w['w1'], w['ln_g'], w['ln_b'], w['coef'], w['bias4'])
    new_conv, z, ysk, eacs, xw, dec, bm, cm = _ssd_sample_pre(h, conv_state.transpose(1, 0, 2), w)
    yd, new_state = _ssd_sample_state(ssm_state, cm, bm, xw, dec, eacs, ysk, z, w['gate_norm_g'])
    y = _out_proj(yc, yd, rows, w['w_out'], w['g_post'])
    return (y.reshape(n_seq, DEC_SEQ, D_MODEL), vn.reshape(n_seq, DEC_SEQ, HALF),
            new_conv.transpose(1, 0, 2), new_state)


def kernel(x_prompt, x_sample, state_conv_a, cache_win_k, cache_win_v, state_conv_d, state_ssm, rel_bias,
           l0_g_pre, l0_w_in, l0_conv_w, l0_sinks, l0_w_out, l0_g_post,
           l1_g_pre, l1_w_in, l1_ln_g, l1_ln_b, l1_w_s, l1_b_s, l1_conv_w, l1_conv_b, l1_dt_bias, l1_a_log,
           l1_d_skip, l1_gate_norm_g, l1_w_out, l1_g_post):
    w0 = _prep_layer0(l0_g_pre, l0_w_in, l0_conv_w, rel_bias, l0_sinks, l0_w_out, l0_g_post)
    w1 = _prep_layer1(l1_g_pre, l1_w_in, l1_ln_g, l1_ln_b, l1_w_s, l1_b_s, l1_conv_w, l1_conv_b, l1_dt_bias,
                      l1_a_log, l1_d_skip, l1_gate_norm_g, l1_w_out, l1_g_post)
    yp, p_conv_a, p_win_k, p_win_v = _layer0_prompt(x_prompt[0], w0)
    ys, s_conv_a, s_win_k, s_win_v = _layer0_sample(x_sample, state_conv_a, cache_win_k, cache_win_v, w0)
    yp, p_conv_d, p_ssm = _layer1_prompt(yp, w1)
    ys, s_chunk_v, s_conv_d, s_ssm = _layer1_sample(ys, state_conv_d, state_ssm, w1)
    return (yp[None], ys, p_conv_a[None], s_conv_a, p_win_k[None], p_win_v[None], s_win_k, s_win_v, s_chunk_v,
            p_conv_d[None], s_conv_d, p_ssm[None], s_ssm)
```

```python
import functools
import math

import jax
import jax.numpy as jnp
import numpy as np
from jax import lax
from jax.experimental import pallas as pl
from jax.experimental.pallas import tpu as pltpu

F32 = jnp.float32
BF16 = jnp.bfloat16

D_MODEL = 2048
HALF = 1024
HEAD_DIM = 64
N_HEADS = 16
N_KV = 4
GROUP = 4
WINDOW = 128
NUM_BUCKETS = 32
MAX_DISTANCE = 128
CMLP_GROUPS = 8
CHUNK = 128
SSD_HEADS = 16
SSD_STATE = 128
SSD_GROUPS = 2
SSD_CONV_DIM = HALF + 2 * SSD_GROUPS * SSD_STATE
DEC_SEQ = 4
NORM_EPS = 1e-6
MASK_VALUE = -1e30

ROW_TILE = 512
VMEM_LIMIT = 56 * 1024 * 1024


def _params(n_axes=1):
    return pltpu.CompilerParams(dimension_semantics=("arbitrary",) * n_axes,
                                vmem_limit_bytes=VMEM_LIMIT)


def _resident(shape):
    nd = len(shape)
    return pl.BlockSpec(shape, lambda *_: (0,) * nd, pipeline_mode=pl.Buffered(1))


def _rows(tile, width):
    return pl.BlockSpec((tile, width), lambda i: (i, 0))


def _cols(rows, width, block):
    return pl.BlockSpec((rows, width), lambda *_: (0, block), pipeline_mode=pl.Buffered(1))


def _rowwin(height, cols, block):
    return pl.BlockSpec((height, cols), lambda *_: (block, 0), pipeline_mode=pl.Buffered(1))


def _rms_bf16(x, g):
    ms = jnp.mean(x * x, axis=-1, keepdims=True)
    return (x * lax.rsqrt(ms + NORM_EPS) * g).astype(BF16)


def _silu(x):
    return x * jax.nn.sigmoid(x)


def _dot(a, b):
    return jnp.dot(a, b, preferred_element_type=F32)


def _dot_nt(a, b):
    return lax.dot_general(a, b, (((1,), (1,)), ((), ())), preferred_element_type=F32)


def _dot_tn(a, b):
    return lax.dot_general(a, b, (((0,), (0,)), ((), ())), preferred_element_type=F32)


def _dot_w(a, w):
    return _dot(a, w.astype(BF16))


def _dot_wt(a, wt):
    return _dot_nt(a, wt.astype(BF16))


W_PIECE = 256


def _load_transposed(w_hbm, first_row, n_rows, dst_scr, stage_scr, sem):
    n = n_rows // W_PIECE

    def piece(p):
        return pltpu.make_async_copy(w_hbm.at[pl.ds(first_row + p * W_PIECE, W_PIECE), :],
                                     stage_scr.at[p % 2], sem.at[p % 2])

    piece(0).start()
    for p in range(n):
        if p + 1 < n:
            piece(p + 1).start()
        piece(p).wait()
        dst_scr[:, p * W_PIECE:(p + 1) * W_PIECE] = stage_scr[p % 2].T.astype(BF16)


def _split3(x):
    hi = x.astype(BF16)
    r1 = x - hi.astype(F32)
    mid = r1.astype(BF16)
    lo = (r1 - mid.astype(F32)).astype(BF16)
    return hi, mid, lo


def _place_steps(t, placements):
    n_seq = placements[0][1].shape[0]
    row = lax.broadcasted_iota(jnp.int32, (t, n_seq), 0)
    seq = lax.broadcasted_iota(jnp.int32, (t, n_seq), 1)
    lhs, rhs = [], []
    for step, state in placements:
        sel = jnp.where(row == DEC_SEQ * seq + step, 1.0, 0.0).astype(BF16)
        lhs += [sel] * 3
        rhs += list(_split3(state))
    return _dot(jnp.concatenate(lhs, axis=1), jnp.concatenate(rhs, axis=0))


def _out_proj_kernel(ya_ref, yb_ref, x_ref, w_ref, g_ref, o_ref):
    y = _dot_w(ya_ref[...], w_ref[0:HALF, :]) + _dot_w(yb_ref[...], w_ref[HALF:2 * HALF, :])
    ms = jnp.mean(y * y, axis=-1, keepdims=True)
    o_ref[...] = x_ref[...] + y * lax.rsqrt(ms + NORM_EPS) * g_ref[...]


def _out_proj(ya, yb, x, w_bf, g):
    t = x.shape[0]
    tile = min(ROW_TILE, t)
    return pl.pallas_call(
        _out_proj_kernel,
        grid=(t // tile,),
        in_specs=[_rows(tile, HALF), _rows(tile, HALF), _rows(tile, D_MODEL),
                  _resident((2 * HALF, D_MODEL)), _resident((1, D_MODEL))],
        out_specs=_rows(tile, D_MODEL),
        out_shape=jax.ShapeDtypeStruct((t, D_MODEL), F32),
        compiler_params=_params(),
        name="out_proj",
    )(ya, yb, x, w_bf, g)


CONV_A_CHUNK = 256


def _conv_a_kernel(*refs, sample):
    if sample:
        x_ref, g_ref, w_ref, cw_ref, st_ref, ya_ref, s_ref, h_ref = refs
    else:
        x_ref, g_ref, w_ref, cw_ref, ya_ref, s_ref, h_ref, shift_scr = refs
    tile = x_ref.shape[0]
    cc = CONV_A_CHUNK
    h = _rms_bf16(x_ref[...], g_ref[...])
    h_ref[...] = h
    if not sample:
        @pl.when(pl.program_id(0) == 0)
        def _():
            s_ref[...] = jnp.zeros_like(s_ref)
    for c in range(HALF // cc):
        lanes = slice(c * cc, (c + 1) * cc)
        a_b, a_c, a_h, a_g = (_dot_w(h, w_ref[:, j * HALF + c * cc:j * HALF + (c + 1) * cc]) for j in range(4))
        s = a_c * a_h
        if sample:
            t_in = lax.broadcasted_iota(jnp.int32, s.shape, 0) % DEC_SEQ
            old0 = st_ref[:, c * cc:(c + 1) * cc]
            old1 = st_ref[:, HALF + c * cc:HALF + (c + 1) * cc]
            p1 = jnp.where(t_in >= 1, pltpu.roll(s, 1, 0), 0.0) + _place_steps(tile, [(0, old1)])
            p2 = jnp.where(t_in >= 2, pltpu.roll(s, 2, 0), 0.0) + _place_steps(tile, [(0, old0), (1, old1)])
            s_ref[:, lanes] = s
        else:
            shift_scr[0:8, :] = s_ref[:, lanes]
            shift_scr[8:8 + tile, :] = s
            p1 = shift_scr[7:7 + tile, :]
            p2 = shift_scr[6:6 + tile, :]
            s_ref[:, lanes] = s[tile - 8:tile, :]
        conv = p2 * cw_ref[0:1, lanes] + p1 * cw_ref[1:2, lanes] + s * cw_ref[2:3, lanes]
        ya_ref[:, lanes] = (a_b * conv * _silu(a_g)).astype(BF16)


def _conv_a(x, g_pre, w0, conv_w, state=None):
    t = x.shape[0]
    sample = state is not None
    tile = t if sample else min(ROW_TILE, t)
    in_specs = [_rows(tile, D_MODEL), _resident((1, D_MODEL)), _cols(D_MODEL, 4 * HALF, 0), _resident((3, HALF))]
    args = [x, g_pre, w0, conv_w]
    scratch = []
    if sample:
        in_specs.append(_resident(state.shape))
        args.append(state)
        s_spec, s_shape = _rows(tile, HALF), (t, HALF)
    else:
        s_spec, s_shape = pl.BlockSpec((8, HALF), lambda i: (0, 0)), (8, HALF)
        scratch = [pltpu.VMEM((8 + tile, CONV_A_CHUNK), F32)]
    return pl.pallas_call(
        functools.partial(_conv_a_kernel, sample=sample),
        grid=(t // tile,),
        in_specs=in_specs,
        out_specs=[_rows(tile, HALF), s_spec, _rows(tile, D_MODEL)],
        out_shape=[jax.ShapeDtypeStruct((t, HALF), BF16), jax.ShapeDtypeStruct(s_shape, F32),
                   jax.ShapeDtypeStruct((t, D_MODEL), BF16)],
        scratch_shapes=scratch,
        compiler_params=_params(),
        name="conv_a_sample" if sample else "conv_a_prompt",
    )(*args)


def _rel_bucket(dist):
    max_exact = NUM_BUCKETS // 2
    d = jnp.maximum(dist, 0)
    ratio = jnp.maximum(d, max_exact).astype(F32) / max_exact
    large = max_exact + (jnp.log(ratio) / math.log(MAX_DISTANCE / max_exact)
                         * (NUM_BUCKETS - max_exact)).astype(jnp.int32)
    return jnp.where(d < max_exact, d, jnp.minimum(large, NUM_BUCKETS - 1))


def _attn_softmax_pv(s, sink, v_bf, v_transposed=False):
    m = jnp.maximum(jnp.max(s, axis=-1, keepdims=True), sink)
    p = jnp.exp(s - m)
    den = jnp.sum(p, axis=-1, keepdims=True) + jnp.exp(sink - m)
    pv = _dot_nt(p.astype(BF16), v_bf) if v_transposed else _dot(p.astype(BF16), v_bf)
    return pv / den


def _attn_prompt_kernel(h_ref, wq_ref, wkv_ref, wg0_ref, wg1_ref, tab_ref, sink_ref, yb_ref, kwin_ref,
                        vwin_ref, q_scr, gate_scr, k_scr, v_scr, bias_scr):
    tile = h_ref.shape[0]
    i = pl.program_id(0)
    kv_w = N_KV * HEAD_DIM
    h = h_ref[...]

    kw, vw = 2 * HEAD_DIM, 4 * HEAD_DIM

    @pl.when(i == 0)
    def _():
        k_scr[0:WINDOW, :] = jnp.zeros((WINDOW, N_KV * kw), BF16)
        v_scr[0:WINDOW, :] = jnp.zeros((WINDOW, N_KV * vw), BF16)
        for hk in range(N_KV):
            v_scr[:, hk * vw + kw:(hk + 1) * vw] = jnp.ones((tile + WINDOW, kw), BF16)
        in_own = lax.broadcasted_iota(jnp.int32, (WINDOW, 2 * WINDOW), 1) >= WINDOW
        for head in range(N_HEADS):
            row = jnp.broadcast_to(tab_ref[head:head + 1, :], (WINDOW, BIAS_SPAN))
            band = pltpu.roll(row, 0, 1, stride=1, stride_axis=0)[:, 0:2 * WINDOW]
            rows = slice((head % 2) * WINDOW, (head % 2 + 1) * WINDOW)
            bias_scr[1, head // 2, rows, :] = band
            bias_scr[0, head // 2, rows, :] = jnp.where(in_own, band, MASK_VALUE)

    q_scr[...] = (_dot_w(h, wq_ref[...]) * (HEAD_DIM ** -0.5)).astype(BF16)
    k = _dot_w(h, wkv_ref[:, 0:kv_w])
    v = _dot_w(h, wkv_ref[:, kv_w:2 * kv_w])
    gate_scr[:, 0:HALF // 2] = _silu(_dot_w(h, wg0_ref[...]))
    gate_scr[:, HALF // 2:HALF] = _silu(_dot_w(h, wg1_ref[...]))
    for hk in range(N_KV):
        k_h = k[:, hk * HEAD_DIM:(hk + 1) * HEAD_DIM].astype(BF16)
        v_h = v[:, hk * HEAD_DIM:(hk + 1) * HEAD_DIM].astype(BF16)
        k_scr[WINDOW:WINDOW + tile, hk * kw:(hk + 1) * kw] = jnp.concatenate([k_h, k_h], axis=1)
        v_scr[WINDOW:WINDOW + tile, hk * vw:hk * vw + kw] = jnp.concatenate([v_h, v_h], axis=1)
    kwin_ref[...] = k[tile - WINDOW:tile, :]
    vwin_ref[...] = v[tile - WINDOW:tile, :]

    lane = lax.broadcasted_iota(jnp.int32, (WINDOW, kw), 1)
    lo = lane < HEAD_DIM
    keep_a = jnp.where(lo, 1.0, 0.0).astype(BF16)
    keep_b = jnp.where(lo, 0.0, 1.0).astype(BF16)
    is_a = lax.broadcasted_iota(jnp.int32, (2 * WINDOW, 1), 0) < WINDOW

    def block(n, carry):
        r0 = pl.multiple_of(n * WINDOW, WINDOW)
        rows = pl.ds(r0, WINDOW)
        keys = pl.ds(r0, 2 * WINDOW)
        first = jnp.where(jnp.logical_and(i == 0, n == 0), 0, 1)
        for hk in range(N_KV):
            for gp in range(GROUP // 2):
                a = hk * GROUP + 2 * gp
                slab = slice(a * HEAD_DIM, (a + 2) * HEAD_DIM)
                q2 = q_scr[rows, slab]
                lhs = jnp.concatenate([q2 * keep_a, q2 * keep_b], axis=0)
                s = _dot_nt(lhs, k_scr[keys, hk * kw:(hk + 1) * kw]) + bias_scr[first, a // 2]
                sink = jnp.where(is_a, sink_ref[a], sink_ref[a + 1])
                m = jnp.maximum(jnp.max(s, axis=-1, keepdims=True), sink)
                p = jnp.exp(s - m).astype(BF16)
                pv = _dot(p, v_scr[keys, hk * vw:(hk + 1) * vw])
                num = jnp.where(lo, pv[0:WINDOW, 0:kw], pv[WINDOW:2 * WINDOW, 0:kw])
                den = jnp.where(lo, pv[0:WINDOW, kw:2 * kw], pv[WINDOW:2 * WINDOW, kw:2 * kw])
                m_slab = jnp.where(lo, m[0:WINDOW], m[WINDOW:2 * WINDOW])
                den = den + jnp.exp(jnp.where(lo, sink_ref[a], sink_ref[a + 1]) - m_slab)
                yb_ref[rows, slab] = (num / den * gate_scr[rows, slab]).astype(BF16)
        return carry

    lax.fori_loop(0, tile // WINDOW, block, 0, unroll=True)
    k_scr[0:WINDOW, :] = k_scr[tile:tile + WINDOW, :]
    v_scr[0:WINDOW, :] = v_scr[tile:tile + WINDOW, :]


BIAS_SPAN = 3 * WINDOW


def _prompt_bias_table(rel_bias):
    dist = WINDOW - jnp.arange(BIAS_SPAN)
    table = jnp.where(((dist >= 0) & (dist < WINDOW))[:, None], rel_bias.astype(F32)[_rel_bucket(dist)], MASK_VALUE)
    return table.T


def _attn_prompt(h, w0, table, sinks):
    t = h.shape[0]
    tile = min(ROW_TILE, t)
    kv_w = N_KV * HEAD_DIM
    win_spec = pl.BlockSpec((WINDOW, kv_w), lambda i: (0, 0))
    return pl.pallas_call(
        _attn_prompt_kernel,
        grid=(t // tile,),
        in_specs=[_rows(tile, D_MODEL),
                  _cols(D_MODEL, HALF, 4), _cols(D_MODEL, 2 * kv_w, 10),
                  _cols(D_MODEL, HALF // 2, 11), _cols(D_MODEL, HALF // 2, 12),
                  _resident(table.shape), pl.BlockSpec(memory_space=pltpu.SMEM)],
        out_specs=[_rows(tile, HALF), win_spec, win_spec],
        out_shape=[jax.ShapeDtypeStruct((t, HALF), BF16),
                   jax.ShapeDtypeStruct((WINDOW, kv_w), F32), jax.ShapeDtypeStruct((WINDOW, kv_w), F32)],
        scratch_shapes=[pltpu.VMEM((tile, HALF), BF16), pltpu.VMEM((tile, HALF), F32),
                        pltpu.VMEM((tile + WINDOW, 2 * kv_w), BF16), pltpu.VMEM((tile + WINDOW, 4 * kv_w), BF16),
                        pltpu.VMEM((2, N_HEADS // 2, 2 * WINDOW, 2 * WINDOW), F32)],
        compiler_params=_params(),
        name="attn_prompt",
    )(h, w0, w0, w0, w0, table, sinks)


def _attn_proj_kernel(h_ref, wq_ref, wkv_ref, wg0_ref, wg1_ref, qg_ref, kt_ref, vt_ref, kv_scr):
    kv_w = N_KV * HEAD_DIM
    h = h_ref[...]
    q = _dot_w(h, wq_ref[...]) * (HEAD_DIM ** -0.5)
    for hk in range(N_KV):
        for g in range(GROUP):
            src = (hk * GROUP + g) * HEAD_DIM
            dst = (g * N_KV + hk) * HEAD_DIM
            qg_ref[:, dst:dst + HEAD_DIM] = q[:, src:src + HEAD_DIM]
    qg_ref[:, HALF:HALF + HALF // 2] = _dot_w(h, wg0_ref[...])
    qg_ref[:, HALF + HALF // 2:2 * HALF] = _dot_w(h, wg1_ref[...])
    kv_scr[...] = _dot_w(h, wkv_ref[...])
    for j in range(kt_ref.shape[0]):
        kt_ref[j] = kv_scr[j * WINDOW:(j + 1) * WINDOW, 0:kv_w].T
        vt_ref[j] = kv_scr[j * WINDOW:(j + 1) * WINDOW, kv_w:2 * kv_w].T


def _attn_proj(h, w0):
    t = h.shape[0]
    kv_w = N_KV * HEAD_DIM
    out_shape = [jax.ShapeDtypeStruct((t, 2 * HALF), F32), jax.ShapeDtypeStruct((t // WINDOW, kv_w, WINDOW), F32),
                 jax.ShapeDtypeStruct((t // WINDOW, kv_w, WINDOW), F32)]
    return pl.pallas_call(
        _attn_proj_kernel,
        grid=(1,),
        in_specs=[_resident((t, D_MODEL)),
                  _cols(D_MODEL, HALF, 4), _cols(D_MODEL, 2 * kv_w, 10),
                  _cols(D_MODEL, HALF // 2, 11), _cols(D_MODEL, HALF // 2, 12)],
        out_specs=[_resident(s.shape) for s in out_shape],
        out_shape=out_shape,
        scratch_shapes=[pltpu.VMEM((t, 2 * kv_w), F32)],
        compiler_params=_params(),
        name="attn_proj_sample",
    )(h, w0, w0, w0, w0)


ATTN_S_BATCH = 16
KEYS_PAD = 2 * WINDOW


def _attn_sample_kernel(qg_ref, ktn_ref, vtn_ref, ck_ref, cv_ref, bias_ref, sink_ref, yb_ref, nk_ref, nv_ref):
    kv_w = N_KV * HEAD_DIM
    row8 = lax.broadcasted_iota(jnp.int32, (8, kv_w), 0)
    lane_head = lax.broadcasted_iota(jnp.int32, (8, kv_w), 1) // HEAD_DIM
    lower = row8 < DEC_SEQ
    pick = [jnp.where(lane_head == 2 * hp + jnp.where(lower, 0, 1), 1.0, 0.0).astype(F32) for hp in range(2)]
    lower_w = lax.broadcasted_iota(jnp.int32, (8, HALF), 0) < DEC_SEQ
    kept = lax.broadcasted_iota(jnp.int32, (kv_w, WINDOW), 1) < WINDOW - DEC_SEQ
    seq0 = pl.program_id(0) * ATTN_S_BATCH
    per_tile = WINDOW // DEC_SEQ

    def slide(old, new_tile, shift):
        return jnp.where(kept, pltpu.roll(old, WINDOW - DEC_SEQ, 1), pltpu.roll(new_tile, shift, 1))

    def pair(p, carry):
        r0 = pl.multiple_of(p * 8, 8)
        rows = qg_ref[pl.ds(r0, 8), :]
        q8 = rows[:, 0:HALF]
        gate8 = rows[:, HALF:2 * HALF]
        out8 = []
        for sub in range(2):
            b = 2 * p + sub
            q_swap = pltpu.roll(q8, 4, 0)
            q_dup = jnp.where(lower_w, q8, q_swap) if sub == 0 else jnp.where(lower_w, q_swap, q8)
            tile = (seq0 + b) // per_tile
            shift = (2 * WINDOW - DEC_SEQ - DEC_SEQ * ((seq0 + b) % per_tile)) % WINDOW
            k_old = ck_ref[b].reshape(kv_w, WINDOW)
            v_old = cv_ref[b].reshape(kv_w, WINDOW)
            k_win = slide(k_old, ktn_ref[tile], shift)
            v_win = slide(v_old, vtn_ref[tile], shift)
            nk_ref[b] = k_win.reshape(N_KV, HEAD_DIM, WINDOW)
            nv_ref[b] = v_win.reshape(N_KV, HEAD_DIM, WINDOW)
            k_all = jnp.concatenate([k_old, k_win], axis=1).astype(BF16)
            v_all = jnp.concatenate([v_old, v_win], axis=1).astype(BF16)
            q_bd = jnp.concatenate(
                [q_dup[:, g * kv_w:(g + 1) * kv_w] * pick[hp] for g in range(GROUP) for hp in range(2)], axis=0)
            s = _dot(q_bd.astype(BF16), k_all) + bias_ref[...]
            o = _attn_softmax_pv(s, sink_ref[:, 0:1], v_all, v_transposed=True)
            out_g = []
            for g in range(GROUP):
                acc = None
                for hp in range(2):
                    piece = o[(2 * g + hp) * 8:(2 * g + hp + 1) * 8, :] * pick[hp]
                    piece = piece + pltpu.roll(piece, 4, 0)
                    acc = piece if acc is None else acc + piece
                out_g.append(acc)
            out8.append(jnp.concatenate(
                [out_g[g][:, hk * HEAD_DIM:(hk + 1) * HEAD_DIM] for hk in range(N_KV) for g in range(GROUP)], axis=1))
        o8 = jnp.where(lower_w, out8[0], out8[1])
        yb_ref[pl.ds(r0, 8), :] = (o8 * _silu(gate8)).astype(BF16)
        return carry

    lax.fori_loop(0, ATTN_S_BATCH // 2, pair, 0, unroll=2)


def _sample_bias(rel_bias, sinks):
    t = jnp.arange(DEC_SEQ)[:, None]
    j = jnp.arange(KEYS_PAD)[None, :]
    pos = jnp.where(j < WINDOW, j, j - (KEYS_PAD - DEC_SEQ) + WINDOW)
    dist = t + WINDOW - pos
    valid = (dist >= 0) & (dist < WINDOW) & ((j < WINDOW) | (j >= KEYS_PAD - DEC_SEQ))
    bias = jnp.where(valid[:, :, None], rel_bias.astype(F32)[_rel_bucket(dist)], MASK_VALUE)
    bias = bias.reshape(DEC_SEQ, KEYS_PAD, N_KV, GROUP).transpose(3, 2, 0, 1).reshape(N_HEADS * DEC_SEQ, KEYS_PAD)
    sink = jnp.broadcast_to(sinks.astype(F32).reshape(N_KV, GROUP).T[:, :, None], (GROUP, N_KV, DEC_SEQ))
    return bias, jnp.broadcast_to(sink.reshape(N_HEADS * DEC_SEQ, 1), (N_HEADS * DEC_SEQ, 128))


def _attn_sample(qg, kt_new, vt_new, cache_kt, cache_vt, bias, sink):
    n_seq = cache_kt.shape[0]
    bb = ATTN_S_BATCH
    cache_spec = pl.BlockSpec((bb, N_KV, HEAD_DIM, WINDOW), lambda i: (i, 0, 0, 0))
    return pl.pallas_call(
        _attn_sample_kernel,
        grid=(n_seq // bb,),
        in_specs=[_rows(bb * DEC_SEQ, 2 * HALF), _resident(kt_new.shape), _resident(vt_new.shape),
                  cache_spec, cache_spec, _resident(bias.shape), _resident(sink.shape)],
        out_specs=[_rows(bb * DEC_SEQ, HALF), cache_spec, cache_spec],
        out_shape=[jax.ShapeDtypeStruct((n_seq * DEC_SEQ, HALF), BF16),
                   jax.ShapeDtypeStruct(cache_kt.shape, F32), jax.ShapeDtypeStruct(cache_vt.shape, F32)],
        compiler_params=_params(),
        name="attn_sample",
    )(qg, kt_new, vt_new, cache_kt, cache_vt, bias, sink)


def _prep_layer0(g_pre, w_in, conv_w, rel_bias, sinks, w_out, g_post):
    return dict(
        g_pre=g_pre.reshape(1, D_MODEL), w0=w_in, conv_w=conv_w, rel_bias=rel_bias, sinks=sinks,
        w_out=w_out, g_post=g_post.reshape(1, D_MODEL))


def _layer0_prompt(x, w):
    ya, s_tail, h = _conv_a(x, w['g_pre'], w['w0'], w['conv_w'])
    yb, kwin, vwin = _attn_prompt(h, w['w0'], _prompt_bias_table(w['rel_bias']), w['sinks'])
    y = _out_proj(ya, yb, x, w['w_out'], w['g_post'])
    return (y, s_tail[6:8], kwin.reshape(WINDOW, N_KV, HEAD_DIM), vwin.reshape(WINDOW, N_KV, HEAD_DIM))


def _layer0_sample(x, conv_state, cache_k, cache_v, w):
    n_seq = x.shape[0]
    rows = x.reshape(n_seq * DEC_SEQ, D_MODEL)
    ya, s, h = _conv_a(rows, w['g_pre'], w['w0'], w['conv_w'], conv_state.reshape(n_seq, 2 * HALF))
    qg, kt_new, vt_new = _attn_proj(h, w['w0'])
    bias, sink = _sample_bias(w['rel_bias'], w['sinks'])
    yb, new_kt, new_vt = _attn_sample(qg, kt_new, vt_new, cache_k.transpose(0, 2, 3, 1), cache_v.transpose(0, 2, 3, 1),
                                      bias, sink)
    y = _out_proj(ya, yb, rows, w['w_out'], w['g_post'])
    return (y.reshape(n_seq, DEC_SEQ, D_MODEL), s.reshape(n_seq, DEC_SEQ, HALF)[:, DEC_SEQ - 2:],
            new_kt.transpose(0, 3, 1, 2), new_vt.transpose(0, 3, 1, 2))


def _layer_norm(v, g, b):
    xc = v - jnp.mean(v, axis=-1, keepdims=True)
    return xc * lax.rsqrt(jnp.mean(xc * xc, axis=-1, keepdims=True) + NORM_EPS) * g + b


def _cmlp_prompt_kernel(x_ref, g_ref, w_hbm, lng_ref, lnb_ref, ws_ref, bs_ref, yc_ref, h_ref,
                        vn_scr, w_scr, stage_scr, sem):
    tile = x_ref.shape[0]

    @pl.when(pl.program_id(0) == 0)
    def _():
        _load_transposed(w_hbm, 0, 3 * HALF, w_scr, stage_scr, sem)

    h = _rms_bf16(x_ref[...], g_ref[...])
    h_ref[...] = h
    v = _dot(h, w_scr[:, HALF:2 * HALF])
    vn_scr[...] = _layer_norm(v, lng_ref[...], lnb_ref[...]).astype(BF16)
    gw = HALF // CMLP_GROUPS
    cols = 2 * gw
    for cb in range(HALF // cols):
        u = _dot(h, w_scr[:, cb * cols:(cb + 1) * cols])
        gate = _silu(_dot(h, w_scr[:, 2 * HALF + cb * cols:2 * HALF + (cb + 1) * cols]))
        for gi in range(2):
            grp = 2 * cb + gi
            lanes = slice(grp * gw, (grp + 1) * gw)
            for n in range(tile // CHUNK):
                rows = slice(n * CHUNK, (n + 1) * CHUNK)
                mixed = _dot(ws_ref[grp], vn_scr[rows, lanes]) + bs_ref[grp]
                yc_ref[rows, lanes] = (u[rows, gi * gw:(gi + 1) * gw] * mixed
                                       * gate[rows, gi * gw:(gi + 1) * gw]).astype(BF16)


def _cmlp_prompt(x, g_pre, w_c, ln_g, ln_b, ws_tril, bs_rows):
    t = x.shape[0]
    tile = min(ROW_TILE, t)
    return pl.pallas_call(
        _cmlp_prompt_kernel,
        grid=(t // tile,),
        in_specs=[_rows(tile, D_MODEL), _resident((1, D_MODEL)), pl.BlockSpec(memory_space=pl.ANY),
                  _resident((1, HALF)), _resident((1, HALF)), _resident(ws_tril.shape), _resident(bs_rows.shape)],
        out_specs=[_rows(tile, HALF), _rows(tile, D_MODEL)],
        out_shape=[jax.ShapeDtypeStruct((t, HALF), BF16), jax.ShapeDtypeStruct((t, D_MODEL), BF16)],
        scratch_shapes=[pltpu.VMEM((tile, HALF), BF16), pltpu.VMEM((D_MODEL, 3 * HALF), BF16),
                        pltpu.VMEM((2, W_PIECE, D_MODEL), F32), pltpu.SemaphoreType.DMA((2,))],
        compiler_params=_params(),
        name="cmlp_prompt",
    )(x, g_pre, w_c, ln_g, ln_b, ws_tril, bs_rows)


def _cmlp_sample_kernel(x_ref, g_ref, w_ref, lng_ref, lnb_ref, coef_ref, bias_ref, yc_ref, vn_ref, h_ref):
    t = x_ref.shape[0]
    h = _rms_bf16(x_ref[...], g_ref[...])
    h_ref[...] = h
    u = _dot_wt(h, w_ref[0:HALF, :])
    vn = _layer_norm(_dot_wt(h, w_ref[HALF:2 * HALF, :]), lng_ref[...], lnb_ref[...])
    gate = _silu(_dot_wt(h, w_ref[2 * HALF:3 * HALF, :]))
    vn_ref[...] = vn

    def tiled(a):
        return a.reshape(t // 8, 8, HALF)

    mixed = tiled(vn) * coef_ref[0][None] + bias_ref[...][None]
    for k in range(1, DEC_SEQ):
        mixed = mixed + tiled(pltpu.roll(vn, k, 0)) * coef_ref[k][None]
    yc_ref[...] = (u * mixed.reshape(t, HALF) * gate).astype(BF16)


def _cmlp_sample(x, g_pre, w_c, ln_g, ln_b, coef, bias):
    t = x.shape[0]
    return pl.pallas_call(
        _cmlp_sample_kernel,
        grid=(1,),
        in_specs=[_resident((t, D_MODEL)), _resident((1, D_MODEL)), _rowwin(3 * HALF, D_MODEL, 0),
                  _resident((1, HALF)), _resident((1, HALF)), _resident(coef.shape), _resident(bias.shape)],
        out_specs=[_resident((t, HALF)), _resident((t, HALF)), _resident((t, D_MODEL))],
        out_shape=[jax.ShapeDtypeStruct((t, HALF), BF16), jax.ShapeDtypeStruct((t, HALF), F32),
                   jax.ShapeDtypeStruct((t, D_MODEL), BF16)],
        compiler_params=_params(),
        name="cmlp_sample",
    )(x, g_pre, w_c, ln_g, ln_b, coef, bias)


HEAD_LANES = 128
SSD_GW = HALF // SSD_GROUPS


def _softplus(x):
    return jnp.maximum(x, 0.0) + jnp.log1p(jnp.exp(-jnp.abs(x)))


def _dt_proj(h, wdt_ref):
    pad = jnp.zeros((HEAD_LANES - SSD_HEADS, D_MODEL), F32)
    return _dot_wt(h, jnp.concatenate([wdt_ref[...], pad], axis=0))


def _group_norm_gate(y, z, gn):
    gated = y * _silu(z)
    parts = []
    for g in range(SSD_GROUPS):
        part = gated[:, g * SSD_GW:(g + 1) * SSD_GW]
        parts.append(part * lax.rsqrt(jnp.mean(part * part, axis=-1, keepdims=True) + NORM_EPS))
    return (jnp.concatenate(parts, axis=1) * gn).astype(BF16)


def _ssd_prompt_kernel(h_ref, w_hbm, wdt_ref, cw_ref, cb_ref, dtb_ref, alog_ref,
                       dskip_ref, gn_ref, e3_ref, tril3_ref, yd_ref, tail_ref, ssm_ref,
                       xbc_scr, z_scr, dt_scr, ht_scr, shift_scr, w_scr, stage_scr, sem):
    tile = h_ref.shape[0]
    i = pl.program_id(0)
    cd = SSD_CONV_DIM
    h = h_ref[...]

    @pl.when(i == 0)
    def _():
        tail_ref[...] = jnp.zeros_like(tail_ref)
        ht_scr[...] = jnp.zeros_like(ht_scr)
        _load_transposed(w_hbm, 3 * HALF, HALF + cd, w_scr, stage_scr, sem)

    z_scr[...] = _dot(h, w_scr[:, 0:HALF])
    dt_scr[...] = _softplus(_dt_proj(h, wdt_ref) + dtb_ref[...])
    third = cd // 3
    for j in range(3):
        cols = slice(j * third, (j + 1) * third)
        raw = _dot(h, w_scr[:, HALF + j * third:HALF + (j + 1) * third])
        shift_scr[0:8, :] = tail_ref[:, cols]
        shift_scr[8:8 + tile, :] = raw
        conv = raw * cw_ref[3:4, cols] + cb_ref[:, cols]
        for k in range(1, 4):
            conv = conv + shift_scr[8 - k:8 - k + tile, :] * cw_ref[3 - k:4 - k, cols]
        xbc_scr[:, cols] = _silu(conv)
        tail_ref[:, cols] = raw[tile - 8:tile, :]

    a16 = -jnp.exp(alog_ref[...])
    causal =(lax.broadcasted_iota(jnp.int32, (CHUNK, CHUNK), 0)
              >= lax.broadcasted_iota(jnp.int32, (CHUNK, CHUNK), 1))
    first_half = lax.broadcasted_iota(jnp.int32, (CHUNK, 2 * HEAD_DIM), 1) < HEAD_DIM
    keep_a = jnp.where(first_half, 1.0, 0.0).astype(BF16)
    keep_b = jnp.where(first_half, 0.0, 1.0).astype(BF16)

    def chunk(n, carry):
        r0 = pl.multiple_of(n * CHUNK, CHUNK)
        rows = pl.ds(r0, CHUNK)
        xs = xbc_scr[rows, 0:HALF]
        dt16 = dt_scr[rows, :]
        dt_e = _dot(jnp.concatenate(_split3(dt16), axis=1), e3_ref[...])
        acs16 = _dot(tril3_ref[...], jnp.concatenate(_split3(dt16 * a16), axis=0))
        acs_e = _dot(jnp.concatenate(_split3(acs16), axis=1), e3_ref[...])
        acs_t = acs16.T
        last_e = acs_e[CHUNK - 1:CHUNK, :]
        xdt = xs * dt_e
        xdt_bf = xdt.astype(BF16)
        xw = (jnp.exp(last_e - acs_e) * xdt).astype(BF16)
        dec_e = jnp.exp(last_e)
        y_parts = []
        yoff_parts = []
        for g in range(SSD_GROUPS):
            c_g = xbc_scr[rows, HALF + 2 * SSD_STATE + g * SSD_STATE:HALF + 2 * SSD_STATE + (g + 1) * SSD_STATE].astype(BF16)
            b_g = xbc_scr[rows, HALF + g * SSD_STATE:HALF + (g + 1) * SSD_STATE].astype(BF16)
            cb = _dot_nt(c_g, b_g)
            h_prev = ht_scr[g]
            yoff_parts.append(_dot(c_g, h_prev.astype(BF16)))
            for r in range(0, SSD_HEADS // SSD_GROUPS, 2):
                wgt = []
                for hd in (g * (SSD_HEADS // SSD_GROUPS) + r, g * (SSD_HEADS // SSD_GROUPS) + r + 1):
                    seg = acs16[:, hd:hd + 1] - acs_t[hd:hd + 1, :]
                    wgt.append(cb * jnp.exp(jnp.where(causal, seg, -jnp.inf)))
                a = g * (SSD_HEADS // SSD_GROUPS) + r
                slab = xdt_bf[:, a * HEAD_DIM:(a + 2) * HEAD_DIM]
                rhs = jnp.concatenate([slab * keep_a, slab * keep_b], axis=0)
                y_parts.append(_dot(jnp.concatenate(wgt, axis=1).astype(BF16), rhs))
            lanes = slice(g * SSD_GW, (g + 1) * SSD_GW)
            ht_scr[g] = h_prev * dec_e[:, lanes] + _dot_tn(b_g, xw[:, lanes])
        y = (jnp.concatenate(y_parts, axis=1) + jnp.concatenate(yoff_parts, axis=1) * jnp.exp(acs_e)
             + dskip_ref[...] * xs)
        yd_ref[rows, :] = _group_norm_gate(y, z_scr[rows, :], gn_ref[...])
        return carry

    lax.fori_loop(0, tile // CHUNK, chunk, 0, unroll=True)

    @pl.when(i == pl.num_programs(0) - 1)
    def _():
        for g in range(SSD_GROUPS):
            ssm_ref[g * SSD_GW:(g + 1) * SSD_GW, :] = ht_scr[g].T


def _ssd_weight_specs():
    third = SSD_CONV_DIM // 3
    first = 4 * HALF // third
    return ([_rowwin(HALF, D_MODEL, 3)] + [_rowwin(third, D_MODEL, first + j) for j in range(3)]
            + [_rowwin(SSD_HEADS, D_MODEL, (4 * HALF + SSD_CONV_DIM) // SSD_HEADS)])


def _ssd_prompt(h, w):
    t = h.shape[0]
    tile = min(ROW_TILE, t)
    cd = SSD_CONV_DIM
    consts = [w['conv_w'], w['conv_b'], w['dt_bias16'], w['a_log16'],
              w['d_skip_e'], w['gate_norm_g'], w['expand3'], w['tril3']]
    return pl.pallas_call(
        _ssd_prompt_kernel,
        grid=(t // tile,),
        in_specs=[_rows(tile, D_MODEL), pl.BlockSpec(memory_space=pl.ANY), _ssd_weight_specs()[-1]]
                 + [_resident(c.shape) for c in consts],
        out_specs=[_rows(tile, HALF), pl.BlockSpec((8, cd), lambda i: (0, 0)),
                   pl.BlockSpec((HALF, SSD_STATE), lambda i: (0, 0))],
        out_shape=[jax.ShapeDtypeStruct((t, HALF), BF16), jax.ShapeDtypeStruct((8, cd), F32),
                   jax.ShapeDtypeStruct((HALF, SSD_STATE), F32)],
        scratch_shapes=[pltpu.VMEM((tile, cd), F32), pltpu.VMEM((tile, HALF), F32),
                        pltpu.VMEM((tile, HEAD_LANES), F32), pltpu.VMEM((SSD_GROUPS, SSD_STATE, SSD_GW), F32),
                        pltpu.VMEM((8 + tile, cd // 3), F32), pltpu.VMEM((D_MODEL, HALF + cd), BF16),
                        pltpu.VMEM((2, W_PIECE, D_MODEL), F32), pltpu.SemaphoreType.DMA((2,))],
        compiler_params=_params(),
        name="ssd_prompt",
    )(h, w['w1'], w['w1'], *consts)


def _ssd_sample_pre_kernel(h_ref, wz_ref, wx0_ref, wx1_ref, wx2_ref, wdt_ref, cw_ref, cb_ref, st_ref,
                           dtb_ref, aloge_ref, dskip_ref, e3_ref, seg_ref,
                           nconv_ref, z_ref, ysk_ref, eacs_ref, xw_ref, dec_ref, b_ref, c_ref, raw_scr):
    t = h_ref.shape[0]
    n_seq = t // DEC_SEQ
    h = h_ref[...]
    z_ref[...] = _dot_wt(h, wz_ref[...])
    raw = jnp.concatenate([_dot_wt(h, wx0_ref[...]), _dot_wt(h, wx1_ref[...]), _dot_wt(h, wx2_ref[...])], axis=1)
    for c in range(raw_scr.shape[0]):
        lanes = slice(c * 128, (c + 1) * 128)
        raw_scr[c] = raw[:, lanes]
        for j in range(3):
            nconv_ref[j, :, lanes] = raw_scr[c, pl.ds(j + 1, n_seq, stride=DEC_SEQ), :]
    dt16 = _softplus(_dt_proj(h, wdt_ref) + dtb_ref[...])
    dt = _dot(jnp.concatenate(_split3(dt16), axis=1), e3_ref[...])
    old = [st_ref[j] for j in range(3)]
    p1 = _place_steps(t, [(0, old[2])])
    p2 = _place_steps(t, [(0, old[1]), (1, old[2])])
    p3 = _place_steps(t, [(0, old[0]), (1, old[1]), (2, old[2])])

    def step_of(width):
        return lax.broadcasted_iota(jnp.int32, (t, width), 0) % DEC_SEQ

    def back(a, k):
        return jnp.where(step_of(a.shape[1]) >= k, pltpu.roll(a, k, 0), 0.0)

    def ahead(a, k):
        return jnp.where(step_of(a.shape[1]) + k < DEC_SEQ, pltpu.roll(a, t - k, 0), 0.0)

    conv = (raw * cw_ref[3:4, :] + (back(raw, 1) + p1) * cw_ref[2:3, :]
            + (back(raw, 2) + p2) * cw_ref[1:2, :] + (back(raw, 3) + p3) * cw_ref[0:1, :]
            + cb_ref[...])
    xbc = _silu(conv)
    xs = xbc[:, 0:HALF]
    bm = xbc[:, HALF:HALF + 2 * SSD_STATE]
    cm = xbc[:, HALF + 2 * SSD_STATE:]
    b_ref[...] = bm
    c_ref[...] = cm
    da = dt * (-jnp.exp(aloge_ref[...]))
    acs = da + back(da, 1) + back(da, 2) + back(da, 3)
    suffix = ahead(da, 1) + ahead(da, 2) + ahead(da, 3)
    xdt = xs * dt
    y = _dot((cm * bm).astype(BF16), seg_ref[...]) * xdt
    for k in range(1, DEC_SEQ):
        cbk = _dot((cm * pltpu.roll(bm, k, 0)).astype(BF16), seg_ref[...])
        term = cbk * jnp.exp(acs - pltpu.roll(acs, k, 0)) * pltpu.roll(xdt, k, 0)
        y = y + jnp.where(step_of(HALF) >= k, term, 0.0)
    ysk_ref[...] = y + dskip_ref[...] * xs
    eacs_ref[...] = jnp.exp(acs)
    xw_ref[...] = jnp.exp(suffix) * xdt
    dec_ref[...] = jnp.exp(acs + suffix)


def _ssd_sample_pre(h, conv_state, w):
    t = h.shape[0]
    cd = SSD_CONV_DIM
    tile = min(SSD_PRE_ROWS, t)
    seqs = tile // DEC_SEQ
    state_spec = pl.BlockSpec((3, seqs, cd), lambda i: (0, i, 0))
    head = [w['conv_w'], w['conv_b']]
    tail = [w['dt_bias16'], w['a_log_e'], w['d_skip_e'], w['expand3'], w['seg_expand']]
    args = [h, w['w1'], w['w1'], w['w1'], w['w1'], w['w1']] + head + [conv_state] + tail
    wide = jax.ShapeDtypeStruct((t, HALF), F32)
    narrow = jax.ShapeDtypeStruct((t, 2 * SSD_STATE), F32)
    out_shape = [jax.ShapeDtypeStruct(conv_state.shape, F32), wide, wide, wide, wide, wide, narrow, narrow]
    return pl.pallas_call(
        _ssd_sample_pre_kernel,
        grid=(t // tile,),
        in_specs=[_rows(tile, D_MODEL)] + _ssd_weight_specs()
                 + [_resident(c.shape) for c in head] + [state_spec] + [_resident(c.shape) for c in tail],
        out_specs=[state_spec] + [_rows(tile, HALF)] * 5 + [_rows(tile, 2 * SSD_STATE)] * 2,
        out_shape=out_shape,
        scratch_shapes=[pltpu.VMEM((cd // 128, tile, 128), F32)],
        compiler_params=_params(),
        name="ssd_sample_pre",
    )(*args)


SSD_S_BATCH = 8
SSD_PRE_ROWS = 256


def _ssd_sample_state_kernel(st_ref, c_ref, b_ref, xw_ref, dec_ref, eacs_ref, ysk_ref, z_ref, gn_ref,
                             yd_ref, nst_ref):
    row_n = lax.broadcasted_iota(jnp.int32, (8, SSD_STATE), 0)
    row_w = lax.broadcasted_iota(jnp.int32, (8, SSD_GW), 0)
    row_f = lax.broadcasted_iota(jnp.int32, (8, HALF), 0)
    ones_rows = jnp.where((row_n >= 4) & (row_n < 7), 1.0, 0.0).astype(BF16)
    hpg = SSD_HEADS // SSD_GROUPS

    def pair(p, carry):
        r0 = pl.multiple_of(p * 8, 8)
        rows = pl.ds(r0, 8)
        c8 = c_ref[rows, :].astype(BF16)
        b8 = b_ref[rows, :]
        xw8 = xw_ref[rows, :]
        dec8 = dec_ref[rows, :]
        yoff = []
        for sub in range(2):
            b = 2 * p + sub
            xw_own = xw8 if sub == 0 else pltpu.roll(xw8, 4, 0)
            b_own = b8 if sub == 0 else pltpu.roll(b8, 4, 0)
            hi, mid, lo = (term.astype(F32) for term in _split3(dec8[4 * sub:4 * sub + 1, :]))
            parts = []
            for g in range(SSD_GROUPS):
                lanes = slice(g * SSD_GW, (g + 1) * SSD_GW)
                heads = pl.ds(g * hpg, hpg)
                h0 = st_ref[b, heads].reshape(SSD_GW, SSD_STATE)
                parts.append(_dot_nt(c8[:, g * SSD_STATE:(g + 1) * SSD_STATE], h0.astype(BF16)))
                lhs = jnp.where(row_w < 4, xw_own[:, lanes],
                                jnp.where(row_w == 4, hi[:, lanes],
                                          jnp.where(row_w == 5, mid[:, lanes],
                                                    jnp.where(row_w == 6, lo[:, lanes], 0.0)))).astype(BF16)
                rhs_b = jnp.where(row_n < 4, b_own[:, g * SSD_STATE:(g + 1) * SSD_STATE], 0.0).astype(BF16)
                decay = _dot_tn(lhs, ones_rows)
                nst_ref[b, heads] = (h0 * decay + _dot_tn(lhs, rhs_b)).reshape(hpg, HEAD_DIM, SSD_STATE)
            yoff.append(jnp.concatenate(parts, axis=1))
        yoff8 = jnp.where(row_f < 4, yoff[0], yoff[1])
        y = ysk_ref[rows, :] + yoff8 * eacs_ref[rows, :]
        yd_ref[rows, :] = _group_norm_gate(y, z_ref[rows, :], gn_ref[...])
        return carry

    lax.fori_loop(0, SSD_S_BATCH // 2, pair, 0, unroll=True)


def _ssd_sample_state(state, cm, bm, xw, dec, eacs, ysk, z, gn):
    n_seq = state.shape[0]
    bb = SSD_S_BATCH
    r = bb * DEC_SEQ
    st_spec = pl.BlockSpec((bb, SSD_HEADS, HEAD_DIM, SSD_STATE), lambda i: (i, 0, 0, 0))
    return pl.pallas_call(
        _ssd_sample_state_kernel,
        grid=(n_seq // bb,),
        in_specs=[st_spec, _rows(r, 2 * SSD_STATE), _rows(r, 2 * SSD_STATE)] + [_rows(r, HALF)] * 5
                 + [_resident((1, HALF))],
        out_specs=[_rows(r, HALF), st_spec],
        out_shape=[jax.ShapeDtypeStruct((n_seq * DEC_SEQ, HALF), BF16), jax.ShapeDtypeStruct(state.shape, F32)],
        compiler_params=_params(),
        name="ssd_sample_state",
    )(state, cm, bm, xw, dec, eacs, ysk, z, gn)


def _prep_layer1(g_pre, w_in, ln_g, ln_b, w_s, b_s, conv_w, conv_b, dt_bias, a_log, d_skip, gate_norm_g,
                 w_out, g_post):
    cd = SSD_CONV_DIM
    gw = HALF // CMLP_GROUPS
    w1 = w_in.T

    def lanes16(v):
        return jnp.pad(v.astype(F32), (0, HEAD_LANES - SSD_HEADS)).reshape(1, HEAD_LANES)

    def per_channel(v):
        return jnp.repeat(v.astype(F32), HEAD_DIM).reshape(1, HALF)

    head_of = np.arange(HALF) // HEAD_DIM
    expand = jnp.asarray(np.arange(HEAD_LANES)[:, None] == head_of[None, :], BF16)
    tril = jnp.asarray(np.tril(np.ones((CHUNK, CHUNK))), BF16)
    grp_rows = np.arange(2 * SSD_STATE) // SSD_STATE
    seg_expand = jnp.asarray(grp_rows[:, None] == (head_of // (SSD_HEADS // SSD_GROUPS))[None, :], BF16)

    w4 = jnp.tril(w_s[:, :DEC_SEQ, :DEC_SEQ])
    steps = jnp.arange(DEC_SEQ)
    coef = []
    for k in range(DEC_SEQ):
        src = steps - k
        ck = jnp.where((src >= 0)[None, :], w4[:, steps, jnp.maximum(src, 0)], 0.0)
        ck = jnp.repeat(ck.T, gw, axis=1)
        coef.append(jnp.concatenate([ck, ck], axis=0))
    bias4 = jnp.repeat(b_s[:, :DEC_SEQ].T, gw, axis=1)
    return dict(
        g_pre=g_pre.reshape(1, D_MODEL), w1=w1, ln_g=ln_g.reshape(1, HALF), ln_b=ln_b.reshape(1, HALF),
        ws_tril=jnp.tril(w_s).astype(BF16),
        bs_rows=jnp.broadcast_to(b_s.astype(F32)[:, :, None], (CMLP_GROUPS, CHUNK, gw)),
        coef=jnp.stack(coef).astype(F32), bias4=jnp.concatenate([bias4, bias4], axis=0).astype(F32),
        conv_w=conv_w, conv_b=conv_b.reshape(1, cd), dt_bias16=lanes16(dt_bias), a_log16=lanes16(a_log),
        a_log_e=per_channel(a_log), d_skip_e=per_channel(d_skip), gate_norm_g=gate_norm_g.reshape(1, HALF),
        expand3=jnp.concatenate([expand] * 3, axis=0), tril3=jnp.concatenate([tril] * 3, axis=1),
        seg_expand=seg_expand, w_out=w_out, g_post=g_post.reshape(1, D_MODEL))


def _layer1_prompt(x, w):
    yc, h = _cmlp_prompt(x, w['g_pre'], w['w1'], w['ln_g'], w['ln_b'], w['ws_tril'], w['bs_rows'])
    yd, tail, ssm = _ssd_prompt(h, w)
    y = _out_proj(yc, yd, x, w['w_out'], w['g_post'])
    return y, tail[5:8], ssm.reshape(SSD_HEADS, HEAD_DIM, SSD_STATE)


def _layer1_sample(x, conv_state, ssm_state, w):
    n_seq = x.shape[0]
    t = n_seq * DEC_SEQ
    rows = x.reshape(t, D_MODEL)
    yc, vn, h = _cmlp_sample(rows, w['g_pre'], w['w1'], w['ln_g'], w['ln_b'], w['coef'], w['bias4'])
    new_conv, z, ysk, eacs, xw, dec, bm, cm = _ssd_sample_pre(h, conv_state.transpose(1, 0, 2), w)
    yd, new_state = _ssd_sample_state(ssm_state, cm, bm, xw, dec, eacs, ysk, z, w['gate_norm_g'])
    y = _out_proj(yc, yd, rows, w['w_out'], w['g_post'])
    return (y.reshape(n_seq, DEC_SEQ, D_MODEL), vn.reshape(n_seq, DEC_SEQ, HALF),
            new_conv.transpose(1, 0, 2), new_state)


def kernel(x_prompt, x_sample, state_conv_a, cache_win_k, cache_win_v, state_conv_d, state_ssm, rel_bias,
           l0_g_pre, l0_w_in, l0_conv_w, l0_sinks, l0_w_out, l0_g_post,
           l1_g_pre, l1_w_in, l1_ln_g, l1_ln_b, l1_w_s, l1_b_s, l1_conv_w, l1_conv_b, l1_dt_bias, l1_a_log,
           l1_d_skip, l1_gate_norm_g, l1_w_out, l1_g_post):
    w0 = _prep_layer0(l0_g_pre, l0_w_in, l0_conv_w, rel_bias, l0_sinks, l0_w_out, l0_g_post)
    w1 = _prep_layer1(l1_g_pre, l1_w_in, l1_ln_g, l1_ln_b, l1_w_s, l1_b_s, l1_conv_w, l1_conv_b, l1_dt_bias,
                      l1_a_log, l1_d_skip, l1_gate_norm_g, l1_w_out, l1_g_post)
    yp, p_conv_a, p_win_k, p_win_v = _layer0_prompt(x_prompt[0], w0)
    ys, s_conv_a, s_win_k, s_win_v = _layer0_sample(x_sample, state_conv_a, cache_win_k, cache_win_v, w0)
    yp, p_conv_d, p_ssm = _layer1_prompt(yp, w1)
    ys, s_chunk_v, s_conv_d, s_ssm = _layer1_sample(ys, state_conv_d, state_ssm, w1)
    return (yp[None], ys, p_conv_a[None], s_conv_a, p_win_k[None], p_win_v[None], s_win_k, s_win_v, s_chunk_v,
            p_conv_d[None], s_conv_d, p_ssm[None], s_ssm)
```

```python
import functools
import math

import jax
import jax.numpy as jnp
import numpy as np
from jax import lax
from jax.experimental import pallas as pl
from jax.experimental.pallas import tpu as pltpu

F32 = jnp.float32
BF16 = jnp.bfloat16

D_MODEL = 2048
HALF = 1024
HEAD_DIM = 64
N_HEADS = 16
N_KV = 4
GROUP = 4
WINDOW = 128
NUM_BUCKETS = 32
MAX_DISTANCE = 128
CMLP_GROUPS = 8
CHUNK = 128
SSD_HEADS = 16
SSD_STATE = 128
SSD_GROUPS = 2
SSD_CONV_DIM = HALF + 2 * SSD_GROUPS * SSD_STATE
DEC_SEQ = 4
NORM_EPS = 1e-6
MASK_VALUE = -1e30

ROW_TILE = 512
VMEM_LIMIT = 56 * 1024 * 1024


def _params(n_axes=1):
    return pltpu.CompilerParams(dimension_semantics=("arbitrary",) * n_axes,
                                vmem_limit_bytes=VMEM_LIMIT)


def _resident(shape):
    nd = len(shape)
    return pl.BlockSpec(shape, lambda *_: (0,) * nd, pipeline_mode=pl.Buffered(1))


def _rows(tile, width):
    return pl.BlockSpec((tile, width), lambda i: (i, 0))


def _cols(rows, width, block):
    return pl.BlockSpec((rows, width), lambda *_: (0, block), pipeline_mode=pl.Buffered(1))


def _rowwin(height, cols, block):
    return pl.BlockSpec((height, cols), lambda *_: (block, 0), pipeline_mode=pl.Buffered(1))


def _rms_bf16(x, g):
    ms = jnp.mean(x * x, axis=-1, keepdims=True)
    return (x * lax.rsqrt(ms + NORM_EPS) * g).astype(BF16)


def _silu(x):
    return x * jax.nn.sigmoid(x)


def _dot(a, b):
    return jnp.dot(a, b, preferred_element_type=F32)


def _dot_nt(a, b):
    return lax.dot_general(a, b, (((1,), (1,)), ((), ())), preferred_element_type=F32)


def _dot_tn(a, b):
    return lax.dot_general(a, b, (((0,), (0,)), ((), ())), preferred_element_type=F32)


def _dot_w(a, w):
    return _dot(a, w.astype(BF16))


def _dot_wt(a, wt):
    return _dot_nt(a, wt.astype(BF16))


def _split3(x):
    hi = x.astype(BF16)
    r1 = x - hi.astype(F32)
    mid = r1.astype(BF16)
    lo = (r1 - mid.astype(F32)).astype(BF16)
    return hi, mid, lo


def _place_steps(t, placements):
    n_seq = placements[0][1].shape[0]
    row = lax.broadcasted_iota(jnp.int32, (t, n_seq), 0)
    seq = lax.broadcasted_iota(jnp.int32, (t, n_seq), 1)
    lhs, rhs = [], []
    for step, state in placements:
        sel = jnp.where(row == DEC_SEQ * seq + step, 1.0, 0.0).astype(BF16)
        lhs += [sel] * 3
        rhs += list(_split3(state))
    return _dot(jnp.concatenate(lhs, axis=1), jnp.concatenate(rhs, axis=0))


def _prompt_rows(tile, width, n_p):
    return pl.BlockSpec((tile, width), lambda i: (jnp.minimum(i, n_p - 1), 0))


def _group_specs(arg, tile, width, n_p):
    if isinstance(arg, tuple):
        return [_prompt_rows(tile, width, n_p), _resident((tile, width))]
    return [_rows(tile, width)]


def _out_proj_tile(ya_ref, yb_ref, x_ref, w_ref, g_ref, o_ref):
    y = _dot_w(ya_ref[...], w_ref[0:HALF, :]) + _dot_w(yb_ref[...], w_ref[HALF:2 * HALF, :])
    ms = jnp.mean(y * y, axis=-1, keepdims=True)
    o_ref[...] = x_ref[...] + y * lax.rsqrt(ms + NORM_EPS) * g_ref[...]


def _out_proj_kernel(*refs, n_p, n_ya, n_yb, n_x):
    refs = list(refs)
    ya, yb, x = refs[:n_ya], refs[n_ya:n_ya + n_yb], refs[n_ya + n_yb:n_ya + n_yb + n_x]
    w_ref, g_ref = refs[n_ya + n_yb + n_x:n_ya + n_yb + n_x + 2]
    outs = refs[n_ya + n_yb + n_x + 2:]
    is_sample = pl.program_id(0) == n_p

    @pl.when(jnp.logical_not(is_sample))
    def _():
        _out_proj_tile(ya[0], yb[0], x[0], w_ref, g_ref, outs[0])

    @pl.when(is_sample)
    def _():
        _out_proj_tile(ya[-1], yb[-1], x[-1], w_ref, g_ref, outs[-1])


def _out_proj(ya, yb, x, w, g, n_p, tile, out_pair):
    groups = [(ya, HALF), (yb, HALF), (x, D_MODEL)]
    in_specs, args = [], []
    for arg, width in groups:
        in_specs += _group_specs(arg, tile, width, n_p)
        args += list(arg) if isinstance(arg, tuple) else [arg]
    if out_pair:
        out_specs = [_prompt_rows(tile, D_MODEL, n_p), _resident((tile, D_MODEL))]
        out_shape = [jax.ShapeDtypeStruct((n_p * tile, D_MODEL), F32), jax.ShapeDtypeStruct((tile, D_MODEL), F32)]
    else:
        out_specs = [_rows(tile, D_MODEL)]
        out_shape = [jax.ShapeDtypeStruct(((n_p + 1) * tile, D_MODEL), F32)]
    n_of = [2 if isinstance(arg, tuple) else 1 for arg, _ in groups]
    return pl.pallas_call(
        functools.partial(_out_proj_kernel, n_p=n_p, n_ya=n_of[0], n_yb=n_of[1], n_x=n_of[2]),
        grid=(n_p + 1,),
        in_specs=in_specs + [_resident((2 * HALF, D_MODEL)), _resident((1, D_MODEL))],
        out_specs=out_specs,
        out_shape=out_shape,
        compiler_params=_params(),
        name="out_proj",
    )(*args, w, g)


CONV_A_CHUNK = 256


def _conv_a_kernel(*refs, sample):
    if sample:
        x_ref, g_ref, w_ref, cw_ref, st_ref, ya_ref, s_ref, h_ref = refs
    else:
        x_ref, g_ref, w_ref, cw_ref, ya_ref, s_ref, h_ref, shift_scr = refs
    tile = x_ref.shape[0]
    cc = CONV_A_CHUNK
    h = _rms_bf16(x_ref[...], g_ref[...])
    h_ref[...] = h
    if not sample:
        @pl.when(pl.program_id(0) == 0)
        def _():
            s_ref[...] = jnp.zeros_like(s_ref)
    for c in range(HALF // cc):
        lanes = slice(c * cc, (c + 1) * cc)
        a_b, a_c, a_h, a_g = (_dot_w(h, w_ref[:, j * HALF + c * cc:j * HALF + (c + 1) * cc]) for j in range(4))
        s = a_c * a_h
        if sample:
            t_in = lax.broadcasted_iota(jnp.int32, s.shape, 0) % DEC_SEQ
            old0 = st_ref[:, c * cc:(c + 1) * cc]
            old1 = st_ref[:, HALF + c * cc:HALF + (c + 1) * cc]
            p1 = jnp.where(t_in >= 1, pltpu.roll(s, 1, 0), 0.0) + _place_steps(tile, [(0, old1)])
            p2 = jnp.where(t_in >= 2, pltpu.roll(s, 2, 0), 0.0) + _place_steps(tile, [(0, old0), (1, old1)])
            s_ref[:, lanes] = s
        else:
            shift_scr[0:8, :] = s_ref[:, lanes]
            shift_scr[8:8 + tile, :] = s
            p1 = shift_scr[7:7 + tile, :]
            p2 = shift_scr[6:6 + tile, :]
            s_ref[:, lanes] = s[tile - 8:tile, :]
        conv = p2 * cw_ref[0:1, lanes] + p1 * cw_ref[1:2, lanes] + s * cw_ref[2:3, lanes]
        ya_ref[:, lanes] = (a_b * conv * _silu(a_g)).astype(BF16)


def _conv_a(x, g_pre, w0, conv_w, state=None):
    t = x.shape[0]
    sample = state is not None
    tile = t if sample else min(ROW_TILE, t)
    in_specs = [_rows(tile, D_MODEL), _resident((1, D_MODEL)), _cols(D_MODEL, 4 * HALF, 0), _resident((3, HALF))]
    args = [x, g_pre, w0, conv_w]
    scratch = []
    if sample:
        in_specs.append(_resident(state.shape))
        args.append(state)
        s_spec, s_shape = _rows(tile, HALF), (t, HALF)
    else:
        s_spec, s_shape = pl.BlockSpec((8, HALF), lambda i: (0, 0)), (8, HALF)
        scratch = [pltpu.VMEM((8 + tile, CONV_A_CHUNK), F32)]
    return pl.pallas_call(
        functools.partial(_conv_a_kernel, sample=sample),
        grid=(t // tile,),
        in_specs=in_specs,
        out_specs=[_rows(tile, HALF), s_spec, _rows(tile, D_MODEL)],
        out_shape=[jax.ShapeDtypeStruct((t, HALF), BF16), jax.ShapeDtypeStruct(s_shape, F32),
                   jax.ShapeDtypeStruct((t, D_MODEL), BF16)],
        scratch_shapes=scratch,
        compiler_params=_params(),
        name="conv_a_sample" if sample else "conv_a_prompt",
    )(*args)


def _rel_bucket(dist):
    max_exact = NUM_BUCKETS // 2
    d = jnp.maximum(dist, 0)
    ratio = jnp.maximum(d, max_exact).astype(F32) / max_exact
    large = max_exact + (jnp.log(ratio) / math.log(MAX_DISTANCE / max_exact)
                         * (NUM_BUCKETS - max_exact)).astype(jnp.int32)
    return jnp.where(d < max_exact, d, jnp.minimum(large, NUM_BUCKETS - 1))


def _attn_softmax_pv(s, sink, v_bf, v_transposed=False):
    m = jnp.maximum(jnp.max(s, axis=-1, keepdims=True), sink)
    p = jnp.exp(s - m)
    den = jnp.sum(p, axis=-1, keepdims=True) + jnp.exp(sink - m)
    pv = _dot_nt(p.astype(BF16), v_bf) if v_transposed else _dot(p.astype(BF16), v_bf)
    return pv / den


def _attn_prompt_kernel(h_ref, wq_ref, wkv_ref, wg0_ref, wg1_ref, tab_ref, sink_ref, yb_ref, kwin_ref,
                        vwin_ref, q_scr, gate_scr, k_scr, v_scr, bias_scr):
    tile = h_ref.shape[0]
    i = pl.program_id(0)
    kv_w = N_KV * HEAD_DIM
    h = h_ref[...]

    kw, vw = 2 * HEAD_DIM, 4 * HEAD_DIM

    @pl.when(i == 0)
    def _():
        k_scr[0:WINDOW, :] = jnp.zeros((WINDOW, N_KV * kw), BF16)
        v_scr[0:WINDOW, :] = jnp.zeros((WINDOW, N_KV * vw), BF16)
        for hk in range(N_KV):
            v_scr[:, hk * vw + kw:(hk + 1) * vw] = jnp.ones((tile + WINDOW, kw), BF16)
        in_own = lax.broadcasted_iota(jnp.int32, (WINDOW, 2 * WINDOW), 1) >= WINDOW
        for head in range(N_HEADS):
            row = jnp.broadcast_to(tab_ref[head:head + 1, :], (WINDOW, BIAS_SPAN))
            band = pltpu.roll(row, 0, 1, stride=1, stride_axis=0)[:, 0:2 * WINDOW]
            rows = slice((head % 2) * WINDOW, (head % 2 + 1) * WINDOW)
            bias_scr[1, head // 2, rows, :] = band
            bias_scr[0, head // 2, rows, :] = jnp.where(in_own, band, MASK_VALUE)

    q_scr[...] = (_dot_w(h, wq_ref[...]) * (HEAD_DIM ** -0.5)).astype(BF16)
    k = _dot_w(h, wkv_ref[:, 0:kv_w])
    v = _dot_w(h, wkv_ref[:, kv_w:2 * kv_w])
    gate_scr[:, 0:HALF // 2] = _silu(_dot_w(h, wg0_ref[...]))
    gate_scr[:, HALF // 2:HALF] = _silu(_dot_w(h, wg1_ref[...]))
    for hk in range(N_KV):
        k_h = k[:, hk * HEAD_DIM:(hk + 1) * HEAD_DIM].astype(BF16)
        v_h = v[:, hk * HEAD_DIM:(hk + 1) * HEAD_DIM].astype(BF16)
        k_scr[WINDOW:WINDOW + tile, hk * kw:(hk + 1) * kw] = jnp.concatenate([k_h, k_h], axis=1)
        v_scr[WINDOW:WINDOW + tile, hk * vw:hk * vw + kw] = jnp.concatenate([v_h, v_h], axis=1)
    kwin_ref[...] = k[tile - WINDOW:tile, :]
    vwin_ref[...] = v[tile - WINDOW:tile, :]

    lane = lax.broadcasted_iota(jnp.int32, (WINDOW, kw), 1)
    lo = lane < HEAD_DIM
    keep_a = jnp.where(lo, 1.0, 0.0).astype(BF16)
    keep_b = jnp.where(lo, 0.0, 1.0).astype(BF16)
    is_a = lax.broadcasted_iota(jnp.int32, (2 * WINDOW, 1), 0) < WINDOW

    def block(n, carry):
        r0 = pl.multiple_of(n * WINDOW, WINDOW)
        rows = pl.ds(r0, WINDOW)
        keys = pl.ds(r0, 2 * WINDOW)
        first = jnp.where(jnp.logical_and(i == 0, n == 0), 0, 1)
        for hk in range(N_KV):
            for gp in range(GROUP // 2):
                a = hk * GROUP + 2 * gp
                slab = slice(a * HEAD_DIM, (a + 2) * HEAD_DIM)
                q2 = q_scr[rows, slab]
                lhs = jnp.concatenate([q2 * keep_a, q2 * keep_b], axis=0)
                s = _dot_nt(lhs, k_scr[keys, hk * kw:(hk + 1) * kw]) + bias_scr[first, a // 2]
                sink = jnp.where(is_a, sink_ref[a], sink_ref[a + 1])
                m = jnp.maximum(jnp.max(s, axis=-1, keepdims=True), sink)
                p = jnp.exp(s - m).astype(BF16)
                pv = _dot(p, v_scr[keys, hk * vw:(hk + 1) * vw])
                num = jnp.where(lo, pv[0:WINDOW, 0:kw], pv[WINDOW:2 * WINDOW, 0:kw])
                den = jnp.where(lo, pv[0:WINDOW, kw:2 * kw], pv[WINDOW:2 * WINDOW, kw:2 * kw])
                m_slab = jnp.where(lo, m[0:WINDOW], m[WINDOW:2 * WINDOW])
                den = den + jnp.exp(jnp.where(lo, sink_ref[a], sink_ref[a + 1]) - m_slab)
                yb_ref[rows, slab] = (num / den * gate_scr[rows, slab]).astype(BF16)
        return carry

    lax.fori_loop(0, tile // WINDOW, block, 0, unroll=True)
    k_scr[0:WINDOW, :] = k_scr[tile:tile + WINDOW, :]
    v_scr[0:WINDOW, :] = v_scr[tile:tile + WINDOW, :]


BIAS_SPAN = 3 * WINDOW


def _prompt_bias_table(rel_bias):
    dist = WINDOW - jnp.arange(BIAS_SPAN)
    table = jnp.where(((dist >= 0) & (dist < WINDOW))[:, None], rel_bias.astype(F32)[_rel_bucket(dist)], MASK_VALUE)
    return table.T


def _attn_prompt(h, w0, table, sinks):
    t = h.shape[0]
    tile = min(ROW_TILE, t)
    kv_w = N_KV * HEAD_DIM
    win_spec = pl.BlockSpec((WINDOW, kv_w), lambda i: (0, 0))
    return pl.pallas_call(
        _attn_prompt_kernel,
        grid=(t // tile,),
        in_specs=[_rows(tile, D_MODEL),
                  _cols(D_MODEL, HALF, 4), _cols(D_MODEL, 2 * kv_w, 10),
                  _cols(D_MODEL, HALF // 2, 11), _cols(D_MODEL, HALF // 2, 12),
                  _resident(table.shape), pl.BlockSpec(memory_space=pltpu.SMEM)],
        out_specs=[_rows(tile, HALF), win_spec, win_spec],
        out_shape=[jax.ShapeDtypeStruct((t, HALF), BF16),
                   jax.ShapeDtypeStruct((WINDOW, kv_w), F32), jax.ShapeDtypeStruct((WINDOW, kv_w), F32)],
        scratch_shapes=[pltpu.VMEM((tile, HALF), BF16), pltpu.VMEM((tile, HALF), F32),
                        pltpu.VMEM((tile + WINDOW, 2 * kv_w), BF16), pltpu.VMEM((tile + WINDOW, 4 * kv_w), BF16),
                        pltpu.VMEM((2, N_HEADS // 2, 2 * WINDOW, 2 * WINDOW), F32)],
        compiler_params=_params(),
        name="attn_prompt",
    )(h, w0, w0, w0, w0, table, sinks)


def _attn_proj_kernel(h_ref, wq_ref, wkv_ref, wg0_ref, wg1_ref, qg_ref, kt_ref, vt_ref, kv_scr):
    kv_w = N_KV * HEAD_DIM
    h = h_ref[...]
    q = _dot_w(h, wq_ref[...]) * (HEAD_DIM ** -0.5)
    for hk in range(N_KV):
        for g in range(GROUP):
            src = (hk * GROUP + g) * HEAD_DIM
            dst = (g * N_KV + hk) * HEAD_DIM
            qg_ref[:, dst:dst + HEAD_DIM] = q[:, src:src + HEAD_DIM]
    qg_ref[:, HALF:HALF + HALF // 2] = _dot_w(h, wg0_ref[...])
    qg_ref[:, HALF + HALF // 2:2 * HALF] = _dot_w(h, wg1_ref[...])
    kv_scr[...] = _dot_w(h, wkv_ref[...])
    for j in range(kt_ref.shape[0]):
        kt_ref[j] = kv_scr[j * WINDOW:(j + 1) * WINDOW, 0:kv_w].T
        vt_ref[j] = kv_scr[j * WINDOW:(j + 1) * WINDOW, kv_w:2 * kv_w].T


def _attn_proj(h, w0):
    t = h.shape[0]
    kv_w = N_KV * HEAD_DIM
    out_shape = [jax.ShapeDtypeStruct((t, 2 * HALF), F32), jax.ShapeDtypeStruct((t // WINDOW, kv_w, WINDOW), F32),
                 jax.ShapeDtypeStruct((t // WINDOW, kv_w, WINDOW), F32)]
    return pl.pallas_call(
        _attn_proj_kernel,
        grid=(1,),
        in_specs=[_resident((t, D_MODEL)),
                  _cols(D_MODEL, HALF, 4), _cols(D_MODEL, 2 * kv_w, 10),
                  _cols(D_MODEL, HALF // 2, 11), _cols(D_MODEL, HALF // 2, 12)],
        out_specs=[_resident(s.shape) for s in out_shape],
        out_shape=out_shape,
        scratch_shapes=[pltpu.VMEM((t, 2 * kv_w), F32)],
        compiler_params=_params(),
        name="attn_proj_sample",
    )(h, w0, w0, w0, w0)


ATTN_S_BATCH = 16
KEYS_PAD = 2 * WINDOW


def _attn_sample_kernel(qg_ref, ktn_ref, vtn_ref, ck_ref, cv_ref, bias_ref, sink_ref, yb_ref, nk_ref, nv_ref):
    kv_w = N_KV * HEAD_DIM
    row8 = lax.broadcasted_iota(jnp.int32, (8, kv_w), 0)
    lane_head = lax.broadcasted_iota(jnp.int32, (8, kv_w), 1) // HEAD_DIM
    lower = row8 < DEC_SEQ
    pick = [jnp.where(lane_head == 2 * hp + jnp.where(lower, 0, 1), 1.0, 0.0).astype(F32) for hp in range(2)]
    lower_w = lax.broadcasted_iota(jnp.int32, (8, HALF), 0) < DEC_SEQ
    kept = lax.broadcasted_iota(jnp.int32, (kv_w, WINDOW), 1) < WINDOW - DEC_SEQ
    seq0 = pl.program_id(0) * ATTN_S_BATCH
    per_tile = WINDOW // DEC_SEQ

    def slide(old, new_tile, shift):
        return jnp.where(kept, pltpu.roll(old, WINDOW - DEC_SEQ, 1), pltpu.roll(new_tile, shift, 1))

    def pair(p, carry):
        r0 = pl.multiple_of(p * 8, 8)
        rows = qg_ref[pl.ds(r0, 8), :]
        q8 = rows[:, 0:HALF]
        gate8 = rows[:, HALF:2 * HALF]
        out8 = []
        for sub in range(2):
            b = 2 * p + sub
            q_swap = pltpu.roll(q8, 4, 0)
            q_dup = jnp.where(lower_w, q8, q_swap) if sub == 0 else jnp.where(lower_w, q_swap, q8)
            tile = (seq0 + b) // per_tile
            shift = (2 * WINDOW - DEC_SEQ - DEC_SEQ * ((seq0 + b) % per_tile)) % WINDOW
            k_old = ck_ref[b].reshape(kv_w, WINDOW)
            v_old = cv_ref[b].reshape(kv_w, WINDOW)
            k_win = slide(k_old, ktn_ref[tile], shift)
            v_win = slide(v_old, vtn_ref[tile], shift)
            nk_ref[b] = k_win.reshape(N_KV, HEAD_DIM, WINDOW)
            nv_ref[b] = v_win.reshape(N_KV, HEAD_DIM, WINDOW)
            k_all = jnp.concatenate([k_old, k_win], axis=1).astype(BF16)
            v_all = jnp.concatenate([v_old, v_win], axis=1).astype(BF16)
            q_bd = jnp.concatenate(
                [q_dup[:, g * kv_w:(g + 1) * kv_w] * pick[hp] for g in range(GROUP) for hp in range(2)], axis=0)
            s = _dot(q_bd.astype(BF16), k_all) + bias_ref[...]
            o = _attn_softmax_pv(s, sink_ref[:, 0:1], v_all, v_transposed=True)
            out_g = []
            for g in range(GROUP):
                acc = None
                for hp in range(2):
                    piece = o[(2 * g + hp) * 8:(2 * g + hp + 1) * 8, :] * pick[hp]
                    piece = piece + pltpu.roll(piece, 4, 0)
                    acc = piece if acc is None else acc + piece
                out_g.append(acc)
            out8.append(jnp.concatenate(
                [out_g[g][:, hk * HEAD_DIM:(hk + 1) * HEAD_DIM] for hk in range(N_KV) for g in range(GROUP)], axis=1))
        o8 = jnp.where(lower_w, out8[0], out8[1])
        yb_ref[pl.ds(r0, 8), :] = (o8 * _silu(gate8)).astype(BF16)
        return carry

    lax.fori_loop(0, ATTN_S_BATCH // 2, pair, 0, unroll=2)


def _sample_bias(rel_bias, sinks):
    t = jnp.arange(DEC_SEQ)[:, None]
    j = jnp.arange(KEYS_PAD)[None, :]
    pos = jnp.where(j < WINDOW, j, j - (KEYS_PAD - DEC_SEQ) + WINDOW)
    dist = t + WINDOW - pos
    valid = (dist >= 0) & (dist < WINDOW) & ((j < WINDOW) | (j >= KEYS_PAD - DEC_SEQ))
    bias = jnp.where(valid[:, :, None], rel_bias.astype(F32)[_rel_bucket(dist)], MASK_VALUE)
    bias = bias.reshape(DEC_SEQ, KEYS_PAD, N_KV, GROUP).transpose(3, 2, 0, 1).reshape(N_HEADS * DEC_SEQ, KEYS_PAD)
    sink = jnp.broadcast_to(sinks.astype(F32).reshape(N_KV, GROUP).T[:, :, None], (GROUP, N_KV, DEC_SEQ))
    return bias, jnp.broadcast_to(sink.reshape(N_HEADS * DEC_SEQ, 1), (N_HEADS * DEC_SEQ, 128))


def _attn_sample(qg, kt_new, vt_new, cache_kt, cache_vt, bias, sink):
    n_seq = cache_kt.shape[0]
    bb = ATTN_S_BATCH
    cache_spec = pl.BlockSpec((bb, N_KV, HEAD_DIM, WINDOW), lambda i: (i, 0, 0, 0))
    return pl.pallas_call(
        _attn_sample_kernel,
        grid=(n_seq // bb,),
        in_specs=[_rows(bb * DEC_SEQ, 2 * HALF), _resident(kt_new.shape), _resident(vt_new.shape),
                  cache_spec, cache_spec, _resident(bias.shape), _resident(sink.shape)],
        out_specs=[_rows(bb * DEC_SEQ, HALF), cache_spec, cache_spec],
        out_shape=[jax.ShapeDtypeStruct((n_seq * DEC_SEQ, HALF), BF16),
                   jax.ShapeDtypeStruct(cache_kt.shape, F32), jax.ShapeDtypeStruct(cache_vt.shape, F32)],
        compiler_params=_params(),
        name="attn_sample",
    )(qg, kt_new, vt_new, cache_kt, cache_vt, bias, sink)


def _prep_layer0(g_pre, w_in, conv_w, rel_bias, sinks, w_out, g_post):
    return dict(
        g_pre=g_pre.reshape(1, D_MODEL), w0=w_in, conv_w=conv_w, rel_bias=rel_bias, sinks=sinks,
        w_out=w_out, g_post=g_post.reshape(1, D_MODEL))


def _layer0(x_p, x_s, conv_state, cache_k, cache_v, w):
    n_seq = x_s.shape[0]
    rows = x_s.reshape(n_seq * DEC_SEQ, D_MODEL)
    tile = rows.shape[0]
    n_p = x_p.shape[0] // tile
    ya_p, s_tail, h_p = _conv_a(x_p, w['g_pre'], w['w0'], w['conv_w'])
    yb_p, kwin, vwin = _attn_prompt(h_p, w['w0'], _prompt_bias_table(w['rel_bias']), w['sinks'])
    ya_s, s_s, h_s = _conv_a(rows, w['g_pre'], w['w0'], w['conv_w'], conv_state.reshape(n_seq, 2 * HALF))
    qg, kt_new, vt_new = _attn_proj(h_s, w['w0'])
    bias, sink = _sample_bias(w['rel_bias'], w['sinks'])
    yb_s, new_kt, new_vt = _attn_sample(qg, kt_new, vt_new, cache_k.transpose(0, 2, 3, 1),
                                        cache_v.transpose(0, 2, 3, 1), bias, sink)
    (y_all,) = _out_proj((ya_p, ya_s), (yb_p, yb_s), (x_p, rows), w['w_out'], w['g_post'], n_p, tile,
                         out_pair=False)
    prompt_state = (s_tail[6:8], kwin.reshape(WINDOW, N_KV, HEAD_DIM), vwin.reshape(WINDOW, N_KV, HEAD_DIM))
    sample_state = (s_s.reshape(n_seq, DEC_SEQ, HALF)[:, DEC_SEQ - 2:], new_kt.transpose(0, 3, 1, 2),
                    new_vt.transpose(0, 3, 1, 2))
    return y_all, prompt_state, sample_state


def _layer_norm(v, g, b):
    xc = v - jnp.mean(v, axis=-1, keepdims=True)
    return xc * lax.rsqrt(jnp.mean(xc * xc, axis=-1, keepdims=True) + NORM_EPS) * g + b


def _cmlp_prompt_kernel(x_ref, g_ref, w_ref, lng_ref, lnb_ref, ws_ref, bs_ref, yc_ref, h_ref, vn_scr):
    tile = x_ref.shape[0]
    h = _rms_bf16(x_ref[...], g_ref[...])
    h_ref[...] = h
    v = _dot_wt(h, w_ref[HALF:2 * HALF, :])
    vn_scr[...] = _layer_norm(v, lng_ref[...], lnb_ref[...]).astype(BF16)
    gw = HALF // CMLP_GROUPS
    cols = 2 * gw
    for cb in range(HALF // cols):
        u = _dot_wt(h, w_ref[cb * cols:(cb + 1) * cols, :])
        gate = _silu(_dot_wt(h, w_ref[2 * HALF + cb * cols:2 * HALF + (cb + 1) * cols, :]))
        for gi in range(2):
            grp = 2 * cb + gi
            lanes = slice(grp * gw, (grp + 1) * gw)
            for n in range(tile // CHUNK):
                rows = slice(n * CHUNK, (n + 1) * CHUNK)
                mixed = _dot(ws_ref[grp], vn_scr[rows, lanes]) + bs_ref[grp]
                yc_ref[rows, lanes] = (u[rows, gi * gw:(gi + 1) * gw] * mixed
                                       * gate[rows, gi * gw:(gi + 1) * gw]).astype(BF16)


def _cmlp_kernel(x_ref, g_ref, w_ref, lng_ref, lnb_ref, ws_ref, bs_ref, coef_ref, bias_ref, yc_ref, h_ref, vns_ref,
                 vn_scr, *, n_p):
    is_sample = pl.program_id(0) == n_p

    @pl.when(jnp.logical_not(is_sample))
    def _():
        _cmlp_prompt_kernel(x_ref, g_ref, w_ref, lng_ref, lnb_ref, ws_ref, bs_ref, yc_ref, h_ref, vn_scr)

    @pl.when(is_sample)
    def _():
        _cmlp_sample_kernel(x_ref, g_ref, w_ref, lng_ref, lnb_ref, coef_ref, bias_ref, yc_ref, vns_ref, h_ref)


def _cmlp(x_all, n_p, tile, w):
    t = x_all.shape[0]
    consts = [w['ln_g'], w['ln_b'], w['ws_tril'], w['bs_rows'], w['coef'], w['bias4']]
    return pl.pallas_call(
        functools.partial(_cmlp_kernel, n_p=n_p),
        grid=(n_p + 1,),
        in_specs=[_rows(tile, D_MODEL), _resident((1, D_MODEL)), _rowwin(3 * HALF, D_MODEL, 0)]
                 + [_resident(c.shape) for c in consts],
        out_specs=[_rows(tile, HALF), _rows(tile, D_MODEL), _resident((tile, HALF))],
        out_shape=[jax.ShapeDtypeStruct((t, HALF), BF16), jax.ShapeDtypeStruct((t, D_MODEL), BF16),
                   jax.ShapeDtypeStruct((tile, HALF), F32)],
        scratch_shapes=[pltpu.VMEM((tile, HALF), BF16)],
        compiler_params=_params(),
        name="cmlp",
    )(x_all, w['g_pre'], w['w1'], *consts)


def _cmlp_sample_kernel(x_ref, g_ref, w_ref, lng_ref, lnb_ref, coef_ref, bias_ref, yc_ref, vn_ref, h_ref):
    t = x_ref.shape[0]
    h = _rms_bf16(x_ref[...], g_ref[...])
    h_ref[...] = h
    u = _dot_wt(h, w_ref[0:HALF, :])
    vn = _layer_norm(_dot_wt(h, w_ref[HALF:2 * HALF, :]), lng_ref[...], lnb_ref[...])
    gate = _silu(_dot_wt(h, w_ref[2 * HALF:3 * HALF, :]))
    vn_ref[...] = vn

    def tiled(a):
        return a.reshape(t // 8, 8, HALF)

    mixed = tiled(vn) * coef_ref[0][None] + bias_ref[...][None]
    for k in range(1, DEC_SEQ):
        mixed = mixed + tiled(pltpu.roll(vn, k, 0)) * coef_ref[k][None]
    yc_ref[...] = (u * mixed.reshape(t, HALF) * gate).astype(BF16)


HEAD_LANES = 128
SSD_GW = HALF // SSD_GROUPS


def _softplus(x):
    return jnp.maximum(x, 0.0) + jnp.log1p(jnp.exp(-jnp.abs(x)))


def _dt_proj(h, wdt_ref):
    pad = jnp.zeros((HEAD_LANES - SSD_HEADS, D_MODEL), F32)
    return _dot_wt(h, jnp.concatenate([wdt_ref[...], pad], axis=0))


def _group_norm_gate(y, z, gn):
    gated = y * _silu(z)
    parts = []
    for g in range(SSD_GROUPS):
        part = gated[:, g * SSD_GW:(g + 1) * SSD_GW]
        parts.append(part * lax.rsqrt(jnp.mean(part * part, axis=-1, keepdims=True) + NORM_EPS))
    return (jnp.concatenate(parts, axis=1) * gn).astype(BF16)


def _ssd_prompt_kernel(h_ref, wz_ref, wx0_ref, wx1_ref, wx2_ref, wdt_ref, cw_ref, cb_ref, dtb_ref, alog_ref,
                       dskip_ref, gn_ref, e3_ref, tril3_ref, yd_ref, tail_ref, ssm_ref,
                       xbc_scr, z_scr, dt_scr, ht_scr, shift_scr):
    tile = h_ref.shape[0]
    i = pl.program_id(0)
    cd = SSD_CONV_DIM
    h = h_ref[...]

    @pl.when(i == 0)
    def _():
        tail_ref[...] = jnp.zeros_like(tail_ref)
        ht_scr[...] = jnp.zeros_like(ht_scr)

    z_scr[...] = _dot_wt(h, wz_ref[...])
    dt_scr[...] = _softplus(_dt_proj(h, wdt_ref) + dtb_ref[...])
    third = cd // 3
    for j, wx_ref in enumerate((wx0_ref, wx1_ref, wx2_ref)):
        cols = slice(j * third, (j + 1) * third)
        raw = _dot_wt(h, wx_ref[...])
        shift_scr[0:8, :] = tail_ref[:, cols]
        shift_scr[8:8 + tile, :] = raw
        conv = raw * cw_ref[3:4, cols] + cb_ref[:, cols]
        for k in range(1, 4):
            conv = conv + shift_scr[8 - k:8 - k + tile, :] * cw_ref[3 - k:4 - k, cols]
        xbc_scr[:, cols] = _silu(conv)
        tail_ref[:, cols] = raw[tile - 8:tile, :]

    a16 = -jnp.exp(alog_ref[...])
    causal =(lax.broadcasted_iota(jnp.int32, (CHUNK, CHUNK), 0)
              >= lax.broadcasted_iota(jnp.int32, (CHUNK, CHUNK), 1))
    first_half = lax.broadcasted_iota(jnp.int32, (CHUNK, 2 * HEAD_DIM), 1) < HEAD_DIM
    keep_a = jnp.where(first_half, 1.0, 0.0).astype(BF16)
    keep_b = jnp.where(first_half, 0.0, 1.0).astype(BF16)

    def chunk(n, carry):
        r0 = pl.multiple_of(n * CHUNK, CHUNK)
        rows = pl.ds(r0, CHUNK)
        xs = xbc_scr[rows, 0:HALF]
        dt16 = dt_scr[rows, :]
        dt_e = _dot(jnp.concatenate(_split3(dt16), axis=1), e3_ref[...])
        acs16 = _dot(tril3_ref[...], jnp.concatenate(_split3(dt16 * a16), axis=0))
        acs_e = _dot(jnp.concatenate(_split3(acs16), axis=1), e3_ref[...])
        acs_t = acs16.T
        last_e = acs_e[CHUNK - 1:CHUNK, :]
        xdt = xs * dt_e
        xdt_bf = xdt.astype(BF16)
        xw = (jnp.exp(last_e - acs_e) * xdt).astype(BF16)
        dec_e = jnp.exp(last_e)
        y_parts = []
        yoff_parts = []
        for g in range(SSD_GROUPS):
            c_g = xbc_scr[rows, HALF + 2 * SSD_STATE + g * SSD_STATE:HALF + 2 * SSD_STATE + (g + 1) * SSD_STATE].astype(BF16)
            b_g = xbc_scr[rows, HALF + g * SSD_STATE:HALF + (g + 1) * SSD_STATE].astype(BF16)
            cb = _dot_nt(c_g, b_g)
            h_prev = ht_scr[g]
            yoff_parts.append(_dot(c_g, h_prev.astype(BF16)))
            for r in range(0, SSD_HEADS // SSD_GROUPS, 2):
                wgt = []
                for hd in (g * (SSD_HEADS // SSD_GROUPS) + r, g * (SSD_HEADS // SSD_GROUPS) + r + 1):
                    seg = acs16[:, hd:hd + 1] - acs_t[hd:hd + 1, :]
                    wgt.append(cb * jnp.exp(jnp.where(causal, seg, -jnp.inf)))
                a = g * (SSD_HEADS // SSD_GROUPS) + r
                slab = xdt_bf[:, a * HEAD_DIM:(a + 2) * HEAD_DIM]
                rhs = jnp.concatenate([slab * keep_a, slab * keep_b], axis=0)
                y_parts.append(_dot(jnp.concatenate(wgt, axis=1).astype(BF16), rhs))
            lanes = slice(g * SSD_GW, (g + 1) * SSD_GW)
            ht_scr[g] = h_prev * dec_e[:, lanes] + _dot_tn(b_g, xw[:, lanes])
        y = (jnp.concatenate(y_parts, axis=1) + jnp.concatenate(yoff_parts, axis=1) * jnp.exp(acs_e)
             + dskip_ref[...] * xs)
        yd_ref[rows, :] = _group_norm_gate(y, z_scr[rows, :], gn_ref[...])
        return carry

    lax.fori_loop(0, tile // CHUNK, chunk, 0, unroll=True)

    @pl.when(i == pl.num_programs(0) - 1)
    def _():
        for g in range(SSD_GROUPS):
            ssm_ref[g * SSD_GW:(g + 1) * SSD_GW, :] = ht_scr[g].T


def _ssd_weight_specs():
    third = SSD_CONV_DIM // 3
    first = 4 * HALF // third
    return ([_rowwin(HALF, D_MODEL, 3)] + [_rowwin(third, D_MODEL, first + j) for j in range(3)]
            + [_rowwin(SSD_HEADS, D_MODEL, (4 * HALF + SSD_CONV_DIM) // SSD_HEADS)])


def _ssd_prompt(h, n_p, tile, w):
    t = n_p * tile
    cd = SSD_CONV_DIM
    consts = [w['conv_w'], w['conv_b'], w['dt_bias16'], w['a_log16'],
              w['d_skip_e'], w['gate_norm_g'], w['expand3'], w['tril3']]
    return pl.pallas_call(
        _ssd_prompt_kernel,
        grid=(t // tile,),
        in_specs=[_rows(tile, D_MODEL)] + _ssd_weight_specs() + [_resident(c.shape) for c in consts],
        out_specs=[_rows(tile, HALF), pl.BlockSpec((8, cd), lambda i: (0, 0)),
                   pl.BlockSpec((HALF, SSD_STATE), lambda i: (0, 0))],
        out_shape=[jax.ShapeDtypeStruct((t, HALF), BF16), jax.ShapeDtypeStruct((8, cd), F32),
                   jax.ShapeDtypeStruct((HALF, SSD_STATE), F32)],
        scratch_shapes=[pltpu.VMEM((tile, cd), F32), pltpu.VMEM((tile, HALF), F32),
                        pltpu.VMEM((tile, HEAD_LANES), F32), pltpu.VMEM((SSD_GROUPS, SSD_STATE, SSD_GW), F32),
                        pltpu.VMEM((8 + tile, cd // 3), F32)],
        compiler_params=_params(),
        name="ssd_prompt",
    )(h, w['w1'], w['w1'], w['w1'], w['w1'], w['w1'], *consts)


def _ssd_sample_pre_kernel(h_ref, wz_ref, wx0_ref, wx1_ref, wx2_ref, wdt_ref, cw_ref, cb_ref, st_ref,
                           dtb_ref, aloge_ref, dskip_ref, e3_ref, seg_ref,
                           nconv_ref, z_ref, ysk_ref, eacs_ref, xw_ref, dec_ref, b_ref, c_ref, raw_scr):
    t = h_ref.shape[0]
    n_seq = t // DEC_SEQ
    h = h_ref[...]
    z_ref[...] = _dot_wt(h, wz_ref[...])
    raw = jnp.concatenate([_dot_wt(h, wx0_ref[...]), _dot_wt(h, wx1_ref[...]), _dot_wt(h, wx2_ref[...])], axis=1)
    for c in range(raw_scr.shape[0]):
        lanes = slice(c * 128, (c + 1) * 128)
        raw_scr[c] = raw[:, lanes]
        for j in range(3):
            nconv_ref[j, :, lanes] = raw_scr[c, pl.ds(j + 1, n_seq, stride=DEC_SEQ), :]
    dt16 = _softplus(_dt_proj(h, wdt_ref) + dtb_ref[...])
    dt = _dot(jnp.concatenate(_split3(dt16), axis=1), e3_ref[...])
    old = [st_ref[j] for j in range(3)]
    p1 = _place_steps(t, [(0, old[2])])
    p2 = _place_steps(t, [(0, old[1]), (1, old[2])])
    p3 = _place_steps(t, [(0, old[0]), (1, old[1]), (2, old[2])])

    def step_of(width):
        return lax.broadcasted_iota(jnp.int32, (t, width), 0) % DEC_SEQ

    def back(a, k):
        return jnp.where(step_of(a.shape[1]) >= k, pltpu.roll(a, k, 0), 0.0)

    def ahead(a, k):
        return jnp.where(step_of(a.shape[1]) + k < DEC_SEQ, pltpu.roll(a, t - k, 0), 0.0)

    conv = (raw * cw_ref[3:4, :] + (back(raw, 1) + p1) * cw_ref[2:3, :]
            + (back(raw, 2) + p2) * cw_ref[1:2, :] + (back(raw, 3) + p3) * cw_ref[0:1, :]
            + cb_ref[...])
    xbc = _silu(conv)
    xs = xbc[:, 0:HALF]
    bm = xbc[:, HALF:HALF + 2 * SSD_STATE]
    cm = xbc[:, HALF + 2 * SSD_STATE:]
    b_ref[...] = bm
    c_ref[...] = cm
    da = dt * (-jnp.exp(aloge_ref[...]))
    acs = da + back(da, 1) + back(da, 2) + back(da, 3)
    suffix = ahead(da, 1) + ahead(da, 2) + ahead(da, 3)
    xdt = xs * dt
    y = _dot((cm * bm).astype(BF16), seg_ref[...]) * xdt
    for k in range(1, DEC_SEQ):
        cbk = _dot((cm * pltpu.roll(bm, k, 0)).astype(BF16), seg_ref[...])
        term = cbk * jnp.exp(acs - pltpu.roll(acs, k, 0)) * pltpu.roll(xdt, k, 0)
        y = y + jnp.where(step_of(HALF) >= k, term, 0.0)
    ysk_ref[...] = y + dskip_ref[...] * xs
    eacs_ref[...] = jnp.exp(acs)
    xw_ref[...] = jnp.exp(suffix) * xdt
    dec_ref[...] = jnp.exp(acs + suffix)


def _ssd_sample_pre(h, first_row, conv_state, w):
    t = conv_state.shape[1] * DEC_SEQ
    cd = SSD_CONV_DIM
    tile = min(SSD_PRE_ROWS, t)
    first = first_row // tile
    seqs = tile // DEC_SEQ
    state_spec = pl.BlockSpec((3, seqs, cd), lambda i: (0, i, 0))
    head = [w['conv_w'], w['conv_b']]
    tail = [w['dt_bias16'], w['a_log_e'], w['d_skip_e'], w['expand3'], w['seg_expand']]
    args = [h, w['w1'], w['w1'], w['w1'], w['w1'], w['w1']] + head + [conv_state] + tail
    wide = jax.ShapeDtypeStruct((t, HALF), F32)
    narrow = jax.ShapeDtypeStruct((t, 2 * SSD_STATE), F32)
    out_shape = [jax.ShapeDtypeStruct(conv_state.shape, F32), wide, wide, wide, wide, wide, narrow, narrow]
    return pl.pallas_call(
        _ssd_sample_pre_kernel,
        grid=(t // tile,),
        in_specs=[pl.BlockSpec((tile, D_MODEL), lambda i: (first + i, 0))] + _ssd_weight_specs()
                 + [_resident(c.shape) for c in head] + [state_spec] + [_resident(c.shape) for c in tail],
        out_specs=[state_spec] + [_rows(tile, HALF)] * 5 + [_rows(tile, 2 * SSD_STATE)] * 2,
        out_shape=out_shape,
        scratch_shapes=[pltpu.VMEM((cd // 128, tile, 128), F32)],
        compiler_params=_params(),
        name="ssd_sample_pre",
    )(*args)


SSD_S_BATCH = 8
SSD_PRE_ROWS = 256


def _ssd_sample_state_kernel(st_ref, c_ref, b_ref, xw_ref, dec_ref, eacs_ref, ysk_ref, z_ref, gn_ref,
                             yd_ref, nst_ref):
    row_n = lax.broadcasted_iota(jnp.int32, (8, SSD_STATE), 0)
    row_w = lax.broadcasted_iota(jnp.int32, (8, SSD_GW), 0)
    row_f = lax.broadcasted_iota(jnp.int32, (8, HALF), 0)
    ones_rows = jnp.where((row_n >= 4) & (row_n < 7), 1.0, 0.0).astype(BF16)
    hpg = SSD_HEADS // SSD_GROUPS

    def pair(p, carry):
        r0 = pl.multiple_of(p * 8, 8)
        rows = pl.ds(r0, 8)
        c8 = c_ref[rows, :].astype(BF16)
        b8 = b_ref[rows, :]
        xw8 = xw_ref[rows, :]
        dec8 = dec_ref[rows, :]
        yoff = []
        for sub in range(2):
            b = 2 * p + sub
            xw_own = xw8 if sub == 0 else pltpu.roll(xw8, 4, 0)
            b_own = b8 if sub == 0 else pltpu.roll(b8, 4, 0)
            hi, mid, lo = (term.astype(F32) for term in _split3(dec8[4 * sub:4 * sub + 1, :]))
            parts = []
            for g in range(SSD_GROUPS):
                lanes = slice(g * SSD_GW, (g + 1) * SSD_GW)
                heads = pl.ds(g * hpg, hpg)
                h0 = st_ref[b, heads].reshape(SSD_GW, SSD_STATE)
                parts.append(_dot_nt(c8[:, g * SSD_STATE:(g + 1) * SSD_STATE], h0.astype(BF16)))
                lhs = jnp.where(row_w < 4, xw_own[:, lanes],
                                jnp.where(row_w == 4, hi[:, lanes],
                                          jnp.where(row_w == 5, mid[:, lanes],
                                                    jnp.where(row_w == 6, lo[:, lanes], 0.0)))).astype(BF16)
                rhs_b = jnp.where(row_n < 4, b_own[:, g * SSD_STATE:(g + 1) * SSD_STATE], 0.0).astype(BF16)
                decay = _dot_tn(lhs, ones_rows)
                nst_ref[b, heads] = (h0 * decay + _dot_tn(lhs, rhs_b)).reshape(hpg, HEAD_DIM, SSD_STATE)
            yoff.append(jnp.concatenate(parts, axis=1))
        yoff8 = jnp.where(row_f < 4, yoff[0], yoff[1])
        y = ysk_ref[rows, :] + yoff8 * eacs_ref[rows, :]
        yd_ref[rows, :] = _group_norm_gate(y, z_ref[rows, :], gn_ref[...])
        return carry

    lax.fori_loop(0, SSD_S_BATCH // 2, pair, 0, unroll=True)


def _ssd_sample_state(state, cm, bm, xw, dec, eacs, ysk, z, gn):
    n_seq = state.shape[0]
    bb = SSD_S_BATCH
    r = bb * DEC_SEQ
    st_spec = pl.BlockSpec((bb, SSD_HEADS, HEAD_DIM, SSD_STATE), lambda i: (i, 0, 0, 0))
    return pl.pallas_call(
        _ssd_sample_state_kernel,
        grid=(n_seq // bb,),
        in_specs=[st_spec, _rows(r, 2 * SSD_STATE), _rows(r, 2 * SSD_STATE)] + [_rows(r, HALF)] * 5
                 + [_resident((1, HALF))],
        out_specs=[_rows(r, HALF), st_spec],
        out_shape=[jax.ShapeDtypeStruct((n_seq * DEC_SEQ, HALF), BF16), jax.ShapeDtypeStruct(state.shape, F32)],
        compiler_params=_params(),
        name="ssd_sample_state",
    )(state, cm, bm, xw, dec, eacs, ysk, z, gn)


def _prep_layer1(g_pre, w_in, ln_g, ln_b, w_s, b_s, conv_w, conv_b, dt_bias, a_log, d_skip, gate_norm_g,
                 w_out, g_post):
    cd = SSD_CONV_DIM
    gw = HALF // CMLP_GROUPS
    w1 = w_in.T

    def lanes16(v):
        return jnp.pad(v.astype(F32), (0, HEAD_LANES - SSD_HEADS)).reshape(1, HEAD_LANES)

    def per_channel(v):
        return jnp.repeat(v.astype(F32), HEAD_DIM).reshape(1, HALF)

    head_of = np.arange(HALF) // HEAD_DIM
    expand = jnp.asarray(np.arange(HEAD_LANES)[:, None] == head_of[None, :], BF16)
    tril = jnp.asarray(np.tril(np.ones((CHUNK, CHUNK))), BF16)
    grp_rows = np.arange(2 * SSD_STATE) // SSD_STATE
    seg_expand = jnp.asarray(grp_rows[:, None] == (head_of // (SSD_HEADS // SSD_GROUPS))[None, :], BF16)

    w4 = jnp.tril(w_s[:, :DEC_SEQ, :DEC_SEQ])
    steps = jnp.arange(DEC_SEQ)
    coef = []
    for k in range(DEC_SEQ):
        src = steps - k
        ck = jnp.where((src >= 0)[None, :], w4[:, steps, jnp.maximum(src, 0)], 0.0)
        ck = jnp.repeat(ck.T, gw, axis=1)
        coef.append(jnp.concatenate([ck, ck], axis=0))
    bias4 = jnp.repeat(b_s[:, :DEC_SEQ].T, gw, axis=1)
    return dict(
        g_pre=g_pre.reshape(1, D_MODEL), w1=w1, ln_g=ln_g.reshape(1, HALF), ln_b=ln_b.reshape(1, HALF),
        ws_tril=jnp.tril(w_s).astype(BF16),
        bs_rows=jnp.broadcast_to(b_s.astype(F32)[:, :, None], (CMLP_GROUPS, CHUNK, gw)),
        coef=jnp.stack(coef).astype(F32), bias4=jnp.concatenate([bias4, bias4], axis=0).astype(F32),
        conv_w=conv_w, conv_b=conv_b.reshape(1, cd), dt_bias16=lanes16(dt_bias), a_log16=lanes16(a_log),
        a_log_e=per_channel(a_log), d_skip_e=per_channel(d_skip), gate_norm_g=gate_norm_g.reshape(1, HALF),
        expand3=jnp.concatenate([expand] * 3, axis=0), tril3=jnp.concatenate([tril] * 3, axis=1),
        seg_expand=seg_expand, w_out=w_out, g_post=g_post.reshape(1, D_MODEL))


def _layer1(x_all, n_p, conv_state, ssm_state, w):
    n_seq = ssm_state.shape[0]
    tile = n_seq * DEC_SEQ
    yc_all, h_all, vn = _cmlp(x_all, n_p, tile, w)
    yd_p, tail, ssm = _ssd_prompt(h_all, n_p, tile, w)
    new_conv, z, ysk, eacs, xw, dec, bm, cm = _ssd_sample_pre(h_all, n_p * tile, conv_state.transpose(1, 0, 2), w)
    yd_s, new_state = _ssd_sample_state(ssm_state, cm, bm, xw, dec, eacs, ysk, z, w['gate_norm_g'])
    y_p, y_s = _out_proj(yc_all, (yd_p, yd_s), x_all, w['w_out'], w['g_post'], n_p, tile, out_pair=True)
    prompt_out = (y_p, tail[5:8], ssm.reshape(SSD_HEADS, HEAD_DIM, SSD_STATE))
    sample_out = (y_s.reshape(n_seq, DEC_SEQ, D_MODEL), vn.reshape(n_seq, DEC_SEQ, HALF),
                  new_conv.transpose(1, 0, 2), new_state)
    return prompt_out, sample_out


def kernel(x_prompt, x_sample, state_conv_a, cache_win_k, cache_win_v, state_conv_d, state_ssm, rel_bias,
           l0_g_pre, l0_w_in, l0_conv_w, l0_sinks, l0_w_out, l0_g_post,
           l1_g_pre, l1_w_in, l1_ln_g, l1_ln_b, l1_w_s, l1_b_s, l1_conv_w, l1_conv_b, l1_dt_bias, l1_a_log,
           l1_d_skip, l1_gate_norm_g, l1_w_out, l1_g_post):
    w0 = _prep_layer0(l0_g_pre, l0_w_in, l0_conv_w, rel_bias, l0_sinks, l0_w_out, l0_g_post)
    w1 = _prep_layer1(l1_g_pre, l1_w_in, l1_ln_g, l1_ln_b, l1_w_s, l1_b_s, l1_conv_w, l1_conv_b, l1_dt_bias,
                      l1_a_log, l1_d_skip, l1_gate_norm_g, l1_w_out, l1_g_post)
    x_p = x_prompt[0]
    n_p = x_p.shape[0] // (x_sample.shape[0] * DEC_SEQ)
    y_all, (p_conv_a, p_win_k, p_win_v), (s_conv_a, s_win_k, s_win_v) = _layer0(
        x_p, x_sample, state_conv_a, cache_win_k, cache_win_v, w0)
    (yp, p_conv_d, p_ssm), (ys, s_chunk_v, s_conv_d, s_ssm) = _layer1(y_all, n_p, state_conv_d, state_ssm, w1)
    return (yp[None], ys, p_conv_a[None], s_conv_a, p_win_k[None], p_win_v[None], s_win_k, s_win_v, s_chunk_v,
            p_conv_d[None], s_conv_d, p_ssm[None], s_ssm)
```

```python
import functools
import math

import jax
import jax.numpy as jnp
import numpy as np
from jax import lax
from jax.experimental import pallas as pl
from jax.experimental.pallas import tpu as pltpu

F32 = jnp.float32
BF16 = jnp.bfloat16

D_MODEL = 2048
HALF = 1024
HEAD_DIM = 64
N_HEADS = 16
N_KV = 4
GROUP = 4
WINDOW = 128
NUM_BUCKETS = 32
MAX_DISTANCE = 128
CMLP_GROUPS = 8
CHUNK = 128
SSD_HEADS = 16
SSD_STATE = 128
SSD_GROUPS = 2
SSD_CONV_DIM = HALF + 2 * SSD_GROUPS * SSD_STATE
DEC_SEQ = 4
NORM_EPS = 1e-6
MASK_VALUE = -1e30

ROW_TILE = 512
VMEM_LIMIT = 56 * 1024 * 1024
CONV_A_VMEM_LIMIT = 60 * 1024 * 1024


def _params(n_axes=1):
    return pltpu.CompilerParams(dimension_semantics=("arbitrary",) * n_axes,
                                vmem_limit_bytes=VMEM_LIMIT)


def _resident(shape):
    nd = len(shape)
    return pl.BlockSpec(shape, lambda *_: (0,) * nd, pipeline_mode=pl.Buffered(1))


def _rows(tile, width):
    return pl.BlockSpec((tile, width), lambda i: (i, 0))


def _cols(rows, width, block):
    return pl.BlockSpec((rows, width), lambda *_: (0, block), pipeline_mode=pl.Buffered(1))


def _rowwin(height, cols, block):
    return pl.BlockSpec((height, cols), lambda *_: (block, 0), pipeline_mode=pl.Buffered(1))


def _rms_bf16(x, g):
    ms = jnp.mean(x * x, axis=-1, keepdims=True)
    return (x * lax.rsqrt(ms + NORM_EPS) * g).astype(BF16)


def _silu(x):
    return x * jax.nn.sigmoid(x)


def _dot(a, b):
    return jnp.dot(a, b, preferred_element_type=F32)


def _dot_nt(a, b):
    return lax.dot_general(a, b, (((1,), (1,)), ((), ())), preferred_element_type=F32)


def _dot_tn(a, b):
    return lax.dot_general(a, b, (((0,), (0,)), ((), ())), preferred_element_type=F32)


def _dot_w(a, w):
    return _dot(a, w.astype(BF16))


def _dot_wt(a, wt):
    return _dot_nt(a, wt.astype(BF16))


def _split3(x):
    hi = x.astype(BF16)
    r1 = x - hi.astype(F32)
    mid = r1.astype(BF16)
    lo = (r1 - mid.astype(F32)).astype(BF16)
    return hi, mid, lo


def _place_steps(t, placements):
    n_seq = placements[0][1].shape[0]
    row = lax.broadcasted_iota(jnp.int32, (t, n_seq), 0)
    seq = lax.broadcasted_iota(jnp.int32, (t, n_seq), 1)
    lhs, rhs = [], []
    for step, state in placements:
        sel = jnp.where(row == DEC_SEQ * seq + step, 1.0, 0.0).astype(BF16)
        lhs += [sel] * 3
        rhs += list(_split3(state))
    return _dot(jnp.concatenate(lhs, axis=1), jnp.concatenate(rhs, axis=0))


def _prompt_rows(tile, width, n_p):
    return pl.BlockSpec((tile, width), lambda i: (jnp.minimum(i, n_p - 1), 0))


def _group_specs(arg, tile, width, n_p):
    if isinstance(arg, tuple):
        return [_prompt_rows(tile, width, n_p), _resident((tile, width))]
    return [_rows(tile, width)]


def _out_proj_tile(ya_ref, yb_ref, x_ref, w_ref, g_ref, o_ref):
    y = _dot_w(ya_ref[...], w_ref[0:HALF, :]) + _dot_w(yb_ref[...], w_ref[HALF:2 * HALF, :])
    ms = jnp.mean(y * y, axis=-1, keepdims=True)
    o_ref[...] = x_ref[...] + y * lax.rsqrt(ms + NORM_EPS) * g_ref[...]


def _out_proj_kernel(*refs, n_p, n_ya, n_yb, n_x):
    refs = list(refs)
    ya, yb, x = refs[:n_ya], refs[n_ya:n_ya + n_yb], refs[n_ya + n_yb:n_ya + n_yb + n_x]
    w_ref, g_ref = refs[n_ya + n_yb + n_x:n_ya + n_yb + n_x + 2]
    outs = refs[n_ya + n_yb + n_x + 2:]
    is_sample = pl.program_id(0) == n_p

    @pl.when(jnp.logical_not(is_sample))
    def _():
        _out_proj_tile(ya[0], yb[0], x[0], w_ref, g_ref, outs[0])

    @pl.when(is_sample)
    def _():
        _out_proj_tile(ya[-1], yb[-1], x[-1], w_ref, g_ref, outs[-1])


def _out_proj(ya, yb, x, w, g, n_p, tile, out_pair):
    groups = [(ya, HALF), (yb, HALF), (x, D_MODEL)]
    in_specs, args = [], []
    for arg, width in groups:
        in_specs += _group_specs(arg, tile, width, n_p)
        args += list(arg) if isinstance(arg, tuple) else [arg]
    if out_pair:
        out_specs = [_prompt_rows(tile, D_MODEL, n_p), _resident((tile, D_MODEL))]
        out_shape = [jax.ShapeDtypeStruct((n_p * tile, D_MODEL), F32), jax.ShapeDtypeStruct((tile, D_MODEL), F32)]
    else:
        out_specs = [_rows(tile, D_MODEL)]
        out_shape = [jax.ShapeDtypeStruct(((n_p + 1) * tile, D_MODEL), F32)]
    n_of = [2 if isinstance(arg, tuple) else 1 for arg, _ in groups]
    return pl.pallas_call(
        functools.partial(_out_proj_kernel, n_p=n_p, n_ya=n_of[0], n_yb=n_of[1], n_x=n_of[2]),
        grid=(n_p + 1,),
        in_specs=in_specs + [_resident((2 * HALF, D_MODEL)), _resident((1, D_MODEL))],
        out_specs=out_specs,
        out_shape=out_shape,
        compiler_params=_params(),
        name="out_proj",
    )(*args, w, g)


CONV_A_CHUNK = 256


def _conv_a_kernel(*refs, sample):
    if sample:
        x_ref, g_ref, w_ref, cw_ref, st_ref, ya_ref, s_ref, h_ref = refs
    else:
        x_ref, g_ref, w_ref, cw_ref, ya_ref, s_ref, h_ref, shift_scr = refs
    tile = x_ref.shape[0]
    cc = CONV_A_CHUNK
    h = _rms_bf16(x_ref[...], g_ref[...])
    h_ref[...] = h
    if not sample:
        @pl.when(pl.program_id(0) == 0)
        def _():
            s_ref[...] = jnp.zeros_like(s_ref)
    for c in range(HALF // cc):
        lanes = slice(c * cc, (c + 1) * cc)
        a_b, a_c, a_h, a_g = (_dot_w(h, w_ref[:, j * HALF + c * cc:j * HALF + (c + 1) * cc]) for j in range(4))
        s = a_c * a_h
        if sample:
            t_in = lax.broadcasted_iota(jnp.int32, s.shape, 0) % DEC_SEQ
            old0 = st_ref[:, c * cc:(c + 1) * cc]
            old1 = st_ref[:, HALF + c * cc:HALF + (c + 1) * cc]
            p1 = jnp.where(t_in >= 1, pltpu.roll(s, 1, 0), 0.0) + _place_steps(tile, [(0, old1)])
            p2 = jnp.where(t_in >= 2, pltpu.roll(s, 2, 0), 0.0) + _place_steps(tile, [(0, old0), (1, old1)])
            s_ref[:, lanes] = s
        else:
            shift_scr[0:8, :] = s_ref[:, lanes]
            shift_scr[8:8 + tile, :] = s
            p1 = shift_scr[7:7 + tile, :]
            p2 = shift_scr[6:6 + tile, :]
            s_ref[:, lanes] = s[tile - 8:tile, :]
        conv = p2 * cw_ref[0:1, lanes] + p1 * cw_ref[1:2, lanes] + s * cw_ref[2:3, lanes]
        ya_ref[:, lanes] = (a_b * conv * _silu(a_g)).astype(BF16)


def _conv_a_rowspace_kernel(xp_ref, xs_ref, g_ref, w_ref, cw_ref, st_ref, ya_ref, tail_ref, ss_ref, h_ref, shift_scr,
                            *, n_p):
    is_sample = pl.program_id(0) == n_p

    @pl.when(jnp.logical_not(is_sample))
    def _():
        _conv_a_kernel(xp_ref, g_ref, w_ref, cw_ref, ya_ref, tail_ref, h_ref, shift_scr, sample=False)

    @pl.when(is_sample)
    def _():
        _conv_a_kernel(xs_ref, g_ref, w_ref, cw_ref, st_ref, ya_ref, ss_ref, h_ref, sample=True)


def _conv_a(x_p, x_s, g_pre, w0, conv_w, state):
    tile = x_s.shape[0]
    n_p = x_p.shape[0] // tile
    t = (n_p + 1) * tile
    return pl.pallas_call(
        functools.partial(_conv_a_rowspace_kernel, n_p=n_p),
        grid=(n_p + 1,),
        in_specs=[_prompt_rows(tile, D_MODEL, n_p), _resident((tile, D_MODEL)), _resident((1, D_MODEL)),
                  _cols(D_MODEL, 4 * HALF, 0), _resident((3, HALF)), _resident(state.shape)],
        out_specs=[_rows(tile, HALF), _resident((8, HALF)), _resident((tile, HALF)), _rows(tile, D_MODEL)],
        out_shape=[jax.ShapeDtypeStruct((t, HALF), BF16), jax.ShapeDtypeStruct((8, HALF), F32),
                   jax.ShapeDtypeStruct((tile, HALF), F32), jax.ShapeDtypeStruct((t, D_MODEL), BF16)],
        scratch_shapes=[pltpu.VMEM((8 + tile, CONV_A_CHUNK), F32)],
        compiler_params=pltpu.CompilerParams(dimension_semantics=("arbitrary",), vmem_limit_bytes=CONV_A_VMEM_LIMIT),
        name="conv_a",
    )(x_p, x_s, g_pre, w0, conv_w, state)


def _rel_bucket(dist):
    max_exact = NUM_BUCKETS // 2
    d = jnp.maximum(dist, 0)
    ratio = jnp.maximum(d, max_exact).astype(F32) / max_exact
    large = max_exact + (jnp.log(ratio) / math.log(MAX_DISTANCE / max_exact)
                         * (NUM_BUCKETS - max_exact)).astype(jnp.int32)
    return jnp.where(d < max_exact, d, jnp.minimum(large, NUM_BUCKETS - 1))


def _attn_softmax_pv(s, sink, v_bf, v_transposed=False):
    m = jnp.maximum(jnp.max(s, axis=-1, keepdims=True), sink)
    p = jnp.exp(s - m)
    den = jnp.sum(p, axis=-1, keepdims=True) + jnp.exp(sink - m)
    pv = _dot_nt(p.astype(BF16), v_bf) if v_transposed else _dot(p.astype(BF16), v_bf)
    return pv / den


def _attn_prompt_kernel(h_ref, wq_ref, wkv_ref, wg0_ref, wg1_ref, tab_ref, sink_ref, yb_ref, kwin_ref,
                        vwin_ref, q_scr, gate_scr, k_scr, v_scr, bias_scr):
    tile = h_ref.shape[0]
    i = pl.program_id(0)
    kv_w = N_KV * HEAD_DIM
    h = h_ref[...]

    kw, vw = 2 * HEAD_DIM, 4 * HEAD_DIM

    @pl.when(i == 0)
    def _():
        k_scr[0:WINDOW, :] = jnp.zeros((WINDOW, N_KV * kw), BF16)
        v_scr[0:WINDOW, :] = jnp.zeros((WINDOW, N_KV * vw), BF16)
        for hk in range(N_KV):
            v_scr[:, hk * vw + kw:(hk + 1) * vw] = jnp.ones((tile + WINDOW, kw), BF16)
        in_own = lax.broadcasted_iota(jnp.int32, (WINDOW, 2 * WINDOW), 1) >= WINDOW
        for head in range(N_HEADS):
            row = jnp.broadcast_to(tab_ref[head:head + 1, :], (WINDOW, BIAS_SPAN))
            band = pltpu.roll(row, 0, 1, stride=1, stride_axis=0)[:, 0:2 * WINDOW]
            rows = slice((head % 2) * WINDOW, (head % 2 + 1) * WINDOW)
            bias_scr[1, head // 2, rows, :] = band
            bias_scr[0, head // 2, rows, :] = jnp.where(in_own, band, MASK_VALUE)

    q_scr[...] = (_dot_w(h, wq_ref[...]) * (HEAD_DIM ** -0.5)).astype(BF16)
    k = _dot_w(h, wkv_ref[:, 0:kv_w])
    v = _dot_w(h, wkv_ref[:, kv_w:2 * kv_w])
    gate_scr[:, 0:HALF // 2] = _silu(_dot_w(h, wg0_ref[...]))
    gate_scr[:, HALF // 2:HALF] = _silu(_dot_w(h, wg1_ref[...]))
    for hk in range(N_KV):
        k_h = k[:, hk * HEAD_DIM:(hk + 1) * HEAD_DIM].astype(BF16)
        v_h = v[:, hk * HEAD_DIM:(hk + 1) * HEAD_DIM].astype(BF16)
        k_scr[WINDOW:WINDOW + tile, hk * kw:(hk + 1) * kw] = jnp.concatenate([k_h, k_h], axis=1)
        v_scr[WINDOW:WINDOW + tile, hk * vw:hk * vw + kw] = jnp.concatenate([v_h, v_h], axis=1)
    kwin_ref[...] = k[tile - WINDOW:tile, :]
    vwin_ref[...] = v[tile - WINDOW:tile, :]

    lane = lax.broadcasted_iota(jnp.int32, (WINDOW, kw), 1)
    lo = lane < HEAD_DIM
    keep_a = jnp.where(lo, 1.0, 0.0).astype(BF16)
    keep_b = jnp.where(lo, 0.0, 1.0).astype(BF16)
    is_a = lax.broadcasted_iota(jnp.int32, (2 * WINDOW, 1), 0) < WINDOW

    def block(n, carry):
        r0 = pl.multiple_of(n * WINDOW, WINDOW)
        rows = pl.ds(r0, WINDOW)
        keys = pl.ds(r0, 2 * WINDOW)
        first = jnp.where(jnp.logical_and(i == 0, n == 0), 0, 1)
        for hk in range(N_KV):
            for gp in range(GROUP // 2):
                a = hk * GROUP + 2 * gp
                slab = slice(a * HEAD_DIM, (a + 2) * HEAD_DIM)
                q2 = q_scr[rows, slab]
                lhs = jnp.concatenate([q2 * keep_a, q2 * keep_b], axis=0)
                s = _dot_nt(lhs, k_scr[keys, hk * kw:(hk + 1) * kw]) + bias_scr[first, a // 2]
                sink = jnp.where(is_a, sink_ref[a], sink_ref[a + 1])
                m = jnp.maximum(jnp.max(s, axis=-1, keepdims=True), sink)
                p = jnp.exp(s - m).astype(BF16)
                pv = _dot(p, v_scr[keys, hk * vw:(hk + 1) * vw])
                num = jnp.where(lo, pv[0:WINDOW, 0:kw], pv[WINDOW:2 * WINDOW, 0:kw])
                den = jnp.where(lo, pv[0:WINDOW, kw:2 * kw], pv[WINDOW:2 * WINDOW, kw:2 * kw])
                m_slab = jnp.where(lo, m[0:WINDOW], m[WINDOW:2 * WINDOW])
                den = den + jnp.exp(jnp.where(lo, sink_ref[a], sink_ref[a + 1]) - m_slab)
                yb_ref[rows, slab] = (num / den * gate_scr[rows, slab]).astype(BF16)
        return carry

    lax.fori_loop(0, tile // WINDOW, block, 0, unroll=True)
    k_scr[0:WINDOW, :] = k_scr[tile:tile + WINDOW, :]
    v_scr[0:WINDOW, :] = v_scr[tile:tile + WINDOW, :]


BIAS_SPAN = 3 * WINDOW


def _prompt_bias_table(rel_bias):
    dist = WINDOW - jnp.arange(BIAS_SPAN)
    table = jnp.where(((dist >= 0) & (dist < WINDOW))[:, None], rel_bias.astype(F32)[_rel_bucket(dist)], MASK_VALUE)
    return table.T


def _attn_proj_kernel(h_ref, wq_ref, wkv_ref, wg0_ref, wg1_ref, qg_ref, kt_ref, vt_ref, kv_scr):
    kv_w = N_KV * HEAD_DIM
    h = h_ref[...]
    q = _dot_w(h, wq_ref[...]) * (HEAD_DIM ** -0.5)
    for hk in range(N_KV):
        for g in range(GROUP):
            src = (hk * GROUP + g) * HEAD_DIM
            dst = (g * N_KV + hk) * HEAD_DIM
            qg_ref[:, dst:dst + HEAD_DIM] = q[:, src:src + HEAD_DIM]
    qg_ref[:, HALF:HALF + HALF // 2] = _dot_w(h, wg0_ref[...])
    qg_ref[:, HALF + HALF // 2:2 * HALF] = _dot_w(h, wg1_ref[...])
    kv_scr[...] = _dot_w(h, wkv_ref[...])
    for j in range(kt_ref.shape[0]):
        kt_ref[j] = kv_scr[j * WINDOW:(j + 1) * WINDOW, 0:kv_w].T
        vt_ref[j] = kv_scr[j * WINDOW:(j + 1) * WINDOW, kv_w:2 * kv_w].T


def _attn_rowspace_kernel(h_ref, wq_ref, wkv_ref, wg0_ref, wg1_ref, tab_ref, sink_ref,
                          yb_ref, kwin_ref, vwin_ref, qg_ref, kt_ref, vt_ref,
                          q_scr, gate_scr, k_scr, v_scr, bias_scr, kv_scr, *, n_p):
    is_sample = pl.program_id(0) == n_p

    @pl.when(jnp.logical_not(is_sample))
    def _():
        _attn_prompt_kernel(h_ref, wq_ref, wkv_ref, wg0_ref, wg1_ref, tab_ref, sink_ref, yb_ref, kwin_ref, vwin_ref,
                            q_scr, gate_scr, k_scr, v_scr, bias_scr)

    @pl.when(is_sample)
    def _():
        _attn_proj_kernel(h_ref, wq_ref, wkv_ref, wg0_ref, wg1_ref, qg_ref, kt_ref, vt_ref, kv_scr)


def _attn(h_all, n_p, tile, w0, table, sinks):
    kv_w = N_KV * HEAD_DIM
    t = n_p * tile
    win_spec = pl.BlockSpec((WINDOW, kv_w), lambda i: (0, 0))
    out_shape = [jax.ShapeDtypeStruct((t, HALF), BF16),
                 jax.ShapeDtypeStruct((WINDOW, kv_w), F32), jax.ShapeDtypeStruct((WINDOW, kv_w), F32),
                 jax.ShapeDtypeStruct((tile, 2 * HALF), F32), jax.ShapeDtypeStruct((tile // WINDOW, kv_w, WINDOW), F32),
                 jax.ShapeDtypeStruct((tile // WINDOW, kv_w, WINDOW), F32)]
    return pl.pallas_call(
        functools.partial(_attn_rowspace_kernel, n_p=n_p),
        grid=(n_p + 1,),
        in_specs=[_rows(tile, D_MODEL),
                  _cols(D_MODEL, HALF, 4), _cols(D_MODEL, 2 * kv_w, 10),
                  _cols(D_MODEL, HALF // 2, 11), _cols(D_MODEL, HALF // 2, 12),
                  _resident(table.shape), pl.BlockSpec(memory_space=pltpu.SMEM)],
        out_specs=[_prompt_rows(tile, HALF, n_p), win_spec, win_spec] + [_resident(s.shape) for s in out_shape[3:]],
        out_shape=out_shape,
        scratch_shapes=[pltpu.VMEM((tile, HALF), BF16), pltpu.VMEM((tile, HALF), F32),
                        pltpu.VMEM((tile + WINDOW, 2 * kv_w), BF16), pltpu.VMEM((tile + WINDOW, 4 * kv_w), BF16),
                        pltpu.VMEM((2, N_HEADS // 2, 2 * WINDOW, 2 * WINDOW), F32),
                        pltpu.VMEM((tile, 2 * kv_w), F32)],
        compiler_params=_params(),
        name="attn",
    )(h_all, w0, w0, w0, w0, table, sinks)


ATTN_S_BATCH = 16
KEYS_PAD = 2 * WINDOW


def _attn_sample_kernel(qg_ref, ktn_ref, vtn_ref, ck_ref, cv_ref, bias_ref, sink_ref, yb_ref, nk_ref, nv_ref):
    kv_w = N_KV * HEAD_DIM
    row8 = lax.broadcasted_iota(jnp.int32, (8, kv_w), 0)
    lane_head = lax.broadcasted_iota(jnp.int32, (8, kv_w), 1) // HEAD_DIM
    lower = row8 < DEC_SEQ
    pick = [jnp.where(lane_head == 2 * hp + jnp.where(lower, 0, 1), 1.0, 0.0).astype(F32) for hp in range(2)]
    lower_w = lax.broadcasted_iota(jnp.int32, (8, HALF), 0) < DEC_SEQ
    kept = lax.broadcasted_iota(jnp.int32, (kv_w, WINDOW), 1) < WINDOW - DEC_SEQ
    seq0 = pl.program_id(0) * ATTN_S_BATCH
    per_tile = WINDOW // DEC_SEQ

    def slide(old, new_tile, shift):
        return jnp.where(kept, pltpu.roll(old, WINDOW - DEC_SEQ, 1), pltpu.roll(new_tile, shift, 1))

    def pair(p, carry):
        r0 = pl.multiple_of(p * 8, 8)
        rows = qg_ref[pl.ds(r0, 8), :]
        q8 = rows[:, 0:HALF]
        gate8 = rows[:, HALF:2 * HALF]
        out8 = []
        for sub in range(2):
            b = 2 * p + sub
            q_swap = pltpu.roll(q8, 4, 0)
            q_dup = jnp.where(lower_w, q8, q_swap) if sub == 0 else jnp.where(lower_w, q_swap, q8)
            tile = (seq0 + b) // per_tile
            shift = (2 * WINDOW - DEC_SEQ - DEC_SEQ * ((seq0 + b) % per_tile)) % WINDOW
            k_old = ck_ref[b].reshape(kv_w, WINDOW)
            v_old = cv_ref[b].reshape(kv_w, WINDOW)
            k_win = slide(k_old, ktn_ref[tile], shift)
            v_win = slide(v_old, vtn_ref[tile], shift)
            nk_ref[b] = k_win.reshape(N_KV, HEAD_DIM, WINDOW)
            nv_ref[b] = v_win.reshape(N_KV, HEAD_DIM, WINDOW)
            k_all = jnp.concatenate([k_old, k_win], axis=1).astype(BF16)
            v_all = jnp.concatenate([v_old, v_win], axis=1).astype(BF16)
            q_bd = jnp.concatenate(
                [q_dup[:, g * kv_w:(g + 1) * kv_w] * pick[hp] for g in range(GROUP) for hp in range(2)], axis=0)
            s = _dot(q_bd.astype(BF16), k_all) + bias_ref[...]
            o = _attn_softmax_pv(s, sink_ref[:, 0:1], v_all, v_transposed=True)
            out_g = []
            for g in range(GROUP):
                acc = None
                for hp in range(2):
                    piece = o[(2 * g + hp) * 8:(2 * g + hp + 1) * 8, :] * pick[hp]
                    piece = piece + pltpu.roll(piece, 4, 0)
                    acc = piece if acc is None else acc + piece
                out_g.append(acc)
            out8.append(jnp.concatenate(
                [out_g[g][:, hk * HEAD_DIM:(hk + 1) * HEAD_DIM] for hk in range(N_KV) for g in range(GROUP)], axis=1))
        o8 = jnp.where(lower_w, out8[0], out8[1])
        yb_ref[pl.ds(r0, 8), :] = (o8 * _silu(gate8)).astype(BF16)
        return carry

    lax.fori_loop(0, ATTN_S_BATCH // 2, pair, 0, unroll=2)


def _sample_bias(rel_bias, sinks):
    t = jnp.arange(DEC_SEQ)[:, None]
    j = jnp.arange(KEYS_PAD)[None, :]
    pos = jnp.where(j < WINDOW, j, j - (KEYS_PAD - DEC_SEQ) + WINDOW)
    dist = t + WINDOW - pos
    valid = (dist >= 0) & (dist < WINDOW) & ((j < WINDOW) | (j >= KEYS_PAD - DEC_SEQ))
    bias = jnp.where(valid[:, :, None], rel_bias.astype(F32)[_rel_bucket(dist)], MASK_VALUE)
    bias = bias.reshape(DEC_SEQ, KEYS_PAD, N_KV, GROUP).transpose(3, 2, 0, 1).reshape(N_HEADS * DEC_SEQ, KEYS_PAD)
    sink = jnp.broadcast_to(sinks.astype(F32).reshape(N_KV, GROUP).T[:, :, None], (GROUP, N_KV, DEC_SEQ))
    return bias, jnp.broadcast_to(sink.reshape(N_HEADS * DEC_SEQ, 1), (N_HEADS * DEC_SEQ, 128))


def _attn_sample(qg, kt_new, vt_new, cache_kt, cache_vt, bias, sink):
    n_seq = cache_kt.shape[0]
    bb = ATTN_S_BATCH
    cache_spec = pl.BlockSpec((bb, N_KV, HEAD_DIM, WINDOW), lambda i: (i, 0, 0, 0))
    return pl.pallas_call(
        _attn_sample_kernel,
        grid=(n_seq // bb,),
        in_specs=[_rows(bb * DEC_SEQ, 2 * HALF), _resident(kt_new.shape), _resident(vt_new.shape),
                  cache_spec, cache_spec, _resident(bias.shape), _resident(sink.shape)],
        out_specs=[_rows(bb * DEC_SEQ, HALF), cache_spec, cache_spec],
        out_shape=[jax.ShapeDtypeStruct((n_seq * DEC_SEQ, HALF), BF16),
                   jax.ShapeDtypeStruct(cache_kt.shape, F32), jax.ShapeDtypeStruct(cache_vt.shape, F32)],
        compiler_params=_params(),
        name="attn_sample",
    )(qg, kt_new, vt_new, cache_kt, cache_vt, bias, sink)


def _prep_layer0(g_pre, w_in, conv_w, rel_bias, sinks, w_out, g_post):
    return dict(
        g_pre=g_pre.reshape(1, D_MODEL), w0=w_in, conv_w=conv_w, rel_bias=rel_bias, sinks=sinks,
        w_out=w_out, g_post=g_post.reshape(1, D_MODEL))


def _layer0(x_p, x_s, conv_state, cache_k, cache_v, w):
    n_seq = x_s.shape[0]
    rows = x_s.reshape(n_seq * DEC_SEQ, D_MODEL)
    tile = rows.shape[0]
    n_p = x_p.shape[0] // tile
    ya_all, s_tail, s_s, h_all = _conv_a(x_p, rows, w['g_pre'], w['w0'], w['conv_w'],
                                         conv_state.reshape(n_seq, 2 * HALF))
    yb_p, kwin, vwin, qg, kt_new, vt_new = _attn(h_all, n_p, tile, w['w0'], _prompt_bias_table(w['rel_bias']),
                                                 w['sinks'])
    bias, sink = _sample_bias(w['rel_bias'], w['sinks'])
    yb_s, new_kt, new_vt = _attn_sample(qg, kt_new, vt_new, cache_k.transpose(0, 2, 3, 1),
                                        cache_v.transpose(0, 2, 3, 1), bias, sink)
    (y_all,) = _out_proj(ya_all, (yb_p, yb_s), (x_p, rows), w['w_out'], w['g_post'], n_p, tile, out_pair=False)
    prompt_state = (s_tail[6:8], kwin.reshape(WINDOW, N_KV, HEAD_DIM), vwin.reshape(WINDOW, N_KV, HEAD_DIM))
    sample_state = (s_s.reshape(n_seq, DEC_SEQ, HALF)[:, DEC_SEQ - 2:], new_kt.transpose(0, 3, 1, 2),
                    new_vt.transpose(0, 3, 1, 2))
    return y_all, prompt_state, sample_state


def _layer_norm(v, g, b):
    xc = v - jnp.mean(v, axis=-1, keepdims=True)
    return xc * lax.rsqrt(jnp.mean(xc * xc, axis=-1, keepdims=True) + NORM_EPS) * g + b


def _cmlp_prompt_kernel(x_ref, g_ref, w_ref, lng_ref, lnb_ref, ws_ref, bs_ref, yc_ref, h_ref, vn_scr):
    tile = x_ref.shape[0]
    h = _rms_bf16(x_ref[...], g_ref[...])
    h_ref[...] = h
    v = _dot_wt(h, w_ref[HALF:2 * HALF, :])
    vn_scr[...] = _layer_norm(v, lng_ref[...], lnb_ref[...]).astype(BF16)
    gw = HALF // CMLP_GROUPS
    cols = 2 * gw
    for cb in range(HALF // cols):
        u = _dot_wt(h, w_ref[cb * cols:(cb + 1) * cols, :])
        gate = _silu(_dot_wt(h, w_ref[2 * HALF + cb * cols:2 * HALF + (cb + 1) * cols, :]))
        for gi in range(2):
            grp = 2 * cb + gi
            lanes = slice(grp * gw, (grp + 1) * gw)
            for n in range(tile // CHUNK):
                rows = slice(n * CHUNK, (n + 1) * CHUNK)
                mixed = _dot(ws_ref[grp], vn_scr[rows, lanes]) + bs_ref[grp]
                yc_ref[rows, lanes] = (u[rows, gi * gw:(gi + 1) * gw] * mixed
                                       * gate[rows, gi * gw:(gi + 1) * gw]).astype(BF16)


def _cmlp_kernel(x_ref, g_ref, w_ref, lng_ref, lnb_ref, ws_ref, bs_ref, coef_ref, bias_ref, yc_ref, h_ref, vns_ref,
                 vn_scr, *, n_p):
    is_sample = pl.program_id(0) == n_p

    @pl.when(jnp.logical_not(is_sample))
    def _():
        _cmlp_prompt_kernel(x_ref, g_ref, w_ref, lng_ref, lnb_ref, ws_ref, bs_ref, yc_ref, h_ref, vn_scr)

    @pl.when(is_sample)
    def _():
        _cmlp_sample_kernel(x_ref, g_ref, w_ref, lng_ref, lnb_ref, coef_ref, bias_ref, yc_ref, vns_ref, h_ref)


def _cmlp(x_all, n_p, tile, w):
    t = x_all.shape[0]
    consts = [w['ln_g'], w['ln_b'], w['ws_tril'], w['bs_rows'], w['coef'], w['bias4']]
    return pl.pallas_call(
        functools.partial(_cmlp_kernel, n_p=n_p),
        grid=(n_p + 1,),
        in_specs=[_rows(tile, D_MODEL), _resident((1, D_MODEL)), _rowwin(3 * HALF, D_MODEL, 0)]
                 + [_resident(c.shape) for c in consts],
        out_specs=[_rows(tile, HALF), _rows(tile, D_MODEL), _resident((tile, HALF))],
        out_shape=[jax.ShapeDtypeStruct((t, HALF), BF16), jax.ShapeDtypeStruct((t, D_MODEL), BF16),
                   jax.ShapeDtypeStruct((tile, HALF), F32)],
        scratch_shapes=[pltpu.VMEM((tile, HALF), BF16)],
        compiler_params=_params(),
        name="cmlp",
    )(x_all, w['g_pre'], w['w1'], *consts)


def _cmlp_sample_kernel(x_ref, g_ref, w_ref, lng_ref, lnb_ref, coef_ref, bias_ref, yc_ref, vn_ref, h_ref):
    t = x_ref.shape[0]
    h = _rms_bf16(x_ref[...], g_ref[...])
    h_ref[...] = h
    u = _dot_wt(h, w_ref[0:HALF, :])
    vn = _layer_norm(_dot_wt(h, w_ref[HALF:2 * HALF, :]), lng_ref[...], lnb_ref[...])
    gate = _silu(_dot_wt(h, w_ref[2 * HALF:3 * HALF, :]))
    vn_ref[...] = vn

    def tiled(a):
        return a.reshape(t // 8, 8, HALF)

    mixed = tiled(vn) * coef_ref[0][None] + bias_ref[...][None]
    for k in range(1, DEC_SEQ):
        mixed = mixed + tiled(pltpu.roll(vn, k, 0)) * coef_ref[k][None]
    yc_ref[...] = (u * mixed.reshape(t, HALF) * gate).astype(BF16)


HEAD_LANES = 128
SSD_GW = HALF // SSD_GROUPS


def _softplus(x):
    return jnp.maximum(x, 0.0) + jnp.log1p(jnp.exp(-jnp.abs(x)))


def _dt_proj(h, wdt_ref):
    pad = jnp.zeros((HEAD_LANES - SSD_HEADS, D_MODEL), F32)
    return _dot_wt(h, jnp.concatenate([wdt_ref[...], pad], axis=0))


def _group_norm_gate(y, z, gn):
    gated = y * _silu(z)
    parts = []
    for g in range(SSD_GROUPS):
        part = gated[:, g * SSD_GW:(g + 1) * SSD_GW]
        parts.append(part * lax.rsqrt(jnp.mean(part * part, axis=-1, keepdims=True) + NORM_EPS))
    return (jnp.concatenate(parts, axis=1) * gn).astype(BF16)


def _ssd_prompt_kernel(h_ref, wz_ref, wx0_ref, wx1_ref, wx2_ref, wdt_ref, cw_ref, cb_ref, dtb_ref, alog_ref,
                       dskip_ref, gn_ref, e3_ref, tril3_ref, yd_ref, tail_ref, ssm_ref,
                       xbc_scr, z_scr, dt_scr, ht_scr, shift_scr):
    tile = h_ref.shape[0]
    i = pl.program_id(0)
    cd = SSD_CONV_DIM
    h = h_ref[...]

    @pl.when(i == 0)
    def _():
        tail_ref[...] = jnp.zeros_like(tail_ref)
        ht_scr[...] = jnp.zeros_like(ht_scr)

    z_scr[...] = _dot_wt(h, wz_ref[...])
    dt_scr[...] = _softplus(_dt_proj(h, wdt_ref) + dtb_ref[...])
    third = cd // 3
    for j, wx_ref in enumerate((wx0_ref, wx1_ref, wx2_ref)):
        cols = slice(j * third, (j + 1) * third)
        raw = _dot_wt(h, wx_ref[...])
        shift_scr[0:8, :] = tail_ref[:, cols]
        shift_scr[8:8 + tile, :] = raw
        conv = raw * cw_ref[3:4, cols] + cb_ref[:, cols]
        for k in range(1, 4):
            conv = conv + shift_scr[8 - k:8 - k + tile, :] * cw_ref[3 - k:4 - k, cols]
        xbc_scr[:, cols] = _silu(conv)
        tail_ref[:, cols] = raw[tile - 8:tile, :]

    a16 = -jnp.exp(alog_ref[...])
    causal =(lax.broadcasted_iota(jnp.int32, (CHUNK, CHUNK), 0)
              >= lax.broadcasted_iota(jnp.int32, (CHUNK, CHUNK), 1))
    first_half = lax.broadcasted_iota(jnp.int32, (CHUNK, 2 * HEAD_DIM), 1) < HEAD_DIM
    keep_a = jnp.where(first_half, 1.0, 0.0).astype(BF16)
    keep_b = jnp.where(first_half, 0.0, 1.0).astype(BF16)

    def chunk(n, carry):
        r0 = pl.multiple_of(n * CHUNK, CHUNK)
        rows = pl.ds(r0, CHUNK)
        xs = xbc_scr[rows, 0:HALF]
        dt16 = dt_scr[rows, :]
        dt_e = _dot(jnp.concatenate(_split3(dt16), axis=1), e3_ref[...])
        acs16 = _dot(tril3_ref[...], jnp.concatenate(_split3(dt16 * a16), axis=0))
        acs_e = _dot(jnp.concatenate(_split3(acs16), axis=1), e3_ref[...])
        acs_t = acs16.T
        last_e = acs_e[CHUNK - 1:CHUNK, :]
        xdt = xs * dt_e
        xdt_bf = xdt.astype(BF16)
        xw = (jnp.exp(last_e - acs_e) * xdt).astype(BF16)
        dec_e = jnp.exp(last_e)
        y_parts = []
        yoff_parts = []
        for g in range(SSD_GROUPS):
            c_g = xbc_scr[rows, HALF + 2 * SSD_STATE + g * SSD_STATE:HALF + 2 * SSD_STATE + (g + 1) * SSD_STATE].astype(BF16)
            b_g = xbc_scr[rows, HALF + g * SSD_STATE:HALF + (g + 1) * SSD_STATE].astype(BF16)
            cb = _dot_nt(c_g, b_g)
            h_prev = ht_scr[g]
            yoff_parts.append(_dot(c_g, h_prev.astype(BF16)))
            for r in range(0, SSD_HEADS // SSD_GROUPS, 2):
                wgt = []
                for hd in (g * (SSD_HEADS // SSD_GROUPS) + r, g * (SSD_HEADS // SSD_GROUPS) + r + 1):
                    seg = acs16[:, hd:hd + 1] - acs_t[hd:hd + 1, :]
                    wgt.append(cb * jnp.exp(jnp.where(causal, seg, -jnp.inf)))
                a = g * (SSD_HEADS // SSD_GROUPS) + r
                slab = xdt_bf[:, a * HEAD_DIM:(a + 2) * HEAD_DIM]
                rhs = jnp.concatenate([slab * keep_a, slab * keep_b], axis=0)
                y_parts.append(_dot(jnp.concatenate(wgt, axis=1).astype(BF16), rhs))
            lanes = slice(g * SSD_GW, (g + 1) * SSD_GW)
            ht_scr[g] = h_prev * dec_e[:, lanes] + _dot_tn(b_g, xw[:, lanes])
        y = (jnp.concatenate(y_parts, axis=1) + jnp.concatenate(yoff_parts, axis=1) * jnp.exp(acs_e)
             + dskip_ref[...] * xs)
        yd_ref[rows, :] = _group_norm_gate(y, z_scr[rows, :], gn_ref[...])
        return carry

    lax.fori_loop(0, tile // CHUNK, chunk, 0, unroll=True)

    @pl.when(i == pl.num_programs(0) - 1)
    def _():
        for g in range(SSD_GROUPS):
            ssm_ref[g * SSD_GW:(g + 1) * SSD_GW, :] = ht_scr[g].T


def _ssd_weight_specs():
    third = SSD_CONV_DIM // 3
    first = 4 * HALF // third
    return ([_rowwin(HALF, D_MODEL, 3)] + [_rowwin(third, D_MODEL, first + j) for j in range(3)]
            + [_rowwin(SSD_HEADS, D_MODEL, (4 * HALF + SSD_CONV_DIM) // SSD_HEADS)])


def _ssd_prompt(h, n_p, tile, w):
    t = n_p * tile
    cd = SSD_CONV_DIM
    consts = [w['conv_w'], w['conv_b'], w['dt_bias16'], w['a_log16'],
              w['d_skip_e'], w['gate_norm_g'], w['expand3'], w['tril3']]
    return pl.pallas_call(
        _ssd_prompt_kernel,
        grid=(t // tile,),
        in_specs=[_rows(tile, D_MODEL)] + _ssd_weight_specs() + [_resident(c.shape) for c in consts],
        out_specs=[_rows(tile, HALF), pl.BlockSpec((8, cd), lambda i: (0, 0)),
                   pl.BlockSpec((HALF, SSD_STATE), lambda i: (0, 0))],
        out_shape=[jax.ShapeDtypeStruct((t, HALF), BF16), jax.ShapeDtypeStruct((8, cd), F32),
                   jax.ShapeDtypeStruct((HALF, SSD_STATE), F32)],
        scratch_shapes=[pltpu.VMEM((tile, cd), F32), pltpu.VMEM((tile, HALF), F32),
                        pltpu.VMEM((tile, HEAD_LANES), F32), pltpu.VMEM((SSD_GROUPS, SSD_STATE, SSD_GW), F32),
                        pltpu.VMEM((8 + tile, cd // 3), F32)],
        compiler_params=_params(),
        name="ssd_prompt",
    )(h, w['w1'], w['w1'], w['w1'], w['w1'], w['w1'], *consts)


def _ssd_sample_pre_kernel(h_ref, wz_ref, wx0_ref, wx1_ref, wx2_ref, wdt_ref, cw_ref, cb_ref, st_ref,
                           dtb_ref, aloge_ref, dskip_ref, e3_ref, seg_ref,
                           nconv_ref, z_ref, ysk_ref, eacs_ref, xw_ref, dec_ref, b_ref, c_ref, raw_scr):
    t = h_ref.shape[0]
    n_seq = t // DEC_SEQ
    h = h_ref[...]
    z_ref[...] = _dot_wt(h, wz_ref[...])
    raw = jnp.concatenate([_dot_wt(h, wx0_ref[...]), _dot_wt(h, wx1_ref[...]), _dot_wt(h, wx2_ref[...])], axis=1)
    for c in range(raw_scr.shape[0]):
        lanes = slice(c * 128, (c + 1) * 128)
        raw_scr[c] = raw[:, lanes]
        for j in range(3):
            nconv_ref[j, :, lanes] = raw_scr[c, pl.ds(j + 1, n_seq, stride=DEC_SEQ), :]
    dt16 = _softplus(_dt_proj(h, wdt_ref) + dtb_ref[...])
    dt = _dot(jnp.concatenate(_split3(dt16), axis=1), e3_ref[...])
    old = [st_ref[j] for j in range(3)]
    p1 = _place_steps(t, [(0, old[2])])
    p2 = _place_steps(t, [(0, old[1]), (1, old[2])])
    p3 = _place_steps(t, [(0, old[0]), (1, old[1]), (2, old[2])])

    def step_of(width):
        return lax.broadcasted_iota(jnp.int32, (t, width), 0) % DEC_SEQ

    def back(a, k):
        return jnp.where(step_of(a.shape[1]) >= k, pltpu.roll(a, k, 0), 0.0)

    def ahead(a, k):
        return jnp.where(step_of(a.shape[1]) + k < DEC_SEQ, pltpu.roll(a, t - k, 0), 0.0)

    conv = (raw * cw_ref[3:4, :] + (back(raw, 1) + p1) * cw_ref[2:3, :]
            + (back(raw, 2) + p2) * cw_ref[1:2, :] + (back(raw, 3) + p3) * cw_ref[0:1, :]
            + cb_ref[...])
    xbc = _silu(conv)
    xs = xbc[:, 0:HALF]
    bm = xbc[:, HALF:HALF + 2 * SSD_STATE]
    cm = xbc[:, HALF + 2 * SSD_STATE:]
    b_ref[...] = bm
    c_ref[...] = cm
    da = dt * (-jnp.exp(aloge_ref[...]))
    acs = da + back(da, 1) + back(da, 2) + back(da, 3)
    suffix = ahead(da, 1) + ahead(da, 2) + ahead(da, 3)
    xdt = xs * dt
    y = _dot((cm * bm).astype(BF16), seg_ref[...]) * xdt
    for k in range(1, DEC_SEQ):
        cbk = _dot((cm * pltpu.roll(bm, k, 0)).astype(BF16), seg_ref[...])
        term = cbk * jnp.exp(acs - pltpu.roll(acs, k, 0)) * pltpu.roll(xdt, k, 0)
        y = y + jnp.where(step_of(HALF) >= k, term, 0.0)
    ysk_ref[...] = y + dskip_ref[...] * xs
    eacs_ref[...] = jnp.exp(acs)
    xw_ref[...] = jnp.exp(suffix) * xdt
    dec_ref[...] = jnp.exp(acs + suffix)


def _ssd_sample_pre(h, first_row, conv_state, w):
    t = conv_state.shape[1] * DEC_SEQ
    cd = SSD_CONV_DIM
    tile = min(SSD_PRE_ROWS, t)
    first = first_row // tile
    seqs = tile // DEC_SEQ
    state_spec = pl.BlockSpec((3, seqs, cd), lambda i: (0, i, 0))
    head = [w['conv_w'], w['conv_b']]
    tail = [w['dt_bias16'], w['a_log_e'], w['d_skip_e'], w['expand3'], w['seg_expand']]
    args = [h, w['w1'], w['w1'], w['w1'], w['w1'], w['w1']] + head + [conv_state] + tail
    wide = jax.ShapeDtypeStruct((t, HALF), F32)
    narrow = jax.ShapeDtypeStruct((t, 2 * SSD_STATE), F32)
    out_shape = [jax.ShapeDtypeStruct(conv_state.shape, F32), wide, wide, wide, wide, wide, narrow, narrow]
    return pl.pallas_call(
        _ssd_sample_pre_kernel,
        grid=(t // tile,),
        in_specs=[pl.BlockSpec((tile, D_MODEL), lambda i: (first + i, 0))] + _ssd_weight_specs()
                 + [_resident(c.shape) for c in head] + [state_spec] + [_resident(c.shape) for c in tail],
        out_specs=[state_spec] + [_rows(tile, HALF)] * 5 + [_rows(tile, 2 * SSD_STATE)] * 2,
        out_shape=out_shape,
        scratch_shapes=[pltpu.VMEM((cd // 128, tile, 128), F32)],
        compiler_params=_params(),
        name="ssd_sample_pre",
    )(*args)


SSD_S_BATCH = 8
SSD_PRE_ROWS = 256


def _ssd_sample_state_kernel(st_ref, c_ref, b_ref, xw_ref, dec_ref, eacs_ref, ysk_ref, z_ref, gn_ref,
                             yd_ref, nst_ref):
    row_n = lax.broadcasted_iota(jnp.int32, (8, SSD_STATE), 0)
    row_w = lax.broadcasted_iota(jnp.int32, (8, SSD_GW), 0)
    row_f = lax.broadcasted_iota(jnp.int32, (8, HALF), 0)
    ones_rows = jnp.where((row_n >= 4) & (row_n < 7), 1.0, 0.0).astype(BF16)
    hpg = SSD_HEADS // SSD_GROUPS

    def pair(p, carry):
        r0 = pl.multiple_of(p * 8, 8)
        rows = pl.ds(r0, 8)
        c8 = c_ref[rows, :].astype(BF16)
        b8 = b_ref[rows, :]
        xw8 = xw_ref[rows, :]
        dec8 = dec_ref[rows, :]
        yoff = []
        for sub in range(2):
            b = 2 * p + sub
            xw_own = xw8 if sub == 0 else pltpu.roll(xw8, 4, 0)
            b_own = b8 if sub == 0 else pltpu.roll(b8, 4, 0)
            hi, mid, lo = (term.astype(F32) for term in _split3(dec8[4 * sub:4 * sub + 1, :]))
            parts = []
            for g in range(SSD_GROUPS):
                lanes = slice(g * SSD_GW, (g + 1) * SSD_GW)
                heads = pl.ds(g * hpg, hpg)
                h0 = st_ref[b, heads].reshape(SSD_GW, SSD_STATE)
                parts.append(_dot_nt(c8[:, g * SSD_STATE:(g + 1) * SSD_STATE], h0.astype(BF16)))
                lhs = jnp.where(row_w < 4, xw_own[:, lanes],
                                jnp.where(row_w == 4, hi[:, lanes],
                                          jnp.where(row_w == 5, mid[:, lanes],
                                                    jnp.where(row_w == 6, lo[:, lanes], 0.0)))).astype(BF16)
                rhs_b = jnp.where(row_n < 4, b_own[:, g * SSD_STATE:(g + 1) * SSD_STATE], 0.0).astype(BF16)
                decay = _dot_tn(lhs, ones_rows)
                nst_ref[b, heads] = (h0 * decay + _dot_tn(lhs, rhs_b)).reshape(hpg, HEAD_DIM, SSD_STATE)
            yoff.append(jnp.concatenate(parts, axis=1))
        yoff8 = jnp.where(row_f < 4, yoff[0], yoff[1])
        y = ysk_ref[rows, :] + yoff8 * eacs_ref[rows, :]
        yd_ref[rows, :] = _group_norm_gate(y, z_ref[rows, :], gn_ref[...])
        return carry

    lax.fori_loop(0, SSD_S_BATCH // 2, pair, 0, unroll=True)


def _ssd_sample_state(state, cm, bm, xw, dec, eacs, ysk, z, gn):
    n_seq = state.shape[0]
    bb = SSD_S_BATCH
    r = bb * DEC_SEQ
    st_spec = pl.BlockSpec((bb, SSD_HEADS, HEAD_DIM, SSD_STATE), lambda i: (i, 0, 0, 0))
    return pl.pallas_call(
        _ssd_sample_state_kernel,
        grid=(n_seq // bb,),
        in_specs=[st_spec, _rows(r, 2 * SSD_STATE), _rows(r, 2 * SSD_STATE)] + [_rows(r, HALF)] * 5
                 + [_resident((1, HALF))],
        out_specs=[_rows(r, HALF), st_spec],
        out_shape=[jax.ShapeDtypeStruct((n_seq * DEC_SEQ, HALF), BF16), jax.ShapeDtypeStruct(state.shape, F32)],
        compiler_params=_params(),
        name="ssd_sample_state",
    )(state, cm, bm, xw, dec, eacs, ysk, z, gn)


def _prep_layer1(g_pre, w_in, ln_g, ln_b, w_s, b_s, conv_w, conv_b, dt_bias, a_log, d_skip, gate_norm_g,
                 w_out, g_post):
    cd = SSD_CONV_DIM
    gw = HALF // CMLP_GROUPS
    w1 = w_in.T

    def lanes16(v):
        return jnp.pad(v.astype(F32), (0, HEAD_LANES - SSD_HEADS)).reshape(1, HEAD_LANES)

    def per_channel(v):
        return jnp.repeat(v.astype(F32), HEAD_DIM).reshape(1, HALF)

    head_of = np.arange(HALF) // HEAD_DIM
    expand = jnp.asarray(np.arange(HEAD_LANES)[:, None] == head_of[None, :], BF16)
    tril = jnp.asarray(np.tril(np.ones((CHUNK, CHUNK))), BF16)
    grp_rows = np.arange(2 * SSD_STATE) // SSD_STATE
    seg_expand = jnp.asarray(grp_rows[:, None] == (head_of // (SSD_HEADS // SSD_GROUPS))[None, :], BF16)

    w4 = jnp.tril(w_s[:, :DEC_SEQ, :DEC_SEQ])
    steps = jnp.arange(DEC_SEQ)
    coef = []
    for k in range(DEC_SEQ):
        src = steps - k
        ck = jnp.where((src >= 0)[None, :], w4[:, steps, jnp.maximum(src, 0)], 0.0)
        ck = jnp.repeat(ck.T, gw, axis=1)
        coef.append(jnp.concatenate([ck, ck], axis=0))
    bias4 = jnp.repeat(b_s[:, :DEC_SEQ].T, gw, axis=1)
    return dict(
        g_pre=g_pre.reshape(1, D_MODEL), w1=w1, ln_g=ln_g.reshape(1, HALF), ln_b=ln_b.reshape(1, HALF),
        ws_tril=jnp.tril(w_s).astype(BF16),
        bs_rows=jnp.broadcast_to(b_s.astype(F32)[:, :, None], (CMLP_GROUPS, CHUNK, gw)),
        coef=jnp.stack(coef).astype(F32), bias4=jnp.concatenate([bias4, bias4], axis=0).astype(F32),
        conv_w=conv_w, conv_b=conv_b.reshape(1, cd), dt_bias16=lanes16(dt_bias), a_log16=lanes16(a_log),
        a_log_e=per_channel(a_log), d_skip_e=per_channel(d_skip), gate_norm_g=gate_norm_g.reshape(1, HALF),
        expand3=jnp.concatenate([expand] * 3, axis=0), tril3=jnp.concatenate([tril] * 3, axis=1),
        seg_expand=seg_expand, w_out=w_out, g_post=g_post.reshape(1, D_MODEL))


def _layer1(x_all, n_p, conv_state, ssm_state, w):
    n_seq = ssm_state.shape[0]
    tile = n_seq * DEC_SEQ
    yc_all, h_all, vn = _cmlp(x_all, n_p, tile, w)
    yd_p, tail, ssm = _ssd_prompt(h_all, n_p, tile, w)
    new_conv, z, ysk, eacs, xw, dec, bm, cm = _ssd_sample_pre(h_all, n_p * tile, conv_state.transpose(1, 0, 2), w)
    yd_s, new_state = _ssd_sample_state(ssm_state, cm, bm, xw, dec, eacs, ysk, z, w['gate_norm_g'])
    y_p, y_s = _out_proj(yc_all, (yd_p, yd_s), x_all, w['w_out'], w['g_post'], n_p, tile, out_pair=True)
    prompt_out = (y_p, tail[5:8], ssm.reshape(SSD_HEADS, HEAD_DIM, SSD_STATE))
    sample_out = (y_s.reshape(n_seq, DEC_SEQ, D_MODEL), vn.reshape(n_seq, DEC_SEQ, HALF),
                  new_conv.transpose(1, 0, 2), new_state)
    return prompt_out, sample_out


def kernel(x_prompt, x_sample, state_conv_a, cache_win_k, cache_win_v, state_conv_d, state_ssm, rel_bias,
           l0_g_pre, l0_w_in, l0_conv_w, l0_sinks, l0_w_out, l0_g_post,
           l1_g_pre, l1_w_in, l1_ln_g, l1_ln_b, l1_w_s, l1_b_s, l1_conv_w, l1_conv_b, l1_dt_bias, l1_a_log,
           l1_d_skip, l1_gate_norm_g, l1_w_out, l1_g_post):
    w0 = _prep_layer0(l0_g_pre, l0_w_in, l0_conv_w, rel_bias, l0_sinks, l0_w_out, l0_g_post)
    w1 = _prep_layer1(l1_g_pre, l1_w_in, l1_ln_g, l1_ln_b, l1_w_s, l1_b_s, l1_conv_w, l1_conv_b, l1_dt_bias,
                      l1_a_log, l1_d_skip, l1_gate_norm_g, l1_w_out, l1_g_post)
    x_p = x_prompt[0]
    n_p = x_p.shape[0] // (x_sample.shape[0] * DEC_SEQ)
    y_all, (p_conv_a, p_win_k, p_win_v), (s_conv_a, s_win_k, s_win_v) = _layer0(
        x_p, x_sample, state_conv_a, cache_win_k, cache_win_v, w0)
    (yp, p_conv_d, p_ssm), (ys, s_chunk_v, s_conv_d, s_ssm) = _layer1(y_all, n_p, state_conv_d, state_ssm, w1)
    return (yp[None], ys, p_conv_a[None], s_conv_a, p_win_k[None], p_win_v[None], s_win_k, s_win_v, s_chunk_v,
            p_conv_d[None], s_conv_d, p_ssm[None], s_ssm)
```

```python
import functools
import math

import jax
import jax.numpy as jnp
import numpy as np
from jax import lax
from jax.experimental import pallas as pl
from jax.experimental.pallas import tpu as pltpu

F32 = jnp.float32
BF16 = jnp.bfloat16

D_MODEL = 2048
HALF = 1024
HEAD_DIM = 64
N_HEADS = 16
N_KV = 4
GROUP = 4
WINDOW = 128
NUM_BUCKETS = 32
MAX_DISTANCE = 128
CMLP_GROUPS = 8
CHUNK = 128
SSD_HEADS = 16
SSD_STATE = 128
SSD_GROUPS = 2
SSD_CONV_DIM = HALF + 2 * SSD_GROUPS * SSD_STATE
DEC_SEQ = 4
NORM_EPS = 1e-6
MASK_VALUE = -1e30

ROW_TILE = 512
VMEM_LIMIT = 56 * 1024 * 1024
CONV_A_VMEM_LIMIT = 60 * 1024 * 1024


def _params(n_axes=1):
    return pltpu.CompilerParams(dimension_semantics=("arbitrary",) * n_axes,
                                vmem_limit_bytes=VMEM_LIMIT)


def _resident(shape):
    nd = len(shape)
    return pl.BlockSpec(shape, lambda *_: (0,) * nd, pipeline_mode=pl.Buffered(1))


def _rows(tile, width):
    return pl.BlockSpec((tile, width), lambda i: (i, 0))


def _cols(rows, width, block):
    return pl.BlockSpec((rows, width), lambda *_: (0, block), pipeline_mode=pl.Buffered(1))


def _rowwin(height, cols, block):
    return pl.BlockSpec((height, cols), lambda *_: (block, 0), pipeline_mode=pl.Buffered(1))


def _rms_bf16(x, g):
    ms = jnp.mean(x * x, axis=-1, keepdims=True)
    return (x * lax.rsqrt(ms + NORM_EPS) * g).astype(BF16)


def _silu(x):
    return x * jax.nn.sigmoid(x)


def _dot(a, b):
    return jnp.dot(a, b, preferred_element_type=F32)


def _dot_nt(a, b):
    return lax.dot_general(a, b, (((1,), (1,)), ((), ())), preferred_element_type=F32)


def _dot_tn(a, b):
    return lax.dot_general(a, b, (((0,), (0,)), ((), ())), preferred_element_type=F32)


def _dot_w(a, w):
    return _dot(a, w.astype(BF16))


def _dot_wt(a, wt):
    return _dot_nt(a, wt.astype(BF16))


def _split3(x):
    hi = x.astype(BF16)
    r1 = x - hi.astype(F32)
    mid = r1.astype(BF16)
    lo = (r1 - mid.astype(F32)).astype(BF16)
    return hi, mid, lo


def _place_steps(t, placements):
    n_seq = placements[0][1].shape[0]
    row = lax.broadcasted_iota(jnp.int32, (t, n_seq), 0)
    seq = lax.broadcasted_iota(jnp.int32, (t, n_seq), 1)
    lhs, rhs = [], []
    for step, state in placements:
        sel = jnp.where(row == DEC_SEQ * seq + step, 1.0, 0.0).astype(BF16)
        lhs += [sel] * 3
        rhs += list(_split3(state))
    return _dot(jnp.concatenate(lhs, axis=1), jnp.concatenate(rhs, axis=0))


def _prompt_rows(tile, width, n_p):
    return pl.BlockSpec((tile, width), lambda i: (jnp.minimum(i, n_p - 1), 0))


def _group_specs(arg, tile, width, n_p):
    if isinstance(arg, tuple):
        return [_prompt_rows(tile, width, n_p), _resident((tile, width))]
    return [_rows(tile, width)]


def _out_proj_tile(ya_ref, yb_ref, x_ref, w_ref, g_ref, o_ref):
    y = _dot_w(ya_ref[...], w_ref[0:HALF, :]) + _dot_w(yb_ref[...], w_ref[HALF:2 * HALF, :])
    ms = jnp.mean(y * y, axis=-1, keepdims=True)
    o_ref[...] = x_ref[...] + y * lax.rsqrt(ms + NORM_EPS) * g_ref[...]


def _out_proj_kernel(*refs, n_p, n_ya, n_yb, n_x):
    refs = list(refs)
    ya, yb, x = refs[:n_ya], refs[n_ya:n_ya + n_yb], refs[n_ya + n_yb:n_ya + n_yb + n_x]
    w_ref, g_ref = refs[n_ya + n_yb + n_x:n_ya + n_yb + n_x + 2]
    outs = refs[n_ya + n_yb + n_x + 2:]
    is_sample = pl.program_id(0) == n_p

    @pl.when(jnp.logical_not(is_sample))
    def _():
        _out_proj_tile(ya[0], yb[0], x[0], w_ref, g_ref, outs[0])

    @pl.when(is_sample)
    def _():
        _out_proj_tile(ya[-1], yb[-1], x[-1], w_ref, g_ref, outs[-1])


def _out_proj(ya, yb, x, w, g, n_p, tile, out_pair):
    groups = [(ya, HALF), (yb, HALF), (x, D_MODEL)]
    in_specs, args = [], []
    for arg, width in groups:
        in_specs += _group_specs(arg, tile, width, n_p)
        args += list(arg) if isinstance(arg, tuple) else [arg]
    if out_pair:
        out_specs = [_prompt_rows(tile, D_MODEL, n_p), _resident((tile, D_MODEL))]
        out_shape = [jax.ShapeDtypeStruct((n_p * tile, D_MODEL), F32), jax.ShapeDtypeStruct((tile, D_MODEL), F32)]
    else:
        out_specs = [_rows(tile, D_MODEL)]
        out_shape = [jax.ShapeDtypeStruct(((n_p + 1) * tile, D_MODEL), F32)]
    n_of = [2 if isinstance(arg, tuple) else 1 for arg, _ in groups]
    return pl.pallas_call(
        functools.partial(_out_proj_kernel, n_p=n_p, n_ya=n_of[0], n_yb=n_of[1], n_x=n_of[2]),
        grid=(n_p + 1,),
        in_specs=in_specs + [_resident((2 * HALF, D_MODEL)), _resident((1, D_MODEL))],
        out_specs=out_specs,
        out_shape=out_shape,
        compiler_params=_params(),
        name="out_proj",
    )(*args, w, g)


CONV_A_CHUNK = 256


def _conv_a_kernel(*refs, sample):
    if sample:
        x_ref, g_ref, w_ref, cw_ref, st_ref, ya_ref, s_ref, h_ref = refs
    else:
        x_ref, g_ref, w_ref, cw_ref, ya_ref, s_ref, h_ref, shift_scr = refs
    tile = x_ref.shape[0]
    cc = CONV_A_CHUNK
    h = _rms_bf16(x_ref[...], g_ref[...])
    h_ref[...] = h
    if not sample:
        @pl.when(pl.program_id(0) == 0)
        def _():
            s_ref[...] = jnp.zeros_like(s_ref)
    for c in range(HALF // cc):
        lanes = slice(c * cc, (c + 1) * cc)
        a_b, a_c, a_h, a_g = (_dot_w(h, w_ref[:, j * HALF + c * cc:j * HALF + (c + 1) * cc]) for j in range(4))
        s = a_c * a_h
        if sample:
            t_in = lax.broadcasted_iota(jnp.int32, s.shape, 0) % DEC_SEQ
            old0 = st_ref[:, c * cc:(c + 1) * cc]
            old1 = st_ref[:, HALF + c * cc:HALF + (c + 1) * cc]
            p1 = jnp.where(t_in >= 1, pltpu.roll(s, 1, 0), 0.0) + _place_steps(tile, [(0, old1)])
            p2 = jnp.where(t_in >= 2, pltpu.roll(s, 2, 0), 0.0) + _place_steps(tile, [(0, old0), (1, old1)])
            s_ref[:, lanes] = s
        else:
            shift_scr[0:8, :] = s_ref[:, lanes]
            shift_scr[8:8 + tile, :] = s
            p1 = shift_scr[7:7 + tile, :]
            p2 = shift_scr[6:6 + tile, :]
            s_ref[:, lanes] = s[tile - 8:tile, :]
        conv = p2 * cw_ref[0:1, lanes] + p1 * cw_ref[1:2, lanes] + s * cw_ref[2:3, lanes]
        ya_ref[:, lanes] = (a_b * conv * _silu(a_g)).astype(BF16)


def _conv_a_rowspace_kernel(xp_ref, xs_ref, g_ref, w_ref, cw_ref, st_ref, ya_ref, tail_ref, ss_ref, h_ref, shift_scr,
                            *, n_p):
    is_sample = pl.program_id(0) == n_p

    @pl.when(jnp.logical_not(is_sample))
    def _():
        _conv_a_kernel(xp_ref, g_ref, w_ref, cw_ref, ya_ref, tail_ref, h_ref, shift_scr, sample=False)

    @pl.when(is_sample)
    def _():
        _conv_a_kernel(xs_ref, g_ref, w_ref, cw_ref, st_ref, ya_ref, ss_ref, h_ref, sample=True)


def _conv_a(x_p, x_s, g_pre, w0, conv_w, state):
    tile = x_s.shape[0]
    n_p = x_p.shape[0] // tile
    t = (n_p + 1) * tile
    return pl.pallas_call(
        functools.partial(_conv_a_rowspace_kernel, n_p=n_p),
        grid=(n_p + 1,),
        in_specs=[_prompt_rows(tile, D_MODEL, n_p), _resident((tile, D_MODEL)), _resident((1, D_MODEL)),
                  _cols(D_MODEL, 4 * HALF, 0), _resident((3, HALF)), _resident(state.shape)],
        out_specs=[_rows(tile, HALF), _resident((8, HALF)), _resident((tile, HALF)), _rows(tile, D_MODEL)],
        out_shape=[jax.ShapeDtypeStruct((t, HALF), BF16), jax.ShapeDtypeStruct((8, HALF), F32),
                   jax.ShapeDtypeStruct((tile, HALF), F32), jax.ShapeDtypeStruct((t, D_MODEL), BF16)],
        scratch_shapes=[pltpu.VMEM((8 + tile, CONV_A_CHUNK), F32)],
        compiler_params=pltpu.CompilerParams(dimension_semantics=("arbitrary",), vmem_limit_bytes=CONV_A_VMEM_LIMIT),
        name="conv_a",
    )(x_p, x_s, g_pre, w0, conv_w, state)


def _rel_bucket(dist):
    max_exact = NUM_BUCKETS // 2
    d = jnp.maximum(dist, 0)
    ratio = jnp.maximum(d, max_exact).astype(F32) / max_exact
    large = max_exact + (jnp.log(ratio) / math.log(MAX_DISTANCE / max_exact)
                         * (NUM_BUCKETS - max_exact)).astype(jnp.int32)
    return jnp.where(d < max_exact, d, jnp.minimum(large, NUM_BUCKETS - 1))


def _attn_softmax_pv(s, sink, v_bf, v_transposed=False):
    m = jnp.maximum(jnp.max(s, axis=-1, keepdims=True), sink)
    p = jnp.exp(s - m)
    den = jnp.sum(p, axis=-1, keepdims=True) + jnp.exp(sink - m)
    pv = _dot_nt(p.astype(BF16), v_bf) if v_transposed else _dot(p.astype(BF16), v_bf)
    return pv / den


def _attn_prompt_kernel(h_ref, wq_ref, wkv_ref, wg0_ref, wg1_ref, tab_ref, sink_ref, yb_ref, kwin_ref,
                        vwin_ref, q_scr, gate_scr, k_scr, v_scr, bias_scr):
    tile = h_ref.shape[0]
    i = pl.program_id(0)
    kv_w = N_KV * HEAD_DIM
    h = h_ref[...]

    kw, vw = 2 * HEAD_DIM, 4 * HEAD_DIM

    @pl.when(i == 0)
    def _():
        k_scr[0:WINDOW, :] = jnp.zeros((WINDOW, N_KV * kw), BF16)
        v_scr[0:WINDOW, :] = jnp.zeros((WINDOW, N_KV * vw), BF16)
        for hk in range(N_KV):
            v_scr[:, hk * vw + kw:(hk + 1) * vw] = jnp.ones((tile + WINDOW, kw), BF16)
        in_own = lax.broadcasted_iota(jnp.int32, (WINDOW, 2 * WINDOW), 1) >= WINDOW
        for head in range(N_HEADS):
            row = jnp.broadcast_to(tab_ref[head:head + 1, :], (WINDOW, BIAS_SPAN))
            band = pltpu.roll(row, 0, 1, stride=1, stride_axis=0)[:, 0:2 * WINDOW]
            rows = slice((head % 2) * WINDOW, (head % 2 + 1) * WINDOW)
            bias_scr[1, head // 2, rows, :] = band
            bias_scr[0, head // 2, rows, :] = jnp.where(in_own, band, MASK_VALUE)

    q_scr[...] = (_dot_w(h, wq_ref[...]) * (HEAD_DIM ** -0.5)).astype(BF16)
    k = _dot_w(h, wkv_ref[:, 0:kv_w])
    v = _dot_w(h, wkv_ref[:, kv_w:2 * kv_w])
    gate_scr[:, 0:HALF // 2] = _silu(_dot_w(h, wg0_ref[...]))
    gate_scr[:, HALF // 2:HALF] = _silu(_dot_w(h, wg1_ref[...]))
    for hk in range(N_KV):
        k_h = k[:, hk * HEAD_DIM:(hk + 1) * HEAD_DIM].astype(BF16)
        v_h = v[:, hk * HEAD_DIM:(hk + 1) * HEAD_DIM].astype(BF16)
        k_scr[WINDOW:WINDOW + tile, hk * kw:(hk + 1) * kw] = jnp.concatenate([k_h, k_h], axis=1)
        v_scr[WINDOW:WINDOW + tile, hk * vw:hk * vw + kw] = jnp.concatenate([v_h, v_h], axis=1)
    kwin_ref[...] = k[tile - WINDOW:tile, :]
    vwin_ref[...] = v[tile - WINDOW:tile, :]

    lane = lax.broadcasted_iota(jnp.int32, (WINDOW, kw), 1)
    lo = lane < HEAD_DIM
    keep_a = jnp.where(lo, 1.0, 0.0).astype(BF16)
    keep_b = jnp.where(lo, 0.0, 1.0).astype(BF16)
    is_a = lax.broadcasted_iota(jnp.int32, (2 * WINDOW, 1), 0) < WINDOW

    def block(n, carry):
        r0 = pl.multiple_of(n * WINDOW, WINDOW)
        rows = pl.ds(r0, WINDOW)
        keys = pl.ds(r0, 2 * WINDOW)
        first = jnp.where(jnp.logical_and(i == 0, n == 0), 0, 1)
        for hk in range(N_KV):
            for gp in range(GROUP // 2):
                a = hk * GROUP + 2 * gp
                slab = slice(a * HEAD_DIM, (a + 2) * HEAD_DIM)
                q2 = q_scr[rows, slab]
                lhs = jnp.concatenate([q2 * keep_a, q2 * keep_b], axis=0)
                s = _dot_nt(lhs, k_scr[keys, hk * kw:(hk + 1) * kw]) + bias_scr[first, a // 2]
                sink = jnp.where(is_a, sink_ref[a], sink_ref[a + 1])
                m = jnp.maximum(jnp.max(s, axis=-1, keepdims=True), sink)
                p = jnp.exp(s - m).astype(BF16)
                pv = _dot(p, v_scr[keys, hk * vw:(hk + 1) * vw])
                num = jnp.where(lo, pv[0:WINDOW, 0:kw], pv[WINDOW:2 * WINDOW, 0:kw])
                den = jnp.where(lo, pv[0:WINDOW, kw:2 * kw], pv[WINDOW:2 * WINDOW, kw:2 * kw])
                m_slab = jnp.where(lo, m[0:WINDOW], m[WINDOW:2 * WINDOW])
                den = den + jnp.exp(jnp.where(lo, sink_ref[a], sink_ref[a + 1]) - m_slab)
                yb_ref[rows, slab] = (num / den * gate_scr[rows, slab]).astype(BF16)
        return carry

    lax.fori_loop(0, tile // WINDOW, block, 0, unroll=True)
    k_scr[0:WINDOW, :] = k_scr[tile:tile + WINDOW, :]
    v_scr[0:WINDOW, :] = v_scr[tile:tile + WINDOW, :]


BIAS_SPAN = 3 * WINDOW


def _prompt_bias_table(rel_bias):
    dist = WINDOW - jnp.arange(BIAS_SPAN)
    table = jnp.where(((dist >= 0) & (dist < WINDOW))[:, None], rel_bias.astype(F32)[_rel_bucket(dist)], MASK_VALUE)
    return table.T


def _attn_proj_kernel(h_ref, wq_ref, wkv_ref, wg0_ref, wg1_ref, qg_ref, kt_ref, vt_ref, kv_scr):
    kv_w = N_KV * HEAD_DIM
    h = h_ref[...]
    q = _dot_w(h, wq_ref[...]) * (HEAD_DIM ** -0.5)
    for hk in range(N_KV):
        for g in range(GROUP):
            src = (hk * GROUP + g) * HEAD_DIM
            dst = (g * N_KV + hk) * HEAD_DIM
            qg_ref[:, dst:dst + HEAD_DIM] = q[:, src:src + HEAD_DIM]
    qg_ref[:, HALF:HALF + HALF // 2] = _dot_w(h, wg0_ref[...])
    qg_ref[:, HALF + HALF // 2:2 * HALF] = _dot_w(h, wg1_ref[...])
    kv_scr[...] = _dot_w(h, wkv_ref[...])
    for j in range(kt_ref.shape[0]):
        kt_ref[j] = kv_scr[j * WINDOW:(j + 1) * WINDOW, 0:kv_w].T
        vt_ref[j] = kv_scr[j * WINDOW:(j + 1) * WINDOW, kv_w:2 * kv_w].T


def _attn_rowspace_kernel(h_ref, wq_ref, wkv_ref, wg0_ref, wg1_ref, tab_ref, sink_ref,
                          yb_ref, kwin_ref, vwin_ref, qg_ref, kt_ref, vt_ref,
                          q_scr, gate_scr, k_scr, v_scr, bias_scr, kv_scr, *, n_p):
    is_sample = pl.program_id(0) == n_p

    @pl.when(jnp.logical_not(is_sample))
    def _():
        _attn_prompt_kernel(h_ref, wq_ref, wkv_ref, wg0_ref, wg1_ref, tab_ref, sink_ref, yb_ref, kwin_ref, vwin_ref,
                            q_scr, gate_scr, k_scr, v_scr, bias_scr)

    @pl.when(is_sample)
    def _():
        _attn_proj_kernel(h_ref, wq_ref, wkv_ref, wg0_ref, wg1_ref, qg_ref, kt_ref, vt_ref, kv_scr)


def _attn(h_all, n_p, tile, w0, table, sinks):
    kv_w = N_KV * HEAD_DIM
    t = n_p * tile
    win_spec = pl.BlockSpec((WINDOW, kv_w), lambda i: (0, 0))
    out_shape = [jax.ShapeDtypeStruct((t, HALF), BF16),
                 jax.ShapeDtypeStruct((WINDOW, kv_w), F32), jax.ShapeDtypeStruct((WINDOW, kv_w), F32),
                 jax.ShapeDtypeStruct((tile, 2 * HALF), F32), jax.ShapeDtypeStruct((tile // WINDOW, kv_w, WINDOW), F32),
                 jax.ShapeDtypeStruct((tile // WINDOW, kv_w, WINDOW), F32)]
    return pl.pallas_call(
        functools.partial(_attn_rowspace_kernel, n_p=n_p),
        grid=(n_p + 1,),
        in_specs=[_rows(tile, D_MODEL),
                  _cols(D_MODEL, HALF, 4), _cols(D_MODEL, 2 * kv_w, 10),
                  _cols(D_MODEL, HALF // 2, 11), _cols(D_MODEL, HALF // 2, 12),
                  _resident(table.shape), pl.BlockSpec(memory_space=pltpu.SMEM)],
        out_specs=[_prompt_rows(tile, HALF, n_p), win_spec, win_spec] + [_resident(s.shape) for s in out_shape[3:]],
        out_shape=out_shape,
        scratch_shapes=[pltpu.VMEM((tile, HALF), BF16), pltpu.VMEM((tile, HALF), F32),
                        pltpu.VMEM((tile + WINDOW, 2 * kv_w), BF16), pltpu.VMEM((tile + WINDOW, 4 * kv_w), BF16),
                        pltpu.VMEM((2, N_HEADS // 2, 2 * WINDOW, 2 * WINDOW), F32),
                        pltpu.VMEM((tile, 2 * kv_w), F32)],
        compiler_params=_params(),
        name="attn",
    )(h_all, w0, w0, w0, w0, table, sinks)


ATTN_S_BATCH = 32
KEYS_PAD = 2 * WINDOW


def _attn_sample_kernel(qg_ref, ktn_ref, vtn_ref, ck_ref, cv_ref, bias_ref, sink_ref, yb_ref, nk_ref, nv_ref):
    kv_w = N_KV * HEAD_DIM
    row8 = lax.broadcasted_iota(jnp.int32, (8, kv_w), 0)
    lane_head = lax.broadcasted_iota(jnp.int32, (8, kv_w), 1) // HEAD_DIM
    lower = row8 < DEC_SEQ
    pick = [jnp.where(lane_head == 2 * hp + jnp.where(lower, 0, 1), 1.0, 0.0).astype(F32) for hp in range(2)]
    lower_w = lax.broadcasted_iota(jnp.int32, (8, HALF), 0) < DEC_SEQ
    kept = lax.broadcasted_iota(jnp.int32, (kv_w, WINDOW), 1) < WINDOW - DEC_SEQ
    seq0 = pl.program_id(0) * ATTN_S_BATCH
    per_tile = WINDOW // DEC_SEQ

    def slide(old, new_tile, shift):
        return jnp.where(kept, pltpu.roll(old, WINDOW - DEC_SEQ, 1), pltpu.roll(new_tile, shift, 1))

    def pair(p, carry):
        r0 = pl.multiple_of(p * 8, 8)
        rows = qg_ref[pl.ds(r0, 8), :]
        q8 = rows[:, 0:HALF]
        gate8 = rows[:, HALF:2 * HALF]
        out8 = []
        for sub in range(2):
            b = 2 * p + sub
            q_swap = pltpu.roll(q8, 4, 0)
            q_dup = jnp.where(lower_w, q8, q_swap) if sub == 0 else jnp.where(lower_w, q_swap, q8)
            tile = (seq0 + b) // per_tile
            shift = (2 * WINDOW - DEC_SEQ - DEC_SEQ * ((seq0 + b) % per_tile)) % WINDOW
            k_old = ck_ref[b].reshape(kv_w, WINDOW)
            v_old = cv_ref[b].reshape(kv_w, WINDOW)
            k_win = slide(k_old, ktn_ref[tile], shift)
            v_win = slide(v_old, vtn_ref[tile], shift)
            nk_ref[b] = k_win.reshape(N_KV, HEAD_DIM, WINDOW)
            nv_ref[b] = v_win.reshape(N_KV, HEAD_DIM, WINDOW)
            k_all = jnp.concatenate([k_old, k_win], axis=1).astype(BF16)
            v_all = jnp.concatenate([v_old, v_win], axis=1).astype(BF16)
            q_bd = jnp.concatenate(
                [q_dup[:, g * kv_w:(g + 1) * kv_w] * pick[hp] for g in range(GROUP) for hp in range(2)], axis=0)
            s = _dot(q_bd.astype(BF16), k_all) + bias_ref[...]
            o = _attn_softmax_pv(s, sink_ref[:, 0:1], v_all, v_transposed=True)
            out_g = []
            for g in range(GROUP):
                acc = None
                for hp in range(2):
                    piece = o[(2 * g + hp) * 8:(2 * g + hp + 1) * 8, :] * pick[hp]
                    piece = piece + pltpu.roll(piece, 4, 0)
                    acc = piece if acc is None else acc + piece
                out_g.append(acc)
            out8.append(jnp.concatenate(
                [out_g[g][:, hk * HEAD_DIM:(hk + 1) * HEAD_DIM] for hk in range(N_KV) for g in range(GROUP)], axis=1))
        o8 = jnp.where(lower_w, out8[0], out8[1])
        yb_ref[pl.ds(r0, 8), :] = (o8 * _silu(gate8)).astype(BF16)
        return carry

    lax.fori_loop(0, ATTN_S_BATCH // 2, pair, 0, unroll=2)


def _sample_bias(rel_bias, sinks):
    t = jnp.arange(DEC_SEQ)[:, None]
    j = jnp.arange(KEYS_PAD)[None, :]
    pos = jnp.where(j < WINDOW, j, j - (KEYS_PAD - DEC_SEQ) + WINDOW)
    dist = t + WINDOW - pos
    valid = (dist >= 0) & (dist < WINDOW) & ((j < WINDOW) | (j >= KEYS_PAD - DEC_SEQ))
    bias = jnp.where(valid[:, :, None], rel_bias.astype(F32)[_rel_bucket(dist)], MASK_VALUE)
    bias = bias.reshape(DEC_SEQ, KEYS_PAD, N_KV, GROUP).transpose(3, 2, 0, 1).reshape(N_HEADS * DEC_SEQ, KEYS_PAD)
    sink = jnp.broadcast_to(sinks.astype(F32).reshape(N_KV, GROUP).T[:, :, None], (GROUP, N_KV, DEC_SEQ))
    return bias, jnp.broadcast_to(sink.reshape(N_HEADS * DEC_SEQ, 1), (N_HEADS * DEC_SEQ, 128))


def _attn_sample(qg, kt_new, vt_new, cache_kt, cache_vt, bias, sink):
    n_seq = cache_kt.shape[0]
    bb = ATTN_S_BATCH
    cache_spec = pl.BlockSpec((bb, N_KV, HEAD_DIM, WINDOW), lambda i: (i, 0, 0, 0))
    return pl.pallas_call(
        _attn_sample_kernel,
        grid=(n_seq // bb,),
        in_specs=[_rows(bb * DEC_SEQ, 2 * HALF), _resident(kt_new.shape), _resident(vt_new.shape),
                  cache_spec, cache_spec, _resident(bias.shape), _resident(sink.shape)],
        out_specs=[_rows(bb * DEC_SEQ, HALF), cache_spec, cache_spec],
        out_shape=[jax.ShapeDtypeStruct((n_seq * DEC_SEQ, HALF), BF16),
                   jax.ShapeDtypeStruct(cache_kt.shape, F32), jax.ShapeDtypeStruct(cache_vt.shape, F32)],
        compiler_params=_params(),
        name="attn_sample",
    )(qg, kt_new, vt_new, cache_kt, cache_vt, bias, sink)


def _prep_layer0(g_pre, w_in, conv_w, rel_bias, sinks, w_out, g_post):
    return dict(
        g_pre=g_pre.reshape(1, D_MODEL), w0=w_in, conv_w=conv_w, rel_bias=rel_bias, sinks=sinks,
        w_out=w_out, g_post=g_post.reshape(1, D_MODEL))


def _layer0(x_p, x_s, conv_state, cache_k, cache_v, w):
    n_seq = x_s.shape[0]
    rows = x_s.reshape(n_seq * DEC_SEQ, D_MODEL)
    tile = rows.shape[0]
    n_p = x_p.shape[0] // tile
    ya_all, s_tail, s_s, h_all = _conv_a(x_p, rows, w['g_pre'], w['w0'], w['conv_w'],
                                         conv_state.reshape(n_seq, 2 * HALF))
    yb_p, kwin, vwin, qg, kt_new, vt_new = _attn(h_all, n_p, tile, w['w0'], _prompt_bias_table(w['rel_bias']),
                                                 w['sinks'])
    bias, sink = _sample_bias(w['rel_bias'], w['sinks'])
    yb_s, new_kt, new_vt = _attn_sample(qg, kt_new, vt_new, cache_k.transpose(0, 2, 3, 1),
                                        cache_v.transpose(0, 2, 3, 1), bias, sink)
    (y_all,) = _out_proj(ya_all, (yb_p, yb_s), (x_p, rows), w['w_out'], w['g_post'], n_p, tile, out_pair=False)
    prompt_state = (s_tail[6:8], kwin.reshape(WINDOW, N_KV, HEAD_DIM), vwin.reshape(WINDOW, N_KV, HEAD_DIM))
    sample_state = (s_s.reshape(n_seq, DEC_SEQ, HALF)[:, DEC_SEQ - 2:], new_kt.transpose(0, 3, 1, 2),
                    new_vt.transpose(0, 3, 1, 2))
    return y_all, prompt_state, sample_state


def _layer_norm(v, g, b):
    xc = v - jnp.mean(v, axis=-1, keepdims=True)
    return xc * lax.rsqrt(jnp.mean(xc * xc, axis=-1, keepdims=True) + NORM_EPS) * g + b


def _cmlp_prompt_kernel(x_ref, g_ref, w_ref, lng_ref, lnb_ref, ws_ref, bs_ref, yc_ref, h_ref, vn_scr):
    tile = x_ref.shape[0]
    h = _rms_bf16(x_ref[...], g_ref[...])
    h_ref[...] = h
    v = _dot_wt(h, w_ref[HALF:2 * HALF, :])
    vn_scr[...] = _layer_norm(v, lng_ref[...], lnb_ref[...]).astype(BF16)
    gw = HALF // CMLP_GROUPS
    cols = 2 * gw
    for cb in range(HALF // cols):
        u = _dot_wt(h, w_ref[cb * cols:(cb + 1) * cols, :])
        gate = _silu(_dot_wt(h, w_ref[2 * HALF + cb * cols:2 * HALF + (cb + 1) * cols, :]))
        for gi in range(2):
            grp = 2 * cb + gi
            lanes = slice(grp * gw, (grp + 1) * gw)
            for n in range(tile // CHUNK):
                rows = slice(n * CHUNK, (n + 1) * CHUNK)
                mixed = _dot(ws_ref[grp], vn_scr[rows, lanes]) + bs_ref[grp]
                yc_ref[rows, lanes] = (u[rows, gi * gw:(gi + 1) * gw] * mixed
                                       * gate[rows, gi * gw:(gi + 1) * gw]).astype(BF16)


def _cmlp_kernel(x_ref, g_ref, w_ref, lng_ref, lnb_ref, ws_ref, bs_ref, coef_ref, bias_ref, yc_ref, h_ref, vns_ref,
                 vn_scr, *, n_p):
    is_sample = pl.program_id(0) == n_p

    @pl.when(jnp.logical_not(is_sample))
    def _():
        _cmlp_prompt_kernel(x_ref, g_ref, w_ref, lng_ref, lnb_ref, ws_ref, bs_ref, yc_ref, h_ref, vn_scr)

    @pl.when(is_sample)
    def _():
        _cmlp_sample_kernel(x_ref, g_ref, w_ref, lng_ref, lnb_ref, coef_ref, bias_ref, yc_ref, vns_ref, h_ref)


def _cmlp(x_all, n_p, tile, w):
    t = x_all.shape[0]
    consts = [w['ln_g'], w['ln_b'], w['ws_tril'], w['bs_rows'], w['coef'], w['bias4']]
    return pl.pallas_call(
        functools.partial(_cmlp_kernel, n_p=n_p),
        grid=(n_p + 1,),
        in_specs=[_rows(tile, D_MODEL), _resident((1, D_MODEL)), _rowwin(3 * HALF, D_MODEL, 0)]
                 + [_resident(c.shape) for c in consts],
        out_specs=[_rows(tile, HALF), _rows(tile, D_MODEL), _resident((tile, HALF))],
        out_shape=[jax.ShapeDtypeStruct((t, HALF), BF16), jax.ShapeDtypeStruct((t, D_MODEL), BF16),
                   jax.ShapeDtypeStruct((tile, HALF), F32)],
        scratch_shapes=[pltpu.VMEM((tile, HALF), BF16)],
        compiler_params=_params(),
        name="cmlp",
    )(x_all, w['g_pre'], w['w1'], *consts)


def _cmlp_sample_kernel(x_ref, g_ref, w_ref, lng_ref, lnb_ref, coef_ref, bias_ref, yc_ref, vn_ref, h_ref):
    t = x_ref.shape[0]
    h = _rms_bf16(x_ref[...], g_ref[...])
    h_ref[...] = h
    u = _dot_wt(h, w_ref[0:HALF, :])
    vn = _layer_norm(_dot_wt(h, w_ref[HALF:2 * HALF, :]), lng_ref[...], lnb_ref[...])
    gate = _silu(_dot_wt(h, w_ref[2 * HALF:3 * HALF, :]))
    vn_ref[...] = vn

    def tiled(a):
        return a.reshape(t // 8, 8, HALF)

    mixed = tiled(vn) * coef_ref[0][None] + bias_ref[...][None]
    for k in range(1, DEC_SEQ):
        mixed = mixed + tiled(pltpu.roll(vn, k, 0)) * coef_ref[k][None]
    yc_ref[...] = (u * mixed.reshape(t, HALF) * gate).astype(BF16)


HEAD_LANES = 128
SSD_GW = HALF // SSD_GROUPS


def _softplus(x):
    return jnp.maximum(x, 0.0) + jnp.log1p(jnp.exp(-jnp.abs(x)))


def _dt_proj(h, wdt_ref):
    pad = jnp.zeros((HEAD_LANES - SSD_HEADS, D_MODEL), F32)
    return _dot_wt(h, jnp.concatenate([wdt_ref[...], pad], axis=0))


def _group_norm_gate(y, z, gn):
    gated = y * _silu(z)
    parts = []
    for g in range(SSD_GROUPS):
        part = gated[:, g * SSD_GW:(g + 1) * SSD_GW]
        parts.append(part * lax.rsqrt(jnp.mean(part * part, axis=-1, keepdims=True) + NORM_EPS))
    return (jnp.concatenate(parts, axis=1) * gn).astype(BF16)


def _ssd_prompt_kernel(h_ref, wz_ref, wx0_ref, wx1_ref, wx2_ref, wdt_ref, cw_ref, cb_ref, dtb_ref, alog_ref,
                       dskip_ref, gn_ref, e3_ref, tril3_ref, yd_ref, tail_ref, ssm_ref,
                       xbc_scr, z_scr, dt_scr, ht_scr, shift_scr):
    tile = h_ref.shape[0]
    i = pl.program_id(0)
    cd = SSD_CONV_DIM
    h = h_ref[...]

    @pl.when(i == 0)
    def _():
        tail_ref[...] = jnp.zeros_like(tail_ref)
        ht_scr[...] = jnp.zeros_like(ht_scr)

    z_scr[...] = _dot_wt(h, wz_ref[...])
    dt_scr[...] = _softplus(_dt_proj(h, wdt_ref) + dtb_ref[...])
    third = cd // 3
    for j, wx_ref in enumerate((wx0_ref, wx1_ref, wx2_ref)):
        cols = slice(j * third, (j + 1) * third)
        raw = _dot_wt(h, wx_ref[...])
        shift_scr[0:8, :] = tail_ref[:, cols]
        shift_scr[8:8 + tile, :] = raw
        conv = raw * cw_ref[3:4, cols] + cb_ref[:, cols]
        for k in range(1, 4):
            conv = conv + shift_scr[8 - k:8 - k + tile, :] * cw_ref[3 - k:4 - k, cols]
        xbc_scr[:, cols] = _silu(conv)
        tail_ref[:, cols] = raw[tile - 8:tile, :]

    a16 = -jnp.exp(alog_ref[...])
    causal =(lax.broadcasted_iota(jnp.int32, (CHUNK, CHUNK), 0)
              >= lax.broadcasted_iota(jnp.int32, (CHUNK, CHUNK), 1))
    first_half = lax.broadcasted_iota(jnp.int32, (CHUNK, 2 * HEAD_DIM), 1) < HEAD_DIM
    keep_a = jnp.where(first_half, 1.0, 0.0).astype(BF16)
    keep_b = jnp.where(first_half, 0.0, 1.0).astype(BF16)

    def chunk(n, carry):
        r0 = pl.multiple_of(n * CHUNK, CHUNK)
        rows = pl.ds(r0, CHUNK)
        xs = xbc_scr[rows, 0:HALF]
        dt16 = dt_scr[rows, :]
        dt_e = _dot(jnp.concatenate(_split3(dt16), axis=1), e3_ref[...])
        acs16 = _dot(tril3_ref[...], jnp.concatenate(_split3(dt16 * a16), axis=0))
        acs_e = _dot(jnp.concatenate(_split3(acs16), axis=1), e3_ref[...])
        acs_t = acs16.T
        last_e = acs_e[CHUNK - 1:CHUNK, :]
        xdt = xs * dt_e
        xdt_bf = xdt.astype(BF16)
        xw = (jnp.exp(last_e - acs_e) * xdt).astype(BF16)
        dec_e = jnp.exp(last_e)
        y_parts = []
        yoff_parts = []
        for g in range(SSD_GROUPS):
            c_g = xbc_scr[rows, HALF + 2 * SSD_STATE + g * SSD_STATE:HALF + 2 * SSD_STATE + (g + 1) * SSD_STATE].astype(BF16)
            b_g = xbc_scr[rows, HALF + g * SSD_STATE:HALF + (g + 1) * SSD_STATE].astype(BF16)
            cb = _dot_nt(c_g, b_g)
            h_prev = ht_scr[g]
            yoff_parts.append(_dot(c_g, h_prev.astype(BF16)))
            for r in range(0, SSD_HEADS // SSD_GROUPS, 2):
                wgt = []
                for hd in (g * (SSD_HEADS // SSD_GROUPS) + r, g * (SSD_HEADS // SSD_GROUPS) + r + 1):
                    seg = acs16[:, hd:hd + 1] - acs_t[hd:hd + 1, :]
                    wgt.append(cb * jnp.exp(jnp.where(causal, seg, -jnp.inf)))
                a = g * (SSD_HEADS // SSD_GROUPS) + r
                slab = xdt_bf[:, a * HEAD_DIM:(a + 2) * HEAD_DIM]
                rhs = jnp.concatenate([slab * keep_a, slab * keep_b], axis=0)
                y_parts.append(_dot(jnp.concatenate(wgt, axis=1).astype(BF16), rhs))
            lanes = slice(g * SSD_GW, (g + 1) * SSD_GW)
            ht_scr[g] = h_prev * dec_e[:, lanes] + _dot_tn(b_g, xw[:, lanes])
        y = (jnp.concatenate(y_parts, axis=1) + jnp.concatenate(yoff_parts, axis=1) * jnp.exp(acs_e)
             + dskip_ref[...] * xs)
        yd_ref[rows, :] = _group_norm_gate(y, z_scr[rows, :], gn_ref[...])
        return carry

    lax.fori_loop(0, tile // CHUNK, chunk, 0, unroll=True)

    @pl.when(i == pl.num_programs(0) - 1)
    def _():
        for g in range(SSD_GROUPS):
            ssm_ref[g * SSD_GW:(g + 1) * SSD_GW, :] = ht_scr[g].T


def _ssd_weight_specs():
    third = SSD_CONV_DIM // 3
    first = 4 * HALF // third
    return ([_rowwin(HALF, D_MODEL, 3)] + [_rowwin(third, D_MODEL, first + j) for j in range(3)]
            + [_rowwin(SSD_HEADS, D_MODEL, (4 * HALF + SSD_CONV_DIM) // SSD_HEADS)])


def _ssd_prompt(h, n_p, tile, w):
    t = n_p * tile
    cd = SSD_CONV_DIM
    consts = [w['conv_w'], w['conv_b'], w['dt_bias16'], w['a_log16'],
              w['d_skip_e'], w['gate_norm_g'], w['expand3'], w['tril3']]
    return pl.pallas_call(
        _ssd_prompt_kernel,
        grid=(t // tile,),
        in_specs=[_rows(tile, D_MODEL)] + _ssd_weight_specs() + [_resident(c.shape) for c in consts],
        out_specs=[_rows(tile, HALF), pl.BlockSpec((8, cd), lambda i: (0, 0)),
                   pl.BlockSpec((HALF, SSD_STATE), lambda i: (0, 0))],
        out_shape=[jax.ShapeDtypeStruct((t, HALF), BF16), jax.ShapeDtypeStruct((8, cd), F32),
                   jax.ShapeDtypeStruct((HALF, SSD_STATE), F32)],
        scratch_shapes=[pltpu.VMEM((tile, cd), F32), pltpu.VMEM((tile, HALF), F32),
                        pltpu.VMEM((tile, HEAD_LANES), F32), pltpu.VMEM((SSD_GROUPS, SSD_STATE, SSD_GW), F32),
                        pltpu.VMEM((8 + tile, cd // 3), F32)],
        compiler_params=_params(),
        name="ssd_prompt",
    )(h, w['w1'], w['w1'], w['w1'], w['w1'], w['w1'], *consts)


def _ssd_sample_pre_kernel(h_ref, wz_ref, wx0_ref, wx1_ref, wx2_ref, wdt_ref, cw_ref, cb_ref, st_ref,
                           dtb_ref, aloge_ref, dskip_ref, e3_ref, seg_ref,
                           nconv_ref, z_ref, ysk_ref, eacs_ref, xw_ref, dec_ref, b_ref, c_ref, raw_scr):
    t = h_ref.shape[0]
    n_seq = t // DEC_SEQ
    h = h_ref[...]
    z_ref[...] = _dot_wt(h, wz_ref[...])
    raw = jnp.concatenate([_dot_wt(h, wx0_ref[...]), _dot_wt(h, wx1_ref[...]), _dot_wt(h, wx2_ref[...])], axis=1)
    for c in range(raw_scr.shape[0]):
        lanes = slice(c * 128, (c + 1) * 128)
        raw_scr[c] = raw[:, lanes]
        for j in range(3):
            nconv_ref[j, :, lanes] = raw_scr[c, pl.ds(j + 1, n_seq, stride=DEC_SEQ), :]
    dt16 = _softplus(_dt_proj(h, wdt_ref) + dtb_ref[...])
    dt = _dot(jnp.concatenate(_split3(dt16), axis=1), e3_ref[...])
    old = [st_ref[j] for j in range(3)]
    p1 = _place_steps(t, [(0, old[2])])
    p2 = _place_steps(t, [(0, old[1]), (1, old[2])])
    p3 = _place_steps(t, [(0, old[0]), (1, old[1]), (2, old[2])])

    def step_of(width):
        return lax.broadcasted_iota(jnp.int32, (t, width), 0) % DEC_SEQ

    def back(a, k):
        return jnp.where(step_of(a.shape[1]) >= k, pltpu.roll(a, k, 0), 0.0)

    def ahead(a, k):
        return jnp.where(step_of(a.shape[1]) + k < DEC_SEQ, pltpu.roll(a, t - k, 0), 0.0)

    conv = (raw * cw_ref[3:4, :] + (back(raw, 1) + p1) * cw_ref[2:3, :]
            + (back(raw, 2) + p2) * cw_ref[1:2, :] + (back(raw, 3) + p3) * cw_ref[0:1, :]
            + cb_ref[...])
    xbc = _silu(conv)
    xs = xbc[:, 0:HALF]
    bm = xbc[:, HALF:HALF + 2 * SSD_STATE]
    cm = xbc[:, HALF + 2 * SSD_STATE:]
    b_ref[...] = bm
    c_ref[...] = cm
    da = dt * (-jnp.exp(aloge_ref[...]))
    acs = da + back(da, 1) + back(da, 2) + back(da, 3)
    suffix = ahead(da, 1) + ahead(da, 2) + ahead(da, 3)
    xdt = xs * dt
    y = _dot((cm * bm).astype(BF16), seg_ref[...]) * xdt
    for k in range(1, DEC_SEQ):
        cbk = _dot((cm * pltpu.roll(bm, k, 0)).astype(BF16), seg_ref[...])
        term = cbk * jnp.exp(acs - pltpu.roll(acs, k, 0)) * pltpu.roll(xdt, k, 0)
        y = y + jnp.where(step_of(HALF) >= k, term, 0.0)
    ysk_ref[...] = y + dskip_ref[...] * xs
    eacs_ref[...] = jnp.exp(acs)
    xw_ref[...] = jnp.exp(suffix) * xdt
    dec_ref[...] = jnp.exp(acs + suffix)


def _ssd_sample_pre(h, first_row, conv_state, w):
    t = conv_state.shape[1] * DEC_SEQ
    cd = SSD_CONV_DIM
    tile = min(SSD_PRE_ROWS, t)
    first = first_row // tile
    seqs = tile // DEC_SEQ
    state_spec = pl.BlockSpec((3, seqs, cd), lambda i: (0, i, 0))
    head = [w['conv_w'], w['conv_b']]
    tail = [w['dt_bias16'], w['a_log_e'], w['d_skip_e'], w['expand3'], w['seg_expand']]
    args = [h, w['w1'], w['w1'], w['w1'], w['w1'], w['w1']] + head + [conv_state] + tail
    wide = jax.ShapeDtypeStruct((t, HALF), F32)
    narrow = jax.ShapeDtypeStruct((t, 2 * SSD_STATE), F32)
    out_shape = [jax.ShapeDtypeStruct(conv_state.shape, F32), wide, wide, wide, wide, wide, narrow, narrow]
    return pl.pallas_call(
        _ssd_sample_pre_kernel,
        grid=(t // tile,),
        in_specs=[pl.BlockSpec((tile, D_MODEL), lambda i: (first + i, 0))] + _ssd_weight_specs()
                 + [_resident(c.shape) for c in head] + [state_spec] + [_resident(c.shape) for c in tail],
        out_specs=[state_spec] + [_rows(tile, HALF)] * 5 + [_rows(tile, 2 * SSD_STATE)] * 2,
        out_shape=out_shape,
        scratch_shapes=[pltpu.VMEM((cd // 128, tile, 128), F32)],
        compiler_params=_params(),
        name="ssd_sample_pre",
    )(*args)


SSD_S_BATCH = 16
SSD_PRE_ROWS = 256


def _ssd_sample_state_kernel(st_ref, c_ref, b_ref, xw_ref, dec_ref, eacs_ref, ysk_ref, z_ref, gn_ref,
                             yd_ref, nst_ref):
    row_n = lax.broadcasted_iota(jnp.int32, (8, SSD_STATE), 0)
    row_w = lax.broadcasted_iota(jnp.int32, (8, SSD_GW), 0)
    row_f = lax.broadcasted_iota(jnp.int32, (8, HALF), 0)
    ones_rows = jnp.where((row_n >= 4) & (row_n < 7), 1.0, 0.0).astype(BF16)
    hpg = SSD_HEADS // SSD_GROUPS

    def pair(p, carry):
        r0 = pl.multiple_of(p * 8, 8)
        rows = pl.ds(r0, 8)
        c8 = c_ref[rows, :].astype(BF16)
        b8 = b_ref[rows, :]
        xw8 = xw_ref[rows, :]
        dec8 = dec_ref[rows, :]
        yoff = []
        for sub in range(2):
            b = 2 * p + sub
            xw_own = xw8 if sub == 0 else pltpu.roll(xw8, 4, 0)
            b_own = b8 if sub == 0 else pltpu.roll(b8, 4, 0)
            hi, mid, lo = (term.astype(F32) for term in _split3(dec8[4 * sub:4 * sub + 1, :]))
            parts = []
            for g in range(SSD_GROUPS):
                lanes = slice(g * SSD_GW, (g + 1) * SSD_GW)
                heads = pl.ds(g * hpg, hpg)
                h0 = st_ref[b, heads].reshape(SSD_GW, SSD_STATE)
                parts.append(_dot_nt(c8[:, g * SSD_STATE:(g + 1) * SSD_STATE], h0.astype(BF16)))
                lhs = jnp.where(row_w < 4, xw_own[:, lanes],
                                jnp.where(row_w == 4, hi[:, lanes],
                                          jnp.where(row_w == 5, mid[:, lanes],
                                                    jnp.where(row_w == 6, lo[:, lanes], 0.0)))).astype(BF16)
                rhs_b = jnp.where(row_n < 4, b_own[:, g * SSD_STATE:(g + 1) * SSD_STATE], 0.0).astype(BF16)
                decay = _dot_tn(lhs, ones_rows)
                nst_ref[b, heads] = (h0 * decay + _dot_tn(lhs, rhs_b)).reshape(hpg, HEAD_DIM, SSD_STATE)
            yoff.append(jnp.concatenate(parts, axis=1))
        yoff8 = jnp.where(row_f < 4, yoff[0], yoff[1])
        y = ysk_ref[rows, :] + yoff8 * eacs_ref[rows, :]
        yd_ref[rows, :] = _group_norm_gate(y, z_ref[rows, :], gn_ref[...])
        return carry

    lax.fori_loop(0, SSD_S_BATCH // 2, pair, 0, unroll=4)


def _ssd_sample_state(state, cm, bm, xw, dec, eacs, ysk, z, gn):
    n_seq = state.shape[0]
    bb = SSD_S_BATCH
    r = bb * DEC_SEQ
    st_spec = pl.BlockSpec((bb, SSD_HEADS, HEAD_DIM, SSD_STATE), lambda i: (i, 0, 0, 0))
    return pl.pallas_call(
        _ssd_sample_state_kernel,
        grid=(n_seq // bb,),
        in_specs=[st_spec, _rows(r, 2 * SSD_STATE), _rows(r, 2 * SSD_STATE)] + [_rows(r, HALF)] * 5
                 + [_resident((1, HALF))],
        out_specs=[_rows(r, HALF), st_spec],
        out_shape=[jax.ShapeDtypeStruct((n_seq * DEC_SEQ, HALF), BF16), jax.ShapeDtypeStruct(state.shape, F32)],
        compiler_params=_params(),
        name="ssd_sample_state",
    )(state, cm, bm, xw, dec, eacs, ysk, z, gn)


def _prep_layer1(g_pre, w_in, ln_g, ln_b, w_s, b_s, conv_w, conv_b, dt_bias, a_log, d_skip, gate_norm_g,
                 w_out, g_post):
    cd = SSD_CONV_DIM
    gw = HALF // CMLP_GROUPS
    w1 = w_in.T

    def lanes16(v):
        return jnp.pad(v.astype(F32), (0, HEAD_LANES - SSD_HEADS)).reshape(1, HEAD_LANES)

    def per_channel(v):
        return jnp.repeat(v.astype(F32), HEAD_DIM).reshape(1, HALF)

    head_of = np.arange(HALF) // HEAD_DIM
    expand = jnp.asarray(np.arange(HEAD_LANES)[:, None] == head_of[None, :], BF16)
    tril = jnp.asarray(np.tril(np.ones((CHUNK, CHUNK))), BF16)
    grp_rows = np.arange(2 * SSD_STATE) // SSD_STATE
    seg_expand = jnp.asarray(grp_rows[:, None] == (head_of // (SSD_HEADS // SSD_GROUPS))[None, :], BF16)

    w4 = jnp.tril(w_s[:, :DEC_SEQ, :DEC_SEQ])
    steps = jnp.arange(DEC_SEQ)
    coef = []
    for k in range(DEC_SEQ):
        src = steps - k
        ck = jnp.where((src >= 0)[None, :], w4[:, steps, jnp.maximum(src, 0)], 0.0)
        ck = jnp.repeat(ck.T, gw, axis=1)
        coef.append(jnp.concatenate([ck, ck], axis=0))
    bias4 = jnp.repeat(b_s[:, :DEC_SEQ].T, gw, axis=1)
    return dict(
        g_pre=g_pre.reshape(1, D_MODEL), w1=w1, ln_g=ln_g.reshape(1, HALF), ln_b=ln_b.reshape(1, HALF),
        ws_tril=jnp.tril(w_s).astype(BF16),
        bs_rows=jnp.broadcast_to(b_s.astype(F32)[:, :, None], (CMLP_GROUPS, CHUNK, gw)),
        coef=jnp.stack(coef).astype(F32), bias4=jnp.concatenate([bias4, bias4], axis=0).astype(F32),
        conv_w=conv_w, conv_b=conv_b.reshape(1, cd), dt_bias16=lanes16(dt_bias), a_log16=lanes16(a_log),
        a_log_e=per_channel(a_log), d_skip_e=per_channel(d_skip), gate_norm_g=gate_norm_g.reshape(1, HALF),
        expand3=jnp.concatenate([expand] * 3, axis=0), tril3=jnp.concatenate([tril] * 3, axis=1),
        seg_expand=seg_expand, w_out=w_out, g_post=g_post.reshape(1, D_MODEL))


def _layer1(x_all, n_p, conv_state, ssm_state, w):
    n_seq = ssm_state.shape[0]
    tile = n_seq * DEC_SEQ
    yc_all, h_all, vn = _cmlp(x_all, n_p, tile, w)
    yd_p, tail, ssm = _ssd_prompt(h_all, n_p, tile, w)
    new_conv, z, ysk, eacs, xw, dec, bm, cm = _ssd_sample_pre(h_all, n_p * tile, conv_state.transpose(1, 0, 2), w)
    yd_s, new_state = _ssd_sample_state(ssm_state, cm, bm, xw, dec, eacs, ysk, z, w['gate_norm_g'])
    y_p, y_s = _out_proj(yc_all, (yd_p, yd_s), x_all, w['w_out'], w['g_post'], n_p, tile, out_pair=True)
    prompt_out = (y_p, tail[5:8], ssm.reshape(SSD_HEADS, HEAD_DIM, SSD_STATE))
    sample_out = (y_s.reshape(n_seq, DEC_SEQ, D_MODEL), vn.reshape(n_seq, DEC_SEQ, HALF),
                  new_conv.transpose(1, 0, 2), new_state)
    return prompt_out, sample_out


def kernel(x_prompt, x_sample, state_conv_a, cache_win_k, cache_win_v, state_conv_d, state_ssm, rel_bias,
           l0_g_pre, l0_w_in, l0_conv_w, l0_sinks, l0_w_out, l0_g_post,
           l1_g_pre, l1_w_in, l1_ln_g, l1_ln_b, l1_w_s, l1_b_s, l1_conv_w, l1_conv_b, l1_dt_bias, l1_a_log,
           l1_d_skip, l1_gate_norm_g, l1_w_out, l1_g_post):
    w0 = _prep_layer0(l0_g_pre, l0_w_in, l0_conv_w, rel_bias, l0_sinks, l0_w_out, l0_g_post)
    w1 = _prep_layer1(l1_g_pre, l1_w_in, l1_ln_g, l1_ln_b, l1_w_s, l1_b_s, l1_conv_w, l1_conv_b, l1_dt_bias,
                      l1_a_log, l1_d_skip, l1_gate_norm_g, l1_w_out, l1_g_post)
    x_p = x_prompt[0]
    n_p = x_p.shape[0] // (x_sample.shape[0] * DEC_SEQ)
    y_all, (p_conv_a, p_win_k, p_win_v), (s_conv_a, s_win_k, s_win_v) = _layer0(
        x_p, x_sample, state_conv_a, cache_win_k, cache_win_v, w0)
    (yp, p_conv_d, p_ssm), (ys, s_chunk_v, s_conv_d, s_ssm) = _layer1(y_all, n_p, state_conv_d, state_ssm, w1)
    return (yp[None], ys, p_conv_a[None], s_conv_a, p_win_k[None], p_win_v[None], s_win_k, s_win_v, s_chunk_v,
            p_conv_d[None], s_conv_d, p_ssm[None], s_ssm)
```

```python
import functools
import math

import jax
import jax.numpy as jnp
import numpy as np
from jax import lax
from jax.experimental import pallas as pl
from jax.experimental.pallas import tpu as pltpu

F32 = jnp.float32
BF16 = jnp.bfloat16

D_MODEL = 2048
HALF = 1024
HEAD_DIM = 64
N_HEADS = 16
N_KV = 4
GROUP = 4
WINDOW = 128
NUM_BUCKETS = 32
MAX_DISTANCE = 128
CMLP_GROUPS = 8
CHUNK = 128
SSD_HEADS = 16
SSD_STATE = 128
SSD_GROUPS = 2
SSD_CONV_DIM = HALF + 2 * SSD_GROUPS * SSD_STATE
DEC_SEQ = 4
NORM_EPS = 1e-6
MASK_VALUE = -1e30

V7X_VMEM_BYTES = 64 * 1024 * 1024
VMEM_LIMIT = V7X_VMEM_BYTES - 8 * 1024 * 1024
CONV_A_VMEM_LIMIT = V7X_VMEM_BYTES - 4 * 1024 * 1024


def _params(n_axes=1):
    return pltpu.CompilerParams(dimension_semantics=("arbitrary",) * n_axes,
                                vmem_limit_bytes=VMEM_LIMIT)


def _resident(shape):
    nd = len(shape)
    return pl.BlockSpec(shape, lambda *_: (0,) * nd, pipeline_mode=pl.Buffered(1))


def _rows(tile, width):
    return pl.BlockSpec((tile, width), lambda i: (i, 0))


def _cols(rows, width, block):
    return pl.BlockSpec((rows, width), lambda *_: (0, block), pipeline_mode=pl.Buffered(1))


def _rowwin(height, cols, block):
    return pl.BlockSpec((height, cols), lambda *_: (block, 0), pipeline_mode=pl.Buffered(1))


def _rms_bf16(x, g):
    ms = jnp.mean(x * x, axis=-1, keepdims=True)
    return (x * lax.rsqrt(ms + NORM_EPS) * g).astype(BF16)


def _silu(x):
    return x * jax.nn.sigmoid(x)


def _dot(a, b):
    return jnp.dot(a, b, preferred_element_type=F32)


def _dot_nt(a, b):
    return lax.dot_general(a, b, (((1,), (1,)), ((), ())), preferred_element_type=F32)


def _dot_tn(a, b):
    return lax.dot_general(a, b, (((0,), (0,)), ((), ())), preferred_element_type=F32)


def _dot_w(a, w):
    return _dot(a, w.astype(BF16))


def _dot_wt(a, wt):
    return _dot_nt(a, wt.astype(BF16))


def _split3(x):
    hi = x.astype(BF16)
    r1 = x - hi.astype(F32)
    mid = r1.astype(BF16)
    lo = (r1 - mid.astype(F32)).astype(BF16)
    return hi, mid, lo


def _place_steps(t, placements):
    n_seq = placements[0][1].shape[0]
    row = lax.broadcasted_iota(jnp.int32, (t, n_seq), 0)
    seq = lax.broadcasted_iota(jnp.int32, (t, n_seq), 1)
    lhs, rhs = [], []
    for step, state in placements:
        sel = jnp.where(row == DEC_SEQ * seq + step, 1.0, 0.0).astype(BF16)
        lhs += [sel] * 3
        rhs += list(_split3(state))
    return _dot(jnp.concatenate(lhs, axis=1), jnp.concatenate(rhs, axis=0))


def _prompt_rows(tile, width, n_p):
    return pl.BlockSpec((tile, width), lambda i: (jnp.minimum(i, n_p - 1), 0))


def _group_specs(arg, tile, width, n_p):
    if isinstance(arg, tuple):
        return [_prompt_rows(tile, width, n_p), _resident((tile, width))]
    return [_rows(tile, width)]


def _out_proj_tile(ya_ref, yb_ref, x_ref, w_ref, g_ref, o_ref):
    y = _dot_w(ya_ref[...], w_ref[0:HALF, :]) + _dot_w(yb_ref[...], w_ref[HALF:2 * HALF, :])
    ms = jnp.mean(y * y, axis=-1, keepdims=True)
    o_ref[...] = x_ref[...] + y * lax.rsqrt(ms + NORM_EPS) * g_ref[...]


def _out_proj_kernel(*refs, n_p, n_ya, n_yb, n_x):
    refs = list(refs)
    ya, yb, x = refs[:n_ya], refs[n_ya:n_ya + n_yb], refs[n_ya + n_yb:n_ya + n_yb + n_x]
    w_ref, g_ref = refs[n_ya + n_yb + n_x:n_ya + n_yb + n_x + 2]
    outs = refs[n_ya + n_yb + n_x + 2:]
    is_sample = pl.program_id(0) == n_p

    @pl.when(jnp.logical_not(is_sample))
    def _():
        _out_proj_tile(ya[0], yb[0], x[0], w_ref, g_ref, outs[0])

    @pl.when(is_sample)
    def _():
        _out_proj_tile(ya[-1], yb[-1], x[-1], w_ref, g_ref, outs[-1])


def _out_proj(ya, yb, x, w, g, n_p, tile, out_pair):
    groups = [(ya, HALF), (yb, HALF), (x, D_MODEL)]
    in_specs, args = [], []
    for arg, width in groups:
        in_specs += _group_specs(arg, tile, width, n_p)
        args += list(arg) if isinstance(arg, tuple) else [arg]
    if out_pair:
        out_specs = [_prompt_rows(tile, D_MODEL, n_p), _resident((tile, D_MODEL))]
        out_shape = [jax.ShapeDtypeStruct((n_p * tile, D_MODEL), F32), jax.ShapeDtypeStruct((tile, D_MODEL), F32)]
    else:
        out_specs = [_rows(tile, D_MODEL)]
        out_shape = [jax.ShapeDtypeStruct(((n_p + 1) * tile, D_MODEL), F32)]
    n_of = [2 if isinstance(arg, tuple) else 1 for arg, _ in groups]
    return pl.pallas_call(
        functools.partial(_out_proj_kernel, n_p=n_p, n_ya=n_of[0], n_yb=n_of[1], n_x=n_of[2]),
        grid=(n_p + 1,),
        in_specs=in_specs + [_resident((2 * HALF, D_MODEL)), _resident((1, D_MODEL))],
        out_specs=out_specs,
        out_shape=out_shape,
        compiler_params=_params(),
        name="out_proj",
    )(*args, w, g)


CONV_A_CHUNK = 256


def _conv_a_kernel(*refs, sample):
    if sample:
        x_ref, g_ref, w_ref, cw_ref, st_ref, ya_ref, s_ref, h_ref = refs
    else:
        x_ref, g_ref, w_ref, cw_ref, ya_ref, s_ref, h_ref, shift_scr = refs
    tile = x_ref.shape[0]
    cc = CONV_A_CHUNK
    h = _rms_bf16(x_ref[...], g_ref[...])
    h_ref[...] = h
    if not sample:
        @pl.when(pl.program_id(0) == 0)
        def _():
            s_ref[...] = jnp.zeros_like(s_ref)
    for c in range(HALF // cc):
        lanes = slice(c * cc, (c + 1) * cc)
        a_b, a_c, a_h, a_g = (_dot_w(h, w_ref[:, j * HALF + c * cc:j * HALF + (c + 1) * cc]) for j in range(4))
        s = a_c * a_h
        if sample:
            t_in = lax.broadcasted_iota(jnp.int32, s.shape, 0) % DEC_SEQ
            old0 = st_ref[:, c * cc:(c + 1) * cc]
            old1 = st_ref[:, HALF + c * cc:HALF + (c + 1) * cc]
            p1 = jnp.where(t_in >= 1, pltpu.roll(s, 1, 0), 0.0) + _place_steps(tile, [(0, old1)])
            p2 = jnp.where(t_in >= 2, pltpu.roll(s, 2, 0), 0.0) + _place_steps(tile, [(0, old0), (1, old1)])
            s_ref[:, lanes] = s
        else:
            shift_scr[0:8, :] = s_ref[:, lanes]
            shift_scr[8:8 + tile, :] = s
            p1 = shift_scr[7:7 + tile, :]
            p2 = shift_scr[6:6 + tile, :]
            s_ref[:, lanes] = s[tile - 8:tile, :]
        conv = p2 * cw_ref[0:1, lanes] + p1 * cw_ref[1:2, lanes] + s * cw_ref[2:3, lanes]
        ya_ref[:, lanes] = (a_b * conv * _silu(a_g)).astype(BF16)


def _conv_a_rowspace_kernel(xp_ref, xs_ref, g_ref, w_ref, cw_ref, st_ref, ya_ref, tail_ref, ss_ref, h_ref, shift_scr,
                            *, n_p):
    is_sample = pl.program_id(0) == n_p

    @pl.when(jnp.logical_not(is_sample))
    def _():
        _conv_a_kernel(xp_ref, g_ref, w_ref, cw_ref, ya_ref, tail_ref, h_ref, shift_scr, sample=False)

    @pl.when(is_sample)
    def _():
        _conv_a_kernel(xs_ref, g_ref, w_ref, cw_ref, st_ref, ya_ref, ss_ref, h_ref, sample=True)


def _conv_a(x_p, x_s, g_pre, w0, conv_w, state):
    tile = x_s.shape[0]
    n_p = x_p.shape[0] // tile
    t = (n_p + 1) * tile
    return pl.pallas_call(
        functools.partial(_conv_a_rowspace_kernel, n_p=n_p),
        grid=(n_p + 1,),
        in_specs=[_prompt_rows(tile, D_MODEL, n_p), _resident((tile, D_MODEL)), _resident((1, D_MODEL)),
                  _cols(D_MODEL, 4 * HALF, 0), _resident((3, HALF)), _resident(state.shape)],
        out_specs=[_rows(tile, HALF), _resident((8, HALF)), _resident((tile, HALF)), _rows(tile, D_MODEL)],
        out_shape=[jax.ShapeDtypeStruct((t, HALF), BF16), jax.ShapeDtypeStruct((8, HALF), F32),
                   jax.ShapeDtypeStruct((tile, HALF), F32), jax.ShapeDtypeStruct((t, D_MODEL), BF16)],
        scratch_shapes=[pltpu.VMEM((8 + tile, CONV_A_CHUNK), F32)],
        compiler_params=pltpu.CompilerParams(dimension_semantics=("arbitrary",), vmem_limit_bytes=CONV_A_VMEM_LIMIT),
        name="conv_a",
    )(x_p, x_s, g_pre, w0, conv_w, state)


def _rel_bucket(dist):
    max_exact = NUM_BUCKETS // 2
    d = jnp.maximum(dist, 0)
    ratio = jnp.maximum(d, max_exact).astype(F32) / max_exact
    large = max_exact + (jnp.log(ratio) / math.log(MAX_DISTANCE / max_exact)
                         * (NUM_BUCKETS - max_exact)).astype(jnp.int32)
    return jnp.where(d < max_exact, d, jnp.minimum(large, NUM_BUCKETS - 1))


def _attn_softmax_pv(s, sink, v_bf, v_transposed=False):
    m = jnp.maximum(jnp.max(s, axis=-1, keepdims=True), sink)
    p = jnp.exp(s - m)
    den = jnp.sum(p, axis=-1, keepdims=True) + jnp.exp(sink - m)
    pv = _dot_nt(p.astype(BF16), v_bf) if v_transposed else _dot(p.astype(BF16), v_bf)
    return pv / den


def _attn_prompt_kernel(h_ref, wq_ref, wkv_ref, wg0_ref, wg1_ref, tab_ref, sink_ref, yb_ref, kwin_ref,
                        vwin_ref, q_scr, gate_scr, k_scr, v_scr, bias_scr):
    tile = h_ref.shape[0]
    i = pl.program_id(0)
    kv_w = N_KV * HEAD_DIM
    h = h_ref[...]

    kw, vw = 2 * HEAD_DIM, 4 * HEAD_DIM

    @pl.when(i == 0)
    def _():
        k_scr[0:WINDOW, :] = jnp.zeros((WINDOW, N_KV * kw), BF16)
        v_scr[0:WINDOW, :] = jnp.zeros((WINDOW, N_KV * vw), BF16)
        for hk in range(N_KV):
            v_scr[:, hk * vw + kw:(hk + 1) * vw] = jnp.ones((tile + WINDOW, kw), BF16)
        in_own = lax.broadcasted_iota(jnp.int32, (WINDOW, 2 * WINDOW), 1) >= WINDOW
        for head in range(N_HEADS):
            row = jnp.broadcast_to(tab_ref[head:head + 1, :], (WINDOW, BIAS_SPAN))
            band = pltpu.roll(row, 0, 1, stride=1, stride_axis=0)[:, 0:2 * WINDOW]
            rows = slice((head % 2) * WINDOW, (head % 2 + 1) * WINDOW)
            bias_scr[1, head // 2, rows, :] = band
            bias_scr[0, head // 2, rows, :] = jnp.where(in_own, band, MASK_VALUE)

    q_scr[...] = (_dot_w(h, wq_ref[...]) * (HEAD_DIM ** -0.5)).astype(BF16)
    k = _dot_w(h, wkv_ref[:, 0:kv_w])
    v = _dot_w(h, wkv_ref[:, kv_w:2 * kv_w])
    gate_scr[:, 0:HALF // 2] = _silu(_dot_w(h, wg0_ref[...]))
    gate_scr[:, HALF // 2:HALF] = _silu(_dot_w(h, wg1_ref[...]))
    for hk in range(N_KV):
        k_h = k[:, hk * HEAD_DIM:(hk + 1) * HEAD_DIM].astype(BF16)
        v_h = v[:, hk * HEAD_DIM:(hk + 1) * HEAD_DIM].astype(BF16)
        k_scr[WINDOW:WINDOW + tile, hk * kw:(hk + 1) * kw] = jnp.concatenate([k_h, k_h], axis=1)
        v_scr[WINDOW:WINDOW + tile, hk * vw:hk * vw + kw] = jnp.concatenate([v_h, v_h], axis=1)
    kwin_ref[...] = k[tile - WINDOW:tile, :]
    vwin_ref[...] = v[tile - WINDOW:tile, :]

    lane = lax.broadcasted_iota(jnp.int32, (WINDOW, kw), 1)
    lo = lane < HEAD_DIM
    keep_a = jnp.where(lo, 1.0, 0.0).astype(BF16)
    keep_b = jnp.where(lo, 0.0, 1.0).astype(BF16)
    is_a = lax.broadcasted_iota(jnp.int32, (2 * WINDOW, 1), 0) < WINDOW

    def block(n, carry):
        r0 = pl.multiple_of(n * WINDOW, WINDOW)
        rows = pl.ds(r0, WINDOW)
        keys = pl.ds(r0, 2 * WINDOW)
        first = jnp.where(jnp.logical_and(i == 0, n == 0), 0, 1)
        for hk in range(N_KV):
            for gp in range(GROUP // 2):
                a = hk * GROUP + 2 * gp
                slab = slice(a * HEAD_DIM, (a + 2) * HEAD_DIM)
                q2 = q_scr[rows, slab]
                lhs = jnp.concatenate([q2 * keep_a, q2 * keep_b], axis=0)
                s = _dot_nt(lhs, k_scr[keys, hk * kw:(hk + 1) * kw]) + bias_scr[first, a // 2]
                sink = jnp.where(is_a, sink_ref[a], sink_ref[a + 1])
                m = jnp.maximum(jnp.max(s, axis=-1, keepdims=True), sink)
                p = jnp.exp(s - m).astype(BF16)
                pv = _dot(p, v_scr[keys, hk * vw:(hk + 1) * vw])
                num = jnp.where(lo, pv[0:WINDOW, 0:kw], pv[WINDOW:2 * WINDOW, 0:kw])
                den = jnp.where(lo, pv[0:WINDOW, kw:2 * kw], pv[WINDOW:2 * WINDOW, kw:2 * kw])
                m_slab = jnp.where(lo, m[0:WINDOW], m[WINDOW:2 * WINDOW])
                den = den + jnp.exp(jnp.where(lo, sink_ref[a], sink_ref[a + 1]) - m_slab)
                yb_ref[rows, slab] = (num / den * gate_scr[rows, slab]).astype(BF16)
        return carry

    lax.fori_loop(0, tile // WINDOW, block, 0, unroll=True)
    k_scr[0:WINDOW, :] = k_scr[tile:tile + WINDOW, :]
    v_scr[0:WINDOW, :] = v_scr[tile:tile + WINDOW, :]


BIAS_SPAN = 3 * WINDOW


def _prompt_bias_table(rel_bias):
    dist = WINDOW - jnp.arange(BIAS_SPAN)
    table = jnp.where(((dist >= 0) & (dist < WINDOW))[:, None], rel_bias.astype(F32)[_rel_bucket(dist)], MASK_VALUE)
    return table.T


def _attn_proj_kernel(h_ref, wq_ref, wkv_ref, wg0_ref, wg1_ref, qg_ref, kt_ref, vt_ref, kv_scr):
    kv_w = N_KV * HEAD_DIM
    h = h_ref[...]
    q = _dot_w(h, wq_ref[...]) * (HEAD_DIM ** -0.5)
    for hk in range(N_KV):
        for g in range(GROUP):
            src = (hk * GROUP + g) * HEAD_DIM
            dst = (g * N_KV + hk) * HEAD_DIM
            qg_ref[:, dst:dst + HEAD_DIM] = q[:, src:src + HEAD_DIM]
    qg_ref[:, HALF:HALF + HALF // 2] = _dot_w(h, wg0_ref[...])
    qg_ref[:, HALF + HALF // 2:2 * HALF] = _dot_w(h, wg1_ref[...])
    kv_scr[...] = _dot_w(h, wkv_ref[...])
    for j in range(kt_ref.shape[0]):
        kt_ref[j] = kv_scr[j * WINDOW:(j + 1) * WINDOW, 0:kv_w].T
        vt_ref[j] = kv_scr[j * WINDOW:(j + 1) * WINDOW, kv_w:2 * kv_w].T


def _attn_rowspace_kernel(h_ref, wq_ref, wkv_ref, wg0_ref, wg1_ref, tab_ref, sink_ref,
                          yb_ref, kwin_ref, vwin_ref, qg_ref, kt_ref, vt_ref,
                          q_scr, gate_scr, k_scr, v_scr, bias_scr, kv_scr, *, n_p):
    is_sample = pl.program_id(0) == n_p

    @pl.when(jnp.logical_not(is_sample))
    def _():
        _attn_prompt_kernel(h_ref, wq_ref, wkv_ref, wg0_ref, wg1_ref, tab_ref, sink_ref, yb_ref, kwin_ref, vwin_ref,
                            q_scr, gate_scr, k_scr, v_scr, bias_scr)

    @pl.when(is_sample)
    def _():
        _attn_proj_kernel(h_ref, wq_ref, wkv_ref, wg0_ref, wg1_ref, qg_ref, kt_ref, vt_ref, kv_scr)


def _attn(h_all, n_p, tile, w0, table, sinks):
    kv_w = N_KV * HEAD_DIM
    t = n_p * tile
    win_spec = pl.BlockSpec((WINDOW, kv_w), lambda i: (0, 0))
    out_shape = [jax.ShapeDtypeStruct((t, HALF), BF16),
                 jax.ShapeDtypeStruct((WINDOW, kv_w), F32), jax.ShapeDtypeStruct((WINDOW, kv_w), F32),
                 jax.ShapeDtypeStruct((tile, 2 * HALF), F32), jax.ShapeDtypeStruct((tile // WINDOW, kv_w, WINDOW), F32),
                 jax.ShapeDtypeStruct((tile // WINDOW, kv_w, WINDOW), F32)]
    return pl.pallas_call(
        functools.partial(_attn_rowspace_kernel, n_p=n_p),
        grid=(n_p + 1,),
        in_specs=[_rows(tile, D_MODEL),
                  _cols(D_MODEL, HALF, 4), _cols(D_MODEL, 2 * kv_w, 10),
                  _cols(D_MODEL, HALF // 2, 11), _cols(D_MODEL, HALF // 2, 12),
                  _resident(table.shape), pl.BlockSpec(memory_space=pltpu.SMEM)],
        out_specs=[_prompt_rows(tile, HALF, n_p), win_spec, win_spec] + [_resident(s.shape) for s in out_shape[3:]],
        out_shape=out_shape,
        scratch_shapes=[pltpu.VMEM((tile, HALF), BF16), pltpu.VMEM((tile, HALF), F32),
                        pltpu.VMEM((tile + WINDOW, 2 * kv_w), BF16), pltpu.VMEM((tile + WINDOW, 4 * kv_w), BF16),
                        pltpu.VMEM((2, N_HEADS // 2, 2 * WINDOW, 2 * WINDOW), F32),
                        pltpu.VMEM((tile, 2 * kv_w), F32)],
        compiler_params=_params(),
        name="attn",
    )(h_all, w0, w0, w0, w0, table, sinks)


ATTN_S_BATCH = 16
KEYS_PAD = 2 * WINDOW


def _attn_sample_kernel(qg_ref, ktn_ref, vtn_ref, ck_ref, cv_ref, bias_ref, sink_ref, yb_ref, nk_ref, nv_ref,
                        kall_scr, vall_scr):
    kv_w = N_KV * HEAD_DIM
    row8 = lax.broadcasted_iota(jnp.int32, (8, kv_w), 0)
    lane_head = lax.broadcasted_iota(jnp.int32, (8, kv_w), 1) // HEAD_DIM
    lower = row8 < DEC_SEQ
    pick = [jnp.where(lane_head == 2 * hp + jnp.where(lower, 0, 1), 1.0, 0.0).astype(F32) for hp in range(2)]
    lower_w = lax.broadcasted_iota(jnp.int32, (8, HALF), 0) < DEC_SEQ
    kept = lax.broadcasted_iota(jnp.int32, (kv_w, WINDOW), 1) < WINDOW - DEC_SEQ
    seq0 = pl.program_id(0) * ATTN_S_BATCH
    per_tile = WINDOW // DEC_SEQ

    def slide(old, new_tile, shift):
        return jnp.where(kept, pltpu.roll(old, WINDOW - DEC_SEQ, 1), pltpu.roll(new_tile, shift, 1))

    def update(b, carry):
        tile = (seq0 + b) // per_tile
        shift = (2 * WINDOW - DEC_SEQ - DEC_SEQ * ((seq0 + b) % per_tile)) % WINDOW
        k_old = ck_ref[b].reshape(kv_w, WINDOW)
        v_old = cv_ref[b].reshape(kv_w, WINDOW)
        k_win = slide(k_old, ktn_ref[tile], shift)
        v_win = slide(v_old, vtn_ref[tile], shift)
        nk_ref[b] = k_win.reshape(N_KV, HEAD_DIM, WINDOW)
        nv_ref[b] = v_win.reshape(N_KV, HEAD_DIM, WINDOW)
        kall_scr[b] = jnp.concatenate([k_old, k_win], axis=1).astype(BF16)
        vall_scr[b] = jnp.concatenate([v_old, v_win], axis=1).astype(BF16)
        return carry

    lax.fori_loop(0, ATTN_S_BATCH, update, 0, unroll=4)

    def pair(p, carry):
        r0 = pl.multiple_of(p * 8, 8)
        rows = qg_ref[pl.ds(r0, 8), :]
        q8 = rows[:, 0:HALF]
        gate8 = rows[:, HALF:2 * HALF]
        q_swap = pltpu.roll(q8, 4, 0)
        out8 = []
        for sub in range(2):
            b = 2 * p + sub
            q_dup = jnp.where(lower_w, q8, q_swap) if sub == 0 else jnp.where(lower_w, q_swap, q8)
            q_bd = jnp.concatenate(
                [q_dup[:, g * kv_w:(g + 1) * kv_w] * pick[hp] for g in range(GROUP) for hp in range(2)], axis=0)
            s = _dot(q_bd.astype(BF16), kall_scr[b]) + bias_ref[...]
            o = _attn_softmax_pv(s, sink_ref[:, 0:1], vall_scr[b], v_transposed=True)
            out_g = []
            for g in range(GROUP):
                acc = None
                for hp in range(2):
                    piece = o[(2 * g + hp) * 8:(2 * g + hp + 1) * 8, :] * pick[hp]
                    piece = piece + pltpu.roll(piece, 4, 0)
                    acc = piece if acc is None else acc + piece
                out_g.append(acc)
            out8.append(jnp.concatenate(
                [out_g[g][:, hk * HEAD_DIM:(hk + 1) * HEAD_DIM] for hk in range(N_KV) for g in range(GROUP)], axis=1))
        o8 = jnp.where(lower_w, out8[0], out8[1])
        yb_ref[pl.ds(r0, 8), :] = (o8 * _silu(gate8)).astype(BF16)
        return carry

    lax.fori_loop(0, ATTN_S_BATCH // 2, pair, 0, unroll=2)


def _sample_bias(rel_bias, sinks):
    t = jnp.arange(DEC_SEQ)[:, None]
    j = jnp.arange(KEYS_PAD)[None, :]
    pos = jnp.where(j < WINDOW, j, j - (KEYS_PAD - DEC_SEQ) + WINDOW)
    dist = t + WINDOW - pos
    valid = (dist >= 0) & (dist < WINDOW) & ((j < WINDOW) | (j >= KEYS_PAD - DEC_SEQ))
    bias = jnp.where(valid[:, :, None], rel_bias.astype(F32)[_rel_bucket(dist)], MASK_VALUE)
    bias = bias.reshape(DEC_SEQ, KEYS_PAD, N_KV, GROUP).transpose(3, 2, 0, 1).reshape(N_HEADS * DEC_SEQ, KEYS_PAD)
    sink = jnp.broadcast_to(sinks.astype(F32).reshape(N_KV, GROUP).T[:, :, None], (GROUP, N_KV, DEC_SEQ))
    return bias, jnp.broadcast_to(sink.reshape(N_HEADS * DEC_SEQ, 1), (N_HEADS * DEC_SEQ, 128))


def _attn_sample(qg, kt_new, vt_new, cache_kt, cache_vt, bias, sink):
    n_seq = cache_kt.shape[0]
    bb = ATTN_S_BATCH
    cache_spec = pl.BlockSpec((bb, N_KV, HEAD_DIM, WINDOW), lambda i: (i, 0, 0, 0))
    return pl.pallas_call(
        _attn_sample_kernel,
        grid=(n_seq // bb,),
        in_specs=[_rows(bb * DEC_SEQ, 2 * HALF), _resident(kt_new.shape), _resident(vt_new.shape),
                  cache_spec, cache_spec, _resident(bias.shape), _resident(sink.shape)],
        out_specs=[_rows(bb * DEC_SEQ, HALF), cache_spec, cache_spec],
        out_shape=[jax.ShapeDtypeStruct((n_seq * DEC_SEQ, HALF), BF16),
                   jax.ShapeDtypeStruct(cache_kt.shape, F32), jax.ShapeDtypeStruct(cache_vt.shape, F32)],
        scratch_shapes=[pltpu.VMEM((bb, N_KV * HEAD_DIM, KEYS_PAD), BF16),
                        pltpu.VMEM((bb, N_KV * HEAD_DIM, KEYS_PAD), BF16)],
        compiler_params=_params(),
        name="attn_sample",
    )(qg, kt_new, vt_new, cache_kt, cache_vt, bias, sink)


def _prep_layer0(g_pre, w_in, conv_w, rel_bias, sinks, w_out, g_post):
    return dict(
        g_pre=g_pre.reshape(1, D_MODEL), w0=w_in, conv_w=conv_w, rel_bias=rel_bias, sinks=sinks,
        w_out=w_out, g_post=g_post.reshape(1, D_MODEL))


def _layer0(x_p, x_s, conv_state, cache_k, cache_v, w):
    n_seq = x_s.shape[0]
    rows = x_s.reshape(n_seq * DEC_SEQ, D_MODEL)
    tile = rows.shape[0]
    n_p = x_p.shape[0] // tile
    ya_all, s_tail, s_s, h_all = _conv_a(x_p, rows, w['g_pre'], w['w0'], w['conv_w'],
                                         conv_state.reshape(n_seq, 2 * HALF))
    yb_p, kwin, vwin, qg, kt_new, vt_new = _attn(h_all, n_p, tile, w['w0'], _prompt_bias_table(w['rel_bias']),
                                                 w['sinks'])
    bias, sink = _sample_bias(w['rel_bias'], w['sinks'])
    yb_s, new_kt, new_vt = _attn_sample(qg, kt_new, vt_new, cache_k.transpose(0, 2, 3, 1),
                                        cache_v.transpose(0, 2, 3, 1), bias, sink)
    (y_all,) = _out_proj(ya_all, (yb_p, yb_s), (x_p, rows), w['w_out'], w['g_post'], n_p, tile, out_pair=False)
    prompt_state = (s_tail[6:8], kwin.reshape(WINDOW, N_KV, HEAD_DIM), vwin.reshape(WINDOW, N_KV, HEAD_DIM))
    sample_state = (s_s.reshape(n_seq, DEC_SEQ, HALF)[:, DEC_SEQ - 2:], new_kt.transpose(0, 3, 1, 2),
                    new_vt.transpose(0, 3, 1, 2))
    return y_all, prompt_state, sample_state


def _layer_norm(v, g, b):
    xc = v - jnp.mean(v, axis=-1, keepdims=True)
    return xc * lax.rsqrt(jnp.mean(xc * xc, axis=-1, keepdims=True) + NORM_EPS) * g + b


def _cmlp_prompt_kernel(x_ref, g_ref, w_ref, lng_ref, lnb_ref, ws_ref, bs_ref, yc_ref, h_ref, vn_scr):
    tile = x_ref.shape[0]
    h = _rms_bf16(x_ref[...], g_ref[...])
    h_ref[...] = h
    v = _dot_wt(h, w_ref[HALF:2 * HALF, :])
    vn_scr[...] = _layer_norm(v, lng_ref[...], lnb_ref[...]).astype(BF16)
    gw = HALF // CMLP_GROUPS
    cols = 2 * gw
    for cb in range(HALF // cols):
        u = _dot_wt(h, w_ref[cb * cols:(cb + 1) * cols, :])
        gate = _silu(_dot_wt(h, w_ref[2 * HALF + cb * cols:2 * HALF + (cb + 1) * cols, :]))
        for gi in range(2):
            grp = 2 * cb + gi
            lanes = slice(grp * gw, (grp + 1) * gw)
            for n in range(tile // CHUNK):
                rows = slice(n * CHUNK, (n + 1) * CHUNK)
                mixed = _dot(ws_ref[grp], vn_scr[rows, lanes]) + bs_ref[grp]
                yc_ref[rows, lanes] = (u[rows, gi * gw:(gi + 1) * gw] * mixed
                                       * gate[rows, gi * gw:(gi + 1) * gw]).astype(BF16)


def _cmlp_kernel(x_ref, g_ref, w_ref, lng_ref, lnb_ref, ws_ref, bs_ref, coef_ref, bias_ref, yc_ref, h_ref, vns_ref,
                 vn_scr, *, n_p):
    is_sample = pl.program_id(0) == n_p

    @pl.when(jnp.logical_not(is_sample))
    def _():
        _cmlp_prompt_kernel(x_ref, g_ref, w_ref, lng_ref, lnb_ref, ws_ref, bs_ref, yc_ref, h_ref, vn_scr)

    @pl.when(is_sample)
    def _():
        _cmlp_sample_kernel(x_ref, g_ref, w_ref, lng_ref, lnb_ref, coef_ref, bias_ref, yc_ref, vns_ref, h_ref)


def _cmlp(x_all, n_p, tile, w):
    t = x_all.shape[0]
    consts = [w['ln_g'], w['ln_b'], w['ws_tril'], w['bs_rows'], w['coef'], w['bias4']]
    return pl.pallas_call(
        functools.partial(_cmlp_kernel, n_p=n_p),
        grid=(n_p + 1,),
        in_specs=[_rows(tile, D_MODEL), _resident((1, D_MODEL)), _rowwin(3 * HALF, D_MODEL, 0)]
                 + [_resident(c.shape) for c in consts],
        out_specs=[_rows(tile, HALF), _rows(tile, D_MODEL), _resident((tile, HALF))],
        out_shape=[jax.ShapeDtypeStruct((t, HALF), BF16), jax.ShapeDtypeStruct((t, D_MODEL), BF16),
                   jax.ShapeDtypeStruct((tile, HALF), F32)],
        scratch_shapes=[pltpu.VMEM((tile, HALF), BF16)],
        compiler_params=_params(),
        name="cmlp",
    )(x_all, w['g_pre'], w['w1'], *consts)


def _cmlp_sample_kernel(x_ref, g_ref, w_ref, lng_ref, lnb_ref, coef_ref, bias_ref, yc_ref, vn_ref, h_ref):
    t = x_ref.shape[0]
    h = _rms_bf16(x_ref[...], g_ref[...])
    h_ref[...] = h
    u = _dot_wt(h, w_ref[0:HALF, :])
    vn = _layer_norm(_dot_wt(h, w_ref[HALF:2 * HALF, :]), lng_ref[...], lnb_ref[...])
    gate = _silu(_dot_wt(h, w_ref[2 * HALF:3 * HALF, :]))
    vn_ref[...] = vn

    def tiled(a):
        return a.reshape(t // 8, 8, HALF)

    mixed = tiled(vn) * coef_ref[0][None] + bias_ref[...][None]
    for k in range(1, DEC_SEQ):
        mixed = mixed + tiled(pltpu.roll(vn, k, 0)) * coef_ref[k][None]
    yc_ref[...] = (u * mixed.reshape(t, HALF) * gate).astype(BF16)


HEAD_LANES = 128
SSD_GW = HALF // SSD_GROUPS


def _softplus(x):
    return jnp.maximum(x, 0.0) + jnp.log1p(jnp.exp(-jnp.abs(x)))


def _dt_proj(h, wdt_ref):
    pad = jnp.zeros((HEAD_LANES - SSD_HEADS, D_MODEL), F32)
    return _dot_wt(h, jnp.concatenate([wdt_ref[...], pad], axis=0))


def _group_norm_gate(y, z, gn):
    gated = y * _silu(z)
    parts = []
    for g in range(SSD_GROUPS):
        part = gated[:, g * SSD_GW:(g + 1) * SSD_GW]
        parts.append(part * lax.rsqrt(jnp.mean(part * part, axis=-1, keepdims=True) + NORM_EPS))
    return (jnp.concatenate(parts, axis=1) * gn).astype(BF16)


def _ssd_prompt_kernel(h_ref, wz_ref, wx0_ref, wx1_ref, wx2_ref, wdt_ref, cw_ref, cb_ref, dtb_ref, alog_ref,
                       dskip_ref, gn_ref, e3_ref, tril3_ref, yd_ref, tail_ref, ssm_ref,
                       xbc_scr, z_scr, dt_scr, ht_scr, shift_scr):
    tile = h_ref.shape[0]
    i = pl.program_id(0)
    cd = SSD_CONV_DIM
    h = h_ref[...]

    @pl.when(i == 0)
    def _():
        tail_ref[...] = jnp.zeros_like(tail_ref)
        ht_scr[...] = jnp.zeros_like(ht_scr)

    z_scr[...] = _dot_wt(h, wz_ref[...])
    dt_scr[...] = _softplus(_dt_proj(h, wdt_ref) + dtb_ref[...])
    third = cd // 3
    for j, wx_ref in enumerate((wx0_ref, wx1_ref, wx2_ref)):
        cols = slice(j * third, (j + 1) * third)
        raw = _dot_wt(h, wx_ref[...])
        shift_scr[0:8, :] = tail_ref[:, cols]
        shift_scr[8:8 + tile, :] = raw
        conv = raw * cw_ref[3:4, cols] + cb_ref[:, cols]
        for k in range(1, 4):
            conv = conv + shift_scr[8 - k:8 - k + tile, :] * cw_ref[3 - k:4 - k, cols]
        xbc_scr[:, cols] = _silu(conv)
        tail_ref[:, cols] = raw[tile - 8:tile, :]

    a16 = -jnp.exp(alog_ref[...])
    causal =(lax.broadcasted_iota(jnp.int32, (CHUNK, CHUNK), 0)
              >= lax.broadcasted_iota(jnp.int32, (CHUNK, CHUNK), 1))
    first_half = lax.broadcasted_iota(jnp.int32, (CHUNK, 2 * HEAD_DIM), 1) < HEAD_DIM
    keep_a = jnp.where(first_half, 1.0, 0.0).astype(BF16)
    keep_b = jnp.where(first_half, 0.0, 1.0).astype(BF16)

    def chunk(n, carry):
        r0 = pl.multiple_of(n * CHUNK, CHUNK)
        rows = pl.ds(r0, CHUNK)
        xs = xbc_scr[rows, 0:HALF]
        dt16 = dt_scr[rows, :]
        dt_e = _dot(jnp.concatenate(_split3(dt16), axis=1), e3_ref[...])
        acs16 = _dot(tril3_ref[...], jnp.concatenate(_split3(dt16 * a16), axis=0))
        acs_e = _dot(jnp.concatenate(_split3(acs16), axis=1), e3_ref[...])
        acs_t = acs16.T
        last_e = acs_e[CHUNK - 1:CHUNK, :]
        xdt = xs * dt_e
        xdt_bf = xdt.astype(BF16)
        xw = (jnp.exp(last_e - acs_e) * xdt).astype(BF16)
        dec_e = jnp.exp(last_e)
        y_parts = []
        yoff_parts = []
        for g in range(SSD_GROUPS):
            c_g = xbc_scr[rows, HALF + 2 * SSD_STATE + g * SSD_STATE:HALF + 2 * SSD_STATE + (g + 1) * SSD_STATE].astype(BF16)
            b_g = xbc_scr[rows, HALF + g * SSD_STATE:HALF + (g + 1) * SSD_STATE].astype(BF16)
            cb = _dot_nt(c_g, b_g)
            h_prev = ht_scr[g]
            yoff_parts.append(_dot(c_g, h_prev.astype(BF16)))
            for r in range(0, SSD_HEADS // SSD_GROUPS, 2):
                wgt = []
                for hd in (g * (SSD_HEADS // SSD_GROUPS) + r, g * (SSD_HEADS // SSD_GROUPS) + r + 1):
                    seg = acs16[:, hd:hd + 1] - acs_t[hd:hd + 1, :]
                    wgt.append(cb * jnp.exp(jnp.where(causal, seg, -jnp.inf)))
                a = g * (SSD_HEADS // SSD_GROUPS) + r
                slab = xdt_bf[:, a * HEAD_DIM:(a + 2) * HEAD_DIM]
                rhs = jnp.concatenate([slab * keep_a, slab * keep_b], axis=0)
                y_parts.append(_dot(jnp.concatenate(wgt, axis=1).astype(BF16), rhs))
            lanes = slice(g * SSD_GW, (g + 1) * SSD_GW)
            ht_scr[g] = h_prev * dec_e[:, lanes] + _dot_tn(b_g, xw[:, lanes])
        y = (jnp.concatenate(y_parts, axis=1) + jnp.concatenate(yoff_parts, axis=1) * jnp.exp(acs_e)
             + dskip_ref[...] * xs)
        yd_ref[rows, :] = _group_norm_gate(y, z_scr[rows, :], gn_ref[...])
        return carry

    lax.fori_loop(0, tile // CHUNK, chunk, 0, unroll=True)

    @pl.when(i == pl.num_programs(0) - 1)
    def _():
        for g in range(SSD_GROUPS):
            ssm_ref[g * SSD_GW:(g + 1) * SSD_GW, :] = ht_scr[g].T


def _ssd_weight_specs():
    third = SSD_CONV_DIM // 3
    first = 4 * HALF // third
    return ([_rowwin(HALF, D_MODEL, 3)] + [_rowwin(third, D_MODEL, first + j) for j in range(3)]
            + [_rowwin(SSD_HEADS, D_MODEL, (4 * HALF + SSD_CONV_DIM) // SSD_HEADS)])


def _ssd_prompt(h, n_p, tile, w):
    t = n_p * tile
    cd = SSD_CONV_DIM
    consts = [w['conv_w'], w['conv_b'], w['dt_bias16'], w['a_log16'],
              w['d_skip_e'], w['gate_norm_g'], w['expand3'], w['tril3']]
    return pl.pallas_call(
        _ssd_prompt_kernel,
        grid=(t // tile,),
        in_specs=[_rows(tile, D_MODEL)] + _ssd_weight_specs() + [_resident(c.shape) for c in consts],
        out_specs=[_rows(tile, HALF), pl.BlockSpec((8, cd), lambda i: (0, 0)),
                   pl.BlockSpec((HALF, SSD_STATE), lambda i: (0, 0))],
        out_shape=[jax.ShapeDtypeStruct((t, HALF), BF16), jax.ShapeDtypeStruct((8, cd), F32),
                   jax.ShapeDtypeStruct((HALF, SSD_STATE), F32)],
        scratch_shapes=[pltpu.VMEM((tile, cd), F32), pltpu.VMEM((tile, HALF), F32),
                        pltpu.VMEM((tile, HEAD_LANES), F32), pltpu.VMEM((SSD_GROUPS, SSD_STATE, SSD_GW), F32),
                        pltpu.VMEM((8 + tile, cd // 3), F32)],
        compiler_params=_params(),
        name="ssd_prompt",
    )(h, w['w1'], w['w1'], w['w1'], w['w1'], w['w1'], *consts)


def _ssd_sample_pre_kernel(h_ref, wz_ref, wx0_ref, wx1_ref, wx2_ref, wdt_ref, cw_ref, cb_ref, st_ref,
                           dtb_ref, aloge_ref, dskip_ref, e3_ref, seg_ref,
                           nconv_ref, z_ref, ysk_ref, eacs_ref, xw_ref, dec_ref, b_ref, c_ref, raw_scr):
    t = h_ref.shape[0]
    n_seq = t // DEC_SEQ
    h = h_ref[...]
    z_ref[...] = _dot_wt(h, wz_ref[...])
    raw = jnp.concatenate([_dot_wt(h, wx0_ref[...]), _dot_wt(h, wx1_ref[...]), _dot_wt(h, wx2_ref[...])], axis=1)
    for c in range(raw_scr.shape[0]):
        lanes = slice(c * 128, (c + 1) * 128)
        raw_scr[c] = raw[:, lanes]
        for j in range(3):
            nconv_ref[j, :, lanes] = raw_scr[c, pl.ds(j + 1, n_seq, stride=DEC_SEQ), :]
    dt16 = _softplus(_dt_proj(h, wdt_ref) + dtb_ref[...])
    dt = _dot(jnp.concatenate(_split3(dt16), axis=1), e3_ref[...])
    old = [st_ref[j] for j in range(3)]
    p1 = _place_steps(t, [(0, old[2])])
    p2 = _place_steps(t, [(0, old[1]), (1, old[2])])
    p3 = _place_steps(t, [(0, old[0]), (1, old[1]), (2, old[2])])

    def step_of(width):
        return lax.broadcasted_iota(jnp.int32, (t, width), 0) % DEC_SEQ

    def back(a, k):
        return jnp.where(step_of(a.shape[1]) >= k, pltpu.roll(a, k, 0), 0.0)

    def ahead(a, k):
        return jnp.where(step_of(a.shape[1]) + k < DEC_SEQ, pltpu.roll(a, t - k, 0), 0.0)

    conv = (raw * cw_ref[3:4, :] + (back(raw, 1) + p1) * cw_ref[2:3, :]
            + (back(raw, 2) + p2) * cw_ref[1:2, :] + (back(raw, 3) + p3) * cw_ref[0:1, :]
            + cb_ref[...])
    xbc = _silu(conv)
    xs = xbc[:, 0:HALF]
    bm = xbc[:, HALF:HALF + 2 * SSD_STATE]
    cm = xbc[:, HALF + 2 * SSD_STATE:]
    b_ref[...] = bm
    c_ref[...] = cm
    da = dt * (-jnp.exp(aloge_ref[...]))
    acs = da + back(da, 1) + back(da, 2) + back(da, 3)
    suffix = ahead(da, 1) + ahead(da, 2) + ahead(da, 3)
    xdt = xs * dt
    y = _dot((cm * bm).astype(BF16), seg_ref[...]) * xdt
    for k in range(1, DEC_SEQ):
        cbk = _dot((cm * pltpu.roll(bm, k, 0)).astype(BF16), seg_ref[...])
        term = cbk * jnp.exp(acs - pltpu.roll(acs, k, 0)) * pltpu.roll(xdt, k, 0)
        y = y + jnp.where(step_of(HALF) >= k, term, 0.0)
    ysk_ref[...] = y + dskip_ref[...] * xs
    eacs_ref[...] = jnp.exp(acs)
    xw_ref[...] = jnp.exp(suffix) * xdt
    dec_ref[...] = jnp.exp(acs + suffix)


def _ssd_sample_pre(h, first_row, conv_state, w):
    t = conv_state.shape[1] * DEC_SEQ
    cd = SSD_CONV_DIM
    tile = min(SSD_PRE_ROWS, t)
    first = first_row // tile
    seqs = tile // DEC_SEQ
    state_spec = pl.BlockSpec((3, seqs, cd), lambda i: (0, i, 0))
    head = [w['conv_w'], w['conv_b']]
    tail = [w['dt_bias16'], w['a_log_e'], w['d_skip_e'], w['expand3'], w['seg_expand']]
    args = [h, w['w1'], w['w1'], w['w1'], w['w1'], w['w1']] + head + [conv_state] + tail
    wide = jax.ShapeDtypeStruct((t, HALF), F32)
    narrow = jax.ShapeDtypeStruct((t, 2 * SSD_STATE), F32)
    out_shape = [jax.ShapeDtypeStruct(conv_state.shape, F32), wide, wide, wide, wide, wide, narrow, narrow]
    return pl.pallas_call(
        _ssd_sample_pre_kernel,
        grid=(t // tile,),
        in_specs=[pl.BlockSpec((tile, D_MODEL), lambda i: (first + i, 0))] + _ssd_weight_specs()
                 + [_resident(c.shape) for c in head] + [state_spec] + [_resident(c.shape) for c in tail],
        out_specs=[state_spec] + [_rows(tile, HALF)] * 5 + [_rows(tile, 2 * SSD_STATE)] * 2,
        out_shape=out_shape,
        scratch_shapes=[pltpu.VMEM((cd // 128, tile, 128), F32)],
        compiler_params=_params(),
        name="ssd_sample_pre",
    )(*args)


SSD_S_BATCH = 8
SSD_PRE_ROWS = 256


def _ssd_sample_state_kernel(st_ref, c_ref, b_ref, xw_ref, dec_ref, eacs_ref, ysk_ref, z_ref, gn_ref,
                             yd_ref, nst_ref):
    row_n = lax.broadcasted_iota(jnp.int32, (8, SSD_STATE), 0)
    row_w = lax.broadcasted_iota(jnp.int32, (8, SSD_GW), 0)
    row_f = lax.broadcasted_iota(jnp.int32, (8, HALF), 0)
    ones_rows = jnp.where((row_n >= 4) & (row_n < 7), 1.0, 0.0).astype(BF16)
    hpg = SSD_HEADS // SSD_GROUPS

    def pair(p, carry):
        r0 = pl.multiple_of(p * 8, 8)
        rows = pl.ds(r0, 8)
        c8 = c_ref[rows, :].astype(BF16)
        b8 = b_ref[rows, :]
        xw8 = xw_ref[rows, :]
        dec8 = dec_ref[rows, :]
        yoff = []
        for sub in range(2):
            b = 2 * p + sub
            xw_own = xw8 if sub == 0 else pltpu.roll(xw8, 4, 0)
            b_own = b8 if sub == 0 else pltpu.roll(b8, 4, 0)
            hi, mid, lo = (term.astype(F32) for term in _split3(dec8[4 * sub:4 * sub + 1, :]))
            parts = []
            for g in range(SSD_GROUPS):
                lanes = slice(g * SSD_GW, (g + 1) * SSD_GW)
                heads = pl.ds(g * hpg, hpg)
                h0 = st_ref[b, heads].reshape(SSD_GW, SSD_STATE)
                parts.append(_dot_nt(c8[:, g * SSD_STATE:(g + 1) * SSD_STATE], h0.astype(BF16)))
                lhs = jnp.where(row_w < 4, xw_own[:, lanes],
                                jnp.where(row_w == 4, hi[:, lanes],
                                          jnp.where(row_w == 5, mid[:, lanes],
                                                    jnp.where(row_w == 6, lo[:, lanes], 0.0)))).astype(BF16)
                rhs_b = jnp.where(row_n < 4, b_own[:, g * SSD_STATE:(g + 1) * SSD_STATE], 0.0).astype(BF16)
                decay = _dot_tn(lhs, ones_rows)
                nst_ref[b, heads] = (h0 * decay + _dot_tn(lhs, rhs_b)).reshape(hpg, HEAD_DIM, SSD_STATE)
            yoff.append(jnp.concatenate(parts, axis=1))
        yoff8 = jnp.where(row_f < 4, yoff[0], yoff[1])
        y = ysk_ref[rows, :] + yoff8 * eacs_ref[rows, :]
        yd_ref[rows, :] = _group_norm_gate(y, z_ref[rows, :], gn_ref[...])
        return carry

    lax.fori_loop(0, SSD_S_BATCH // 2, pair, 0, unroll=True)


def _ssd_sample_state(state, cm, bm, xw, dec, eacs, ysk, z, gn):
    n_seq = state.shape[0]
    bb = SSD_S_BATCH
    r = bb * DEC_SEQ
    st_spec = pl.BlockSpec((bb, SSD_HEADS, HEAD_DIM, SSD_STATE), lambda i: (i, 0, 0, 0))
    return pl.pallas_call(
        _ssd_sample_state_kernel,
        grid=(n_seq // bb,),
        in_specs=[st_spec, _rows(r, 2 * SSD_STATE), _rows(r, 2 * SSD_STATE)] + [_rows(r, HALF)] * 5
                 + [_resident((1, HALF))],
        out_specs=[_rows(r, HALF), st_spec],
        out_shape=[jax.ShapeDtypeStruct((n_seq * DEC_SEQ, HALF), BF16), jax.ShapeDtypeStruct(state.shape, F32)],
        compiler_params=_params(),
        name="ssd_sample_state",
    )(state, cm, bm, xw, dec, eacs, ysk, z, gn)


def _prep_layer1(g_pre, w_in, ln_g, ln_b, w_s, b_s, conv_w, conv_b, dt_bias, a_log, d_skip, gate_norm_g,
                 w_out, g_post):
    cd = SSD_CONV_DIM
    gw = HALF // CMLP_GROUPS
    w1 = w_in.T

    def lanes16(v):
        return jnp.pad(v.astype(F32), (0, HEAD_LANES - SSD_HEADS)).reshape(1, HEAD_LANES)

    def per_channel(v):
        return jnp.repeat(v.astype(F32), HEAD_DIM).reshape(1, HALF)

    head_of = np.arange(HALF) // HEAD_DIM
    expand = jnp.asarray(np.arange(HEAD_LANES)[:, None] == head_of[None, :], BF16)
    tril = jnp.asarray(np.tril(np.ones((CHUNK, CHUNK))), BF16)
    grp_rows = np.arange(2 * SSD_STATE) // SSD_STATE
    seg_expand = jnp.asarray(grp_rows[:, None] == (head_of // (SSD_HEADS // SSD_GROUPS))[None, :], BF16)

    w4 = jnp.tril(w_s[:, :DEC_SEQ, :DEC_SEQ])
    steps = jnp.arange(DEC_SEQ)
    coef = []
    for k in range(DEC_SEQ):
        src = steps - k
        ck = jnp.where((src >= 0)[None, :], w4[:, steps, jnp.maximum(src, 0)], 0.0)
        ck = jnp.repeat(ck.T, gw, axis=1)
        coef.append(jnp.concatenate([ck, ck], axis=0))
    bias4 = jnp.repeat(b_s[:, :DEC_SEQ].T, gw, axis=1)
    return dict(
        g_pre=g_pre.reshape(1, D_MODEL), w1=w1, ln_g=ln_g.reshape(1, HALF), ln_b=ln_b.reshape(1, HALF),
        ws_tril=jnp.tril(w_s).astype(BF16),
        bs_rows=jnp.broadcast_to(b_s.astype(F32)[:, :, None], (CMLP_GROUPS, CHUNK, gw)),
        coef=jnp.stack(coef).astype(F32), bias4=jnp.concatenate([bias4, bias4], axis=0).astype(F32),
        conv_w=conv_w, conv_b=conv_b.reshape(1, cd), dt_bias16=lanes16(dt_bias), a_log16=lanes16(a_log),
        a_log_e=per_channel(a_log), d_skip_e=per_channel(d_skip), gate_norm_g=gate_norm_g.reshape(1, HALF),
        expand3=jnp.concatenate([expand] * 3, axis=0), tril3=jnp.concatenate([tril] * 3, axis=1),
        seg_expand=seg_expand, w_out=w_out, g_post=g_post.reshape(1, D_MODEL))


def _layer1(x_all, n_p, conv_state, ssm_state, w):
    n_seq = ssm_state.shape[0]
    tile = n_seq * DEC_SEQ
    yc_all, h_all, vn = _cmlp(x_all, n_p, tile, w)
    yd_p, tail, ssm = _ssd_prompt(h_all, n_p, tile, w)
    new_conv, z, ysk, eacs, xw, dec, bm, cm = _ssd_sample_pre(h_all, n_p * tile, conv_state.transpose(1, 0, 2), w)
    yd_s, new_state = _ssd_sample_state(ssm_state, cm, bm, xw, dec, eacs, ysk, z, w['gate_norm_g'])
    y_p, y_s = _out_proj(yc_all, (yd_p, yd_s), x_all, w['w_out'], w['g_post'], n_p, tile, out_pair=True)
    prompt_out = (y_p, tail[5:8], ssm.reshape(SSD_HEADS, HEAD_DIM, SSD_STATE))
    sample_out = (y_s.reshape(n_seq, DEC_SEQ, D_MODEL), vn.reshape(n_seq, DEC_SEQ, HALF),
                  new_conv.transpose(1, 0, 2), new_state)
    return prompt_out, sample_out


def kernel(x_prompt, x_sample, state_conv_a, cache_win_k, cache_win_v, state_conv_d, state_ssm, rel_bias,
           l0_g_pre, l0_w_in, l0_conv_w, l0_sinks, l0_w_out, l0_g_post,
           l1_g_pre, l1_w_in, l1_ln_g, l1_ln_b, l1_w_s, l1_b_s, l1_conv_w, l1_conv_b, l1_dt_bias, l1_a_log,
           l1_d_skip, l1_gate_norm_g, l1_w_out, l1_g_post):
    w0 = _prep_layer0(l0_g_pre, l0_w_in, l0_conv_w, rel_bias, l0_sinks, l0_w_out, l0_g_post)
    w1 = _prep_layer1(l1_g_pre, l1_w_in, l1_ln_g, l1_ln_b, l1_w_s, l1_b_s, l1_conv_w, l1_conv_b, l1_dt_bias,
                      l1_a_log, l1_d_skip, l1_gate_norm_g, l1_w_out, l1_g_post)
    x_p = x_prompt[0]
    n_p = x_p.shape[0] // (x_sample.shape[0] * DEC_SEQ)
    y_all, (p_conv_a, p_win_k, p_win_v), (s_conv_a, s_win_k, s_win_v) = _layer0(
        x_p, x_sample, state_conv_a, cache_win_k, cache_win_v, w0)
    (yp, p_conv_d, p_ssm), (ys, s_chunk_v, s_conv_d, s_ssm) = _layer1(y_all, n_p, state_conv_d, state_ssm, w1)
    return (yp[None], ys, p_conv_a[None], s_conv_a, p_win_k[None], p_win_v[None], s_win_k, s_win_v, s_chunk_v,
            p_conv_d[None], s_conv_d, p_ssm[None], s_ssm)
```

```python
import functools
import math

import jax
import jax.numpy as jnp
import numpy as np
from jax import lax
from jax.experimental import pallas as pl
from jax.experimental.pallas import tpu as pltpu

F32 = jnp.float32
BF16 = jnp.bfloat16

D_MODEL = 2048
HALF = 1024
HEAD_DIM = 64
N_HEADS = 16
N_KV = 4
GROUP = 4
WINDOW = 128
NUM_BUCKETS = 32
MAX_DISTANCE = 128
CMLP_GROUPS = 8
CHUNK = 128
SSD_HEADS = 16
SSD_STATE = 128
SSD_GROUPS = 2
SSD_CONV_DIM = HALF + 2 * SSD_GROUPS * SSD_STATE
DEC_SEQ = 4
NORM_EPS = 1e-6
MASK_VALUE = -1e30

V7X_VMEM_BYTES = 64 * 1024 * 1024
VMEM_LIMIT = V7X_VMEM_BYTES - 8 * 1024 * 1024
CONV_A_VMEM_LIMIT = V7X_VMEM_BYTES - 4 * 1024 * 1024


def _params(n_axes=1):
    return pltpu.CompilerParams(dimension_semantics=("arbitrary",) * n_axes,
                                vmem_limit_bytes=VMEM_LIMIT)


def _resident(shape):
    nd = len(shape)
    return pl.BlockSpec(shape, lambda *_: (0,) * nd, pipeline_mode=pl.Buffered(1))


def _rows(tile, width):
    return pl.BlockSpec((tile, width), lambda i: (i, 0))


def _cols(rows, width, block):
    return pl.BlockSpec((rows, width), lambda *_: (0, block), pipeline_mode=pl.Buffered(1))


def _rowwin(height, cols, block):
    return pl.BlockSpec((height, cols), lambda *_: (block, 0), pipeline_mode=pl.Buffered(1))


def _rms_bf16(x, g):
    ms = jnp.mean(x * x, axis=-1, keepdims=True)
    return (x * lax.rsqrt(ms + NORM_EPS) * g).astype(BF16)


def _silu(x):
    return x * jax.nn.sigmoid(x)


def _dot(a, b):
    return jnp.dot(a, b, preferred_element_type=F32)


def _dot_nt(a, b):
    return lax.dot_general(a, b, (((1,), (1,)), ((), ())), preferred_element_type=F32)


def _dot_tn(a, b):
    return lax.dot_general(a, b, (((0,), (0,)), ((), ())), preferred_element_type=F32)


def _dot_w(a, w):
    return _dot(a, w.astype(BF16))


def _dot_wt(a, wt):
    return _dot_nt(a, wt.astype(BF16))


def _split3(x):
    hi = x.astype(BF16)
    r1 = x - hi.astype(F32)
    mid = r1.astype(BF16)
    lo = (r1 - mid.astype(F32)).astype(BF16)
    return hi, mid, lo


def _place_steps(t, placements):
    n_seq = placements[0][1].shape[0]
    row = lax.broadcasted_iota(jnp.int32, (t, n_seq), 0)
    seq = lax.broadcasted_iota(jnp.int32, (t, n_seq), 1)
    lhs, rhs = [], []
    for step, state in placements:
        sel = jnp.where(row == DEC_SEQ * seq + step, 1.0, 0.0).astype(BF16)
        lhs += [sel] * 3
        rhs += list(_split3(state))
    return _dot(jnp.concatenate(lhs, axis=1), jnp.concatenate(rhs, axis=0))


def _prompt_rows(tile, width, n_p):
    return pl.BlockSpec((tile, width), lambda i: (jnp.minimum(i, n_p - 1), 0))


def _group_specs(arg, tile, width, n_p):
    if isinstance(arg, tuple):
        return [_prompt_rows(tile, width, n_p), _resident(arg[1].shape)]
    return [_rows(tile, width)]


def _out_proj_tile(ya_ref, yb_ref, x_ref, w_ref, g_ref, o_ref):
    y = _dot_w(ya_ref[...], w_ref[0:HALF, :]) + _dot_w(yb_ref[...], w_ref[HALF:2 * HALF, :])
    ms = jnp.mean(y * y, axis=-1, keepdims=True)
    out = x_ref[...].reshape(y.shape) + y * lax.rsqrt(ms + NORM_EPS) * g_ref[...]
    o_ref[...] = out.reshape(o_ref.shape)


def _out_proj_kernel(*refs, n_p, n_ya, n_yb, n_x):
    refs = list(refs)
    ya, yb, x = refs[:n_ya], refs[n_ya:n_ya + n_yb], refs[n_ya + n_yb:n_ya + n_yb + n_x]
    w_ref, g_ref = refs[n_ya + n_yb + n_x:n_ya + n_yb + n_x + 2]
    outs = refs[n_ya + n_yb + n_x + 2:]
    is_sample = pl.program_id(0) == n_p

    @pl.when(jnp.logical_not(is_sample))
    def _():
        _out_proj_tile(ya[0], yb[0], x[0], w_ref, g_ref, outs[0])

    @pl.when(is_sample)
    def _():
        _out_proj_tile(ya[-1], yb[-1], x[-1], w_ref, g_ref, outs[-1])


def _out_proj(ya, yb, x, w, g, n_p, tile, sample_out_shape=None):
    out_pair = sample_out_shape is not None
    groups = [(ya, HALF), (yb, HALF), (x, D_MODEL)]
    in_specs, args = [], []
    for arg, width in groups:
        in_specs += _group_specs(arg, tile, width, n_p)
        args += list(arg) if isinstance(arg, tuple) else [arg]
    if out_pair:
        out_specs = [_prompt_rows(tile, D_MODEL, n_p), _resident(sample_out_shape)]
        out_shape = [jax.ShapeDtypeStruct((n_p * tile, D_MODEL), F32), jax.ShapeDtypeStruct(sample_out_shape, F32)]
    else:
        out_specs = [_rows(tile, D_MODEL)]
        out_shape = [jax.ShapeDtypeStruct(((n_p + 1) * tile, D_MODEL), F32)]
    n_of = [2 if isinstance(arg, tuple) else 1 for arg, _ in groups]
    return pl.pallas_call(
        functools.partial(_out_proj_kernel, n_p=n_p, n_ya=n_of[0], n_yb=n_of[1], n_x=n_of[2]),
        grid=(n_p + 1,),
        in_specs=in_specs + [_resident((2 * HALF, D_MODEL)), _resident((1, D_MODEL))],
        out_specs=out_specs,
        out_shape=out_shape,
        compiler_params=_params(),
        name="out_proj",
    )(*args, w, g)


CONV_A_CHUNK = 256


def _conv_a_kernel(*refs, sample):
    if sample:
        x_ref, g_ref, w_ref, cw_ref, st_ref, ya_ref, s_ref, h_ref = refs
    else:
        x_ref, g_ref, w_ref, cw_ref, ya_ref, s_ref, h_ref, shift_scr = refs
    tile = ya_ref.shape[0]
    cc = CONV_A_CHUNK
    h = _rms_bf16(x_ref[...].reshape(tile, D_MODEL), g_ref[...])
    h_ref[...] = h
    if not sample:
        @pl.when(pl.program_id(0) == 0)
        def _():
            s_ref[...] = jnp.zeros_like(s_ref)
    for c in range(HALF // cc):
        lanes = slice(c * cc, (c + 1) * cc)
        a_b, a_c, a_h, a_g = (_dot_w(h, w_ref[:, j * HALF + c * cc:j * HALF + (c + 1) * cc]) for j in range(4))
        s = a_c * a_h
        if sample:
            t_in = lax.broadcasted_iota(jnp.int32, s.shape, 0) % DEC_SEQ
            old0 = st_ref[:, c * cc:(c + 1) * cc]
            old1 = st_ref[:, HALF + c * cc:HALF + (c + 1) * cc]
            p1 = jnp.where(t_in >= 1, pltpu.roll(s, 1, 0), 0.0) + _place_steps(tile, [(0, old1)])
            p2 = jnp.where(t_in >= 2, pltpu.roll(s, 2, 0), 0.0) + _place_steps(tile, [(0, old0), (1, old1)])
            s_ref[:, :, lanes] = s.reshape(tile // DEC_SEQ, DEC_SEQ, cc)
        else:
            shift_scr[0:8, :] = s_ref[:, lanes]
            shift_scr[8:8 + tile, :] = s
            p1 = shift_scr[7:7 + tile, :]
            p2 = shift_scr[6:6 + tile, :]
            s_ref[:, lanes] = s[tile - 8:tile, :]
        conv = p2 * cw_ref[0:1, lanes] + p1 * cw_ref[1:2, lanes] + s * cw_ref[2:3, lanes]
        ya_ref[:, lanes] = (a_b * conv * _silu(a_g)).astype(BF16)


def _conv_a_rowspace_kernel(xp_ref, xs_ref, g_ref, w_ref, cw_ref, st_ref, ya_ref, tail_ref, ss_ref, h_ref, shift_scr,
                            *, n_p):
    is_sample = pl.program_id(0) == n_p

    @pl.when(jnp.logical_not(is_sample))
    def _():
        _conv_a_kernel(xp_ref, g_ref, w_ref, cw_ref, ya_ref, tail_ref, h_ref, shift_scr, sample=False)

    @pl.when(is_sample)
    def _():
        _conv_a_kernel(xs_ref, g_ref, w_ref, cw_ref, st_ref, ya_ref, ss_ref, h_ref, sample=True)


def _conv_a(x_p, x_s, g_pre, w0, conv_w, state):
    tile = x_s.shape[0] * x_s.shape[1]
    n_p = x_p.shape[0] // tile
    t = (n_p + 1) * tile
    return pl.pallas_call(
        functools.partial(_conv_a_rowspace_kernel, n_p=n_p),
        grid=(n_p + 1,),
        in_specs=[_prompt_rows(tile, D_MODEL, n_p), _resident(x_s.shape), _resident((1, D_MODEL)),
                  _cols(D_MODEL, 4 * HALF, 0), _resident((3, HALF)), _resident(state.shape)],
        out_specs=[_rows(tile, HALF), _resident((8, HALF)), _resident(x_s.shape[:2] + (HALF,)), _rows(tile, D_MODEL)],
        out_shape=[jax.ShapeDtypeStruct((t, HALF), BF16), jax.ShapeDtypeStruct((8, HALF), F32),
                   jax.ShapeDtypeStruct(x_s.shape[:2] + (HALF,), F32), jax.ShapeDtypeStruct((t, D_MODEL), BF16)],
        scratch_shapes=[pltpu.VMEM((8 + tile, CONV_A_CHUNK), F32)],
        compiler_params=pltpu.CompilerParams(dimension_semantics=("arbitrary",), vmem_limit_bytes=CONV_A_VMEM_LIMIT),
        name="conv_a",
    )(x_p, x_s, g_pre, w0, conv_w, state)


def _rel_bucket(dist):
    max_exact = NUM_BUCKETS // 2
    d = jnp.maximum(dist, 0)
    ratio = jnp.maximum(d, max_exact).astype(F32) / max_exact
    large = max_exact + (jnp.log(ratio) / math.log(MAX_DISTANCE / max_exact)
                         * (NUM_BUCKETS - max_exact)).astype(jnp.int32)
    return jnp.where(d < max_exact, d, jnp.minimum(large, NUM_BUCKETS - 1))


def _attn_softmax_pv(s, sink, v_bf, v_transposed=False):
    m = jnp.maximum(jnp.max(s, axis=-1, keepdims=True), sink)
    p = jnp.exp(s - m)
    den = jnp.sum(p, axis=-1, keepdims=True) + jnp.exp(sink - m)
    pv = _dot_nt(p.astype(BF16), v_bf) if v_transposed else _dot(p.astype(BF16), v_bf)
    return pv / den


def _attn_prompt_kernel(h_ref, wq_ref, wkv_ref, wg0_ref, wg1_ref, tab_ref, sink_ref, yb_ref, kwin_ref,
                        vwin_ref, q_scr, gate_scr, k_scr, v_scr, bias_scr):
    tile = h_ref.shape[0]
    i = pl.program_id(0)
    kv_w = N_KV * HEAD_DIM
    h = h_ref[...]

    kw, vw = 2 * HEAD_DIM, 4 * HEAD_DIM

    @pl.when(i == 0)
    def _():
        k_scr[0:WINDOW, :] = jnp.zeros((WINDOW, N_KV * kw), BF16)
        v_scr[0:WINDOW, :] = jnp.zeros((WINDOW, N_KV * vw), BF16)
        for hk in range(N_KV):
            v_scr[:, hk * vw + kw:(hk + 1) * vw] = jnp.ones((tile + WINDOW, kw), BF16)
        in_own = lax.broadcasted_iota(jnp.int32, (WINDOW, 2 * WINDOW), 1) >= WINDOW
        for head in range(N_HEADS):
            row = jnp.broadcast_to(tab_ref[head:head + 1, :], (WINDOW, BIAS_SPAN))
            band = pltpu.roll(row, 0, 1, stride=1, stride_axis=0)[:, 0:2 * WINDOW]
            rows = slice((head % 2) * WINDOW, (head % 2 + 1) * WINDOW)
            bias_scr[1, head // 2, rows, :] = band
            bias_scr[0, head // 2, rows, :] = jnp.where(in_own, band, MASK_VALUE)

    q_scr[...] = (_dot_w(h, wq_ref[...]) * (HEAD_DIM ** -0.5)).astype(BF16)
    k = _dot_w(h, wkv_ref[:, 0:kv_w])
    v = _dot_w(h, wkv_ref[:, kv_w:2 * kv_w])
    gate_scr[:, 0:HALF // 2] = _silu(_dot_w(h, wg0_ref[...]))
    gate_scr[:, HALF // 2:HALF] = _silu(_dot_w(h, wg1_ref[...]))
    for hk in range(N_KV):
        k_h = k[:, hk * HEAD_DIM:(hk + 1) * HEAD_DIM].astype(BF16)
        v_h = v[:, hk * HEAD_DIM:(hk + 1) * HEAD_DIM].astype(BF16)
        k_scr[WINDOW:WINDOW + tile, hk * kw:(hk + 1) * kw] = jnp.concatenate([k_h, k_h], axis=1)
        v_scr[WINDOW:WINDOW + tile, hk * vw:hk * vw + kw] = jnp.concatenate([v_h, v_h], axis=1)
    kwin_ref[...] = k[tile - WINDOW:tile, :]
    vwin_ref[...] = v[tile - WINDOW:tile, :]

    lane = lax.broadcasted_iota(jnp.int32, (WINDOW, kw), 1)
    lo = lane < HEAD_DIM
    keep_a = jnp.where(lo, 1.0, 0.0).astype(BF16)
    keep_b = jnp.where(lo, 0.0, 1.0).astype(BF16)
    is_a = lax.broadcasted_iota(jnp.int32, (2 * WINDOW, 1), 0) < WINDOW

    def block(n, carry):
        r0 = pl.multiple_of(n * WINDOW, WINDOW)
        rows = pl.ds(r0, WINDOW)
        keys = pl.ds(r0, 2 * WINDOW)
        first = jnp.where(jnp.logical_and(i == 0, n == 0), 0, 1)
        for hk in range(N_KV):
            for gp in range(GROUP // 2):
                a = hk * GROUP + 2 * gp
                slab = slice(a * HEAD_DIM, (a + 2) * HEAD_DIM)
                q2 = q_scr[rows, slab]
                lhs = jnp.concatenate([q2 * keep_a, q2 * keep_b], axis=0)
                s = _dot_nt(lhs, k_scr[keys, hk * kw:(hk + 1) * kw]) + bias_scr[first, a // 2]
                sink = jnp.where(is_a, sink_ref[a], sink_ref[a + 1])
                m = jnp.maximum(jnp.max(s, axis=-1, keepdims=True), sink)
                p = jnp.exp(s - m).astype(BF16)
                pv = _dot(p, v_scr[keys, hk * vw:(hk + 1) * vw])
                num = jnp.where(lo, pv[0:WINDOW, 0:kw], pv[WINDOW:2 * WINDOW, 0:kw])
                den = jnp.where(lo, pv[0:WINDOW, kw:2 * kw], pv[WINDOW:2 * WINDOW, kw:2 * kw])
                m_slab = jnp.where(lo, m[0:WINDOW], m[WINDOW:2 * WINDOW])
                den = den + jnp.exp(jnp.where(lo, sink_ref[a], sink_ref[a + 1]) - m_slab)
                yb_ref[rows, slab] = (num / den * gate_scr[rows, slab]).astype(BF16)
        return carry

    lax.fori_loop(0, tile // WINDOW, block, 0, unroll=True)
    k_scr[0:WINDOW, :] = k_scr[tile:tile + WINDOW, :]
    v_scr[0:WINDOW, :] = v_scr[tile:tile + WINDOW, :]


BIAS_SPAN = 3 * WINDOW


def _prompt_bias_table(rel_bias):
    dist = WINDOW - jnp.arange(BIAS_SPAN)
    table = jnp.where(((dist >= 0) & (dist < WINDOW))[:, None], rel_bias.astype(F32)[_rel_bucket(dist)], MASK_VALUE)
    return table.T


def _attn_proj_kernel(h_ref, wq_ref, wkv_ref, wg0_ref, wg1_ref, qg_ref, kt_ref, vt_ref, kv_scr):
    kv_w = N_KV * HEAD_DIM
    h = h_ref[...]
    q = _dot_w(h, wq_ref[...]) * (HEAD_DIM ** -0.5)
    for hk in range(N_KV):
        for g in range(GROUP):
            src = (hk * GROUP + g) * HEAD_DIM
            dst = (g * N_KV + hk) * HEAD_DIM
            qg_ref[:, dst:dst + HEAD_DIM] = q[:, src:src + HEAD_DIM]
    qg_ref[:, HALF:HALF + HALF // 2] = _dot_w(h, wg0_ref[...])
    qg_ref[:, HALF + HALF // 2:2 * HALF] = _dot_w(h, wg1_ref[...])
    kv_scr[...] = _dot_w(h, wkv_ref[...])
    for j in range(kt_ref.shape[0]):
        kt_ref[j] = kv_scr[j * WINDOW:(j + 1) * WINDOW, 0:kv_w].T
        vt_ref[j] = kv_scr[j * WINDOW:(j + 1) * WINDOW, kv_w:2 * kv_w].T


def _attn_rowspace_kernel(h_ref, wq_ref, wkv_ref, wg0_ref, wg1_ref, tab_ref, sink_ref,
                          yb_ref, kwin_ref, vwin_ref, qg_ref, kt_ref, vt_ref,
                          q_scr, gate_scr, k_scr, v_scr, bias_scr, kv_scr, *, n_p):
    is_sample = pl.program_id(0) == n_p

    @pl.when(jnp.logical_not(is_sample))
    def _():
        _attn_prompt_kernel(h_ref, wq_ref, wkv_ref, wg0_ref, wg1_ref, tab_ref, sink_ref, yb_ref, kwin_ref, vwin_ref,
                            q_scr, gate_scr, k_scr, v_scr, bias_scr)

    @pl.when(is_sample)
    def _():
        _attn_proj_kernel(h_ref, wq_ref, wkv_ref, wg0_ref, wg1_ref, qg_ref, kt_ref, vt_ref, kv_scr)


def _attn(h_all, n_p, tile, w0, table, sinks):
    kv_w = N_KV * HEAD_DIM
    t = n_p * tile
    win_spec = pl.BlockSpec((WINDOW, kv_w), lambda i: (0, 0))
    out_shape = [jax.ShapeDtypeStruct((t, HALF), BF16),
                 jax.ShapeDtypeStruct((WINDOW, kv_w), F32), jax.ShapeDtypeStruct((WINDOW, kv_w), F32),
                 jax.ShapeDtypeStruct((tile, 2 * HALF), F32), jax.ShapeDtypeStruct((tile // WINDOW, kv_w, WINDOW), F32),
                 jax.ShapeDtypeStruct((tile // WINDOW, kv_w, WINDOW), F32)]
    return pl.pallas_call(
        functools.partial(_attn_rowspace_kernel, n_p=n_p),
        grid=(n_p + 1,),
        in_specs=[_rows(tile, D_MODEL),
                  _cols(D_MODEL, HALF, 4), _cols(D_MODEL, 2 * kv_w, 10),
                  _cols(D_MODEL, HALF // 2, 11), _cols(D_MODEL, HALF // 2, 12),
                  _resident(table.shape), pl.BlockSpec(memory_space=pltpu.SMEM)],
        out_specs=[_prompt_rows(tile, HALF, n_p), win_spec, win_spec] + [_resident(s.shape) for s in out_shape[3:]],
        out_shape=out_shape,
        scratch_shapes=[pltpu.VMEM((tile, HALF), BF16), pltpu.VMEM((tile, HALF), F32),
                        pltpu.VMEM((tile + WINDOW, 2 * kv_w), BF16), pltpu.VMEM((tile + WINDOW, 4 * kv_w), BF16),
                        pltpu.VMEM((2, N_HEADS // 2, 2 * WINDOW, 2 * WINDOW), F32),
                        pltpu.VMEM((tile, 2 * kv_w), F32)],
        compiler_params=_params(),
        name="attn",
    )(h_all, w0, w0, w0, w0, table, sinks)


ATTN_S_BATCH = 16
KEYS_PAD = 2 * WINDOW


def _attn_sample_kernel(qg_ref, ktn_ref, vtn_ref, ck_ref, cv_ref, bias_ref, sink_ref, yb_ref, nk_ref, nv_ref):
    kv_w = N_KV * HEAD_DIM
    row8 = lax.broadcasted_iota(jnp.int32, (8, kv_w), 0)
    lane_head = lax.broadcasted_iota(jnp.int32, (8, kv_w), 1) // HEAD_DIM
    lower = row8 < DEC_SEQ
    pick = [jnp.where(lane_head == 2 * hp + jnp.where(lower, 0, 1), 1.0, 0.0).astype(F32) for hp in range(2)]
    lower_w = lax.broadcasted_iota(jnp.int32, (8, HALF), 0) < DEC_SEQ
    kept = lax.broadcasted_iota(jnp.int32, (kv_w, WINDOW), 1) < WINDOW - DEC_SEQ
    seq0 = pl.program_id(0) * ATTN_S_BATCH
    per_tile = WINDOW // DEC_SEQ

    def slide(old, new_tile, shift):
        return jnp.where(kept, pltpu.roll(old, WINDOW - DEC_SEQ, 1), pltpu.roll(new_tile, shift, 1))

    def pair(p, carry):
        r0 = pl.multiple_of(p * 8, 8)
        rows = qg_ref[pl.ds(r0, 8), :]
        q8 = rows[:, 0:HALF]
        gate8 = rows[:, HALF:2 * HALF]
        out8 = []
        for sub in range(2):
            b = 2 * p + sub
            q_swap = pltpu.roll(q8, 4, 0)
            q_dup = jnp.where(lower_w, q8, q_swap) if sub == 0 else jnp.where(lower_w, q_swap, q8)
            tile = (seq0 + b) // per_tile
            shift = (2 * WINDOW - DEC_SEQ - DEC_SEQ * ((seq0 + b) % per_tile)) % WINDOW
            k_old = ck_ref[b].reshape(kv_w, WINDOW)
            v_old = cv_ref[b].reshape(kv_w, WINDOW)
            k_win = slide(k_old, ktn_ref[tile], shift)
            v_win = slide(v_old, vtn_ref[tile], shift)
            nk_ref[b] = k_win.reshape(N_KV, HEAD_DIM, WINDOW)
            nv_ref[b] = v_win.reshape(N_KV, HEAD_DIM, WINDOW)
            k_all = jnp.concatenate([k_old, k_win], axis=1).astype(BF16)
            v_all = jnp.concatenate([v_old, v_win], axis=1).astype(BF16)
            q_bd = jnp.concatenate(
                [q_dup[:, g * kv_w:(g + 1) * kv_w] * pick[hp] for g in range(GROUP) for hp in range(2)], axis=0)
            s = _dot(q_bd.astype(BF16), k_all) + bias_ref[...]
            o = _attn_softmax_pv(s, sink_ref[:, 0:1], v_all, v_transposed=True)
            out_g = []
            for g in range(GROUP):
                acc = None
                for hp in range(2):
                    piece = o[(2 * g + hp) * 8:(2 * g + hp + 1) * 8, :] * pick[hp]
                    piece = piece + pltpu.roll(piece, 4, 0)
                    acc = piece if acc is None else acc + piece
                out_g.append(acc)
            out8.append(jnp.concatenate(
                [out_g[g][:, hk * HEAD_DIM:(hk + 1) * HEAD_DIM] for hk in range(N_KV) for g in range(GROUP)], axis=1))
        o8 = jnp.where(lower_w, out8[0], out8[1])
        yb_ref[pl.ds(r0, 8), :] = (o8 * _silu(gate8)).astype(BF16)
        return carry

    lax.fori_loop(0, ATTN_S_BATCH // 2, pair, 0, unroll=2)


def _sample_bias(rel_bias, sinks):
    t = jnp.arange(DEC_SEQ)[:, None]
    j = jnp.arange(KEYS_PAD)[None, :]
    pos = jnp.where(j < WINDOW, j, j - (KEYS_PAD - DEC_SEQ) + WINDOW)
    dist = t + WINDOW - pos
    valid = (dist >= 0) & (dist < WINDOW) & ((j < WINDOW) | (j >= KEYS_PAD - DEC_SEQ))
    bias = jnp.where(valid[:, :, None], rel_bias.astype(F32)[_rel_bucket(dist)], MASK_VALUE)
    bias = bias.reshape(DEC_SEQ, KEYS_PAD, N_KV, GROUP).transpose(3, 2, 0, 1).reshape(N_HEADS * DEC_SEQ, KEYS_PAD)
    sink = jnp.broadcast_to(sinks.astype(F32).reshape(N_KV, GROUP).T[:, :, None], (GROUP, N_KV, DEC_SEQ))
    return bias, jnp.broadcast_to(sink.reshape(N_HEADS * DEC_SEQ, 1), (N_HEADS * DEC_SEQ, 128))


def _attn_sample(qg, kt_new, vt_new, cache_kt, cache_vt, bias, sink):
    n_seq = cache_kt.shape[0]
    bb = ATTN_S_BATCH
    cache_spec = pl.BlockSpec((bb, N_KV, HEAD_DIM, WINDOW), lambda i: (i, 0, 0, 0))
    return pl.pallas_call(
        _attn_sample_kernel,
        grid=(n_seq // bb,),
        in_specs=[_rows(bb * DEC_SEQ, 2 * HALF), _resident(kt_new.shape), _resident(vt_new.shape),
                  cache_spec, cache_spec, _resident(bias.shape), _resident(sink.shape)],
        out_specs=[_rows(bb * DEC_SEQ, HALF), cache_spec, cache_spec],
        out_shape=[jax.ShapeDtypeStruct((n_seq * DEC_SEQ, HALF), BF16),
                   jax.ShapeDtypeStruct(cache_kt.shape, F32), jax.ShapeDtypeStruct(cache_vt.shape, F32)],
        compiler_params=_params(),
        name="attn_sample",
    )(qg, kt_new, vt_new, cache_kt, cache_vt, bias, sink)


def _prep_layer0(g_pre, w_in, conv_w, rel_bias, sinks, w_out, g_post):
    return dict(
        g_pre=g_pre.reshape(1, D_MODEL), w0=w_in, conv_w=conv_w, rel_bias=rel_bias, sinks=sinks,
        w_out=w_out, g_post=g_post.reshape(1, D_MODEL))


def _layer0(x_p, x_s, conv_state, cache_k, cache_v, w):
    n_seq = x_s.shape[0]
    tile = n_seq * DEC_SEQ
    n_p = x_p.shape[0] // tile
    ya_all, s_tail, s_s, h_all = _conv_a(x_p, x_s, w['g_pre'], w['w0'], w['conv_w'],
                                         conv_state.reshape(n_seq, 2 * HALF))
    yb_p, kwin, vwin, qg, kt_new, vt_new = _attn(h_all, n_p, tile, w['w0'], _prompt_bias_table(w['rel_bias']),
                                                 w['sinks'])
    bias, sink = _sample_bias(w['rel_bias'], w['sinks'])
    yb_s, new_kt, new_vt = _attn_sample(qg, kt_new, vt_new, cache_k.transpose(0, 2, 3, 1),
                                        cache_v.transpose(0, 2, 3, 1), bias, sink)
    (y_all,) = _out_proj(ya_all, (yb_p, yb_s), (x_p, x_s), w['w_out'], w['g_post'], n_p, tile)
    prompt_state = (s_tail[6:8], kwin.reshape(WINDOW, N_KV, HEAD_DIM), vwin.reshape(WINDOW, N_KV, HEAD_DIM))
    sample_state = (s_s[:, DEC_SEQ - 2:], new_kt.transpose(0, 3, 1, 2),
                    new_vt.transpose(0, 3, 1, 2))
    return y_all, prompt_state, sample_state


def _layer_norm(v, g, b):
    xc = v - jnp.mean(v, axis=-1, keepdims=True)
    return xc * lax.rsqrt(jnp.mean(xc * xc, axis=-1, keepdims=True) + NORM_EPS) * g + b


def _cmlp_prompt_kernel(x_ref, g_ref, w_ref, lng_ref, lnb_ref, ws_ref, bs_ref, yc_ref, h_ref, vn_scr):
    tile = x_ref.shape[0]
    h = _rms_bf16(x_ref[...], g_ref[...])
    h_ref[...] = h
    v = _dot_wt(h, w_ref[HALF:2 * HALF, :])
    vn_scr[...] = _layer_norm(v, lng_ref[...], lnb_ref[...]).astype(BF16)
    gw = HALF // CMLP_GROUPS
    cols = 2 * gw
    for cb in range(HALF // cols):
        u = _dot_wt(h, w_ref[cb * cols:(cb + 1) * cols, :])
        gate = _silu(_dot_wt(h, w_ref[2 * HALF + cb * cols:2 * HALF + (cb + 1) * cols, :]))
        for gi in range(2):
            grp = 2 * cb + gi
            lanes = slice(grp * gw, (grp + 1) * gw)
            for n in range(tile // CHUNK):
                rows = slice(n * CHUNK, (n + 1) * CHUNK)
                mixed = _dot(ws_ref[grp], vn_scr[rows, lanes]) + bs_ref[grp]
                yc_ref[rows, lanes] = (u[rows, gi * gw:(gi + 1) * gw] * mixed
                                       * gate[rows, gi * gw:(gi + 1) * gw]).astype(BF16)


def _cmlp_kernel(x_ref, g_ref, w_ref, lng_ref, lnb_ref, ws_ref, bs_ref, coef_ref, bias_ref, yc_ref, h_ref, vns_ref,
                 vn_scr, *, n_p):
    is_sample = pl.program_id(0) == n_p

    @pl.when(jnp.logical_not(is_sample))
    def _():
        _cmlp_prompt_kernel(x_ref, g_ref, w_ref, lng_ref, lnb_ref, ws_ref, bs_ref, yc_ref, h_ref, vn_scr)

    @pl.when(is_sample)
    def _():
        _cmlp_sample_kernel(x_ref, g_ref, w_ref, lng_ref, lnb_ref, coef_ref, bias_ref, yc_ref, vns_ref, h_ref)


def _cmlp(x_all, n_p, tile, w):
    t = x_all.shape[0]
    consts = [w['ln_g'], w['ln_b'], w['ws_tril'], w['bs_rows'], w['coef'], w['bias4']]
    return pl.pallas_call(
        functools.partial(_cmlp_kernel, n_p=n_p),
        grid=(n_p + 1,),
        in_specs=[_rows(tile, D_MODEL), _resident((1, D_MODEL)), _rowwin(3 * HALF, D_MODEL, 0)]
                 + [_resident(c.shape) for c in consts],
        out_specs=[_rows(tile, HALF), _rows(tile, D_MODEL), _resident((tile // DEC_SEQ, DEC_SEQ, HALF))],
        out_shape=[jax.ShapeDtypeStruct((t, HALF), BF16), jax.ShapeDtypeStruct((t, D_MODEL), BF16),
                   jax.ShapeDtypeStruct((tile // DEC_SEQ, DEC_SEQ, HALF), F32)],
        scratch_shapes=[pltpu.VMEM((tile, HALF), BF16)],
        compiler_params=_params(),
        name="cmlp",
    )(x_all, w['g_pre'], w['w1'], *consts)


def _cmlp_sample_kernel(x_ref, g_ref, w_ref, lng_ref, lnb_ref, coef_ref, bias_ref, yc_ref, vn_ref, h_ref):
    t = x_ref.shape[0]
    h = _rms_bf16(x_ref[...], g_ref[...])
    h_ref[...] = h
    u = _dot_wt(h, w_ref[0:HALF, :])
    vn = _layer_norm(_dot_wt(h, w_ref[HALF:2 * HALF, :]), lng_ref[...], lnb_ref[...])
    gate = _silu(_dot_wt(h, w_ref[2 * HALF:3 * HALF, :]))
    vn_ref[...] = vn.reshape(vn_ref.shape)

    def tiled(a):
        return a.reshape(t // 8, 8, HALF)

    mixed = tiled(vn) * coef_ref[0][None] + bias_ref[...][None]
    for k in range(1, DEC_SEQ):
        mixed = mixed + tiled(pltpu.roll(vn, k, 0)) * coef_ref[k][None]
    yc_ref[...] = (u * mixed.reshape(t, HALF) * gate).astype(BF16)


HEAD_LANES = 128
SSD_GW = HALF // SSD_GROUPS


def _softplus(x):
    return jnp.maximum(x, 0.0) + jnp.log1p(jnp.exp(-jnp.abs(x)))


def _dt_proj(h, wdt_ref):
    pad = jnp.zeros((HEAD_LANES - SSD_HEADS, D_MODEL), F32)
    return _dot_wt(h, jnp.concatenate([wdt_ref[...], pad], axis=0))


def _group_norm_gate(y, z, gn):
    gated = y * _silu(z)
    parts = []
    for g in range(SSD_GROUPS):
        part = gated[:, g * SSD_GW:(g + 1) * SSD_GW]
        parts.append(part * lax.rsqrt(jnp.mean(part * part, axis=-1, keepdims=True) + NORM_EPS))
    return (jnp.concatenate(parts, axis=1) * gn).astype(BF16)


def _ssd_prompt_kernel(h_ref, wz_ref, wx0_ref, wx1_ref, wx2_ref, wdt_ref, cw_ref, cb_ref, dtb_ref, alog_ref,
                       dskip_ref, gn_ref, e3_ref, tril3_ref, yd_ref, tail_ref, ssm_ref,
                       xbc_scr, z_scr, dt_scr, ht_scr, shift_scr):
    tile = h_ref.shape[0]
    i = pl.program_id(0)
    cd = SSD_CONV_DIM
    h = h_ref[...]

    @pl.when(i == 0)
    def _():
        tail_ref[...] = jnp.zeros_like(tail_ref)
        ht_scr[...] = jnp.zeros_like(ht_scr)

    z_scr[...] = _dot_wt(h, wz_ref[...])
    dt_scr[...] = _softplus(_dt_proj(h, wdt_ref) + dtb_ref[...])
    third = cd // 3
    for j, wx_ref in enumerate((wx0_ref, wx1_ref, wx2_ref)):
        cols = slice(j * third, (j + 1) * third)
        raw = _dot_wt(h, wx_ref[...])
        shift_scr[0:8, :] = tail_ref[:, cols]
        shift_scr[8:8 + tile, :] = raw
        conv = raw * cw_ref[3:4, cols] + cb_ref[:, cols]
        for k in range(1, 4):
            conv = conv + shift_scr[8 - k:8 - k + tile, :] * cw_ref[3 - k:4 - k, cols]
        xbc_scr[:, cols] = _silu(conv)
        tail_ref[:, cols] = raw[tile - 8:tile, :]

    a16 = -jnp.exp(alog_ref[...])
    causal =(lax.broadcasted_iota(jnp.int32, (CHUNK, CHUNK), 0)
              >= lax.broadcasted_iota(jnp.int32, (CHUNK, CHUNK), 1))
    first_half = lax.broadcasted_iota(jnp.int32, (CHUNK, 2 * HEAD_DIM), 1) < HEAD_DIM
    keep_a = jnp.where(first_half, 1.0, 0.0).astype(BF16)
    keep_b = jnp.where(first_half, 0.0, 1.0).astype(BF16)

    def chunk(n, carry):
        r0 = pl.multiple_of(n * CHUNK, CHUNK)
        rows = pl.ds(r0, CHUNK)
        xs = xbc_scr[rows, 0:HALF]
        dt16 = dt_scr[rows, :]
        dt_e = _dot(jnp.concatenate(_split3(dt16), axis=1), e3_ref[...])
        acs16 = _dot(tril3_ref[...], jnp.concatenate(_split3(dt16 * a16), axis=0))
        acs_e = _dot(jnp.concatenate(_split3(acs16), axis=1), e3_ref[...])
        acs_t = acs16.T
        last_e = acs_e[CHUNK - 1:CHUNK, :]
        xdt = xs * dt_e
        xdt_bf = xdt.astype(BF16)
        xw = (jnp.exp(last_e - acs_e) * xdt).astype(BF16)
        dec_e = jnp.exp(last_e)
        y_parts = []
        yoff_parts = []
        for g in range(SSD_GROUPS):
            c_g = xbc_scr[rows, HALF + 2 * SSD_STATE + g * SSD_STATE:HALF + 2 * SSD_STATE + (g + 1) * SSD_STATE].astype(BF16)
            b_g = xbc_scr[rows, HALF + g * SSD_STATE:HALF + (g + 1) * SSD_STATE].astype(BF16)
            cb = _dot_nt(c_g, b_g)
            h_prev = ht_scr[g]
            yoff_parts.append(_dot(c_g, h_prev.astype(BF16)))
            for r in range(0, SSD_HEADS // SSD_GROUPS, 2):
                wgt = []
                for hd in (g * (SSD_HEADS // SSD_GROUPS) + r, g * (SSD_HEADS // SSD_GROUPS) + r + 1):
                    seg = acs16[:, hd:hd + 1] - acs_t[hd:hd + 1, :]
                    wgt.append(cb * jnp.exp(jnp.where(causal, seg, -jnp.inf)))
                a = g * (SSD_HEADS // SSD_GROUPS) + r
                slab = xdt_bf[:, a * HEAD_DIM:(a + 2) * HEAD_DIM]
                rhs = jnp.concatenate([slab * keep_a, slab * keep_b], axis=0)
                y_parts.append(_dot(jnp.concatenate(wgt, axis=1).astype(BF16), rhs))
            lanes = slice(g * SSD_GW, (g + 1) * SSD_GW)
            ht_scr[g] = h_prev * dec_e[:, lanes] + _dot_tn(b_g, xw[:, lanes])
        y = (jnp.concatenate(y_parts, axis=1) + jnp.concatenate(yoff_parts, axis=1) * jnp.exp(acs_e)
             + dskip_ref[...] * xs)
        yd_ref[rows, :] = _group_norm_gate(y, z_scr[rows, :], gn_ref[...])
        return carry

    lax.fori_loop(0, tile // CHUNK, chunk, 0, unroll=True)

    @pl.when(i == pl.num_programs(0) - 1)
    def _():
        for g in range(SSD_GROUPS):
            ssm_ref[g * SSD_GW:(g + 1) * SSD_GW, :] = ht_scr[g].T


def _ssd_weight_specs():
    third = SSD_CONV_DIM // 3
    first = 4 * HALF // third
    return ([_rowwin(HALF, D_MODEL, 3)] + [_rowwin(third, D_MODEL, first + j) for j in range(3)]
            + [_rowwin(SSD_HEADS, D_MODEL, (4 * HALF + SSD_CONV_DIM) // SSD_HEADS)])


def _ssd_prompt(h, n_p, tile, w):
    t = n_p * tile
    cd = SSD_CONV_DIM
    consts = [w['conv_w'], w['conv_b'], w['dt_bias16'], w['a_log16'],
              w['d_skip_e'], w['gate_norm_g'], w['expand3'], w['tril3']]
    return pl.pallas_call(
        _ssd_prompt_kernel,
        grid=(t // tile,),
        in_specs=[_rows(tile, D_MODEL)] + _ssd_weight_specs() + [_resident(c.shape) for c in consts],
        out_specs=[_rows(tile, HALF), pl.BlockSpec((8, cd), lambda i: (0, 0)),
                   pl.BlockSpec((HALF, SSD_STATE), lambda i: (0, 0))],
        out_shape=[jax.ShapeDtypeStruct((t, HALF), BF16), jax.ShapeDtypeStruct((8, cd), F32),
                   jax.ShapeDtypeStruct((HALF, SSD_STATE), F32)],
        scratch_shapes=[pltpu.VMEM((tile, cd), F32), pltpu.VMEM((tile, HALF), F32),
                        pltpu.VMEM((tile, HEAD_LANES), F32), pltpu.VMEM((SSD_GROUPS, SSD_STATE, SSD_GW), F32),
                        pltpu.VMEM((8 + tile, cd // 3), F32)],
        compiler_params=_params(),
        name="ssd_prompt",
    )(h, w['w1'], w['w1'], w['w1'], w['w1'], w['w1'], *consts)


def _ssd_sample_pre_kernel(h_ref, wz_ref, wx0_ref, wx1_ref, wx2_ref, wdt_ref, cw_ref, cb_ref, st_ref,
                           dtb_ref, aloge_ref, dskip_ref, e3_ref, seg_ref,
                           nconv_ref, z_ref, ysk_ref, eacs_ref, xw_ref, dec_ref, b_ref, c_ref, raw_scr):
    t = h_ref.shape[0]
    n_seq = t // DEC_SEQ
    h = h_ref[...]
    z_ref[...] = _dot_wt(h, wz_ref[...])
    raw = jnp.concatenate([_dot_wt(h, wx0_ref[...]), _dot_wt(h, wx1_ref[...]), _dot_wt(h, wx2_ref[...])], axis=1)
    for c in range(raw_scr.shape[0]):
        lanes = slice(c * 128, (c + 1) * 128)
        raw_scr[c] = raw[:, lanes]
        for j in range(3):
            nconv_ref[j, :, lanes] = raw_scr[c, pl.ds(j + 1, n_seq, stride=DEC_SEQ), :]
    dt16 = _softplus(_dt_proj(h, wdt_ref) + dtb_ref[...])
    dt = _dot(jnp.concatenate(_split3(dt16), axis=1), e3_ref[...])
    old = [st_ref[j] for j in range(3)]
    p1 = _place_steps(t, [(0, old[2])])
    p2 = _place_steps(t, [(0, old[1]), (1, old[2])])
    p3 = _place_steps(t, [(0, old[0]), (1, old[1]), (2, old[2])])

    def step_of(width):
        return lax.broadcasted_iota(jnp.int32, (t, width), 0) % DEC_SEQ

    def back(a, k):
        return jnp.where(step_of(a.shape[1]) >= k, pltpu.roll(a, k, 0), 0.0)

    def ahead(a, k):
        return jnp.where(step_of(a.shape[1]) + k < DEC_SEQ, pltpu.roll(a, t - k, 0), 0.0)

    conv = (raw * cw_ref[3:4, :] + (back(raw, 1) + p1) * cw_ref[2:3, :]
            + (back(raw, 2) + p2) * cw_ref[1:2, :] + (back(raw, 3) + p3) * cw_ref[0:1, :]
            + cb_ref[...])
    xbc = _silu(conv)
    xs = xbc[:, 0:HALF]
    bm = xbc[:, HALF:HALF + 2 * SSD_STATE]
    cm = xbc[:, HALF + 2 * SSD_STATE:]
    b_ref[...] = bm
    c_ref[...] = cm
    da = dt * (-jnp.exp(aloge_ref[...]))
    acs = da + back(da, 1) + back(da, 2) + back(da, 3)
    suffix = ahead(da, 1) + ahead(da, 2) + ahead(da, 3)
    xdt = xs * dt
    y = _dot((cm * bm).astype(BF16), seg_ref[...]) * xdt
    for k in range(1, DEC_SEQ):
        cbk = _dot((cm * pltpu.roll(bm, k, 0)).astype(BF16), seg_ref[...])
        term = cbk * jnp.exp(acs - pltpu.roll(acs, k, 0)) * pltpu.roll(xdt, k, 0)
        y = y + jnp.where(step_of(HALF) >= k, term, 0.0)
    ysk_ref[...] = y + dskip_ref[...] * xs
    eacs_ref[...] = jnp.exp(acs)
    xw_ref[...] = jnp.exp(suffix) * xdt
    dec_ref[...] = jnp.exp(acs + suffix)


def _ssd_sample_pre(h, first_row, conv_state, w):
    t = conv_state.shape[1] * DEC_SEQ
    cd = SSD_CONV_DIM
    tile = min(SSD_PRE_ROWS, t)
    first = first_row // tile
    seqs = tile // DEC_SEQ
    state_spec = pl.BlockSpec((3, seqs, cd), lambda i: (0, i, 0))
    head = [w['conv_w'], w['conv_b']]
    tail = [w['dt_bias16'], w['a_log_e'], w['d_skip_e'], w['expand3'], w['seg_expand']]
    args = [h, w['w1'], w['w1'], w['w1'], w['w1'], w['w1']] + head + [conv_state] + tail
    wide = jax.ShapeDtypeStruct((t, HALF), F32)
    narrow = jax.ShapeDtypeStruct((t, 2 * SSD_STATE), F32)
    out_shape = [jax.ShapeDtypeStruct(conv_state.shape, F32), wide, wide, wide, wide, wide, narrow, narrow]
    return pl.pallas_call(
        _ssd_sample_pre_kernel,
        grid=(t // tile,),
        in_specs=[pl.BlockSpec((tile, D_MODEL), lambda i: (first + i, 0))] + _ssd_weight_specs()
                 + [_resident(c.shape) for c in head] + [state_spec] + [_resident(c.shape) for c in tail],
        out_specs=[state_spec] + [_rows(tile, HALF)] * 5 + [_rows(tile, 2 * SSD_STATE)] * 2,
        out_shape=out_shape,
        scratch_shapes=[pltpu.VMEM((cd // 128, tile, 128), F32)],
        compiler_params=_params(),
        name="ssd_sample_pre",
    )(*args)


SSD_S_BATCH = 8
SSD_PRE_ROWS = 256


def _ssd_sample_state_kernel(st_ref, c_ref, b_ref, xw_ref, dec_ref, eacs_ref, ysk_ref, z_ref, gn_ref,
                             yd_ref, nst_ref):
    row_n = lax.broadcasted_iota(jnp.int32, (8, SSD_STATE), 0)
    row_w = lax.broadcasted_iota(jnp.int32, (8, SSD_GW), 0)
    row_f = lax.broadcasted_iota(jnp.int32, (8, HALF), 0)
    ones_rows = jnp.where((row_n >= 4) & (row_n < 7), 1.0, 0.0).astype(BF16)
    hpg = SSD_HEADS // SSD_GROUPS

    def pair(p, carry):
        r0 = pl.multiple_of(p * 8, 8)
        rows = pl.ds(r0, 8)
        c8 = c_ref[rows, :].astype(BF16)
        b8 = b_ref[rows, :]
        xw8 = xw_ref[rows, :]
        dec8 = dec_ref[rows, :]
        yoff = []
        for sub in range(2):
            b = 2 * p + sub
            xw_own = xw8 if sub == 0 else pltpu.roll(xw8, 4, 0)
            b_own = b8 if sub == 0 else pltpu.roll(b8, 4, 0)
            hi, mid, lo = (term.astype(F32) for term in _split3(dec8[4 * sub:4 * sub + 1, :]))
            parts = []
            for g in range(SSD_GROUPS):
                lanes = slice(g * SSD_GW, (g + 1) * SSD_GW)
                heads = pl.ds(g * hpg, hpg)
                h0 = st_ref[b, heads].reshape(SSD_GW, SSD_STATE)
                parts.append(_dot_nt(c8[:, g * SSD_STATE:(g + 1) * SSD_STATE], h0.astype(BF16)))
                lhs = jnp.where(row_w < 4, xw_own[:, lanes],
                                jnp.where(row_w == 4, hi[:, lanes],
                                          jnp.where(row_w == 5, mid[:, lanes],
                                                    jnp.where(row_w == 6, lo[:, lanes], 0.0)))).astype(BF16)
                rhs_b = jnp.where(row_n < 4, b_own[:, g * SSD_STATE:(g + 1) * SSD_STATE], 0.0).astype(BF16)
                decay = _dot_tn(lhs, ones_rows)
                nst_ref[b, heads] = (h0 * decay + _dot_tn(lhs, rhs_b)).reshape(hpg, HEAD_DIM, SSD_STATE)
            yoff.append(jnp.concatenate(parts, axis=1))
        yoff8 = jnp.where(row_f < 4, yoff[0], yoff[1])
        y = ysk_ref[rows, :] + yoff8 * eacs_ref[rows, :]
        yd_ref[rows, :] = _group_norm_gate(y, z_ref[rows, :], gn_ref[...])
        return carry

    lax.fori_loop(0, SSD_S_BATCH // 2, pair, 0, unroll=True)


def _ssd_sample_state(state, cm, bm, xw, dec, eacs, ysk, z, gn):
    n_seq = state.shape[0]
    bb = SSD_S_BATCH
    r = bb * DEC_SEQ
    st_spec = pl.BlockSpec((bb, SSD_HEADS, HEAD_DIM, SSD_STATE), lambda i: (i, 0, 0, 0))
    return pl.pallas_call(
        _ssd_sample_state_kernel,
        grid=(n_seq // bb,),
        in_specs=[st_spec, _rows(r, 2 * SSD_STATE), _rows(r, 2 * SSD_STATE)] + [_rows(r, HALF)] * 5
                 + [_resident((1, HALF))],
        out_specs=[_rows(r, HALF), st_spec],
        out_shape=[jax.ShapeDtypeStruct((n_seq * DEC_SEQ, HALF), BF16), jax.ShapeDtypeStruct(state.shape, F32)],
        compiler_params=_params(),
        name="ssd_sample_state",
    )(state, cm, bm, xw, dec, eacs, ysk, z, gn)


def _prep_layer1(g_pre, w_in, ln_g, ln_b, w_s, b_s, conv_w, conv_b, dt_bias, a_log, d_skip, gate_norm_g,
                 w_out, g_post):
    cd = SSD_CONV_DIM
    gw = HALF // CMLP_GROUPS
    w1 = w_in.T

    def lanes16(v):
        return jnp.pad(v.astype(F32), (0, HEAD_LANES - SSD_HEADS)).reshape(1, HEAD_LANES)

    def per_channel(v):
        return jnp.repeat(v.astype(F32), HEAD_DIM).reshape(1, HALF)

    head_of = np.arange(HALF) // HEAD_DIM
    expand = jnp.asarray(np.arange(HEAD_LANES)[:, None] == head_of[None, :], BF16)
    tril = jnp.asarray(np.tril(np.ones((CHUNK, CHUNK))), BF16)
    grp_rows = np.arange(2 * SSD_STATE) // SSD_STATE
    seg_expand = jnp.asarray(grp_rows[:, None] == (head_of // (SSD_HEADS // SSD_GROUPS))[None, :], BF16)

    w4 = jnp.tril(w_s[:, :DEC_SEQ, :DEC_SEQ])
    steps = jnp.arange(DEC_SEQ)
    coef = []
    for k in range(DEC_SEQ):
        src = steps - k
        ck = jnp.where((src >= 0)[None, :], w4[:, steps, jnp.maximum(src, 0)], 0.0)
        ck = jnp.repeat(ck.T, gw, axis=1)
        coef.append(jnp.concatenate([ck, ck], axis=0))
    bias4 = jnp.repeat(b_s[:, :DEC_SEQ].T, gw, axis=1)
    return dict(
        g_pre=g_pre.reshape(1, D_MODEL), w1=w1, ln_g=ln_g.reshape(1, HALF), ln_b=ln_b.reshape(1, HALF),
        ws_tril=jnp.tril(w_s).astype(BF16),
        bs_rows=jnp.broadcast_to(b_s.astype(F32)[:, :, None], (CMLP_GROUPS, CHUNK, gw)),
        coef=jnp.stack(coef).astype(F32), bias4=jnp.concatenate([bias4, bias4], axis=0).astype(F32),
        conv_w=conv_w, conv_b=conv_b.reshape(1, cd), dt_bias16=lanes16(dt_bias), a_log16=lanes16(a_log),
        a_log_e=per_channel(a_log), d_skip_e=per_channel(d_skip), gate_norm_g=gate_norm_g.reshape(1, HALF),
        expand3=jnp.concatenate([expand] * 3, axis=0), tril3=jnp.concatenate([tril] * 3, axis=1),
        seg_expand=seg_expand, w_out=w_out, g_post=g_post.reshape(1, D_MODEL))


def _layer1(x_all, n_p, conv_state, ssm_state, w):
    n_seq = ssm_state.shape[0]
    tile = n_seq * DEC_SEQ
    yc_all, h_all, vn = _cmlp(x_all, n_p, tile, w)
    yd_p, tail, ssm = _ssd_prompt(h_all, n_p, tile, w)
    new_conv, z, ysk, eacs, xw, dec, bm, cm = _ssd_sample_pre(h_all, n_p * tile, conv_state.transpose(1, 0, 2), w)
    yd_s, new_state = _ssd_sample_state(ssm_state, cm, bm, xw, dec, eacs, ysk, z, w['gate_norm_g'])
    y_p, y_s = _out_proj(yc_all, (yd_p, yd_s), x_all, w['w_out'], w['g_post'], n_p, tile,
                         sample_out_shape=(n_seq, DEC_SEQ, D_MODEL))
    prompt_out = (y_p, tail[5:8], ssm.reshape(SSD_HEADS, HEAD_DIM, SSD_STATE))
    sample_out = (y_s, vn, new_conv.transpose(1, 0, 2), new_state)
    return prompt_out, sample_out


def kernel(x_prompt, x_sample, state_conv_a, cache_win_k, cache_win_v, state_conv_d, state_ssm, rel_bias,
           l0_g_pre, l0_w_in, l0_conv_w, l0_sinks, l0_w_out, l0_g_post,
           l1_g_pre, l1_w_in, l1_ln_g, l1_ln_b, l1_w_s, l1_b_s, l1_conv_w, l1_conv_b, l1_dt_bias, l1_a_log,
           l1_d_skip, l1_gate_norm_g, l1_w_out, l1_g_post):
    w0 = _prep_layer0(l0_g_pre, l0_w_in, l0_conv_w, rel_bias, l0_sinks, l0_w_out, l0_g_post)
    w1 = _prep_layer1(l1_g_pre, l1_w_in, l1_ln_g, l1_ln_b, l1_w_s, l1_b_s, l1_conv_w, l1_conv_b, l1_dt_bias,
                      l1_a_log, l1_d_skip, l1_gate_norm_g, l1_w_out, l1_g_post)
    x_p = x_prompt[0]
    n_p = x_p.shape[0] // (x_sample.shape[0] * DEC_SEQ)
    y_all, (p_conv_a, p_win_k, p_win_v), (s_conv_a, s_win_k, s_win_v) = _layer0(
        x_p, x_sample, state_conv_a, cache_win_k, cache_win_v, w0)
    (yp, p_conv_d, p_ssm), (ys, s_chunk_v, s_conv_d, s_ssm) = _layer1(y_all, n_p, state_conv_d, state_ssm, w1)
    return (yp[None], ys, p_conv_a[None], s_conv_a, p_win_k[None], p_win_v[None], s_win_k, s_win_v, s_chunk_v,
            p_conv_d[None], s_conv_d, p_ssm[None], s_ssm)
```

```python
import functools
import math

import jax
import jax.numpy as jnp
import numpy as np
from jax import lax
from jax.experimental import pallas as pl
from jax.experimental.pallas import tpu as pltpu

F32 = jnp.float32
BF16 = jnp.bfloat16

D_MODEL = 2048
HALF = 1024
HEAD_DIM = 64
N_HEADS = 16
N_KV = 4
GROUP = 4
WINDOW = 128
NUM_BUCKETS = 32
MAX_DISTANCE = 128
CMLP_GROUPS = 8
CHUNK = 128
SSD_HEADS = 16
SSD_STATE = 128
SSD_GROUPS = 2
SSD_CONV_DIM = HALF + 2 * SSD_GROUPS * SSD_STATE
DEC_SEQ = 4
NORM_EPS = 1e-6
MASK_VALUE = -1e30

V7X_VMEM_BYTES = 64 * 1024 * 1024
VMEM_LIMIT = V7X_VMEM_BYTES - 8 * 1024 * 1024
CONV_A_VMEM_LIMIT = V7X_VMEM_BYTES - 4 * 1024 * 1024


def _params(n_axes=1):
    return pltpu.CompilerParams(dimension_semantics=("arbitrary",) * n_axes,
                                vmem_limit_bytes=VMEM_LIMIT)


def _resident(shape):
    nd = len(shape)
    return pl.BlockSpec(shape, lambda *_: (0,) * nd, pipeline_mode=pl.Buffered(1))


def _rows(tile, width):
    return pl.BlockSpec((tile, width), lambda i: (i, 0))


def _cols(rows, width, block):
    return pl.BlockSpec((rows, width), lambda *_: (0, block), pipeline_mode=pl.Buffered(1))


def _rowwin(height, cols, block):
    return pl.BlockSpec((height, cols), lambda *_: (block, 0), pipeline_mode=pl.Buffered(1))


def _rms_bf16(x, g):
    ms = jnp.mean(x * x, axis=-1, keepdims=True)
    return (x * lax.rsqrt(ms + NORM_EPS) * g).astype(BF16)


def _silu(x):
    return x * jax.nn.sigmoid(x)


def _dot(a, b):
    return jnp.dot(a, b, preferred_element_type=F32)


def _dot_nt(a, b):
    return lax.dot_general(a, b, (((1,), (1,)), ((), ())), preferred_element_type=F32)


def _dot_tn(a, b):
    return lax.dot_general(a, b, (((0,), (0,)), ((), ())), preferred_element_type=F32)


def _dot_w(a, w):
    return _dot(a, w.astype(BF16))


def _dot_wt(a, wt):
    return _dot_nt(a, wt.astype(BF16))


def _split3(x):
    hi = x.astype(BF16)
    r1 = x - hi.astype(F32)
    mid = r1.astype(BF16)
    lo = (r1 - mid.astype(F32)).astype(BF16)
    return hi, mid, lo


def _place_steps(t, placements):
    n_seq = placements[0][1].shape[0]
    row = lax.broadcasted_iota(jnp.int32, (t, n_seq), 0)
    seq = lax.broadcasted_iota(jnp.int32, (t, n_seq), 1)
    lhs, rhs = [], []
    for step, state in placements:
        sel = jnp.where(row == DEC_SEQ * seq + step, 1.0, 0.0).astype(BF16)
        lhs += [sel] * 3
        rhs += list(_split3(state))
    return _dot(jnp.concatenate(lhs, axis=1), jnp.concatenate(rhs, axis=0))


def _prompt_rows(tile, width, n_p):
    return pl.BlockSpec((tile, width), lambda i: (jnp.minimum(i, n_p - 1), 0))


def _group_specs(arg, tile, width, n_p):
    if isinstance(arg, tuple):
        return [_prompt_rows(tile, width, n_p), _resident(arg[1].shape)]
    return [_rows(tile, width)]


def _out_proj_tile(ya_ref, yb_ref, x_ref, w_ref, g_ref, o_ref):
    y = _dot_w(ya_ref[...], w_ref[0:HALF, :]) + _dot_w(yb_ref[...], w_ref[HALF:2 * HALF, :])
    ms = jnp.mean(y * y, axis=-1, keepdims=True)
    out = x_ref[...].reshape(y.shape) + y * lax.rsqrt(ms + NORM_EPS) * g_ref[...]
    o_ref[...] = out.reshape(o_ref.shape)


def _out_proj_kernel(*refs, n_p, n_ya, n_yb, n_x):
    refs = list(refs)
    ya, yb, x = refs[:n_ya], refs[n_ya:n_ya + n_yb], refs[n_ya + n_yb:n_ya + n_yb + n_x]
    w_ref, g_ref = refs[n_ya + n_yb + n_x:n_ya + n_yb + n_x + 2]
    outs = refs[n_ya + n_yb + n_x + 2:]
    is_sample = pl.program_id(0) == n_p

    @pl.when(jnp.logical_not(is_sample))
    def _():
        _out_proj_tile(ya[0], yb[0], x[0], w_ref, g_ref, outs[0])

    @pl.when(is_sample)
    def _():
        _out_proj_tile(ya[-1], yb[-1], x[-1], w_ref, g_ref, outs[-1])


def _out_proj(ya, yb, x, w, g, n_p, tile, sample_out_shape=None):
    out_pair = sample_out_shape is not None
    groups = [(ya, HALF), (yb, HALF), (x, D_MODEL)]
    in_specs, args = [], []
    for arg, width in groups:
        in_specs += _group_specs(arg, tile, width, n_p)
        args += list(arg) if isinstance(arg, tuple) else [arg]
    if out_pair:
        out_specs = [_prompt_rows(tile, D_MODEL, n_p), _resident(sample_out_shape)]
        out_shape = [jax.ShapeDtypeStruct((n_p * tile, D_MODEL), F32), jax.ShapeDtypeStruct(sample_out_shape, F32)]
    else:
        out_specs = [_rows(tile, D_MODEL)]
        out_shape = [jax.ShapeDtypeStruct(((n_p + 1) * tile, D_MODEL), F32)]
    n_of = [2 if isinstance(arg, tuple) else 1 for arg, _ in groups]
    return pl.pallas_call(
        functools.partial(_out_proj_kernel, n_p=n_p, n_ya=n_of[0], n_yb=n_of[1], n_x=n_of[2]),
        grid=(n_p + 1,),
        in_specs=in_specs + [_resident((2 * HALF, D_MODEL)), _resident((1, D_MODEL))],
        out_specs=out_specs,
        out_shape=out_shape,
        compiler_params=_params(),
        name="out_proj",
    )(*args, w, g)


CONV_A_CHUNK = 256


def _conv_a_kernel(*refs, sample):
    if sample:
        x_ref, g_ref, w_ref, cw_ref, st_ref, ya_ref, s_ref, h_ref = refs
    else:
        x_ref, g_ref, w_ref, cw_ref, ya_ref, new_ref, h_ref, shift_scr, s_ref = refs
    tile = ya_ref.shape[0]
    cc = CONV_A_CHUNK
    h = _rms_bf16(x_ref[...].reshape(tile, D_MODEL), g_ref[...])
    h_ref[...] = h
    if not sample:
        @pl.when(pl.program_id(0) == 0)
        def _():
            s_ref[...] = jnp.zeros_like(s_ref)
    for c in range(HALF // cc):
        lanes = slice(c * cc, (c + 1) * cc)
        a_b, a_c, a_h, a_g = (_dot_w(h, w_ref[:, j * HALF + c * cc:j * HALF + (c + 1) * cc]) for j in range(4))
        s = a_c * a_h
        if sample:
            t_in = lax.broadcasted_iota(jnp.int32, s.shape, 0) % DEC_SEQ
            old0 = st_ref[:, c * cc:(c + 1) * cc]
            old1 = st_ref[:, HALF + c * cc:HALF + (c + 1) * cc]
            p1 = jnp.where(t_in >= 1, pltpu.roll(s, 1, 0), 0.0) + _place_steps(tile, [(0, old1)])
            p2 = jnp.where(t_in >= 2, pltpu.roll(s, 2, 0), 0.0) + _place_steps(tile, [(0, old0), (1, old1)])
            s_ref[:, :, lanes] = s.reshape(tile // DEC_SEQ, DEC_SEQ, cc)[:, DEC_SEQ - 2:, :]
        else:
            shift_scr[0:8, :] = s_ref[:, lanes]
            shift_scr[8:8 + tile, :] = s
            p1 = shift_scr[7:7 + tile, :]
            p2 = shift_scr[6:6 + tile, :]
            s_ref[:, lanes] = s[tile - 8:tile, :]
            new_ref[0, :, lanes] = s[tile - 2:tile, :]
        conv = p2 * cw_ref[0:1, lanes] + p1 * cw_ref[1:2, lanes] + s * cw_ref[2:3, lanes]
        ya_ref[:, lanes] = (a_b * conv * _silu(a_g)).astype(BF16)


def _conv_a_rowspace_kernel(xp_ref, xs_ref, g_ref, w_ref, cw_ref, st_ref, ya_ref, new_ref, ss_ref, h_ref, shift_scr,
                            tail_scr, *, n_p):
    is_sample = pl.program_id(0) == n_p

    @pl.when(jnp.logical_not(is_sample))
    def _():
        _conv_a_kernel(xp_ref, g_ref, w_ref, cw_ref, ya_ref, new_ref, h_ref, shift_scr, tail_scr, sample=False)

    @pl.when(is_sample)
    def _():
        _conv_a_kernel(xs_ref, g_ref, w_ref, cw_ref, st_ref, ya_ref, ss_ref, h_ref, sample=True)


def _conv_a(x_p, x_s, g_pre, w0, conv_w, state):
    tile = x_s.shape[0] * x_s.shape[1]
    n_p = x_p.shape[0] // tile
    t = (n_p + 1) * tile
    return pl.pallas_call(
        functools.partial(_conv_a_rowspace_kernel, n_p=n_p),
        grid=(n_p + 1,),
        in_specs=[_prompt_rows(tile, D_MODEL, n_p), _resident(x_s.shape), _resident((1, D_MODEL)),
                  _cols(D_MODEL, 4 * HALF, 0), _resident((3, HALF)), _resident(state.shape)],
        out_specs=[_rows(tile, HALF), _resident((1, 2, HALF)), _resident((x_s.shape[0], 2, HALF)),
                   _rows(tile, D_MODEL)],
        out_shape=[jax.ShapeDtypeStruct((t, HALF), BF16), jax.ShapeDtypeStruct((1, 2, HALF), F32),
                   jax.ShapeDtypeStruct((x_s.shape[0], 2, HALF), F32), jax.ShapeDtypeStruct((t, D_MODEL), BF16)],
        scratch_shapes=[pltpu.VMEM((8 + tile, CONV_A_CHUNK), F32), pltpu.VMEM((8, HALF), F32)],
        compiler_params=pltpu.CompilerParams(dimension_semantics=("arbitrary",), vmem_limit_bytes=CONV_A_VMEM_LIMIT),
        name="conv_a",
    )(x_p, x_s, g_pre, w0, conv_w, state)


def _rel_bucket(dist):
    max_exact = NUM_BUCKETS // 2
    d = jnp.maximum(dist, 0)
    ratio = jnp.maximum(d, max_exact).astype(F32) / max_exact
    large = max_exact + (jnp.log(ratio) / math.log(MAX_DISTANCE / max_exact)
                         * (NUM_BUCKETS - max_exact)).astype(jnp.int32)
    return jnp.where(d < max_exact, d, jnp.minimum(large, NUM_BUCKETS - 1))


def _attn_softmax_pv(s, sink, v_bf, v_transposed=False):
    m = jnp.maximum(jnp.max(s, axis=-1, keepdims=True), sink)
    p = jnp.exp(s - m)
    den = jnp.sum(p, axis=-1, keepdims=True) + jnp.exp(sink - m)
    pv = _dot_nt(p.astype(BF16), v_bf) if v_transposed else _dot(p.astype(BF16), v_bf)
    return pv / den


def _attn_prompt_kernel(h_ref, wq_ref, wkv_ref, wg0_ref, wg1_ref, tab_ref, sink_ref, yb_ref,
                        q_scr, gate_scr, k_scr, v_scr, bias_scr, kv_scr):
    tile = h_ref.shape[0]
    i = pl.program_id(0)
    kv_w = N_KV * HEAD_DIM
    h = h_ref[...]

    kw, vw = 2 * HEAD_DIM, 4 * HEAD_DIM

    @pl.when(i == 0)
    def _():
        k_scr[0:WINDOW, :] = jnp.zeros((WINDOW, N_KV * kw), BF16)
        v_scr[0:WINDOW, :] = jnp.zeros((WINDOW, N_KV * vw), BF16)
        for hk in range(N_KV):
            v_scr[:, hk * vw + kw:(hk + 1) * vw] = jnp.ones((tile + WINDOW, kw), BF16)
        in_own = lax.broadcasted_iota(jnp.int32, (WINDOW, 2 * WINDOW), 1) >= WINDOW
        for head in range(N_HEADS):
            row = jnp.broadcast_to(tab_ref[head:head + 1, :], (WINDOW, BIAS_SPAN))
            band = pltpu.roll(row, 0, 1, stride=1, stride_axis=0)[:, 0:2 * WINDOW]
            rows = slice((head % 2) * WINDOW, (head % 2 + 1) * WINDOW)
            bias_scr[1, head // 2, rows, :] = band
            bias_scr[0, head // 2, rows, :] = jnp.where(in_own, band, MASK_VALUE)

    q_scr[...] = (_dot_w(h, wq_ref[...]) * (HEAD_DIM ** -0.5)).astype(BF16)
    k = _dot_w(h, wkv_ref[:, 0:kv_w])
    v = _dot_w(h, wkv_ref[:, kv_w:2 * kv_w])
    gate_scr[:, 0:HALF // 2] = _silu(_dot_w(h, wg0_ref[...]))
    gate_scr[:, HALF // 2:HALF] = _silu(_dot_w(h, wg1_ref[...]))
    for hk in range(N_KV):
        k_h = k[:, hk * HEAD_DIM:(hk + 1) * HEAD_DIM].astype(BF16)
        v_h = v[:, hk * HEAD_DIM:(hk + 1) * HEAD_DIM].astype(BF16)
        k_scr[WINDOW:WINDOW + tile, hk * kw:(hk + 1) * kw] = jnp.concatenate([k_h, k_h], axis=1)
        v_scr[WINDOW:WINDOW + tile, hk * vw:hk * vw + kw] = jnp.concatenate([v_h, v_h], axis=1)
    kv_scr[0:WINDOW, 0:kv_w] = k[tile - WINDOW:tile, :]
    kv_scr[0:WINDOW, kv_w:2 * kv_w] = v[tile - WINDOW:tile, :]

    lane = lax.broadcasted_iota(jnp.int32, (WINDOW, kw), 1)
    lo = lane < HEAD_DIM
    keep_a = jnp.where(lo, 1.0, 0.0).astype(BF16)
    keep_b = jnp.where(lo, 0.0, 1.0).astype(BF16)
    is_a = lax.broadcasted_iota(jnp.int32, (2 * WINDOW, 1), 0) < WINDOW

    def block(n, carry):
        r0 = pl.multiple_of(n * WINDOW, WINDOW)
        rows = pl.ds(r0, WINDOW)
        keys = pl.ds(r0, 2 * WINDOW)
        first = jnp.where(jnp.logical_and(i == 0, n == 0), 0, 1)
        for hk in range(N_KV):
            for gp in range(GROUP // 2):
                a = hk * GROUP + 2 * gp
                slab = slice(a * HEAD_DIM, (a + 2) * HEAD_DIM)
                q2 = q_scr[rows, slab]
                lhs = jnp.concatenate([q2 * keep_a, q2 * keep_b], axis=0)
                s = _dot_nt(lhs, k_scr[keys, hk * kw:(hk + 1) * kw]) + bias_scr[first, a // 2]
                sink = jnp.where(is_a, sink_ref[a], sink_ref[a + 1])
                m = jnp.maximum(jnp.max(s, axis=-1, keepdims=True), sink)
                p = jnp.exp(s - m).astype(BF16)
                pv = _dot(p, v_scr[keys, hk * vw:(hk + 1) * vw])
                num = jnp.where(lo, pv[0:WINDOW, 0:kw], pv[WINDOW:2 * WINDOW, 0:kw])
                den = jnp.where(lo, pv[0:WINDOW, kw:2 * kw], pv[WINDOW:2 * WINDOW, kw:2 * kw])
                m_slab = jnp.where(lo, m[0:WINDOW], m[WINDOW:2 * WINDOW])
                den = den + jnp.exp(jnp.where(lo, sink_ref[a], sink_ref[a + 1]) - m_slab)
                yb_ref[rows, slab] = (num / den * gate_scr[rows, slab]).astype(BF16)
        return carry

    lax.fori_loop(0, tile // WINDOW, block, 0, unroll=True)
    k_scr[0:WINDOW, :] = k_scr[tile:tile + WINDOW, :]
    v_scr[0:WINDOW, :] = v_scr[tile:tile + WINDOW, :]


BIAS_SPAN = 3 * WINDOW


def _prompt_bias_table(rel_bias):
    dist = WINDOW - jnp.arange(BIAS_SPAN)
    table = jnp.where(((dist >= 0) & (dist < WINDOW))[:, None], rel_bias.astype(F32)[_rel_bucket(dist)], MASK_VALUE)
    return table.T


def _attn_proj_kernel(h_ref, wq_ref, wkv_ref, wg0_ref, wg1_ref, kwin_ref, vwin_ref, qg_ref, kt_ref, vt_ref, kv_scr):
    kv_w = N_KV * HEAD_DIM
    kwin_ref[...] = kv_scr[0:WINDOW, 0:kv_w].T
    vwin_ref[...] = kv_scr[0:WINDOW, kv_w:2 * kv_w].T
    h = h_ref[...]
    q = _dot_w(h, wq_ref[...]) * (HEAD_DIM ** -0.5)
    for hk in range(N_KV):
        for g in range(GROUP):
            src = (hk * GROUP + g) * HEAD_DIM
            dst = (g * N_KV + hk) * HEAD_DIM
            qg_ref[:, dst:dst + HEAD_DIM] = q[:, src:src + HEAD_DIM]
    qg_ref[:, HALF:HALF + HALF // 2] = _dot_w(h, wg0_ref[...])
    qg_ref[:, HALF + HALF // 2:2 * HALF] = _dot_w(h, wg1_ref[...])
    kv_scr[...] = _dot_w(h, wkv_ref[...])
    for j in range(kt_ref.shape[0]):
        kt_ref[j] = kv_scr[j * WINDOW:(j + 1) * WINDOW, 0:kv_w].T
        vt_ref[j] = kv_scr[j * WINDOW:(j + 1) * WINDOW, kv_w:2 * kv_w].T


def _attn_rowspace_kernel(h_ref, wq_ref, wkv_ref, wg0_ref, wg1_ref, tab_ref, sink_ref,
                          yb_ref, kwin_ref, vwin_ref, qg_ref, kt_ref, vt_ref,
                          q_scr, gate_scr, k_scr, v_scr, bias_scr, kv_scr, *, n_p):
    is_sample = pl.program_id(0) == n_p

    @pl.when(jnp.logical_not(is_sample))
    def _():
        _attn_prompt_kernel(h_ref, wq_ref, wkv_ref, wg0_ref, wg1_ref, tab_ref, sink_ref, yb_ref,
                            q_scr, gate_scr, k_scr, v_scr, bias_scr, kv_scr)

    @pl.when(is_sample)
    def _():
        _attn_proj_kernel(h_ref, wq_ref, wkv_ref, wg0_ref, wg1_ref, kwin_ref, vwin_ref, qg_ref, kt_ref, vt_ref,
                          kv_scr)


def _attn(h_all, n_p, tile, w0, table, sinks):
    kv_w = N_KV * HEAD_DIM
    t = n_p * tile
    win_spec = _resident((kv_w, WINDOW))
    out_shape = [jax.ShapeDtypeStruct((t, HALF), BF16),
                 jax.ShapeDtypeStruct((kv_w, WINDOW), F32), jax.ShapeDtypeStruct((kv_w, WINDOW), F32),
                 jax.ShapeDtypeStruct((tile, 2 * HALF), F32), jax.ShapeDtypeStruct((tile // WINDOW, kv_w, WINDOW), F32),
                 jax.ShapeDtypeStruct((tile // WINDOW, kv_w, WINDOW), F32)]
    return pl.pallas_call(
        functools.partial(_attn_rowspace_kernel, n_p=n_p),
        grid=(n_p + 1,),
        in_specs=[_rows(tile, D_MODEL),
                  _cols(D_MODEL, HALF, 4), _cols(D_MODEL, 2 * kv_w, 10),
                  _cols(D_MODEL, HALF // 2, 11), _cols(D_MODEL, HALF // 2, 12),
                  _resident(table.shape), pl.BlockSpec(memory_space=pltpu.SMEM)],
        out_specs=[_prompt_rows(tile, HALF, n_p), win_spec, win_spec] + [_resident(s.shape) for s in out_shape[3:]],
        out_shape=out_shape,
        scratch_shapes=[pltpu.VMEM((tile, HALF), BF16), pltpu.VMEM((tile, HALF), F32),
                        pltpu.VMEM((tile + WINDOW, 2 * kv_w), BF16), pltpu.VMEM((tile + WINDOW, 4 * kv_w), BF16),
                        pltpu.VMEM((2, N_HEADS // 2, 2 * WINDOW, 2 * WINDOW), F32),
                        pltpu.VMEM((tile, 2 * kv_w), F32)],
        compiler_params=_params(),
        name="attn",
    )(h_all, w0, w0, w0, w0, table, sinks)


ATTN_S_BATCH = 16
KEYS_PAD = 2 * WINDOW


def _attn_sample_kernel(qg_ref, ktn_ref, vtn_ref, ck_ref, cv_ref, bias_ref, sink_ref, yb_ref, nk_ref, nv_ref):
    kv_w = N_KV * HEAD_DIM
    row8 = lax.broadcasted_iota(jnp.int32, (8, kv_w), 0)
    lane_head = lax.broadcasted_iota(jnp.int32, (8, kv_w), 1) // HEAD_DIM
    lower = row8 < DEC_SEQ
    pick = [jnp.where(lane_head == 2 * hp + jnp.where(lower, 0, 1), 1.0, 0.0).astype(F32) for hp in range(2)]
    lower_w = lax.broadcasted_iota(jnp.int32, (8, HALF), 0) < DEC_SEQ
    kept = lax.broadcasted_iota(jnp.int32, (kv_w, WINDOW), 1) < WINDOW - DEC_SEQ
    seq0 = pl.program_id(0) * ATTN_S_BATCH
    per_tile = WINDOW // DEC_SEQ

    def slide(old, new_tile, shift):
        return jnp.where(kept, pltpu.roll(old, WINDOW - DEC_SEQ, 1), pltpu.roll(new_tile, shift, 1))

    def pair(p, carry):
        r0 = pl.multiple_of(p * 8, 8)
        rows = qg_ref[pl.ds(r0, 8), :]
        q8 = rows[:, 0:HALF]
        gate8 = rows[:, HALF:2 * HALF]
        out8 = []
        for sub in range(2):
            b = 2 * p + sub
            q_swap = pltpu.roll(q8, 4, 0)
            q_dup = jnp.where(lower_w, q8, q_swap) if sub == 0 else jnp.where(lower_w, q_swap, q8)
            tile = (seq0 + b) // per_tile
            shift = (2 * WINDOW - DEC_SEQ - DEC_SEQ * ((seq0 + b) % per_tile)) % WINDOW
            k_old = ck_ref[b].reshape(kv_w, WINDOW)
            v_old = cv_ref[b].reshape(kv_w, WINDOW)
            k_win = slide(k_old, ktn_ref[tile], shift)
            v_win = slide(v_old, vtn_ref[tile], shift)
            nk_ref[b] = k_win.reshape(N_KV, HEAD_DIM, WINDOW)
            nv_ref[b] = v_win.reshape(N_KV, HEAD_DIM, WINDOW)
            k_all = jnp.concatenate([k_old, k_win], axis=1).astype(BF16)
            v_all = jnp.concatenate([v_old, v_win], axis=1).astype(BF16)
            q_bd = jnp.concatenate(
                [q_dup[:, g * kv_w:(g + 1) * kv_w] * pick[hp] for g in range(GROUP) for hp in range(2)], axis=0)
            s = _dot(q_bd.astype(BF16), k_all) + bias_ref[...]
            o = _attn_softmax_pv(s, sink_ref[:, 0:1], v_all, v_transposed=True)
            out_g = []
            for g in range(GROUP):
                acc = None
                for hp in range(2):
                    piece = o[(2 * g + hp) * 8:(2 * g + hp + 1) * 8, :] * pick[hp]
                    piece = piece + pltpu.roll(piece, 4, 0)
                    acc = piece if acc is None else acc + piece
                out_g.append(acc)
            out8.append(jnp.concatenate(
                [out_g[g][:, hk * HEAD_DIM:(hk + 1) * HEAD_DIM] for hk in range(N_KV) for g in range(GROUP)], axis=1))
        o8 = jnp.where(lower_w, out8[0], out8[1])
        yb_ref[pl.ds(r0, 8), :] = (o8 * _silu(gate8)).astype(BF16)
        return carry

    lax.fori_loop(0, ATTN_S_BATCH // 2, pair, 0, unroll=2)


def _sample_bias(rel_bias, sinks):
    t = jnp.arange(DEC_SEQ)[:, None]
    j = jnp.arange(KEYS_PAD)[None, :]
    pos = jnp.where(j < WINDOW, j, j - (KEYS_PAD - DEC_SEQ) + WINDOW)
    dist = t + WINDOW - pos
    valid = (dist >= 0) & (dist < WINDOW) & ((j < WINDOW) | (j >= KEYS_PAD - DEC_SEQ))
    bias = jnp.where(valid[:, :, None], rel_bias.astype(F32)[_rel_bucket(dist)], MASK_VALUE)
    bias = bias.reshape(DEC_SEQ, KEYS_PAD, N_KV, GROUP).transpose(3, 2, 0, 1).reshape(N_HEADS * DEC_SEQ, KEYS_PAD)
    sink = jnp.broadcast_to(sinks.astype(F32).reshape(N_KV, GROUP).T[:, :, None], (GROUP, N_KV, DEC_SEQ))
    return bias, jnp.broadcast_to(sink.reshape(N_HEADS * DEC_SEQ, 1), (N_HEADS * DEC_SEQ, 128))


def _attn_sample(qg, kt_new, vt_new, cache_kt, cache_vt, bias, sink):
    n_seq = cache_kt.shape[0]
    bb = ATTN_S_BATCH
    cache_spec = pl.BlockSpec((bb, N_KV, HEAD_DIM, WINDOW), lambda i: (i, 0, 0, 0))
    return pl.pallas_call(
        _attn_sample_kernel,
        grid=(n_seq // bb,),
        in_specs=[_rows(bb * DEC_SEQ, 2 * HALF), _resident(kt_new.shape), _resident(vt_new.shape),
                  cache_spec, cache_spec, _resident(bias.shape), _resident(sink.shape)],
        out_specs=[_rows(bb * DEC_SEQ, HALF), cache_spec, cache_spec],
        out_shape=[jax.ShapeDtypeStruct((n_seq * DEC_SEQ, HALF), BF16),
                   jax.ShapeDtypeStruct(cache_kt.shape, F32), jax.ShapeDtypeStruct(cache_vt.shape, F32)],
        compiler_params=_params(),
        name="attn_sample",
    )(qg, kt_new, vt_new, cache_kt, cache_vt, bias, sink)


def _prep_layer0(g_pre, w_in, conv_w, rel_bias, sinks, w_out, g_post):
    return dict(
        g_pre=g_pre.reshape(1, D_MODEL), w0=w_in, conv_w=conv_w, rel_bias=rel_bias, sinks=sinks,
        w_out=w_out, g_post=g_post.reshape(1, D_MODEL))


def _layer0(x_p, x_s, conv_state, cache_k, cache_v, w):
    n_seq = x_s.shape[0]
    tile = n_seq * DEC_SEQ
    n_p = x_p.shape[0] // tile
    ya_all, conv_p, conv_s, h_all = _conv_a(x_p, x_s, w['g_pre'], w['w0'], w['conv_w'],
                                            conv_state.reshape(n_seq, 2 * HALF))
    yb_p, kwin, vwin, qg, kt_new, vt_new = _attn(h_all, n_p, tile, w['w0'], _prompt_bias_table(w['rel_bias']),
                                                 w['sinks'])
    bias, sink = _sample_bias(w['rel_bias'], w['sinks'])
    yb_s, new_kt, new_vt = _attn_sample(qg, kt_new, vt_new, cache_k.transpose(0, 2, 3, 1),
                                        cache_v.transpose(0, 2, 3, 1), bias, sink)
    (y_all,) = _out_proj(ya_all, (yb_p, yb_s), (x_p, x_s), w['w_out'], w['g_post'], n_p, tile)
    prompt_state = (conv_p[0], kwin.reshape(N_KV, HEAD_DIM, WINDOW).transpose(2, 0, 1),
                    vwin.reshape(N_KV, HEAD_DIM, WINDOW).transpose(2, 0, 1))
    sample_state = (conv_s, new_kt.transpose(0, 3, 1, 2), new_vt.transpose(0, 3, 1, 2))
    return y_all, prompt_state, sample_state


def _layer_norm(v, g, b):
    xc = v - jnp.mean(v, axis=-1, keepdims=True)
    return xc * lax.rsqrt(jnp.mean(xc * xc, axis=-1, keepdims=True) + NORM_EPS) * g + b


def _cmlp_prompt_kernel(x_ref, g_ref, w_ref, lng_ref, lnb_ref, ws_ref, bs_ref, yc_ref, h_ref, vn_scr, ws_scr, bs_scr):
    tile = x_ref.shape[0]

    @pl.when(pl.program_id(0) == 0)
    def _():
        causal = (lax.broadcasted_iota(jnp.int32, (CHUNK, CHUNK), 0)
                  >= lax.broadcasted_iota(jnp.int32, (CHUNK, CHUNK), 1))
        for grp in range(CMLP_GROUPS):
            ws_scr[grp] = jnp.where(causal, ws_ref[grp], 0.0).astype(BF16)
            bs_scr[grp] = jnp.broadcast_to(bs_ref[grp:grp + 1, :], (bs_scr.shape[2], CHUNK)).T

    h = _rms_bf16(x_ref[...], g_ref[...])
    h_ref[...] = h
    v = _dot_wt(h, w_ref[HALF:2 * HALF, :])
    vn_scr[...] = _layer_norm(v, lng_ref[...], lnb_ref[...]).astype(BF16)
    gw = HALF // CMLP_GROUPS
    cols = 2 * gw
    for cb in range(HALF // cols):
        u = _dot_wt(h, w_ref[cb * cols:(cb + 1) * cols, :])
        gate = _silu(_dot_wt(h, w_ref[2 * HALF + cb * cols:2 * HALF + (cb + 1) * cols, :]))
        for gi in range(2):
            grp = 2 * cb + gi
            lanes = slice(grp * gw, (grp + 1) * gw)
            for n in range(tile // CHUNK):
                rows = slice(n * CHUNK, (n + 1) * CHUNK)
                mixed = _dot(ws_scr[grp], vn_scr[rows, lanes]) + bs_scr[grp]
                yc_ref[rows, lanes] = (u[rows, gi * gw:(gi + 1) * gw] * mixed
                                       * gate[rows, gi * gw:(gi + 1) * gw]).astype(BF16)


def _cmlp_kernel(x_ref, g_ref, w_ref, lng_ref, lnb_ref, ws_ref, bs_ref, coef_ref, bias_ref, yc_ref, h_ref, vns_ref,
                 vn_scr, ws_scr, bs_scr, *, n_p):
    is_sample = pl.program_id(0) == n_p

    @pl.when(jnp.logical_not(is_sample))
    def _():
        _cmlp_prompt_kernel(x_ref, g_ref, w_ref, lng_ref, lnb_ref, ws_ref, bs_ref, yc_ref, h_ref, vn_scr, ws_scr,
                            bs_scr)

    @pl.when(is_sample)
    def _():
        _cmlp_sample_kernel(x_ref, g_ref, w_ref, lng_ref, lnb_ref, coef_ref, bias_ref, yc_ref, vns_ref, h_ref)


def _cmlp(x_all, n_p, tile, w):
    t = x_all.shape[0]
    consts = [w['ln_g'], w['ln_b'], w['w_s'], w['b_s'], w['coef'], w['bias4']]
    gw = HALF // CMLP_GROUPS
    return pl.pallas_call(
        functools.partial(_cmlp_kernel, n_p=n_p),
        grid=(n_p + 1,),
        in_specs=[_rows(tile, D_MODEL), _resident((1, D_MODEL)), _rowwin(3 * HALF, D_MODEL, 0)]
                 + [_resident(c.shape) for c in consts],
        out_specs=[_rows(tile, HALF), _rows(tile, D_MODEL), _resident((tile // DEC_SEQ, DEC_SEQ, HALF))],
        out_shape=[jax.ShapeDtypeStruct((t, HALF), BF16), jax.ShapeDtypeStruct((t, D_MODEL), BF16),
                   jax.ShapeDtypeStruct((tile // DEC_SEQ, DEC_SEQ, HALF), F32)],
        scratch_shapes=[pltpu.VMEM((tile, HALF), BF16), pltpu.VMEM((CMLP_GROUPS, CHUNK, CHUNK), BF16),
                        pltpu.VMEM((CMLP_GROUPS, CHUNK, gw), F32)],
        compiler_params=_params(),
        name="cmlp",
    )(x_all, w['g_pre'], w['w1'], *consts)


def _cmlp_sample_kernel(x_ref, g_ref, w_ref, lng_ref, lnb_ref, coef_ref, bias_ref, yc_ref, vn_ref, h_ref):
    t = x_ref.shape[0]
    h = _rms_bf16(x_ref[...], g_ref[...])
    h_ref[...] = h
    u = _dot_wt(h, w_ref[0:HALF, :])
    vn = _layer_norm(_dot_wt(h, w_ref[HALF:2 * HALF, :]), lng_ref[...], lnb_ref[...])
    gate = _silu(_dot_wt(h, w_ref[2 * HALF:3 * HALF, :]))
    vn_ref[...] = vn.reshape(vn_ref.shape)

    def tiled(a):
        return a.reshape(t // 8, 8, HALF)

    mixed = tiled(vn) * coef_ref[0][None] + bias_ref[...][None]
    for k in range(1, DEC_SEQ):
        mixed = mixed + tiled(pltpu.roll(vn, k, 0)) * coef_ref[k][None]
    yc_ref[...] = (u * mixed.reshape(t, HALF) * gate).astype(BF16)


HEAD_LANES = 128
SSD_GW = HALF // SSD_GROUPS


def _softplus(x):
    return jnp.maximum(x, 0.0) + jnp.log1p(jnp.exp(-jnp.abs(x)))


def _dt_proj(h, wdt_ref):
    pad = jnp.zeros((HEAD_LANES - SSD_HEADS, D_MODEL), F32)
    return _dot_wt(h, jnp.concatenate([wdt_ref[...], pad], axis=0))


def _group_norm_gate(y, z, gn):
    gated = y * _silu(z)
    parts = []
    for g in range(SSD_GROUPS):
        part = gated[:, g * SSD_GW:(g + 1) * SSD_GW]
        parts.append(part * lax.rsqrt(jnp.mean(part * part, axis=-1, keepdims=True) + NORM_EPS))
    return (jnp.concatenate(parts, axis=1) * gn).astype(BF16)


def _ssd_prompt_kernel(h_ref, wz_ref, wx0_ref, wx1_ref, wx2_ref, wdt_ref, cw_ref, cb_ref, dtb_ref, alog_ref,
                       dskip_ref, gn_ref, e3_ref, tril3_ref, yd_ref, tail_ref, ssm_ref,
                       xbc_scr, z_scr, dt_scr, ht_scr, shift_scr):
    tile = h_ref.shape[0]
    i = pl.program_id(0)
    cd = SSD_CONV_DIM
    h = h_ref[...]

    @pl.when(i == 0)
    def _():
        tail_ref[...] = jnp.zeros_like(tail_ref)
        ht_scr[...] = jnp.zeros_like(ht_scr)

    z_scr[...] = _dot_wt(h, wz_ref[...])
    dt_scr[...] = _softplus(_dt_proj(h, wdt_ref) + dtb_ref[...])
    third = cd // 3
    for j, wx_ref in enumerate((wx0_ref, wx1_ref, wx2_ref)):
        cols = slice(j * third, (j + 1) * third)
        raw = _dot_wt(h, wx_ref[...])
        shift_scr[0:8, :] = tail_ref[:, cols]
        shift_scr[8:8 + tile, :] = raw
        conv = raw * cw_ref[3:4, cols] + cb_ref[:, cols]
        for k in range(1, 4):
            conv = conv + shift_scr[8 - k:8 - k + tile, :] * cw_ref[3 - k:4 - k, cols]
        xbc_scr[:, cols] = _silu(conv)
        tail_ref[:, cols] = raw[tile - 8:tile, :]

    a16 = -jnp.exp(alog_ref[...])
    causal =(lax.broadcasted_iota(jnp.int32, (CHUNK, CHUNK), 0)
              >= lax.broadcasted_iota(jnp.int32, (CHUNK, CHUNK), 1))
    first_half = lax.broadcasted_iota(jnp.int32, (CHUNK, 2 * HEAD_DIM), 1) < HEAD_DIM
    keep_a = jnp.where(first_half, 1.0, 0.0).astype(BF16)
    keep_b = jnp.where(first_half, 0.0, 1.0).astype(BF16)

    def chunk(n, carry):
        r0 = pl.multiple_of(n * CHUNK, CHUNK)
        rows = pl.ds(r0, CHUNK)
        xs = xbc_scr[rows, 0:HALF]
        dt16 = dt_scr[rows, :]
        dt_e = _dot(jnp.concatenate(_split3(dt16), axis=1), e3_ref[...])
        acs16 = _dot(tril3_ref[...], jnp.concatenate(_split3(dt16 * a16), axis=0))
        acs_e = _dot(jnp.concatenate(_split3(acs16), axis=1), e3_ref[...])
        acs_t = acs16.T
        last_e = acs_e[CHUNK - 1:CHUNK, :]
        xdt = xs * dt_e
        xdt_bf = xdt.astype(BF16)
        xw = (jnp.exp(last_e - acs_e) * xdt).astype(BF16)
        dec_e = jnp.exp(last_e)
        y_parts = []
        yoff_parts = []
        for g in range(SSD_GROUPS):
            c_g = xbc_scr[rows, HALF + 2 * SSD_STATE + g * SSD_STATE:HALF + 2 * SSD_STATE + (g + 1) * SSD_STATE].astype(BF16)
            b_g = xbc_scr[rows, HALF + g * SSD_STATE:HALF + (g + 1) * SSD_STATE].astype(BF16)
            cb = _dot_nt(c_g, b_g)
            h_prev = ht_scr[g]
            yoff_parts.append(_dot(c_g, h_prev.astype(BF16)))
            for r in range(0, SSD_HEADS // SSD_GROUPS, 2):
                wgt = []
                for hd in (g * (SSD_HEADS // SSD_GROUPS) + r, g * (SSD_HEADS // SSD_GROUPS) + r + 1):
                    seg = acs16[:, hd:hd + 1] - acs_t[hd:hd + 1, :]
                    wgt.append(cb * jnp.exp(jnp.where(causal, seg, -jnp.inf)))
                a = g * (SSD_HEADS // SSD_GROUPS) + r
                slab = xdt_bf[:, a * HEAD_DIM:(a + 2) * HEAD_DIM]
                rhs = jnp.concatenate([slab * keep_a, slab * keep_b], axis=0)
                y_parts.append(_dot(jnp.concatenate(wgt, axis=1).astype(BF16), rhs))
            lanes = slice(g * SSD_GW, (g + 1) * SSD_GW)
            ht_scr[g] = h_prev * dec_e[:, lanes] + _dot_tn(b_g, xw[:, lanes])
        y = (jnp.concatenate(y_parts, axis=1) + jnp.concatenate(yoff_parts, axis=1) * jnp.exp(acs_e)
             + dskip_ref[...] * xs)
        yd_ref[rows, :] = _group_norm_gate(y, z_scr[rows, :], gn_ref[...])
        return carry

    lax.fori_loop(0, tile // CHUNK, chunk, 0, unroll=True)

    @pl.when(i == pl.num_programs(0) - 1)
    def _():
        for g in range(SSD_GROUPS):
            ssm_ref[g * SSD_GW:(g + 1) * SSD_GW, :] = ht_scr[g].T


def _ssd_weight_specs():
    third = SSD_CONV_DIM // 3
    first = 4 * HALF // third
    return ([_rowwin(HALF, D_MODEL, 3)] + [_rowwin(third, D_MODEL, first + j) for j in range(3)]
            + [_rowwin(SSD_HEADS, D_MODEL, (4 * HALF + SSD_CONV_DIM) // SSD_HEADS)])


def _ssd_prompt(h, n_p, tile, w):
    t = n_p * tile
    cd = SSD_CONV_DIM
    consts = [w['conv_w'], w['conv_b'], w['dt_bias16'], w['a_log16'],
              w['d_skip_e'], w['gate_norm_g'], w['expand3'], w['tril3']]
    return pl.pallas_call(
        _ssd_prompt_kernel,
        grid=(t // tile,),
        in_specs=[_rows(tile, D_MODEL)] + _ssd_weight_specs() + [_resident(c.shape) for c in consts],
        out_specs=[_rows(tile, HALF), pl.BlockSpec((8, cd), lambda i: (0, 0)),
                   pl.BlockSpec((HALF, SSD_STATE), lambda i: (0, 0))],
        out_shape=[jax.ShapeDtypeStruct((t, HALF), BF16), jax.ShapeDtypeStruct((8, cd), F32),
                   jax.ShapeDtypeStruct((HALF, SSD_STATE), F32)],
        scratch_shapes=[pltpu.VMEM((tile, cd), F32), pltpu.VMEM((tile, HALF), F32),
                        pltpu.VMEM((tile, HEAD_LANES), F32), pltpu.VMEM((SSD_GROUPS, SSD_STATE, SSD_GW), F32),
                        pltpu.VMEM((8 + tile, cd // 3), F32)],
        compiler_params=_params(),
        name="ssd_prompt",
    )(h, w['w1'], w['w1'], w['w1'], w['w1'], w['w1'], *consts)


def _ssd_sample_pre_kernel(h_ref, wz_ref, wx0_ref, wx1_ref, wx2_ref, wdt_ref, cw_ref, cb_ref, st_ref,
                           dtb_ref, aloge_ref, dskip_ref, e3_ref, seg_ref,
                           nconv_ref, z_ref, ysk_ref, eacs_ref, xw_ref, dec_ref, b_ref, c_ref, raw_scr):
    t = h_ref.shape[0]
    n_seq = t // DEC_SEQ
    h = h_ref[...]
    z_ref[...] = _dot_wt(h, wz_ref[...])
    raw = jnp.concatenate([_dot_wt(h, wx0_ref[...]), _dot_wt(h, wx1_ref[...]), _dot_wt(h, wx2_ref[...])], axis=1)
    for c in range(raw_scr.shape[0]):
        lanes = slice(c * 128, (c + 1) * 128)
        raw_scr[c] = raw[:, lanes]
        for j in range(3):
            nconv_ref[j, :, lanes] = raw_scr[c, pl.ds(j + 1, n_seq, stride=DEC_SEQ), :]
    dt16 = _softplus(_dt_proj(h, wdt_ref) + dtb_ref[...])
    dt = _dot(jnp.concatenate(_split3(dt16), axis=1), e3_ref[...])
    old = [st_ref[j] for j in range(3)]
    p1 = _place_steps(t, [(0, old[2])])
    p2 = _place_steps(t, [(0, old[1]), (1, old[2])])
    p3 = _place_steps(t, [(0, old[0]), (1, old[1]), (2, old[2])])

    def step_of(width):
        return lax.broadcasted_iota(jnp.int32, (t, width), 0) % DEC_SEQ

    def back(a, k):
        return jnp.where(step_of(a.shape[1]) >= k, pltpu.roll(a, k, 0), 0.0)

    def ahead(a, k):
        return jnp.where(step_of(a.shape[1]) + k < DEC_SEQ, pltpu.roll(a, t - k, 0), 0.0)

    conv = (raw * cw_ref[3:4, :] + (back(raw, 1) + p1) * cw_ref[2:3, :]
            + (back(raw, 2) + p2) * cw_ref[1:2, :] + (back(raw, 3) + p3) * cw_ref[0:1, :]
            + cb_ref[...])
    xbc = _silu(conv)
    xs = xbc[:, 0:HALF]
    bm = xbc[:, HALF:HALF + 2 * SSD_STATE]
    cm = xbc[:, HALF + 2 * SSD_STATE:]
    b_ref[...] = bm
    c_ref[...] = cm
    da = dt * (-jnp.exp(aloge_ref[...]))
    acs = da + back(da, 1) + back(da, 2) + back(da, 3)
    suffix = ahead(da, 1) + ahead(da, 2) + ahead(da, 3)
    xdt = xs * dt
    y = _dot((cm * bm).astype(BF16), seg_ref[...]) * xdt
    for k in range(1, DEC_SEQ):
        cbk = _dot((cm * pltpu.roll(bm, k, 0)).astype(BF16), seg_ref[...])
        term = cbk * jnp.exp(acs - pltpu.roll(acs, k, 0)) * pltpu.roll(xdt, k, 0)
        y = y + jnp.where(step_of(HALF) >= k, term, 0.0)
    ysk_ref[...] = y + dskip_ref[...] * xs
    eacs_ref[...] = jnp.exp(acs)
    xw_ref[...] = jnp.exp(suffix) * xdt
    dec_ref[...] = jnp.exp(acs + suffix)


def _ssd_sample_pre(h, first_row, conv_state, w):
    t = conv_state.shape[1] * DEC_SEQ
    cd = SSD_CONV_DIM
    tile = min(SSD_PRE_ROWS, t)
    first = first_row // tile
    seqs = tile // DEC_SEQ
    state_spec = pl.BlockSpec((3, seqs, cd), lambda i: (0, i, 0))
    head = [w['conv_w'], w['conv_b']]
    tail = [w['dt_bias16'], w['a_log_e'], w['d_skip_e'], w['expand3'], w['seg_expand']]
    args = [h, w['w1'], w['w1'], w['w1'], w['w1'], w['w1']] + head + [conv_state] + tail
    wide = jax.ShapeDtypeStruct((t, HALF), F32)
    narrow = jax.ShapeDtypeStruct((t, 2 * SSD_STATE), F32)
    out_shape = [jax.ShapeDtypeStruct(conv_state.shape, F32), wide, wide, wide, wide, wide, narrow, narrow]
    return pl.pallas_call(
        _ssd_sample_pre_kernel,
        grid=(t // tile,),
        in_specs=[pl.BlockSpec((tile, D_MODEL), lambda i: (first + i, 0))] + _ssd_weight_specs()
                 + [_resident(c.shape) for c in head] + [state_spec] + [_resident(c.shape) for c in tail],
        out_specs=[state_spec] + [_rows(tile, HALF)] * 5 + [_rows(tile, 2 * SSD_STATE)] * 2,
        out_shape=out_shape,
        scratch_shapes=[pltpu.VMEM((cd // 128, tile, 128), F32)],
        compiler_params=_params(),
        name="ssd_sample_pre",
    )(*args)


SSD_S_BATCH = 8
SSD_PRE_ROWS = 256


def _ssd_sample_state_kernel(st_ref, c_ref, b_ref, xw_ref, dec_ref, eacs_ref, ysk_ref, z_ref, gn_ref,
                             yd_ref, nst_ref):
    row_n = lax.broadcasted_iota(jnp.int32, (8, SSD_STATE), 0)
    row_w = lax.broadcasted_iota(jnp.int32, (8, SSD_GW), 0)
    row_f = lax.broadcasted_iota(jnp.int32, (8, HALF), 0)
    ones_rows = jnp.where((row_n >= 4) & (row_n < 7), 1.0, 0.0).astype(BF16)
    hpg = SSD_HEADS // SSD_GROUPS

    def pair(p, carry):
        r0 = pl.multiple_of(p * 8, 8)
        rows = pl.ds(r0, 8)
        c8 = c_ref[rows, :].astype(BF16)
        b8 = b_ref[rows, :]
        xw8 = xw_ref[rows, :]
        dec8 = dec_ref[rows, :]
        yoff = []
        for sub in range(2):
            b = 2 * p + sub
            xw_own = xw8 if sub == 0 else pltpu.roll(xw8, 4, 0)
            b_own = b8 if sub == 0 else pltpu.roll(b8, 4, 0)
            hi, mid, lo = (term.astype(F32) for term in _split3(dec8[4 * sub:4 * sub + 1, :]))
            parts = []
            for g in range(SSD_GROUPS):
                lanes = slice(g * SSD_GW, (g + 1) * SSD_GW)
                heads = pl.ds(g * hpg, hpg)
                h0 = st_ref[b, heads].reshape(SSD_GW, SSD_STATE)
                parts.append(_dot_nt(c8[:, g * SSD_STATE:(g + 1) * SSD_STATE], h0.astype(BF16)))
                lhs = jnp.where(row_w < 4, xw_own[:, lanes],
                                jnp.where(row_w == 4, hi[:, lanes],
                                          jnp.where(row_w == 5, mid[:, lanes],
                                                    jnp.where(row_w == 6, lo[:, lanes], 0.0)))).astype(BF16)
                rhs_b = jnp.where(row_n < 4, b_own[:, g * SSD_STATE:(g + 1) * SSD_STATE], 0.0).astype(BF16)
                decay = _dot_tn(lhs, ones_rows)
                nst_ref[b, heads] = (h0 * decay + _dot_tn(lhs, rhs_b)).reshape(hpg, HEAD_DIM, SSD_STATE)
            yoff.append(jnp.concatenate(parts, axis=1))
        yoff8 = jnp.where(row_f < 4, yoff[0], yoff[1])
        y = ysk_ref[rows, :] + yoff8 * eacs_ref[rows, :]
        yd_ref[rows, :] = _group_norm_gate(y, z_ref[rows, :], gn_ref[...])
        return carry

    lax.fori_loop(0, SSD_S_BATCH // 2, pair, 0, unroll=True)


def _ssd_sample_state(state, cm, bm, xw, dec, eacs, ysk, z, gn):
    n_seq = state.shape[0]
    bb = SSD_S_BATCH
    r = bb * DEC_SEQ
    st_spec = pl.BlockSpec((bb, SSD_HEADS, HEAD_DIM, SSD_STATE), lambda i: (i, 0, 0, 0))
    return pl.pallas_call(
        _ssd_sample_state_kernel,
        grid=(n_seq // bb,),
        in_specs=[st_spec, _rows(r, 2 * SSD_STATE), _rows(r, 2 * SSD_STATE)] + [_rows(r, HALF)] * 5
                 + [_resident((1, HALF))],
        out_specs=[_rows(r, HALF), st_spec],
        out_shape=[jax.ShapeDtypeStruct((n_seq * DEC_SEQ, HALF), BF16), jax.ShapeDtypeStruct(state.shape, F32)],
        compiler_params=_params(),
        name="ssd_sample_state",
    )(state, cm, bm, xw, dec, eacs, ysk, z, gn)


def _prep_layer1(g_pre, w_in, ln_g, ln_b, w_s, b_s, conv_w, conv_b, dt_bias, a_log, d_skip, gate_norm_g,
                 w_out, g_post):
    cd = SSD_CONV_DIM
    gw = HALF // CMLP_GROUPS
    w1 = w_in.T

    def lanes16(v):
        return jnp.pad(v.astype(F32), (0, HEAD_LANES - SSD_HEADS)).reshape(1, HEAD_LANES)

    def per_channel(v):
        return jnp.repeat(v.astype(F32), HEAD_DIM).reshape(1, HALF)

    head_of = np.arange(HALF) // HEAD_DIM
    expand = jnp.asarray(np.arange(HEAD_LANES)[:, None] == head_of[None, :], BF16)
    tril = jnp.asarray(np.tril(np.ones((CHUNK, CHUNK))), BF16)
    grp_rows = np.arange(2 * SSD_STATE) // SSD_STATE
    seg_expand = jnp.asarray(grp_rows[:, None] == (head_of // (SSD_HEADS // SSD_GROUPS))[None, :], BF16)

    w4 = jnp.tril(w_s[:, :DEC_SEQ, :DEC_SEQ])
    steps = jnp.arange(DEC_SEQ)
    coef = []
    for k in range(DEC_SEQ):
        src = steps - k
        ck = jnp.where((src >= 0)[None, :], w4[:, steps, jnp.maximum(src, 0)], 0.0)
        ck = jnp.repeat(ck.T, gw, axis=1)
        coef.append(jnp.concatenate([ck, ck], axis=0))
    bias4 = jnp.repeat(b_s[:, :DEC_SEQ].T, gw, axis=1)
    return dict(
        g_pre=g_pre.reshape(1, D_MODEL), w1=w1, ln_g=ln_g.reshape(1, HALF), ln_b=ln_b.reshape(1, HALF),
        w_s=w_s, b_s=b_s,
        coef=jnp.stack(coef).astype(F32), bias4=jnp.concatenate([bias4, bias4], axis=0).astype(F32),
        conv_w=conv_w, conv_b=conv_b.reshape(1, cd), dt_bias16=lanes16(dt_bias), a_log16=lanes16(a_log),
        a_log_e=per_channel(a_log), d_skip_e=per_channel(d_skip), gate_norm_g=gate_norm_g.reshape(1, HALF),
        expand3=jnp.concatenate([expand] * 3, axis=0), tril3=jnp.concatenate([tril] * 3, axis=1),
        seg_expand=seg_expand, w_out=w_out, g_post=g_post.reshape(1, D_MODEL))


def _layer1(x_all, n_p, conv_state, ssm_state, w):
    n_seq = ssm_state.shape[0]
    tile = n_seq * DEC_SEQ
    yc_all, h_all, vn = _cmlp(x_all, n_p, tile, w)
    yd_p, tail, ssm = _ssd_prompt(h_all, n_p, tile, w)
    new_conv, z, ysk, eacs, xw, dec, bm, cm = _ssd_sample_pre(h_all, n_p * tile, conv_state.transpose(1, 0, 2), w)
    yd_s, new_state = _ssd_sample_state(ssm_state, cm, bm, xw, dec, eacs, ysk, z, w['gate_norm_g'])
    y_p, y_s = _out_proj(yc_all, (yd_p, yd_s), x_all, w['w_out'], w['g_post'], n_p, tile,
                         sample_out_shape=(n_seq, DEC_SEQ, D_MODEL))
    prompt_out = (y_p, tail[5:8], ssm.reshape(SSD_HEADS, HEAD_DIM, SSD_STATE))
    sample_out = (y_s, vn, new_conv.transpose(1, 0, 2), new_state)
    return prompt_out, sample_out


def kernel(x_prompt, x_sample, state_conv_a, cache_win_k, cache_win_v, state_conv_d, state_ssm, rel_bias,
           l0_g_pre, l0_w_in, l0_conv_w, l0_sinks, l0_w_out, l0_g_post,
           l1_g_pre, l1_w_in, l1_ln_g, l1_ln_b, l1_w_s, l1_b_s, l1_conv_w, l1_conv_b, l1_dt_bias, l1_a_log,
           l1_d_skip, l1_gate_norm_g, l1_w_out, l1_g_post):
    w0 = _prep_layer0(l0_g_pre, l0_w_in, l0_conv_w, rel_bias, l0_sinks, l0_w_out, l0_g_post)
    w1 = _prep_layer1(l1_g_pre, l1_w_in, l1_ln_g, l1_ln_b, l1_w_s, l1_b_s, l1_conv_w, l1_conv_b, l1_dt_bias,
                      l1_a_log, l1_d_skip, l1_gate_norm_g, l1_w_out, l1_g_post)
    x_p = x_prompt[0]
    n_p = x_p.shape[0] // (x_sample.shape[0] * DEC_SEQ)
    y_all, (p_conv_a, p_win_k, p_win_v), (s_conv_a, s_win_k, s_win_v) = _layer0(
        x_p, x_sample, state_conv_a, cache_win_k, cache_win_v, w0)
    (yp, p_conv_d, p_ssm), (ys, s_chunk_v, s_conv_d, s_ssm) = _layer1(y_all, n_p, state_conv_d, state_ssm, w1)
    return (yp[None], ys, p_conv_a[None], s_conv_a, p_win_k[None], p_win_v[None], s_win_k, s_win_v, s_chunk_v,
            p_conv_d[None], s_conv_d, p_ssm[None], s_ssm)
```

```python
import functools
import math

import jax
import jax.numpy as jnp
import numpy as np
from jax import lax
from jax.experimental import pallas as pl
from jax.experimental.pallas import tpu as pltpu

F32 = jnp.float32
BF16 = jnp.bfloat16

D_MODEL = 2048
HALF = 1024
HEAD_DIM = 64
N_HEADS = 16
N_KV = 4
GROUP = 4
WINDOW = 128
NUM_BUCKETS = 32
MAX_DISTANCE = 128
CMLP_GROUPS = 8
CHUNK = 128
SSD_HEADS = 16
SSD_STATE = 128
SSD_GROUPS = 2
SSD_CONV_DIM = HALF + 2 * SSD_GROUPS * SSD_STATE
DEC_SEQ = 4
NORM_EPS = 1e-6
MASK_VALUE = -1e30

V7X_VMEM_BYTES = 64 * 1024 * 1024
VMEM_LIMIT = V7X_VMEM_BYTES - 8 * 1024 * 1024
CONV_A_VMEM_LIMIT = V7X_VMEM_BYTES - 4 * 1024 * 1024


def _params(n_axes=1):
    return pltpu.CompilerParams(dimension_semantics=("arbitrary",) * n_axes,
                                vmem_limit_bytes=VMEM_LIMIT)


def _resident(shape):
    nd = len(shape)
    return pl.BlockSpec(shape, lambda *_: (0,) * nd, pipeline_mode=pl.Buffered(1))


def _rows(tile, width):
    return pl.BlockSpec((tile, width), lambda i: (i, 0))


def _cols(rows, width, block):
    return pl.BlockSpec((rows, width), lambda *_: (0, block), pipeline_mode=pl.Buffered(1))


def _rowwin(height, cols, block):
    return pl.BlockSpec((height, cols), lambda *_: (block, 0), pipeline_mode=pl.Buffered(1))


def _rms_bf16(x, g):
    ms = jnp.mean(x * x, axis=-1, keepdims=True)
    return (x * lax.rsqrt(ms + NORM_EPS) * g).astype(BF16)


def _silu(x):
    return x * jax.nn.sigmoid(x)


def _dot(a, b):
    return jnp.dot(a, b, preferred_element_type=F32)


def _dot_nt(a, b):
    return lax.dot_general(a, b, (((1,), (1,)), ((), ())), preferred_element_type=F32)


def _dot_tn(a, b):
    return lax.dot_general(a, b, (((0,), (0,)), ((), ())), preferred_element_type=F32)


def _dot_w(a, w):
    return _dot(a, w.astype(BF16))


def _dot_wt(a, wt):
    return _dot_nt(a, wt.astype(BF16))


def _split3(x):
    hi = x.astype(BF16)
    r1 = x - hi.astype(F32)
    mid = r1.astype(BF16)
    lo = (r1 - mid.astype(F32)).astype(BF16)
    return hi, mid, lo


def _place_steps(t, placements):
    n_seq = placements[0][1].shape[0]
    row = lax.broadcasted_iota(jnp.int32, (t, n_seq), 0)
    seq = lax.broadcasted_iota(jnp.int32, (t, n_seq), 1)
    lhs, rhs = [], []
    for step, state in placements:
        sel = jnp.where(row == DEC_SEQ * seq + step, 1.0, 0.0).astype(BF16)
        lhs += [sel] * 3
        rhs += list(_split3(state))
    return _dot(jnp.concatenate(lhs, axis=1), jnp.concatenate(rhs, axis=0))


def _prompt_rows(tile, width, n_p):
    return pl.BlockSpec((tile, width), lambda i: (jnp.minimum(i, n_p - 1), 0))


def _group_specs(arg, tile, width, n_p):
    if isinstance(arg, tuple):
        return [_prompt_rows(tile, width, n_p), _resident(arg[1].shape)]
    return [_rows(tile, width)]


def _out_proj_tile(ya_ref, yb_ref, x_ref, w_ref, g_ref, o_ref):
    y = _dot_w(ya_ref[...], w_ref[0:HALF, :]) + _dot_w(yb_ref[...], w_ref[HALF:2 * HALF, :])
    ms = jnp.mean(y * y, axis=-1, keepdims=True)
    out = x_ref[...].reshape(y.shape) + y * lax.rsqrt(ms + NORM_EPS) * g_ref[...]
    o_ref[...] = out.reshape(o_ref.shape)


def _out_proj_kernel(*refs, n_p, n_ya, n_yb, n_x):
    refs = list(refs)
    ya, yb, x = refs[:n_ya], refs[n_ya:n_ya + n_yb], refs[n_ya + n_yb:n_ya + n_yb + n_x]
    w_ref, g_ref = refs[n_ya + n_yb + n_x:n_ya + n_yb + n_x + 2]
    outs = refs[n_ya + n_yb + n_x + 2:]
    is_sample = pl.program_id(0) == n_p

    @pl.when(jnp.logical_not(is_sample))
    def _():
        _out_proj_tile(ya[0], yb[0], x[0], w_ref, g_ref, outs[0])

    @pl.when(is_sample)
    def _():
        _out_proj_tile(ya[-1], yb[-1], x[-1], w_ref, g_ref, outs[-1])


def _out_proj(ya, yb, x, w, g, n_p, tile, sample_out_shape=None):
    out_pair = sample_out_shape is not None
    groups = [(ya, HALF), (yb, HALF), (x, D_MODEL)]
    in_specs, args = [], []
    for arg, width in groups:
        in_specs += _group_specs(arg, tile, width, n_p)
        args += list(arg) if isinstance(arg, tuple) else [arg]
    if out_pair:
        out_specs = [_prompt_rows(tile, D_MODEL, n_p), _resident(sample_out_shape)]
        out_shape = [jax.ShapeDtypeStruct((n_p * tile, D_MODEL), F32), jax.ShapeDtypeStruct(sample_out_shape, F32)]
    else:
        out_specs = [_rows(tile, D_MODEL)]
        out_shape = [jax.ShapeDtypeStruct(((n_p + 1) * tile, D_MODEL), F32)]
    n_of = [2 if isinstance(arg, tuple) else 1 for arg, _ in groups]
    return pl.pallas_call(
        functools.partial(_out_proj_kernel, n_p=n_p, n_ya=n_of[0], n_yb=n_of[1], n_x=n_of[2]),
        grid=(n_p + 1,),
        in_specs=in_specs + [_resident((2 * HALF, D_MODEL)), _resident((1, D_MODEL))],
        out_specs=out_specs,
        out_shape=out_shape,
        compiler_params=_params(),
        name="out_proj",
    )(*args, w, g)


CONV_A_CHUNK = 256


def _conv_a_weight_copy(w_hbm, w_scr, sems, c, j):
    cols = pl.ds(j * HALF + c * CONV_A_CHUNK, CONV_A_CHUNK)
    return pltpu.make_async_copy(w_hbm.at[:, cols], w_scr.at[:, cols], sems.at[4 * c + j])


def _conv_a_kernel(*refs, sample, before_chunk=None):
    if sample:
        x_ref, g_ref, w_ref, cw_ref, st_ref, ya_ref, s_ref, h_ref = refs
    else:
        x_ref, g_ref, w_ref, cw_ref, ya_ref, new_ref, h_ref, shift_scr, s_ref = refs
    tile = ya_ref.shape[0]
    cc = CONV_A_CHUNK
    h = _rms_bf16(x_ref[...].reshape(tile, D_MODEL), g_ref[...])
    h_ref[...] = h
    if not sample:
        @pl.when(pl.program_id(0) == 0)
        def _():
            s_ref[...] = jnp.zeros_like(s_ref)
    for c in range(HALF // cc):
        lanes = slice(c * cc, (c + 1) * cc)
        if before_chunk is not None:
            before_chunk(c)
        a_b, a_c, a_h, a_g = (_dot_w(h, w_ref[:, j * HALF + c * cc:j * HALF + (c + 1) * cc]) for j in range(4))
        s = a_c * a_h
        if sample:
            t_in = lax.broadcasted_iota(jnp.int32, s.shape, 0) % DEC_SEQ
            old0 = st_ref[:, c * cc:(c + 1) * cc]
            old1 = st_ref[:, HALF + c * cc:HALF + (c + 1) * cc]
            p1 = jnp.where(t_in >= 1, pltpu.roll(s, 1, 0), 0.0) + _place_steps(tile, [(0, old1)])
            p2 = jnp.where(t_in >= 2, pltpu.roll(s, 2, 0), 0.0) + _place_steps(tile, [(0, old0), (1, old1)])
            s_ref[:, :, lanes] = s.reshape(tile // DEC_SEQ, DEC_SEQ, cc)[:, DEC_SEQ - 2:, :]
        else:
            shift_scr[0:8, :] = s_ref[:, lanes]
            shift_scr[8:8 + tile, :] = s
            p1 = shift_scr[7:7 + tile, :]
            p2 = shift_scr[6:6 + tile, :]
            s_ref[:, lanes] = s[tile - 8:tile, :]
            new_ref[0, :, lanes] = s[tile - 2:tile, :]
        conv = p2 * cw_ref[0:1, lanes] + p1 * cw_ref[1:2, lanes] + s * cw_ref[2:3, lanes]
        ya_ref[:, lanes] = (a_b * conv * _silu(a_g)).astype(BF16)


def _conv_a_rowspace_kernel(xp_ref, xs_ref, g_ref, w_hbm, cw_ref, st_ref, ya_ref, new_ref, ss_ref, h_ref, shift_scr,
                            tail_scr, w_ref, sems, *, n_p):
    i = pl.program_id(0)
    n_chunks = HALF // CONV_A_CHUNK

    def pieces(c):
        return [_conv_a_weight_copy(w_hbm, w_ref, sems, c, j) for j in range(4)]

    @pl.when(i == 0)
    def _():
        for c in range(n_chunks):
            for piece in pieces(c):
                piece.start()

        def wait_chunk(c):
            for piece in pieces(c):
                piece.wait()

        _conv_a_kernel(xp_ref, g_ref, w_ref, cw_ref, ya_ref, new_ref, h_ref, shift_scr, tail_scr, sample=False,
                       before_chunk=wait_chunk)

    @pl.when(jnp.logical_and(i > 0, i < n_p))
    def _():
        _conv_a_kernel(xp_ref, g_ref, w_ref, cw_ref, ya_ref, new_ref, h_ref, shift_scr, tail_scr, sample=False)

    @pl.when(i == n_p)
    def _():
        _conv_a_kernel(xs_ref, g_ref, w_ref, cw_ref, st_ref, ya_ref, ss_ref, h_ref, sample=True)


def _conv_a(x_p, x_s, g_pre, w0, conv_w, state):
    tile = x_s.shape[0] * x_s.shape[1]
    n_p = x_p.shape[0] // tile
    t = (n_p + 1) * tile
    return pl.pallas_call(
        functools.partial(_conv_a_rowspace_kernel, n_p=n_p),
        grid=(n_p + 1,),
        in_specs=[_prompt_rows(tile, D_MODEL, n_p), _resident(x_s.shape), _resident((1, D_MODEL)),
                  pl.BlockSpec(memory_space=pl.ANY), _resident((3, HALF)), _resident(state.shape)],
        out_specs=[_rows(tile, HALF), _resident((1, 2, HALF)), _resident((x_s.shape[0], 2, HALF)),
                   _rows(tile, D_MODEL)],
        out_shape=[jax.ShapeDtypeStruct((t, HALF), BF16), jax.ShapeDtypeStruct((1, 2, HALF), F32),
                   jax.ShapeDtypeStruct((x_s.shape[0], 2, HALF), F32), jax.ShapeDtypeStruct((t, D_MODEL), BF16)],
        scratch_shapes=[pltpu.VMEM((8 + tile, CONV_A_CHUNK), F32), pltpu.VMEM((8, HALF), F32),
                        pltpu.VMEM((D_MODEL, 4 * HALF), F32), pltpu.SemaphoreType.DMA((4 * HALF // CONV_A_CHUNK,))],
        compiler_params=pltpu.CompilerParams(dimension_semantics=("arbitrary",), vmem_limit_bytes=CONV_A_VMEM_LIMIT),
        name="conv_a",
    )(x_p, x_s, g_pre, w0, conv_w, state)


def _rel_bucket(dist):
    max_exact = NUM_BUCKETS // 2
    d = jnp.maximum(dist, 0)
    ratio = jnp.maximum(d, max_exact).astype(F32) / max_exact
    large = max_exact + (jnp.log(ratio) / math.log(MAX_DISTANCE / max_exact)
                         * (NUM_BUCKETS - max_exact)).astype(jnp.int32)
    return jnp.where(d < max_exact, d, jnp.minimum(large, NUM_BUCKETS - 1))


def _attn_softmax_pv(s, sink, v_bf, v_transposed=False):
    m = jnp.maximum(jnp.max(s, axis=-1, keepdims=True), sink)
    p = jnp.exp(s - m)
    den = jnp.sum(p, axis=-1, keepdims=True) + jnp.exp(sink - m)
    pv = _dot_nt(p.astype(BF16), v_bf) if v_transposed else _dot(p.astype(BF16), v_bf)
    return pv / den


def _attn_prompt_kernel(h_ref, wq_ref, wkv_ref, wg0_ref, wg1_ref, tab_ref, sink_ref, yb_ref,
                        q_scr, gate_scr, k_scr, v_scr, bias_scr, kv_scr):
    tile = h_ref.shape[0]
    i = pl.program_id(0)
    kv_w = N_KV * HEAD_DIM
    h = h_ref[...]

    kw, vw = 2 * HEAD_DIM, 4 * HEAD_DIM

    @pl.when(i == 0)
    def _():
        k_scr[0:WINDOW, :] = jnp.zeros((WINDOW, N_KV * kw), BF16)
        v_scr[0:WINDOW, :] = jnp.zeros((WINDOW, N_KV * vw), BF16)
        for hk in range(N_KV):
            v_scr[:, hk * vw + kw:(hk + 1) * vw] = jnp.ones((tile + WINDOW, kw), BF16)
        in_own = lax.broadcasted_iota(jnp.int32, (WINDOW, 2 * WINDOW), 1) >= WINDOW
        for head in range(N_HEADS):
            row = jnp.broadcast_to(tab_ref[head:head + 1, :], (WINDOW, BIAS_SPAN))
            band = pltpu.roll(row, 0, 1, stride=1, stride_axis=0)[:, 0:2 * WINDOW]
            rows = slice((head % 2) * WINDOW, (head % 2 + 1) * WINDOW)
            bias_scr[1, head // 2, rows, :] = band
            bias_scr[0, head // 2, rows, :] = jnp.where(in_own, band, MASK_VALUE)

    q_scr[...] = (_dot_w(h, wq_ref[...]) * (HEAD_DIM ** -0.5)).astype(BF16)
    k = _dot_w(h, wkv_ref[:, 0:kv_w])
    v = _dot_w(h, wkv_ref[:, kv_w:2 * kv_w])
    gate_scr[:, 0:HALF // 2] = _silu(_dot_w(h, wg0_ref[...]))
    gate_scr[:, HALF // 2:HALF] = _silu(_dot_w(h, wg1_ref[...]))
    for hk in range(N_KV):
        k_h = k[:, hk * HEAD_DIM:(hk + 1) * HEAD_DIM].astype(BF16)
        v_h = v[:, hk * HEAD_DIM:(hk + 1) * HEAD_DIM].astype(BF16)
        k_scr[WINDOW:WINDOW + tile, hk * kw:(hk + 1) * kw] = jnp.concatenate([k_h, k_h], axis=1)
        v_scr[WINDOW:WINDOW + tile, hk * vw:hk * vw + kw] = jnp.concatenate([v_h, v_h], axis=1)
    kv_scr[0:WINDOW, 0:kv_w] = k[tile - WINDOW:tile, :]
    kv_scr[0:WINDOW, kv_w:2 * kv_w] = v[tile - WINDOW:tile, :]

    lane = lax.broadcasted_iota(jnp.int32, (WINDOW, kw), 1)
    lo = lane < HEAD_DIM
    keep_a = jnp.where(lo, 1.0, 0.0).astype(BF16)
    keep_b = jnp.where(lo, 0.0, 1.0).astype(BF16)
    is_a = lax.broadcasted_iota(jnp.int32, (2 * WINDOW, 1), 0) < WINDOW

    def block(n, carry):
        r0 = pl.multiple_of(n * WINDOW, WINDOW)
        rows = pl.ds(r0, WINDOW)
        keys = pl.ds(r0, 2 * WINDOW)
        first = jnp.where(jnp.logical_and(i == 0, n == 0), 0, 1)
        for hk in range(N_KV):
            for gp in range(GROUP // 2):
                a = hk * GROUP + 2 * gp
                slab = slice(a * HEAD_DIM, (a + 2) * HEAD_DIM)
                q2 = q_scr[rows, slab]
                lhs = jnp.concatenate([q2 * keep_a, q2 * keep_b], axis=0)
                s = _dot_nt(lhs, k_scr[keys, hk * kw:(hk + 1) * kw]) + bias_scr[first, a // 2]
                sink = jnp.where(is_a, sink_ref[a], sink_ref[a + 1])
                m = jnp.maximum(jnp.max(s, axis=-1, keepdims=True), sink)
                p = jnp.exp(s - m).astype(BF16)
                pv = _dot(p, v_scr[keys, hk * vw:(hk + 1) * vw])
                num = jnp.where(lo, pv[0:WINDOW, 0:kw], pv[WINDOW:2 * WINDOW, 0:kw])
                den = jnp.where(lo, pv[0:WINDOW, kw:2 * kw], pv[WINDOW:2 * WINDOW, kw:2 * kw])
                m_slab = jnp.where(lo, m[0:WINDOW], m[WINDOW:2 * WINDOW])
                den = den + jnp.exp(jnp.where(lo, sink_ref[a], sink_ref[a + 1]) - m_slab)
                yb_ref[rows, slab] = (num / den * gate_scr[rows, slab]).astype(BF16)
        return carry

    lax.fori_loop(0, tile // WINDOW, block, 0, unroll=True)
    k_scr[0:WINDOW, :] = k_scr[tile:tile + WINDOW, :]
    v_scr[0:WINDOW, :] = v_scr[tile:tile + WINDOW, :]


BIAS_SPAN = 3 * WINDOW


def _prompt_bias_table(rel_bias):
    dist = WINDOW - jnp.arange(BIAS_SPAN)
    table = jnp.where(((dist >= 0) & (dist < WINDOW))[:, None], rel_bias.astype(F32)[_rel_bucket(dist)], MASK_VALUE)
    return table.T


def _attn_proj_kernel(h_ref, wq_ref, wkv_ref, wg0_ref, wg1_ref, kwin_ref, vwin_ref, qg_ref, kt_ref, vt_ref, kv_scr):
    kv_w = N_KV * HEAD_DIM
    kwin_ref[...] = kv_scr[0:WINDOW, 0:kv_w].T
    vwin_ref[...] = kv_scr[0:WINDOW, kv_w:2 * kv_w].T
    h = h_ref[...]
    q = _dot_w(h, wq_ref[...]) * (HEAD_DIM ** -0.5)
    for hk in range(N_KV):
        for g in range(GROUP):
            src = (hk * GROUP + g) * HEAD_DIM
            dst = (g * N_KV + hk) * HEAD_DIM
            qg_ref[:, dst:dst + HEAD_DIM] = q[:, src:src + HEAD_DIM]
    qg_ref[:, HALF:HALF + HALF // 2] = _dot_w(h, wg0_ref[...])
    qg_ref[:, HALF + HALF // 2:2 * HALF] = _dot_w(h, wg1_ref[...])
    kv_scr[...] = _dot_w(h, wkv_ref[...])
    for j in range(kt_ref.shape[0]):
        kt_ref[j] = kv_scr[j * WINDOW:(j + 1) * WINDOW, 0:kv_w].T
        vt_ref[j] = kv_scr[j * WINDOW:(j + 1) * WINDOW, kv_w:2 * kv_w].T


def _attn_rowspace_kernel(h_ref, wq_ref, wkv_ref, wg0_ref, wg1_ref, tab_ref, sink_ref,
                          yb_ref, kwin_ref, vwin_ref, qg_ref, kt_ref, vt_ref,
                          q_scr, gate_scr, k_scr, v_scr, bias_scr, kv_scr, *, n_p):
    is_sample = pl.program_id(0) == n_p

    @pl.when(jnp.logical_not(is_sample))
    def _():
        _attn_prompt_kernel(h_ref, wq_ref, wkv_ref, wg0_ref, wg1_ref, tab_ref, sink_ref, yb_ref,
                            q_scr, gate_scr, k_scr, v_scr, bias_scr, kv_scr)

    @pl.when(is_sample)
    def _():
        _attn_proj_kernel(h_ref, wq_ref, wkv_ref, wg0_ref, wg1_ref, kwin_ref, vwin_ref, qg_ref, kt_ref, vt_ref,
                          kv_scr)


def _attn(h_all, n_p, tile, w0, table, sinks):
    kv_w = N_KV * HEAD_DIM
    t = n_p * tile
    win_spec = _resident((kv_w, WINDOW))
    out_shape = [jax.ShapeDtypeStruct((t, HALF), BF16),
                 jax.ShapeDtypeStruct((kv_w, WINDOW), F32), jax.ShapeDtypeStruct((kv_w, WINDOW), F32),
                 jax.ShapeDtypeStruct((tile, 2 * HALF), F32), jax.ShapeDtypeStruct((tile // WINDOW, kv_w, WINDOW), F32),
                 jax.ShapeDtypeStruct((tile // WINDOW, kv_w, WINDOW), F32)]
    return pl.pallas_call(
        functools.partial(_attn_rowspace_kernel, n_p=n_p),
        grid=(n_p + 1,),
        in_specs=[_rows(tile, D_MODEL),
                  _cols(D_MODEL, HALF, 4), _cols(D_MODEL, 2 * kv_w, 10),
                  _cols(D_MODEL, HALF // 2, 11), _cols(D_MODEL, HALF // 2, 12),
                  _resident(table.shape), pl.BlockSpec(memory_space=pltpu.SMEM)],
        out_specs=[_prompt_rows(tile, HALF, n_p), win_spec, win_spec] + [_resident(s.shape) for s in out_shape[3:]],
        out_shape=out_shape,
        scratch_shapes=[pltpu.VMEM((tile, HALF), BF16), pltpu.VMEM((tile, HALF), F32),
                        pltpu.VMEM((tile + WINDOW, 2 * kv_w), BF16), pltpu.VMEM((tile + WINDOW, 4 * kv_w), BF16),
                        pltpu.VMEM((2, N_HEADS // 2, 2 * WINDOW, 2 * WINDOW), F32),
                        pltpu.VMEM((tile, 2 * kv_w), F32)],
        compiler_params=_params(),
        name="attn",
    )(h_all, w0, w0, w0, w0, table, sinks)


ATTN_S_BATCH = 16
KEYS_PAD = 2 * WINDOW


def _attn_sample_kernel(qg_ref, ktn_ref, vtn_ref, ck_ref, cv_ref, bias_ref, sink_ref, yb_ref, nk_ref, nv_ref):
    kv_w = N_KV * HEAD_DIM
    row8 = lax.broadcasted_iota(jnp.int32, (8, kv_w), 0)
    lane_head = lax.broadcasted_iota(jnp.int32, (8, kv_w), 1) // HEAD_DIM
    lower = row8 < DEC_SEQ
    pick = [jnp.where(lane_head == 2 * hp + jnp.where(lower, 0, 1), 1.0, 0.0).astype(F32) for hp in range(2)]
    lower_w = lax.broadcasted_iota(jnp.int32, (8, HALF), 0) < DEC_SEQ
    kept = lax.broadcasted_iota(jnp.int32, (kv_w, WINDOW), 1) < WINDOW - DEC_SEQ
    seq0 = pl.program_id(0) * ATTN_S_BATCH
    per_tile = WINDOW // DEC_SEQ

    def slide(old, new_tile, shift):
        return jnp.where(kept, pltpu.roll(old, WINDOW - DEC_SEQ, 1), pltpu.roll(new_tile, shift, 1))

    def pair(p, carry):
        r0 = pl.multiple_of(p * 8, 8)
        rows = qg_ref[pl.ds(r0, 8), :]
        q8 = rows[:, 0:HALF]
        gate8 = rows[:, HALF:2 * HALF]
        out8 = []
        for sub in range(2):
            b = 2 * p + sub
            q_swap = pltpu.roll(q8, 4, 0)
            q_dup = jnp.where(lower_w, q8, q_swap) if sub == 0 else jnp.where(lower_w, q_swap, q8)
            tile = (seq0 + b) // per_tile
            shift = (2 * WINDOW - DEC_SEQ - DEC_SEQ * ((seq0 + b) % per_tile)) % WINDOW
            k_old = ck_ref[b].reshape(kv_w, WINDOW)
            v_old = cv_ref[b].reshape(kv_w, WINDOW)
            k_win = slide(k_old, ktn_ref[tile], shift)
            v_win = slide(v_old, vtn_ref[tile], shift)
            nk_ref[b] = k_win.reshape(N_KV, HEAD_DIM, WINDOW)
            nv_ref[b] = v_win.reshape(N_KV, HEAD_DIM, WINDOW)
            k_all = jnp.concatenate([k_old, k_win], axis=1).astype(BF16)
            v_all = jnp.concatenate([v_old, v_win], axis=1).astype(BF16)
            q_bd = jnp.concatenate(
                [q_dup[:, g * kv_w:(g + 1) * kv_w] * pick[hp] for g in range(GROUP) for hp in range(2)], axis=0)
            s = _dot(q_bd.astype(BF16), k_all) + bias_ref[...]
            o = _attn_softmax_pv(s, sink_ref[:, 0:1], v_all, v_transposed=True)
            out_g = []
            for g in range(GROUP):
                acc = None
                for hp in range(2):
                    piece = o[(2 * g + hp) * 8:(2 * g + hp + 1) * 8, :] * pick[hp]
                    piece = piece + pltpu.roll(piece, 4, 0)
                    acc = piece if acc is None else acc + piece
                out_g.append(acc)
            out8.append(jnp.concatenate(
                [out_g[g][:, hk * HEAD_DIM:(hk + 1) * HEAD_DIM] for hk in range(N_KV) for g in range(GROUP)], axis=1))
        o8 = jnp.where(lower_w, out8[0], out8[1])
        yb_ref[pl.ds(r0, 8), :] = (o8 * _silu(gate8)).astype(BF16)
        return carry

    lax.fori_loop(0, ATTN_S_BATCH // 2, pair, 0, unroll=2)


def _sample_bias(rel_bias, sinks):
    t = jnp.arange(DEC_SEQ)[:, None]
    j = jnp.arange(KEYS_PAD)[None, :]
    pos = jnp.where(j < WINDOW, j, j - (KEYS_PAD - DEC_SEQ) + WINDOW)
    dist = t + WINDOW - pos
    valid = (dist >= 0) & (dist < WINDOW) & ((j < WINDOW) | (j >= KEYS_PAD - DEC_SEQ))
    bias = jnp.where(valid[:, :, None], rel_bias.astype(F32)[_rel_bucket(dist)], MASK_VALUE)
    bias = bias.reshape(DEC_SEQ, KEYS_PAD, N_KV, GROUP).transpose(3, 2, 0, 1).reshape(N_HEADS * DEC_SEQ, KEYS_PAD)
    sink = jnp.broadcast_to(sinks.astype(F32).reshape(N_KV, GROUP).T[:, :, None], (GROUP, N_KV, DEC_SEQ))
    return bias, jnp.broadcast_to(sink.reshape(N_HEADS * DEC_SEQ, 1), (N_HEADS * DEC_SEQ, 128))


def _attn_sample(qg, kt_new, vt_new, cache_kt, cache_vt, bias, sink):
    n_seq = cache_kt.shape[0]
    bb = ATTN_S_BATCH
    cache_spec = pl.BlockSpec((bb, N_KV, HEAD_DIM, WINDOW), lambda i: (i, 0, 0, 0))
    return pl.pallas_call(
        _attn_sample_kernel,
        grid=(n_seq // bb,),
        in_specs=[_rows(bb * DEC_SEQ, 2 * HALF), _resident(kt_new.shape), _resident(vt_new.shape),
                  cache_spec, cache_spec, _resident(bias.shape), _resident(sink.shape)],
        out_specs=[_rows(bb * DEC_SEQ, HALF), cache_spec, cache_spec],
        out_shape=[jax.ShapeDtypeStruct((n_seq * DEC_SEQ, HALF), BF16),
                   jax.ShapeDtypeStruct(cache_kt.shape, F32), jax.ShapeDtypeStruct(cache_vt.shape, F32)],
        compiler_params=_params(),
        name="attn_sample",
    )(qg, kt_new, vt_new, cache_kt, cache_vt, bias, sink)


def _prep_layer0(g_pre, w_in, conv_w, rel_bias, sinks, w_out, g_post):
    return dict(
        g_pre=g_pre.reshape(1, D_MODEL), w0=w_in, conv_w=conv_w, rel_bias=rel_bias, sinks=sinks,
        w_out=w_out, g_post=g_post.reshape(1, D_MODEL))


def _layer0(x_p, x_s, conv_state, cache_k, cache_v, w):
    n_seq = x_s.shape[0]
    tile = n_seq * DEC_SEQ
    n_p = x_p.shape[0] // tile
    ya_all, conv_p, conv_s, h_all = _conv_a(x_p, x_s, w['g_pre'], w['w0'], w['conv_w'],
                                            conv_state.reshape(n_seq, 2 * HALF))
    yb_p, kwin, vwin, qg, kt_new, vt_new = _attn(h_all, n_p, tile, w['w0'], _prompt_bias_table(w['rel_bias']),
                                                 w['sinks'])
    bias, sink = _sample_bias(w['rel_bias'], w['sinks'])
    yb_s, new_kt, new_vt = _attn_sample(qg, kt_new, vt_new, cache_k.transpose(0, 2, 3, 1),
                                        cache_v.transpose(0, 2, 3, 1), bias, sink)
    (y_all,) = _out_proj(ya_all, (yb_p, yb_s), (x_p, x_s), w['w_out'], w['g_post'], n_p, tile)
    prompt_state = (conv_p[0], kwin.reshape(N_KV, HEAD_DIM, WINDOW).transpose(2, 0, 1),
                    vwin.reshape(N_KV, HEAD_DIM, WINDOW).transpose(2, 0, 1))
    sample_state = (conv_s, new_kt.transpose(0, 3, 1, 2), new_vt.transpose(0, 3, 1, 2))
    return y_all, prompt_state, sample_state


def _layer_norm(v, g, b):
    xc = v - jnp.mean(v, axis=-1, keepdims=True)
    return xc * lax.rsqrt(jnp.mean(xc * xc, axis=-1, keepdims=True) + NORM_EPS) * g + b


def _cmlp_prompt_kernel(x_ref, g_ref, w_ref, lng_ref, lnb_ref, ws_ref, bs_ref, yc_ref, h_ref, vn_scr, ws_scr, bs_scr):
    tile = x_ref.shape[0]

    @pl.when(pl.program_id(0) == 0)
    def _():
        causal = (lax.broadcasted_iota(jnp.int32, (CHUNK, CHUNK), 0)
                  >= lax.broadcasted_iota(jnp.int32, (CHUNK, CHUNK), 1))
        for grp in range(CMLP_GROUPS):
            ws_scr[grp] = jnp.where(causal, ws_ref[grp], 0.0).astype(BF16)
            bs_scr[grp] = jnp.broadcast_to(bs_ref[grp:grp + 1, :], (bs_scr.shape[2], CHUNK)).T

    h = _rms_bf16(x_ref[...], g_ref[...])
    h_ref[...] = h
    v = _dot_wt(h, w_ref[HALF:2 * HALF, :])
    vn_scr[...] = _layer_norm(v, lng_ref[...], lnb_ref[...]).astype(BF16)
    gw = HALF // CMLP_GROUPS
    cols = 2 * gw
    for cb in range(HALF // cols):
        u = _dot_wt(h, w_ref[cb * cols:(cb + 1) * cols, :])
        gate = _silu(_dot_wt(h, w_ref[2 * HALF + cb * cols:2 * HALF + (cb + 1) * cols, :]))
        for gi in range(2):
            grp = 2 * cb + gi
            lanes = slice(grp * gw, (grp + 1) * gw)
            for n in range(tile // CHUNK):
                rows = slice(n * CHUNK, (n + 1) * CHUNK)
                mixed = _dot(ws_scr[grp], vn_scr[rows, lanes]) + bs_scr[grp]
                yc_ref[rows, lanes] = (u[rows, gi * gw:(gi + 1) * gw] * mixed
                                       * gate[rows, gi * gw:(gi + 1) * gw]).astype(BF16)


def _cmlp_kernel(x_ref, g_ref, w_ref, lng_ref, lnb_ref, ws_ref, bs_ref, coef_ref, bias_ref, yc_ref, h_ref, vns_ref,
                 vn_scr, ws_scr, bs_scr, *, n_p):
    is_sample = pl.program_id(0) == n_p

    @pl.when(jnp.logical_not(is_sample))
    def _():
        _cmlp_prompt_kernel(x_ref, g_ref, w_ref, lng_ref, lnb_ref, ws_ref, bs_ref, yc_ref, h_ref, vn_scr, ws_scr,
                            bs_scr)

    @pl.when(is_sample)
    def _():
        _cmlp_sample_kernel(x_ref, g_ref, w_ref, lng_ref, lnb_ref, coef_ref, bias_ref, yc_ref, vns_ref, h_ref)


def _cmlp(x_all, n_p, tile, w):
    t = x_all.shape[0]
    consts = [w['ln_g'], w['ln_b'], w['w_s'], w['b_s'], w['coef'], w['bias4']]
    gw = HALF // CMLP_GROUPS
    return pl.pallas_call(
        functools.partial(_cmlp_kernel, n_p=n_p),
        grid=(n_p + 1,),
        in_specs=[_rows(tile, D_MODEL), _resident((1, D_MODEL)), _rowwin(3 * HALF, D_MODEL, 0)]
                 + [_resident(c.shape) for c in consts],
        out_specs=[_rows(tile, HALF), _rows(tile, D_MODEL), _resident((tile // DEC_SEQ, DEC_SEQ, HALF))],
        out_shape=[jax.ShapeDtypeStruct((t, HALF), BF16), jax.ShapeDtypeStruct((t, D_MODEL), BF16),
                   jax.ShapeDtypeStruct((tile // DEC_SEQ, DEC_SEQ, HALF), F32)],
        scratch_shapes=[pltpu.VMEM((tile, HALF), BF16), pltpu.VMEM((CMLP_GROUPS, CHUNK, CHUNK), BF16),
                        pltpu.VMEM((CMLP_GROUPS, CHUNK, gw), F32)],
        compiler_params=_params(),
        name="cmlp",
    )(x_all, w['g_pre'], w['w1'], *consts)


def _cmlp_sample_kernel(x_ref, g_ref, w_ref, lng_ref, lnb_ref, coef_ref, bias_ref, yc_ref, vn_ref, h_ref):
    t = x_ref.shape[0]
    h = _rms_bf16(x_ref[...], g_ref[...])
    h_ref[...] = h
    u = _dot_wt(h, w_ref[0:HALF, :])
    vn = _layer_norm(_dot_wt(h, w_ref[HALF:2 * HALF, :]), lng_ref[...], lnb_ref[...])
    gate = _silu(_dot_wt(h, w_ref[2 * HALF:3 * HALF, :]))
    vn_ref[...] = vn.reshape(vn_ref.shape)

    def tiled(a):
        return a.reshape(t // 8, 8, HALF)

    mixed = tiled(vn) * coef_ref[0][None] + bias_ref[...][None]
    for k in range(1, DEC_SEQ):
        mixed = mixed + tiled(pltpu.roll(vn, k, 0)) * coef_ref[k][None]
    yc_ref[...] = (u * mixed.reshape(t, HALF) * gate).astype(BF16)


HEAD_LANES = 128
SSD_GW = HALF // SSD_GROUPS


def _softplus(x):
    return jnp.maximum(x, 0.0) + jnp.log1p(jnp.exp(-jnp.abs(x)))


def _dt_proj(h, wdt_ref):
    pad = jnp.zeros((HEAD_LANES - SSD_HEADS, D_MODEL), F32)
    return _dot_wt(h, jnp.concatenate([wdt_ref[...], pad], axis=0))


def _group_norm_gate(y, z, gn):
    gated = y * _silu(z)
    parts = []
    for g in range(SSD_GROUPS):
        part = gated[:, g * SSD_GW:(g + 1) * SSD_GW]
        parts.append(part * lax.rsqrt(jnp.mean(part * part, axis=-1, keepdims=True) + NORM_EPS))
    return (jnp.concatenate(parts, axis=1) * gn).astype(BF16)


def _ssd_prompt_kernel(h_ref, wz_ref, wx0_ref, wx1_ref, wx2_ref, wdt_ref, cw_ref, cb_ref, dtb_ref, alog_ref,
                       dskip_ref, gn_ref, e3_ref, tril3_ref, yd_ref, tail_ref, ssm_ref,
                       xbc_scr, z_scr, dt_scr, ht_scr, shift_scr):
    tile = h_ref.shape[0]
    i = pl.program_id(0)
    cd = SSD_CONV_DIM
    h = h_ref[...]

    @pl.when(i == 0)
    def _():
        tail_ref[...] = jnp.zeros_like(tail_ref)
        ht_scr[...] = jnp.zeros_like(ht_scr)

    z_scr[...] = _dot_wt(h, wz_ref[...])
    dt_scr[...] = _softplus(_dt_proj(h, wdt_ref) + dtb_ref[...])
    third = cd // 3
    for j, wx_ref in enumerate((wx0_ref, wx1_ref, wx2_ref)):
        cols = slice(j * third, (j + 1) * third)
        raw = _dot_wt(h, wx_ref[...])
        shift_scr[0:8, :] = tail_ref[:, cols]
        shift_scr[8:8 + tile, :] = raw
        conv = raw * cw_ref[3:4, cols] + cb_ref[:, cols]
        for k in range(1, 4):
            conv = conv + shift_scr[8 - k:8 - k + tile, :] * cw_ref[3 - k:4 - k, cols]
        xbc_scr[:, cols] = _silu(conv)
        tail_ref[:, cols] = raw[tile - 8:tile, :]

    a16 = -jnp.exp(alog_ref[...])
    causal =(lax.broadcasted_iota(jnp.int32, (CHUNK, CHUNK), 0)
              >= lax.broadcasted_iota(jnp.int32, (CHUNK, CHUNK), 1))
    first_half = lax.broadcasted_iota(jnp.int32, (CHUNK, 2 * HEAD_DIM), 1) < HEAD_DIM
    keep_a = jnp.where(first_half, 1.0, 0.0).astype(BF16)
    keep_b = jnp.where(first_half, 0.0, 1.0).astype(BF16)

    def chunk(n, carry):
        r0 = pl.multiple_of(n * CHUNK, CHUNK)
        rows = pl.ds(r0, CHUNK)
        xs = xbc_scr[rows, 0:HALF]
        dt16 = dt_scr[rows, :]
        dt_e = _dot(jnp.concatenate(_split3(dt16), axis=1), e3_ref[...])
        acs16 = _dot(tril3_ref[...], jnp.concatenate(_split3(dt16 * a16), axis=0))
        acs_e = _dot(jnp.concatenate(_split3(acs16), axis=1), e3_ref[...])
        acs_t = acs16.T
        last_e = acs_e[CHUNK - 1:CHUNK, :]
        xdt = xs * dt_e
        xdt_bf = xdt.astype(BF16)
        xw = (jnp.exp(last_e - acs_e) * xdt).astype(BF16)
        dec_e = jnp.exp(last_e)
        y_parts = []
        yoff_parts = []
        for g in range(SSD_GROUPS):
            c_g = xbc_scr[rows, HALF + 2 * SSD_STATE + g * SSD_STATE:HALF + 2 * SSD_STATE + (g + 1) * SSD_STATE].astype(BF16)
            b_g = xbc_scr[rows, HALF + g * SSD_STATE:HALF + (g + 1) * SSD_STATE].astype(BF16)
            cb = _dot_nt(c_g, b_g)
            h_prev = ht_scr[g]
            yoff_parts.append(_dot(c_g, h_prev.astype(BF16)))
            for r in range(0, SSD_HEADS // SSD_GROUPS, 2):
                wgt = []
                for hd in (g * (SSD_HEADS // SSD_GROUPS) + r, g * (SSD_HEADS // SSD_GROUPS) + r + 1):
                    seg = acs16[:, hd:hd + 1] - acs_t[hd:hd + 1, :]
                    wgt.append(cb * jnp.exp(jnp.where(causal, seg, -jnp.inf)))
                a = g * (SSD_HEADS // SSD_GROUPS) + r
                slab = xdt_bf[:, a * HEAD_DIM:(a + 2) * HEAD_DIM]
                rhs = jnp.concatenate([slab * keep_a, slab * keep_b], axis=0)
                y_parts.append(_dot(jnp.concatenate(wgt, axis=1).astype(BF16), rhs))
            lanes = slice(g * SSD_GW, (g + 1) * SSD_GW)
            ht_scr[g] = h_prev * dec_e[:, lanes] + _dot_tn(b_g, xw[:, lanes])
        y = (jnp.concatenate(y_parts, axis=1) + jnp.concatenate(yoff_parts, axis=1) * jnp.exp(acs_e)
             + dskip_ref[...] * xs)
        yd_ref[rows, :] = _group_norm_gate(y, z_scr[rows, :], gn_ref[...])
        return carry

    lax.fori_loop(0, tile // CHUNK, chunk, 0, unroll=True)

    @pl.when(i == pl.num_programs(0) - 1)
    def _():
        for g in range(SSD_GROUPS):
            ssm_ref[g * SSD_GW:(g + 1) * SSD_GW, :] = ht_scr[g].T


def _ssd_weight_specs():
    third = SSD_CONV_DIM // 3
    first = 4 * HALF // third
    return ([_rowwin(HALF, D_MODEL, 3)] + [_rowwin(third, D_MODEL, first + j) for j in range(3)]
            + [_rowwin(SSD_HEADS, D_MODEL, (4 * HALF + SSD_CONV_DIM) // SSD_HEADS)])


def _ssd_prompt(h, n_p, tile, w):
    t = n_p * tile
    cd = SSD_CONV_DIM
    consts = [w['conv_w'], w['conv_b'], w['dt_bias16'], w['a_log16'],
              w['d_skip_e'], w['gate_norm_g'], w['expand3'], w['tril3']]
    return pl.pallas_call(
        _ssd_prompt_kernel,
        grid=(t // tile,),
        in_specs=[_rows(tile, D_MODEL)] + _ssd_weight_specs() + [_resident(c.shape) for c in consts],
        out_specs=[_rows(tile, HALF), pl.BlockSpec((8, cd), lambda i: (0, 0)),
                   pl.BlockSpec((HALF, SSD_STATE), lambda i: (0, 0))],
        out_shape=[jax.ShapeDtypeStruct((t, HALF), BF16), jax.ShapeDtypeStruct((8, cd), F32),
                   jax.ShapeDtypeStruct((HALF, SSD_STATE), F32)],
        scratch_shapes=[pltpu.VMEM((tile, cd), F32), pltpu.VMEM((tile, HALF), F32),
                        pltpu.VMEM((tile, HEAD_LANES), F32), pltpu.VMEM((SSD_GROUPS, SSD_STATE, SSD_GW), F32),
                        pltpu.VMEM((8 + tile, cd // 3), F32)],
        compiler_params=_params(),
        name="ssd_prompt",
    )(h, w['w1'], w['w1'], w['w1'], w['w1'], w['w1'], *consts)


def _ssd_sample_pre_kernel(h_ref, wz_ref, wx0_ref, wx1_ref, wx2_ref, wdt_ref, cw_ref, cb_ref, st_ref,
                           dtb_ref, aloge_ref, dskip_ref, e3_ref, seg_ref,
                           nconv_ref, z_ref, ysk_ref, eacs_ref, xw_ref, dec_ref, b_ref, c_ref, raw_scr):
    t = h_ref.shape[0]
    n_seq = t // DEC_SEQ
    h = h_ref[...]
    z_ref[...] = _dot_wt(h, wz_ref[...])
    raw = jnp.concatenate([_dot_wt(h, wx0_ref[...]), _dot_wt(h, wx1_ref[...]), _dot_wt(h, wx2_ref[...])], axis=1)
    for c in range(raw_scr.shape[0]):
        lanes = slice(c * 128, (c + 1) * 128)
        raw_scr[c] = raw[:, lanes]
        for j in range(3):
            nconv_ref[j, :, lanes] = raw_scr[c, pl.ds(j + 1, n_seq, stride=DEC_SEQ), :]
    dt16 = _softplus(_dt_proj(h, wdt_ref) + dtb_ref[...])
    dt = _dot(jnp.concatenate(_split3(dt16), axis=1), e3_ref[...])
    old = [st_ref[j] for j in range(3)]
    p1 = _place_steps(t, [(0, old[2])])
    p2 = _place_steps(t, [(0, old[1]), (1, old[2])])
    p3 = _place_steps(t, [(0, old[0]), (1, old[1]), (2, old[2])])

    def step_of(width):
        return lax.broadcasted_iota(jnp.int32, (t, width), 0) % DEC_SEQ

    def back(a, k):
        return jnp.where(step_of(a.shape[1]) >= k, pltpu.roll(a, k, 0), 0.0)

    def ahead(a, k):
        return jnp.where(step_of(a.shape[1]) + k < DEC_SEQ, pltpu.roll(a, t - k, 0), 0.0)

    conv = (raw * cw_ref[3:4, :] + (back(raw, 1) + p1) * cw_ref[2:3, :]
            + (back(raw, 2) + p2) * cw_ref[1:2, :] + (back(raw, 3) + p3) * cw_ref[0:1, :]
            + cb_ref[...])
    xbc = _silu(conv)
    xs = xbc[:, 0:HALF]
    bm = xbc[:, HALF:HALF + 2 * SSD_STATE]
    cm = xbc[:, HALF + 2 * SSD_STATE:]
    b_ref[...] = bm
    c_ref[...] = cm
    da = dt * (-jnp.exp(aloge_ref[...]))
    acs = da + back(da, 1) + back(da, 2) + back(da, 3)
    suffix = ahead(da, 1) + ahead(da, 2) + ahead(da, 3)
    xdt = xs * dt
    y = _dot((cm * bm).astype(BF16), seg_ref[...]) * xdt
    for k in range(1, DEC_SEQ):
        cbk = _dot((cm * pltpu.roll(bm, k, 0)).astype(BF16), seg_ref[...])
        term = cbk * jnp.exp(acs - pltpu.roll(acs, k, 0)) * pltpu.roll(xdt, k, 0)
        y = y + jnp.where(step_of(HALF) >= k, term, 0.0)
    ysk_ref[...] = y + dskip_ref[...] * xs
    eacs_ref[...] = jnp.exp(acs)
    xw_ref[...] = jnp.exp(suffix) * xdt
    dec_ref[...] = jnp.exp(acs + suffix)


def _ssd_sample_pre(h, first_row, conv_state, w):
    t = conv_state.shape[1] * DEC_SEQ
    cd = SSD_CONV_DIM
    tile = min(SSD_PRE_ROWS, t)
    first = first_row // tile
    seqs = tile // DEC_SEQ
    state_spec = pl.BlockSpec((3, seqs, cd), lambda i: (0, i, 0))
    head = [w['conv_w'], w['conv_b']]
    tail = [w['dt_bias16'], w['a_log_e'], w['d_skip_e'], w['expand3'], w['seg_expand']]
    args = [h, w['w1'], w['w1'], w['w1'], w['w1'], w['w1']] + head + [conv_state] + tail
    wide = jax.ShapeDtypeStruct((t, HALF), F32)
    narrow = jax.ShapeDtypeStruct((t, 2 * SSD_STATE), F32)
    out_shape = [jax.ShapeDtypeStruct(conv_state.shape, F32), wide, wide, wide, wide, wide, narrow, narrow]
    return pl.pallas_call(
        _ssd_sample_pre_kernel,
        grid=(t // tile,),
        in_specs=[pl.BlockSpec((tile, D_MODEL), lambda i: (first + i, 0))] + _ssd_weight_specs()
                 + [_resident(c.shape) for c in head] + [state_spec] + [_resident(c.shape) for c in tail],
        out_specs=[state_spec] + [_rows(tile, HALF)] * 5 + [_rows(tile, 2 * SSD_STATE)] * 2,
        out_shape=out_shape,
        scratch_shapes=[pltpu.VMEM((cd // 128, tile, 128), F32)],
        compiler_params=_params(),
        name="ssd_sample_pre",
    )(*args)


SSD_S_BATCH = 8
SSD_PRE_ROWS = 256


def _ssd_sample_state_kernel(st_ref, c_ref, b_ref, xw_ref, dec_ref, eacs_ref, ysk_ref, z_ref, gn_ref,
                             yd_ref, nst_ref):
    row_n = lax.broadcasted_iota(jnp.int32, (8, SSD_STATE), 0)
    row_w = lax.broadcasted_iota(jnp.int32, (8, SSD_GW), 0)
    row_f = lax.broadcasted_iota(jnp.int32, (8, HALF), 0)
    ones_rows = jnp.where((row_n >= 4) & (row_n < 7), 1.0, 0.0).astype(BF16)
    hpg = SSD_HEADS // SSD_GROUPS

    def pair(p, carry):
        r0 = pl.multiple_of(p * 8, 8)
        rows = pl.ds(r0, 8)
        c8 = c_ref[rows, :].astype(BF16)
        b8 = b_ref[rows, :]
        xw8 = xw_ref[rows, :]
        dec8 = dec_ref[rows, :]
        yoff = []
        for sub in range(2):
            b = 2 * p + sub
            xw_own = xw8 if sub == 0 else pltpu.roll(xw8, 4, 0)
            b_own = b8 if sub == 0 else pltpu.roll(b8, 4, 0)
            hi, mid, lo = (term.astype(F32) for term in _split3(dec8[4 * sub:4 * sub + 1, :]))
            parts = []
            for g in range(SSD_GROUPS):
                lanes = slice(g * SSD_GW, (g + 1) * SSD_GW)
                heads = pl.ds(g * hpg, hpg)
                h0 = st_ref[b, heads].reshape(SSD_GW, SSD_STATE)
                parts.append(_dot_nt(c8[:, g * SSD_STATE:(g + 1) * SSD_STATE], h0.astype(BF16)))
                lhs = jnp.where(row_w < 4, xw_own[:, lanes],
                                jnp.where(row_w == 4, hi[:, lanes],
                                          jnp.where(row_w == 5, mid[:, lanes],
                                                    jnp.where(row_w == 6, lo[:, lanes], 0.0)))).astype(BF16)
                rhs_b = jnp.where(row_n < 4, b_own[:, g * SSD_STATE:(g + 1) * SSD_STATE], 0.0).astype(BF16)
                decay = _dot_tn(lhs, ones_rows)
                nst_ref[b, heads] = (h0 * decay + _dot_tn(lhs, rhs_b)).reshape(hpg, HEAD_DIM, SSD_STATE)
            yoff.append(jnp.concatenate(parts, axis=1))
        yoff8 = jnp.where(row_f < 4, yoff[0], yoff[1])
        y = ysk_ref[rows, :] + yoff8 * eacs_ref[rows, :]
        yd_ref[rows, :] = _group_norm_gate(y, z_ref[rows, :], gn_ref[...])
        return carry

    lax.fori_loop(0, SSD_S_BATCH // 2, pair, 0, unroll=True)


def _ssd_sample_state(state, cm, bm, xw, dec, eacs, ysk, z, gn):
    n_seq = state.shape[0]
    bb = SSD_S_BATCH
    r = bb * DEC_SEQ
    st_spec = pl.BlockSpec((bb, SSD_HEADS, HEAD_DIM, SSD_STATE), lambda i: (i, 0, 0, 0))
    return pl.pallas_call(
        _ssd_sample_state_kernel,
        grid=(n_seq // bb,),
        in_specs=[st_spec, _rows(r, 2 * SSD_STATE), _rows(r, 2 * SSD_STATE)] + [_rows(r, HALF)] * 5
                 + [_resident((1, HALF))],
        out_specs=[_rows(r, HALF), st_spec],
        out_shape=[jax.ShapeDtypeStruct((n_seq * DEC_SEQ, HALF), BF16), jax.ShapeDtypeStruct(state.shape, F32)],
        compiler_params=_params(),
        name="ssd_sample_state",
    )(state, cm, bm, xw, dec, eacs, ysk, z, gn)


def _prep_layer1(g_pre, w_in, ln_g, ln_b, w_s, b_s, conv_w, conv_b, dt_bias, a_log, d_skip, gate_norm_g,
                 w_out, g_post):
    cd = SSD_CONV_DIM
    gw = HALF // CMLP_GROUPS
    w1 = w_in.T

    def lanes16(v):
        return jnp.pad(v.astype(F32), (0, HEAD_LANES - SSD_HEADS)).reshape(1, HEAD_LANES)

    def per_channel(v):
        return jnp.repeat(v.astype(F32), HEAD_DIM).reshape(1, HALF)

    head_of = np.arange(HALF) // HEAD_DIM
    expand = jnp.asarray(np.arange(HEAD_LANES)[:, None] == head_of[None, :], BF16)
    tril = jnp.asarray(np.tril(np.ones((CHUNK, CHUNK))), BF16)
    grp_rows = np.arange(2 * SSD_STATE) // SSD_STATE
    seg_expand = jnp.asarray(grp_rows[:, None] == (head_of // (SSD_HEADS // SSD_GROUPS))[None, :], BF16)

    w4 = w_s[:, :DEC_SEQ, :DEC_SEQ].astype(F32)
    steps = np.arange(DEC_SEQ)
    back = jnp.asarray(steps[None, None, :] == steps[None, :, None] - steps[:, None, None])
    ck = jnp.sum(jnp.where(back[:, None], w4[None], 0.0), axis=-1)
    ck = jnp.repeat(ck.transpose(0, 2, 1), gw, axis=2)
    coef = jnp.concatenate([ck, ck], axis=1)
    bias4 = jnp.repeat(b_s[:, :DEC_SEQ].T, gw, axis=1)
    return dict(
        g_pre=g_pre.reshape(1, D_MODEL), w1=w1, ln_g=ln_g.reshape(1, HALF), ln_b=ln_b.reshape(1, HALF),
        w_s=w_s, b_s=b_s,
        coef=coef, bias4=jnp.concatenate([bias4, bias4], axis=0).astype(F32),
        conv_w=conv_w, conv_b=conv_b.reshape(1, cd), dt_bias16=lanes16(dt_bias), a_log16=lanes16(a_log),
        a_log_e=per_channel(a_log), d_skip_e=per_channel(d_skip), gate_norm_g=gate_norm_g.reshape(1, HALF),
        expand3=jnp.concatenate([expand] * 3, axis=0), tril3=jnp.concatenate([tril] * 3, axis=1),
        seg_expand=seg_expand, w_out=w_out, g_post=g_post.reshape(1, D_MODEL))


def _layer1(x_all, n_p, conv_state, ssm_state, w):
    n_seq = ssm_state.shape[0]
    tile = n_seq * DEC_SEQ
    yc_all, h_all, vn = _cmlp(x_all, n_p, tile, w)
    yd_p, tail, ssm = _ssd_prompt(h_all, n_p, tile, w)
    new_conv, z, ysk, eacs, xw, dec, bm, cm = _ssd_sample_pre(h_all, n_p * tile, conv_state.transpose(1, 0, 2), w)
    yd_s, new_state = _ssd_sample_state(ssm_state, cm, bm, xw, dec, eacs, ysk, z, w['gate_norm_g'])
    y_p, y_s = _out_proj(yc_all, (yd_p, yd_s), x_all, w['w_out'], w['g_post'], n_p, tile,
                         sample_out_shape=(n_seq, DEC_SEQ, D_MODEL))
    prompt_out = (y_p, tail[5:8], ssm.reshape(SSD_HEADS, HEAD_DIM, SSD_STATE))
    sample_out = (y_s, vn, new_conv.transpose(1, 0, 2), new_state)
    return prompt_out, sample_out


def kernel(x_prompt, x_sample, state_conv_a, cache_win_k, cache_win_v, state_conv_d, state_ssm, rel_bias,
           l0_g_pre, l0_w_in, l0_conv_w, l0_sinks, l0_w_out, l0_g_post,
           l1_g_pre, l1_w_in, l1_ln_g, l1_ln_b, l1_w_s, l1_b_s, l1_conv_w, l1_conv_b, l1_dt_bias, l1_a_log,
           l1_d_skip, l1_gate_norm_g, l1_w_out, l1_g_post):
    w0 = _prep_layer0(l0_g_pre, l0_w_in, l0_conv_w, rel_bias, l0_sinks, l0_w_out, l0_g_post)
    w1 = _prep_layer1(l1_g_pre, l1_w_in, l1_ln_g, l1_ln_b, l1_w_s, l1_b_s, l1_conv_w, l1_conv_b, l1_dt_bias,
                      l1_a_log, l1_d_skip, l1_gate_norm_g, l1_w_out, l1_g_post)
    x_p = x_prompt[0]
    n_p = x_p.shape[0] // (x_sample.shape[0] * DEC_SEQ)
    y_all, (p_conv_a, p_win_k, p_win_v), (s_conv_a, s_win_k, s_win_v) = _layer0(
        x_p, x_sample, state_conv_a, cache_win_k, cache_win_v, w0)
    (yp, p_conv_d, p_ssm), (ys, s_chunk_v, s_conv_d, s_ssm) = _layer1(y_all, n_p, state_conv_d, state_ssm, w1)
    return (yp[None], ys, p_conv_a[None], s_conv_a, p_win_k[None], p_win_v[None], s_win_k, s_win_v, s_chunk_v,
            p_conv_d[None], s_conv_d, p_ssm[None], s_ssm)
```

```python
import functools
import math

import jax
import jax.numpy as jnp
import numpy as np
from jax import lax
from jax.experimental import pallas as pl
from jax.experimental.pallas import tpu as pltpu

F32 = jnp.float32
BF16 = jnp.bfloat16

D_MODEL = 2048
HALF = 1024
HEAD_DIM = 64
N_HEADS = 16
N_KV = 4
GROUP = 4
WINDOW = 128
NUM_BUCKETS = 32
MAX_DISTANCE = 128
CMLP_GROUPS = 8
CHUNK = 128
SSD_HEADS = 16
SSD_STATE = 128
SSD_GROUPS = 2
SSD_CONV_DIM = HALF + 2 * SSD_GROUPS * SSD_STATE
DEC_SEQ = 4
NORM_EPS = 1e-6
MASK_VALUE = -1e30

V7X_VMEM_BYTES = 64 * 1024 * 1024
VMEM_LIMIT = V7X_VMEM_BYTES - 8 * 1024 * 1024
CONV_A_VMEM_LIMIT = V7X_VMEM_BYTES - 4 * 1024 * 1024


def _params(n_axes=1):
    return pltpu.CompilerParams(dimension_semantics=("arbitrary",) * n_axes,
                                vmem_limit_bytes=VMEM_LIMIT)


def _resident(shape):
    nd = len(shape)
    return pl.BlockSpec(shape, lambda *_: (0,) * nd, pipeline_mode=pl.Buffered(1))


def _rows(tile, width):
    return pl.BlockSpec((tile, width), lambda i: (i, 0))


def _cols(rows, width, block):
    return pl.BlockSpec((rows, width), lambda *_: (0, block), pipeline_mode=pl.Buffered(1))


def _rowwin(height, cols, block):
    return pl.BlockSpec((height, cols), lambda *_: (block, 0), pipeline_mode=pl.Buffered(1))


def _rms_bf16(x, g):
    ms = jnp.mean(x * x, axis=-1, keepdims=True)
    return (x * lax.rsqrt(ms + NORM_EPS) * g).astype(BF16)


def _silu(x):
    return x * jax.nn.sigmoid(x)


def _dot(a, b):
    return jnp.dot(a, b, preferred_element_type=F32)


def _dot_nt(a, b):
    return lax.dot_general(a, b, (((1,), (1,)), ((), ())), preferred_element_type=F32)


def _dot_tn(a, b):
    return lax.dot_general(a, b, (((0,), (0,)), ((), ())), preferred_element_type=F32)


def _dot_w(a, w):
    return _dot(a, w.astype(BF16))


def _dot_wt(a, wt):
    return _dot_nt(a, wt.astype(BF16))


def _split3(x):
    hi = x.astype(BF16)
    r1 = x - hi.astype(F32)
    mid = r1.astype(BF16)
    lo = (r1 - mid.astype(F32)).astype(BF16)
    return hi, mid, lo


def _place_steps(t, placements):
    n_seq = placements[0][1].shape[0]
    row = lax.broadcasted_iota(jnp.int32, (t, n_seq), 0)
    seq = lax.broadcasted_iota(jnp.int32, (t, n_seq), 1)
    lhs, rhs = [], []
    for step, state in placements:
        sel = jnp.where(row == DEC_SEQ * seq + step, 1.0, 0.0).astype(BF16)
        lhs += [sel] * 3
        rhs += list(_split3(state))
    return _dot(jnp.concatenate(lhs, axis=1), jnp.concatenate(rhs, axis=0))


def _prompt_rows(tile, width, n_p):
    return pl.BlockSpec((tile, width), lambda i: (jnp.minimum(i, n_p - 1), 0))


def _group_specs(arg, tile, width, n_p):
    if isinstance(arg, tuple):
        return [_prompt_rows(tile, width, n_p), _resident(arg[1].shape)]
    return [_rows(tile, width)]


def _out_proj_tile(ya_ref, yb_ref, x_ref, w_ref, g_ref, o_ref):
    y = _dot_w(ya_ref[...], w_ref[0:HALF, :]) + _dot_w(yb_ref[...], w_ref[HALF:2 * HALF, :])
    ms = jnp.mean(y * y, axis=-1, keepdims=True)
    out = x_ref[...].reshape(y.shape) + y * lax.rsqrt(ms + NORM_EPS) * g_ref[...]
    o_ref[...] = out.reshape(o_ref.shape)


def _out_proj_kernel(*refs, n_p, n_ya, n_yb, n_x):
    refs = list(refs)
    ya, yb, x = refs[:n_ya], refs[n_ya:n_ya + n_yb], refs[n_ya + n_yb:n_ya + n_yb + n_x]
    w_ref, g_ref = refs[n_ya + n_yb + n_x:n_ya + n_yb + n_x + 2]
    outs = refs[n_ya + n_yb + n_x + 2:]
    is_sample = pl.program_id(0) == n_p

    @pl.when(jnp.logical_not(is_sample))
    def _():
        _out_proj_tile(ya[0], yb[0], x[0], w_ref, g_ref, outs[0])

    @pl.when(is_sample)
    def _():
        _out_proj_tile(ya[-1], yb[-1], x[-1], w_ref, g_ref, outs[-1])


def _out_proj(ya, yb, x, w, g, n_p, tile, sample_out_shape=None):
    out_pair = sample_out_shape is not None
    groups = [(ya, HALF), (yb, HALF), (x, D_MODEL)]
    in_specs, args = [], []
    for arg, width in groups:
        in_specs += _group_specs(arg, tile, width, n_p)
        args += list(arg) if isinstance(arg, tuple) else [arg]
    if out_pair:
        out_specs = [_prompt_rows(tile, D_MODEL, n_p), _resident(sample_out_shape)]
        out_shape = [jax.ShapeDtypeStruct((n_p * tile, D_MODEL), F32), jax.ShapeDtypeStruct(sample_out_shape, F32)]
    else:
        out_specs = [_rows(tile, D_MODEL)]
        out_shape = [jax.ShapeDtypeStruct(((n_p + 1) * tile, D_MODEL), F32)]
    n_of = [2 if isinstance(arg, tuple) else 1 for arg, _ in groups]
    return pl.pallas_call(
        functools.partial(_out_proj_kernel, n_p=n_p, n_ya=n_of[0], n_yb=n_of[1], n_x=n_of[2]),
        grid=(n_p + 1,),
        in_specs=in_specs + [_resident((2 * HALF, D_MODEL)), _resident((1, D_MODEL))],
        out_specs=out_specs,
        out_shape=out_shape,
        compiler_params=_params(),
        name="out_proj",
    )(*args, w, g)


CONV_A_CHUNK = 256


def _conv_a_kernel(*refs, sample):
    if sample:
        x_ref, g_ref, w_ref, cw_ref, st_ref, ya_ref, s_ref, h_ref = refs
    else:
        x_ref, g_ref, w_ref, cw_ref, ya_ref, new_ref, h_ref, shift_scr, s_ref = refs
    tile = ya_ref.shape[0]
    cc = CONV_A_CHUNK
    h = _rms_bf16(x_ref[...].reshape(tile, D_MODEL), g_ref[...])
    h_ref[...] = h
    if not sample:
        @pl.when(pl.program_id(0) == 0)
        def _():
            s_ref[...] = jnp.zeros_like(s_ref)
    for c in range(HALF // cc):
        lanes = slice(c * cc, (c + 1) * cc)
        a_b, a_c, a_h, a_g = (_dot_w(h, w_ref[:, j * HALF + c * cc:j * HALF + (c + 1) * cc]) for j in range(4))
        s = a_c * a_h
        if sample:
            t_in = lax.broadcasted_iota(jnp.int32, s.shape, 0) % DEC_SEQ
            old0 = st_ref[:, c * cc:(c + 1) * cc]
            old1 = st_ref[:, HALF + c * cc:HALF + (c + 1) * cc]
            p1 = jnp.where(t_in >= 1, pltpu.roll(s, 1, 0), 0.0) + _place_steps(tile, [(0, old1)])
            p2 = jnp.where(t_in >= 2, pltpu.roll(s, 2, 0), 0.0) + _place_steps(tile, [(0, old0), (1, old1)])
            s_ref[:, :, lanes] = s.reshape(tile // DEC_SEQ, DEC_SEQ, cc)[:, DEC_SEQ - 2:, :]
        else:
            shift_scr[0:8, :] = s_ref[:, lanes]
            shift_scr[8:8 + tile, :] = s
            p1 = shift_scr[7:7 + tile, :]
            p2 = shift_scr[6:6 + tile, :]
            s_ref[:, lanes] = s[tile - 8:tile, :]
            new_ref[0, :, lanes] = s[tile - 2:tile, :]
        conv = p2 * cw_ref[0:1, lanes] + p1 * cw_ref[1:2, lanes] + s * cw_ref[2:3, lanes]
        ya_ref[:, lanes] = (a_b * conv * _silu(a_g)).astype(BF16)


def _conv_a_rowspace_kernel(xp_ref, xs_ref, g_ref, w_ref, cw_ref, st_ref, ya_ref, new_ref, ss_ref, h_ref, shift_scr,
                            tail_scr, *, n_p):
    is_sample = pl.program_id(0) == n_p

    @pl.when(jnp.logical_not(is_sample))
    def _():
        _conv_a_kernel(xp_ref, g_ref, w_ref, cw_ref, ya_ref, new_ref, h_ref, shift_scr, tail_scr, sample=False)

    @pl.when(is_sample)
    def _():
        _conv_a_kernel(xs_ref, g_ref, w_ref, cw_ref, st_ref, ya_ref, ss_ref, h_ref, sample=True)


def _conv_a(x_p, x_s, g_pre, w0, conv_w, state):
    tile = x_s.shape[0] * x_s.shape[1]
    n_p = x_p.shape[0] // tile
    t = (n_p + 1) * tile
    return pl.pallas_call(
        functools.partial(_conv_a_rowspace_kernel, n_p=n_p),
        grid=(n_p + 1,),
        in_specs=[_prompt_rows(tile, D_MODEL, n_p), _resident(x_s.shape), _resident((1, D_MODEL)),
                  _cols(D_MODEL, 4 * HALF, 0), _resident((3, HALF)), _resident(state.shape)],
        out_specs=[_rows(tile, HALF), _resident((1, 2, HALF)), _resident((x_s.shape[0], 2, HALF)),
                   _rows(tile, D_MODEL)],
        out_shape=[jax.ShapeDtypeStruct((t, HALF), BF16), jax.ShapeDtypeStruct((1, 2, HALF), F32),
                   jax.ShapeDtypeStruct((x_s.shape[0], 2, HALF), F32), jax.ShapeDtypeStruct((t, D_MODEL), BF16)],
        scratch_shapes=[pltpu.VMEM((8 + tile, CONV_A_CHUNK), F32), pltpu.VMEM((8, HALF), F32)],
        compiler_params=pltpu.CompilerParams(dimension_semantics=("arbitrary",), vmem_limit_bytes=CONV_A_VMEM_LIMIT),
        name="conv_a",
    )(x_p, x_s, g_pre, w0, conv_w, state)


def _rel_bucket(dist):
    max_exact = NUM_BUCKETS // 2
    d = jnp.maximum(dist, 0)
    ratio = jnp.maximum(d, max_exact).astype(F32) / max_exact
    large = max_exact + (jnp.log(ratio) / math.log(MAX_DISTANCE / max_exact)
                         * (NUM_BUCKETS - max_exact)).astype(jnp.int32)
    return jnp.where(d < max_exact, d, jnp.minimum(large, NUM_BUCKETS - 1))


def _attn_softmax_pv(s, sink, v_bf, v_transposed=False):
    m = jnp.maximum(jnp.max(s, axis=-1, keepdims=True), sink)
    p = jnp.exp(s - m)
    den = jnp.sum(p, axis=-1, keepdims=True) + jnp.exp(sink - m)
    pv = _dot_nt(p.astype(BF16), v_bf) if v_transposed else _dot(p.astype(BF16), v_bf)
    return pv / den


def _attn_prompt_kernel(h_ref, wq_ref, wkv_ref, wg0_ref, wg1_ref, tab_ref, sink_ref, yb_ref,
                        q_scr, gate_scr, k_scr, v_scr, bias_scr, kv_scr):
    tile = h_ref.shape[0]
    i = pl.program_id(0)
    kv_w = N_KV * HEAD_DIM
    h = h_ref[...]

    kw, vw = 2 * HEAD_DIM, 4 * HEAD_DIM

    @pl.when(i == 0)
    def _():
        k_scr[0:WINDOW, :] = jnp.zeros((WINDOW, N_KV * kw), BF16)
        v_scr[0:WINDOW, :] = jnp.zeros((WINDOW, N_KV * vw), BF16)
        for hk in range(N_KV):
            v_scr[:, hk * vw + kw:(hk + 1) * vw] = jnp.ones((tile + WINDOW, kw), BF16)
        in_own = lax.broadcasted_iota(jnp.int32, (WINDOW, 2 * WINDOW), 1) >= WINDOW
        for head in range(N_HEADS):
            row = jnp.broadcast_to(tab_ref[head:head + 1, :], (WINDOW, BIAS_SPAN))
            band = pltpu.roll(row, 0, 1, stride=1, stride_axis=0)[:, 0:2 * WINDOW]
            rows = slice((head % 2) * WINDOW, (head % 2 + 1) * WINDOW)
            bias_scr[1, head // 2, rows, :] = band
            bias_scr[0, head // 2, rows, :] = jnp.where(in_own, band, MASK_VALUE)

    q_scr[...] = (_dot_w(h, wq_ref[...]) * (HEAD_DIM ** -0.5)).astype(BF16)
    k = _dot_w(h, wkv_ref[:, 0:kv_w])
    v = _dot_w(h, wkv_ref[:, kv_w:2 * kv_w])
    gate_scr[:, 0:HALF // 2] = _silu(_dot_w(h, wg0_ref[...]))
    gate_scr[:, HALF // 2:HALF] = _silu(_dot_w(h, wg1_ref[...]))
    for hk in range(N_KV):
        k_h = k[:, hk * HEAD_DIM:(hk + 1) * HEAD_DIM].astype(BF16)
        v_h = v[:, hk * HEAD_DIM:(hk + 1) * HEAD_DIM].astype(BF16)
        k_scr[WINDOW:WINDOW + tile, hk * kw:(hk + 1) * kw] = jnp.concatenate([k_h, k_h], axis=1)
        v_scr[WINDOW:WINDOW + tile, hk * vw:hk * vw + kw] = jnp.concatenate([v_h, v_h], axis=1)
    kv_scr[0:WINDOW, 0:kv_w] = k[tile - WINDOW:tile, :]
    kv_scr[0:WINDOW, kv_w:2 * kv_w] = v[tile - WINDOW:tile, :]

    lane = lax.broadcasted_iota(jnp.int32, (WINDOW, kw), 1)
    lo = lane < HEAD_DIM
    keep_a = jnp.where(lo, 1.0, 0.0).astype(BF16)
    keep_b = jnp.where(lo, 0.0, 1.0).astype(BF16)
    is_a = lax.broadcasted_iota(jnp.int32, (2 * WINDOW, 1), 0) < WINDOW

    def block(n, carry):
        r0 = pl.multiple_of(n * WINDOW, WINDOW)
        rows = pl.ds(r0, WINDOW)
        keys = pl.ds(r0, 2 * WINDOW)
        first = jnp.where(jnp.logical_and(i == 0, n == 0), 0, 1)
        for hk in range(N_KV):
            for gp in range(GROUP // 2):
                a = hk * GROUP + 2 * gp
                slab = slice(a * HEAD_DIM, (a + 2) * HEAD_DIM)
                q2 = q_scr[rows, slab]
                lhs = jnp.concatenate([q2 * keep_a, q2 * keep_b], axis=0)
                s = _dot_nt(lhs, k_scr[keys, hk * kw:(hk + 1) * kw]) + bias_scr[first, a // 2]
                sink = jnp.where(is_a, sink_ref[a], sink_ref[a + 1])
                m = jnp.maximum(jnp.max(s, axis=-1, keepdims=True), sink)
                p = jnp.exp(s - m).astype(BF16)
                pv = _dot(p, v_scr[keys, hk * vw:(hk + 1) * vw])
                num = jnp.where(lo, pv[0:WINDOW, 0:kw], pv[WINDOW:2 * WINDOW, 0:kw])
                den = jnp.where(lo, pv[0:WINDOW, kw:2 * kw], pv[WINDOW:2 * WINDOW, kw:2 * kw])
                m_slab = jnp.where(lo, m[0:WINDOW], m[WINDOW:2 * WINDOW])
                den = den + jnp.exp(jnp.where(lo, sink_ref[a], sink_ref[a + 1]) - m_slab)
                yb_ref[rows, slab] = (num / den * gate_scr[rows, slab]).astype(BF16)
        return carry

    lax.fori_loop(0, tile // WINDOW, block, 0, unroll=True)
    k_scr[0:WINDOW, :] = k_scr[tile:tile + WINDOW, :]
    v_scr[0:WINDOW, :] = v_scr[tile:tile + WINDOW, :]


BIAS_SPAN = 3 * WINDOW


def _bias_by_distance(rel_bias):
    return rel_bias.astype(F32)[_rel_bucket(WINDOW - 1 - jnp.arange(WINDOW))]


def _prompt_bias_table(rel_bias):
    by_dist = _bias_by_distance(rel_bias)
    masked = jnp.full((BIAS_SPAN - WINDOW - 1, N_HEADS), MASK_VALUE, F32)
    return jnp.concatenate([masked[0:1], by_dist, masked], axis=0).T


def _attn_proj_kernel(h_ref, wq_ref, wkv_ref, wg0_ref, wg1_ref, kwin_ref, vwin_ref, qg_ref, kt_ref, vt_ref, kv_scr):
    kv_w = N_KV * HEAD_DIM
    kwin_ref[...] = kv_scr[0:WINDOW, 0:kv_w].T
    vwin_ref[...] = kv_scr[0:WINDOW, kv_w:2 * kv_w].T
    h = h_ref[...]
    q = _dot_w(h, wq_ref[...]) * (HEAD_DIM ** -0.5)
    for hk in range(N_KV):
        for g in range(GROUP):
            src = (hk * GROUP + g) * HEAD_DIM
            dst = (g * N_KV + hk) * HEAD_DIM
            qg_ref[:, dst:dst + HEAD_DIM] = q[:, src:src + HEAD_DIM]
    qg_ref[:, HALF:HALF + HALF // 2] = _dot_w(h, wg0_ref[...])
    qg_ref[:, HALF + HALF // 2:2 * HALF] = _dot_w(h, wg1_ref[...])
    kv_scr[...] = _dot_w(h, wkv_ref[...])
    for j in range(kt_ref.shape[0]):
        kt_ref[j] = kv_scr[j * WINDOW:(j + 1) * WINDOW, 0:kv_w].T
        vt_ref[j] = kv_scr[j * WINDOW:(j + 1) * WINDOW, kv_w:2 * kv_w].T


def _attn_rowspace_kernel(h_ref, wq_ref, wkv_ref, wg0_ref, wg1_ref, tab_ref, sink_ref,
                          yb_ref, kwin_ref, vwin_ref, qg_ref, kt_ref, vt_ref,
                          q_scr, gate_scr, k_scr, v_scr, bias_scr, kv_scr, *, n_p):
    is_sample = pl.program_id(0) == n_p

    @pl.when(jnp.logical_not(is_sample))
    def _():
        _attn_prompt_kernel(h_ref, wq_ref, wkv_ref, wg0_ref, wg1_ref, tab_ref, sink_ref, yb_ref,
                            q_scr, gate_scr, k_scr, v_scr, bias_scr, kv_scr)

    @pl.when(is_sample)
    def _():
        _attn_proj_kernel(h_ref, wq_ref, wkv_ref, wg0_ref, wg1_ref, kwin_ref, vwin_ref, qg_ref, kt_ref, vt_ref,
                          kv_scr)


def _attn(h_all, n_p, tile, w0, table, sinks):
    kv_w = N_KV * HEAD_DIM
    t = n_p * tile
    win_spec = _resident((kv_w, WINDOW))
    out_shape = [jax.ShapeDtypeStruct((t, HALF), BF16),
                 jax.ShapeDtypeStruct((kv_w, WINDOW), F32), jax.ShapeDtypeStruct((kv_w, WINDOW), F32),
                 jax.ShapeDtypeStruct((tile, 2 * HALF), F32), jax.ShapeDtypeStruct((tile // WINDOW, kv_w, WINDOW), F32),
                 jax.ShapeDtypeStruct((tile // WINDOW, kv_w, WINDOW), F32)]
    return pl.pallas_call(
        functools.partial(_attn_rowspace_kernel, n_p=n_p),
        grid=(n_p + 1,),
        in_specs=[_rows(tile, D_MODEL),
                  _cols(D_MODEL, HALF, 4), _cols(D_MODEL, 2 * kv_w, 10),
                  _cols(D_MODEL, HALF // 2, 11), _cols(D_MODEL, HALF // 2, 12),
                  _resident(table.shape), pl.BlockSpec(memory_space=pltpu.SMEM)],
        out_specs=[_prompt_rows(tile, HALF, n_p), win_spec, win_spec] + [_resident(s.shape) for s in out_shape[3:]],
        out_shape=out_shape,
        scratch_shapes=[pltpu.VMEM((tile, HALF), BF16), pltpu.VMEM((tile, HALF), F32),
                        pltpu.VMEM((tile + WINDOW, 2 * kv_w), BF16), pltpu.VMEM((tile + WINDOW, 4 * kv_w), BF16),
                        pltpu.VMEM((2, N_HEADS // 2, 2 * WINDOW, 2 * WINDOW), F32),
                        pltpu.VMEM((tile, 2 * kv_w), F32)],
        compiler_params=_params(),
        name="attn",
    )(h_all, w0, w0, w0, w0, table, sinks)


ATTN_S_BATCH = 16
KEYS_PAD = 2 * WINDOW


def _attn_sample_kernel(qg_ref, ktn_ref, vtn_ref, ck_ref, cv_ref, bias_ref, sink_ref, yb_ref, nk_ref, nv_ref):
    kv_w = N_KV * HEAD_DIM
    row8 = lax.broadcasted_iota(jnp.int32, (8, kv_w), 0)
    lane_head = lax.broadcasted_iota(jnp.int32, (8, kv_w), 1) // HEAD_DIM
    lower = row8 < DEC_SEQ
    pick = [jnp.where(lane_head == 2 * hp + jnp.where(lower, 0, 1), 1.0, 0.0).astype(F32) for hp in range(2)]
    lower_w = lax.broadcasted_iota(jnp.int32, (8, HALF), 0) < DEC_SEQ
    kept = lax.broadcasted_iota(jnp.int32, (kv_w, WINDOW), 1) < WINDOW - DEC_SEQ
    seq0 = pl.program_id(0) * ATTN_S_BATCH
    per_tile = WINDOW // DEC_SEQ

    def slide(old, new_tile, shift):
        return jnp.where(kept, pltpu.roll(old, WINDOW - DEC_SEQ, 1), pltpu.roll(new_tile, shift, 1))

    def pair(p, carry):
        r0 = pl.multiple_of(p * 8, 8)
        rows = qg_ref[pl.ds(r0, 8), :]
        q8 = rows[:, 0:HALF]
        gate8 = rows[:, HALF:2 * HALF]
        out8 = []
        for sub in range(2):
            b = 2 * p + sub
            q_swap = pltpu.roll(q8, 4, 0)
            q_dup = jnp.where(lower_w, q8, q_swap) if sub == 0 else jnp.where(lower_w, q_swap, q8)
            tile = (seq0 + b) // per_tile
            shift = (2 * WINDOW - DEC_SEQ - DEC_SEQ * ((seq0 + b) % per_tile)) % WINDOW
            k_old = ck_ref[b].reshape(kv_w, WINDOW)
            v_old = cv_ref[b].reshape(kv_w, WINDOW)
            k_win = slide(k_old, ktn_ref[tile], shift)
            v_win = slide(v_old, vtn_ref[tile], shift)
            nk_ref[b] = k_win.reshape(N_KV, HEAD_DIM, WINDOW)
            nv_ref[b] = v_win.reshape(N_KV, HEAD_DIM, WINDOW)
            k_all = jnp.concatenate([k_old, k_win], axis=1).astype(BF16)
            v_all = jnp.concatenate([v_old, v_win], axis=1).astype(BF16)
            q_bd = jnp.concatenate(
                [q_dup[:, g * kv_w:(g + 1) * kv_w] * pick[hp] for g in range(GROUP) for hp in range(2)], axis=0)
            s = _dot(q_bd.astype(BF16), k_all) + bias_ref[...]
            o = _attn_softmax_pv(s, sink_ref[:, 0:1], v_all, v_transposed=True)
            out_g = []
            for g in range(GROUP):
                acc = None
                for hp in range(2):
                    piece = o[(2 * g + hp) * 8:(2 * g + hp + 1) * 8, :] * pick[hp]
                    piece = piece + pltpu.roll(piece, 4, 0)
                    acc = piece if acc is None else acc + piece
                out_g.append(acc)
            out8.append(jnp.concatenate(
                [out_g[g][:, hk * HEAD_DIM:(hk + 1) * HEAD_DIM] for hk in range(N_KV) for g in range(GROUP)], axis=1))
        o8 = jnp.where(lower_w, out8[0], out8[1])
        yb_ref[pl.ds(r0, 8), :] = (o8 * _silu(gate8)).astype(BF16)
        return carry

    lax.fori_loop(0, ATTN_S_BATCH // 2, pair, 0, unroll=2)


def _sample_bias(rel_bias, sinks):
    by_dist = _bias_by_distance(rel_bias)

    def masked(n):
        return jnp.full((n, N_HEADS), MASK_VALUE, F32)

    rows = [jnp.concatenate([masked(t + 1), by_dist[0:WINDOW - 1 - t], masked(KEYS_PAD - DEC_SEQ - WINDOW),
                             by_dist[WINDOW - 1 - t:WINDOW], masked(DEC_SEQ - 1 - t)], axis=0)
            for t in range(DEC_SEQ)]
    bias = jnp.stack(rows)
    bias = bias.reshape(DEC_SEQ, KEYS_PAD, N_KV, GROUP).transpose(3, 2, 0, 1).reshape(N_HEADS * DEC_SEQ, KEYS_PAD)
    sink = jnp.broadcast_to(sinks.astype(F32).reshape(N_KV, GROUP).T[:, :, None], (GROUP, N_KV, DEC_SEQ))
    return bias, jnp.broadcast_to(sink.reshape(N_HEADS * DEC_SEQ, 1), (N_HEADS * DEC_SEQ, 128))


def _attn_sample(qg, kt_new, vt_new, cache_kt, cache_vt, bias, sink):
    n_seq = cache_kt.shape[0]
    bb = ATTN_S_BATCH
    cache_spec = pl.BlockSpec((bb, N_KV, HEAD_DIM, WINDOW), lambda i: (i, 0, 0, 0))
    return pl.pallas_call(
        _attn_sample_kernel,
        grid=(n_seq // bb,),
        in_specs=[_rows(bb * DEC_SEQ, 2 * HALF), _resident(kt_new.shape), _resident(vt_new.shape),
                  cache_spec, cache_spec, _resident(bias.shape), _resident(sink.shape)],
        out_specs=[_rows(bb * DEC_SEQ, HALF), cache_spec, cache_spec],
        out_shape=[jax.ShapeDtypeStruct((n_seq * DEC_SEQ, HALF), BF16),
                   jax.ShapeDtypeStruct(cache_kt.shape, F32), jax.ShapeDtypeStruct(cache_vt.shape, F32)],
        compiler_params=_params(),
        name="attn_sample",
    )(qg, kt_new, vt_new, cache_kt, cache_vt, bias, sink)


def _prep_layer0(g_pre, w_in, conv_w, rel_bias, sinks, w_out, g_post):
    return dict(
        g_pre=g_pre.reshape(1, D_MODEL), w0=w_in, conv_w=conv_w, rel_bias=rel_bias, sinks=sinks,
        w_out=w_out, g_post=g_post.reshape(1, D_MODEL))


def _layer0(x_p, x_s, conv_state, cache_k, cache_v, w):
    n_seq = x_s.shape[0]
    tile = n_seq * DEC_SEQ
    n_p = x_p.shape[0] // tile
    ya_all, conv_p, conv_s, h_all = _conv_a(x_p, x_s, w['g_pre'], w['w0'], w['conv_w'],
                                            conv_state.reshape(n_seq, 2 * HALF))
    yb_p, kwin, vwin, qg, kt_new, vt_new = _attn(h_all, n_p, tile, w['w0'], _prompt_bias_table(w['rel_bias']),
                                                 w['sinks'])
    bias, sink = _sample_bias(w['rel_bias'], w['sinks'])
    yb_s, new_kt, new_vt = _attn_sample(qg, kt_new, vt_new, cache_k.transpose(0, 2, 3, 1),
                                        cache_v.transpose(0, 2, 3, 1), bias, sink)
    (y_all,) = _out_proj(ya_all, (yb_p, yb_s), (x_p, x_s), w['w_out'], w['g_post'], n_p, tile)
    prompt_state = (conv_p[0], kwin.reshape(N_KV, HEAD_DIM, WINDOW).transpose(2, 0, 1),
                    vwin.reshape(N_KV, HEAD_DIM, WINDOW).transpose(2, 0, 1))
    sample_state = (conv_s, new_kt.transpose(0, 3, 1, 2), new_vt.transpose(0, 3, 1, 2))
    return y_all, prompt_state, sample_state


def _layer_norm(v, g, b):
    xc = v - jnp.mean(v, axis=-1, keepdims=True)
    return xc * lax.rsqrt(jnp.mean(xc * xc, axis=-1, keepdims=True) + NORM_EPS) * g + b


def _cmlp_prompt_kernel(x_ref, g_ref, w_ref, lng_ref, lnb_ref, ws_ref, bs_ref, yc_ref, h_ref, vn_scr, ws_scr, bs_scr):
    tile = x_ref.shape[0]

    @pl.when(pl.program_id(0) == 0)
    def _():
        causal = (lax.broadcasted_iota(jnp.int32, (CHUNK, CHUNK), 0)
                  >= lax.broadcasted_iota(jnp.int32, (CHUNK, CHUNK), 1))
        for grp in range(CMLP_GROUPS):
            ws_scr[grp] = jnp.where(causal, ws_ref[grp], 0.0).astype(BF16)
            bs_scr[grp] = jnp.broadcast_to(bs_ref[grp:grp + 1, :], (bs_scr.shape[2], CHUNK)).T

    h = _rms_bf16(x_ref[...], g_ref[...])
    h_ref[...] = h
    v = _dot_wt(h, w_ref[HALF:2 * HALF, :])
    vn_scr[...] = _layer_norm(v, lng_ref[...], lnb_ref[...]).astype(BF16)
    gw = HALF // CMLP_GROUPS
    cols = 2 * gw
    for cb in range(HALF // cols):
        u = _dot_wt(h, w_ref[cb * cols:(cb + 1) * cols, :])
        gate = _silu(_dot_wt(h, w_ref[2 * HALF + cb * cols:2 * HALF + (cb + 1) * cols, :]))
        for gi in range(2):
            grp = 2 * cb + gi
            lanes = slice(grp * gw, (grp + 1) * gw)
            for n in range(tile // CHUNK):
                rows = slice(n * CHUNK, (n + 1) * CHUNK)
                mixed = _dot(ws_scr[grp], vn_scr[rows, lanes]) + bs_scr[grp]
                yc_ref[rows, lanes] = (u[rows, gi * gw:(gi + 1) * gw] * mixed
                                       * gate[rows, gi * gw:(gi + 1) * gw]).astype(BF16)


def _cmlp_kernel(x_ref, g_ref, w_ref, lng_ref, lnb_ref, ws_ref, bs_ref, coef_ref, bias_ref, yc_ref, h_ref, vns_ref,
                 vn_scr, ws_scr, bs_scr, *, n_p):
    is_sample = pl.program_id(0) == n_p

    @pl.when(jnp.logical_not(is_sample))
    def _():
        _cmlp_prompt_kernel(x_ref, g_ref, w_ref, lng_ref, lnb_ref, ws_ref, bs_ref, yc_ref, h_ref, vn_scr, ws_scr,
                            bs_scr)

    @pl.when(is_sample)
    def _():
        _cmlp_sample_kernel(x_ref, g_ref, w_ref, lng_ref, lnb_ref, coef_ref, bias_ref, yc_ref, vns_ref, h_ref)


def _cmlp(x_all, n_p, tile, w):
    t = x_all.shape[0]
    consts = [w['ln_g'], w['ln_b'], w['w_s'], w['b_s'], w['coef'], w['bias4']]
    gw = HALF // CMLP_GROUPS
    return pl.pallas_call(
        functools.partial(_cmlp_kernel, n_p=n_p),
        grid=(n_p + 1,),
        in_specs=[_rows(tile, D_MODEL), _resident((1, D_MODEL)), _rowwin(3 * HALF, D_MODEL, 0)]
                 + [_resident(c.shape) for c in consts],
        out_specs=[_rows(tile, HALF), _rows(tile, D_MODEL), _resident((tile // DEC_SEQ, DEC_SEQ, HALF))],
        out_shape=[jax.ShapeDtypeStruct((t, HALF), BF16), jax.ShapeDtypeStruct((t, D_MODEL), BF16),
                   jax.ShapeDtypeStruct((tile // DEC_SEQ, DEC_SEQ, HALF), F32)],
        scratch_shapes=[pltpu.VMEM((tile, HALF), BF16), pltpu.VMEM((CMLP_GROUPS, CHUNK, CHUNK), BF16),
                        pltpu.VMEM((CMLP_GROUPS, CHUNK, gw), F32)],
        compiler_params=_params(),
        name="cmlp",
    )(x_all, w['g_pre'], w['w1'], *consts)


def _cmlp_sample_kernel(x_ref, g_ref, w_ref, lng_ref, lnb_ref, coef_ref, bias_ref, yc_ref, vn_ref, h_ref):
    t = x_ref.shape[0]
    h = _rms_bf16(x_ref[...], g_ref[...])
    h_ref[...] = h
    u = _dot_wt(h, w_ref[0:HALF, :])
    vn = _layer_norm(_dot_wt(h, w_ref[HALF:2 * HALF, :]), lng_ref[...], lnb_ref[...])
    gate = _silu(_dot_wt(h, w_ref[2 * HALF:3 * HALF, :]))
    vn_ref[...] = vn.reshape(vn_ref.shape)

    def tiled(a):
        return a.reshape(t // 8, 8, HALF)

    mixed = tiled(vn) * coef_ref[0][None] + bias_ref[...][None]
    for k in range(1, DEC_SEQ):
        mixed = mixed + tiled(pltpu.roll(vn, k, 0)) * coef_ref[k][None]
    yc_ref[...] = (u * mixed.reshape(t, HALF) * gate).astype(BF16)


HEAD_LANES = 128
SSD_GW = HALF // SSD_GROUPS


def _softplus(x):
    return jnp.maximum(x, 0.0) + jnp.log1p(jnp.exp(-jnp.abs(x)))


def _dt_proj(h, wdt_ref):
    pad = jnp.zeros((HEAD_LANES - SSD_HEADS, D_MODEL), F32)
    return _dot_wt(h, jnp.concatenate([wdt_ref[...], pad], axis=0))


def _group_norm_gate(y, z, gn):
    gated = y * _silu(z)
    parts = []
    for g in range(SSD_GROUPS):
        part = gated[:, g * SSD_GW:(g + 1) * SSD_GW]
        parts.append(part * lax.rsqrt(jnp.mean(part * part, axis=-1, keepdims=True) + NORM_EPS))
    return (jnp.concatenate(parts, axis=1) * gn).astype(BF16)


def _ssd_prompt_kernel(h_ref, wz_ref, wx0_ref, wx1_ref, wx2_ref, wdt_ref, cw_ref, cb_ref, dtb_ref, alog_ref,
                       dskip_ref, gn_ref, e3_ref, tril3_ref, yd_ref, tail_ref, ssm_ref,
                       xbc_scr, z_scr, dt_scr, ht_scr, shift_scr):
    tile = h_ref.shape[0]
    i = pl.program_id(0)
    cd = SSD_CONV_DIM
    h = h_ref[...]

    @pl.when(i == 0)
    def _():
        tail_ref[...] = jnp.zeros_like(tail_ref)
        ht_scr[...] = jnp.zeros_like(ht_scr)

    z_scr[...] = _dot_wt(h, wz_ref[...])
    dt_scr[...] = _softplus(_dt_proj(h, wdt_ref) + dtb_ref[...])
    third = cd // 3
    for j, wx_ref in enumerate((wx0_ref, wx1_ref, wx2_ref)):
        cols = slice(j * third, (j + 1) * third)
        raw = _dot_wt(h, wx_ref[...])
        shift_scr[0:8, :] = tail_ref[:, cols]
        shift_scr[8:8 + tile, :] = raw
        conv = raw * cw_ref[3:4, cols] + cb_ref[:, cols]
        for k in range(1, 4):
            conv = conv + shift_scr[8 - k:8 - k + tile, :] * cw_ref[3 - k:4 - k, cols]
        xbc_scr[:, cols] = _silu(conv)
        tail_ref[:, cols] = raw[tile - 8:tile, :]

    a16 = -jnp.exp(alog_ref[...])
    causal =(lax.broadcasted_iota(jnp.int32, (CHUNK, CHUNK), 0)
              >= lax.broadcasted_iota(jnp.int32, (CHUNK, CHUNK), 1))
    first_half = lax.broadcasted_iota(jnp.int32, (CHUNK, 2 * HEAD_DIM), 1) < HEAD_DIM
    keep_a = jnp.where(first_half, 1.0, 0.0).astype(BF16)
    keep_b = jnp.where(first_half, 0.0, 1.0).astype(BF16)

    def chunk(n, carry):
        r0 = pl.multiple_of(n * CHUNK, CHUNK)
        rows = pl.ds(r0, CHUNK)
        xs = xbc_scr[rows, 0:HALF]
        dt16 = dt_scr[rows, :]
        dt_e = _dot(jnp.concatenate(_split3(dt16), axis=1), e3_ref[...])
        acs16 = _dot(tril3_ref[...], jnp.concatenate(_split3(dt16 * a16), axis=0))
        acs_e = _dot(jnp.concatenate(_split3(acs16), axis=1), e3_ref[...])
        acs_t = acs16.T
        last_e = acs_e[CHUNK - 1:CHUNK, :]
        xdt = xs * dt_e
        xdt_bf = xdt.astype(BF16)
        xw = (jnp.exp(last_e - acs_e) * xdt).astype(BF16)
        dec_e = jnp.exp(last_e)
        y_parts = []
        yoff_parts = []
        for g in range(SSD_GROUPS):
            c_g = xbc_scr[rows, HALF + 2 * SSD_STATE + g * SSD_STATE:HALF + 2 * SSD_STATE + (g + 1) * SSD_STATE].astype(BF16)
            b_g = xbc_scr[rows, HALF + g * SSD_STATE:HALF + (g + 1) * SSD_STATE].astype(BF16)
            cb = _dot_nt(c_g, b_g)
            h_prev = ht_scr[g]
            yoff_parts.append(_dot(c_g, h_prev.astype(BF16)))
            for r in range(0, SSD_HEADS // SSD_GROUPS, 2):
                wgt = []
                for hd in (g * (SSD_HEADS // SSD_GROUPS) + r, g * (SSD_HEADS // SSD_GROUPS) + r + 1):
                    seg = acs16[:, hd:hd + 1] - acs_t[hd:hd + 1, :]
                    wgt.append(cb * jnp.exp(jnp.where(causal, seg, -jnp.inf)))
                a = g * (SSD_HEADS // SSD_GROUPS) + r
                slab = xdt_bf[:, a * HEAD_DIM:(a + 2) * HEAD_DIM]
                rhs = jnp.concatenate([slab * keep_a, slab * keep_b], axis=0)
                y_parts.append(_dot(jnp.concatenate(wgt, axis=1).astype(BF16), rhs))
            lanes = slice(g * SSD_GW, (g + 1) * SSD_GW)
            ht_scr[g] = h_prev * dec_e[:, lanes] + _dot_tn(b_g, xw[:, lanes])
        y = (jnp.concatenate(y_parts, axis=1) + jnp.concatenate(yoff_parts, axis=1) * jnp.exp(acs_e)
             + dskip_ref[...] * xs)
        yd_ref[rows, :] = _group_norm_gate(y, z_scr[rows, :], gn_ref[...])
        return carry

    lax.fori_loop(0, tile // CHUNK, chunk, 0, unroll=True)

    @pl.when(i == pl.num_programs(0) - 1)
    def _():
        for g in range(SSD_GROUPS):
            ssm_ref[g * SSD_GW:(g + 1) * SSD_GW, :] = ht_scr[g].T


def _ssd_weight_specs():
    third = SSD_CONV_DIM // 3
    first = 4 * HALF // third
    return ([_rowwin(HALF, D_MODEL, 3)] + [_rowwin(third, D_MODEL, first + j) for j in range(3)]
            + [_rowwin(SSD_HEADS, D_MODEL, (4 * HALF + SSD_CONV_DIM) // SSD_HEADS)])


def _ssd_prompt(h, n_p, tile, w):
    t = n_p * tile
    cd = SSD_CONV_DIM
    consts = [w['conv_w'], w['conv_b'], w['dt_bias16'], w['a_log16'],
              w['d_skip_e'], w['gate_norm_g'], w['expand3'], w['tril3']]
    return pl.pallas_call(
        _ssd_prompt_kernel,
        grid=(t // tile,),
        in_specs=[_rows(tile, D_MODEL)] + _ssd_weight_specs() + [_resident(c.shape) for c in consts],
        out_specs=[_rows(tile, HALF), pl.BlockSpec((8, cd), lambda i: (0, 0)),
                   pl.BlockSpec((HALF, SSD_STATE), lambda i: (0, 0))],
        out_shape=[jax.ShapeDtypeStruct((t, HALF), BF16), jax.ShapeDtypeStruct((8, cd), F32),
                   jax.ShapeDtypeStruct((HALF, SSD_STATE), F32)],
        scratch_shapes=[pltpu.VMEM((tile, cd), F32), pltpu.VMEM((tile, HALF), F32),
                        pltpu.VMEM((tile, HEAD_LANES), F32), pltpu.VMEM((SSD_GROUPS, SSD_STATE, SSD_GW), F32),
                        pltpu.VMEM((8 + tile, cd // 3), F32)],
        compiler_params=_params(),
        name="ssd_prompt",
    )(h, w['w1'], w['w1'], w['w1'], w['w1'], w['w1'], *consts)


def _ssd_sample_pre_kernel(h_ref, wz_ref, wx0_ref, wx1_ref, wx2_ref, wdt_ref, cw_ref, cb_ref, st_ref,
                           dtb_ref, aloge_ref, dskip_ref, e3_ref, seg_ref,
                           nconv_ref, z_ref, ysk_ref, eacs_ref, xw_ref, dec_ref, b_ref, c_ref, raw_scr):
    t = h_ref.shape[0]
    n_seq = t // DEC_SEQ
    h = h_ref[...]
    z_ref[...] = _dot_wt(h, wz_ref[...])
    raw = jnp.concatenate([_dot_wt(h, wx0_ref[...]), _dot_wt(h, wx1_ref[...]), _dot_wt(h, wx2_ref[...])], axis=1)
    for c in range(raw_scr.shape[0]):
        lanes = slice(c * 128, (c + 1) * 128)
        raw_scr[c] = raw[:, lanes]
        for j in range(3):
            nconv_ref[j, :, lanes] = raw_scr[c, pl.ds(j + 1, n_seq, stride=DEC_SEQ), :]
    dt16 = _softplus(_dt_proj(h, wdt_ref) + dtb_ref[...])
    dt = _dot(jnp.concatenate(_split3(dt16), axis=1), e3_ref[...])
    old = [st_ref[j] for j in range(3)]
    p1 = _place_steps(t, [(0, old[2])])
    p2 = _place_steps(t, [(0, old[1]), (1, old[2])])
    p3 = _place_steps(t, [(0, old[0]), (1, old[1]), (2, old[2])])

    def step_of(width):
        return lax.broadcasted_iota(jnp.int32, (t, width), 0) % DEC_SEQ

    def back(a, k):
        return jnp.where(step_of(a.shape[1]) >= k, pltpu.roll(a, k, 0), 0.0)

    def ahead(a, k):
        return jnp.where(step_of(a.shape[1]) + k < DEC_SEQ, pltpu.roll(a, t - k, 0), 0.0)

    conv = (raw * cw_ref[3:4, :] + (back(raw, 1) + p1) * cw_ref[2:3, :]
            + (back(raw, 2) + p2) * cw_ref[1:2, :] + (back(raw, 3) + p3) * cw_ref[0:1, :]
            + cb_ref[...])
    xbc = _silu(conv)
    xs = xbc[:, 0:HALF]
    bm = xbc[:, HALF:HALF + 2 * SSD_STATE]
    cm = xbc[:, HALF + 2 * SSD_STATE:]
    b_ref[...] = bm
    c_ref[...] = cm
    da = dt * (-jnp.exp(aloge_ref[...]))
    acs = da + back(da, 1) + back(da, 2) + back(da, 3)
    suffix = ahead(da, 1) + ahead(da, 2) + ahead(da, 3)
    xdt = xs * dt
    y = _dot((cm * bm).astype(BF16), seg_ref[...]) * xdt
    for k in range(1, DEC_SEQ):
        cbk = _dot((cm * pltpu.roll(bm, k, 0)).astype(BF16), seg_ref[...])
        term = cbk * jnp.exp(acs - pltpu.roll(acs, k, 0)) * pltpu.roll(xdt, k, 0)
        y = y + jnp.where(step_of(HALF) >= k, term, 0.0)
    ysk_ref[...] = y + dskip_ref[...] * xs
    eacs_ref[...] = jnp.exp(acs)
    xw_ref[...] = jnp.exp(suffix) * xdt
    dec_ref[...] = jnp.exp(acs + suffix)


def _ssd_sample_pre(h, first_row, conv_state, w):
    t = conv_state.shape[1] * DEC_SEQ
    cd = SSD_CONV_DIM
    tile = min(SSD_PRE_ROWS, t)
    first = first_row // tile
    seqs = tile // DEC_SEQ
    state_spec = pl.BlockSpec((3, seqs, cd), lambda i: (0, i, 0))
    head = [w['conv_w'], w['conv_b']]
    tail = [w['dt_bias16'], w['a_log_e'], w['d_skip_e'], w['expand3'], w['seg_expand']]
    args = [h, w['w1'], w['w1'], w['w1'], w['w1'], w['w1']] + head + [conv_state] + tail
    wide = jax.ShapeDtypeStruct((t, HALF), F32)
    narrow = jax.ShapeDtypeStruct((t, 2 * SSD_STATE), F32)
    out_shape = [jax.ShapeDtypeStruct(conv_state.shape, F32), wide, wide, wide, wide, wide, narrow, narrow]
    return pl.pallas_call(
        _ssd_sample_pre_kernel,
        grid=(t // tile,),
        in_specs=[pl.BlockSpec((tile, D_MODEL), lambda i: (first + i, 0))] + _ssd_weight_specs()
                 + [_resident(c.shape) for c in head] + [state_spec] + [_resident(c.shape) for c in tail],
        out_specs=[state_spec] + [_rows(tile, HALF)] * 5 + [_rows(tile, 2 * SSD_STATE)] * 2,
        out_shape=out_shape,
        scratch_shapes=[pltpu.VMEM((cd // 128, tile, 128), F32)],
        compiler_params=_params(),
        name="ssd_sample_pre",
    )(*args)


SSD_S_BATCH = 8
SSD_PRE_ROWS = 256


def _ssd_sample_state_kernel(st_ref, c_ref, b_ref, xw_ref, dec_ref, eacs_ref, ysk_ref, z_ref, gn_ref,
                             yd_ref, nst_ref):
    row_n = lax.broadcasted_iota(jnp.int32, (8, SSD_STATE), 0)
    row_w = lax.broadcasted_iota(jnp.int32, (8, SSD_GW), 0)
    row_f = lax.broadcasted_iota(jnp.int32, (8, HALF), 0)
    ones_rows = jnp.where((row_n >= 4) & (row_n < 7), 1.0, 0.0).astype(BF16)
    hpg = SSD_HEADS // SSD_GROUPS

    def pair(p, carry):
        r0 = pl.multiple_of(p * 8, 8)
        rows = pl.ds(r0, 8)
        c8 = c_ref[rows, :].astype(BF16)
        b8 = b_ref[rows, :]
        xw8 = xw_ref[rows, :]
        dec8 = dec_ref[rows, :]
        yoff = []
        for sub in range(2):
            b = 2 * p + sub
            xw_own = xw8 if sub == 0 else pltpu.roll(xw8, 4, 0)
            b_own = b8 if sub == 0 else pltpu.roll(b8, 4, 0)
            hi, mid, lo = (term.astype(F32) for term in _split3(dec8[4 * sub:4 * sub + 1, :]))
            parts = []
            for g in range(SSD_GROUPS):
                lanes = slice(g * SSD_GW, (g + 1) * SSD_GW)
                heads = pl.ds(g * hpg, hpg)
                h0 = st_ref[b, heads].reshape(SSD_GW, SSD_STATE)
                parts.append(_dot_nt(c8[:, g * SSD_STATE:(g + 1) * SSD_STATE], h0.astype(BF16)))
                lhs = jnp.where(row_w < 4, xw_own[:, lanes],
                                jnp.where(row_w == 4, hi[:, lanes],
                                          jnp.where(row_w == 5, mid[:, lanes],
                                                    jnp.where(row_w == 6, lo[:, lanes], 0.0)))).astype(BF16)
                rhs_b = jnp.where(row_n < 4, b_own[:, g * SSD_STATE:(g + 1) * SSD_STATE], 0.0).astype(BF16)
                decay = _dot_tn(lhs, ones_rows)
                nst_ref[b, heads] = (h0 * decay + _dot_tn(lhs, rhs_b)).reshape(hpg, HEAD_DIM, SSD_STATE)
            yoff.append(jnp.concatenate(parts, axis=1))
        yoff8 = jnp.where(row_f < 4, yoff[0], yoff[1])
        y = ysk_ref[rows, :] + yoff8 * eacs_ref[rows, :]
        yd_ref[rows, :] = _group_norm_gate(y, z_ref[rows, :], gn_ref[...])
        return carry

    lax.fori_loop(0, SSD_S_BATCH // 2, pair, 0, unroll=True)


def _ssd_sample_state(state, cm, bm, xw, dec, eacs, ysk, z, gn):
    n_seq = state.shape[0]
    bb = SSD_S_BATCH
    r = bb * DEC_SEQ
    st_spec = pl.BlockSpec((bb, SSD_HEADS, HEAD_DIM, SSD_STATE), lambda i: (i, 0, 0, 0))
    return pl.pallas_call(
        _ssd_sample_state_kernel,
        grid=(n_seq // bb,),
        in_specs=[st_spec, _rows(r, 2 * SSD_STATE), _rows(r, 2 * SSD_STATE)] + [_rows(r, HALF)] * 5
                 + [_resident((1, HALF))],
        out_specs=[_rows(r, HALF), st_spec],
        out_shape=[jax.ShapeDtypeStruct((n_seq * DEC_SEQ, HALF), BF16), jax.ShapeDtypeStruct(state.shape, F32)],
        compiler_params=_params(),
        name="ssd_sample_state",
    )(state, cm, bm, xw, dec, eacs, ysk, z, gn)


def _prep_layer1(g_pre, w_in, ln_g, ln_b, w_s, b_s, conv_w, conv_b, dt_bias, a_log, d_skip, gate_norm_g,
                 w_out, g_post):
    cd = SSD_CONV_DIM
    gw = HALF // CMLP_GROUPS
    w1 = w_in.T

    def lanes16(v):
        return jnp.pad(v.astype(F32), (0, HEAD_LANES - SSD_HEADS)).reshape(1, HEAD_LANES)

    def per_channel(v):
        return jnp.repeat(v.astype(F32), HEAD_DIM).reshape(1, HALF)

    head_of = np.arange(HALF) // HEAD_DIM
    expand = jnp.asarray(np.arange(HEAD_LANES)[:, None] == head_of[None, :], BF16)
    tril = jnp.asarray(np.tril(np.ones((CHUNK, CHUNK))), BF16)
    grp_rows = np.arange(2 * SSD_STATE) // SSD_STATE
    seg_expand = jnp.asarray(grp_rows[:, None] == (head_of // (SSD_HEADS // SSD_GROUPS))[None, :], BF16)

    w4 = w_s[:, :DEC_SEQ, :DEC_SEQ].astype(F32)
    steps = np.arange(DEC_SEQ)
    back = jnp.asarray(steps[None, None, :] == steps[None, :, None] - steps[:, None, None])
    ck = jnp.sum(jnp.where(back[:, None], w4[None], 0.0), axis=-1)
    ck = jnp.repeat(ck.transpose(0, 2, 1), gw, axis=2)
    coef = jnp.concatenate([ck, ck], axis=1)
    bias4 = jnp.repeat(b_s[:, :DEC_SEQ].T, gw, axis=1)
    return dict(
        g_pre=g_pre.reshape(1, D_MODEL), w1=w1, ln_g=ln_g.reshape(1, HALF), ln_b=ln_b.reshape(1, HALF),
        w_s=w_s, b_s=b_s,
        coef=coef, bias4=jnp.concatenate([bias4, bias4], axis=0).astype(F32),
        conv_w=conv_w, conv_b=conv_b.reshape(1, cd), dt_bias16=lanes16(dt_bias), a_log16=lanes16(a_log),
        a_log_e=per_channel(a_log), d_skip_e=per_channel(d_skip), gate_norm_g=gate_norm_g.reshape(1, HALF),
        expand3=jnp.concatenate([expand] * 3, axis=0), tril3=jnp.concatenate([tril] * 3, axis=1),
        seg_expand=seg_expand, w_out=w_out, g_post=g_post.reshape(1, D_MODEL))


def _layer1(x_all, n_p, conv_state, ssm_state, w):
    n_seq = ssm_state.shape[0]
    tile = n_seq * DEC_SEQ
    yc_all, h_all, vn = _cmlp(x_all, n_p, tile, w)
    yd_p, tail, ssm = _ssd_prompt(h_all, n_p, tile, w)
    new_conv, z, ysk, eacs, xw, dec, bm, cm = _ssd_sample_pre(h_all, n_p * tile, conv_state.transpose(1, 0, 2), w)
    yd_s, new_state = _ssd_sample_state(ssm_state, cm, bm, xw, dec, eacs, ysk, z, w['gate_norm_g'])
    y_p, y_s = _out_proj(yc_all, (yd_p, yd_s), x_all, w['w_out'], w['g_post'], n_p, tile,
                         sample_out_shape=(n_seq, DEC_SEQ, D_MODEL))
    prompt_out = (y_p, tail[5:8], ssm.reshape(SSD_HEADS, HEAD_DIM, SSD_STATE))
    sample_out = (y_s, vn, new_conv.transpose(1, 0, 2), new_state)
    return prompt_out, sample_out


def kernel(x_prompt, x_sample, state_conv_a, cache_win_k, cache_win_v, state_conv_d, state_ssm, rel_bias,
           l0_g_pre, l0_w_in, l0_conv_w, l0_sinks, l0_w_out, l0_g_post,
           l1_g_pre, l1_w_in, l1_ln_g, l1_ln_b, l1_w_s, l1_b_s, l1_conv_w, l1_conv_b, l1_dt_bias, l1_a_log,
           l1_d_skip, l1_gate_norm_g, l1_w_out, l1_g_post):
    w0 = _prep_layer0(l0_g_pre, l0_w_in, l0_conv_w, rel_bias, l0_sinks, l0_w_out, l0_g_post)
    w1 = _prep_layer1(l1_g_pre, l1_w_in, l1_ln_g, l1_ln_b, l1_w_s, l1_b_s, l1_conv_w, l1_conv_b, l1_dt_bias,
                      l1_a_log, l1_d_skip, l1_gate_norm_g, l1_w_out, l1_g_post)
    x_p = x_prompt[0]
    n_p = x_p.shape[0] // (x_sample.shape[0] * DEC_SEQ)
    y_all, (p_conv_a, p_win_k, p_win_v), (s_conv_a, s_win_k, s_win_v) = _layer0(
        x_p, x_sample, state_conv_a, cache_win_k, cache_win_v, w0)
    (yp, p_conv_d, p_ssm), (ys, s_chunk_v, s_conv_d, s_ssm) = _layer1(y_all, n_p, state_conv_d, state_ssm, w1)
    return (yp[None], ys, p_conv_a[None], s_conv_a, p_win_k[None], p_win_v[None], s_win_k, s_win_v, s_chunk_v,
            p_conv_d[None], s_conv_d, p_ssm[None], s_ssm)
```
